```python
import jax, jax.numpy as jnp
from jax import lax
import numpy as np

D_MODEL = 1024
BATCH = 8
SEQ = 8192
DEPTH = 1

CTX_LEN = 256
GRID_W = 64
D_CONV = 1024
CONV_W = 3
N_Q_HEADS = 8
N_KV_HEADS = 2
HEAD_DIM = 128
GROUP = N_Q_HEADS // N_KV_HEADS
ROPE_THETA = 10000.0
D_FF = 2816
Q_BLOCK = 128
EPS = 1e-6
N_MOD = 9

Q_W = N_Q_HEADS * HEAD_DIM
KV_W = N_KV_HEADS * HEAD_DIM
PIECE_WIDTHS = (D_CONV, D_CONV, D_CONV, Q_W, KV_W, KV_W, 2 * D_MODEL)
D_IN = int(sum(PIECE_WIDTHS))
SPLITS = tuple(int(s) for s in np.cumsum(PIECE_WIDTHS)[:-1])

kernel_name = "hybrid_shortconv_gqa_macaron_dit_layer"


def rms_norm(x, g):
    xf = x.astype(jnp.float32)
    y = xf * lax.rsqrt(jnp.mean(xf * xf, axis=-1, keepdims=True) + EPS)
    return (y * g.astype(jnp.float32)).astype(x.dtype)


def modulate(h, shift, scale):
    return h * (1.0 + scale) + shift


def swiglu(h, w_in, w_out):
    a, b = jnp.split(h @ w_in, 2, axis=-1)
    return (jax.nn.silu(a) * b) @ w_out


def axial_rope_tables(n_tokens):
    rows = n_tokens // GRID_W
    row = jnp.repeat(jnp.arange(rows), GRID_W).astype(jnp.float32)
    col = jnp.tile(jnp.arange(GRID_W), rows).astype(jnp.float32)
    n_freq = HEAD_DIM // 4
    inv = ROPE_THETA ** (-jnp.arange(n_freq, dtype=jnp.float32) / n_freq)
    ang = jnp.stack([row[:, None] * inv, col[:, None] * inv], axis=1)
    return jnp.cos(ang), jnp.sin(ang)


def apply_axial_rope(x, cos, sin):
    xs = x.reshape(x.shape[:3] + (2, 2, HEAD_DIM // 4))
    x1, x2 = xs[..., 0, :], xs[..., 1, :]
    c = cos[None, :, None].astype(x.dtype)
    s = sin[None, :, None].astype(x.dtype)
    out = jnp.stack([x1 * c - x2 * s, x2 * c + x1 * s], axis=-2)
    return out.reshape(x.shape)


def short_conv(u, w):
    L = u.shape[1]
    pad = CONV_W // 2
    up = jnp.pad(u, ((0, 0), (pad, pad), (0, 0)))
    y = up[:, 0:L] * w[0]
    for j in range(1, CONV_W):
        y = y + up[:, j:j + L] * w[j]
    return y


def conv_branch(bg, cg, vc, conv_w):
    return bg * short_conv(cg * vc, conv_w)


def gqa_attend(q, k, v):
    s = jnp.einsum('btkgd,blkd->bkgtl', q, k, preferred_element_type=jnp.float32) * (HEAD_DIM ** -0.5)
    p = jax.nn.softmax(s, axis=-1).astype(v.dtype)
    return jnp.einsum('bkgtl,blkd->btkgd', p, v)


def latent_attention(q_lat, k_all, v_all):
    B, S = q_lat.shape[:2]
    nblk = S // Q_BLOCK
    qb = q_lat.reshape(B, nblk, Q_BLOCK, N_KV_HEADS, GROUP, HEAD_DIM).swapaxes(0, 1)
    ob = lax.map(lambda qi: gqa_attend(qi, k_all, v_all), qb)
    return ob.swapaxes(0, 1).reshape(B, S, Q_W)


def split_heads(q, k, v, q_g, k_g):
    B, L = q.shape[:2]
    q = rms_norm(q.reshape(B, L, N_Q_HEADS, HEAD_DIM), q_g)
    k = rms_norm(k.reshape(B, L, N_KV_HEADS, HEAD_DIM), k_g)
    v = v.reshape(B, L, N_KV_HEADS, HEAD_DIM)
    return q, k, v


def merge_branches(y_conv, o_attn, gate_logits, w_bc, w_ba, w_out):
    g_conv, g_attn = jnp.split(jax.nn.sigmoid(gate_logits), 2, axis=-1)
    return (g_conv * (y_conv @ w_bc) + g_attn * (o_attn @ w_ba)) @ w_out


def _fwd_setup_inputs(seed: int = 0) -> dict:
    key = jax.random.key(seed)
    ks = jax.random.split(key, 24)

    def nrm(k, shape, scale):
        return jax.random.normal(k, shape, jnp.float32) * scale

    def gain(k, shape):
        return 1.0 + 0.1 * jax.random.normal(k, shape, jnp.float32)

    L = DEPTH
    return {
        "x": nrm(ks[0], (BATCH, SEQ, D_MODEL), 1.0),
        "c": nrm(ks[1], (BATCH, D_MODEL), 1.0),
        "ctx": nrm(ks[2], (BATCH, CTX_LEN, D_MODEL), 1.0),
        "c_ctx": nrm(ks[3], (D_MODEL,), 1.0),
        "w_mod": nrm(ks[4], (L, D_MODEL, N_MOD * D_MODEL), 0.5 * D_MODEL ** -0.5),
        "b_mod": nrm(ks[5], (L, N_MOD * D_MODEL), 0.02),
        "norm1_g": gain(ks[6], (L, D_MODEL)),
        "norm2_g": gain(ks[7], (L, D_MODEL)),
        "norm3_g": gain(ks[8], (L, D_MODEL)),
        "ffn1_w_in": nrm(ks[9], (L, D_MODEL, 2 * D_FF), D_MODEL ** -0.5),
        "ffn1_w_out": nrm(ks[10], (L, D_FF, D_MODEL), D_FF ** -0.5),
        "w_in": nrm(ks[11], (L, D_MODEL, D_IN), D_MODEL ** -0.5),
        "conv_w": nrm(ks[12], (L, CONV_W, D_CONV), CONV_W ** -0.5),
        "q_norm_g": gain(ks[13], (L, HEAD_DIM)),
        "k_norm_g": gain(ks[14], (L, HEAD_DIM)),
        "w_branch_conv": nrm(ks[15], (L, D_CONV, D_MODEL), D_CONV ** -0.5),
        "w_branch_attn": nrm(ks[16], (L, Q_W, D_MODEL), Q_W ** -0.5),
        "w_out": nrm(ks[17], (L, D_MODEL, D_MODEL), D_MODEL ** -0.5),
        "ffn2_w_in": nrm(ks[18], (L, D_MODEL, 2 * D_FF), D_MODEL ** -0.5),
        "ffn2_w_out": nrm(ks[19], (L, D_FF, D_MODEL), D_FF ** -0.5),
        "final_g": gain(ks[20], (D_MODEL,)),
    }


def _fwd_reference(x, c, ctx, c_ctx, w_mod, b_mod, norm1_g, norm2_g, norm3_g,
              ffn1_w_in, ffn1_w_out, w_in, conv_w, q_norm_g, k_norm_g,
              w_branch_conv, w_branch_attn, w_out, ffn2_w_in, ffn2_w_out, final_g):
    B, S, _ = x.shape
    cos, sin = axial_rope_tables(S)
    cx = ctx
    for layer in range(DEPTH):
        last = layer == DEPTH - 1
        mod_lat = (jax.nn.silu(c) @ w_mod[layer] + b_mod[layer])[:, None, :]
        mod_ctx = (jax.nn.silu(c_ctx) @ w_mod[layer] + b_mod[layer])[None, None, :]
        ml = jnp.split(mod_lat, N_MOD, axis=-1)
        mc = jnp.split(mod_ctx, N_MOD, axis=-1)

        x = x + 0.5 * ml[2] * swiglu(modulate(rms_norm(x, norm1_g[layer]), ml[0], ml[1]),
                                      ffn1_w_in[layer], ffn1_w_out[layer])
        cx = cx + 0.5 * mc[2] * swiglu(modulate(rms_norm(cx, norm1_g[layer]), mc[0], mc[1]),
                                        ffn1_w_in[layer], ffn1_w_out[layer])

        hx = modulate(rms_norm(x, norm2_g[layer]), ml[3], ml[4])
        hc = modulate(rms_norm(cx, norm2_g[layer]), mc[3], mc[4])
        bg_l, cg_l, vc_l, q_l, k_l, v_l, gt_l = jnp.split(hx @ w_in[layer], SPLITS, axis=-1)
        bg_c, cg_c, vc_c, q_c, k_c, v_c, gt_c = jnp.split(hc @ w_in[layer], SPLITS, axis=-1)

        y_conv_l = conv_branch(bg_l, cg_l, vc_l, conv_w[layer])

        q_l, k_l, v_l = split_heads(q_l, k_l, v_l, q_norm_g[layer], k_norm_g[layer])
        q_c, k_c, v_c = split_heads(q_c, k_c, v_c, q_norm_g[layer], k_norm_g[layer])
        q_l = apply_axial_rope(q_l, cos, sin)
        k_l = apply_axial_rope(k_l, cos, sin)
        k_all = jnp.concatenate([k_c, k_l], axis=1)
        v_all = jnp.concatenate([v_c, v_l], axis=1)
        o_l = latent_attention(q_l, k_all, v_all)

        x = x + ml[5] * merge_branches(y_conv_l, o_l, gt_l, w_branch_conv[layer],
                                       w_branch_attn[layer], w_out[layer])

        if not last:
            y_conv_c = conv_branch(bg_c, cg_c, vc_c, conv_w[layer])
            o_c = gqa_attend(q_c.reshape(B, cx.shape[1], N_KV_HEADS, GROUP, HEAD_DIM), k_c, v_c)
            o_c = o_c.reshape(B, cx.shape[1], Q_W)
            cx = cx + mc[5] * merge_branches(y_conv_c, o_c, gt_c, w_branch_conv[layer],
                                             w_branch_attn[layer], w_out[layer])
            cx = cx + 0.5 * mc[8] * swiglu(modulate(rms_norm(cx, norm3_g[layer]), mc[6], mc[7]),
                                            ffn2_w_in[layer], ffn2_w_out[layer])

        x = x + 0.5 * ml[8] * swiglu(modulate(rms_norm(x, norm3_g[layer]), ml[6], ml[7]),
                                      ffn2_w_in[layer], ffn2_w_out[layer])
    return rms_norm(x, final_g)


import jax as _jax
import jax.numpy as _jnp

TWIN_FORMAT = 'train_step'
FWD_PARAMS = ['x', 'c', 'ctx', 'c_ctx', 'w_mod', 'b_mod', 'norm1_g', 'norm2_g', 'norm3_g', 'ffn1_w_in', 'ffn1_w_out', 'w_in', 'conv_w', 'q_norm_g', 'k_norm_g', 'w_branch_conv', 'w_branch_attn', 'w_out', 'ffn2_w_in', 'ffn2_w_out', 'final_g']
TWIN_WEIGHTS = ['c_ctx', 'w_mod', 'b_mod', 'norm1_g', 'norm2_g', 'norm3_g', 'ffn1_w_in', 'ffn1_w_out', 'w_in', 'conv_w', 'q_norm_g', 'k_norm_g', 'w_branch_conv', 'w_branch_attn', 'w_out', 'ffn2_w_in', 'ffn2_w_out', 'final_g']
TWIN_DIFF_INPUT = 'x'
TWIN_INPUTS = ['x', 'c', 'ctx', 'c_ctx', 'w_mod', 'b_mod', 'norm1_g', 'norm2_g', 'norm3_g', 'ffn1_w_in', 'ffn1_w_out', 'w_in', 'conv_w', 'q_norm_g', 'k_norm_g', 'w_branch_conv', 'w_branch_attn', 'w_out', 'ffn2_w_in', 'ffn2_w_out', 'final_g', 'loss_target', 'm_c_ctx', 'm_w_mod', 'm_b_mod', 'm_norm1_g', 'm_norm2_g', 'm_norm3_g', 'm_ffn1_w_in', 'm_ffn1_w_out', 'm_w_in', 'm_conv_w', 'm_q_norm_g', 'm_k_norm_g', 'm_w_branch_conv', 'm_w_branch_attn', 'm_w_out', 'm_ffn2_w_in', 'm_ffn2_w_out', 'm_final_g', 'v_c_ctx', 'v_w_mod', 'v_b_mod', 'v_norm1_g', 'v_norm2_g', 'v_norm3_g', 'v_ffn1_w_in', 'v_ffn1_w_out', 'v_w_in', 'v_conv_w', 'v_q_norm_g', 'v_k_norm_g', 'v_w_branch_conv', 'v_w_branch_attn', 'v_w_out', 'v_ffn2_w_in', 'v_ffn2_w_out', 'v_final_g']
TWIN_OUTPUTS = ['loss', 'grad_x', 'grad_c_ctx', 'grad_w_mod', 'grad_b_mod', 'grad_norm1_g', 'grad_norm2_g', 'grad_norm3_g', 'grad_ffn1_w_in', 'grad_ffn1_w_out', 'grad_w_in', 'grad_conv_w', 'grad_q_norm_g', 'grad_k_norm_g', 'grad_w_branch_conv', 'grad_w_branch_attn', 'grad_w_out', 'grad_ffn2_w_in', 'grad_ffn2_w_out', 'grad_final_g', 'delta_c_ctx', 'delta_w_mod', 'delta_b_mod', 'delta_norm1_g', 'delta_norm2_g', 'delta_norm3_g', 'delta_ffn1_w_in', 'delta_ffn1_w_out', 'delta_w_in', 'delta_conv_w', 'delta_q_norm_g', 'delta_k_norm_g', 'delta_w_branch_conv', 'delta_w_branch_attn', 'delta_w_out', 'delta_ffn2_w_in', 'delta_ffn2_w_out', 'delta_final_g', 'new_m_c_ctx', 'new_m_w_mod', 'new_m_b_mod', 'new_m_norm1_g', 'new_m_norm2_g', 'new_m_norm3_g', 'new_m_ffn1_w_in', 'new_m_ffn1_w_out', 'new_m_w_in', 'new_m_conv_w', 'new_m_q_norm_g', 'new_m_k_norm_g', 'new_m_w_branch_conv', 'new_m_w_branch_attn', 'new_m_w_out', 'new_m_ffn2_w_in', 'new_m_ffn2_w_out', 'new_m_final_g', 'new_v_c_ctx', 'new_v_w_mod', 'new_v_b_mod', 'new_v_norm1_g', 'new_v_norm2_g', 'new_v_norm3_g', 'new_v_ffn1_w_in', 'new_v_ffn1_w_out', 'new_v_w_in', 'new_v_conv_w', 'new_v_q_norm_g', 'new_v_k_norm_g', 'new_v_w_branch_conv', 'new_v_w_branch_attn', 'new_v_w_out', 'new_v_ffn2_w_in', 'new_v_ffn2_w_out', 'new_v_final_g']
TWIN_LEAF_KINDS = {'loss': 'loss', 'grad_x': 'grad_x', 'grad_c_ctx': 'grad_w', 'grad_w_mod': 'grad_w', 'grad_b_mod': 'grad_w', 'grad_norm1_g': 'grad_w', 'grad_norm2_g': 'grad_w', 'grad_norm3_g': 'grad_w', 'grad_ffn1_w_in': 'grad_w', 'grad_ffn1_w_out': 'grad_w', 'grad_w_in': 'grad_w', 'grad_conv_w': 'grad_w', 'grad_q_norm_g': 'grad_w', 'grad_k_norm_g': 'grad_w', 'grad_w_branch_conv': 'grad_w', 'grad_w_branch_attn': 'grad_w', 'grad_w_out': 'grad_w', 'grad_ffn2_w_in': 'grad_w', 'grad_ffn2_w_out': 'grad_w', 'grad_final_g': 'grad_w', 'delta_c_ctx': 'delta_w', 'delta_w_mod': 'delta_w', 'delta_b_mod': 'delta_w', 'delta_norm1_g': 'delta_w', 'delta_norm2_g': 'delta_w', 'delta_norm3_g': 'delta_w', 'delta_ffn1_w_in': 'delta_w', 'delta_ffn1_w_out': 'delta_w', 'delta_w_in': 'delta_w', 'delta_conv_w': 'delta_w', 'delta_q_norm_g': 'delta_w', 'delta_k_norm_g': 'delta_w', 'delta_w_branch_conv': 'delta_w', 'delta_w_branch_attn': 'delta_w', 'delta_w_out': 'delta_w', 'delta_ffn2_w_in': 'delta_w', 'delta_ffn2_w_out': 'delta_w', 'delta_final_g': 'delta_w', 'new_m_c_ctx': 'new_m', 'new_m_w_mod': 'new_m', 'new_m_b_mod': 'new_m', 'new_m_norm1_g': 'new_m', 'new_m_norm2_g': 'new_m', 'new_m_norm3_g': 'new_m', 'new_m_ffn1_w_in': 'new_m', 'new_m_ffn1_w_out': 'new_m', 'new_m_w_in': 'new_m', 'new_m_conv_w': 'new_m', 'new_m_q_norm_g': 'new_m', 'new_m_k_norm_g': 'new_m', 'new_m_w_branch_conv': 'new_m', 'new_m_w_branch_attn': 'new_m', 'new_m_w_out': 'new_m', 'new_m_ffn2_w_in': 'new_m', 'new_m_ffn2_w_out': 'new_m', 'new_m_final_g': 'new_m', 'new_v_c_ctx': 'new_v', 'new_v_w_mod': 'new_v', 'new_v_b_mod': 'new_v', 'new_v_norm1_g': 'new_v', 'new_v_norm2_g': 'new_v', 'new_v_norm3_g': 'new_v', 'new_v_ffn1_w_in': 'new_v', 'new_v_ffn1_w_out': 'new_v', 'new_v_w_in': 'new_v', 'new_v_conv_w': 'new_v', 'new_v_q_norm_g': 'new_v', 'new_v_k_norm_g': 'new_v', 'new_v_w_branch_conv': 'new_v', 'new_v_w_branch_attn': 'new_v', 'new_v_w_out': 'new_v', 'new_v_ffn2_w_in': 'new_v', 'new_v_ffn2_w_out': 'new_v', 'new_v_final_g': 'new_v'}


def _forward(args):
    return _fwd_reference(*[args[k] for k in FWD_PARAMS])


def _output_shape():
    def fwd():
        inp = _fwd_setup_inputs(0)
        return _fwd_reference(*[inp[k] for k in FWD_PARAMS])
    out = _jax.eval_shape(fwd)
    return out.shape, out.dtype

N_MICROBATCH = 1
ADAM_LR = 0.001
ADAM_B1 = 0.9
ADAM_B2 = 0.999
ADAM_EPS = 1e-08
ADAM_WD = 0.01
ADAM_STEP = 10
PER_EXAMPLE_BATCH_AXIS = {'x': 0, 'c': 0, 'ctx': 0, 'loss_target': 0}
SHARED_INPUTS = []
_WEIGHT_DTYPES = {'c_ctx': _jnp.float32, 'w_mod': _jnp.float32, 'b_mod': _jnp.float32, 'norm1_g': _jnp.float32, 'norm2_g': _jnp.float32, 'norm3_g': _jnp.float32, 'ffn1_w_in': _jnp.float32, 'ffn1_w_out': _jnp.float32, 'w_in': _jnp.float32, 'conv_w': _jnp.float32, 'q_norm_g': _jnp.float32, 'k_norm_g': _jnp.float32, 'w_branch_conv': _jnp.float32, 'w_branch_attn': _jnp.float32, 'w_out': _jnp.float32, 'ffn2_w_in': _jnp.float32, 'ffn2_w_out': _jnp.float32, 'final_g': _jnp.float32}
MOMENT_SCALE = {'c_ctx': 6.265952e-03, 'w_mod': 1.002246e-01, 'b_mod': 2.028359e-01, 'norm1_g': 4.039178e-02, 'norm2_g': 9.125426e-02, 'norm3_g': 3.787674e-02, 'ffn1_w_in': 1.819948e-02, 'ffn1_w_out': 2.984973e-02, 'w_in': 3.731866e-02, 'conv_w': 5.423642e-02, 'q_norm_g': 1.013451e-02, 'k_norm_g': 1.027249e-02, 'w_branch_conv': 5.365724e-02, 'w_branch_attn': 1.343256e-02, 'w_out': 5.600329e-02, 'ffn2_w_in': 1.700371e-02, 'ffn2_w_out': 2.793941e-02, 'final_g': 6.461465e+01}


def _to_microbatches(a, axis):
    t = _jnp.moveaxis(a, axis, 0)
    t = t.reshape((N_MICROBATCH, t.shape[0] // N_MICROBATCH) + t.shape[1:])
    return _jnp.moveaxis(t, 1, axis + 1)


def setup_inputs(seed: int = 0) -> dict:
    inp = _fwd_setup_inputs(seed)
    key = _jax.random.fold_in(_jax.random.key(seed), 7919)
    shape, _ = _output_shape()
    out = dict(inp)
    out["loss_target"] = _jax.random.normal(_jax.random.fold_in(key, 0), shape, _jnp.float32)
    for i, name in enumerate(TWIN_WEIGHTS):
        w = inp[name].astype(_jnp.float32)
        if MOMENT_SCALE is None:
            s = _jnp.sqrt(_jnp.mean(_jnp.square(w)) + 1e-30)
        else:
            s = MOMENT_SCALE[name]
        km, kv = _jax.random.split(_jax.random.fold_in(key, i + 1))
        out[name] = w
        out["m_" + name] = s * _jax.random.normal(km, w.shape, _jnp.float32)
        out["v_" + name] = (s * s) * _jax.random.uniform(kv, w.shape, _jnp.float32, 0.5, 1.5)
    if N_MICROBATCH > 1:
        for name, axis in PER_EXAMPLE_BATCH_AXIS.items():
            out[name] = _to_microbatches(out[name], axis)
    return {'x': out['x'], 'c': out['c'], 'ctx': out['ctx'], 'c_ctx': out['c_ctx'], 'w_mod': out['w_mod'], 'b_mod': out['b_mod'], 'norm1_g': out['norm1_g'], 'norm2_g': out['norm2_g'], 'norm3_g': out['norm3_g'], 'ffn1_w_in': out['ffn1_w_in'], 'ffn1_w_out': out['ffn1_w_out'], 'w_in': out['w_in'], 'conv_w': out['conv_w'], 'q_norm_g': out['q_norm_g'], 'k_norm_g': out['k_norm_g'], 'w_branch_conv': out['w_branch_conv'], 'w_branch_attn': out['w_branch_attn'], 'w_out': out['w_out'], 'ffn2_w_in': out['ffn2_w_in'], 'ffn2_w_out': out['ffn2_w_out'], 'final_g': out['final_g'], 'loss_target': out['loss_target'], 'm_c_ctx': out['m_c_ctx'], 'm_w_mod': out['m_w_mod'], 'm_b_mod': out['m_b_mod'], 'm_norm1_g': out['m_norm1_g'], 'm_norm2_g': out['m_norm2_g'], 'm_norm3_g': out['m_norm3_g'], 'm_ffn1_w_in': out['m_ffn1_w_in'], 'm_ffn1_w_out': out['m_ffn1_w_out'], 'm_w_in': out['m_w_in'], 'm_conv_w': out['m_conv_w'], 'm_q_norm_g': out['m_q_norm_g'], 'm_k_norm_g': out['m_k_norm_g'], 'm_w_branch_conv': out['m_w_branch_conv'], 'm_w_branch_attn': out['m_w_branch_attn'], 'm_w_out': out['m_w_out'], 'm_ffn2_w_in': out['m_ffn2_w_in'], 'm_ffn2_w_out': out['m_ffn2_w_out'], 'm_final_g': out['m_final_g'], 'v_c_ctx': out['v_c_ctx'], 'v_w_mod': out['v_w_mod'], 'v_b_mod': out['v_b_mod'], 'v_norm1_g': out['v_norm1_g'], 'v_norm2_g': out['v_norm2_g'], 'v_norm3_g': out['v_norm3_g'], 'v_ffn1_w_in': out['v_ffn1_w_in'], 'v_ffn1_w_out': out['v_ffn1_w_out'], 'v_w_in': out['v_w_in'], 'v_conv_w': out['v_conv_w'], 'v_q_norm_g': out['v_q_norm_g'], 'v_k_norm_g': out['v_k_norm_g'], 'v_w_branch_conv': out['v_w_branch_conv'], 'v_w_branch_attn': out['v_w_branch_attn'], 'v_w_out': out['v_w_out'], 'v_ffn2_w_in': out['v_ffn2_w_in'], 'v_ffn2_w_out': out['v_ffn2_w_out'], 'v_final_g': out['v_final_g']}


def _loss(weights, diff, rest, loss_target):
    with _jax.named_scope("forward"):
        args = {**rest, TWIN_DIFF_INPUT: diff, **{k: w.astype(_WEIGHT_DTYPES[k]) for k, w in weights.items()}}
        y = _forward(args)
    with _jax.named_scope("loss_head"):
        err = _jnp.square(y.astype(_jnp.float32) - loss_target)
        return 0.5 * _jnp.sum(_jnp.mean(err, axis=-1)) if err.ndim else 0.5 * err


def _adamw(w, g, m, v):
    m = ADAM_B1 * m + (1.0 - ADAM_B1) * g
    v = ADAM_B2 * v + (1.0 - ADAM_B2) * _jnp.square(g)
    m_hat = m / (1.0 - ADAM_B1 ** ADAM_STEP)
    v_hat = v / (1.0 - ADAM_B2 ** ADAM_STEP)
    delta = -ADAM_LR * (m_hat / (_jnp.sqrt(v_hat) + ADAM_EPS) + ADAM_WD * w)
    return delta, m, v


def reference(x, c, ctx, c_ctx, w_mod, b_mod, norm1_g, norm2_g, norm3_g, ffn1_w_in, ffn1_w_out, w_in, conv_w, q_norm_g, k_norm_g, w_branch_conv, w_branch_attn, w_out, ffn2_w_in, ffn2_w_out, final_g, loss_target, m_c_ctx, m_w_mod, m_b_mod, m_norm1_g, m_norm2_g, m_norm3_g, m_ffn1_w_in, m_ffn1_w_out, m_w_in, m_conv_w, m_q_norm_g, m_k_norm_g, m_w_branch_conv, m_w_branch_attn, m_w_out, m_ffn2_w_in, m_ffn2_w_out, m_final_g, v_c_ctx, v_w_mod, v_b_mod, v_norm1_g, v_norm2_g, v_norm3_g, v_ffn1_w_in, v_ffn1_w_out, v_w_in, v_conv_w, v_q_norm_g, v_k_norm_g, v_w_branch_conv, v_w_branch_attn, v_w_out, v_ffn2_w_in, v_ffn2_w_out, v_final_g):
    given = dict(x=x, c=c, ctx=ctx, c_ctx=c_ctx, w_mod=w_mod, b_mod=b_mod, norm1_g=norm1_g, norm2_g=norm2_g, norm3_g=norm3_g, ffn1_w_in=ffn1_w_in, ffn1_w_out=ffn1_w_out, w_in=w_in, conv_w=conv_w, q_norm_g=q_norm_g, k_norm_g=k_norm_g, w_branch_conv=w_branch_conv, w_branch_attn=w_branch_attn, w_out=w_out, ffn2_w_in=ffn2_w_in, ffn2_w_out=ffn2_w_out, final_g=final_g, loss_target=loss_target, m_c_ctx=m_c_ctx, m_w_mod=m_w_mod, m_b_mod=m_b_mod, m_norm1_g=m_norm1_g, m_norm2_g=m_norm2_g, m_norm3_g=m_norm3_g, m_ffn1_w_in=m_ffn1_w_in, m_ffn1_w_out=m_ffn1_w_out, m_w_in=m_w_in, m_conv_w=m_conv_w, m_q_norm_g=m_q_norm_g, m_k_norm_g=m_k_norm_g, m_w_branch_conv=m_w_branch_conv, m_w_branch_attn=m_w_branch_attn, m_w_out=m_w_out, m_ffn2_w_in=m_ffn2_w_in, m_ffn2_w_out=m_ffn2_w_out, m_final_g=m_final_g, v_c_ctx=v_c_ctx, v_w_mod=v_w_mod, v_b_mod=v_b_mod, v_norm1_g=v_norm1_g, v_norm2_g=v_norm2_g, v_norm3_g=v_norm3_g, v_ffn1_w_in=v_ffn1_w_in, v_ffn1_w_out=v_ffn1_w_out, v_w_in=v_w_in, v_conv_w=v_conv_w, v_q_norm_g=v_q_norm_g, v_k_norm_g=v_k_norm_g, v_w_branch_conv=v_w_branch_conv, v_w_branch_attn=v_w_branch_attn, v_w_out=v_w_out, v_ffn2_w_in=v_ffn2_w_in, v_ffn2_w_out=v_ffn2_w_out, v_final_g=v_final_g)
    weights = {n: given[n] for n in TWIN_WEIGHTS}
    shared = {n: given[n] for n in SHARED_INPUTS}
    per_example = {n: given[n] for n in ['x', 'c', 'ctx']}
    grad_fn = _jax.value_and_grad(_loss, argnums=(0, 1))

    def one_microbatch(ex, loss_target):
        ex = dict(ex)
        diff = ex.pop(TWIN_DIFF_INPUT)
        return grad_fn(weights, diff, {**shared, **ex}, loss_target)

    if N_MICROBATCH == 1:
        loss, (grad_w, grad_x) = one_microbatch(per_example, given["loss_target"])
    else:
        def body(carry, xs):
            loss_sum, grad_sum = carry
            l_k, (gw_k, gx_k) = one_microbatch(xs[0], xs[1])
            with _jax.named_scope("update"):
                return (loss_sum + l_k, _jax.tree.map(_jnp.add, grad_sum, gw_k)), gx_k

        init = (_jnp.zeros((), _jnp.float32), _jax.tree.map(_jnp.zeros_like, weights))
        (loss, grad_w), grad_x = _jax.lax.scan(body, init, (per_example, given["loss_target"]))
    with _jax.named_scope("update"):
        delta_w, new_m, new_v = {}, {}, {}
        for n in TWIN_WEIGHTS:
            delta_w[n], new_m[n], new_v[n] = _adamw(weights[n], grad_w[n], given["m_" + n], given["v_" + n])
    return (loss, grad_x, *[grad_w[n] for n in TWIN_WEIGHTS], *[delta_w[n] for n in TWIN_WEIGHTS],
            *[new_m[n] for n in TWIN_WEIGHTS], *[new_v[n] for n in TWIN_WEIGHTS])
```

```python
import functools

import jax
import jax.numpy as jnp
from jax import lax
from jax.experimental import pallas as pl
from jax.experimental.pallas import tpu as pltpu

F32 = jnp.float32
BF = jnp.bfloat16
EPS = 1e-6
N_DEV = 8
HEAD_DIM = 128
N_Q_HEADS = 8
N_KV_HEADS = 2
GROUP = N_Q_HEADS // N_KV_HEADS
GRID_W = 64
ROPE_THETA = 10000.0
CONV_TAPS = 3
N_MOD = 9
ADAM_LR = 0.001
ADAM_B1 = 0.9
ADAM_B2 = 0.999
ADAM_EPS = 1e-08
ADAM_WD = 0.01
ADAM_STEP = 10
ROW_TILE = 256
VMEM_BIG = 56 << 20
MESH_ID = pl.DeviceIdType.MESH
HIGHEST = lax.Precision.HIGHEST
NT = (((1,), (1,)), ((), ()))
TN = (((0,), (0,)), ((), ()))


def _pick(n, cands):
    for c in cands:
        if n % c == 0:
            return c
    return n


def _params(vmem=None, sem=None):
    kw = {}
    if vmem is not None:
        kw["vmem_limit_bytes"] = vmem
    if sem is not None:
        kw["dimension_semantics"] = sem
    return pltpu.CompilerParams(**kw)


def _resident(shape):
    nd = len(shape)
    return pl.BlockSpec(shape, lambda *_: (0,) * nd, pipeline_mode=pl.Buffered(1))


def _sigmoid(x):
    return jax.nn.sigmoid(x)


def _coords():
    return lax.axis_index("x"), lax.axis_index("y"), lax.axis_index("c")


def _flip(v, bit):
    return 1 - v if bit else v


def _remote(src, dst, ssem, rsem, dev):
    return pltpu.make_async_remote_copy(src_ref=src, dst_ref=dst, send_sem=ssem, recv_sem=rsem,
                                        device_id=dev, device_id_type=MESH_ID)


def allgather_direct(v, name):
    def body(v_ref, out_ref, ssem, rsem, lsem):
        x, y, c = _coords()
        me = 4 * x + 2 * y + c
        mine = pltpu.make_async_copy(v_ref, out_ref.at[me], lsem)
        mine.start()
        cps = []
        for p in range(1, N_DEV):
            px, py, pc = (p >> 2) & 1, (p >> 1) & 1, p & 1
            cps.append(_remote(v_ref, out_ref.at[me], ssem.at[p - 1], rsem.at[p - 1],
                               (_flip(x, px), _flip(y, py), _flip(c, pc))))
        for cp in cps:
            cp.start()
        for p in range(1, N_DEV):
            px, py, pc = (p >> 2) & 1, (p >> 1) & 1, p & 1
            src = 4 * _flip(x, px) + 2 * _flip(y, py) + _flip(c, pc)
            _remote(v_ref, out_ref.at[src], ssem.at[p - 1], rsem.at[p - 1], (x, y, c)).wait_recv()
        for cp in cps:
            cp.wait_send()
        mine.wait()

    return pl.pallas_call(
        body, name=name,
        out_shape=jax.ShapeDtypeStruct((N_DEV,) + v.shape, v.dtype),
        in_specs=[pl.BlockSpec(memory_space=pl.ANY)],
        out_specs=pl.BlockSpec(memory_space=pl.ANY),
        scratch_shapes=[pltpu.SemaphoreType.DMA((N_DEV - 1,)), pltpu.SemaphoreType.DMA((N_DEV - 1,)),
                        pltpu.SemaphoreType.DMA],
    )(v)


def allgather_two_level(v, name):
    def body(v_ref, out_ref, ssem, rsem, lsem):
        x, y, c = _coords()
        sib = (x, y, 1 - c)
        chips = [(1 - x, y), (x, 1 - y), (1 - x, 1 - y)]

        def slot(px, py, pc):
            return out_ref.at[4 * px + 2 * py + pc]

        mine = pltpu.make_async_copy(v_ref, slot(x, y, c), lsem)
        mine.start()
        first = [_remote(v_ref, slot(x, y, c), ssem.at[0], rsem.at[0], sib)]
        first += [_remote(v_ref, slot(x, y, c), ssem.at[1 + j], rsem.at[1 + j], (*chip, c))
                  for j, chip in enumerate(chips)]
        for cp in first:
            cp.start()
        passed = [_remote(slot(*chip, c), slot(*chip, c), ssem.at[4 + j], rsem.at[4 + j], sib)
                  for j, chip in enumerate(chips)]
        for j, chip in enumerate(chips):
            _remote(v_ref, slot(*chip, c), ssem.at[1 + j], rsem.at[1 + j], (x, y, c)).wait_recv()
            passed[j].start()
        _remote(v_ref, slot(x, y, 1 - c), ssem.at[0], rsem.at[0], (x, y, c)).wait_recv()
        for j, chip in enumerate(chips):
            _remote(v_ref, slot(*chip, 1 - c), ssem.at[4 + j], rsem.at[4 + j], (x, y, c)).wait_recv()
        for cp in first + passed:
            cp.wait_send()
        mine.wait()

    return pl.pallas_call(
        body, name=name,
        out_shape=jax.ShapeDtypeStruct((N_DEV,) + v.shape, v.dtype),
        in_specs=[pl.BlockSpec(memory_space=pl.ANY)],
        out_specs=pl.BlockSpec(memory_space=pl.ANY),
        scratch_shapes=[pltpu.SemaphoreType.DMA((7,)), pltpu.SemaphoreType.DMA((7,)), pltpu.SemaphoreType.DMA],
    )(v)


def rs_sibling_exchange(gs, name):
    def body(g_ref, out_ref, ssem, rsem):
        x, y, c = _coords()
        sib = (x, y, 1 - c)
        cps = [_remote(g_ref.at[2 * k + (1 - c)], out_ref.at[k], ssem.at[k], rsem.at[k], sib) for k in range(4)]
        for cp in cps:
            cp.start()
        for cp in cps:
            cp.wait()

    return pl.pallas_call(
        body, name=name,
        out_shape=jax.ShapeDtypeStruct((4,) + gs.shape[1:], gs.dtype),
        in_specs=[pl.BlockSpec(memory_space=pl.ANY)],
        out_specs=pl.BlockSpec(memory_space=pl.ANY),
        scratch_shapes=[pltpu.SemaphoreType.DMA((4,)), pltpu.SemaphoreType.DMA((4,))],
    )(gs)


def rs_chip_exchange(ps, name):
    def body(p_ref, out_ref, ssem, rsem):
        x, y, c = _coords()
        chips = [(1 - x, y), (x, 1 - y), (1 - x, 1 - y)]
        cps = [_remote(p_ref.at[2 * cx + cy], out_ref.at[j], ssem.at[j], rsem.at[j], (cx, cy, c))
               for j, (cx, cy) in enumerate(chips)]
        for cp in cps:
            cp.start()
        for cp in cps:
            cp.wait()

    return pl.pallas_call(
        body, name=name,
        out_shape=jax.ShapeDtypeStruct((3,) + ps.shape[1:], ps.dtype),
        in_specs=[pl.BlockSpec(memory_space=pl.ANY)],
        out_specs=pl.BlockSpec(memory_space=pl.ANY),
        scratch_shapes=[pltpu.SemaphoreType.DMA((3,)), pltpu.SemaphoreType.DMA((3,))],
    )(ps)


def rs_pair_sum(place, gs, recv_a, name):
    _, R, C = gs.shape
    tr = _pick(R, (512, 256, 128, 64, 32, 16))

    def body(place_ref, g_ref, a_ref, o_ref):
        o_ref[0] = (g_ref[0].astype(F32) + a_ref[0].astype(F32)).astype(o_ref.dtype)

    return pl.pallas_call(
        body, name=name,
        grid_spec=pltpu.PrefetchScalarGridSpec(
            num_scalar_prefetch=1, grid=(4, R // tr),
            in_specs=[pl.BlockSpec((1, tr, C), lambda k, r, pr: (2 * k + pr[0], r, 0)),
                      pl.BlockSpec((1, tr, C), lambda k, r, pr: (k, r, 0))],
            out_specs=pl.BlockSpec((1, tr, C), lambda k, r, pr: (k, r, 0))),
        out_shape=jax.ShapeDtypeStruct((4, R, C), BF),
    )(place, gs, recv_a)


def rs_final_sum(place, ps, recv_b, name):
    _, R, C = ps.shape
    tr = _pick(R, (512, 256, 128, 64, 32, 16))

    def body(place_ref, p_ref, b_ref, o_ref):
        acc = p_ref[0].astype(F32)
        for j in range(3):
            acc = acc + b_ref[j].astype(F32)
        o_ref[...] = acc

    return pl.pallas_call(
        body, name=name,
        grid_spec=pltpu.PrefetchScalarGridSpec(
            num_scalar_prefetch=1, grid=(R // tr,),
            in_specs=[pl.BlockSpec((1, tr, C), lambda r, pr: (pr[1], r, 0)),
                      pl.BlockSpec((3, tr, C), lambda r, pr: (0, r, 0))],
            out_specs=pl.BlockSpec((tr, C), lambda r, pr: (r, 0))),
        out_shape=jax.ShapeDtypeStruct((R, C), F32),
    )(place, ps, recv_b)


def _cond_rows(call_ref, cctx_ref, z_ref, D):
    z_ref[...] = jnp.zeros_like(z_ref)
    for a in range(N_DEV):
        z_ref[a:a + 1, :] = call_ref[a][:, :D]
    z_ref[N_DEV:N_DEV + 1, :] = cctx_ref[...]


def mod_forward(call, c_ctx, w_loc, b_loc, name):
    D, cols = w_loc.shape

    def body(call_ref, cctx_ref, w_ref, b_ref, o_ref, z_ref):
        _cond_rows(call_ref, cctx_ref, z_ref, D)
        z = z_ref[...]
        s = z * _sigmoid(z)
        o_ref[...] = jnp.dot(s, w_ref[...], preferred_element_type=F32, precision=HIGHEST) + b_ref[...]

    return pl.pallas_call(
        body, name=name, out_shape=jax.ShapeDtypeStruct((16, cols), F32),
        scratch_shapes=[pltpu.VMEM((16, D), F32)],
        compiler_params=_params(vmem=VMEM_BIG),
    )(call, c_ctx, w_loc, b_loc)


def mod_backward(call, c_ctx, w_loc, dlat_loc, dctx_loc, gath, n_small_off, n_small, name):
    D, cols = w_loc.shape

    def body(call_ref, cctx_ref, w_ref, dlat_ref, dctx_ref, g_ref, gw_ref, pc_ref, small_ref, dctx_sum_ref, z_ref, dm_ref):
        _cond_rows(call_ref, cctx_ref, z_ref, D)
        z = z_ref[...]
        s = z * _sigmoid(z)
        dctx = dctx_ref[0:1, :]
        for a in range(1, N_DEV):
            dctx = dctx + dctx_ref[a:a + 1, :]
        dm_ref[...] = jnp.zeros_like(dm_ref)
        dm_ref[0:N_DEV, :] = dlat_ref[...]
        dm_ref[N_DEV:N_DEV + 1, :] = dctx
        gw_ref[...] = lax.dot_general(s, dm_ref[...], TN, preferred_element_type=F32, precision=HIGHEST)
        pc_ref[...] = lax.dot_general(dctx, w_ref[...], NT, preferred_element_type=F32, precision=HIGHEST)
        dctx_sum_ref[...] = dctx
        acc = g_ref[0][:, n_small_off:n_small_off + n_small]
        for a in range(1, N_DEV):
            acc = acc + g_ref[a][:, n_small_off:n_small_off + n_small]
        small_ref[...] = acc

    return pl.pallas_call(
        body, name=name,
        out_shape=(jax.ShapeDtypeStruct((D, cols), F32), jax.ShapeDtypeStruct((1, D), F32),
                   jax.ShapeDtypeStruct((1, n_small), F32), jax.ShapeDtypeStruct((1, cols), F32)),
        scratch_shapes=[pltpu.VMEM((16, D), F32), pltpu.VMEM((16, cols), F32)],
        compiler_params=_params(vmem=VMEM_BIG),
    )(call, c_ctx, w_loc, dlat_loc, dctx_loc, gath)


def bmod_and_cctx_grad(gath, pcs, c_ctx, nd, name):
    D = c_ctx.shape[-1]

    def body(g_ref, pc_ref, cctx_ref, gb_ref, gc_ref):
        acc = g_ref[0][:, :nd] + g_ref[0][:, nd:2 * nd]
        for a in range(1, N_DEV):
            acc = acc + (g_ref[a][:, :nd] + g_ref[a][:, nd:2 * nd])
        gb_ref[...] = acc
        p = pc_ref[0]
        for a in range(1, N_DEV):
            p = p + pc_ref[a]
        z = cctx_ref[...]
        sg = _sigmoid(z)
        gc_ref[...] = p * (sg * (1.0 + z * (1.0 - sg)))

    return pl.pallas_call(
        body, name=name,
        out_shape=(jax.ShapeDtypeStruct((1, nd), F32), jax.ShapeDtypeStruct((1, D), F32)),
    )(gath, pcs, c_ctx)


def _mod_spec(D, which, nctx):
    return pl.BlockSpec((1, 3, D), lambda i: (jnp.where(i < nctx, 0, 3) + which, 0, 0))


def _acc_spec(D, nctx):
    return pl.BlockSpec((1, 8, D), lambda i: (jnp.where(i < nctx, 0, 1), 0, 0))


def _hidden_chunks(F):
    n = 2 if F % 256 == 0 else 1
    return n, F // n


def ffn_forward(xin, mod6, which, g, w_in, w_out, nctx, name):
    Tr, D = xin.shape
    F = w_in.shape[1] // 2
    tm = ROW_TILE
    nch, ch = _hidden_chunks(F)

    def body(x_ref, mod_ref, g_ref, win_ref, wout_ref, xo_ref, hm_ref, ab_ref, h_ref, f_ref):
        x = x_ref[...]
        ms = mod_ref[0]
        shift, scale, gate = ms[0:1], ms[1:2], ms[2:3]
        r = lax.rsqrt(jnp.mean(x * x, axis=-1, keepdims=True) + EPS)
        hb = (((x * r) * g_ref[...]) * (1.0 + scale) + shift).astype(BF)
        hm_ref[...] = hb
        f = jnp.zeros((tm, D), F32)
        for k in range(nch):
            lo, hi = k * ch, (k + 1) * ch
            a = jnp.dot(hb, win_ref[:, lo:hi], preferred_element_type=F32)
            b = jnp.dot(hb, win_ref[:, F + lo:F + hi], preferred_element_type=F32)
            ab_ref[:, lo:hi] = a.astype(BF)
            ab_ref[:, F + lo:F + hi] = b.astype(BF)
            h = ((a * _sigmoid(a)) * b).astype(BF)
            h_ref[:, lo:hi] = h
            f = f + jnp.dot(h, wout_ref[lo:hi, :], preferred_element_type=F32)
        f_ref[...] = f.astype(BF)
        xo_ref[...] = x + (0.5 * gate) * f

    row = lambda n: pl.BlockSpec((tm, n), lambda i: (i, 0))
    return pl.pallas_call(
        body, name=name, grid=(Tr // tm,),
        in_specs=[row(D), _mod_spec(D, which, nctx), _resident((1, D)), _resident(w_in.shape), _resident(w_out.shape)],
        out_specs=[row(D), row(D), row(2 * F), row(F), row(D)],
        out_shape=[jax.ShapeDtypeStruct((Tr, D), F32), jax.ShapeDtypeStruct((Tr, D), BF),
                   jax.ShapeDtypeStruct((Tr, 2 * F), BF), jax.ShapeDtypeStruct((Tr, F), BF),
                   jax.ShapeDtypeStruct((Tr, D), BF)],
        compiler_params=_params(vmem=VMEM_BIG, sem=("arbitrary",)),
    )(xin, mod6, g, w_in, w_out)


def ffn_backward_rows(dxo, xin, mod6, which, g, ab, fo, w_in, w_out, nctx, name):
    Tr, D = xin.shape
    F = w_in.shape[1] // 2
    tm = ROW_TILE
    nch, ch = _hidden_chunks(F)

    def body(dxo_ref, x_ref, mod_ref, g_ref, ab_ref, fo_ref, win_ref, wout_ref, dx_ref, dab_ref, df_ref, acc_ref):
        i = pl.program_id(0)

        @pl.when((i == 0) | (i == nctx))
        def _():
            acc_ref[...] = jnp.zeros_like(acc_ref)

        dxo = dxo_ref[...]
        x = x_ref[...]
        ms = mod_ref[0]
        scale, gate = ms[1:2], ms[2:3]
        gg = g_ref[...]
        dgate = jnp.sum(dxo * fo_ref[...].astype(F32), axis=0, keepdims=True) * 0.5
        dfb = (dxo * (0.5 * gate)).astype(BF)
        df_ref[...] = dfb
        dhm = jnp.zeros((tm, D), F32)
        for k in range(nch):
            lo, hi = k * ch, (k + 1) * ch
            dh = lax.dot_general(dfb, wout_ref[lo:hi, :], NT, preferred_element_type=F32)
            a = ab_ref[:, lo:hi].astype(F32)
            b = ab_ref[:, F + lo:F + hi].astype(F32)
            sg = _sigmoid(a)
            da = ((dh * b) * (sg * (1.0 + a * (1.0 - sg)))).astype(BF)
            db = (dh * (a * sg)).astype(BF)
            dab_ref[:, lo:hi] = da
            dab_ref[:, F + lo:F + hi] = db
            dhm = dhm + lax.dot_general(da, win_ref[:, lo:hi], NT, preferred_element_type=F32)
            dhm = dhm + lax.dot_general(db, win_ref[:, F + lo:F + hi], NT, preferred_element_type=F32)
        r = lax.rsqrt(jnp.mean(x * x, axis=-1, keepdims=True) + EPS)
        xh = x * r
        dshift = jnp.sum(dhm, axis=0, keepdims=True)
        dscale = jnp.sum(dhm * (xh * gg), axis=0, keepdims=True)
        dxh_g = dhm * (1.0 + scale)
        dg = jnp.sum(dxh_g * xh, axis=0, keepdims=True)
        dxh = dxh_g * gg
        dx_ref[...] = dxo + r * (dxh - xh * jnp.mean(dxh * xh, axis=-1, keepdims=True))
        for k, val in enumerate((dshift, dscale, dgate, dg)):
            acc_ref[0, k:k + 1, :] += val

    row = lambda n: pl.BlockSpec((tm, n), lambda i: (i, 0))
    return pl.pallas_call(
        body, name=name, grid=(Tr // tm,),
        in_specs=[row(D), row(D), _mod_spec(D, which, nctx), _resident((1, D)), row(2 * F), row(D),
                  _resident(w_in.shape), _resident(w_out.shape)],
        out_specs=[row(D), row(2 * F), row(D), _acc_spec(D, nctx)],
        out_shape=[jax.ShapeDtypeStruct((Tr, D), F32), jax.ShapeDtypeStruct((Tr, 2 * F), BF),
                   jax.ShapeDtypeStruct((Tr, D), BF), jax.ShapeDtypeStruct((2, 8, D), F32)],
        compiler_params=_params(vmem=VMEM_BIG, sem=("arbitrary",)),
    )(dxo, xin, mod6, g, ab, fo, w_in, w_out)


def tn_matmul(a, b, name):
    T, K = a.shape
    N = b.shape[1]
    tk = _pick(K, (1024, 1408, 768, 512, 384, 256, 128))
    tn = _pick(N, (1024, 1408, 1664, 768, 512, 384, 256, 128))
    tt = _pick(T, (1024, 768, 512, 256, 128))

    def body(a_ref, b_ref, o_ref):
        @pl.when(pl.program_id(2) == 0)
        def _():
            o_ref[...] = jnp.zeros_like(o_ref)

        o_ref[...] += lax.dot_general(a_ref[...], b_ref[...], TN, preferred_element_type=F32)

    return pl.pallas_call(
        body, name=name, grid=(K // tk, N // tn, T // tt),
        in_specs=[pl.BlockSpec((tt, tk), lambda k, n, t: (t, k)), pl.BlockSpec((tt, tn), lambda k, n, t: (t, n))],
        out_specs=pl.BlockSpec((tk, tn), lambda k, n, t: (k, n)),
        out_shape=jax.ShapeDtypeStruct((K, N), F32),
        compiler_params=_params(vmem=VMEM_BIG, sem=("parallel", "parallel", "arbitrary")),
    )(a, b)


def _rope_apply(y, cos, s_next, s_prev):
    return y * cos + pltpu.roll(y, HEAD_DIM - 32, 1) * s_next + pltpu.roll(y, 32, 1) * s_prev


def _rope_transpose(dz, cos, s_next, s_prev):
    return dz * cos + pltpu.roll(dz * s_next, 32, 1) + pltpu.roll(dz * s_prev, HEAD_DIM - 32, 1)


def proj_forward(xa, mod6, g, w_in, qg, kg, tabs, offs, nctx, name):
    Tr, D = xa.shape
    P = w_in.shape[1]
    tm = ROW_TILE
    qo, ko = offs["q"], offs["k"]
    qw, kw = N_Q_HEADS * HEAD_DIM, N_KV_HEADS * HEAD_DIM
    scale_q = HEAD_DIM ** -0.5

    def body(x_ref, mod_ref, g_ref, w_ref, qg_ref, kg_ref, tab_ref, hx_ref, pr_ref, q_ref, k_ref):
        x = x_ref[...]
        ms = mod_ref[0]
        shift, scale = ms[0:1], ms[1:2]
        r = lax.rsqrt(jnp.mean(x * x, axis=-1, keepdims=True) + EPS)
        hb = (((x * r) * g_ref[...]) * (1.0 + scale) + shift).astype(BF)
        hx_ref[...] = hb
        pr = jnp.dot(hb, w_ref[...], preferred_element_type=F32)
        pr_ref[...] = pr.astype(BF)
        cos, s_next, s_prev = tab_ref[0], tab_ref[1], tab_ref[2]

        def head(v, gain):
            n = v * lax.rsqrt(jnp.mean(v * v, axis=-1, keepdims=True) + EPS)
            return _rope_apply(n * gain, cos, s_next, s_prev)

        for h in range(N_Q_HEADS):
            lo = qo + h * HEAD_DIM
            q_ref[:, h * HEAD_DIM:(h + 1) * HEAD_DIM] = (head(pr[:, lo:lo + HEAD_DIM], qg_ref[...]) * scale_q).astype(BF)
        for h in range(N_KV_HEADS):
            lo = ko + h * HEAD_DIM
            k_ref[:, h * HEAD_DIM:(h + 1) * HEAD_DIM] = head(pr[:, lo:lo + HEAD_DIM], kg_ref[...]).astype(BF)

    row = lambda n: pl.BlockSpec((tm, n), lambda i: (i, 0))
    return pl.pallas_call(
        body, name=name, grid=(Tr // tm,),
        in_specs=[row(D), _mod_spec(D, 1, nctx), _resident((1, D)), _resident(w_in.shape),
                  _resident((1, HEAD_DIM)), _resident((1, HEAD_DIM)),
                  pl.BlockSpec((3, tm, HEAD_DIM), lambda i: (0, i, 0))],
        out_specs=[row(D), row(P), row(qw), row(kw)],
        out_shape=[jax.ShapeDtypeStruct((Tr, D), BF), jax.ShapeDtypeStruct((Tr, P), BF),
                   jax.ShapeDtypeStruct((Tr, qw), BF), jax.ShapeDtypeStruct((Tr, kw), BF)],
        compiler_params=_params(vmem=VMEM_BIG, sem=("arbitrary",)),
    )(xa, mod6, g, w_in, qg, kg, tabs)


def _shifted(u, first_row, last_row):
    T = u.shape[0]
    prev = jnp.where(first_row, 0.0, pltpu.roll(u, 1, 0))
    nxt = jnp.where(last_row, 0.0, pltpu.roll(u, T - 1, 0))
    return prev, nxt


def conv_forward(proj, conv_w, offs, Tc, name):
    Ta = proj.shape[0]
    T = Ta - Tc
    Dc = conv_w.shape[1]
    nb = Dc // 128
    bo, co, vo = offs["bg"] // 128, offs["cg"] // 128, offs["vc"] // 128

    def body(b_ref, c_ref, v_ref, w_ref, y_ref):
        rows = lax.broadcasted_iota(jnp.int32, (T, 128), 0)
        u = c_ref[pl.ds(Tc, T), :].astype(F32) * v_ref[pl.ds(Tc, T), :].astype(F32)
        prev, nxt = _shifted(u, rows == 0, rows == T - 1)
        w = w_ref[...]
        cv = prev * w[0:1] + u * w[1:2] + nxt * w[2:3]
        y_ref[...] = (b_ref[pl.ds(Tc, T), :].astype(F32) * cv).astype(BF)

    col = lambda o: pl.BlockSpec((Ta, 128), lambda j: (0, o + j))
    return pl.pallas_call(
        body, name=name, grid=(nb,),
        in_specs=[col(bo), col(co), col(vo), pl.BlockSpec((CONV_TAPS, 128), lambda j: (0, j))],
        out_specs=pl.BlockSpec((T, 128), lambda j: (0, j)),
        out_shape=jax.ShapeDtypeStruct((T, Dc), BF),
        compiler_params=_params(vmem=VMEM_BIG, sem=("arbitrary",)),
    )(proj, proj, proj, conv_w)


def conv_backward(dy, proj, conv_w, offs, Tc, name):
    Ta = proj.shape[0]
    T = Ta - Tc
    Dc = conv_w.shape[1]
    nb = Dc // 128
    bo, co, vo = offs["bg"] // 128, offs["cg"] // 128, offs["vc"] // 128

    def body(dy_ref, b_ref, c_ref, v_ref, w_ref, db_ref, dc_ref, dv_ref, dw_ref):
        rows = lax.broadcasted_iota(jnp.int32, (T, 128), 0)
        first, last = rows == 0, rows == T - 1
        cg = c_ref[pl.ds(Tc, T), :].astype(F32)
        vc = v_ref[pl.ds(Tc, T), :].astype(F32)
        bg = b_ref[pl.ds(Tc, T), :].astype(F32)
        dy = dy_ref[...].astype(F32)
        u = cg * vc
        prev, nxt = _shifted(u, first, last)
        w = w_ref[...]
        cv = prev * w[0:1] + u * w[1:2] + nxt * w[2:3]
        db_ref[...] = (dy * cv).astype(BF)
        dcv = dy * bg
        dprev, dnxt = _shifted(dcv, first, last)
        du = dnxt * w[0:1] + dcv * w[1:2] + dprev * w[2:3]
        dc_ref[...] = (du * vc).astype(BF)
        dv_ref[...] = (du * cg).astype(BF)
        dw_ref[...] = jnp.zeros_like(dw_ref)
        for k, tap in enumerate((prev, u, nxt)):
            dw_ref[k:k + 1, :] = jnp.sum(dcv * tap, axis=0, keepdims=True)

    col = lambda o: pl.BlockSpec((Ta, 128), lambda j: (0, o + j))
    lat = pl.BlockSpec((T, 128), lambda j: (0, j))
    return pl.pallas_call(
        body, name=name, grid=(nb,),
        in_specs=[lat, col(bo), col(co), col(vo), pl.BlockSpec((CONV_TAPS, 128), lambda j: (0, j))],
        out_specs=[lat, lat, lat, pl.BlockSpec((8, 128), lambda j: (0, j))],
        out_shape=[jax.ShapeDtypeStruct((T, Dc), BF)] * 3 + [jax.ShapeDtypeStruct((8, Dc), F32)],
        compiler_params=_params(vmem=VMEM_BIG, sem=("arbitrary",)),
    )(dy, proj, proj, proj, conv_w)


def _kv_chunk(Ta):
    return _pick(Ta, (768, 512, 384, 256, 128))


def _stack_heads(v):
    return jnp.concatenate([v[:, h * HEAD_DIM:(h + 1) * HEAD_DIM] for h in range(GROUP)], axis=0)


def attention_forward(q, k, proj, offs, Tc, name):
    Ta = k.shape[0]
    T = Ta - Tc
    tq = ROW_TILE
    kc = _kv_chunk(Ta)
    gw = GROUP * HEAD_DIM
    vblk = offs["v"] // HEAD_DIM
    qoff = Tc // tq

    def body(q_ref, k_ref, v_ref, o_ref, lse_ref):
        qs = _stack_heads(q_ref[...])
        n = GROUP * tq

        def step(c, carry):
            m, l, acc = carry
            kk = k_ref[pl.ds(pl.multiple_of(c * kc, kc), kc), :]
            vv = v_ref[pl.ds(pl.multiple_of(c * kc, kc), kc), :]
            s = lax.dot_general(qs, kk, NT, preferred_element_type=F32)
            m_new = jnp.maximum(m, jnp.max(s, axis=-1, keepdims=True))
            alpha = jnp.exp(m - m_new)
            p = jnp.exp(s - m_new)
            l = alpha * l + jnp.sum(p, axis=-1, keepdims=True)
            acc = alpha * acc + jnp.dot(p.astype(BF), vv, preferred_element_type=F32)
            return m_new, l, acc

        m0 = jnp.full((n, 1), -1e30, F32)
        m, l, acc = lax.fori_loop(0, Ta // kc, step, (m0, jnp.zeros((n, 1), F32), jnp.zeros((n, HEAD_DIM), F32)))
        o = acc / l
        lse = m + jnp.log(l)
        for h in range(GROUP):
            o_ref[:, h * HEAD_DIM:(h + 1) * HEAD_DIM] = o[h * tq:(h + 1) * tq].astype(BF)
            lse_ref[0, :, h:h + 1] = lse[h * tq:(h + 1) * tq]

    return pl.pallas_call(
        body, name=name, grid=(N_KV_HEADS, T // tq),
        in_specs=[pl.BlockSpec((tq, gw), lambda j, i: (i + qoff, j)),
                  pl.BlockSpec((Ta, HEAD_DIM), lambda j, i: (0, j)),
                  pl.BlockSpec((Ta, HEAD_DIM), lambda j, i: (0, vblk + j))],
        out_specs=[pl.BlockSpec((tq, gw), lambda j, i: (i, j)),
                   pl.BlockSpec((1, tq, GROUP), lambda j, i: (j, i, 0))],
        out_shape=[jax.ShapeDtypeStruct((T, N_Q_HEADS * HEAD_DIM), BF),
                   jax.ShapeDtypeStruct((N_KV_HEADS, T, GROUP), F32)],
        compiler_params=_params(vmem=VMEM_BIG, sem=("arbitrary", "arbitrary")),
    )(q, k, proj)


def attention_backward(q, k, proj, o, lse, do, offs, Tc, name):
    Ta = k.shape[0]
    T = Ta - Tc
    tq = ROW_TILE
    kc = _kv_chunk(Ta)
    gw = GROUP * HEAD_DIM
    vblk = offs["v"] // HEAD_DIM
    qoff = Tc // tq

    def body(q_ref, k_ref, v_ref, o_ref, lse_ref, do_ref, dq_ref, dk_ref, dv_ref):
        @pl.when(pl.program_id(1) == 0)
        def _():
            dk_ref[...] = jnp.zeros_like(dk_ref)
            dv_ref[...] = jnp.zeros_like(dv_ref)

        qs = _stack_heads(q_ref[...])
        dob = do_ref[...]
        dos = _stack_heads(dob)
        delta = jnp.concatenate(
            [jnp.sum(dob[:, h * HEAD_DIM:(h + 1) * HEAD_DIM].astype(F32)
                     * o_ref[:, h * HEAD_DIM:(h + 1) * HEAD_DIM].astype(F32), axis=-1, keepdims=True)
             for h in range(GROUP)], axis=0)
        lse = jnp.concatenate([lse_ref[0, :, h:h + 1] for h in range(GROUP)], axis=0)

        def step(c, dq):
            rows = pl.ds(pl.multiple_of(c * kc, kc), kc)
            kk = k_ref[rows, :]
            vv = v_ref[rows, :]
            s = lax.dot_general(qs, kk, NT, preferred_element_type=F32)
            p = jnp.exp(s - lse)
            dp = lax.dot_general(dos, vv, NT, preferred_element_type=F32)
            ds = (p * (dp - delta)).astype(BF)
            dv_ref[rows, :] += lax.dot_general(p.astype(BF), dos, TN, preferred_element_type=F32)
            dk_ref[rows, :] += lax.dot_general(ds, qs, TN, preferred_element_type=F32)
            return dq + jnp.dot(ds, kk, preferred_element_type=F32)

        dq = lax.fori_loop(0, Ta // kc, step, jnp.zeros((GROUP * tq, HEAD_DIM), F32))
        for h in range(GROUP):
            dq_ref[:, h * HEAD_DIM:(h + 1) * HEAD_DIM] = dq[h * tq:(h + 1) * tq]

    return pl.pallas_call(
        body, name=name, grid=(N_KV_HEADS, T // tq),
        in_specs=[pl.BlockSpec((tq, gw), lambda j, i: (i + qoff, j)),
                  pl.BlockSpec((Ta, HEAD_DIM), lambda j, i: (0, j)),
                  pl.BlockSpec((Ta, HEAD_DIM), lambda j, i: (0, vblk + j)),
                  pl.BlockSpec((tq, gw), lambda j, i: (i, j)),
                  pl.BlockSpec((1, tq, GROUP), lambda j, i: (j, i, 0)),
                  pl.BlockSpec((tq, gw), lambda j, i: (i, j))],
        out_specs=[pl.BlockSpec((tq, gw), lambda j, i: (i, j)),
                   pl.BlockSpec((Ta, HEAD_DIM), lambda j, i: (0, j)),
                   pl.BlockSpec((Ta, HEAD_DIM), lambda j, i: (0, j))],
        out_shape=[jax.ShapeDtypeStruct((T, N_Q_HEADS * HEAD_DIM), F32),
                   jax.ShapeDtypeStruct((Ta, N_KV_HEADS * HEAD_DIM), F32),
                   jax.ShapeDtypeStruct((Ta, N_KV_HEADS * HEAD_DIM), F32)],
        compiler_params=_params(vmem=VMEM_BIG, sem=("arbitrary", "arbitrary")),
    )(q, k, proj, o, lse, do)


def qk_norm_backward(dz, proj, gain, tabs, col_off, nheads, row_off_blocks, zscale, name):
    Tr = dz.shape[0]
    tm = ROW_TILE
    w = nheads * HEAD_DIM
    cb = col_off // w

    def body(dz_ref, raw_ref, g_ref, tab_ref, dr_ref, dg_ref):
        @pl.when(pl.program_id(0) == 0)
        def _():
            dg_ref[...] = jnp.zeros_like(dg_ref)

        cos, s_next, s_prev = tab_ref[0], tab_ref[1], tab_ref[2]
        gg = g_ref[...]
        dg = jnp.zeros((1, HEAD_DIM), F32)
        for h in range(nheads):
            sl = slice(h * HEAD_DIM, (h + 1) * HEAD_DIM)
            v = raw_ref[:, sl].astype(F32)
            r = lax.rsqrt(jnp.mean(v * v, axis=-1, keepdims=True) + EPS)
            n = v * r
            dy = _rope_transpose(dz_ref[:, sl] * zscale, cos, s_next, s_prev)
            dg = dg + jnp.sum(dy * n, axis=0, keepdims=True)
            dn = dy * gg
            dr_ref[:, sl] = (r * (dn - n * jnp.mean(dn * n, axis=-1, keepdims=True))).astype(BF)
        dg_ref[0:1, :] += dg

    return pl.pallas_call(
        body, name=name, grid=(Tr // tm,),
        in_specs=[pl.BlockSpec((tm, w), lambda i: (i, 0)),
                  pl.BlockSpec((tm, w), lambda i: (i + row_off_blocks, cb)),
                  _resident((1, HEAD_DIM)),
                  pl.BlockSpec((3, tm, HEAD_DIM), lambda i: (0, i + row_off_blocks, 0))],
        out_specs=[pl.BlockSpec((tm, w), lambda i: (i, 0)), pl.BlockSpec((8, HEAD_DIM), lambda i: (0, 0))],
        out_shape=[jax.ShapeDtypeStruct((Tr, w), BF), jax.ShapeDtypeStruct((8, HEAD_DIM), F32)],
        compiler_params=_params(sem=("arbitrary",)),
    )(dz, proj, gain, tabs)


def _gate_specs(gt_off, D, tm, roff):
    wid = D
    while gt_off % wid:
        wid //= 2
    npc = D // wid
    return [pl.BlockSpec((tm, wid), functools.partial(lambda i, b: (i + roff, b), b=gt_off // wid + p))
            for p in range(2 * npc)], npc


def _gate_logits(refs, npc):
    cat = lambda rs: rs[0][...].astype(F32) if len(rs) == 1 else jnp.concatenate([r[...].astype(F32) for r in rs], axis=1)
    return cat(refs[:npc]), cat(refs[npc:])


def merge_forward(x1, mod6, yc, o, proj, w_bc, w_ba, w_o, offs, Tc, name):
    T, D = yc.shape[0], x1.shape[1]
    tm = ROW_TILE
    roff = Tc // tm
    gspecs, npc = _gate_specs(offs["gt"], D, tm, roff)

    def body(x_ref, mod_ref, yc_ref, o_ref, *rest):
        gate_refs, (wbc_ref, wba_ref, wo_ref, xo_ref, pc_ref, pa_ref, m_ref, z_ref) = rest[:2 * npc], rest[2 * npc:]
        gc, ga = _gate_logits(gate_refs, npc)
        gate = mod_ref[0][2:3]
        pc = jnp.dot(yc_ref[...], wbc_ref[...], preferred_element_type=F32)
        pa = jnp.dot(o_ref[...], wba_ref[...], preferred_element_type=F32)
        pc_ref[...] = pc.astype(BF)
        pa_ref[...] = pa.astype(BF)
        mb = (_sigmoid(gc) * pc + _sigmoid(ga) * pa).astype(BF)
        m_ref[...] = mb
        z = jnp.dot(mb, wo_ref[...], preferred_element_type=F32)
        z_ref[...] = z.astype(BF)
        xo_ref[...] = x_ref[...] + gate * z

    row = lambda n: pl.BlockSpec((tm, n), lambda i: (i, 0))
    return pl.pallas_call(
        body, name=name, grid=(T // tm,),
        in_specs=[pl.BlockSpec((tm, D), lambda i: (i + roff, 0)), _mod_spec(D, 1, 0), row(yc.shape[1]), row(o.shape[1]),
                  *gspecs, _resident(w_bc.shape), _resident(w_ba.shape), _resident(w_o.shape)],
        out_specs=[row(D)] * 5,
        out_shape=[jax.ShapeDtypeStruct((T, D), F32)] + [jax.ShapeDtypeStruct((T, D), BF)] * 4,
        compiler_params=_params(vmem=VMEM_BIG, sem=("arbitrary",)),
    )(x1, mod6, yc, o, *([proj] * (2 * npc)), w_bc, w_ba, w_o)


def merge_backward_rows(dx2, mod6, z, pc, pa, proj, w_bc, w_ba, w_o, offs, Tc, name):
    T, D = dx2.shape
    tm = ROW_TILE
    roff = Tc // tm
    gspecs, npc = _gate_specs(offs["gt"], D, tm, roff)
    dcw, dqw = w_bc.shape[0], w_ba.shape[0]

    def body(dx_ref, mod_ref, z_ref, pc_ref, pa_ref, *rest):
        gate_refs = rest[:2 * npc]
        wbc_ref, wba_ref, wo_ref, dg_ref, dpc_ref, dpa_ref, dyc_ref, do_ref, dgt_ref, acc_ref = rest[2 * npc:]
        gc, ga = _gate_logits(gate_refs, npc)

        @pl.when(pl.program_id(0) == 0)
        def _():
            acc_ref[...] = jnp.zeros_like(acc_ref)

        gate = mod_ref[0][2:3]
        dx = dx_ref[...]
        acc_ref[0:1, :] += jnp.sum(dx * z_ref[...].astype(F32), axis=0, keepdims=True)
        dgb = (dx * gate).astype(BF)
        dg_ref[...] = dgb
        dm = lax.dot_general(dgb, wo_ref[...], NT, preferred_element_type=F32)
        sc = _sigmoid(gc)
        sa = _sigmoid(ga)
        pc = pc_ref[...].astype(F32)
        pa = pa_ref[...].astype(F32)
        dpc = (dm * sc).astype(BF)
        dpa = (dm * sa).astype(BF)
        dpc_ref[...] = dpc
        dpa_ref[...] = dpa
        dgt_ref[:, 0:D] = ((dm * pc) * (sc * (1.0 - sc))).astype(BF)
        dgt_ref[:, D:2 * D] = ((dm * pa) * (sa * (1.0 - sa))).astype(BF)
        dyc_ref[...] = lax.dot_general(dpc, wbc_ref[...], NT, preferred_element_type=F32).astype(BF)
        do_ref[...] = lax.dot_general(dpa, wba_ref[...], NT, preferred_element_type=F32).astype(BF)

    row = lambda n: pl.BlockSpec((tm, n), lambda i: (i, 0))
    return pl.pallas_call(
        body, name=name, grid=(T // tm,),
        in_specs=[row(D), _mod_spec(D, 1, 0), row(D), row(D), row(D), *gspecs,
                  _resident(w_bc.shape), _resident(w_ba.shape), _resident(w_o.shape)],
        out_specs=[row(D), row(D), row(D), row(dcw), row(dqw), row(2 * D), pl.BlockSpec((8, D), lambda i: (0, 0))],
        out_shape=[jax.ShapeDtypeStruct((T, D), BF)] * 3 + [jax.ShapeDtypeStruct((T, dcw), BF), jax.ShapeDtypeStruct((T, dqw), BF),
                   jax.ShapeDtypeStruct((T, 2 * D), BF), jax.ShapeDtypeStruct((8, D), F32)],
        compiler_params=_params(vmem=VMEM_BIG, sem=("arbitrary",)),
    )(dx2, mod6, z, pc, pa, *([proj] * (2 * npc)), w_bc, w_ba, w_o)


def proj_backward_rows(dproj, dres, xa, mod6, g, w_in, nctx, name):
    Tr, D = xa.shape
    P = w_in.shape[1]
    tm = ROW_TILE

    def body(dp_ref, dres_ref, x_ref, mod_ref, g_ref, w_ref, dx_ref, acc_ref):
        i = pl.program_id(0)

        @pl.when((i == 0) | (i == nctx))
        def _():
            acc_ref[...] = jnp.zeros_like(acc_ref)

        x = x_ref[...]
        scale = mod_ref[0][1:2]
        gg = g_ref[...]
        dhm = lax.dot_general(dp_ref[...], w_ref[...], NT, preferred_element_type=F32)
        r = lax.rsqrt(jnp.mean(x * x, axis=-1, keepdims=True) + EPS)
        xh = x * r
        dshift = jnp.sum(dhm, axis=0, keepdims=True)
        dscale = jnp.sum(dhm * (xh * gg), axis=0, keepdims=True)
        dxh_g = dhm * (1.0 + scale)
        dg = jnp.sum(dxh_g * xh, axis=0, keepdims=True)
        dxh = dxh_g * gg
        res = jnp.where(i < nctx, 0.0, dres_ref[...])
        dx_ref[...] = res + r * (dxh - xh * jnp.mean(dxh * xh, axis=-1, keepdims=True))
        for k, val in enumerate((dshift, dscale, dg)):
            acc_ref[0, k:k + 1, :] += val

    row = lambda n: pl.BlockSpec((tm, n), lambda i: (i, 0))
    return pl.pallas_call(
        body, name=name, grid=(Tr // tm,),
        in_specs=[row(P), pl.BlockSpec((tm, D), lambda i: (jnp.maximum(i - nctx, 0), 0)), row(D),
                  _mod_spec(D, 1, nctx), _resident((1, D)), _resident(w_in.shape)],
        out_specs=[row(D), _acc_spec(D, nctx)],
        out_shape=[jax.ShapeDtypeStruct((Tr, D), F32), jax.ShapeDtypeStruct((2, 8, D), F32)],
        compiler_params=_params(vmem=VMEM_BIG, sem=("arbitrary",)),
    )(dproj, dres, xa, mod6, g, w_in)


def loss_and_final_norm_backward(x3, tgt, gf, name):
    T, D = x3.shape
    tm = ROW_TILE

    def body(x_ref, t_ref, g_ref, dx_ref, acc_ref):
        @pl.when(pl.program_id(0) == 0)
        def _():
            acc_ref[...] = jnp.zeros_like(acc_ref)

        x = x_ref[...]
        gg = g_ref[...]
        r = lax.rsqrt(jnp.mean(x * x, axis=-1, keepdims=True) + EPS)
        xh = x * r
        e = xh * gg - t_ref[...]
        part = 0.5 * jnp.sum(jnp.mean(e * e, axis=-1, keepdims=True), axis=0, keepdims=True)
        dy = e * (1.0 / D)
        dyg = dy * gg
        dx_ref[...] = r * (dyg - xh * jnp.mean(dyg * xh, axis=-1, keepdims=True))
        acc_ref[0:1, :] += jnp.sum(dy * xh, axis=0, keepdims=True)
        acc_ref[1:2, :] += jnp.broadcast_to(part, (1, D))

    row = pl.BlockSpec((tm, D), lambda i: (i, 0))
    return pl.pallas_call(
        body, name=name, grid=(T // tm,),
        in_specs=[row, row, _resident((1, D))],
        out_specs=[row, pl.BlockSpec((8, D), lambda i: (0, 0))],
        out_shape=[jax.ShapeDtypeStruct((T, D), F32), jax.ShapeDtypeStruct((8, D), F32)],
        compiler_params=_params(sem=("arbitrary",)),
    )(x3, tgt, gf)


def adamw(w, g, m, v, name):
    R, C = w.shape
    tr = _pick(R, (256, 128, 64, 32, 16, 8)) if R * C > (1 << 18) else R
    c1 = 1.0 - ADAM_B1 ** ADAM_STEP
    c2 = 1.0 - ADAM_B2 ** ADAM_STEP

    def body(w_ref, g_ref, m_ref, v_ref, d_ref, nm_ref, nv_ref):
        gg = g_ref[...]
        m = ADAM_B1 * m_ref[...] + (1.0 - ADAM_B1) * gg
        v = ADAM_B2 * v_ref[...] + (1.0 - ADAM_B2) * (gg * gg)
        nm_ref[...] = m
        nv_ref[...] = v
        d_ref[...] = -ADAM_LR * ((m / c1) / (jnp.sqrt(v / c2) + ADAM_EPS) + ADAM_WD * w_ref[...])

    blk = pl.BlockSpec((tr, C), lambda i: (i, 0))
    return pl.pallas_call(
        body, name=name, grid=(R // tr,),
        in_specs=[blk] * 4, out_specs=[blk] * 3,
        out_shape=[jax.ShapeDtypeStruct((R, C), F32)] * 3,
        compiler_params=_params(sem=("parallel",)),
    )(w, g, m, v)


def _rope_tables(T, Tc):
    rows = T // GRID_W
    row = jnp.repeat(jnp.arange(rows), GRID_W).astype(F32)
    col = jnp.tile(jnp.arange(GRID_W), rows).astype(F32)
    n_freq = HEAD_DIM // 4
    inv = ROPE_THETA ** (-jnp.arange(n_freq, dtype=F32) / n_freq)
    cr, sr = jnp.cos(row[:, None] * inv), jnp.sin(row[:, None] * inv)
    cc, sc = jnp.cos(col[:, None] * inv), jnp.sin(col[:, None] * inv)
    zero = jnp.zeros_like(sr)
    cos = jnp.concatenate([cr, cr, cc, cc], axis=1)
    s_next = jnp.concatenate([-sr, zero, -sc, zero], axis=1)
    s_prev = jnp.concatenate([zero, sr, zero, sc], axis=1)
    lat = jnp.stack([cos, s_next, s_prev])
    ctx = jnp.stack([jnp.ones((Tc, HEAD_DIM), F32), jnp.zeros((Tc, HEAD_DIM), F32), jnp.zeros((Tc, HEAD_DIM), F32)])
    return jnp.concatenate([ctx, lat], axis=1)


BIG = ("ffn1_w_in", "ffn1_w_out", "w_in", "w_branch_conv", "w_branch_attn", "w_out", "ffn2_w_in", "ffn2_w_out")
COL_SHARDED = ("ffn1_w_in", "w_in", "ffn2_w_in")


def _pack_rows(shards, D):
    return jnp.concatenate([shards[n].reshape(-1, D) for n in BIG], axis=0)


def _unpack_full(gathered, shapes, D):
    out, r0 = {}, 0
    for n in BIG:
        K, nn = shapes[n]
        rows = K * nn // D
        blk = gathered[:, r0:r0 + rows, :]
        if n in COL_SHARDED:
            out[n] = blk.reshape(N_DEV, K, nn).transpose(1, 0, 2).reshape(K, N_DEV * nn)
        else:
            out[n] = blk.reshape(N_DEV * K, nn)
        r0 += rows
    return out


def _pack_grads(grads, shapes, D):
    parts = []
    for n in BIG:
        K, nn = shapes[n]
        g = grads[n]
        if n in COL_SHARDED:
            g = g.reshape(K, N_DEV, nn).transpose(1, 0, 2)
        parts.append(g.reshape(N_DEV, K * nn // D, D).astype(BF))
    return jnp.concatenate(parts, axis=1)


def kernel(x, c, ctx, c_ctx, w_mod, b_mod, norm1_g, norm2_g, norm3_g, ffn1_w_in, ffn1_w_out, w_in, conv_w, q_norm_g, k_norm_g, w_branch_conv, w_branch_attn, w_out, ffn2_w_in, ffn2_w_out, final_g, loss_target, m_c_ctx, m_w_mod, m_b_mod, m_norm1_g, m_norm2_g, m_norm3_g, m_ffn1_w_in, m_ffn1_w_out, m_w_in, m_conv_w, m_q_norm_g, m_k_norm_g, m_w_branch_conv, m_w_branch_attn, m_w_out, m_ffn2_w_in, m_ffn2_w_out, m_final_g, v_c_ctx, v_w_mod, v_b_mod, v_norm1_g, v_norm2_g, v_norm3_g, v_ffn1_w_in, v_ffn1_w_out, v_w_in, v_conv_w, v_q_norm_g, v_k_norm_g, v_w_branch_conv, v_w_branch_attn, v_w_out, v_ffn2_w_in, v_ffn2_w_out, v_final_g):
    weights = dict(c_ctx=c_ctx, w_mod=w_mod, b_mod=b_mod, norm1_g=norm1_g, norm2_g=norm2_g, norm3_g=norm3_g,
                   ffn1_w_in=ffn1_w_in, ffn1_w_out=ffn1_w_out, w_in=w_in, conv_w=conv_w, q_norm_g=q_norm_g,
                   k_norm_g=k_norm_g, w_branch_conv=w_branch_conv, w_branch_attn=w_branch_attn, w_out=w_out,
                   ffn2_w_in=ffn2_w_in, ffn2_w_out=ffn2_w_out, final_g=final_g)
    moms = dict(c_ctx=(m_c_ctx, v_c_ctx), w_mod=(m_w_mod, v_w_mod), b_mod=(m_b_mod, v_b_mod),
                norm1_g=(m_norm1_g, v_norm1_g), norm2_g=(m_norm2_g, v_norm2_g), norm3_g=(m_norm3_g, v_norm3_g),
                ffn1_w_in=(m_ffn1_w_in, v_ffn1_w_in), ffn1_w_out=(m_ffn1_w_out, v_ffn1_w_out), w_in=(m_w_in, v_w_in),
                conv_w=(m_conv_w, v_conv_w), q_norm_g=(m_q_norm_g, v_q_norm_g), k_norm_g=(m_k_norm_g, v_k_norm_g),
                w_branch_conv=(m_w_branch_conv, v_w_branch_conv), w_branch_attn=(m_w_branch_attn, v_w_branch_attn),
                w_out=(m_w_out, v_w_out), ffn2_w_in=(m_ffn2_w_in, v_ffn2_w_in), ffn2_w_out=(m_ffn2_w_out, v_ffn2_w_out),
                final_g=(m_final_g, v_final_g))
    order = list(weights)

    T, D = x.shape[1], x.shape[2]
    Tc = ctx.shape[1]
    nctx = Tc // ROW_TILE
    nd = N_MOD * D
    Dc = conv_w.shape[2] * N_DEV
    qw, kw = N_Q_HEADS * HEAD_DIM, N_KV_HEADS * HEAD_DIM
    offs, o = {}, 0
    for nme, wd in (("bg", Dc), ("cg", Dc), ("vc", Dc), ("q", qw), ("k", kw), ("v", kw), ("gt", 2 * D)):
        offs[nme] = o
        o += wd

    ax, ay, ac = lax.axis_index("x"), lax.axis_index("y"), lax.axis_index("c")
    me = 4 * ax + 2 * ay + ac
    place = jnp.stack([ac, 2 * ax + ay]).astype(jnp.int32)

    shards = {n: weights[n][0] for n in BIG}
    shapes = {n: shards[n].shape for n in BIG}
    packed = _pack_rows({n: shards[n].astype(BF) for n in BIG}, D)
    full = _unpack_full(allgather_two_level(packed, "ag_weights"), shapes, D)

    mod_cols = w_mod.shape[2]
    cw_loc = conv_w[0]
    cpad = (-(D + CONV_TAPS * cw_loc.shape[1])) % 128
    pay = jnp.concatenate([c.reshape(1, D), cw_loc.reshape(1, -1), jnp.zeros((1, cpad), F32)], axis=1)
    call = allgather_direct(pay, "ag_cond")
    conv_full = call[:, 0, D:D + CONV_TAPS * cw_loc.shape[1]].reshape(N_DEV, CONV_TAPS, -1).transpose(1, 0, 2).reshape(CONV_TAPS, Dc)
    b_loc = lax.dynamic_slice_in_dim(b_mod, me * mod_cols, mod_cols, axis=1)
    cctx2 = c_ctx.reshape(1, D)
    mod_part = mod_forward(call, cctx2, w_mod[0], b_loc, "mod_fwd")
    mod_all = allgather_direct(mod_part, "ag_mod")
    mod_lat = lax.dynamic_index_in_dim(mod_all, me, axis=1, keepdims=False).reshape(nd)
    mod_ctx = mod_all[:, N_DEV, :].reshape(nd)
    mod6 = jnp.stack([mod_ctx, mod_lat]).reshape(6, 3, D)

    tabs = _rope_tables(T, Tc)

    xa = jnp.concatenate([ctx[0], x[0]], axis=0)
    xa1, hm1, ab1, h1, f1 = ffn_forward(xa, mod6, 0, norm1_g, full["ffn1_w_in"], full["ffn1_w_out"], nctx, "ffn1_fwd")
    hx, proj, qr, kr = proj_forward(xa1, mod6, norm2_g, full["w_in"], q_norm_g, k_norm_g, tabs, offs, nctx, "proj_fwd")
    yc = conv_forward(proj, conv_full, offs, Tc, "conv_fwd")
    oa, lse = attention_forward(qr, kr, proj, offs, Tc, "attn_fwd")
    x2, pc, pa, mm, zz = merge_forward(xa1, mod6, yc, oa, proj, full["w_branch_conv"], full["w_branch_attn"],
                                       full["w_out"], offs, Tc, "merge_fwd")
    x3, hm2, ab2, h2, f2 = ffn_forward(x2, mod6, 2, norm3_g, full["ffn2_w_in"], full["ffn2_w_out"], 0, "ffn2_fwd")
    dx3, lacc = loss_and_final_norm_backward(x3, loss_target[0], final_g.reshape(1, D), "loss_bwd")
    loss = lax.psum(lacc[1, 0], ("x", "y", "c"))

    grads = {}
    dx2, dab2, df2, acc_f2 = ffn_backward_rows(dx3, x2, mod6, 2, norm3_g, ab2, f2, full["ffn2_w_in"], full["ffn2_w_out"], 0, "ffn2_bwd")
    grads["ffn2_w_out"] = tn_matmul(h2, df2, "ffn2_dwout")
    grads["ffn2_w_in"] = tn_matmul(hm2, dab2, "ffn2_dwin")
    dgm, dpc, dpa, dyc, do, dgt, acc_mg = merge_backward_rows(dx2, mod6, zz, pc, pa, proj, full["w_branch_conv"],
                                                              full["w_branch_attn"], full["w_out"], offs, Tc, "merge_bwd")
    grads["w_out"] = tn_matmul(mm, dgm, "dw_out")
    grads["w_branch_conv"] = tn_matmul(yc, dpc, "dw_bc")
    grads["w_branch_attn"] = tn_matmul(oa, dpa, "dw_ba")
    dbg, dcg, dvc, dcw = conv_backward(dyc, proj, conv_full, offs, Tc, "conv_bwd")
    dq, dk, dv = attention_backward(qr, kr, proj, oa, lse, do, offs, Tc, "attn_bwd")
    dq_raw, dqg = qk_norm_backward(dq, proj, q_norm_g, tabs, offs["q"], N_Q_HEADS, nctx, HEAD_DIM ** -0.5, "q_norm_bwd")
    dk_raw, dkg = qk_norm_backward(dk, proj, k_norm_g, tabs, offs["k"], N_KV_HEADS, 0, 1.0, "k_norm_bwd")
    zc = lambda n: jnp.zeros((Tc, n), BF)
    dproj = jnp.concatenate([
        jnp.concatenate([zc(Dc), dbg], axis=0), jnp.concatenate([zc(Dc), dcg], axis=0),
        jnp.concatenate([zc(Dc), dvc], axis=0), jnp.concatenate([zc(qw), dq_raw], axis=0),
        dk_raw, dv.astype(BF), jnp.concatenate([zc(2 * D), dgt], axis=0)], axis=1)
    dxa1, acc_pj = proj_backward_rows(dproj, dx2, xa1, mod6, norm2_g, full["w_in"], nctx, "proj_bwd")
    grads["w_in"] = tn_matmul(hx, dproj, "dw_in")
    dxa, dab1, df1, acc_f1 = ffn_backward_rows(dxa1, xa, mod6, 0, norm1_g, ab1, f1, full["ffn1_w_in"], full["ffn1_w_out"], nctx, "ffn1_bwd")
    grads["ffn1_w_out"] = tn_matmul(h1, df1, "ffn1_dwout")
    grads["ffn1_w_in"] = tn_matmul(hm1, dab1, "ffn1_dwin")
    grad_x = dxa[Tc:][None]

    gs = _pack_grads(grads, shapes, D)
    recv_a = rs_sibling_exchange(gs, "rs_sibling")
    ps = rs_pair_sum(place, gs, recv_a, "rs_pair_sum")
    recv_b = rs_chip_exchange(ps, "rs_chips")
    gsum = rs_final_sum(place, ps, recv_b, "rs_final_sum")
    gout, r0 = {}, 0
    for n in BIG:
        K, nn = shapes[n]
        rows = K * nn // D
        gout[n] = gsum[r0:r0 + rows].reshape(K, nn)
        r0 += rows

    zero_d = jnp.zeros((D,), F32)
    dlat = jnp.concatenate([acc_f1[1, 0], acc_f1[1, 1], acc_f1[1, 2], acc_pj[1, 0], acc_pj[1, 1], acc_mg[0],
                            acc_f2[1, 0], acc_f2[1, 1], acc_f2[1, 2]])
    dctx = jnp.concatenate([acc_f1[0, 0], acc_f1[0, 1], acc_f1[0, 2], acc_pj[0, 0], acc_pj[0, 1]] + [zero_d] * 4)
    small = jnp.concatenate([acc_f1[0, 3] + acc_f1[1, 3], acc_pj[0, 2] + acc_pj[1, 2], acc_f2[1, 3],
                             dqg[0], dkg[0], lacc[0], dcw[0:CONV_TAPS].reshape(-1)])
    n_small = small.shape[0]
    pay_b = jnp.concatenate([dlat, dctx, small]).reshape(1, -1)
    gath = allgather_direct(pay_b, "ag_small_grads")
    dlat_loc = lax.dynamic_slice_in_dim(gath[:, 0, :nd], me * mod_cols, mod_cols, axis=1)
    dctx_loc = lax.dynamic_slice_in_dim(gath[:, 0, nd:2 * nd], me * mod_cols, mod_cols, axis=1)
    g_wmod, pc_part, small_sum, _ = mod_backward(call, cctx2, w_mod[0], dlat_loc, dctx_loc, gath, 2 * nd, n_small, "mod_bwd")
    pcs = allgather_direct(pc_part, "ag_cctx")
    g_bmod, g_cctx = bmod_and_cctx_grad(gath, pcs, cctx2, nd, "small_bwd")
    sm = small_sum[0]
    g_conv_full = sm[3 * D + 2 * HEAD_DIM + D:].reshape(CONV_TAPS, Dc)
    g_conv = lax.dynamic_slice_in_dim(g_conv_full, me * cw_loc.shape[1], cw_loc.shape[1], axis=1)
    gout.update(
        c_ctx=g_cctx, w_mod=g_wmod, b_mod=g_bmod, norm1_g=sm[0:D][None], norm2_g=sm[D:2 * D][None],
        norm3_g=sm[2 * D:3 * D][None], q_norm_g=sm[3 * D:3 * D + HEAD_DIM][None],
        k_norm_g=sm[3 * D + HEAD_DIM:3 * D + 2 * HEAD_DIM][None],
        final_g=sm[3 * D + 2 * HEAD_DIM:3 * D + 2 * HEAD_DIM + D][None], conv_w=g_conv)

    g_out, d_out, m_out, v_out = [], [], [], []
    for n in order:
        w = weights[n]
        shp = w.shape
        two_d = (lambda a: a.reshape(-1, shp[-1]))
        m, v = moms[n]
        g2 = gout[n].reshape(two_d(w).shape)
        d, nm, nv = adamw(two_d(w), g2, two_d(m), two_d(v), "adamw_" + n)
        g_out.append(g2.reshape(shp))
        d_out.append(d.reshape(shp))
        m_out.append(nm.reshape(shp))
        v_out.append(nv.reshape(shp))
    return (loss, grad_x, *g_out, *d_out, *m_out, *v_out)
```

```python
import math

import jax
import jax.numpy as jnp
from jax import lax
from jax.experimental import pallas as pl
from jax.experimental.pallas import tpu as pltpu

F32 = jnp.float32
BF = jnp.bfloat16
EPS = 1e-6
N_DEV = 8
HEAD_DIM = 128
N_Q_HEADS = 8
N_KV_HEADS = 2
GROUP = N_Q_HEADS // N_KV_HEADS
GRID_W = 64
ROPE_THETA = 10000.0
CONV_TAPS = 3
N_MOD = 9
ADAM_LR = 0.001
ADAM_B1 = 0.9
ADAM_B2 = 0.999
ADAM_EPS = 1e-08
ADAM_WD = 0.01
ADAM_STEP = 10
ROW_TILE = 256
VMEM_BIG = 56 << 20
MESH_ID = pl.DeviceIdType.MESH
HIGHEST = lax.Precision.HIGHEST
NT = (((1,), (1,)), ((), ()))
TN = (((0,), (0,)), ((), ()))
LOG2E = math.log2(math.e)


def _pick(n, cands):
    for c in cands:
        if n % c == 0:
            return c
    return n


def _params(vmem=None, sem=None):
    kw = {}
    if vmem is not None:
        kw["vmem_limit_bytes"] = vmem
    if sem is not None:
        kw["dimension_semantics"] = sem
    return pltpu.CompilerParams(**kw)


def _resident(shape):
    nd = len(shape)
    return pl.BlockSpec(shape, lambda *_: (0,) * nd, pipeline_mode=pl.Buffered(1))


def _sigmoid(x):
    return jax.nn.sigmoid(x)


ANY = pl.BlockSpec(memory_space=pl.ANY)


def _coords():
    return lax.axis_index("x"), lax.axis_index("y"), lax.axis_index("c")


def _flip(v, bit):
    return 1 - v if bit else v


def _remote(src, dst, ssem, rsem, dev):
    return pltpu.make_async_remote_copy(src_ref=src, dst_ref=dst, send_sem=ssem, recv_sem=rsem,
                                        device_id=dev, device_id_type=MESH_ID)


def allgather_direct(v, name):
    def body(v_ref, out_ref, ssem, rsem, lsem):
        x, y, c = _coords()
        me = 4 * x + 2 * y + c
        mine = pltpu.make_async_copy(v_ref, out_ref.at[me], lsem)
        mine.start()
        cps = []
        for p in range(1, N_DEV):
            px, py, pc = (p >> 2) & 1, (p >> 1) & 1, p & 1
            cps.append(_remote(v_ref, out_ref.at[me], ssem.at[p - 1], rsem.at[p - 1],
                               (_flip(x, px), _flip(y, py), _flip(c, pc))))
        for cp in cps:
            cp.start()
        for p in range(1, N_DEV):
            px, py, pc = (p >> 2) & 1, (p >> 1) & 1, p & 1
            src = 4 * _flip(x, px) + 2 * _flip(y, py) + _flip(c, pc)
            _remote(v_ref, out_ref.at[src], ssem.at[p - 1], rsem.at[p - 1], (x, y, c)).wait_recv()
        for cp in cps:
            cp.wait_send()
        mine.wait()

    return pl.pallas_call(
        body, name=name,
        out_shape=jax.ShapeDtypeStruct((N_DEV,) + v.shape, v.dtype),
        in_specs=[ANY], out_specs=ANY,
        scratch_shapes=[pltpu.SemaphoreType.DMA((N_DEV - 1,)), pltpu.SemaphoreType.DMA((N_DEV - 1,)),
                        pltpu.SemaphoreType.DMA],
    )(v)


def allgather_two_level(shards, name):
    n = len(shards)

    def body(*refs):
        v_refs, out_refs, (ssem, rsem, lsem) = refs[:n], refs[n:2 * n], refs[2 * n:]
        x, y, c = _coords()
        me = (x, y, c)
        sib = (x, y, 1 - c)
        chips = [(1 - x, y), (x, 1 - y), (1 - x, 1 - y)]

        def slot(w, px, py, pc):
            return out_refs[w].at[4 * px + 2 * py + pc]

        def sem(w, k):
            return ssem.at[7 * w + k], rsem.at[7 * w + k]

        mine = [pltpu.make_async_copy(v_refs[w], slot(w, *me), lsem.at[w]) for w in range(n)]
        for cp in mine:
            cp.start()
        first = []
        for w in range(n):
            first.append(_remote(v_refs[w], slot(w, *me), *sem(w, 0), sib))
            first += [_remote(v_refs[w], slot(w, *me), *sem(w, 1 + j), (*chip, c)) for j, chip in enumerate(chips)]
        for cp in first:
            cp.start()
        passed = []
        for w in range(n):
            for j, chip in enumerate(chips):
                _remote(v_refs[w], slot(w, *chip, c), *sem(w, 1 + j), me).wait_recv()
                cp = _remote(slot(w, *chip, c), slot(w, *chip, c), *sem(w, 4 + j), sib)
                cp.start()
                passed.append(cp)
        for w in range(n):
            _remote(v_refs[w], slot(w, x, y, 1 - c), *sem(w, 0), me).wait_recv()
            for j, chip in enumerate(chips):
                _remote(v_refs[w], slot(w, *chip, 1 - c), *sem(w, 4 + j), me).wait_recv()
        for cp in first + passed:
            cp.wait_send()
        for cp in mine:
            cp.wait()

    return pl.pallas_call(
        body, name=name,
        out_shape=[jax.ShapeDtypeStruct((N_DEV,) + s.shape, s.dtype) for s in shards],
        in_specs=[ANY] * n, out_specs=[ANY] * n,
        scratch_shapes=[pltpu.SemaphoreType.DMA((7 * n,)), pltpu.SemaphoreType.DMA((7 * n,)),
                        pltpu.SemaphoreType.DMA((n,))],
    )(*shards)


def rs_sibling_exchange(gs, name):
    n = len(gs)

    def body(*refs):
        g_refs, out_refs, (ssem, rsem) = refs[:n], refs[n:2 * n], refs[2 * n:]
        x, y, c = _coords()
        sib = (x, y, 1 - c)
        cps = [_remote(g_refs[w].at[2 * k + (1 - c)], out_refs[w].at[k], ssem.at[4 * w + k], rsem.at[4 * w + k], sib)
               for w in range(n) for k in range(4)]
        for cp in cps:
            cp.start()
        for cp in cps:
            cp.wait()

    return pl.pallas_call(
        body, name=name,
        out_shape=[jax.ShapeDtypeStruct((4,) + g.shape[1:], g.dtype) for g in gs],
        in_specs=[ANY] * n, out_specs=[ANY] * n,
        scratch_shapes=[pltpu.SemaphoreType.DMA((4 * n,)), pltpu.SemaphoreType.DMA((4 * n,))],
    )(*gs)


def rs_chip_exchange(ps, name):
    n = len(ps)

    def body(*refs):
        p_refs, out_refs, (ssem, rsem) = refs[:n], refs[n:2 * n], refs[2 * n:]
        x, y, c = _coords()
        chips = [(1 - x, y), (x, 1 - y), (1 - x, 1 - y)]
        cps = [_remote(p_refs[w].at[2 * cx + cy], out_refs[w].at[j], ssem.at[3 * w + j], rsem.at[3 * w + j], (cx, cy, c))
               for w in range(n) for j, (cx, cy) in enumerate(chips)]
        for cp in cps:
            cp.start()
        for cp in cps:
            cp.wait()

    return pl.pallas_call(
        body, name=name,
        out_shape=[jax.ShapeDtypeStruct((3,) + p.shape[1:], p.dtype) for p in ps],
        in_specs=[ANY] * n, out_specs=[ANY] * n,
        scratch_shapes=[pltpu.SemaphoreType.DMA((3 * n,)), pltpu.SemaphoreType.DMA((3 * n,))],
    )(*ps)


def _row_tile(R, C):
    return _pick(R, (256, 128, 64, 32, 16)) if R * C > (1 << 18) else R


def rs_pair_sum(place, gs, recv_a, name):
    _, R, C = gs.shape
    tr = _row_tile(R, C)

    def body(place_ref, g_ref, a_ref, o_ref):
        o_ref[0] = (g_ref[0].astype(F32) + a_ref[0].astype(F32)).astype(o_ref.dtype)

    return pl.pallas_call(
        body, name=name,
        grid_spec=pltpu.PrefetchScalarGridSpec(
            num_scalar_prefetch=1, grid=(4, R // tr),
            in_specs=[pl.BlockSpec((1, tr, C), lambda k, r, pr: (2 * k + pr[0], r, 0)),
                      pl.BlockSpec((1, tr, C), lambda k, r, pr: (k, r, 0))],
            out_specs=pl.BlockSpec((1, tr, C), lambda k, r, pr: (k, r, 0))),
        out_shape=jax.ShapeDtypeStruct((4, R, C), BF),
    )(place, gs, recv_a)


def _cond_rows(call_ref, cctx_ref, z_ref, D):
    z_ref[...] = jnp.zeros_like(z_ref)
    for a in range(N_DEV):
        z_ref[a:a + 1, :] = call_ref[a][:, :D]
    z_ref[N_DEV:N_DEV + 1, :] = cctx_ref[...]


def mod_forward(call, c_ctx, w_loc, b_loc, name):
    D, cols = w_loc.shape

    def body(call_ref, cctx_ref, w_ref, b_ref, o_ref, z_ref):
        _cond_rows(call_ref, cctx_ref, z_ref, D)
        z = z_ref[...]
        s = z * _sigmoid(z)
        o_ref[...] = jnp.dot(s, w_ref[...], preferred_element_type=F32, precision=HIGHEST) + b_ref[...]

    return pl.pallas_call(
        body, name=name, out_shape=jax.ShapeDtypeStruct((16, cols), F32),
        scratch_shapes=[pltpu.VMEM((16, D), F32)],
        compiler_params=_params(vmem=VMEM_BIG),
    )(call, c_ctx, w_loc, b_loc)


def mod_backward(call, c_ctx, w_loc, dlat_loc, dctx_loc, gath, n_small_off, n_small, name):
    D, cols = w_loc.shape

    def body(call_ref, cctx_ref, w_ref, dlat_ref, dctx_ref, g_ref, gw_ref, pc_ref, small_ref, z_ref, dm_ref):
        _cond_rows(call_ref, cctx_ref, z_ref, D)
        z = z_ref[...]
        s = z * _sigmoid(z)
        dctx = dctx_ref[0:1, :]
        for a in range(1, N_DEV):
            dctx = dctx + dctx_ref[a:a + 1, :]
        dm_ref[...] = jnp.zeros_like(dm_ref)
        dm_ref[0:N_DEV, :] = dlat_ref[...]
        dm_ref[N_DEV:N_DEV + 1, :] = dctx
        gw_ref[...] = lax.dot_general(s, dm_ref[...], TN, preferred_element_type=F32, precision=HIGHEST)
        pc_ref[...] = lax.dot_general(dctx, w_ref[...], NT, preferred_element_type=F32, precision=HIGHEST)
        acc = g_ref[0][:, n_small_off:n_small_off + n_small]
        for a in range(1, N_DEV):
            acc = acc + g_ref[a][:, n_small_off:n_small_off + n_small]
        small_ref[...] = acc

    return pl.pallas_call(
        body, name=name,
        out_shape=(jax.ShapeDtypeStruct((D, cols), F32), jax.ShapeDtypeStruct((1, D), F32),
                   jax.ShapeDtypeStruct((1, n_small), F32)),
        scratch_shapes=[pltpu.VMEM((16, D), F32), pltpu.VMEM((16, cols), F32)],
        compiler_params=_params(vmem=VMEM_BIG),
    )(call, c_ctx, w_loc, dlat_loc, dctx_loc, gath)


def bmod_and_cctx_grad(gath, pcs, c_ctx, nd, name):
    D = c_ctx.shape[-1]

    def body(g_ref, pc_ref, cctx_ref, gb_ref, gc_ref):
        acc = g_ref[0][:, :nd] + g_ref[0][:, nd:2 * nd]
        for a in range(1, N_DEV):
            acc = acc + (g_ref[a][:, :nd] + g_ref[a][:, nd:2 * nd])
        gb_ref[...] = acc
        p = pc_ref[0]
        for a in range(1, N_DEV):
            p = p + pc_ref[a]
        z = cctx_ref[...]
        sg = _sigmoid(z)
        gc_ref[...] = p * (sg * (1.0 + z * (1.0 - sg)))

    return pl.pallas_call(
        body, name=name,
        out_shape=(jax.ShapeDtypeStruct((1, nd), F32), jax.ShapeDtypeStruct((1, D), F32)),
    )(gath, pcs, c_ctx)


def _mod_spec(D, which, nctx):
    return pl.BlockSpec((1, 3, D), lambda i: (jnp.where(i < nctx, 0, 3) + which, 0, 0))


def _acc_spec(D, nctx):
    return pl.BlockSpec((1, 8, D), lambda i: (jnp.where(i < nctx, 0, 1), 0, 0))


def _row(tm, n):
    return pl.BlockSpec((tm, n), lambda i: (i, 0))


def _stk(k, tm, n):
    return pl.BlockSpec((k, tm, n), lambda i: (0, i, 0))


def _two_stream_specs(tm, D, nctx):
    return [pl.BlockSpec((tm, D), lambda i: (jnp.minimum(i, nctx - 1), 0)),
            pl.BlockSpec((tm, D), lambda i: (jnp.maximum(i - nctx, 0), 0))]


def ffn_forward(srcs, mod6, which, g, ws, w_out, nctx, name):
    D = srcs[-1].shape[1]
    Tr = sum(s.shape[0] for s in srcs)
    nsh, _, nn = ws.shape
    nh = nsh // 2
    tm = ROW_TILE
    two = len(srcs) == 2

    def body(*refs):
        x_refs, (mod_ref, g_ref, ws_ref, wout_ref, xo_ref, hm_ref, ab_ref, h_ref, f_ref) = refs[:len(srcs)], refs[len(srcs):]
        x = jnp.where(pl.program_id(0) < nctx, x_refs[0][...], x_refs[1][...]) if two else x_refs[0][...]
        ms = mod_ref[0]
        shift, scale, gate = ms[0:1], ms[1:2], ms[2:3]
        r = lax.rsqrt(jnp.mean(x * x, axis=-1, keepdims=True) + EPS)
        hb = (((x * r) * g_ref[...]) * (1.0 + scale) + shift).astype(BF)
        hm_ref[...] = hb
        f = jnp.zeros((tm, D), F32)
        for j in range(nh):
            a = jnp.dot(hb, ws_ref[j], preferred_element_type=F32)
            b = jnp.dot(hb, ws_ref[nh + j], preferred_element_type=F32)
            ab_ref[j] = a.astype(BF)
            ab_ref[nh + j] = b.astype(BF)
            h = ((a * _sigmoid(a)) * b).astype(BF)
            h_ref[j] = h
            f = f + jnp.dot(h, wout_ref[j * nn:(j + 1) * nn, :], preferred_element_type=F32)
        f_ref[...] = f.astype(BF)
        xo_ref[...] = x + (0.5 * gate) * f

    src_specs = _two_stream_specs(tm, D, nctx) if two else [_row(tm, D)]
    return pl.pallas_call(
        body, name=name, grid=(Tr // tm,),
        in_specs=src_specs + [_mod_spec(D, which, nctx), _resident((1, D)), _resident(ws.shape), _resident(w_out.shape)],
        out_specs=[_row(tm, D), _row(tm, D), _stk(nsh, tm, nn), _stk(nh, tm, nn), _row(tm, D)],
        out_shape=[jax.ShapeDtypeStruct((Tr, D), F32), jax.ShapeDtypeStruct((Tr, D), BF),
                   jax.ShapeDtypeStruct((nsh, Tr, nn), BF), jax.ShapeDtypeStruct((nh, Tr, nn), BF),
                   jax.ShapeDtypeStruct((Tr, D), BF)],
        compiler_params=_params(vmem=VMEM_BIG, sem=("arbitrary",)),
    )(*srcs, mod6, g, ws, w_out)


def ffn_backward_rows(dxo, srcs, mod6, which, g, ab, fo, ws, w_out, nctx, name):
    D = srcs[-1].shape[1]
    Tr = sum(s.shape[0] for s in srcs)
    Tl = srcs[-1].shape[0]
    nsh, _, nn = ws.shape
    nh = nsh // 2
    tm = ROW_TILE
    two = len(srcs) == 2

    def body(*refs):
        dxo_ref, x_refs = refs[0], refs[1:1 + len(srcs)]
        mod_ref, g_ref, ab_ref, fo_ref, ws_ref, wout_ref, dx_ref, dab_ref, df_ref, acc_ref = refs[1 + len(srcs):]
        i = pl.program_id(0)

        @pl.when((i == 0) | (i == nctx))
        def _():
            acc_ref[...] = jnp.zeros_like(acc_ref)

        dxo = dxo_ref[...]
        x = jnp.where(i < nctx, x_refs[0][...], x_refs[1][...]) if two else x_refs[0][...]
        ms = mod_ref[0]
        scale, gate = ms[1:2], ms[2:3]
        gg = g_ref[...]
        dgate = jnp.sum(dxo * fo_ref[...].astype(F32), axis=0, keepdims=True) * 0.5
        dfb = (dxo * (0.5 * gate)).astype(BF)
        df_ref[...] = dfb
        dhm = jnp.zeros((tm, D), F32)
        for j in range(nh):
            dh = lax.dot_general(dfb, wout_ref[j * nn:(j + 1) * nn, :], NT, preferred_element_type=F32)
            a = ab_ref[j].astype(F32)
            b = ab_ref[nh + j].astype(F32)
            sg = _sigmoid(a)
            da = ((dh * b) * (sg * (1.0 + a * (1.0 - sg)))).astype(BF)
            db = (dh * (a * sg)).astype(BF)
            dab_ref[j] = da
            dab_ref[nh + j] = db
            dhm = dhm + lax.dot_general(da, ws_ref[j], NT, preferred_element_type=F32)
            dhm = dhm + lax.dot_general(db, ws_ref[nh + j], NT, preferred_element_type=F32)
        r = lax.rsqrt(jnp.mean(x * x, axis=-1, keepdims=True) + EPS)
        xh = x * r
        dshift = jnp.sum(dhm, axis=0, keepdims=True)
        dscale = jnp.sum(dhm * (xh * gg), axis=0, keepdims=True)
        dxh_g = dhm * (1.0 + scale)
        dg = jnp.sum(dxh_g * xh, axis=0, keepdims=True)
        dxh = dxh_g * gg
        dx_ref[...] = dxo + r * (dxh - xh * jnp.mean(dxh * xh, axis=-1, keepdims=True))
        for k, val in enumerate((dshift, dscale, dgate, dg)):
            acc_ref[0, k:k + 1, :] += val

    src_specs = _two_stream_specs(tm, D, nctx) if two else [_row(tm, D)]
    dx_spec = pl.BlockSpec((tm, D), lambda i: (jnp.maximum(i - nctx, 0), 0))
    return pl.pallas_call(
        body, name=name, grid=(Tr // tm,),
        in_specs=[_row(tm, D)] + src_specs + [_mod_spec(D, which, nctx), _resident((1, D)), _stk(nsh, tm, nn), _row(tm, D),
                                              _resident(ws.shape), _resident(w_out.shape)],
        out_specs=[dx_spec, _stk(nsh, tm, nn), _row(tm, D), _acc_spec(D, nctx)],
        out_shape=[jax.ShapeDtypeStruct((Tl, D), F32), jax.ShapeDtypeStruct((nsh, Tr, nn), BF),
                   jax.ShapeDtypeStruct((Tr, D), BF), jax.ShapeDtypeStruct((2, 8, D), F32)],
        compiler_params=_params(vmem=VMEM_BIG, sem=("arbitrary",)),
    )(dxo, *srcs, mod6, g, ab, fo, ws, w_out)


def _token_tile(T):
    return _pick(T, (1024, 768, 512, 256, 128))


def tn_matmul(a, b, name):
    T, K = a.shape
    N = b.shape[1]
    tk = _pick(K, (1024, 1408, 768, 512, 384, 256, 128))
    tn = _pick(N, (1024, 1408, 1664, 768, 512, 384, 256, 128))
    tt = _token_tile(T)
    nt = T // tt

    def body(a_ref, b_ref, o_ref, acc_ref):
        t = pl.program_id(2)

        @pl.when(t == 0)
        def _():
            acc_ref[...] = jnp.zeros_like(acc_ref)

        acc_ref[...] += lax.dot_general(a_ref[...], b_ref[...], TN, preferred_element_type=F32)

        @pl.when(t == nt - 1)
        def _():
            o_ref[...] = acc_ref[...].astype(BF)

    return pl.pallas_call(
        body, name=name, grid=(K // tk, N // tn, nt),
        in_specs=[pl.BlockSpec((tt, tk), lambda k, n, t: (t, k)), pl.BlockSpec((tt, tn), lambda k, n, t: (t, n))],
        out_specs=pl.BlockSpec((tk, tn), lambda k, n, t: (k, n)),
        out_shape=jax.ShapeDtypeStruct((K, N), BF),
        scratch_shapes=[pltpu.VMEM((tk, tn), F32)],
        compiler_params=_params(vmem=VMEM_BIG, sem=("parallel", "parallel", "arbitrary")),
    )(a, b)


def tn_ffn_in(hm, dab, name):
    T, D = hm.shape
    nsh, _, nn = dab.shape
    per = nsh // 2
    tt = _token_tile(T)
    nt = T // tt

    def body(a_ref, b_ref, o_ref, acc_ref):
        t = pl.program_id(1)

        @pl.when(t == 0)
        def _():
            acc_ref[...] = jnp.zeros_like(acc_ref)

        a = a_ref[...]
        for j in range(per):
            acc_ref[j] += lax.dot_general(a, b_ref[j], TN, preferred_element_type=F32)

        @pl.when(t == nt - 1)
        def _():
            o_ref[...] = acc_ref[...].astype(BF)

    return pl.pallas_call(
        body, name=name, grid=(2, nt),
        in_specs=[pl.BlockSpec((tt, D), lambda s, t: (t, 0)), pl.BlockSpec((per, tt, nn), lambda s, t: (s, t, 0))],
        out_specs=pl.BlockSpec((per, D, nn), lambda s, t: (s, 0, 0)),
        out_shape=jax.ShapeDtypeStruct((nsh, D, nn), BF),
        scratch_shapes=[pltpu.VMEM((per, D, nn), F32)],
        compiler_params=_params(vmem=VMEM_BIG, sem=("parallel", "arbitrary")),
    )(hm, dab)


def tn_ffn_out(h, df, name):
    nh, T, nn = h.shape
    D = df.shape[1]
    per = nh // 2
    tt = _token_tile(T)
    nt = T // tt

    def body(a_ref, b_ref, o_ref, acc_ref):
        t = pl.program_id(1)

        @pl.when(t == 0)
        def _():
            acc_ref[...] = jnp.zeros_like(acc_ref)

        b = b_ref[...]
        for j in range(per):
            acc_ref[j] += lax.dot_general(a_ref[j], b, TN, preferred_element_type=F32)

        @pl.when(t == nt - 1)
        def _():
            for j in range(per):
                o_ref[j * nn:(j + 1) * nn, :] = acc_ref[j].astype(BF)

    return pl.pallas_call(
        body, name=name, grid=(2, nt),
        in_specs=[pl.BlockSpec((per, tt, nn), lambda s, t: (s, t, 0)), pl.BlockSpec((tt, D), lambda s, t: (t, 0))],
        out_specs=pl.BlockSpec((per * nn, D), lambda s, t: (s, 0)),
        out_shape=jax.ShapeDtypeStruct((nh * nn, D), BF),
        scratch_shapes=[pltpu.VMEM((per, nn, D), F32)],
        compiler_params=_params(vmem=VMEM_BIG, sem=("parallel", "arbitrary")),
    )(h, df)


def _rope_apply(y, cos, s_next, s_prev):
    return y * cos + pltpu.roll(y, HEAD_DIM - 32, 1) * s_next + pltpu.roll(y, 32, 1) * s_prev


def _rope_transpose(dz, cos, s_next, s_prev):
    return dz * cos + pltpu.roll(dz * s_next, 32, 1) + pltpu.roll(dz * s_prev, HEAD_DIM - 32, 1)


def proj_forward(xa, mod6, g, w_in, qg, kg, tabs, offs, nctx, name):
    Tr, D = xa.shape
    P = w_in.shape[1]
    tm = ROW_TILE
    qo, ko = offs["q"], offs["k"]
    qw, kw = N_Q_HEADS * HEAD_DIM, N_KV_HEADS * HEAD_DIM
    scale_q = HEAD_DIM ** -0.5 * LOG2E

    def body(x_ref, mod_ref, g_ref, w_ref, qg_ref, kg_ref, tab_ref, hx_ref, pr_ref, q_ref, k_ref):
        x = x_ref[...]
        ms = mod_ref[0]
        shift, scale = ms[0:1], ms[1:2]
        r = lax.rsqrt(jnp.mean(x * x, axis=-1, keepdims=True) + EPS)
        hb = (((x * r) * g_ref[...]) * (1.0 + scale) + shift).astype(BF)
        hx_ref[...] = hb
        pr = jnp.dot(hb, w_ref[...], preferred_element_type=F32)
        pr_ref[...] = pr.astype(BF)
        cos, s_next, s_prev = tab_ref[0], tab_ref[1], tab_ref[2]

        def head(v, gain):
            n = v * lax.rsqrt(jnp.mean(v * v, axis=-1, keepdims=True) + EPS)
            return _rope_apply(n * gain, cos, s_next, s_prev)

        for h in range(N_Q_HEADS):
            lo = qo + h * HEAD_DIM
            q_ref[:, h * HEAD_DIM:(h + 1) * HEAD_DIM] = (head(pr[:, lo:lo + HEAD_DIM], qg_ref[...]) * scale_q).astype(BF)
        for h in range(N_KV_HEADS):
            lo = ko + h * HEAD_DIM
            k_ref[:, h * HEAD_DIM:(h + 1) * HEAD_DIM] = head(pr[:, lo:lo + HEAD_DIM], kg_ref[...]).astype(BF)

    return pl.pallas_call(
        body, name=name, grid=(Tr // tm,),
        in_specs=[_row(tm, D), _mod_spec(D, 1, nctx), _resident((1, D)), _resident(w_in.shape),
                  _resident((1, HEAD_DIM)), _resident((1, HEAD_DIM)),
                  pl.BlockSpec((3, tm, HEAD_DIM), lambda i: (0, i, 0))],
        out_specs=[_row(tm, D), _row(tm, P), _row(tm, qw), _row(tm, kw)],
        out_shape=[jax.ShapeDtypeStruct((Tr, D), BF), jax.ShapeDtypeStruct((Tr, P), BF),
                   jax.ShapeDtypeStruct((Tr, qw), BF), jax.ShapeDtypeStruct((Tr, kw), BF)],
        compiler_params=_params(vmem=VMEM_BIG, sem=("arbitrary",)),
    )(xa, mod6, g, w_in, qg, kg, tabs)


def _shifted(u, first_row, last_row):
    T = u.shape[0]
    prev = jnp.where(first_row, 0.0, pltpu.roll(u, 1, 0))
    nxt = jnp.where(last_row, 0.0, pltpu.roll(u, T - 1, 0))
    return prev, nxt


def conv_forward(proj, conv_w, offs, Tc, name):
    Ta = proj.shape[0]
    T = Ta - Tc
    Dc = conv_w.shape[1]
    cb = offs["cv"] // 384

    def body(p_ref, w_ref, y_ref):
        rows = lax.broadcasted_iota(jnp.int32, (T, 128), 0)
        u = p_ref[pl.ds(Tc, T), 128:256].astype(F32) * p_ref[pl.ds(Tc, T), 256:384].astype(F32)
        prev, nxt = _shifted(u, rows == 0, rows == T - 1)
        w = w_ref[...]
        cv = prev * w[0:1] + u * w[1:2] + nxt * w[2:3]
        y_ref[...] = (p_ref[pl.ds(Tc, T), 0:128].astype(F32) * cv).astype(BF)

    return pl.pallas_call(
        body, name=name, grid=(Dc // 128,),
        in_specs=[pl.BlockSpec((Ta, 384), lambda j: (0, cb + j)), pl.BlockSpec((CONV_TAPS, 128), lambda j: (0, j))],
        out_specs=pl.BlockSpec((T, 128), lambda j: (0, j)),
        out_shape=jax.ShapeDtypeStruct((T, Dc), BF),
        compiler_params=_params(vmem=VMEM_BIG, sem=("arbitrary",)),
    )(proj, conv_w)


def conv_backward(dproj, dy, proj, conv_w, offs, Tc, name):
    Ta = proj.shape[0]
    T = Ta - Tc
    Dc = conv_w.shape[1]
    cb = offs["cv"] // 384

    def body(dp_any, dy_ref, p_ref, w_ref, o_ref, dw_ref):
        rows = lax.broadcasted_iota(jnp.int32, (T, 128), 0)
        first, last = rows == 0, rows == T - 1
        bg = p_ref[pl.ds(Tc, T), 0:128].astype(F32)
        cg = p_ref[pl.ds(Tc, T), 128:256].astype(F32)
        vc = p_ref[pl.ds(Tc, T), 256:384].astype(F32)
        dy = dy_ref[...].astype(F32)
        u = cg * vc
        prev, nxt = _shifted(u, first, last)
        w = w_ref[...]
        cv = prev * w[0:1] + u * w[1:2] + nxt * w[2:3]
        o_ref[pl.ds(0, Tc), :] = jnp.zeros((Tc, 384), BF)
        o_ref[pl.ds(Tc, T), 0:128] = (dy * cv).astype(BF)
        dcv = dy * bg
        dprev, dnxt = _shifted(dcv, first, last)
        du = dnxt * w[0:1] + dcv * w[1:2] + dprev * w[2:3]
        o_ref[pl.ds(Tc, T), 128:256] = (du * vc).astype(BF)
        o_ref[pl.ds(Tc, T), 256:384] = (du * cg).astype(BF)
        dw_ref[...] = jnp.zeros_like(dw_ref)
        for k, tap in enumerate((prev, u, nxt)):
            dw_ref[k:k + 1, :] = jnp.sum(dcv * tap, axis=0, keepdims=True)

    blk = pl.BlockSpec((Ta, 384), lambda j: (0, cb + j))
    return pl.pallas_call(
        body, name=name, grid=(Dc // 128,),
        in_specs=[ANY, pl.BlockSpec((T, 128), lambda j: (0, j)), blk, pl.BlockSpec((CONV_TAPS, 128), lambda j: (0, j))],
        out_specs=[blk, pl.BlockSpec((8, 128), lambda j: (0, j))],
        out_shape=[jax.ShapeDtypeStruct(dproj.shape, BF), jax.ShapeDtypeStruct((8, Dc), F32)],
        input_output_aliases={0: 0},
        compiler_params=_params(vmem=VMEM_BIG, sem=("arbitrary",)),
    )(dproj, dy, proj, conv_w)


def _kv_chunk(Ta):
    return _pick(Ta, (768, 512, 384, 256, 128))


def _stack_heads(v):
    return jnp.concatenate([v[:, h * HEAD_DIM:(h + 1) * HEAD_DIM] for h in range(GROUP)], axis=0)


def attention_forward(q, k, proj, offs, Tc, name):
    Ta = k.shape[0]
    T = Ta - Tc
    tq = ROW_TILE
    kc = _kv_chunk(Ta)
    gw = GROUP * HEAD_DIM
    vblk = offs["v"] // HEAD_DIM
    qoff = Tc // tq
    n = GROUP * tq
    halves = 2
    hn = n // halves

    def body(q_ref, k_ref, v_ref, o_ref, lse_ref, vx_ref, m_ref, acc_ref):
        @pl.when(pl.program_id(1) == 0)
        def _():
            vx_ref[:, 0:HEAD_DIM] = v_ref[...]
            vx_ref[:, HEAD_DIM:2 * HEAD_DIM] = jnp.ones((Ta, HEAD_DIM), BF)

        qs = _stack_heads(q_ref[...])
        m_ref[...] = jnp.full((n, 1), -1e30, F32)
        acc_ref[...] = jnp.zeros((n, 2 * HEAD_DIM), F32)

        def step(c, carry):
            rows = pl.ds(pl.multiple_of(c * kc, kc), kc)
            kk = k_ref[rows, :]
            vv = vx_ref[rows, :]
            for hf in range(halves):
                sl = slice(hf * hn, (hf + 1) * hn)
                s = lax.dot_general(qs[sl], kk, NT, preferred_element_type=F32)
                m_prev = m_ref[sl]
                m_new = jnp.maximum(m_prev, jnp.max(s, axis=-1, keepdims=True))
                p = jnp.exp2(s - m_new).astype(BF)
                acc_ref[sl] = jnp.exp2(m_prev - m_new) * acc_ref[sl] + jnp.dot(p, vv, preferred_element_type=F32)
                m_ref[sl] = m_new
            return carry

        lax.fori_loop(0, Ta // kc, step, 0)
        acc = acc_ref[...]
        l = acc[:, HEAD_DIM:HEAD_DIM + 1]
        o = acc[:, 0:HEAD_DIM] / l
        lse = m_ref[...] + jnp.log2(l)
        for h in range(GROUP):
            o_ref[:, h * HEAD_DIM:(h + 1) * HEAD_DIM] = o[h * tq:(h + 1) * tq].astype(BF)
            lse_ref[0, :, h:h + 1] = lse[h * tq:(h + 1) * tq]

    return pl.pallas_call(
        body, name=name, grid=(N_KV_HEADS, T // tq),
        in_specs=[pl.BlockSpec((tq, gw), lambda j, i: (i + qoff, j)),
                  pl.BlockSpec((Ta, HEAD_DIM), lambda j, i: (0, j)),
                  pl.BlockSpec((Ta, HEAD_DIM), lambda j, i: (0, vblk + j))],
        out_specs=[pl.BlockSpec((tq, gw), lambda j, i: (i, j)),
                   pl.BlockSpec((1, tq, GROUP), lambda j, i: (j, i, 0))],
        out_shape=[jax.ShapeDtypeStruct((T, N_Q_HEADS * HEAD_DIM), BF),
                   jax.ShapeDtypeStruct((N_KV_HEADS, T, GROUP), F32)],
        scratch_shapes=[pltpu.VMEM((Ta, 2 * HEAD_DIM), BF), pltpu.VMEM((n, 1), F32), pltpu.VMEM((n, 2 * HEAD_DIM), F32)],
        compiler_params=_params(vmem=VMEM_BIG, sem=("arbitrary", "arbitrary")),
    )(q, k, proj)


def attention_backward(q, k, proj, o, lse, do, offs, Tc, name):
    Ta = k.shape[0]
    T = Ta - Tc
    tq = ROW_TILE
    kc = _kv_chunk(Ta)
    gw = GROUP * HEAD_DIM
    vblk = offs["v"] // HEAD_DIM
    qoff = Tc // tq

    def body(q_ref, k_ref, v_ref, o_ref, lse_ref, do_ref, dq_ref, dk_ref, dv_ref):
        @pl.when(pl.program_id(1) == 0)
        def _():
            dk_ref[...] = jnp.zeros_like(dk_ref)
            dv_ref[...] = jnp.zeros_like(dv_ref)

        qs = _stack_heads(q_ref[...])
        dob = do_ref[...]
        dos = _stack_heads(dob)
        delta = jnp.concatenate(
            [jnp.sum(dob[:, h * HEAD_DIM:(h + 1) * HEAD_DIM].astype(F32)
                     * o_ref[:, h * HEAD_DIM:(h + 1) * HEAD_DIM].astype(F32), axis=-1, keepdims=True)
             for h in range(GROUP)], axis=0)
        lse = jnp.concatenate([lse_ref[0, :, h:h + 1] for h in range(GROUP)], axis=0)

        def step(c, dq):
            rows = pl.ds(pl.multiple_of(c * kc, kc), kc)
            kk = k_ref[rows, :]
            vv = v_ref[rows, :]
            s = lax.dot_general(qs, kk, NT, preferred_element_type=F32)
            p = jnp.exp2(s - lse)
            dp = lax.dot_general(dos, vv, NT, preferred_element_type=F32)
            ds = (p * (dp - delta)).astype(BF)
            dv_ref[rows, :] += lax.dot_general(p.astype(BF), dos, TN, preferred_element_type=F32)
            dk_ref[rows, :] += lax.dot_general(ds, qs, TN, preferred_element_type=F32)
            return dq + jnp.dot(ds, kk, preferred_element_type=F32)

        dq = lax.fori_loop(0, Ta // kc, step, jnp.zeros((GROUP * tq, HEAD_DIM), F32))
        for h in range(GROUP):
            dq_ref[:, h * HEAD_DIM:(h + 1) * HEAD_DIM] = dq[h * tq:(h + 1) * tq]

    return pl.pallas_call(
        body, name=name, grid=(N_KV_HEADS, T // tq),
        in_specs=[pl.BlockSpec((tq, gw), lambda j, i: (i + qoff, j)),
                  pl.BlockSpec((Ta, HEAD_DIM), lambda j, i: (0, j)),
                  pl.BlockSpec((Ta, HEAD_DIM), lambda j, i: (0, vblk + j)),
                  pl.BlockSpec((tq, gw), lambda j, i: (i, j)),
                  pl.BlockSpec((1, tq, GROUP), lambda j, i: (j, i, 0)),
                  pl.BlockSpec((tq, gw), lambda j, i: (i, j))],
        out_specs=[pl.BlockSpec((tq, gw), lambda j, i: (i, j)),
                   pl.BlockSpec((Ta, HEAD_DIM), lambda j, i: (0, j)),
                   pl.BlockSpec((Ta, HEAD_DIM), lambda j, i: (0, j))],
        out_shape=[jax.ShapeDtypeStruct((T, N_Q_HEADS * HEAD_DIM), F32),
                   jax.ShapeDtypeStruct((Ta, N_KV_HEADS * HEAD_DIM), F32),
                   jax.ShapeDtypeStruct((Ta, N_KV_HEADS * HEAD_DIM), F32)],
        compiler_params=_params(vmem=VMEM_BIG, sem=("arbitrary", "arbitrary")),
    )(q, k, proj, o, lse, do)


def _norm_rope_backward(dz, raw, gg, cos, s_next, s_prev):
    r = lax.rsqrt(jnp.mean(raw * raw, axis=-1, keepdims=True) + EPS)
    n = raw * r
    dy = _rope_transpose(dz, cos, s_next, s_prev)
    dn = dy * gg
    return r * (dn - n * jnp.mean(dn * n, axis=-1, keepdims=True)), jnp.sum(dy * n, axis=0, keepdims=True)


def q_norm_backward(dproj, dq, proj, gain, tabs, offs, nctx, name):
    Ta = proj.shape[0]
    tm = ROW_TILE
    w = N_Q_HEADS * HEAD_DIM
    cb = offs["q"] // w
    zscale = HEAD_DIM ** -0.5

    def body(dp_any, dz_ref, raw_ref, g_ref, tab_ref, o_ref, dg_ref):
        i = pl.program_id(0)

        @pl.when(i == 0)
        def _():
            dg_ref[...] = jnp.zeros_like(dg_ref)

        @pl.when(i < nctx)
        def _():
            o_ref[...] = jnp.zeros_like(o_ref)

        @pl.when(i >= nctx)
        def _():
            cos, s_next, s_prev = tab_ref[0], tab_ref[1], tab_ref[2]
            dg = jnp.zeros((1, HEAD_DIM), F32)
            for h in range(N_Q_HEADS):
                sl = slice(h * HEAD_DIM, (h + 1) * HEAD_DIM)
                dv, dgh = _norm_rope_backward(dz_ref[:, sl] * zscale, raw_ref[:, sl].astype(F32), g_ref[...], cos, s_next, s_prev)
                o_ref[:, sl] = dv.astype(BF)
                dg = dg + dgh
            dg_ref[0:1, :] += dg

    return pl.pallas_call(
        body, name=name, grid=(Ta // tm,),
        in_specs=[ANY, pl.BlockSpec((tm, w), lambda i: (jnp.maximum(i - nctx, 0), 0)),
                  pl.BlockSpec((tm, w), lambda i: (i, cb)), _resident((1, HEAD_DIM)),
                  pl.BlockSpec((3, tm, HEAD_DIM), lambda i: (0, i, 0))],
        out_specs=[pl.BlockSpec((tm, w), lambda i: (i, cb)), pl.BlockSpec((8, HEAD_DIM), lambda i: (0, 0))],
        out_shape=[jax.ShapeDtypeStruct(dproj.shape, BF), jax.ShapeDtypeStruct((8, HEAD_DIM), F32)],
        input_output_aliases={0: 0},
        compiler_params=_params(sem=("arbitrary",)),
    )(dproj, dq, proj, gain, tabs)


def kv_backward(dproj, dk, dv, proj, gain, tabs, offs, name):
    Ta = proj.shape[0]
    tm = ROW_TILE
    kw = N_KV_HEADS * HEAD_DIM
    cb = offs["k"] // (2 * kw)
    kb = offs["k"] // kw
    zscale = 1.0 / LOG2E

    def body(dp_any, dk_ref, dv_ref, raw_ref, g_ref, tab_ref, o_ref, dg_ref):
        @pl.when(pl.program_id(0) == 0)
        def _():
            dg_ref[...] = jnp.zeros_like(dg_ref)

        cos, s_next, s_prev = tab_ref[0], tab_ref[1], tab_ref[2]
        dg = jnp.zeros((1, HEAD_DIM), F32)
        for h in range(N_KV_HEADS):
            sl = slice(h * HEAD_DIM, (h + 1) * HEAD_DIM)
            dr, dgh = _norm_rope_backward(dk_ref[:, sl] * zscale, raw_ref[:, sl].astype(F32), g_ref[...], cos, s_next, s_prev)
            o_ref[:, sl] = dr.astype(BF)
            dg = dg + dgh
        o_ref[:, kw:2 * kw] = dv_ref[...].astype(BF)
        dg_ref[0:1, :] += dg

    return pl.pallas_call(
        body, name=name, grid=(Ta // tm,),
        in_specs=[ANY, _row(tm, kw), _row(tm, kw), pl.BlockSpec((tm, kw), lambda i: (i, kb)),
                  _resident((1, HEAD_DIM)), pl.BlockSpec((3, tm, HEAD_DIM), lambda i: (0, i, 0))],
        out_specs=[pl.BlockSpec((tm, 2 * kw), lambda i: (i, cb)), pl.BlockSpec((8, HEAD_DIM), lambda i: (0, 0))],
        out_shape=[jax.ShapeDtypeStruct(dproj.shape, BF), jax.ShapeDtypeStruct((8, HEAD_DIM), F32)],
        input_output_aliases={0: 0},
        compiler_params=_params(sem=("arbitrary",)),
    )(dproj, dk, dv, proj, gain, tabs)


def merge_forward(x1, mod6, yc, o, proj, w_bc, w_ba, w_o, offs, Tc, name):
    T, D = yc.shape[0], x1.shape[1]
    tm = ROW_TILE
    roff = Tc // tm
    gb = offs["gt"] // (2 * D)

    def body(x_ref, mod_ref, yc_ref, o_ref, gt_ref, wbc_ref, wba_ref, wo_ref, xo_ref, pc_ref, pa_ref, m_ref, z_ref):
        gate = mod_ref[0][2:3]
        pc = jnp.dot(yc_ref[...], wbc_ref[...], preferred_element_type=F32)
        pa = jnp.dot(o_ref[...], wba_ref[...], preferred_element_type=F32)
        pc_ref[...] = pc.astype(BF)
        pa_ref[...] = pa.astype(BF)
        mb = (_sigmoid(gt_ref[:, 0:D].astype(F32)) * pc + _sigmoid(gt_ref[:, D:2 * D].astype(F32)) * pa).astype(BF)
        m_ref[...] = mb
        z = jnp.dot(mb, wo_ref[...], preferred_element_type=F32)
        z_ref[...] = z.astype(BF)
        xo_ref[...] = x_ref[...] + gate * z

    return pl.pallas_call(
        body, name=name, grid=(T // tm,),
        in_specs=[pl.BlockSpec((tm, D), lambda i: (i + roff, 0)), _mod_spec(D, 1, 0), _row(tm, yc.shape[1]), _row(tm, o.shape[1]),
                  pl.BlockSpec((tm, 2 * D), lambda i: (i + roff, gb)),
                  _resident(w_bc.shape), _resident(w_ba.shape), _resident(w_o.shape)],
        out_specs=[_row(tm, D)] * 5,
        out_shape=[jax.ShapeDtypeStruct((T, D), F32)] + [jax.ShapeDtypeStruct((T, D), BF)] * 4,
        compiler_params=_params(vmem=VMEM_BIG, sem=("arbitrary",)),
    )(x1, mod6, yc, o, proj, w_bc, w_ba, w_o)


def merge_backward_rows(dx2, mod6, z, pc, pa, proj, w_bc, w_ba, w_o, offs, Tc, name):
    T, D = dx2.shape
    Ta, P = proj.shape
    tm = ROW_TILE
    nctx = Tc // tm
    gb = offs["gt"] // (2 * D)
    dcw, dqw = w_bc.shape[0], w_ba.shape[0]

    def body(dx_ref, mod_ref, z_ref, pc_ref, pa_ref, gt_ref, wbc_ref, wba_ref, wo_ref,
             dgt_ref, dg_ref, dpc_ref, dpa_ref, dyc_ref, do_ref, acc_ref):
        i = pl.program_id(0)

        @pl.when(i == 0)
        def _():
            acc_ref[...] = jnp.zeros_like(acc_ref)

        @pl.when(i < nctx)
        def _():
            dgt_ref[...] = jnp.zeros_like(dgt_ref)

        @pl.when(i >= nctx)
        def _():
            gate = mod_ref[0][2:3]
            dx = dx_ref[...]
            acc_ref[0:1, :] += jnp.sum(dx * z_ref[...].astype(F32), axis=0, keepdims=True)
            dgb = (dx * gate).astype(BF)
            dg_ref[...] = dgb
            dm = lax.dot_general(dgb, wo_ref[...], NT, preferred_element_type=F32)
            sc = _sigmoid(gt_ref[:, 0:D].astype(F32))
            sa = _sigmoid(gt_ref[:, D:2 * D].astype(F32))
            pc = pc_ref[...].astype(F32)
            pa = pa_ref[...].astype(F32)
            dpc = (dm * sc).astype(BF)
            dpa = (dm * sa).astype(BF)
            dpc_ref[...] = dpc
            dpa_ref[...] = dpa
            dgt_ref[:, 0:D] = ((dm * pc) * (sc * (1.0 - sc))).astype(BF)
            dgt_ref[:, D:2 * D] = ((dm * pa) * (sa * (1.0 - sa))).astype(BF)
            dyc_ref[...] = lax.dot_general(dpc, wbc_ref[...], NT, preferred_element_type=F32).astype(BF)
            do_ref[...] = lax.dot_general(dpa, wba_ref[...], NT, preferred_element_type=F32).astype(BF)

    lat = lambda n: pl.BlockSpec((tm, n), lambda i: (jnp.maximum(i - nctx, 0), 0))
    return pl.pallas_call(
        body, name=name, grid=(Ta // tm,),
        in_specs=[lat(D), _mod_spec(D, 1, 0), lat(D), lat(D), lat(D),
                  pl.BlockSpec((tm, 2 * D), lambda i: (i, gb)),
                  _resident(w_bc.shape), _resident(w_ba.shape), _resident(w_o.shape)],
        out_specs=[pl.BlockSpec((tm, 2 * D), lambda i: (i, gb)), lat(D), lat(D), lat(D), lat(dcw), lat(dqw),
                   pl.BlockSpec((8, D), lambda i: (0, 0))],
        out_shape=[jax.ShapeDtypeStruct((Ta, P), BF)] + [jax.ShapeDtypeStruct((T, D), BF)] * 3
                  + [jax.ShapeDtypeStruct((T, dcw), BF), jax.ShapeDtypeStruct((T, dqw), BF), jax.ShapeDtypeStruct((8, D), F32)],
        compiler_params=_params(vmem=VMEM_BIG, sem=("arbitrary",)),
    )(dx2, mod6, z, pc, pa, proj, w_bc, w_ba, w_o)


def proj_backward_rows(dproj, dres, xa, mod6, g, w_in, nctx, name):
    Tr, D = xa.shape
    P = w_in.shape[1]
    tm = ROW_TILE

    def body(dp_ref, dres_ref, x_ref, mod_ref, g_ref, w_ref, dx_ref, acc_ref):
        i = pl.program_id(0)

        @pl.when((i == 0) | (i == nctx))
        def _():
            acc_ref[...] = jnp.zeros_like(acc_ref)

        x = x_ref[...]
        scale = mod_ref[0][1:2]
        gg = g_ref[...]
        dhm = lax.dot_general(dp_ref[...], w_ref[...], NT, preferred_element_type=F32)
        r = lax.rsqrt(jnp.mean(x * x, axis=-1, keepdims=True) + EPS)
        xh = x * r
        dshift = jnp.sum(dhm, axis=0, keepdims=True)
        dscale = jnp.sum(dhm * (xh * gg), axis=0, keepdims=True)
        dxh_g = dhm * (1.0 + scale)
        dg = jnp.sum(dxh_g * xh, axis=0, keepdims=True)
        dxh = dxh_g * gg
        res = jnp.where(i < nctx, 0.0, dres_ref[...])
        dx_ref[...] = res + r * (dxh - xh * jnp.mean(dxh * xh, axis=-1, keepdims=True))
        for k, val in enumerate((dshift, dscale, dg)):
            acc_ref[0, k:k + 1, :] += val

    return pl.pallas_call(
        body, name=name, grid=(Tr // tm,),
        in_specs=[_row(tm, P), pl.BlockSpec((tm, D), lambda i: (jnp.maximum(i - nctx, 0), 0)), _row(tm, D),
                  _mod_spec(D, 1, nctx), _resident((1, D)), _resident(w_in.shape)],
        out_specs=[_row(tm, D), _acc_spec(D, nctx)],
        out_shape=[jax.ShapeDtypeStruct((Tr, D), F32), jax.ShapeDtypeStruct((2, 8, D), F32)],
        compiler_params=_params(vmem=VMEM_BIG, sem=("arbitrary",)),
    )(dproj, dres, xa, mod6, g, w_in)


def loss_and_final_norm_backward(x3, tgt, gf, name):
    T, D = x3.shape
    tm = ROW_TILE

    def body(x_ref, t_ref, g_ref, dx_ref, acc_ref):
        @pl.when(pl.program_id(0) == 0)
        def _():
            acc_ref[...] = jnp.zeros_like(acc_ref)

        x = x_ref[...]
        gg = g_ref[...]
        r = lax.rsqrt(jnp.mean(x * x, axis=-1, keepdims=True) + EPS)
        xh = x * r
        e = xh * gg - t_ref[...]
        part = 0.5 * jnp.sum(jnp.mean(e * e, axis=-1, keepdims=True), axis=0, keepdims=True)
        dy = e * (1.0 / D)
        dyg = dy * gg
        dx_ref[...] = r * (dyg - xh * jnp.mean(dyg * xh, axis=-1, keepdims=True))
        acc_ref[0:1, :] += jnp.sum(dy * xh, axis=0, keepdims=True)
        acc_ref[1:2, :] += jnp.broadcast_to(part, (1, D))

    return pl.pallas_call(
        body, name=name, grid=(T // tm,),
        in_specs=[_row(tm, D), _row(tm, D), _resident((1, D))],
        out_specs=[_row(tm, D), pl.BlockSpec((8, D), lambda i: (0, 0))],
        out_shape=[jax.ShapeDtypeStruct((T, D), F32), jax.ShapeDtypeStruct((8, D), F32)],
        compiler_params=_params(sem=("arbitrary",)),
    )(x3, tgt, gf)


def _adam_update(w, g, m, v):
    c1 = 1.0 - ADAM_B1 ** ADAM_STEP
    c2 = 1.0 - ADAM_B2 ** ADAM_STEP
    m = ADAM_B1 * m + (1.0 - ADAM_B1) * g
    v = ADAM_B2 * v + (1.0 - ADAM_B2) * (g * g)
    return -ADAM_LR * ((m / c1) / (jnp.sqrt(v / c2) + ADAM_EPS) + ADAM_WD * w), m, v


def adamw(w, g, m, v, name):
    R, C = w.shape
    tr = _row_tile(R, C)

    def body(w_ref, g_ref, m_ref, v_ref, d_ref, nm_ref, nv_ref):
        d_ref[...], nm_ref[...], nv_ref[...] = _adam_update(w_ref[...], g_ref[...], m_ref[...], v_ref[...])

    blk = pl.BlockSpec((tr, C), lambda i: (i, 0))
    return pl.pallas_call(
        body, name=name, grid=(R // tr,),
        in_specs=[blk] * 4, out_specs=[blk] * 3,
        out_shape=[jax.ShapeDtypeStruct((R, C), F32)] * 3,
        compiler_params=_params(sem=("parallel",)),
    )(w, g, m, v)


def adamw_reduced(place, ps, recv_b, w, m, v, name):
    R, C = w.shape
    tr = _row_tile(R, C)

    def body(place_ref, p_ref, b_ref, w_ref, m_ref, v_ref, g_ref, d_ref, nm_ref, nv_ref):
        g = p_ref[0].astype(F32)
        for j in range(3):
            g = g + b_ref[j].astype(F32)
        g_ref[...] = g
        d_ref[...], nm_ref[...], nv_ref[...] = _adam_update(w_ref[...], g, m_ref[...], v_ref[...])

    blk = pl.BlockSpec((tr, C), lambda r, pr: (r, 0))
    return pl.pallas_call(
        body, name=name,
        grid_spec=pltpu.PrefetchScalarGridSpec(
            num_scalar_prefetch=1, grid=(R // tr,),
            in_specs=[pl.BlockSpec((1, tr, C), lambda r, pr: (pr[1], r, 0)),
                      pl.BlockSpec((3, tr, C), lambda r, pr: (0, r, 0)), blk, blk, blk],
            out_specs=[blk] * 4),
        out_shape=[jax.ShapeDtypeStruct((R, C), F32)] * 4,
    )(place, ps, recv_b, w, m, v)


def _rope_tables(T, Tc):
    rows = T // GRID_W
    row = jnp.repeat(jnp.arange(rows), GRID_W).astype(F32)
    col = jnp.tile(jnp.arange(GRID_W), rows).astype(F32)
    n_freq = HEAD_DIM // 4
    inv = ROPE_THETA ** (-jnp.arange(n_freq, dtype=F32) / n_freq)
    cr, sr = jnp.cos(row[:, None] * inv), jnp.sin(row[:, None] * inv)
    cc, sc = jnp.cos(col[:, None] * inv), jnp.sin(col[:, None] * inv)
    zero = jnp.zeros_like(sr)
    cos = jnp.concatenate([cr, cr, cc, cc], axis=1)
    s_next = jnp.concatenate([-sr, zero, -sc, zero], axis=1)
    s_prev = jnp.concatenate([zero, sr, zero, sc], axis=1)
    lat = jnp.stack([cos, s_next, s_prev])
    ctx = jnp.stack([jnp.ones((Tc, HEAD_DIM), F32), jnp.zeros((Tc, HEAD_DIM), F32), jnp.zeros((Tc, HEAD_DIM), F32)])
    return jnp.concatenate([ctx, lat], axis=1)


BIG = ("ffn1_w_in", "ffn1_w_out", "w_in", "w_branch_conv", "w_branch_attn", "w_out", "ffn2_w_in", "ffn2_w_out")


def _regroup_w_in(stacked, D, Dc, qw, kw):
    w = stacked.transpose(1, 0, 2).reshape(D, -1)
    o = 0
    parts = {}
    for nme, wd in (("bg", Dc), ("cg", Dc), ("vc", Dc), ("q", qw), ("k", kw), ("v", kw), ("gt", 2 * D)):
        parts[nme] = w[:, o:o + wd]
        o += wd
    nb = Dc // 128
    cv = jnp.stack([parts[n].reshape(D, nb, 128) for n in ("bg", "cg", "vc")], axis=2).reshape(D, 3 * Dc)
    return jnp.concatenate([cv, parts["q"], parts["gt"], parts["k"], parts["v"]], axis=1)


def _ungroup_w_in_grad(g, D, Dc, qw, kw):
    nb = Dc // 128
    cv = g[:, :3 * Dc].reshape(D, nb, 3, 128)
    o = 3 * Dc
    q = g[:, o:o + qw]
    gt = g[:, o + qw:o + qw + 2 * D]
    k = g[:, o + qw + 2 * D:o + qw + 2 * D + kw]
    v = g[:, o + qw + 2 * D + kw:]
    nat = jnp.concatenate([cv[:, :, 0].reshape(D, Dc), cv[:, :, 1].reshape(D, Dc), cv[:, :, 2].reshape(D, Dc), q, k, v, gt], axis=1)
    return nat.reshape(D, N_DEV, -1).transpose(1, 0, 2)


def kernel(x, c, ctx, c_ctx, w_mod, b_mod, norm1_g, norm2_g, norm3_g, ffn1_w_in, ffn1_w_out, w_in, conv_w, q_norm_g, k_norm_g, w_branch_conv, w_branch_attn, w_out, ffn2_w_in, ffn2_w_out, final_g, loss_target, m_c_ctx, m_w_mod, m_b_mod, m_norm1_g, m_norm2_g, m_norm3_g, m_ffn1_w_in, m_ffn1_w_out, m_w_in, m_conv_w, m_q_norm_g, m_k_norm_g, m_w_branch_conv, m_w_branch_attn, m_w_out, m_ffn2_w_in, m_ffn2_w_out, m_final_g, v_c_ctx, v_w_mod, v_b_mod, v_norm1_g, v_norm2_g, v_norm3_g, v_ffn1_w_in, v_ffn1_w_out, v_w_in, v_conv_w, v_q_norm_g, v_k_norm_g, v_w_branch_conv, v_w_branch_attn, v_w_out, v_ffn2_w_in, v_ffn2_w_out, v_final_g):
    weights = dict(c_ctx=c_ctx, w_mod=w_mod, b_mod=b_mod, norm1_g=norm1_g, norm2_g=norm2_g, norm3_g=norm3_g,
                   ffn1_w_in=ffn1_w_in, ffn1_w_out=ffn1_w_out, w_in=w_in, conv_w=conv_w, q_norm_g=q_norm_g,
                   k_norm_g=k_norm_g, w_branch_conv=w_branch_conv, w_branch_attn=w_branch_attn, w_out=w_out,
                   ffn2_w_in=ffn2_w_in, ffn2_w_out=ffn2_w_out, final_g=final_g)
    moms = dict(c_ctx=(m_c_ctx, v_c_ctx), w_mod=(m_w_mod, v_w_mod), b_mod=(m_b_mod, v_b_mod),
                norm1_g=(m_norm1_g, v_norm1_g), norm2_g=(m_norm2_g, v_norm2_g), norm3_g=(m_norm3_g, v_norm3_g),
                ffn1_w_in=(m_ffn1_w_in, v_ffn1_w_in), ffn1_w_out=(m_ffn1_w_out, v_ffn1_w_out), w_in=(m_w_in, v_w_in),
                conv_w=(m_conv_w, v_conv_w), q_norm_g=(m_q_norm_g, v_q_norm_g), k_norm_g=(m_k_norm_g, v_k_norm_g),
                w_branch_conv=(m_w_branch_conv, v_w_branch_conv), w_branch_attn=(m_w_branch_attn, v_w_branch_attn),
                w_out=(m_w_out, v_w_out), ffn2_w_in=(m_ffn2_w_in, v_ffn2_w_in), ffn2_w_out=(m_ffn2_w_out, v_ffn2_w_out),
                final_g=(m_final_g, v_final_g))
    order = list(weights)

    T, D = x.shape[1], x.shape[2]
    Tc = ctx.shape[1]
    nctx = Tc // ROW_TILE
    nd = N_MOD * D
    Dc = conv_w.shape[2] * N_DEV
    qw, kw = N_Q_HEADS * HEAD_DIM, N_KV_HEADS * HEAD_DIM
    offs, o = {}, 0
    for nme, wd in (("cv", 3 * Dc), ("q", qw), ("gt", 2 * D), ("k", kw), ("v", kw)):
        offs[nme] = o
        o += wd

    ax, ay, ac = lax.axis_index("x"), lax.axis_index("y"), lax.axis_index("c")
    me = 4 * ax + 2 * ay + ac
    place = jnp.stack([ac, 2 * ax + ay]).astype(jnp.int32)

    gathered = dict(zip(BIG, allgather_two_level([weights[n][0].astype(BF) for n in BIG], "ag_weights")))
    rows2d = lambda a: a.reshape(-1, a.shape[-1])
    full = {n: (gathered[n] if n in ("ffn1_w_in", "ffn2_w_in") else rows2d(gathered[n])) for n in BIG if n != "w_in"}
    full["w_in"] = _regroup_w_in(gathered["w_in"], D, Dc, qw, kw)

    mod_cols = w_mod.shape[2]
    cw_loc = conv_w[0]
    cpad = (-(D + CONV_TAPS * cw_loc.shape[1])) % 128
    pay = jnp.concatenate([c.reshape(1, D), cw_loc.reshape(1, -1), jnp.zeros((1, cpad), F32)], axis=1)
    call = allgather_direct(pay, "ag_cond")
    conv_full = call[:, 0, D:D + CONV_TAPS * cw_loc.shape[1]].reshape(N_DEV, CONV_TAPS, -1).transpose(1, 0, 2).reshape(CONV_TAPS, Dc)
    b_loc = lax.dynamic_slice_in_dim(b_mod, me * mod_cols, mod_cols, axis=1)
    cctx2 = c_ctx.reshape(1, D)
    mod_part = mod_forward(call, cctx2, w_mod[0], b_loc, "mod_fwd")
    mod_all = allgather_direct(mod_part, "ag_mod")
    mod_lat = lax.dynamic_index_in_dim(mod_all, me, axis=1, keepdims=False).reshape(nd)
    mod_ctx = mod_all[:, N_DEV, :].reshape(nd)
    mod6 = jnp.stack([mod_ctx, mod_lat]).reshape(6, 3, D)

    tabs = _rope_tables(T, Tc)

    srcs1 = (ctx[0], x[0])
    xa1, hm1, ab1, h1, f1 = ffn_forward(srcs1, mod6, 0, norm1_g, full["ffn1_w_in"], full["ffn1_w_out"], nctx, "ffn1_fwd")
    hx, proj, qr, kr = proj_forward(xa1, mod6, norm2_g, full["w_in"], q_norm_g, k_norm_g, tabs, offs, nctx, "proj_fwd")
    yc = conv_forward(proj, conv_full, offs, Tc, "conv_fwd")
    oa, lse = attention_forward(qr, kr, proj, offs, Tc, "attn_fwd")
    x2, pc, pa, mm, zz = merge_forward(xa1, mod6, yc, oa, proj, full["w_branch_conv"], full["w_branch_attn"],
                                       full["w_out"], offs, Tc, "merge_fwd")
    x3, hm2, ab2, h2, f2 = ffn_forward((x2,), mod6, 2, norm3_g, full["ffn2_w_in"], full["ffn2_w_out"], 0, "ffn2_fwd")
    dx3, lacc = loss_and_final_norm_backward(x3, loss_target[0], final_g.reshape(1, D), "loss_bwd")
    loss = lax.psum(lacc[1, 0], ("x", "y", "c"))

    grads = {}
    dx2, dab2, df2, acc_f2 = ffn_backward_rows(dx3, (x2,), mod6, 2, norm3_g, ab2, f2, full["ffn2_w_in"], full["ffn2_w_out"], 0, "ffn2_bwd")
    grads["ffn2_w_out"] = tn_ffn_out(h2, df2, "ffn2_dwout")
    grads["ffn2_w_in"] = tn_ffn_in(hm2, dab2, "ffn2_dwin")
    dproj, dgm, dpc, dpa, dyc, do, acc_mg = merge_backward_rows(dx2, mod6, zz, pc, pa, proj, full["w_branch_conv"],
                                                                full["w_branch_attn"], full["w_out"], offs, Tc, "merge_bwd")
    grads["w_out"] = tn_matmul(mm, dgm, "dw_out")
    grads["w_branch_conv"] = tn_matmul(yc, dpc, "dw_bc")
    grads["w_branch_attn"] = tn_matmul(oa, dpa, "dw_ba")
    dproj, dcw = conv_backward(dproj, dyc, proj, conv_full, offs, Tc, "conv_bwd")
    dq, dk, dv = attention_backward(qr, kr, proj, oa, lse, do, offs, Tc, "attn_bwd")
    dproj, dqg = q_norm_backward(dproj, dq, proj, q_norm_g, tabs, offs, nctx, "q_norm_bwd")
    dproj, dkg = kv_backward(dproj, dk, dv, proj, k_norm_g, tabs, offs, "kv_bwd")
    dxa1, acc_pj = proj_backward_rows(dproj, dx2, xa1, mod6, norm2_g, full["w_in"], nctx, "proj_bwd")
    grads["w_in"] = _ungroup_w_in_grad(tn_matmul(hx, dproj, "dw_in"), D, Dc, qw, kw)
    grad_x2d, dab1, df1, acc_f1 = ffn_backward_rows(dxa1, srcs1, mod6, 0, norm1_g, ab1, f1, full["ffn1_w_in"], full["ffn1_w_out"], nctx, "ffn1_bwd")
    grads["ffn1_w_out"] = tn_ffn_out(h1, df1, "ffn1_dwout")
    grads["ffn1_w_in"] = tn_ffn_in(hm1, dab1, "ffn1_dwin")
    grad_x = grad_x2d[None]

    gs = [grads[n].reshape((N_DEV, -1, grads[n].shape[-1])) for n in BIG]
    recv_a = rs_sibling_exchange(gs, "rs_sibling")
    ps = [rs_pair_sum(place, g, a, "rs_pair_sum_" + n) for n, g, a in zip(BIG, gs, recv_a)]
    recv_b = rs_chip_exchange(ps, "rs_chips")
    reduced = dict(zip(BIG, zip(ps, recv_b)))

    zero_d = jnp.zeros((D,), F32)
    dlat = jnp.concatenate([acc_f1[1, 0], acc_f1[1, 1], acc_f1[1, 2], acc_pj[1, 0], acc_pj[1, 1], acc_mg[0],
                            acc_f2[1, 0], acc_f2[1, 1], acc_f2[1, 2]])
    dctx = jnp.concatenate([acc_f1[0, 0], acc_f1[0, 1], acc_f1[0, 2], acc_pj[0, 0], acc_pj[0, 1]] + [zero_d] * 4)
    small = jnp.concatenate([acc_f1[0, 3] + acc_f1[1, 3], acc_pj[0, 2] + acc_pj[1, 2], acc_f2[1, 3],
                             dqg[0], dkg[0], lacc[0], dcw[0:CONV_TAPS].reshape(-1)])
    n_small = small.shape[0]
    pay_b = jnp.concatenate([dlat, dctx, small]).reshape(1, -1)
    gath = allgather_direct(pay_b, "ag_small_grads")
    dlat_loc = lax.dynamic_slice_in_dim(gath[:, 0, :nd], me * mod_cols, mod_cols, axis=1)
    dctx_loc = lax.dynamic_slice_in_dim(gath[:, 0, nd:2 * nd], me * mod_cols, mod_cols, axis=1)
    g_wmod, pc_part, small_sum = mod_backward(call, cctx2, w_mod[0], dlat_loc, dctx_loc, gath, 2 * nd, n_small, "mod_bwd")
    pcs = allgather_direct(pc_part, "ag_cctx")
    g_bmod, g_cctx = bmod_and_cctx_grad(gath, pcs, cctx2, nd, "small_bwd")
    sm = small_sum[0]
    g_conv_full = sm[3 * D + 2 * HEAD_DIM + D:].reshape(CONV_TAPS, Dc)
    g_conv = lax.dynamic_slice_in_dim(g_conv_full, me * cw_loc.shape[1], cw_loc.shape[1], axis=1)
    gsmall = dict(
        c_ctx=g_cctx, w_mod=g_wmod, b_mod=g_bmod, norm1_g=sm[0:D][None], norm2_g=sm[D:2 * D][None],
        norm3_g=sm[2 * D:3 * D][None], q_norm_g=sm[3 * D:3 * D + HEAD_DIM][None],
        k_norm_g=sm[3 * D + HEAD_DIM:3 * D + 2 * HEAD_DIM][None],
        final_g=sm[3 * D + 2 * HEAD_DIM:3 * D + 2 * HEAD_DIM + D][None], conv_w=g_conv)

    g_out, d_out, m_out, v_out = [], [], [], []
    for n in order:
        w = weights[n]
        shp = w.shape
        two_d = (lambda a: a.reshape(-1, shp[-1]))
        m, v = moms[n]
        if n in reduced:
            g2, d, nm, nv = adamw_reduced(place, *reduced[n], two_d(w), two_d(m), two_d(v), "adamw_" + n)
        else:
            g2 = gsmall[n].reshape(two_d(w).shape)
            d, nm, nv = adamw(two_d(w), g2, two_d(m), two_d(v), "adamw_" + n)
        g_out.append(g2.reshape(shp))
        d_out.append(d.reshape(shp))
        m_out.append(nm.reshape(shp))
        v_out.append(nv.reshape(shp))
    return (loss, grad_x, *g_out, *d_out, *m_out, *v_out)
```

```python
import math

import jax
import jax.numpy as jnp
from jax import lax
from jax.experimental import pallas as pl
from jax.experimental.pallas import tpu as pltpu

F32 = jnp.float32
BF = jnp.bfloat16
EPS = 1e-6
N_DEV = 8
HEAD_DIM = 128
N_Q_HEADS = 8
N_KV_HEADS = 2
GROUP = N_Q_HEADS // N_KV_HEADS
GRID_W = 64
ROPE_THETA = 10000.0
CONV_TAPS = 3
N_MOD = 9
ADAM_LR = 0.001
ADAM_B1 = 0.9
ADAM_B2 = 0.999
ADAM_EPS = 1e-08
ADAM_WD = 0.01
ADAM_STEP = 10
ROW_TILE = 256
VMEM_BIG = 56 << 20
MESH_ID = pl.DeviceIdType.MESH
HIGHEST = lax.Precision.HIGHEST
NT = (((1,), (1,)), ((), ()))
TN = (((0,), (0,)), ((), ()))
LOG2E = math.log2(math.e)


def _pick(n, cands):
    for c in cands:
        if n % c == 0:
            return c
    return n


def _params(vmem=None, sem=None):
    kw = {}
    if vmem is not None:
        kw["vmem_limit_bytes"] = vmem
    if sem is not None:
        kw["dimension_semantics"] = sem
    return pltpu.CompilerParams(**kw)


def _resident(shape):
    nd = len(shape)
    return pl.BlockSpec(shape, lambda *_: (0,) * nd, pipeline_mode=pl.Buffered(1))


def _sigmoid(x):
    return jax.nn.sigmoid(x)


ANY = pl.BlockSpec(memory_space=pl.ANY)


def _coords():
    return lax.axis_index("x"), lax.axis_index("y"), lax.axis_index("c")


def _flip(v, bit):
    return 1 - v if bit else v


def _remote(src, dst, ssem, rsem, dev):
    return pltpu.make_async_remote_copy(src_ref=src, dst_ref=dst, send_sem=ssem, recv_sem=rsem,
                                        device_id=dev, device_id_type=MESH_ID)


def allgather_direct(v, name):
    def body(v_ref, out_ref, ssem, rsem, lsem):
        x, y, c = _coords()
        me = 4 * x + 2 * y + c
        mine = pltpu.make_async_copy(v_ref, out_ref.at[me], lsem)
        mine.start()
        cps = []
        for p in range(1, N_DEV):
            px, py, pc = (p >> 2) & 1, (p >> 1) & 1, p & 1
            cps.append(_remote(v_ref, out_ref.at[me], ssem.at[p - 1], rsem.at[p - 1],
                               (_flip(x, px), _flip(y, py), _flip(c, pc))))
        for cp in cps:
            cp.start()
        for p in range(1, N_DEV):
            px, py, pc = (p >> 2) & 1, (p >> 1) & 1, p & 1
            src = 4 * _flip(x, px) + 2 * _flip(y, py) + _flip(c, pc)
            _remote(v_ref, out_ref.at[src], ssem.at[p - 1], rsem.at[p - 1], (x, y, c)).wait_recv()
        for cp in cps:
            cp.wait_send()
        mine.wait()

    return pl.pallas_call(
        body, name=name,
        out_shape=jax.ShapeDtypeStruct((N_DEV,) + v.shape, v.dtype),
        in_specs=[ANY], out_specs=ANY,
        scratch_shapes=[pltpu.SemaphoreType.DMA((N_DEV - 1,)), pltpu.SemaphoreType.DMA((N_DEV - 1,)),
                        pltpu.SemaphoreType.DMA],
    )(v)


def allgather_two_level(shards, name):
    n = len(shards)

    def body(*refs):
        v_refs, out_refs, (ssem, rsem, lsem) = refs[:n], refs[n:2 * n], refs[2 * n:]
        x, y, c = _coords()
        me = (x, y, c)
        sib = (x, y, 1 - c)
        chips = [(1 - x, y), (x, 1 - y), (1 - x, 1 - y)]

        def slot(w, px, py, pc):
            return out_refs[w].at[4 * px + 2 * py + pc]

        def sem(w, k):
            return ssem.at[7 * w + k], rsem.at[7 * w + k]

        mine = [pltpu.make_async_copy(v_refs[w], slot(w, *me), lsem.at[w]) for w in range(n)]
        for cp in mine:
            cp.start()
        first = []
        for w in range(n):
            first.append(_remote(v_refs[w], slot(w, *me), *sem(w, 0), sib))
            first += [_remote(v_refs[w], slot(w, *me), *sem(w, 1 + j), (*chip, c)) for j, chip in enumerate(chips)]
        for cp in first:
            cp.start()
        passed = []
        for w in range(n):
            for j, chip in enumerate(chips):
                _remote(v_refs[w], slot(w, *chip, c), *sem(w, 1 + j), me).wait_recv()
                cp = _remote(slot(w, *chip, c), slot(w, *chip, c), *sem(w, 4 + j), sib)
                cp.start()
                passed.append(cp)
        for w in range(n):
            _remote(v_refs[w], slot(w, x, y, 1 - c), *sem(w, 0), me).wait_recv()
            for j, chip in enumerate(chips):
                _remote(v_refs[w], slot(w, *chip, 1 - c), *sem(w, 4 + j), me).wait_recv()
        for cp in first + passed:
            cp.wait_send()
        for cp in mine:
            cp.wait()

    return pl.pallas_call(
        body, name=name,
        out_shape=[jax.ShapeDtypeStruct((N_DEV,) + s.shape, s.dtype) for s in shards],
        in_specs=[ANY] * n, out_specs=[ANY] * n,
        scratch_shapes=[pltpu.SemaphoreType.DMA((7 * n,)), pltpu.SemaphoreType.DMA((7 * n,)),
                        pltpu.SemaphoreType.DMA((n,))],
    )(*shards)


def rs_sibling_exchange(gs, name):
    n = len(gs)

    def body(*refs):
        g_refs, out_refs, (ssem, rsem) = refs[:n], refs[n:2 * n], refs[2 * n:]
        x, y, c = _coords()
        sib = (x, y, 1 - c)
        cps = [_remote(g_refs[w].at[2 * k + (1 - c)], out_refs[w].at[k], ssem.at[4 * w + k], rsem.at[4 * w + k], sib)
               for w in range(n) for k in range(4)]
        for cp in cps:
            cp.start()
        for cp in cps:
            cp.wait()

    return pl.pallas_call(
        body, name=name,
        out_shape=[jax.ShapeDtypeStruct((4,) + g.shape[1:], g.dtype) for g in gs],
        in_specs=[ANY] * n, out_specs=[ANY] * n,
        scratch_shapes=[pltpu.SemaphoreType.DMA((4 * n,)), pltpu.SemaphoreType.DMA((4 * n,))],
    )(*gs)


def rs_chip_exchange(ps, name):
    n = len(ps)

    def body(*refs):
        p_refs, out_refs, (ssem, rsem) = refs[:n], refs[n:2 * n], refs[2 * n:]
        x, y, c = _coords()
        chips = [(1 - x, y), (x, 1 - y), (1 - x, 1 - y)]
        cps = [_remote(p_refs[w].at[2 * cx + cy], out_refs[w].at[j], ssem.at[3 * w + j], rsem.at[3 * w + j], (cx, cy, c))
               for w in range(n) for j, (cx, cy) in enumerate(chips)]
        for cp in cps:
            cp.start()
        for cp in cps:
            cp.wait()

    return pl.pallas_call(
        body, name=name,
        out_shape=[jax.ShapeDtypeStruct((3,) + p.shape[1:], p.dtype) for p in ps],
        in_specs=[ANY] * n, out_specs=[ANY] * n,
        scratch_shapes=[pltpu.SemaphoreType.DMA((3 * n,)), pltpu.SemaphoreType.DMA((3 * n,))],
    )(*ps)


class Hosted:
    def __init__(self, gathers=(), scatters=()):
        self.items = [(a, False) for a in gathers] + [(a, True) for a in scatters]
        self.n = len(self.items)
        self.operands = [a for a, _ in self.items]
        self.out_shapes = [jax.ShapeDtypeStruct(a.shape if sc else (N_DEV,) + a.shape, a.dtype) for a, sc in self.items]
        self.scratch = [pltpu.SemaphoreType.DMA((7 * self.n,)), pltpu.SemaphoreType.DMA((7 * self.n,)),
                        pltpu.SemaphoreType.DMA((self.n,))]

    def _copies(self, in_refs, out_refs, ssem, rsem, lsem):
        x, y, c = _coords()
        me = 4 * x + 2 * y + c
        sends, recvs, local = [], [], []
        for w, (_, sc) in enumerate(self.items):
            src, dst = in_refs[w], out_refs[w]
            local.append(pltpu.make_async_copy(src.at[me] if sc else src, dst.at[me], lsem.at[w]))
            for p in range(1, N_DEV):
                px, py, pc = _flip(x, (p >> 2) & 1), _flip(y, (p >> 1) & 1), _flip(c, p & 1)
                peer = 4 * px + 2 * py + pc
                k = 7 * w + p - 1
                sends.append(_remote(src.at[peer] if sc else src, dst.at[me], ssem.at[k], rsem.at[k], (px, py, pc)))
                recvs.append(_remote(src.at[me] if sc else src, dst.at[peer], ssem.at[k], rsem.at[k], (x, y, c)))
        return sends, recvs, local

    def start(self, in_refs, out_refs, ssem, rsem, lsem):
        sends, _, local = self._copies(in_refs, out_refs, ssem, rsem, lsem)
        for cp in local + sends:
            cp.start()

    def wait(self, in_refs, out_refs, ssem, rsem, lsem):
        sends, recvs, local = self._copies(in_refs, out_refs, ssem, rsem, lsem)
        for cp in recvs:
            cp.wait_recv()
        for cp in sends:
            cp.wait_send()
        for cp in local:
            cp.wait()


def _call(body, *, name, grid, in_specs, out_specs, out_shape, operands, params, scratch_shapes=(), aliases=None, hosted=None):
    n_in, n_out, n_scr = len(in_specs), len(out_specs), len(scratch_shapes)
    h = hosted.n if hosted is not None else 0

    def wrapped(*refs):
        ins, cins = refs[:n_in], refs[n_in:n_in + h]
        outs, couts = refs[n_in + h:n_in + h + n_out], refs[n_in + h + n_out:n_in + 2 * h + n_out]
        rest = refs[n_in + 2 * h + n_out:]
        scr, sems = rest[:n_scr], rest[n_scr:]
        if h:
            ids = [pl.program_id(a) for a in range(len(grid))]
            first, last = ids[0] == 0, ids[0] == grid[0] - 1
            for a in range(1, len(grid)):
                first, last = first & (ids[a] == 0), last & (ids[a] == grid[a] - 1)

            @pl.when(first)
            def _():
                hosted.start(cins, couts, *sems)

        body(*ins, *outs, *scr)
        if h:
            @pl.when(last)
            def _():
                hosted.wait(cins, couts, *sems)

    res = pl.pallas_call(
        wrapped, name=name, grid=grid,
        in_specs=list(in_specs) + [ANY] * h, out_specs=list(out_specs) + [ANY] * h,
        out_shape=list(out_shape) + (hosted.out_shapes if h else []),
        scratch_shapes=list(scratch_shapes) + (hosted.scratch if h else []),
        input_output_aliases=aliases or {}, compiler_params=params,
    )(*operands, *(hosted.operands if h else []))
    return list(res[:n_out]), list(res[n_out:])


def _row_tile(R, C):
    return _pick(R, (256, 128, 64, 32, 16)) if R * C > (1 << 18) else R


def rs_pair_sum(place, gs, recv_a, name):
    _, R, C = gs.shape
    tr = _row_tile(R, C)

    def body(place_ref, g_ref, a_ref, o_ref):
        o_ref[0] = (g_ref[0].astype(F32) + a_ref[0].astype(F32)).astype(o_ref.dtype)

    return pl.pallas_call(
        body, name=name,
        grid_spec=pltpu.PrefetchScalarGridSpec(
            num_scalar_prefetch=1, grid=(4, R // tr),
            in_specs=[pl.BlockSpec((1, tr, C), lambda k, r, pr: (2 * k + pr[0], r, 0)),
                      pl.BlockSpec((1, tr, C), lambda k, r, pr: (k, r, 0))],
            out_specs=pl.BlockSpec((1, tr, C), lambda k, r, pr: (k, r, 0))),
        out_shape=jax.ShapeDtypeStruct((4, R, C), BF),
    )(place, gs, recv_a)


def _cond_rows(call_ref, cctx_ref, z_ref, D):
    z_ref[...] = jnp.zeros_like(z_ref)
    for a in range(N_DEV):
        z_ref[a:a + 1, :] = call_ref[a][:, :D]
    z_ref[N_DEV:N_DEV + 1, :] = cctx_ref[...]


def mod_forward(call, c_ctx, w_loc, b_loc, name):
    D, cols = w_loc.shape

    def body(call_ref, cctx_ref, w_ref, b_ref, o_ref, z_ref):
        _cond_rows(call_ref, cctx_ref, z_ref, D)
        z = z_ref[...]
        s = z * _sigmoid(z)
        o_ref[...] = jnp.dot(s, w_ref[...], preferred_element_type=F32, precision=HIGHEST) + b_ref[...]

    return pl.pallas_call(
        body, name=name, out_shape=jax.ShapeDtypeStruct((16, cols), F32),
        scratch_shapes=[pltpu.VMEM((16, D), F32)],
        compiler_params=_params(vmem=VMEM_BIG),
    )(call, c_ctx, w_loc, b_loc)


def mod_backward(call, c_ctx, w_loc, dlat_loc, dctx_loc, gath, n_small_off, n_small, name):
    D, cols = w_loc.shape

    def body(call_ref, cctx_ref, w_ref, dlat_ref, dctx_ref, g_ref, gw_ref, pc_ref, small_ref, z_ref, dm_ref):
        _cond_rows(call_ref, cctx_ref, z_ref, D)
        z = z_ref[...]
        s = z * _sigmoid(z)
        dctx = dctx_ref[0:1, :]
        for a in range(1, N_DEV):
            dctx = dctx + dctx_ref[a:a + 1, :]
        dm_ref[...] = jnp.zeros_like(dm_ref)
        dm_ref[0:N_DEV, :] = dlat_ref[...]
        dm_ref[N_DEV:N_DEV + 1, :] = dctx
        gw_ref[...] = lax.dot_general(s, dm_ref[...], TN, preferred_element_type=F32, precision=HIGHEST)
        pc_ref[...] = lax.dot_general(dctx, w_ref[...], NT, preferred_element_type=F32, precision=HIGHEST)
        acc = g_ref[0][:, n_small_off:n_small_off + n_small]
        for a in range(1, N_DEV):
            acc = acc + g_ref[a][:, n_small_off:n_small_off + n_small]
        small_ref[...] = acc

    return pl.pallas_call(
        body, name=name,
        out_shape=(jax.ShapeDtypeStruct((D, cols), F32), jax.ShapeDtypeStruct((1, D), F32),
                   jax.ShapeDtypeStruct((1, n_small), F32)),
        scratch_shapes=[pltpu.VMEM((16, D), F32), pltpu.VMEM((16, cols), F32)],
        compiler_params=_params(vmem=VMEM_BIG),
    )(call, c_ctx, w_loc, dlat_loc, dctx_loc, gath)


def bmod_and_cctx_grad(gath, pcs, c_ctx, nd, name):
    D = c_ctx.shape[-1]

    def body(g_ref, pc_ref, cctx_ref, gb_ref, gc_ref):
        acc = g_ref[0][:, :nd] + g_ref[0][:, nd:2 * nd]
        for a in range(1, N_DEV):
            acc = acc + (g_ref[a][:, :nd] + g_ref[a][:, nd:2 * nd])
        gb_ref[...] = acc
        p = pc_ref[0]
        for a in range(1, N_DEV):
            p = p + pc_ref[a]
        z = cctx_ref[...]
        sg = _sigmoid(z)
        gc_ref[...] = p * (sg * (1.0 + z * (1.0 - sg)))

    return pl.pallas_call(
        body, name=name,
        out_shape=(jax.ShapeDtypeStruct((1, nd), F32), jax.ShapeDtypeStruct((1, D), F32)),
    )(gath, pcs, c_ctx)


def _mod_spec(D, which, nctx):
    return pl.BlockSpec((1, 3, D), lambda i: (jnp.where(i < nctx, 0, 3) + which, 0, 0))


def _acc_spec(D, nctx):
    return pl.BlockSpec((1, 8, D), lambda i: (jnp.where(i < nctx, 0, 1), 0, 0))


def _row(tm, n):
    return pl.BlockSpec((tm, n), lambda i: (i, 0))


def _stk(k, tm, n):
    return pl.BlockSpec((k, tm, n), lambda i: (0, i, 0))


def _two_stream_specs(tm, D, nctx):
    return [pl.BlockSpec((tm, D), lambda i: (jnp.minimum(i, nctx - 1), 0)),
            pl.BlockSpec((tm, D), lambda i: (jnp.maximum(i - nctx, 0), 0))]


def ffn_forward(srcs, mod6, which, g, ws, w_out, nctx, name, hosted=None):
    D = srcs[-1].shape[1]
    Tr = sum(s.shape[0] for s in srcs)
    nsh, _, nn = ws.shape
    nh = nsh // 2
    tm = ROW_TILE
    two = len(srcs) == 2

    def body(*refs):
        x_refs, (mod_ref, g_ref, ws_ref, wout_ref, xo_ref, hm_ref, ab_ref, h_ref, f_ref) = refs[:len(srcs)], refs[len(srcs):]
        x = jnp.where(pl.program_id(0) < nctx, x_refs[0][...], x_refs[1][...]) if two else x_refs[0][...]
        ms = mod_ref[0]
        shift, scale, gate = ms[0:1], ms[1:2], ms[2:3]
        r = lax.rsqrt(jnp.mean(x * x, axis=-1, keepdims=True) + EPS)
        hb = (((x * r) * g_ref[...]) * (1.0 + scale) + shift).astype(BF)
        hm_ref[...] = hb
        f = jnp.zeros((tm, D), F32)
        for j in range(nh):
            a = jnp.dot(hb, ws_ref[j], preferred_element_type=F32)
            b = jnp.dot(hb, ws_ref[nh + j], preferred_element_type=F32)
            ab_ref[j] = a.astype(BF)
            ab_ref[nh + j] = b.astype(BF)
            h = ((a * _sigmoid(a)) * b).astype(BF)
            h_ref[j] = h
            f = f + jnp.dot(h, wout_ref[j * nn:(j + 1) * nn, :], preferred_element_type=F32)
        f_ref[...] = f.astype(BF)
        xo_ref[...] = x + (0.5 * gate) * f

    src_specs = _two_stream_specs(tm, D, nctx) if two else [_row(tm, D)]
    return _call(
        body, name=name, grid=(Tr // tm,),
        in_specs=src_specs + [_mod_spec(D, which, nctx), _resident((1, D)), _resident(ws.shape), _resident(w_out.shape)],
        out_specs=[_row(tm, D), _row(tm, D), _stk(nsh, tm, nn), _stk(nh, tm, nn), _row(tm, D)],
        out_shape=[jax.ShapeDtypeStruct((Tr, D), F32), jax.ShapeDtypeStruct((Tr, D), BF),
                   jax.ShapeDtypeStruct((nsh, Tr, nn), BF), jax.ShapeDtypeStruct((nh, Tr, nn), BF),
                   jax.ShapeDtypeStruct((Tr, D), BF)],
        operands=[*srcs, mod6, g, ws, w_out], hosted=hosted,
        params=_params(vmem=VMEM_BIG, sem=("arbitrary",)))


def ffn_backward_rows(dxo, srcs, mod6, which, g, ab, fo, ws, w_out, nctx, name, hosted=None):
    D = srcs[-1].shape[1]
    Tr = sum(s.shape[0] for s in srcs)
    Tl = srcs[-1].shape[0]
    nsh, _, nn = ws.shape
    nh = nsh // 2
    tm = ROW_TILE
    two = len(srcs) == 2

    def body(*refs):
        dxo_ref, x_refs = refs[0], refs[1:1 + len(srcs)]
        mod_ref, g_ref, ab_ref, fo_ref, ws_ref, wout_ref, dx_ref, dab_ref, df_ref, acc_ref = refs[1 + len(srcs):]
        i = pl.program_id(0)

        @pl.when((i == 0) | (i == nctx))
        def _():
            acc_ref[...] = jnp.zeros_like(acc_ref)

        dxo = dxo_ref[...]
        x = jnp.where(i < nctx, x_refs[0][...], x_refs[1][...]) if two else x_refs[0][...]
        ms = mod_ref[0]
        scale, gate = ms[1:2], ms[2:3]
        gg = g_ref[...]
        dgate = jnp.sum(dxo * fo_ref[...].astype(F32), axis=0, keepdims=True) * 0.5
        dfb = (dxo * (0.5 * gate)).astype(BF)
        df_ref[...] = dfb
        dhm = jnp.zeros((tm, D), F32)
        for j in range(nh):
            dh = lax.dot_general(dfb, wout_ref[j * nn:(j + 1) * nn, :], NT, preferred_element_type=F32)
            a = ab_ref[j].astype(F32)
            b = ab_ref[nh + j].astype(F32)
            sg = _sigmoid(a)
            da = ((dh * b) * (sg * (1.0 + a * (1.0 - sg)))).astype(BF)
            db = (dh * (a * sg)).astype(BF)
            dab_ref[j] = da
            dab_ref[nh + j] = db
            dhm = dhm + lax.dot_general(da, ws_ref[j], NT, preferred_element_type=F32)
            dhm = dhm + lax.dot_general(db, ws_ref[nh + j], NT, preferred_element_type=F32)
        r = lax.rsqrt(jnp.mean(x * x, axis=-1, keepdims=True) + EPS)
        xh = x * r
        dshift = jnp.sum(dhm, axis=0, keepdims=True)
        dscale = jnp.sum(dhm * (xh * gg), axis=0, keepdims=True)
        dxh_g = dhm * (1.0 + scale)
        dg = jnp.sum(dxh_g * xh, axis=0, keepdims=True)
        dxh = dxh_g * gg
        dx_ref[...] = dxo + r * (dxh - xh * jnp.mean(dxh * xh, axis=-1, keepdims=True))
        for k, val in enumerate((dshift, dscale, dgate, dg)):
            acc_ref[0, k:k + 1, :] += val

    src_specs = _two_stream_specs(tm, D, nctx) if two else [_row(tm, D)]
    dx_spec = pl.BlockSpec((tm, D), lambda i: (jnp.maximum(i - nctx, 0), 0))
    return _call(
        body, name=name, grid=(Tr // tm,),
        in_specs=[_row(tm, D)] + src_specs + [_mod_spec(D, which, nctx), _resident((1, D)), _stk(nsh, tm, nn), _row(tm, D),
                                              _resident(ws.shape), _resident(w_out.shape)],
        out_specs=[dx_spec, _stk(nsh, tm, nn), _row(tm, D), _acc_spec(D, nctx)],
        out_shape=[jax.ShapeDtypeStruct((Tl, D), F32), jax.ShapeDtypeStruct((nsh, Tr, nn), BF),
                   jax.ShapeDtypeStruct((Tr, D), BF), jax.ShapeDtypeStruct((2, 8, D), F32)],
        operands=[dxo, *srcs, mod6, g, ab, fo, ws, w_out], hosted=hosted,
        params=_params(vmem=VMEM_BIG, sem=("arbitrary",)))


def _token_tile(T):
    return _pick(T, (1024, 768, 512, 256, 128))


def tn_matmul(a, b, name):
    T, K = a.shape
    N = b.shape[1]
    tk = _pick(K, (1024, 1408, 768, 512, 384, 256, 128))
    tn = _pick(N, (1024, 1408, 1664, 768, 512, 384, 256, 128))
    tt = _token_tile(T)
    nt = T // tt

    def body(a_ref, b_ref, o_ref, acc_ref):
        t = pl.program_id(2)

        @pl.when(t == 0)
        def _():
            acc_ref[...] = jnp.zeros_like(acc_ref)

        acc_ref[...] += lax.dot_general(a_ref[...], b_ref[...], TN, preferred_element_type=F32)

        @pl.when(t == nt - 1)
        def _():
            o_ref[...] = acc_ref[...].astype(BF)

    return pl.pallas_call(
        body, name=name, grid=(K // tk, N // tn, nt),
        in_specs=[pl.BlockSpec((tt, tk), lambda k, n, t: (t, k)), pl.BlockSpec((tt, tn), lambda k, n, t: (t, n))],
        out_specs=pl.BlockSpec((tk, tn), lambda k, n, t: (k, n)),
        out_shape=jax.ShapeDtypeStruct((K, N), BF),
        scratch_shapes=[pltpu.VMEM((tk, tn), F32)],
        compiler_params=_params(vmem=VMEM_BIG, sem=("parallel", "parallel", "arbitrary")),
    )(a, b)


def tn_ffn_in(hm, dab, name):
    T, D = hm.shape
    nsh, _, nn = dab.shape
    per = nsh // 2
    tt = _token_tile(T)
    nt = T // tt

    def body(a_ref, b_ref, o_ref, acc_ref):
        t = pl.program_id(1)

        @pl.when(t == 0)
        def _():
            acc_ref[...] = jnp.zeros_like(acc_ref)

        a = a_ref[...]
        for j in range(per):
            acc_ref[j] += lax.dot_general(a, b_ref[j], TN, preferred_element_type=F32)

        @pl.when(t == nt - 1)
        def _():
            o_ref[...] = acc_ref[...].astype(BF)

    return pl.pallas_call(
        body, name=name, grid=(2, nt),
        in_specs=[pl.BlockSpec((tt, D), lambda s, t: (t, 0)), pl.BlockSpec((per, tt, nn), lambda s, t: (s, t, 0))],
        out_specs=pl.BlockSpec((per, D, nn), lambda s, t: (s, 0, 0)),
        out_shape=jax.ShapeDtypeStruct((nsh, D, nn), BF),
        scratch_shapes=[pltpu.VMEM((per, D, nn), F32)],
        compiler_params=_params(vmem=VMEM_BIG, sem=("parallel", "arbitrary")),
    )(hm, dab)


def tn_ffn_out(h, df, name):
    nh, T, nn = h.shape
    D = df.shape[1]
    per = nh // 2
    tt = _token_tile(T)
    nt = T // tt

    def body(a_ref, b_ref, o_ref, acc_ref):
        t = pl.program_id(1)

        @pl.when(t == 0)
        def _():
            acc_ref[...] = jnp.zeros_like(acc_ref)

        b = b_ref[...]
        for j in range(per):
            acc_ref[j] += lax.dot_general(a_ref[j], b, TN, preferred_element_type=F32)

        @pl.when(t == nt - 1)
        def _():
            for j in range(per):
                o_ref[j * nn:(j + 1) * nn, :] = acc_ref[j].astype(BF)

    return pl.pallas_call(
        body, name=name, grid=(2, nt),
        in_specs=[pl.BlockSpec((per, tt, nn), lambda s, t: (s, t, 0)), pl.BlockSpec((tt, D), lambda s, t: (t, 0))],
        out_specs=pl.BlockSpec((per * nn, D), lambda s, t: (s, 0)),
        out_shape=jax.ShapeDtypeStruct((nh * nn, D), BF),
        scratch_shapes=[pltpu.VMEM((per, nn, D), F32)],
        compiler_params=_params(vmem=VMEM_BIG, sem=("parallel", "arbitrary")),
    )(h, df)


def _rope_apply(y, cos, s_next, s_prev):
    return y * cos + pltpu.roll(y, HEAD_DIM - 32, 1) * s_next + pltpu.roll(y, 32, 1) * s_prev


def _rope_transpose(dz, cos, s_next, s_prev):
    return dz * cos + pltpu.roll(dz * s_next, 32, 1) + pltpu.roll(dz * s_prev, HEAD_DIM - 32, 1)


def proj_forward(xa, mod6, g, w_in, qg, kg, tabs, offs, nctx, name, hosted=None):
    Tr, D = xa.shape
    P = w_in.shape[1]
    tm = ROW_TILE
    qo, ko = offs["q"], offs["k"]
    qw, kw = N_Q_HEADS * HEAD_DIM, N_KV_HEADS * HEAD_DIM
    scale_q = HEAD_DIM ** -0.5 * LOG2E

    def body(x_ref, mod_ref, g_ref, w_ref, qg_ref, kg_ref, tab_ref, hx_ref, pr_ref, q_ref, k_ref):
        x = x_ref[...]
        ms = mod_ref[0]
        shift, scale = ms[0:1], ms[1:2]
        r = lax.rsqrt(jnp.mean(x * x, axis=-1, keepdims=True) + EPS)
        hb = (((x * r) * g_ref[...]) * (1.0 + scale) + shift).astype(BF)
        hx_ref[...] = hb
        pr = jnp.dot(hb, w_ref[...], preferred_element_type=F32)
        pr_ref[...] = pr.astype(BF)
        cos, s_next, s_prev = tab_ref[0], tab_ref[1], tab_ref[2]

        def head(v, gain):
            n = v * lax.rsqrt(jnp.mean(v * v, axis=-1, keepdims=True) + EPS)
            return _rope_apply(n * gain, cos, s_next, s_prev)

        for h in range(N_Q_HEADS):
            lo = qo + h * HEAD_DIM
            q_ref[:, h * HEAD_DIM:(h + 1) * HEAD_DIM] = (head(pr[:, lo:lo + HEAD_DIM], qg_ref[...]) * scale_q).astype(BF)
        for h in range(N_KV_HEADS):
            lo = ko + h * HEAD_DIM
            k_ref[:, h * HEAD_DIM:(h + 1) * HEAD_DIM] = head(pr[:, lo:lo + HEAD_DIM], kg_ref[...]).astype(BF)

    return _call(
        body, name=name, grid=(Tr // tm,),
        in_specs=[_row(tm, D), _mod_spec(D, 1, nctx), _resident((1, D)), _resident(w_in.shape),
                  _resident((1, HEAD_DIM)), _resident((1, HEAD_DIM)),
                  pl.BlockSpec((3, tm, HEAD_DIM), lambda i: (0, i, 0))],
        out_specs=[_row(tm, D), _row(tm, P), _row(tm, qw), _row(tm, kw)],
        out_shape=[jax.ShapeDtypeStruct((Tr, D), BF), jax.ShapeDtypeStruct((Tr, P), BF),
                   jax.ShapeDtypeStruct((Tr, qw), BF), jax.ShapeDtypeStruct((Tr, kw), BF)],
        operands=[xa, mod6, g, w_in, qg, kg, tabs], hosted=hosted,
        params=_params(vmem=VMEM_BIG, sem=("arbitrary",)))


def _shifted(u, first_row, last_row):
    T = u.shape[0]
    prev = jnp.where(first_row, 0.0, pltpu.roll(u, 1, 0))
    nxt = jnp.where(last_row, 0.0, pltpu.roll(u, T - 1, 0))
    return prev, nxt


def conv_forward(proj, conv_w, offs, Tc, name):
    Ta = proj.shape[0]
    T = Ta - Tc
    Dc = conv_w.shape[1]
    cb = offs["cv"] // 384

    def body(p_ref, w_ref, y_ref):
        rows = lax.broadcasted_iota(jnp.int32, (T, 128), 0)
        u = p_ref[pl.ds(Tc, T), 128:256].astype(F32) * p_ref[pl.ds(Tc, T), 256:384].astype(F32)
        prev, nxt = _shifted(u, rows == 0, rows == T - 1)
        w = w_ref[...]
        cv = prev * w[0:1] + u * w[1:2] + nxt * w[2:3]
        y_ref[...] = (p_ref[pl.ds(Tc, T), 0:128].astype(F32) * cv).astype(BF)

    return pl.pallas_call(
        body, name=name, grid=(Dc // 128,),
        in_specs=[pl.BlockSpec((Ta, 384), lambda j: (0, cb + j)), pl.BlockSpec((CONV_TAPS, 128), lambda j: (0, j))],
        out_specs=pl.BlockSpec((T, 128), lambda j: (0, j)),
        out_shape=jax.ShapeDtypeStruct((T, Dc), BF),
        compiler_params=_params(vmem=VMEM_BIG, sem=("arbitrary",)),
    )(proj, conv_w)


def conv_backward(dproj, dy, proj, conv_w, offs, Tc, name):
    Ta = proj.shape[0]
    T = Ta - Tc
    Dc = conv_w.shape[1]
    cb = offs["cv"] // 384

    def body(dp_any, dy_ref, p_ref, w_ref, o_ref, dw_ref):
        rows = lax.broadcasted_iota(jnp.int32, (T, 128), 0)
        first, last = rows == 0, rows == T - 1
        bg = p_ref[pl.ds(Tc, T), 0:128].astype(F32)
        cg = p_ref[pl.ds(Tc, T), 128:256].astype(F32)
        vc = p_ref[pl.ds(Tc, T), 256:384].astype(F32)
        dy = dy_ref[...].astype(F32)
        u = cg * vc
        prev, nxt = _shifted(u, first, last)
        w = w_ref[...]
        cv = prev * w[0:1] + u * w[1:2] + nxt * w[2:3]
        o_ref[pl.ds(0, Tc), :] = jnp.zeros((Tc, 384), BF)
        o_ref[pl.ds(Tc, T), 0:128] = (dy * cv).astype(BF)
        dcv = dy * bg
        dprev, dnxt = _shifted(dcv, first, last)
        du = dnxt * w[0:1] + dcv * w[1:2] + dprev * w[2:3]
        o_ref[pl.ds(Tc, T), 128:256] = (du * vc).astype(BF)
        o_ref[pl.ds(Tc, T), 256:384] = (du * cg).astype(BF)
        dw_ref[...] = jnp.zeros_like(dw_ref)
        for k, tap in enumerate((prev, u, nxt)):
            dw_ref[k:k + 1, :] = jnp.sum(dcv * tap, axis=0, keepdims=True)

    blk = pl.BlockSpec((Ta, 384), lambda j: (0, cb + j))
    return pl.pallas_call(
        body, name=name, grid=(Dc // 128,),
        in_specs=[ANY, pl.BlockSpec((T, 128), lambda j: (0, j)), blk, pl.BlockSpec((CONV_TAPS, 128), lambda j: (0, j))],
        out_specs=[blk, pl.BlockSpec((8, 128), lambda j: (0, j))],
        out_shape=[jax.ShapeDtypeStruct(dproj.shape, BF), jax.ShapeDtypeStruct((8, Dc), F32)],
        input_output_aliases={0: 0},
        compiler_params=_params(vmem=VMEM_BIG, sem=("arbitrary",)),
    )(dproj, dy, proj, conv_w)


def _kv_chunk(Ta):
    return _pick(Ta, (768, 512, 384, 256, 128))


def _stack_heads(v):
    return jnp.concatenate([v[:, h * HEAD_DIM:(h + 1) * HEAD_DIM] for h in range(GROUP)], axis=0)


def attention_forward(q, k, proj, offs, Tc, name, hosted=None):
    Ta = k.shape[0]
    T = Ta - Tc
    tq = ROW_TILE
    kc = _kv_chunk(Ta)
    nkv = Ta // kc
    gw = GROUP * HEAD_DIM
    vblk = offs["v"] // HEAD_DIM
    qoff = Tc // tq
    n = GROUP * tq

    def body(q_ref, k_ref, v_ref, o_ref, lse_ref, vx_ref, qs_ref, s0_ref, s1_ref, m_ref, acc_ref):
        @pl.when(pl.program_id(1) == 0)
        def _():
            vx_ref[:, 0:HEAD_DIM] = v_ref[...]
            vx_ref[:, HEAD_DIM:2 * HEAD_DIM] = jnp.ones((Ta, HEAD_DIM), BF)

        qs_ref[...] = _stack_heads(q_ref[...])
        m_ref[...] = jnp.full((n, 1), -1e30, F32)
        acc_ref[...] = jnp.zeros((n, 2 * HEAD_DIM), F32)

        def rows(c):
            return pl.ds(pl.multiple_of(c * kc, kc), kc)

        def logits(c, dst):
            dst[...] = lax.dot_general(qs_ref[...], k_ref[rows(c), :], NT, preferred_element_type=F32)

        def consume(src, c):
            s = src[...]
            m_prev = m_ref[...]
            m_new = jnp.maximum(m_prev, jnp.max(s, axis=-1, keepdims=True))
            p = jnp.exp2(s - m_new).astype(BF)
            acc_ref[...] = jnp.exp2(m_prev - m_new) * acc_ref[...] + jnp.dot(p, vx_ref[rows(c), :], preferred_element_type=F32)
            m_ref[...] = m_new

        def pair(i, carry):
            logits(2 * i + 1, s1_ref)
            consume(s0_ref, 2 * i)
            logits(2 * i + 2, s0_ref)
            consume(s1_ref, 2 * i + 1)
            return carry

        logits(0, s0_ref)
        if nkv % 2:
            lax.fori_loop(0, nkv // 2, pair, 0)
            consume(s0_ref, nkv - 1)
        else:
            lax.fori_loop(0, nkv // 2 - 1, pair, 0)
            logits(nkv - 1, s1_ref)
            consume(s0_ref, nkv - 2)
            consume(s1_ref, nkv - 1)
        acc = acc_ref[...]
        l = acc[:, HEAD_DIM:HEAD_DIM + 1]
        o = acc[:, 0:HEAD_DIM] / l
        lse = m_ref[...] + jnp.log2(l)
        for h in range(GROUP):
            o_ref[:, h * HEAD_DIM:(h + 1) * HEAD_DIM] = o[h * tq:(h + 1) * tq].astype(BF)
            lse_ref[0, :, h:h + 1] = lse[h * tq:(h + 1) * tq]

    return _call(
        body, name=name, grid=(N_KV_HEADS, T // tq),
        in_specs=[pl.BlockSpec((tq, gw), lambda j, i: (i + qoff, j)),
                  pl.BlockSpec((Ta, HEAD_DIM), lambda j, i: (0, j)),
                  pl.BlockSpec((Ta, HEAD_DIM), lambda j, i: (0, vblk + j))],
        out_specs=[pl.BlockSpec((tq, gw), lambda j, i: (i, j)),
                   pl.BlockSpec((1, tq, GROUP), lambda j, i: (j, i, 0))],
        out_shape=[jax.ShapeDtypeStruct((T, N_Q_HEADS * HEAD_DIM), BF),
                   jax.ShapeDtypeStruct((N_KV_HEADS, T, GROUP), F32)],
        scratch_shapes=[pltpu.VMEM((Ta, 2 * HEAD_DIM), BF), pltpu.VMEM((n, HEAD_DIM), BF), pltpu.VMEM((n, kc), F32),
                        pltpu.VMEM((n, kc), F32), pltpu.VMEM((n, 1), F32), pltpu.VMEM((n, 2 * HEAD_DIM), F32)],
        operands=[q, k, proj], hosted=hosted,
        params=_params(vmem=VMEM_BIG, sem=("arbitrary", "arbitrary")))


def attention_backward(q, k, proj, o, lse, do, offs, Tc, name, hosted=None):
    Ta = k.shape[0]
    T = Ta - Tc
    tq = ROW_TILE
    kc = _kv_chunk(Ta)
    gw = GROUP * HEAD_DIM
    vblk = offs["v"] // HEAD_DIM
    qoff = Tc // tq

    def body(q_ref, k_ref, v_ref, o_ref, lse_ref, do_ref, dq_ref, dk_ref, dv_ref):
        @pl.when(pl.program_id(1) == 0)
        def _():
            dk_ref[...] = jnp.zeros_like(dk_ref)
            dv_ref[...] = jnp.zeros_like(dv_ref)

        qs = _stack_heads(q_ref[...])
        dob = do_ref[...]
        dos = _stack_heads(dob)
        delta = jnp.concatenate(
            [jnp.sum(dob[:, h * HEAD_DIM:(h + 1) * HEAD_DIM].astype(F32)
                     * o_ref[:, h * HEAD_DIM:(h + 1) * HEAD_DIM].astype(F32), axis=-1, keepdims=True)
             for h in range(GROUP)], axis=0)
        lse = jnp.concatenate([lse_ref[0, :, h:h + 1] for h in range(GROUP)], axis=0)

        def step(c, dq):
            rows = pl.ds(pl.multiple_of(c * kc, kc), kc)
            kk = k_ref[rows, :]
            vv = v_ref[rows, :]
            s = lax.dot_general(qs, kk, NT, preferred_element_type=F32)
            p = jnp.exp2(s - lse)
            dp = lax.dot_general(dos, vv, NT, preferred_element_type=F32)
            ds = (p * (dp - delta)).astype(BF)
            dv_ref[rows, :] += lax.dot_general(p.astype(BF), dos, TN, preferred_element_type=F32)
            dk_ref[rows, :] += lax.dot_general(ds, qs, TN, preferred_element_type=F32)
            return dq + jnp.dot(ds, kk, preferred_element_type=F32)

        dq = lax.fori_loop(0, Ta // kc, step, jnp.zeros((GROUP * tq, HEAD_DIM), F32))
        for h in range(GROUP):
            dq_ref[:, h * HEAD_DIM:(h + 1) * HEAD_DIM] = dq[h * tq:(h + 1) * tq]

    return _call(
        body, name=name, grid=(N_KV_HEADS, T // tq),
        in_specs=[pl.BlockSpec((tq, gw), lambda j, i: (i + qoff, j)),
                  pl.BlockSpec((Ta, HEAD_DIM), lambda j, i: (0, j)),
                  pl.BlockSpec((Ta, HEAD_DIM), lambda j, i: (0, vblk + j)),
                  pl.BlockSpec((tq, gw), lambda j, i: (i, j)),
                  pl.BlockSpec((1, tq, GROUP), lambda j, i: (j, i, 0)),
                  pl.BlockSpec((tq, gw), lambda j, i: (i, j))],
        out_specs=[pl.BlockSpec((tq, gw), lambda j, i: (i, j)),
                   pl.BlockSpec((Ta, HEAD_DIM), lambda j, i: (0, j)),
                   pl.BlockSpec((Ta, HEAD_DIM), lambda j, i: (0, j))],
        out_shape=[jax.ShapeDtypeStruct((T, N_Q_HEADS * HEAD_DIM), F32),
                   jax.ShapeDtypeStruct((Ta, N_KV_HEADS * HEAD_DIM), F32),
                   jax.ShapeDtypeStruct((Ta, N_KV_HEADS * HEAD_DIM), F32)],
        operands=[q, k, proj, o, lse, do], hosted=hosted,
        params=_params(vmem=VMEM_BIG, sem=("arbitrary", "arbitrary")))


def _norm_rope_backward(dz, raw, gg, cos, s_next, s_prev):
    r = lax.rsqrt(jnp.mean(raw * raw, axis=-1, keepdims=True) + EPS)
    n = raw * r
    dy = _rope_transpose(dz, cos, s_next, s_prev)
    dn = dy * gg
    return r * (dn - n * jnp.mean(dn * n, axis=-1, keepdims=True)), jnp.sum(dy * n, axis=0, keepdims=True)


def q_norm_backward(dproj, dq, proj, gain, tabs, offs, nctx, name):
    Ta = proj.shape[0]
    tm = ROW_TILE
    w = N_Q_HEADS * HEAD_DIM
    cb = offs["q"] // w
    zscale = HEAD_DIM ** -0.5

    def body(dp_any, dz_ref, raw_ref, g_ref, tab_ref, o_ref, dg_ref):
        i = pl.program_id(0)

        @pl.when(i == 0)
        def _():
            dg_ref[...] = jnp.zeros_like(dg_ref)

        @pl.when(i < nctx)
        def _():
            o_ref[...] = jnp.zeros_like(o_ref)

        @pl.when(i >= nctx)
        def _():
            cos, s_next, s_prev = tab_ref[0], tab_ref[1], tab_ref[2]
            dg = jnp.zeros((1, HEAD_DIM), F32)
            for h in range(N_Q_HEADS):
                sl = slice(h * HEAD_DIM, (h + 1) * HEAD_DIM)
                dv, dgh = _norm_rope_backward(dz_ref[:, sl] * zscale, raw_ref[:, sl].astype(F32), g_ref[...], cos, s_next, s_prev)
                o_ref[:, sl] = dv.astype(BF)
                dg = dg + dgh
            dg_ref[0:1, :] += dg

    return pl.pallas_call(
        body, name=name, grid=(Ta // tm,),
        in_specs=[ANY, pl.BlockSpec((tm, w), lambda i: (jnp.maximum(i - nctx, 0), 0)),
                  pl.BlockSpec((tm, w), lambda i: (i, cb)), _resident((1, HEAD_DIM)),
                  pl.BlockSpec((3, tm, HEAD_DIM), lambda i: (0, i, 0))],
        out_specs=[pl.BlockSpec((tm, w), lambda i: (i, cb)), pl.BlockSpec((8, HEAD_DIM), lambda i: (0, 0))],
        out_shape=[jax.ShapeDtypeStruct(dproj.shape, BF), jax.ShapeDtypeStruct((8, HEAD_DIM), F32)],
        input_output_aliases={0: 0},
        compiler_params=_params(sem=("arbitrary",)),
    )(dproj, dq, proj, gain, tabs)


def kv_backward(dproj, dk, dv, proj, gain, tabs, offs, name):
    Ta = proj.shape[0]
    tm = ROW_TILE
    kw = N_KV_HEADS * HEAD_DIM
    cb = offs["k"] // (2 * kw)
    kb = offs["k"] // kw
    zscale = 1.0 / LOG2E

    def body(dp_any, dk_ref, dv_ref, raw_ref, g_ref, tab_ref, o_ref, dg_ref):
        @pl.when(pl.program_id(0) == 0)
        def _():
            dg_ref[...] = jnp.zeros_like(dg_ref)

        cos, s_next, s_prev = tab_ref[0], tab_ref[1], tab_ref[2]
        dg = jnp.zeros((1, HEAD_DIM), F32)
        for h in range(N_KV_HEADS):
            sl = slice(h * HEAD_DIM, (h + 1) * HEAD_DIM)
            dr, dgh = _norm_rope_backward(dk_ref[:, sl] * zscale, raw_ref[:, sl].astype(F32), g_ref[...], cos, s_next, s_prev)
            o_ref[:, sl] = dr.astype(BF)
            dg = dg + dgh
        o_ref[:, kw:2 * kw] = dv_ref[...].astype(BF)
        dg_ref[0:1, :] += dg

    return pl.pallas_call(
        body, name=name, grid=(Ta // tm,),
        in_specs=[ANY, _row(tm, kw), _row(tm, kw), pl.BlockSpec((tm, kw), lambda i: (i, kb)),
                  _resident((1, HEAD_DIM)), pl.BlockSpec((3, tm, HEAD_DIM), lambda i: (0, i, 0))],
        out_specs=[pl.BlockSpec((tm, 2 * kw), lambda i: (i, cb)), pl.BlockSpec((8, HEAD_DIM), lambda i: (0, 0))],
        out_shape=[jax.ShapeDtypeStruct(dproj.shape, BF), jax.ShapeDtypeStruct((8, HEAD_DIM), F32)],
        input_output_aliases={0: 0},
        compiler_params=_params(sem=("arbitrary",)),
    )(dproj, dk, dv, proj, gain, tabs)


def merge_forward(x1, mod6, yc, o, proj, w_bc, w_ba, w_o, offs, Tc, name):
    T, D = yc.shape[0], x1.shape[1]
    tm = ROW_TILE
    roff = Tc // tm
    gb = offs["gt"] // (2 * D)

    def body(x_ref, mod_ref, yc_ref, o_ref, gt_ref, wbc_ref, wba_ref, wo_ref, xo_ref, pc_ref, pa_ref, m_ref, z_ref):
        gate = mod_ref[0][2:3]
        pc = jnp.dot(yc_ref[...], wbc_ref[...], preferred_element_type=F32)
        pa = jnp.dot(o_ref[...], wba_ref[...], preferred_element_type=F32)
        pc_ref[...] = pc.astype(BF)
        pa_ref[...] = pa.astype(BF)
        mb = (_sigmoid(gt_ref[:, 0:D].astype(F32)) * pc + _sigmoid(gt_ref[:, D:2 * D].astype(F32)) * pa).astype(BF)
        m_ref[...] = mb
        z = jnp.dot(mb, wo_ref[...], preferred_element_type=F32)
        z_ref[...] = z.astype(BF)
        xo_ref[...] = x_ref[...] + gate * z

    return pl.pallas_call(
        body, name=name, grid=(T // tm,),
        in_specs=[pl.BlockSpec((tm, D), lambda i: (i + roff, 0)), _mod_spec(D, 1, 0), _row(tm, yc.shape[1]), _row(tm, o.shape[1]),
                  pl.BlockSpec((tm, 2 * D), lambda i: (i + roff, gb)),
                  _resident(w_bc.shape), _resident(w_ba.shape), _resident(w_o.shape)],
        out_specs=[_row(tm, D)] * 5,
        out_shape=[jax.ShapeDtypeStruct((T, D), F32)] + [jax.ShapeDtypeStruct((T, D), BF)] * 4,
        compiler_params=_params(vmem=VMEM_BIG, sem=("arbitrary",)),
    )(x1, mod6, yc, o, proj, w_bc, w_ba, w_o)


def merge_backward_rows(dx2, mod6, z, pc, pa, proj, w_bc, w_ba, w_o, offs, Tc, name):
    T, D = dx2.shape
    Ta, P = proj.shape
    tm = ROW_TILE
    nctx = Tc // tm
    gb = offs["gt"] // (2 * D)
    dcw, dqw = w_bc.shape[0], w_ba.shape[0]

    def body(dx_ref, mod_ref, z_ref, pc_ref, pa_ref, gt_ref, wbc_ref, wba_ref, wo_ref,
             dgt_ref, dg_ref, dpc_ref, dpa_ref, dyc_ref, do_ref, acc_ref):
        i = pl.program_id(0)

        @pl.when(i == 0)
        def _():
            acc_ref[...] = jnp.zeros_like(acc_ref)

        @pl.when(i < nctx)
        def _():
            dgt_ref[...] = jnp.zeros_like(dgt_ref)

        @pl.when(i >= nctx)
        def _():
            gate = mod_ref[0][2:3]
            dx = dx_ref[...]
            acc_ref[0:1, :] += jnp.sum(dx * z_ref[...].astype(F32), axis=0, keepdims=True)
            dgb = (dx * gate).astype(BF)
            dg_ref[...] = dgb
            dm = lax.dot_general(dgb, wo_ref[...], NT, preferred_element_type=F32)
            sc = _sigmoid(gt_ref[:, 0:D].astype(F32))
            sa = _sigmoid(gt_ref[:, D:2 * D].astype(F32))
            pc = pc_ref[...].astype(F32)
            pa = pa_ref[...].astype(F32)
            dpc = (dm * sc).astype(BF)
            dpa = (dm * sa).astype(BF)
            dpc_ref[...] = dpc
            dpa_ref[...] = dpa
            dgt_ref[:, 0:D] = ((dm * pc) * (sc * (1.0 - sc))).astype(BF)
            dgt_ref[:, D:2 * D] = ((dm * pa) * (sa * (1.0 - sa))).astype(BF)
            dyc_ref[...] = lax.dot_general(dpc, wbc_ref[...], NT, preferred_element_type=F32).astype(BF)
            do_ref[...] = lax.dot_general(dpa, wba_ref[...], NT, preferred_element_type=F32).astype(BF)

    lat = lambda n: pl.BlockSpec((tm, n), lambda i: (jnp.maximum(i - nctx, 0), 0))
    return pl.pallas_call(
        body, name=name, grid=(Ta // tm,),
        in_specs=[lat(D), _mod_spec(D, 1, 0), lat(D), lat(D), lat(D),
                  pl.BlockSpec((tm, 2 * D), lambda i: (i, gb)),
                  _resident(w_bc.shape), _resident(w_ba.shape), _resident(w_o.shape)],
        out_specs=[pl.BlockSpec((tm, 2 * D), lambda i: (i, gb)), lat(D), lat(D), lat(D), lat(dcw), lat(dqw),
                   pl.BlockSpec((8, D), lambda i: (0, 0))],
        out_shape=[jax.ShapeDtypeStruct((Ta, P), BF)] + [jax.ShapeDtypeStruct((T, D), BF)] * 3
                  + [jax.ShapeDtypeStruct((T, dcw), BF), jax.ShapeDtypeStruct((T, dqw), BF), jax.ShapeDtypeStruct((8, D), F32)],
        compiler_params=_params(vmem=VMEM_BIG, sem=("arbitrary",)),
    )(dx2, mod6, z, pc, pa, proj, w_bc, w_ba, w_o)


def proj_backward_rows(dproj, dres, xa, mod6, g, w_in, nctx, name):
    Tr, D = xa.shape
    P = w_in.shape[1]
    tm = ROW_TILE

    def body(dp_ref, dres_ref, x_ref, mod_ref, g_ref, w_ref, dx_ref, acc_ref):
        i = pl.program_id(0)

        @pl.when((i == 0) | (i == nctx))
        def _():
            acc_ref[...] = jnp.zeros_like(acc_ref)

        x = x_ref[...]
        scale = mod_ref[0][1:2]
        gg = g_ref[...]
        dhm = lax.dot_general(dp_ref[...], w_ref[...], NT, preferred_element_type=F32)
        r = lax.rsqrt(jnp.mean(x * x, axis=-1, keepdims=True) + EPS)
        xh = x * r
        dshift = jnp.sum(dhm, axis=0, keepdims=True)
        dscale = jnp.sum(dhm * (xh * gg), axis=0, keepdims=True)
        dxh_g = dhm * (1.0 + scale)
        dg = jnp.sum(dxh_g * xh, axis=0, keepdims=True)
        dxh = dxh_g * gg
        res = jnp.where(i < nctx, 0.0, dres_ref[...])
        dx_ref[...] = res + r * (dxh - xh * jnp.mean(dxh * xh, axis=-1, keepdims=True))
        for k, val in enumerate((dshift, dscale, dg)):
            acc_ref[0, k:k + 1, :] += val

    return pl.pallas_call(
        body, name=name, grid=(Tr // tm,),
        in_specs=[_row(tm, P), pl.BlockSpec((tm, D), lambda i: (jnp.maximum(i - nctx, 0), 0)), _row(tm, D),
                  _mod_spec(D, 1, nctx), _resident((1, D)), _resident(w_in.shape)],
        out_specs=[_row(tm, D), _acc_spec(D, nctx)],
        out_shape=[jax.ShapeDtypeStruct((Tr, D), F32), jax.ShapeDtypeStruct((2, 8, D), F32)],
        compiler_params=_params(vmem=VMEM_BIG, sem=("arbitrary",)),
    )(dproj, dres, xa, mod6, g, w_in)


def loss_and_final_norm_backward(x3, tgt, gf, name):
    T, D = x3.shape
    tm = ROW_TILE

    def body(x_ref, t_ref, g_ref, dx_ref, acc_ref):
        @pl.when(pl.program_id(0) == 0)
        def _():
            acc_ref[...] = jnp.zeros_like(acc_ref)

        x = x_ref[...]
        gg = g_ref[...]
        r = lax.rsqrt(jnp.mean(x * x, axis=-1, keepdims=True) + EPS)
        xh = x * r
        e = xh * gg - t_ref[...]
        part = 0.5 * jnp.sum(jnp.mean(e * e, axis=-1, keepdims=True), axis=0, keepdims=True)
        dy = e * (1.0 / D)
        dyg = dy * gg
        dx_ref[...] = r * (dyg - xh * jnp.mean(dyg * xh, axis=-1, keepdims=True))
        acc_ref[0:1, :] += jnp.sum(dy * xh, axis=0, keepdims=True)
        acc_ref[1:2, :] += jnp.broadcast_to(part, (1, D))

    return pl.pallas_call(
        body, name=name, grid=(T // tm,),
        in_specs=[_row(tm, D), _row(tm, D), _resident((1, D))],
        out_specs=[_row(tm, D), pl.BlockSpec((8, D), lambda i: (0, 0))],
        out_shape=[jax.ShapeDtypeStruct((T, D), F32), jax.ShapeDtypeStruct((8, D), F32)],
        compiler_params=_params(sem=("arbitrary",)),
    )(x3, tgt, gf)


def _adam_update(w, g, m, v):
    c1 = 1.0 - ADAM_B1 ** ADAM_STEP
    c2 = 1.0 - ADAM_B2 ** ADAM_STEP
    m = ADAM_B1 * m + (1.0 - ADAM_B1) * g
    v = ADAM_B2 * v + (1.0 - ADAM_B2) * (g * g)
    return -ADAM_LR * ((m / c1) / (jnp.sqrt(v / c2) + ADAM_EPS) + ADAM_WD * w), m, v


def adamw(w, g, m, v, name):
    R, C = w.shape
    tr = _row_tile(R, C)

    def body(w_ref, g_ref, m_ref, v_ref, d_ref, nm_ref, nv_ref):
        d_ref[...], nm_ref[...], nv_ref[...] = _adam_update(w_ref[...], g_ref[...], m_ref[...], v_ref[...])

    blk = pl.BlockSpec((tr, C), lambda i: (i, 0))
    return pl.pallas_call(
        body, name=name, grid=(R // tr,),
        in_specs=[blk] * 4, out_specs=[blk] * 3,
        out_shape=[jax.ShapeDtypeStruct((R, C), F32)] * 3,
        compiler_params=_params(sem=("parallel",)),
    )(w, g, m, v)


def adamw_summed(recv, w, m, v, name):
    R, C = w.shape
    tr = _row_tile(R, C)

    def body(r_ref, w_ref, m_ref, v_ref, g_ref, d_ref, nm_ref, nv_ref):
        g = r_ref[0].astype(F32)
        for a in range(1, N_DEV):
            g = g + r_ref[a].astype(F32)
        g_ref[...] = g
        d_ref[...], nm_ref[...], nv_ref[...] = _adam_update(w_ref[...], g, m_ref[...], v_ref[...])

    blk = pl.BlockSpec((tr, C), lambda i: (i, 0))
    return pl.pallas_call(
        body, name=name, grid=(R // tr,),
        in_specs=[pl.BlockSpec((N_DEV, tr, C), lambda i: (0, i, 0)), blk, blk, blk], out_specs=[blk] * 4,
        out_shape=[jax.ShapeDtypeStruct((R, C), F32)] * 4,
        compiler_params=_params(sem=("parallel",)),
    )(recv, w, m, v)


def adamw_reduced(place, ps, recv_b, w, m, v, name):
    R, C = w.shape
    tr = _row_tile(R, C)

    def body(place_ref, p_ref, b_ref, w_ref, m_ref, v_ref, g_ref, d_ref, nm_ref, nv_ref):
        g = p_ref[0].astype(F32)
        for j in range(3):
            g = g + b_ref[j].astype(F32)
        g_ref[...] = g
        d_ref[...], nm_ref[...], nv_ref[...] = _adam_update(w_ref[...], g, m_ref[...], v_ref[...])

    blk = pl.BlockSpec((tr, C), lambda r, pr: (r, 0))
    return pl.pallas_call(
        body, name=name,
        grid_spec=pltpu.PrefetchScalarGridSpec(
            num_scalar_prefetch=1, grid=(R // tr,),
            in_specs=[pl.BlockSpec((1, tr, C), lambda r, pr: (pr[1], r, 0)),
                      pl.BlockSpec((3, tr, C), lambda r, pr: (0, r, 0)), blk, blk, blk],
            out_specs=[blk] * 4),
        out_shape=[jax.ShapeDtypeStruct((R, C), F32)] * 4,
    )(place, ps, recv_b, w, m, v)


def _rope_tables(T, Tc):
    rows = T // GRID_W
    row = jnp.repeat(jnp.arange(rows), GRID_W).astype(F32)
    col = jnp.tile(jnp.arange(GRID_W), rows).astype(F32)
    n_freq = HEAD_DIM // 4
    inv = ROPE_THETA ** (-jnp.arange(n_freq, dtype=F32) / n_freq)
    cr, sr = jnp.cos(row[:, None] * inv), jnp.sin(row[:, None] * inv)
    cc, sc = jnp.cos(col[:, None] * inv), jnp.sin(col[:, None] * inv)
    zero = jnp.zeros_like(sr)
    cos = jnp.concatenate([cr, cr, cc, cc], axis=1)
    s_next = jnp.concatenate([-sr, zero, -sc, zero], axis=1)
    s_prev = jnp.concatenate([zero, sr, zero, sc], axis=1)
    lat = jnp.stack([cos, s_next, s_prev])
    ctx = jnp.stack([jnp.ones((Tc, HEAD_DIM), F32), jnp.zeros((Tc, HEAD_DIM), F32), jnp.zeros((Tc, HEAD_DIM), F32)])
    return jnp.concatenate([ctx, lat], axis=1)


BIG = ("ffn1_w_in", "ffn1_w_out", "w_in", "w_branch_conv", "w_branch_attn", "w_out", "ffn2_w_in", "ffn2_w_out")


def _regroup_w_in(stacked, D, Dc, qw, kw):
    w = stacked.transpose(1, 0, 2).reshape(D, -1)
    o = 0
    parts = {}
    for nme, wd in (("bg", Dc), ("cg", Dc), ("vc", Dc), ("q", qw), ("k", kw), ("v", kw), ("gt", 2 * D)):
        parts[nme] = w[:, o:o + wd]
        o += wd
    nb = Dc // 128
    cv = jnp.stack([parts[n].reshape(D, nb, 128) for n in ("bg", "cg", "vc")], axis=2).reshape(D, 3 * Dc)
    return jnp.concatenate([cv, parts["q"], parts["gt"], parts["k"], parts["v"]], axis=1)


def _ungroup_w_in_grad(g, D, Dc, qw, kw):
    nb = Dc // 128
    cv = g[:, :3 * Dc].reshape(D, nb, 3, 128)
    o = 3 * Dc
    q = g[:, o:o + qw]
    gt = g[:, o + qw:o + qw + 2 * D]
    k = g[:, o + qw + 2 * D:o + qw + 2 * D + kw]
    v = g[:, o + qw + 2 * D + kw:]
    nat = jnp.concatenate([cv[:, :, 0].reshape(D, Dc), cv[:, :, 1].reshape(D, Dc), cv[:, :, 2].reshape(D, Dc), q, k, v, gt], axis=1)
    return nat.reshape(D, N_DEV, -1).transpose(1, 0, 2)


def kernel(x, c, ctx, c_ctx, w_mod, b_mod, norm1_g, norm2_g, norm3_g, ffn1_w_in, ffn1_w_out, w_in, conv_w, q_norm_g, k_norm_g, w_branch_conv, w_branch_attn, w_out, ffn2_w_in, ffn2_w_out, final_g, loss_target, m_c_ctx, m_w_mod, m_b_mod, m_norm1_g, m_norm2_g, m_norm3_g, m_ffn1_w_in, m_ffn1_w_out, m_w_in, m_conv_w, m_q_norm_g, m_k_norm_g, m_w_branch_conv, m_w_branch_attn, m_w_out, m_ffn2_w_in, m_ffn2_w_out, m_final_g, v_c_ctx, v_w_mod, v_b_mod, v_norm1_g, v_norm2_g, v_norm3_g, v_ffn1_w_in, v_ffn1_w_out, v_w_in, v_conv_w, v_q_norm_g, v_k_norm_g, v_w_branch_conv, v_w_branch_attn, v_w_out, v_ffn2_w_in, v_ffn2_w_out, v_final_g):
    weights = dict(c_ctx=c_ctx, w_mod=w_mod, b_mod=b_mod, norm1_g=norm1_g, norm2_g=norm2_g, norm3_g=norm3_g,
                   ffn1_w_in=ffn1_w_in, ffn1_w_out=ffn1_w_out, w_in=w_in, conv_w=conv_w, q_norm_g=q_norm_g,
                   k_norm_g=k_norm_g, w_branch_conv=w_branch_conv, w_branch_attn=w_branch_attn, w_out=w_out,
                   ffn2_w_in=ffn2_w_in, ffn2_w_out=ffn2_w_out, final_g=final_g)
    moms = dict(c_ctx=(m_c_ctx, v_c_ctx), w_mod=(m_w_mod, v_w_mod), b_mod=(m_b_mod, v_b_mod),
                norm1_g=(m_norm1_g, v_norm1_g), norm2_g=(m_norm2_g, v_norm2_g), norm3_g=(m_norm3_g, v_norm3_g),
                ffn1_w_in=(m_ffn1_w_in, v_ffn1_w_in), ffn1_w_out=(m_ffn1_w_out, v_ffn1_w_out), w_in=(m_w_in, v_w_in),
                conv_w=(m_conv_w, v_conv_w), q_norm_g=(m_q_norm_g, v_q_norm_g), k_norm_g=(m_k_norm_g, v_k_norm_g),
                w_branch_conv=(m_w_branch_conv, v_w_branch_conv), w_branch_attn=(m_w_branch_attn, v_w_branch_attn),
                w_out=(m_w_out, v_w_out), ffn2_w_in=(m_ffn2_w_in, v_ffn2_w_in), ffn2_w_out=(m_ffn2_w_out, v_ffn2_w_out),
                final_g=(m_final_g, v_final_g))
    order = list(weights)

    T, D = x.shape[1], x.shape[2]
    Tc = ctx.shape[1]
    nctx = Tc // ROW_TILE
    nd = N_MOD * D
    Dc = conv_w.shape[2] * N_DEV
    qw, kw = N_Q_HEADS * HEAD_DIM, N_KV_HEADS * HEAD_DIM
    offs, o = {}, 0
    for nme, wd in (("cv", 3 * Dc), ("q", qw), ("gt", 2 * D), ("k", kw), ("v", kw)):
        offs[nme] = o
        o += wd

    ax, ay, ac = lax.axis_index("x"), lax.axis_index("y"), lax.axis_index("c")
    me = 4 * ax + 2 * ay + ac
    place = jnp.stack([ac, 2 * ax + ay]).astype(jnp.int32)

    shard = {n: weights[n][0].astype(BF) for n in BIG}
    rows2d = lambda a: a.reshape(-1, a.shape[-1])
    full = {}
    full["ffn1_w_in"], g_ffn1_out = allgather_two_level([shard["ffn1_w_in"], shard["ffn1_w_out"]], "ag_ffn1")
    full["ffn1_w_out"] = rows2d(g_ffn1_out)

    mod_cols = w_mod.shape[2]
    cw_loc = conv_w[0]
    cpad = (-(D + CONV_TAPS * cw_loc.shape[1])) % 128
    pay = jnp.concatenate([c.reshape(1, D), cw_loc.reshape(1, -1), jnp.zeros((1, cpad), F32)], axis=1)
    call = allgather_direct(pay, "ag_cond")
    conv_full = call[:, 0, D:D + CONV_TAPS * cw_loc.shape[1]].reshape(N_DEV, CONV_TAPS, -1).transpose(1, 0, 2).reshape(CONV_TAPS, Dc)
    b_loc = lax.dynamic_slice_in_dim(b_mod, me * mod_cols, mod_cols, axis=1)
    cctx2 = c_ctx.reshape(1, D)
    mod_part = mod_forward(call, cctx2, w_mod[0], b_loc, "mod_fwd")
    mod_all = allgather_direct(mod_part, "ag_mod")
    mod_lat = lax.dynamic_index_in_dim(mod_all, me, axis=1, keepdims=False).reshape(nd)
    mod_ctx = mod_all[:, N_DEV, :].reshape(nd)
    mod6 = jnp.stack([mod_ctx, mod_lat]).reshape(6, 3, D)

    tabs = _rope_tables(T, Tc)

    srcs1 = (ctx[0], x[0])
    (xa1, hm1, ab1, h1, f1), (g_w_in,) = ffn_forward(
        srcs1, mod6, 0, norm1_g, full["ffn1_w_in"], full["ffn1_w_out"], nctx, "ffn1_fwd",
        hosted=Hosted(gathers=[shard["w_in"]]))
    full["w_in"] = _regroup_w_in(g_w_in, D, Dc, qw, kw)
    merge_names = ("w_branch_conv", "w_branch_attn", "w_out")
    (hx, proj, qr, kr), g_merge = proj_forward(
        xa1, mod6, norm2_g, full["w_in"], q_norm_g, k_norm_g, tabs, offs, nctx, "proj_fwd",
        hosted=Hosted(gathers=[shard[n] for n in merge_names]))
    full.update({n: rows2d(g) for n, g in zip(merge_names, g_merge)})
    yc = conv_forward(proj, conv_full, offs, Tc, "conv_fwd")
    (oa, lse), (full["ffn2_w_in"], g_ffn2_out) = attention_forward(
        qr, kr, proj, offs, Tc, "attn_fwd", hosted=Hosted(gathers=[shard["ffn2_w_in"], shard["ffn2_w_out"]]))
    full["ffn2_w_out"] = rows2d(g_ffn2_out)
    x2, pc, pa, mm, zz = merge_forward(xa1, mod6, yc, oa, proj, full["w_branch_conv"], full["w_branch_attn"],
                                       full["w_out"], offs, Tc, "merge_fwd")
    (x3, hm2, ab2, h2, f2), _ = ffn_forward((x2,), mod6, 2, norm3_g, full["ffn2_w_in"], full["ffn2_w_out"], 0, "ffn2_fwd")
    dx3, lacc = loss_and_final_norm_backward(x3, loss_target[0], final_g.reshape(1, D), "loss_bwd")
    loss = lax.psum(lacc[1, 0], ("x", "y", "c"))

    by_dest = lambda g: g.reshape((N_DEV, -1, g.shape[-1]))
    (dx2, dab2, df2, acc_f2), _ = ffn_backward_rows(dx3, (x2,), mod6, 2, norm3_g, ab2, f2, full["ffn2_w_in"], full["ffn2_w_out"], 0, "ffn2_bwd")
    early = {"ffn2_w_out": tn_ffn_out(h2, df2, "ffn2_dwout"), "ffn2_w_in": tn_ffn_in(hm2, dab2, "ffn2_dwin")}
    dproj, dgm, dpc, dpa, dyc, do, acc_mg = merge_backward_rows(dx2, mod6, zz, pc, pa, proj, full["w_branch_conv"],
                                                                full["w_branch_attn"], full["w_out"], offs, Tc, "merge_bwd")
    early["w_out"] = tn_matmul(mm, dgm, "dw_out")
    early["w_branch_conv"] = tn_matmul(yc, dpc, "dw_bc")
    early["w_branch_attn"] = tn_matmul(oa, dpa, "dw_ba")
    dproj, dcw = conv_backward(dproj, dyc, proj, conv_full, offs, Tc, "conv_bwd")
    (dq, dk, dv), summed = attention_backward(qr, kr, proj, oa, lse, do, offs, Tc, "attn_bwd",
                                              hosted=Hosted(scatters=[by_dest(g) for g in early.values()]))
    summed = dict(zip(early, summed))
    dproj, dqg = q_norm_backward(dproj, dq, proj, q_norm_g, tabs, offs, nctx, "q_norm_bwd")
    dproj, dkg = kv_backward(dproj, dk, dv, proj, k_norm_g, tabs, offs, "kv_bwd")
    dxa1, acc_pj = proj_backward_rows(dproj, dx2, xa1, mod6, norm2_g, full["w_in"], nctx, "proj_bwd")
    g_w_in_grad = _ungroup_w_in_grad(tn_matmul(hx, dproj, "dw_in"), D, Dc, qw, kw)
    (grad_x2d, dab1, df1, acc_f1), (summed["w_in"],) = ffn_backward_rows(
        dxa1, srcs1, mod6, 0, norm1_g, ab1, f1, full["ffn1_w_in"], full["ffn1_w_out"], nctx, "ffn1_bwd",
        hosted=Hosted(scatters=[g_w_in_grad]))
    late = ("ffn1_w_out", "ffn1_w_in")
    gs = [by_dest(tn_ffn_out(h1, df1, "ffn1_dwout")), by_dest(tn_ffn_in(hm1, dab1, "ffn1_dwin"))]
    grad_x = grad_x2d[None]
    recv_a = rs_sibling_exchange(gs, "rs_sibling")
    ps = [rs_pair_sum(place, g, a, "rs_pair_sum_" + n) for n, g, a in zip(late, gs, recv_a)]
    recv_b = rs_chip_exchange(ps, "rs_chips")
    reduced = dict(zip(late, zip(ps, recv_b)))

    zero_d = jnp.zeros((D,), F32)
    dlat = jnp.concatenate([acc_f1[1, 0], acc_f1[1, 1], acc_f1[1, 2], acc_pj[1, 0], acc_pj[1, 1], acc_mg[0],
                            acc_f2[1, 0], acc_f2[1, 1], acc_f2[1, 2]])
    dctx = jnp.concatenate([acc_f1[0, 0], acc_f1[0, 1], acc_f1[0, 2], acc_pj[0, 0], acc_pj[0, 1]] + [zero_d] * 4)
    small = jnp.concatenate([acc_f1[0, 3] + acc_f1[1, 3], acc_pj[0, 2] + acc_pj[1, 2], acc_f2[1, 3],
                             dqg[0], dkg[0], lacc[0], dcw[0:CONV_TAPS].reshape(-1)])
    n_small = small.shape[0]
    pay_b = jnp.concatenate([dlat, dctx, small]).reshape(1, -1)
    gath = allgather_direct(pay_b, "ag_small_grads")
    dlat_loc = lax.dynamic_slice_in_dim(gath[:, 0, :nd], me * mod_cols, mod_cols, axis=1)
    dctx_loc = lax.dynamic_slice_in_dim(gath[:, 0, nd:2 * nd], me * mod_cols, mod_cols, axis=1)
    g_wmod, pc_part, small_sum = mod_backward(call, cctx2, w_mod[0], dlat_loc, dctx_loc, gath, 2 * nd, n_small, "mod_bwd")
    pcs = allgather_direct(pc_part, "ag_cctx")
    g_bmod, g_cctx = bmod_and_cctx_grad(gath, pcs, cctx2, nd, "small_bwd")
    sm = small_sum[0]
    g_conv_full = sm[3 * D + 2 * HEAD_DIM + D:].reshape(CONV_TAPS, Dc)
    g_conv = lax.dynamic_slice_in_dim(g_conv_full, me * cw_loc.shape[1], cw_loc.shape[1], axis=1)
    gsmall = dict(
        c_ctx=g_cctx, w_mod=g_wmod, b_mod=g_bmod, norm1_g=sm[0:D][None], norm2_g=sm[D:2 * D][None],
        norm3_g=sm[2 * D:3 * D][None], q_norm_g=sm[3 * D:3 * D + HEAD_DIM][None],
        k_norm_g=sm[3 * D + HEAD_DIM:3 * D + 2 * HEAD_DIM][None],
        final_g=sm[3 * D + 2 * HEAD_DIM:3 * D + 2 * HEAD_DIM + D][None], conv_w=g_conv)

    g_out, d_out, m_out, v_out = [], [], [], []
    for n in order:
        w = weights[n]
        shp = w.shape
        two_d = (lambda a: a.reshape(-1, shp[-1]))
        m, v = moms[n]
        if n in reduced:
            g2, d, nm, nv = adamw_reduced(place, *reduced[n], two_d(w), two_d(m), two_d(v), "adamw_" + n)
        elif n in summed:
            g2, d, nm, nv = adamw_summed(summed[n], two_d(w), two_d(m), two_d(v), "adamw_" + n)
        else:
            g2 = gsmall[n].reshape(two_d(w).shape)
            d, nm, nv = adamw(two_d(w), g2, two_d(m), two_d(v), "adamw_" + n)
        g_out.append(g2.reshape(shp))
        d_out.append(d.reshape(shp))
        m_out.append(nm.reshape(shp))
        v_out.append(nv.reshape(shp))
    return (loss, grad_x, *g_out, *d_out, *m_out, *v_out)
```

```python
import math

import jax
import jax.numpy as jnp
from jax import lax
from jax.experimental import pallas as pl
from jax.experimental.pallas import tpu as pltpu

F32 = jnp.float32
BF = jnp.bfloat16
EPS = 1e-6
N_DEV = 8
HEAD_DIM = 128
N_Q_HEADS = 8
N_KV_HEADS = 2
GROUP = N_Q_HEADS // N_KV_HEADS
GRID_W = 64
ROPE_THETA = 10000.0
CONV_TAPS = 3
N_MOD = 9
ADAM_LR = 0.001
ADAM_B1 = 0.9
ADAM_B2 = 0.999
ADAM_EPS = 1e-08
ADAM_WD = 0.01
ADAM_STEP = 10
ROW_TILE = 256
VMEM_BIG = 56 << 20
MESH_ID = pl.DeviceIdType.MESH
HIGHEST = lax.Precision.HIGHEST
NT = (((1,), (1,)), ((), ()))
TN = (((0,), (0,)), ((), ()))
LOG2E = math.log2(math.e)


def _pick(n, cands):
    for c in cands:
        if n % c == 0:
            return c
    return n


def _params(vmem=None, sem=None):
    kw = {}
    if vmem is not None:
        kw["vmem_limit_bytes"] = vmem
    if sem is not None:
        kw["dimension_semantics"] = sem
    return pltpu.CompilerParams(**kw)


def _resident(shape):
    nd = len(shape)
    return pl.BlockSpec(shape, lambda *_: (0,) * nd, pipeline_mode=pl.Buffered(1))


def _sigmoid(x):
    return jax.nn.sigmoid(x)


ANY = pl.BlockSpec(memory_space=pl.ANY)


def _coords():
    return lax.axis_index("x"), lax.axis_index("y"), lax.axis_index("c")


def _flip(v, bit):
    return 1 - v if bit else v


def _remote(src, dst, ssem, rsem, dev):
    return pltpu.make_async_remote_copy(src_ref=src, dst_ref=dst, send_sem=ssem, recv_sem=rsem,
                                        device_id=dev, device_id_type=MESH_ID)


def allgather_direct(v, name):
    def body(v_ref, out_ref, ssem, rsem, lsem):
        x, y, c = _coords()
        me = 4 * x + 2 * y + c
        mine = pltpu.make_async_copy(v_ref, out_ref.at[me], lsem)
        mine.start()
        cps = []
        for p in range(1, N_DEV):
            px, py, pc = (p >> 2) & 1, (p >> 1) & 1, p & 1
            cps.append(_remote(v_ref, out_ref.at[me], ssem.at[p - 1], rsem.at[p - 1],
                               (_flip(x, px), _flip(y, py), _flip(c, pc))))
        for cp in cps:
            cp.start()
        for p in range(1, N_DEV):
            px, py, pc = (p >> 2) & 1, (p >> 1) & 1, p & 1
            src = 4 * _flip(x, px) + 2 * _flip(y, py) + _flip(c, pc)
            _remote(v_ref, out_ref.at[src], ssem.at[p - 1], rsem.at[p - 1], (x, y, c)).wait_recv()
        for cp in cps:
            cp.wait_send()
        mine.wait()

    return pl.pallas_call(
        body, name=name,
        out_shape=jax.ShapeDtypeStruct((N_DEV,) + v.shape, v.dtype),
        in_specs=[ANY], out_specs=ANY,
        scratch_shapes=[pltpu.SemaphoreType.DMA((N_DEV - 1,)), pltpu.SemaphoreType.DMA((N_DEV - 1,)),
                        pltpu.SemaphoreType.DMA],
    )(v)


def allgather_two_level(shards, name):
    n = len(shards)

    def body(*refs):
        v_refs, out_refs, (ssem, rsem, lsem) = refs[:n], refs[n:2 * n], refs[2 * n:]
        x, y, c = _coords()
        me = (x, y, c)
        sib = (x, y, 1 - c)
        chips = [(1 - x, y), (x, 1 - y), (1 - x, 1 - y)]

        def slot(w, px, py, pc):
            return out_refs[w].at[4 * px + 2 * py + pc]

        def sem(w, k):
            return ssem.at[7 * w + k], rsem.at[7 * w + k]

        mine = [pltpu.make_async_copy(v_refs[w], slot(w, *me), lsem.at[w]) for w in range(n)]
        for cp in mine:
            cp.start()
        first = []
        for w in range(n):
            first.append(_remote(v_refs[w], slot(w, *me), *sem(w, 0), sib))
            first += [_remote(v_refs[w], slot(w, *me), *sem(w, 1 + j), (*chip, c)) for j, chip in enumerate(chips)]
        for cp in first:
            cp.start()
        passed = []
        for w in range(n):
            for j, chip in enumerate(chips):
                _remote(v_refs[w], slot(w, *chip, c), *sem(w, 1 + j), me).wait_recv()
                cp = _remote(slot(w, *chip, c), slot(w, *chip, c), *sem(w, 4 + j), sib)
                cp.start()
                passed.append(cp)
        for w in range(n):
            _remote(v_refs[w], slot(w, x, y, 1 - c), *sem(w, 0), me).wait_recv()
            for j, chip in enumerate(chips):
                _remote(v_refs[w], slot(w, *chip, 1 - c), *sem(w, 4 + j), me).wait_recv()
        for cp in first + passed:
            cp.wait_send()
        for cp in mine:
            cp.wait()

    return pl.pallas_call(
        body, name=name,
        out_shape=[jax.ShapeDtypeStruct((N_DEV,) + s.shape, s.dtype) for s in shards],
        in_specs=[ANY] * n, out_specs=[ANY] * n,
        scratch_shapes=[pltpu.SemaphoreType.DMA((7 * n,)), pltpu.SemaphoreType.DMA((7 * n,)),
                        pltpu.SemaphoreType.DMA((n,))],
    )(*shards)


def rs_sibling_exchange(gs, name):
    n = len(gs)

    def body(*refs):
        g_refs, out_refs, (ssem, rsem) = refs[:n], refs[n:2 * n], refs[2 * n:]
        x, y, c = _coords()
        sib = (x, y, 1 - c)
        cps = [_remote(g_refs[w].at[2 * k + (1 - c)], out_refs[w].at[k], ssem.at[4 * w + k], rsem.at[4 * w + k], sib)
               for w in range(n) for k in range(4)]
        for cp in cps:
            cp.start()
        for cp in cps:
            cp.wait()

    return pl.pallas_call(
        body, name=name,
        out_shape=[jax.ShapeDtypeStruct((4,) + g.shape[1:], g.dtype) for g in gs],
        in_specs=[ANY] * n, out_specs=[ANY] * n,
        scratch_shapes=[pltpu.SemaphoreType.DMA((4 * n,)), pltpu.SemaphoreType.DMA((4 * n,))],
    )(*gs)


def rs_chip_exchange(ps, name):
    n = len(ps)

    def body(*refs):
        p_refs, out_refs, (ssem, rsem) = refs[:n], refs[n:2 * n], refs[2 * n:]
        x, y, c = _coords()
        chips = [(1 - x, y), (x, 1 - y), (1 - x, 1 - y)]
        cps = [_remote(p_refs[w].at[2 * cx + cy], out_refs[w].at[j], ssem.at[3 * w + j], rsem.at[3 * w + j], (cx, cy, c))
               for w in range(n) for j, (cx, cy) in enumerate(chips)]
        for cp in cps:
            cp.start()
        for cp in cps:
            cp.wait()

    return pl.pallas_call(
        body, name=name,
        out_shape=[jax.ShapeDtypeStruct((3,) + p.shape[1:], p.dtype) for p in ps],
        in_specs=[ANY] * n, out_specs=[ANY] * n,
        scratch_shapes=[pltpu.SemaphoreType.DMA((3 * n,)), pltpu.SemaphoreType.DMA((3 * n,))],
    )(*ps)


class Hosted:
    def __init__(self, gathers=(), scatters=()):
        self.items = [(a, False) for a in gathers] + [(a, True) for a in scatters]
        self.n = len(self.items)
        self.operands = [a for a, _ in self.items]
        self.out_shapes = [jax.ShapeDtypeStruct(a.shape if sc else (N_DEV,) + a.shape, a.dtype) for a, sc in self.items]
        self.scratch = [pltpu.SemaphoreType.DMA((7 * self.n,)), pltpu.SemaphoreType.DMA((7 * self.n,)),
                        pltpu.SemaphoreType.DMA((self.n,))]

    def _copies(self, in_refs, out_refs, ssem, rsem, lsem, arrivals):
        x, y, c = _coords()
        me = 4 * x + 2 * y + c
        remote, local = [], []
        for w, (_, sc) in enumerate(self.items):
            src, dst = in_refs[w], out_refs[w]
            local.append(pltpu.make_async_copy(src.at[me] if sc else src, dst.at[me], lsem.at[w]))
            for p in range(1, N_DEV):
                px, py, pc = _flip(x, (p >> 2) & 1), _flip(y, (p >> 1) & 1), _flip(c, p & 1)
                peer = 4 * px + 2 * py + pc
                k = 7 * w + p - 1
                if arrivals:
                    remote.append(_remote(src.at[me] if sc else src, dst.at[peer], ssem.at[k], rsem.at[k], (x, y, c)))
                else:
                    remote.append(_remote(src.at[peer] if sc else src, dst.at[me], ssem.at[k], rsem.at[k], (px, py, pc)))
        return remote, local

    def start(self, in_refs, out_refs, ssem, rsem, lsem):
        sends, local = self._copies(in_refs, out_refs, ssem, rsem, lsem, False)
        for cp in local + sends:
            cp.start()

    def wait(self, in_refs, out_refs, ssem, rsem, lsem):
        arrivals, local = self._copies(in_refs, out_refs, ssem, rsem, lsem, True)
        for cp in arrivals:
            cp.wait_recv()
        for cp in arrivals:
            cp.wait_send()
        for cp in local:
            cp.wait()


def _call(body, *, name, grid, in_specs, out_specs, out_shape, operands, params, scratch_shapes=(), aliases=None, hosted=None):
    n_in, n_out, n_scr = len(in_specs), len(out_specs), len(scratch_shapes)
    h = hosted.n if hosted is not None else 0

    def wrapped(*refs):
        ins, cins = refs[:n_in], refs[n_in:n_in + h]
        outs, couts = refs[n_in + h:n_in + h + n_out], refs[n_in + h + n_out:n_in + 2 * h + n_out]
        rest = refs[n_in + 2 * h + n_out:]
        scr, sems = rest[:n_scr], rest[n_scr:]
        if h:
            ids = [pl.program_id(a) for a in range(len(grid))]
            first, last = ids[0] == 0, ids[0] == grid[0] - 1
            for a in range(1, len(grid)):
                first, last = first & (ids[a] == 0), last & (ids[a] == grid[a] - 1)

            @pl.when(first)
            def _():
                hosted.start(cins, couts, *sems)

        body(*ins, *outs, *scr)
        if h:
            @pl.when(last)
            def _():
                hosted.wait(cins, couts, *sems)

    res = pl.pallas_call(
        wrapped, name=name, grid=grid,
        in_specs=list(in_specs) + [ANY] * h, out_specs=list(out_specs) + [ANY] * h,
        out_shape=list(out_shape) + (hosted.out_shapes if h else []),
        scratch_shapes=list(scratch_shapes) + (hosted.scratch if h else []),
        input_output_aliases=aliases or {}, compiler_params=params,
    )(*operands, *(hosted.operands if h else []))
    return list(res[:n_out]), list(res[n_out:])


def _row_tile(R, C):
    return _pick(R, (256, 128, 64, 32, 16)) if R * C > (1 << 18) else R


def rs_pair_sum(place, gs, recv_a, name):
    _, R, C = gs.shape
    tr = _row_tile(R, C)

    def body(place_ref, g_ref, a_ref, o_ref):
        o_ref[0] = (g_ref[0].astype(F32) + a_ref[0].astype(F32)).astype(o_ref.dtype)

    return pl.pallas_call(
        body, name=name,
        grid_spec=pltpu.PrefetchScalarGridSpec(
            num_scalar_prefetch=1, grid=(4, R // tr),
            in_specs=[pl.BlockSpec((1, tr, C), lambda k, r, pr: (2 * k + pr[0], r, 0)),
                      pl.BlockSpec((1, tr, C), lambda k, r, pr: (k, r, 0))],
            out_specs=pl.BlockSpec((1, tr, C), lambda k, r, pr: (k, r, 0))),
        out_shape=jax.ShapeDtypeStruct((4, R, C), BF),
    )(place, gs, recv_a)


def _cond_rows(call_ref, cctx_ref, z_ref, D):
    z_ref[...] = jnp.zeros_like(z_ref)
    for a in range(N_DEV):
        z_ref[a:a + 1, :] = call_ref[a][:, :D]
    z_ref[N_DEV:N_DEV + 1, :] = cctx_ref[...]


def mod_forward(call, c_ctx, w_loc, b_loc, name):
    D, cols = w_loc.shape

    def body(call_ref, cctx_ref, w_ref, b_ref, o_ref, z_ref):
        _cond_rows(call_ref, cctx_ref, z_ref, D)
        z = z_ref[...]
        s = z * _sigmoid(z)
        o_ref[...] = jnp.dot(s, w_ref[...], preferred_element_type=F32, precision=HIGHEST) + b_ref[...]

    return pl.pallas_call(
        body, name=name, out_shape=jax.ShapeDtypeStruct((16, cols), F32),
        scratch_shapes=[pltpu.VMEM((16, D), F32)],
        compiler_params=_params(vmem=VMEM_BIG),
    )(call, c_ctx, w_loc, b_loc)


def mod_backward(call, c_ctx, w_loc, dlat_loc, dctx_loc, gath, n_small_off, n_small, name):
    D, cols = w_loc.shape

    def body(call_ref, cctx_ref, w_ref, dlat_ref, dctx_ref, g_ref, gw_ref, pc_ref, small_ref, z_ref, dm_ref):
        _cond_rows(call_ref, cctx_ref, z_ref, D)
        z = z_ref[...]
        s = z * _sigmoid(z)
        dctx = dctx_ref[0:1, :]
        for a in range(1, N_DEV):
            dctx = dctx + dctx_ref[a:a + 1, :]
        dm_ref[...] = jnp.zeros_like(dm_ref)
        dm_ref[0:N_DEV, :] = dlat_ref[...]
        dm_ref[N_DEV:N_DEV + 1, :] = dctx
        gw_ref[...] = lax.dot_general(s, dm_ref[...], TN, preferred_element_type=F32, precision=HIGHEST)
        pc_ref[...] = lax.dot_general(dctx, w_ref[...], NT, preferred_element_type=F32, precision=HIGHEST)
        acc = g_ref[0][:, n_small_off:n_small_off + n_small]
        for a in range(1, N_DEV):
            acc = acc + g_ref[a][:, n_small_off:n_small_off + n_small]
        small_ref[...] = acc

    return pl.pallas_call(
        body, name=name,
        out_shape=(jax.ShapeDtypeStruct((D, cols), F32), jax.ShapeDtypeStruct((1, D), F32),
                   jax.ShapeDtypeStruct((1, n_small), F32)),
        scratch_shapes=[pltpu.VMEM((16, D), F32), pltpu.VMEM((16, cols), F32)],
        compiler_params=_params(vmem=VMEM_BIG),
    )(call, c_ctx, w_loc, dlat_loc, dctx_loc, gath)


def bmod_and_cctx_grad(gath, pcs, c_ctx, nd, name):
    D = c_ctx.shape[-1]

    def body(g_ref, pc_ref, cctx_ref, gb_ref, gc_ref):
        acc = g_ref[0][:, :nd] + g_ref[0][:, nd:2 * nd]
        for a in range(1, N_DEV):
            acc = acc + (g_ref[a][:, :nd] + g_ref[a][:, nd:2 * nd])
        gb_ref[...] = acc
        p = pc_ref[0]
        for a in range(1, N_DEV):
            p = p + pc_ref[a]
        z = cctx_ref[...]
        sg = _sigmoid(z)
        gc_ref[...] = p * (sg * (1.0 + z * (1.0 - sg)))

    return pl.pallas_call(
        body, name=name,
        out_shape=(jax.ShapeDtypeStruct((1, nd), F32), jax.ShapeDtypeStruct((1, D), F32)),
    )(gath, pcs, c_ctx)


def _mod_spec(D, which, nctx):
    return pl.BlockSpec((1, 3, D), lambda i: (jnp.where(i < nctx, 0, 3) + which, 0, 0))


def _acc_spec(D, nctx):
    return pl.BlockSpec((1, 8, D), lambda i: (jnp.where(i < nctx, 0, 1), 0, 0))


def _row(tm, n):
    return pl.BlockSpec((tm, n), lambda i: (i, 0))


def _stk(k, tm, n):
    return pl.BlockSpec((k, tm, n), lambda i: (0, i, 0))


def _two_stream_specs(tm, D, nctx):
    return [pl.BlockSpec((tm, D), lambda i: (jnp.minimum(i, nctx - 1), 0)),
            pl.BlockSpec((tm, D), lambda i: (jnp.maximum(i - nctx, 0), 0))]


def ffn_forward(srcs, mod6, which, g, ws, w_out, nctx, name, hosted=None):
    D = srcs[-1].shape[1]
    Tr = sum(s.shape[0] for s in srcs)
    nsh, _, nn = ws.shape
    nh = nsh // 2
    tm = ROW_TILE
    two = len(srcs) == 2

    def body(*refs):
        x_refs, (mod_ref, g_ref, ws_ref, wout_ref, xo_ref, hm_ref, ab_ref, h_ref, f_ref) = refs[:len(srcs)], refs[len(srcs):]
        x = jnp.where(pl.program_id(0) < nctx, x_refs[0][...], x_refs[1][...]) if two else x_refs[0][...]
        ms = mod_ref[0]
        shift, scale, gate = ms[0:1], ms[1:2], ms[2:3]
        r = lax.rsqrt(jnp.mean(x * x, axis=-1, keepdims=True) + EPS)
        hb = (((x * r) * g_ref[...]) * (1.0 + scale) + shift).astype(BF)
        hm_ref[...] = hb
        f = jnp.zeros((tm, D), F32)
        for j in range(nh):
            a = jnp.dot(hb, ws_ref[j], preferred_element_type=F32)
            b = jnp.dot(hb, ws_ref[nh + j], preferred_element_type=F32)
            ab_ref[j] = a.astype(BF)
            ab_ref[nh + j] = b.astype(BF)
            h = ((a * _sigmoid(a)) * b).astype(BF)
            h_ref[j] = h
            f = f + jnp.dot(h, wout_ref[j * nn:(j + 1) * nn, :], preferred_element_type=F32)
        f_ref[...] = f.astype(BF)
        xo_ref[...] = x + (0.5 * gate) * f

    src_specs = _two_stream_specs(tm, D, nctx) if two else [_row(tm, D)]
    return _call(
        body, name=name, grid=(Tr // tm,),
        in_specs=src_specs + [_mod_spec(D, which, nctx), _resident((1, D)), _resident(ws.shape), _resident(w_out.shape)],
        out_specs=[_row(tm, D), _row(tm, D), _stk(nsh, tm, nn), _stk(nh, tm, nn), _row(tm, D)],
        out_shape=[jax.ShapeDtypeStruct((Tr, D), F32), jax.ShapeDtypeStruct((Tr, D), BF),
                   jax.ShapeDtypeStruct((nsh, Tr, nn), BF), jax.ShapeDtypeStruct((nh, Tr, nn), BF),
                   jax.ShapeDtypeStruct((Tr, D), BF)],
        operands=[*srcs, mod6, g, ws, w_out], hosted=hosted,
        params=_params(vmem=VMEM_BIG, sem=("arbitrary",)))


def ffn_backward_rows(dxo, srcs, mod6, which, g, ab, fo, ws, w_out, nctx, name, hosted=None):
    D = srcs[-1].shape[1]
    Tr = sum(s.shape[0] for s in srcs)
    Tl = srcs[-1].shape[0]
    nsh, _, nn = ws.shape
    nh = nsh // 2
    tm = ROW_TILE
    two = len(srcs) == 2

    def body(*refs):
        dxo_ref, x_refs = refs[0], refs[1:1 + len(srcs)]
        mod_ref, g_ref, ab_ref, fo_ref, ws_ref, wout_ref, dx_ref, dab_ref, df_ref, acc_ref = refs[1 + len(srcs):]
        i = pl.program_id(0)

        @pl.when((i == 0) | (i == nctx))
        def _():
            acc_ref[...] = jnp.zeros_like(acc_ref)

        dxo = dxo_ref[...]
        x = jnp.where(i < nctx, x_refs[0][...], x_refs[1][...]) if two else x_refs[0][...]
        ms = mod_ref[0]
        scale, gate = ms[1:2], ms[2:3]
        gg = g_ref[...]
        dgate = jnp.sum(dxo * fo_ref[...].astype(F32), axis=0, keepdims=True) * 0.5
        dfb = (dxo * (0.5 * gate)).astype(BF)
        df_ref[...] = dfb
        dhm = jnp.zeros((tm, D), F32)
        for j in range(nh):
            dh = lax.dot_general(dfb, wout_ref[j * nn:(j + 1) * nn, :], NT, preferred_element_type=F32)
            a = ab_ref[j].astype(F32)
            b = ab_ref[nh + j].astype(F32)
            sg = _sigmoid(a)
            da = ((dh * b) * (sg * (1.0 + a * (1.0 - sg)))).astype(BF)
            db = (dh * (a * sg)).astype(BF)
            dab_ref[j] = da
            dab_ref[nh + j] = db
            dhm = dhm + lax.dot_general(da, ws_ref[j], NT, preferred_element_type=F32)
            dhm = dhm + lax.dot_general(db, ws_ref[nh + j], NT, preferred_element_type=F32)
        r = lax.rsqrt(jnp.mean(x * x, axis=-1, keepdims=True) + EPS)
        xh = x * r
        dshift = jnp.sum(dhm, axis=0, keepdims=True)
        dscale = jnp.sum(dhm * (xh * gg), axis=0, keepdims=True)
        dxh_g = dhm * (1.0 + scale)
        dg = jnp.sum(dxh_g * xh, axis=0, keepdims=True)
        dxh = dxh_g * gg
        dx_ref[...] = dxo + r * (dxh - xh * jnp.mean(dxh * xh, axis=-1, keepdims=True))
        for k, val in enumerate((dshift, dscale, dgate, dg)):
            acc_ref[0, k:k + 1, :] += val

    src_specs = _two_stream_specs(tm, D, nctx) if two else [_row(tm, D)]
    dx_spec = pl.BlockSpec((tm, D), lambda i: (jnp.maximum(i - nctx, 0), 0))
    return _call(
        body, name=name, grid=(Tr // tm,),
        in_specs=[_row(tm, D)] + src_specs + [_mod_spec(D, which, nctx), _resident((1, D)), _stk(nsh, tm, nn), _row(tm, D),
                                              _resident(ws.shape), _resident(w_out.shape)],
        out_specs=[dx_spec, _stk(nsh, tm, nn), _row(tm, D), _acc_spec(D, nctx)],
        out_shape=[jax.ShapeDtypeStruct((Tl, D), F32), jax.ShapeDtypeStruct((nsh, Tr, nn), BF),
                   jax.ShapeDtypeStruct((Tr, D), BF), jax.ShapeDtypeStruct((2, 8, D), F32)],
        operands=[dxo, *srcs, mod6, g, ab, fo, ws, w_out], hosted=hosted,
        params=_params(vmem=VMEM_BIG, sem=("arbitrary",)))


def _token_tile(T):
    return _pick(T, (1024, 768, 512, 256, 128))


def tn_matmul(a, b, name):
    T, K = a.shape
    N = b.shape[1]
    tk = _pick(K, (1024, 1408, 1664, 768, 512, 384, 256, 128))
    tn = _pick(N, (1024, 1408, 1664, 768, 512, 384, 256, 128))
    tt = _token_tile(T)
    nt = T // tt

    def body(a_ref, b_ref, o_ref, acc_ref):
        t = pl.program_id(2)

        @pl.when(t == 0)
        def _():
            acc_ref[...] = jnp.zeros_like(acc_ref)

        acc_ref[...] += lax.dot_general(a_ref[...], b_ref[...], TN, preferred_element_type=F32)

        @pl.when(t == nt - 1)
        def _():
            o_ref[...] = acc_ref[...].astype(BF)

    return pl.pallas_call(
        body, name=name, grid=(K // tk, N // tn, nt),
        in_specs=[pl.BlockSpec((tt, tk), lambda k, n, t: (t, k)), pl.BlockSpec((tt, tn), lambda k, n, t: (t, n))],
        out_specs=pl.BlockSpec((tk, tn), lambda k, n, t: (k, n)),
        out_shape=jax.ShapeDtypeStruct((K, N), BF),
        scratch_shapes=[pltpu.VMEM((tk, tn), F32)],
        compiler_params=_params(vmem=VMEM_BIG, sem=("parallel", "parallel", "arbitrary")),
    )(a, b)


def tn_ffn_in(hm, dab, name, hosted=None):
    T, D = hm.shape
    nsh, _, nn = dab.shape
    per = nsh // 2
    tt = _token_tile(T)
    nt = T // tt

    def body(a_ref, b_ref, o_ref, acc_ref):
        t = pl.program_id(1)

        @pl.when(t == 0)
        def _():
            acc_ref[...] = jnp.zeros_like(acc_ref)

        a = a_ref[...]
        for j in range(per):
            acc_ref[j] += lax.dot_general(b_ref[j], a, TN, preferred_element_type=F32)

        @pl.when(t == nt - 1)
        def _():
            o_ref[...] = acc_ref[...].astype(BF)

    (out,), exchanged = _call(
        body, name=name, grid=(2, nt),
        in_specs=[pl.BlockSpec((tt, D), lambda s, t: (t, 0)), pl.BlockSpec((per, tt, nn), lambda s, t: (s, t, 0))],
        out_specs=[pl.BlockSpec((per, nn, D), lambda s, t: (s, 0, 0))],
        out_shape=[jax.ShapeDtypeStruct((nsh, nn, D), BF)],
        scratch_shapes=[pltpu.VMEM((per, nn, D), F32)],
        operands=[hm, dab], hosted=hosted,
        params=_params(vmem=VMEM_BIG, sem=("arbitrary", "arbitrary")))
    return out, exchanged


def tn_ffn_out(h, df, name):
    nh, T, nn = h.shape
    D = df.shape[1]
    per = nh // 2
    tt = _token_tile(T)
    nt = T // tt

    def body(a_ref, b_ref, o_ref, acc_ref):
        t = pl.program_id(1)

        @pl.when(t == 0)
        def _():
            acc_ref[...] = jnp.zeros_like(acc_ref)

        b = b_ref[...]
        for j in range(per):
            acc_ref[j] += lax.dot_general(a_ref[j], b, TN, preferred_element_type=F32)

        @pl.when(t == nt - 1)
        def _():
            for j in range(per):
                o_ref[j * nn:(j + 1) * nn, :] = acc_ref[j].astype(BF)

    return pl.pallas_call(
        body, name=name, grid=(2, nt),
        in_specs=[pl.BlockSpec((per, tt, nn), lambda s, t: (s, t, 0)), pl.BlockSpec((tt, D), lambda s, t: (t, 0))],
        out_specs=pl.BlockSpec((per * nn, D), lambda s, t: (s, 0)),
        out_shape=jax.ShapeDtypeStruct((nh * nn, D), BF),
        scratch_shapes=[pltpu.VMEM((per, nn, D), F32)],
        compiler_params=_params(vmem=VMEM_BIG, sem=("parallel", "arbitrary")),
    )(h, df)


def _rope_apply(y, cos, s_next, s_prev):
    return y * cos + pltpu.roll(y, HEAD_DIM - 32, 1) * s_next + pltpu.roll(y, 32, 1) * s_prev


def _rope_transpose(dz, cos, s_next, s_prev):
    return dz * cos + pltpu.roll(dz * s_next, 32, 1) + pltpu.roll(dz * s_prev, HEAD_DIM - 32, 1)


def proj_forward(xa, mod6, g, w_in, qg, kg, tabs, offs, nctx, name, hosted=None):
    Tr, D = xa.shape
    P = w_in.shape[1]
    tm = ROW_TILE
    qo, ko = offs["q"], offs["k"]
    qw, kw = N_Q_HEADS * HEAD_DIM, N_KV_HEADS * HEAD_DIM
    scale_q = HEAD_DIM ** -0.5 * LOG2E

    def body(x_ref, mod_ref, g_ref, w_ref, qg_ref, kg_ref, tab_ref, hx_ref, pr_ref, q_ref, k_ref):
        x = x_ref[...]
        ms = mod_ref[0]
        shift, scale = ms[0:1], ms[1:2]
        r = lax.rsqrt(jnp.mean(x * x, axis=-1, keepdims=True) + EPS)
        hb = (((x * r) * g_ref[...]) * (1.0 + scale) + shift).astype(BF)
        hx_ref[...] = hb
        pr = jnp.dot(hb, w_ref[...], preferred_element_type=F32)
        pr_ref[...] = pr.astype(BF)
        cos, s_next, s_prev = tab_ref[0], tab_ref[1], tab_ref[2]

        def head(v, gain):
            n = v * lax.rsqrt(jnp.mean(v * v, axis=-1, keepdims=True) + EPS)
            return _rope_apply(n * gain, cos, s_next, s_prev)

        for h in range(N_Q_HEADS):
            lo = qo + h * HEAD_DIM
            q_ref[:, h * HEAD_DIM:(h + 1) * HEAD_DIM] = (head(pr[:, lo:lo + HEAD_DIM], qg_ref[...]) * scale_q).astype(BF)
        for h in range(N_KV_HEADS):
            lo = ko + h * HEAD_DIM
            k_ref[:, h * HEAD_DIM:(h + 1) * HEAD_DIM] = head(pr[:, lo:lo + HEAD_DIM], kg_ref[...]).astype(BF)

    return _call(
        body, name=name, grid=(Tr // tm,),
        in_specs=[_row(tm, D), _mod_spec(D, 1, nctx), _resident((1, D)), _resident(w_in.shape),
                  _resident((1, HEAD_DIM)), _resident((1, HEAD_DIM)),
                  pl.BlockSpec((3, tm, HEAD_DIM), lambda i: (0, i, 0))],
        out_specs=[_row(tm, D), _row(tm, P), _row(tm, qw), _row(tm, kw)],
        out_shape=[jax.ShapeDtypeStruct((Tr, D), BF), jax.ShapeDtypeStruct((Tr, P), BF),
                   jax.ShapeDtypeStruct((Tr, qw), BF), jax.ShapeDtypeStruct((Tr, kw), BF)],
        operands=[xa, mod6, g, w_in, qg, kg, tabs], hosted=hosted,
        params=_params(vmem=VMEM_BIG, sem=("arbitrary",)))


def _shifted(u, first_row, last_row):
    T = u.shape[0]
    prev = jnp.where(first_row, 0.0, pltpu.roll(u, 1, 0))
    nxt = jnp.where(last_row, 0.0, pltpu.roll(u, T - 1, 0))
    return prev, nxt


def conv_forward(proj, conv_w, offs, Tc, name):
    Ta = proj.shape[0]
    T = Ta - Tc
    Dc = conv_w.shape[1]
    cb = offs["cv"] // 384

    def body(p_ref, w_ref, y_ref):
        rows = lax.broadcasted_iota(jnp.int32, (T, 128), 0)
        u = p_ref[pl.ds(Tc, T), 128:256].astype(F32) * p_ref[pl.ds(Tc, T), 256:384].astype(F32)
        prev, nxt = _shifted(u, rows == 0, rows == T - 1)
        w = w_ref[...]
        cv = prev * w[0:1] + u * w[1:2] + nxt * w[2:3]
        y_ref[...] = (p_ref[pl.ds(Tc, T), 0:128].astype(F32) * cv).astype(BF)

    return pl.pallas_call(
        body, name=name, grid=(Dc // 128,),
        in_specs=[pl.BlockSpec((Ta, 384), lambda j: (0, cb + j)), pl.BlockSpec((CONV_TAPS, 128), lambda j: (0, j))],
        out_specs=pl.BlockSpec((T, 128), lambda j: (0, j)),
        out_shape=jax.ShapeDtypeStruct((T, Dc), BF),
        compiler_params=_params(vmem=VMEM_BIG, sem=("arbitrary",)),
    )(proj, conv_w)


def conv_backward(dproj, dy, proj, conv_w, offs, Tc, name):
    Ta = proj.shape[0]
    T = Ta - Tc
    Dc = conv_w.shape[1]
    cb = offs["cv"] // 384

    def body(dp_any, dy_ref, p_ref, w_ref, o_ref, dw_ref):
        rows = lax.broadcasted_iota(jnp.int32, (T, 128), 0)
        first, last = rows == 0, rows == T - 1
        bg = p_ref[pl.ds(Tc, T), 0:128].astype(F32)
        cg = p_ref[pl.ds(Tc, T), 128:256].astype(F32)
        vc = p_ref[pl.ds(Tc, T), 256:384].astype(F32)
        dy = dy_ref[...].astype(F32)
        u = cg * vc
        prev, nxt = _shifted(u, first, last)
        w = w_ref[...]
        cv = prev * w[0:1] + u * w[1:2] + nxt * w[2:3]
        o_ref[pl.ds(0, Tc), :] = jnp.zeros((Tc, 384), BF)
        o_ref[pl.ds(Tc, T), 0:128] = (dy * cv).astype(BF)
        dcv = dy * bg
        dprev, dnxt = _shifted(dcv, first, last)
        du = dnxt * w[0:1] + dcv * w[1:2] + dprev * w[2:3]
        o_ref[pl.ds(Tc, T), 128:256] = (du * vc).astype(BF)
        o_ref[pl.ds(Tc, T), 256:384] = (du * cg).astype(BF)
        dw_ref[...] = jnp.zeros_like(dw_ref)
        for k, tap in enumerate((prev, u, nxt)):
            dw_ref[k:k + 1, :] = jnp.sum(dcv * tap, axis=0, keepdims=True)

    blk = pl.BlockSpec((Ta, 384), lambda j: (0, cb + j))
    return pl.pallas_call(
        body, name=name, grid=(Dc // 128,),
        in_specs=[ANY, pl.BlockSpec((T, 128), lambda j: (0, j)), blk, pl.BlockSpec((CONV_TAPS, 128), lambda j: (0, j))],
        out_specs=[blk, pl.BlockSpec((8, 128), lambda j: (0, j))],
        out_shape=[jax.ShapeDtypeStruct(dproj.shape, BF), jax.ShapeDtypeStruct((8, Dc), F32)],
        input_output_aliases={0: 0},
        compiler_params=_params(vmem=VMEM_BIG, sem=("arbitrary",)),
    )(dproj, dy, proj, conv_w)


def _kv_chunk(Ta):
    return _pick(Ta, (768, 512, 384, 256, 128))


def _stack_heads(v):
    return jnp.concatenate([v[:, h * HEAD_DIM:(h + 1) * HEAD_DIM] for h in range(GROUP)], axis=0)


def attention_forward(q, k, proj, offs, Tc, name, hosted=None):
    Ta = k.shape[0]
    T = Ta - Tc
    tq = ROW_TILE
    kc = _kv_chunk(Ta)
    nkv = Ta // kc
    gw = GROUP * HEAD_DIM
    vblk = offs["v"] // HEAD_DIM
    qoff = Tc // tq
    n = GROUP * tq

    def body(q_ref, k_ref, v_ref, o_ref, lse_ref, vx_ref, qs_ref, s0_ref, s1_ref, m_ref, acc_ref):
        @pl.when(pl.program_id(1) == 0)
        def _():
            vx_ref[:, 0:HEAD_DIM] = v_ref[...]
            vx_ref[:, HEAD_DIM:2 * HEAD_DIM] = jnp.ones((Ta, HEAD_DIM), BF)

        qs_ref[...] = _stack_heads(q_ref[...])
        m_ref[...] = jnp.full((n, 1), -1e30, F32)
        acc_ref[...] = jnp.zeros((n, 2 * HEAD_DIM), F32)

        def rows(c):
            return pl.ds(pl.multiple_of(c * kc, kc), kc)

        def logits(c, dst):
            dst[...] = lax.dot_general(qs_ref[...], k_ref[rows(c), :], NT, preferred_element_type=F32)

        def consume(src, c):
            s = src[...]
            m_prev = m_ref[...]
            m_new = jnp.maximum(m_prev, jnp.max(s, axis=-1, keepdims=True))
            p = jnp.exp2(s - m_new).astype(BF)
            acc_ref[...] = jnp.exp2(m_prev - m_new) * acc_ref[...] + jnp.dot(p, vx_ref[rows(c), :], preferred_element_type=F32)
            m_ref[...] = m_new

        def pair(i, carry):
            logits(2 * i + 1, s1_ref)
            consume(s0_ref, 2 * i)
            logits(2 * i + 2, s0_ref)
            consume(s1_ref, 2 * i + 1)
            return carry

        logits(0, s0_ref)
        if nkv % 2:
            lax.fori_loop(0, nkv // 2, pair, 0)
            consume(s0_ref, nkv - 1)
        else:
            lax.fori_loop(0, nkv // 2 - 1, pair, 0)
            logits(nkv - 1, s1_ref)
            consume(s0_ref, nkv - 2)
            consume(s1_ref, nkv - 1)
        acc = acc_ref[...]
        l = acc[:, HEAD_DIM:HEAD_DIM + 1]
        o = acc[:, 0:HEAD_DIM] / l
        lse = m_ref[...] + jnp.log2(l)
        for h in range(GROUP):
            o_ref[:, h * HEAD_DIM:(h + 1) * HEAD_DIM] = o[h * tq:(h + 1) * tq].astype(BF)
            lse_ref[0, :, h:h + 1] = lse[h * tq:(h + 1) * tq]

    return _call(
        body, name=name, grid=(N_KV_HEADS, T // tq),
        in_specs=[pl.BlockSpec((tq, gw), lambda j, i: (i + qoff, j)),
                  pl.BlockSpec((Ta, HEAD_DIM), lambda j, i: (0, j)),
                  pl.BlockSpec((Ta, HEAD_DIM), lambda j, i: (0, vblk + j))],
        out_specs=[pl.BlockSpec((tq, gw), lambda j, i: (i, j)),
                   pl.BlockSpec((1, tq, GROUP), lambda j, i: (j, i, 0))],
        out_shape=[jax.ShapeDtypeStruct((T, N_Q_HEADS * HEAD_DIM), BF),
                   jax.ShapeDtypeStruct((N_KV_HEADS, T, GROUP), F32)],
        scratch_shapes=[pltpu.VMEM((Ta, 2 * HEAD_DIM), BF), pltpu.VMEM((n, HEAD_DIM), BF), pltpu.VMEM((n, kc), F32),
                        pltpu.VMEM((n, kc), F32), pltpu.VMEM((n, 1), F32), pltpu.VMEM((n, 2 * HEAD_DIM), F32)],
        operands=[q, k, proj], hosted=hosted,
        params=_params(vmem=VMEM_BIG, sem=("arbitrary", "arbitrary")))


def attention_backward(q, k, proj, o, lse, do, offs, Tc, name, hosted=None):
    Ta = k.shape[0]
    T = Ta - Tc
    tq = ROW_TILE
    kc = _kv_chunk(Ta)
    gw = GROUP * HEAD_DIM
    vblk = offs["v"] // HEAD_DIM
    qoff = Tc // tq

    def body(q_ref, k_ref, v_ref, o_ref, lse_ref, do_ref, dq_ref, dk_ref, dv_ref):
        @pl.when(pl.program_id(1) == 0)
        def _():
            dk_ref[...] = jnp.zeros_like(dk_ref)
            dv_ref[...] = jnp.zeros_like(dv_ref)

        qs = _stack_heads(q_ref[...])
        dob = do_ref[...]
        dos = _stack_heads(dob)
        delta = jnp.concatenate(
            [jnp.sum(dob[:, h * HEAD_DIM:(h + 1) * HEAD_DIM].astype(F32)
                     * o_ref[:, h * HEAD_DIM:(h + 1) * HEAD_DIM].astype(F32), axis=-1, keepdims=True)
             for h in range(GROUP)], axis=0)
        lse = jnp.concatenate([lse_ref[0, :, h:h + 1] for h in range(GROUP)], axis=0)

        def step(c, dq):
            rows = pl.ds(pl.multiple_of(c * kc, kc), kc)
            kk = k_ref[rows, :]
            vv = v_ref[rows, :]
            s = lax.dot_general(qs, kk, NT, preferred_element_type=F32)
            p = jnp.exp2(s - lse)
            dp = lax.dot_general(dos, vv, NT, preferred_element_type=F32)
            ds = (p * (dp - delta)).astype(BF)
            dv_ref[rows, :] += lax.dot_general(p.astype(BF), dos, TN, preferred_element_type=F32)
            dk_ref[rows, :] += lax.dot_general(ds, qs, TN, preferred_element_type=F32)
            return dq + jnp.dot(ds, kk, preferred_element_type=F32)

        dq = lax.fori_loop(0, Ta // kc, step, jnp.zeros((GROUP * tq, HEAD_DIM), F32))
        for h in range(GROUP):
            dq_ref[:, h * HEAD_DIM:(h + 1) * HEAD_DIM] = dq[h * tq:(h + 1) * tq]

    return _call(
        body, name=name, grid=(N_KV_HEADS, T // tq),
        in_specs=[pl.BlockSpec((tq, gw), lambda j, i: (i + qoff, j)),
                  pl.BlockSpec((Ta, HEAD_DIM), lambda j, i: (0, j)),
                  pl.BlockSpec((Ta, HEAD_DIM), lambda j, i: (0, vblk + j)),
                  pl.BlockSpec((tq, gw), lambda j, i: (i, j)),
                  pl.BlockSpec((1, tq, GROUP), lambda j, i: (j, i, 0)),
                  pl.BlockSpec((tq, gw), lambda j, i: (i, j))],
        out_specs=[pl.BlockSpec((tq, gw), lambda j, i: (i, j)),
                   pl.BlockSpec((Ta, HEAD_DIM), lambda j, i: (0, j)),
                   pl.BlockSpec((Ta, HEAD_DIM), lambda j, i: (0, j))],
        out_shape=[jax.ShapeDtypeStruct((T, N_Q_HEADS * HEAD_DIM), F32),
                   jax.ShapeDtypeStruct((Ta, N_KV_HEADS * HEAD_DIM), F32),
                   jax.ShapeDtypeStruct((Ta, N_KV_HEADS * HEAD_DIM), F32)],
        operands=[q, k, proj, o, lse, do], hosted=hosted,
        params=_params(vmem=VMEM_BIG, sem=("arbitrary", "arbitrary")))


def _norm_rope_backward(dz, raw, gg, cos, s_next, s_prev):
    r = lax.rsqrt(jnp.mean(raw * raw, axis=-1, keepdims=True) + EPS)
    n = raw * r
    dy = _rope_transpose(dz, cos, s_next, s_prev)
    dn = dy * gg
    return r * (dn - n * jnp.mean(dn * n, axis=-1, keepdims=True)), jnp.sum(dy * n, axis=0, keepdims=True)


def q_norm_backward(dproj, dq, proj, gain, tabs, offs, nctx, name):
    Ta = proj.shape[0]
    tm = ROW_TILE
    w = N_Q_HEADS * HEAD_DIM
    cb = offs["q"] // w
    zscale = HEAD_DIM ** -0.5

    def body(dp_any, dz_ref, raw_ref, g_ref, tab_ref, o_ref, dg_ref):
        i = pl.program_id(0)

        @pl.when(i == 0)
        def _():
            dg_ref[...] = jnp.zeros_like(dg_ref)

        @pl.when(i < nctx)
        def _():
            o_ref[...] = jnp.zeros_like(o_ref)

        @pl.when(i >= nctx)
        def _():
            cos, s_next, s_prev = tab_ref[0], tab_ref[1], tab_ref[2]
            dg = jnp.zeros((1, HEAD_DIM), F32)
            for h in range(N_Q_HEADS):
                sl = slice(h * HEAD_DIM, (h + 1) * HEAD_DIM)
                dv, dgh = _norm_rope_backward(dz_ref[:, sl] * zscale, raw_ref[:, sl].astype(F32), g_ref[...], cos, s_next, s_prev)
                o_ref[:, sl] = dv.astype(BF)
                dg = dg + dgh
            dg_ref[0:1, :] += dg

    return pl.pallas_call(
        body, name=name, grid=(Ta // tm,),
        in_specs=[ANY, pl.BlockSpec((tm, w), lambda i: (jnp.maximum(i - nctx, 0), 0)),
                  pl.BlockSpec((tm, w), lambda i: (i, cb)), _resident((1, HEAD_DIM)),
                  pl.BlockSpec((3, tm, HEAD_DIM), lambda i: (0, i, 0))],
        out_specs=[pl.BlockSpec((tm, w), lambda i: (i, cb)), pl.BlockSpec((8, HEAD_DIM), lambda i: (0, 0))],
        out_shape=[jax.ShapeDtypeStruct(dproj.shape, BF), jax.ShapeDtypeStruct((8, HEAD_DIM), F32)],
        input_output_aliases={0: 0},
        compiler_params=_params(sem=("arbitrary",)),
    )(dproj, dq, proj, gain, tabs)


def kv_backward(dproj, dk, dv, proj, gain, tabs, offs, name):
    Ta = proj.shape[0]
    tm = ROW_TILE
    kw = N_KV_HEADS * HEAD_DIM
    cb = offs["k"] // (2 * kw)
    kb = offs["k"] // kw
    zscale = 1.0 / LOG2E

    def body(dp_any, dk_ref, dv_ref, raw_ref, g_ref, tab_ref, o_ref, dg_ref):
        @pl.when(pl.program_id(0) == 0)
        def _():
            dg_ref[...] = jnp.zeros_like(dg_ref)

        cos, s_next, s_prev = tab_ref[0], tab_ref[1], tab_ref[2]
        dg = jnp.zeros((1, HEAD_DIM), F32)
        for h in range(N_KV_HEADS):
            sl = slice(h * HEAD_DIM, (h + 1) * HEAD_DIM)
            dr, dgh = _norm_rope_backward(dk_ref[:, sl] * zscale, raw_ref[:, sl].astype(F32), g_ref[...], cos, s_next, s_prev)
            o_ref[:, sl] = dr.astype(BF)
            dg = dg + dgh
        o_ref[:, kw:2 * kw] = dv_ref[...].astype(BF)
        dg_ref[0:1, :] += dg

    return pl.pallas_call(
        body, name=name, grid=(Ta // tm,),
        in_specs=[ANY, _row(tm, kw), _row(tm, kw), pl.BlockSpec((tm, kw), lambda i: (i, kb)),
                  _resident((1, HEAD_DIM)), pl.BlockSpec((3, tm, HEAD_DIM), lambda i: (0, i, 0))],
        out_specs=[pl.BlockSpec((tm, 2 * kw), lambda i: (i, cb)), pl.BlockSpec((8, HEAD_DIM), lambda i: (0, 0))],
        out_shape=[jax.ShapeDtypeStruct(dproj.shape, BF), jax.ShapeDtypeStruct((8, HEAD_DIM), F32)],
        input_output_aliases={0: 0},
        compiler_params=_params(sem=("arbitrary",)),
    )(dproj, dk, dv, proj, gain, tabs)


def merge_forward(x1, mod6, yc, o, proj, w_bc, w_ba, w_o, offs, Tc, name):
    T, D = yc.shape[0], x1.shape[1]
    tm = ROW_TILE
    roff = Tc // tm
    gb = offs["gt"] // (2 * D)

    def body(x_ref, mod_ref, yc_ref, o_ref, gt_ref, wbc_ref, wba_ref, wo_ref, xo_ref, pc_ref, pa_ref, m_ref, z_ref):
        gate = mod_ref[0][2:3]
        pc = jnp.dot(yc_ref[...], wbc_ref[...], preferred_element_type=F32)
        pa = jnp.dot(o_ref[...], wba_ref[...], preferred_element_type=F32)
        pc_ref[...] = pc.astype(BF)
        pa_ref[...] = pa.astype(BF)
        mb = (_sigmoid(gt_ref[:, 0:D].astype(F32)) * pc + _sigmoid(gt_ref[:, D:2 * D].astype(F32)) * pa).astype(BF)
        m_ref[...] = mb
        z = jnp.dot(mb, wo_ref[...], preferred_element_type=F32)
        z_ref[...] = z.astype(BF)
        xo_ref[...] = x_ref[...] + gate * z

    return pl.pallas_call(
        body, name=name, grid=(T // tm,),
        in_specs=[pl.BlockSpec((tm, D), lambda i: (i + roff, 0)), _mod_spec(D, 1, 0), _row(tm, yc.shape[1]), _row(tm, o.shape[1]),
                  pl.BlockSpec((tm, 2 * D), lambda i: (i + roff, gb)),
                  _resident(w_bc.shape), _resident(w_ba.shape), _resident(w_o.shape)],
        out_specs=[_row(tm, D)] * 5,
        out_shape=[jax.ShapeDtypeStruct((T, D), F32)] + [jax.ShapeDtypeStruct((T, D), BF)] * 4,
        compiler_params=_params(vmem=VMEM_BIG, sem=("arbitrary",)),
    )(x1, mod6, yc, o, proj, w_bc, w_ba, w_o)


def merge_backward_rows(dx2, mod6, z, pc, pa, proj, w_bc, w_ba, w_o, offs, Tc, name):
    T, D = dx2.shape
    Ta, P = proj.shape
    tm = ROW_TILE
    nctx = Tc // tm
    gb = offs["gt"] // (2 * D)
    dcw, dqw = w_bc.shape[0], w_ba.shape[0]

    def body(dx_ref, mod_ref, z_ref, pc_ref, pa_ref, gt_ref, wbc_ref, wba_ref, wo_ref,
             dgt_ref, dg_ref, dpc_ref, dpa_ref, dyc_ref, do_ref, acc_ref):
        i = pl.program_id(0)

        @pl.when(i == 0)
        def _():
            acc_ref[...] = jnp.zeros_like(acc_ref)

        @pl.when(i < nctx)
        def _():
            dgt_ref[...] = jnp.zeros_like(dgt_ref)

        @pl.when(i >= nctx)
        def _():
            gate = mod_ref[0][2:3]
            dx = dx_ref[...]
            acc_ref[0:1, :] += jnp.sum(dx * z_ref[...].astype(F32), axis=0, keepdims=True)
            dgb = (dx * gate).astype(BF)
            dg_ref[...] = dgb
            dm = lax.dot_general(dgb, wo_ref[...], NT, preferred_element_type=F32)
            sc = _sigmoid(gt_ref[:, 0:D].astype(F32))
            sa = _sigmoid(gt_ref[:, D:2 * D].astype(F32))
            pc = pc_ref[...].astype(F32)
            pa = pa_ref[...].astype(F32)
            dpc = (dm * sc).astype(BF)
            dpa = (dm * sa).astype(BF)
            dpc_ref[...] = dpc
            dpa_ref[...] = dpa
            dgt_ref[:, 0:D] = ((dm * pc) * (sc * (1.0 - sc))).astype(BF)
            dgt_ref[:, D:2 * D] = ((dm * pa) * (sa * (1.0 - sa))).astype(BF)
            dyc_ref[...] = lax.dot_general(dpc, wbc_ref[...], NT, preferred_element_type=F32).astype(BF)
            do_ref[...] = lax.dot_general(dpa, wba_ref[...], NT, preferred_element_type=F32).astype(BF)

    lat = lambda n: pl.BlockSpec((tm, n), lambda i: (jnp.maximum(i - nctx, 0), 0))
    return pl.pallas_call(
        body, name=name, grid=(Ta // tm,),
        in_specs=[lat(D), _mod_spec(D, 1, 0), lat(D), lat(D), lat(D),
                  pl.BlockSpec((tm, 2 * D), lambda i: (i, gb)),
                  _resident(w_bc.shape), _resident(w_ba.shape), _resident(w_o.shape)],
        out_specs=[pl.BlockSpec((tm, 2 * D), lambda i: (i, gb)), lat(D), lat(D), lat(D), lat(dcw), lat(dqw),
                   pl.BlockSpec((8, D), lambda i: (0, 0))],
        out_shape=[jax.ShapeDtypeStruct((Ta, P), BF)] + [jax.ShapeDtypeStruct((T, D), BF)] * 3
                  + [jax.ShapeDtypeStruct((T, dcw), BF), jax.ShapeDtypeStruct((T, dqw), BF), jax.ShapeDtypeStruct((8, D), F32)],
        compiler_params=_params(vmem=VMEM_BIG, sem=("arbitrary",)),
    )(dx2, mod6, z, pc, pa, proj, w_bc, w_ba, w_o)


def proj_backward_rows(dproj, dres, xa, mod6, g, w_in, nctx, name, hosted=None):
    Tr, D = xa.shape
    P = w_in.shape[1]
    tm = ROW_TILE

    def body(dp_ref, dres_ref, x_ref, mod_ref, g_ref, w_ref, dx_ref, acc_ref):
        i = pl.program_id(0)

        @pl.when((i == 0) | (i == nctx))
        def _():
            acc_ref[...] = jnp.zeros_like(acc_ref)

        x = x_ref[...]
        scale = mod_ref[0][1:2]
        gg = g_ref[...]
        dhm = lax.dot_general(dp_ref[...], w_ref[...], NT, preferred_element_type=F32)
        r = lax.rsqrt(jnp.mean(x * x, axis=-1, keepdims=True) + EPS)
        xh = x * r
        dshift = jnp.sum(dhm, axis=0, keepdims=True)
        dscale = jnp.sum(dhm * (xh * gg), axis=0, keepdims=True)
        dxh_g = dhm * (1.0 + scale)
        dg = jnp.sum(dxh_g * xh, axis=0, keepdims=True)
        dxh = dxh_g * gg
        res = jnp.where(i < nctx, 0.0, dres_ref[...])
        dx_ref[...] = res + r * (dxh - xh * jnp.mean(dxh * xh, axis=-1, keepdims=True))
        for k, val in enumerate((dshift, dscale, dg)):
            acc_ref[0, k:k + 1, :] += val

    return _call(
        body, name=name, grid=(Tr // tm,),
        in_specs=[_row(tm, P), pl.BlockSpec((tm, D), lambda i: (jnp.maximum(i - nctx, 0), 0)), _row(tm, D),
                  _mod_spec(D, 1, nctx), _resident((1, D)), _resident(w_in.shape)],
        out_specs=[_row(tm, D), _acc_spec(D, nctx)],
        out_shape=[jax.ShapeDtypeStruct((Tr, D), F32), jax.ShapeDtypeStruct((2, 8, D), F32)],
        operands=[dproj, dres, xa, mod6, g, w_in], hosted=hosted,
        params=_params(vmem=VMEM_BIG, sem=("arbitrary",)))


def loss_and_final_norm_backward(x3, tgt, gf, name):
    T, D = x3.shape
    tm = ROW_TILE

    def body(x_ref, t_ref, g_ref, dx_ref, acc_ref):
        @pl.when(pl.program_id(0) == 0)
        def _():
            acc_ref[...] = jnp.zeros_like(acc_ref)

        x = x_ref[...]
        gg = g_ref[...]
        r = lax.rsqrt(jnp.mean(x * x, axis=-1, keepdims=True) + EPS)
        xh = x * r
        e = xh * gg - t_ref[...]
        part = 0.5 * jnp.sum(jnp.mean(e * e, axis=-1, keepdims=True), axis=0, keepdims=True)
        dy = e * (1.0 / D)
        dyg = dy * gg
        dx_ref[...] = r * (dyg - xh * jnp.mean(dyg * xh, axis=-1, keepdims=True))
        acc_ref[0:1, :] += jnp.sum(dy * xh, axis=0, keepdims=True)
        acc_ref[1:2, :] += jnp.broadcast_to(part, (1, D))

    return pl.pallas_call(
        body, name=name, grid=(T // tm,),
        in_specs=[_row(tm, D), _row(tm, D), _resident((1, D))],
        out_specs=[_row(tm, D), pl.BlockSpec((8, D), lambda i: (0, 0))],
        out_shape=[jax.ShapeDtypeStruct((T, D), F32), jax.ShapeDtypeStruct((8, D), F32)],
        compiler_params=_params(sem=("arbitrary",)),
    )(x3, tgt, gf)


def _adam_update(w, g, m, v):
    c1 = 1.0 - ADAM_B1 ** ADAM_STEP
    c2 = 1.0 - ADAM_B2 ** ADAM_STEP
    m = ADAM_B1 * m + (1.0 - ADAM_B1) * g
    v = ADAM_B2 * v + (1.0 - ADAM_B2) * (g * g)
    return -ADAM_LR * ((m / c1) / (jnp.sqrt(v / c2) + ADAM_EPS) + ADAM_WD * w), m, v


def adamw(w, g, m, v, name):
    R, C = w.shape
    tr = _row_tile(R, C)

    def body(w_ref, g_ref, m_ref, v_ref, d_ref, nm_ref, nv_ref):
        d_ref[...], nm_ref[...], nv_ref[...] = _adam_update(w_ref[...], g_ref[...], m_ref[...], v_ref[...])

    blk = pl.BlockSpec((tr, C), lambda i: (i, 0))
    return pl.pallas_call(
        body, name=name, grid=(R // tr,),
        in_specs=[blk] * 4, out_specs=[blk] * 3,
        out_shape=[jax.ShapeDtypeStruct((R, C), F32)] * 3,
        compiler_params=_params(sem=("parallel",)),
    )(w, g, m, v)


def adamw_summed(recv, w, m, v, name):
    R, C = w.shape
    tr = _row_tile(R, C)

    def body(r_ref, w_ref, m_ref, v_ref, g_ref, d_ref, nm_ref, nv_ref):
        g = r_ref[0].astype(F32)
        for a in range(1, N_DEV):
            g = g + r_ref[a].astype(F32)
        g_ref[...] = g
        d_ref[...], nm_ref[...], nv_ref[...] = _adam_update(w_ref[...], g, m_ref[...], v_ref[...])

    blk = pl.BlockSpec((tr, C), lambda i: (i, 0))
    return pl.pallas_call(
        body, name=name, grid=(R // tr,),
        in_specs=[pl.BlockSpec((N_DEV, tr, C), lambda i: (0, i, 0)), blk, blk, blk], out_specs=[blk] * 4,
        out_shape=[jax.ShapeDtypeStruct((R, C), F32)] * 4,
        compiler_params=_params(sem=("parallel",)),
    )(recv, w, m, v)


def adamw_reduced(place, ps, recv_b, w, m, v, name):
    R, C = w.shape
    tr = _row_tile(R, C)

    def body(place_ref, p_ref, b_ref, w_ref, m_ref, v_ref, g_ref, d_ref, nm_ref, nv_ref):
        g = p_ref[0].astype(F32)
        for j in range(3):
            g = g + b_ref[j].astype(F32)
        g_ref[...] = g
        d_ref[...], nm_ref[...], nv_ref[...] = _adam_update(w_ref[...], g, m_ref[...], v_ref[...])

    blk = pl.BlockSpec((tr, C), lambda r, pr: (r, 0))
    return pl.pallas_call(
        body, name=name,
        grid_spec=pltpu.PrefetchScalarGridSpec(
            num_scalar_prefetch=1, grid=(R // tr,),
            in_specs=[pl.BlockSpec((1, tr, C), lambda r, pr: (pr[1], r, 0)),
                      pl.BlockSpec((3, tr, C), lambda r, pr: (0, r, 0)), blk, blk, blk],
            out_specs=[blk] * 4),
        out_shape=[jax.ShapeDtypeStruct((R, C), F32)] * 4,
    )(place, ps, recv_b, w, m, v)


def _rope_tables(T, Tc):
    rows = T // GRID_W
    n_freq = HEAD_DIM // 4
    inv = ROPE_THETA ** (-jnp.arange(n_freq, dtype=F32) / n_freq)
    ang_r = jnp.arange(rows).astype(F32)[:, None] * inv
    ang_c = jnp.arange(GRID_W).astype(F32)[:, None] * inv
    per_row = lambda a: jnp.broadcast_to(a[:, None, :], (rows, GRID_W, n_freq)).reshape(T, n_freq)
    per_col = lambda a: jnp.broadcast_to(a[None, :, :], (rows, GRID_W, n_freq)).reshape(T, n_freq)
    cr, sr = per_row(jnp.cos(ang_r)), per_row(jnp.sin(ang_r))
    cc, sc = per_col(jnp.cos(ang_c)), per_col(jnp.sin(ang_c))
    zero = jnp.zeros_like(sr)
    cos = jnp.concatenate([cr, cr, cc, cc], axis=1)
    s_next = jnp.concatenate([-sr, zero, -sc, zero], axis=1)
    s_prev = jnp.concatenate([zero, sr, zero, sc], axis=1)
    lat = jnp.stack([cos, s_next, s_prev])
    ctx = jnp.stack([jnp.ones((Tc, HEAD_DIM), F32), jnp.zeros((Tc, HEAD_DIM), F32), jnp.zeros((Tc, HEAD_DIM), F32)])
    return jnp.concatenate([ctx, lat], axis=1)


BIG = ("ffn1_w_in", "ffn1_w_out", "w_in", "w_branch_conv", "w_branch_attn", "w_out", "ffn2_w_in", "ffn2_w_out")


def _regroup_w_in(stacked, D, Dc, qw, kw):
    w = stacked.transpose(1, 0, 2).reshape(D, -1)
    o = 0
    parts = {}
    for nme, wd in (("bg", Dc), ("cg", Dc), ("vc", Dc), ("q", qw), ("k", kw), ("v", kw), ("gt", 2 * D)):
        parts[nme] = w[:, o:o + wd]
        o += wd
    nb = Dc // 128
    cv = jnp.stack([parts[n].reshape(D, nb, 128) for n in ("bg", "cg", "vc")], axis=2).reshape(D, 3 * Dc)
    return jnp.concatenate([cv, parts["q"], parts["gt"], parts["k"], parts["v"]], axis=1)


def _ungroup_w_in_grad(gt_, D, Dc, qw, kw):
    nb = Dc // 128
    cv = gt_[:3 * Dc].reshape(nb, 3, 128, D)
    o = 3 * Dc
    q = gt_[o:o + qw]
    gt = gt_[o + qw:o + qw + 2 * D]
    k = gt_[o + qw + 2 * D:o + qw + 2 * D + kw]
    v = gt_[o + qw + 2 * D + kw:]
    nat = jnp.concatenate([cv[:, 0].reshape(Dc, D), cv[:, 1].reshape(Dc, D), cv[:, 2].reshape(Dc, D), q, k, v, gt], axis=0)
    return nat.reshape(N_DEV, -1, D)


def kernel(x, c, ctx, c_ctx, w_mod, b_mod, norm1_g, norm2_g, norm3_g, ffn1_w_in, ffn1_w_out, w_in, conv_w, q_norm_g, k_norm_g, w_branch_conv, w_branch_attn, w_out, ffn2_w_in, ffn2_w_out, final_g, loss_target, m_c_ctx, m_w_mod, m_b_mod, m_norm1_g, m_norm2_g, m_norm3_g, m_ffn1_w_in, m_ffn1_w_out, m_w_in, m_conv_w, m_q_norm_g, m_k_norm_g, m_w_branch_conv, m_w_branch_attn, m_w_out, m_ffn2_w_in, m_ffn2_w_out, m_final_g, v_c_ctx, v_w_mod, v_b_mod, v_norm1_g, v_norm2_g, v_norm3_g, v_ffn1_w_in, v_ffn1_w_out, v_w_in, v_conv_w, v_q_norm_g, v_k_norm_g, v_w_branch_conv, v_w_branch_attn, v_w_out, v_ffn2_w_in, v_ffn2_w_out, v_final_g):
    weights = dict(c_ctx=c_ctx, w_mod=w_mod, b_mod=b_mod, norm1_g=norm1_g, norm2_g=norm2_g, norm3_g=norm3_g,
                   ffn1_w_in=ffn1_w_in, ffn1_w_out=ffn1_w_out, w_in=w_in, conv_w=conv_w, q_norm_g=q_norm_g,
                   k_norm_g=k_norm_g, w_branch_conv=w_branch_conv, w_branch_attn=w_branch_attn, w_out=w_out,
                   ffn2_w_in=ffn2_w_in, ffn2_w_out=ffn2_w_out, final_g=final_g)
    moms = dict(c_ctx=(m_c_ctx, v_c_ctx), w_mod=(m_w_mod, v_w_mod), b_mod=(m_b_mod, v_b_mod),
                norm1_g=(m_norm1_g, v_norm1_g), norm2_g=(m_norm2_g, v_norm2_g), norm3_g=(m_norm3_g, v_norm3_g),
                ffn1_w_in=(m_ffn1_w_in, v_ffn1_w_in), ffn1_w_out=(m_ffn1_w_out, v_ffn1_w_out), w_in=(m_w_in, v_w_in),
                conv_w=(m_conv_w, v_conv_w), q_norm_g=(m_q_norm_g, v_q_norm_g), k_norm_g=(m_k_norm_g, v_k_norm_g),
                w_branch_conv=(m_w_branch_conv, v_w_branch_conv), w_branch_attn=(m_w_branch_attn, v_w_branch_attn),
                w_out=(m_w_out, v_w_out), ffn2_w_in=(m_ffn2_w_in, v_ffn2_w_in), ffn2_w_out=(m_ffn2_w_out, v_ffn2_w_out),
                final_g=(m_final_g, v_final_g))
    order = list(weights)

    T, D = x.shape[1], x.shape[2]
    Tc = ctx.shape[1]
    nctx = Tc // ROW_TILE
    nd = N_MOD * D
    Dc = conv_w.shape[2] * N_DEV
    qw, kw = N_Q_HEADS * HEAD_DIM, N_KV_HEADS * HEAD_DIM
    offs, o = {}, 0
    for nme, wd in (("cv", 3 * Dc), ("q", qw), ("gt", 2 * D), ("k", kw), ("v", kw)):
        offs[nme] = o
        o += wd

    ax, ay, ac = lax.axis_index("x"), lax.axis_index("y"), lax.axis_index("c")
    me = 4 * ax + 2 * ay + ac
    place = jnp.stack([ac, 2 * ax + ay]).astype(jnp.int32)

    shard = {n: weights[n][0].astype(BF) for n in BIG}
    rows2d = lambda a: a.reshape(-1, a.shape[-1])
    full = {}
    full["ffn1_w_in"], g_ffn1_out = allgather_two_level([shard["ffn1_w_in"], shard["ffn1_w_out"]], "ag_ffn1")
    full["ffn1_w_out"] = rows2d(g_ffn1_out)

    mod_cols = w_mod.shape[2]
    cw_loc = conv_w[0]
    cpad = (-(D + CONV_TAPS * cw_loc.shape[1])) % 128
    pay = jnp.concatenate([c.reshape(1, D), cw_loc.reshape(1, -1), jnp.zeros((1, cpad), F32)], axis=1)
    call = allgather_direct(pay, "ag_cond")
    conv_full = call[:, 0, D:D + CONV_TAPS * cw_loc.shape[1]].reshape(N_DEV, CONV_TAPS, -1).transpose(1, 0, 2).reshape(CONV_TAPS, Dc)
    b_loc = lax.dynamic_slice_in_dim(b_mod, me * mod_cols, mod_cols, axis=1)
    cctx2 = c_ctx.reshape(1, D)
    mod_part = mod_forward(call, cctx2, w_mod[0], b_loc, "mod_fwd")
    mod_all = allgather_direct(mod_part, "ag_mod")
    mod_lat = lax.dynamic_index_in_dim(mod_all, me, axis=1, keepdims=False).reshape(nd)
    mod_ctx = mod_all[:, N_DEV, :].reshape(nd)
    mod6 = jnp.stack([mod_ctx, mod_lat]).reshape(6, 3, D)

    tabs = _rope_tables(T, Tc)

    srcs1 = (ctx[0], x[0])
    (xa1, hm1, ab1, h1, f1), (g_w_in,) = ffn_forward(
        srcs1, mod6, 0, norm1_g, full["ffn1_w_in"], full["ffn1_w_out"], nctx, "ffn1_fwd",
        hosted=Hosted(gathers=[shard["w_in"]]))
    full["w_in"] = _regroup_w_in(g_w_in, D, Dc, qw, kw)
    merge_names = ("w_branch_conv", "w_branch_attn", "w_out")
    (hx, proj, qr, kr), g_merge = proj_forward(
        xa1, mod6, norm2_g, full["w_in"], q_norm_g, k_norm_g, tabs, offs, nctx, "proj_fwd",
        hosted=Hosted(gathers=[shard[n] for n in merge_names]))
    full.update({n: rows2d(g) for n, g in zip(merge_names, g_merge)})
    yc = conv_forward(proj, conv_full, offs, Tc, "conv_fwd")
    (oa, lse), (full["ffn2_w_in"], g_ffn2_out) = attention_forward(
        qr, kr, proj, offs, Tc, "attn_fwd", hosted=Hosted(gathers=[shard["ffn2_w_in"], shard["ffn2_w_out"]]))
    full["ffn2_w_out"] = rows2d(g_ffn2_out)
    x2, pc, pa, mm, zz = merge_forward(xa1, mod6, yc, oa, proj, full["w_branch_conv"], full["w_branch_attn"],
                                       full["w_out"], offs, Tc, "merge_fwd")
    (x3, hm2, ab2, h2, f2), _ = ffn_forward((x2,), mod6, 2, norm3_g, full["ffn2_w_in"], full["ffn2_w_out"], 0, "ffn2_fwd")
    dx3, lacc = loss_and_final_norm_backward(x3, loss_target[0], final_g.reshape(1, D), "loss_bwd")
    loss = lax.psum(lacc[1, 0], ("x", "y", "c"))

    by_dest = lambda g: g.reshape((N_DEV, -1, g.shape[-1]))
    (dx2, dab2, df2, acc_f2), _ = ffn_backward_rows(dx3, (x2,), mod6, 2, norm3_g, ab2, f2, full["ffn2_w_in"], full["ffn2_w_out"], 0, "ffn2_bwd")
    early = {"ffn2_w_out": tn_ffn_out(h2, df2, "ffn2_dwout"), "ffn2_w_in": tn_ffn_in(hm2, dab2, "ffn2_dwin")[0]}
    dproj, dgm, dpc, dpa, dyc, do, acc_mg = merge_backward_rows(dx2, mod6, zz, pc, pa, proj, full["w_branch_conv"],
                                                                full["w_branch_attn"], full["w_out"], offs, Tc, "merge_bwd")
    early["w_out"] = tn_matmul(mm, dgm, "dw_out")
    early["w_branch_conv"] = tn_matmul(yc, dpc, "dw_bc")
    early["w_branch_attn"] = tn_matmul(oa, dpa, "dw_ba")
    dproj, dcw = conv_backward(dproj, dyc, proj, conv_full, offs, Tc, "conv_bwd")
    (dq, dk, dv), summed = attention_backward(qr, kr, proj, oa, lse, do, offs, Tc, "attn_bwd",
                                              hosted=Hosted(scatters=[by_dest(g) for g in early.values()]))
    summed = dict(zip(early, summed))
    dproj, dqg = q_norm_backward(dproj, dq, proj, q_norm_g, tabs, offs, nctx, "q_norm_bwd")
    dproj, dkg = kv_backward(dproj, dk, dv, proj, k_norm_g, tabs, offs, "kv_bwd")
    g_w_in_grad = _ungroup_w_in_grad(tn_matmul(dproj, hx, "dw_in"), D, Dc, qw, kw)
    (dxa1, acc_pj), (summed["w_in"],) = proj_backward_rows(dproj, dx2, xa1, mod6, norm2_g, full["w_in"], nctx, "proj_bwd",
                                                          hosted=Hosted(scatters=[g_w_in_grad]))
    (grad_x2d, dab1, df1, acc_f1), _ = ffn_backward_rows(
        dxa1, srcs1, mod6, 0, norm1_g, ab1, f1, full["ffn1_w_in"], full["ffn1_w_out"], nctx, "ffn1_bwd")
    g_ffn1_in, (summed["ffn1_w_out"],) = tn_ffn_in(
        hm1, dab1, "ffn1_dwin", hosted=Hosted(scatters=[by_dest(tn_ffn_out(h1, df1, "ffn1_dwout"))]))
    late = ("ffn1_w_in",)
    gs = [by_dest(g_ffn1_in)]
    grad_x = grad_x2d[None]
    recv_a = rs_sibling_exchange(gs, "rs_sibling")
    ps = [rs_pair_sum(place, g, a, "rs_pair_sum_" + n) for n, g, a in zip(late, gs, recv_a)]
    recv_b = rs_chip_exchange(ps, "rs_chips")
    reduced = dict(zip(late, zip(ps, recv_b)))

    zero_d = jnp.zeros((D,), F32)
    dlat = jnp.concatenate([acc_f1[1, 0], acc_f1[1, 1], acc_f1[1, 2], acc_pj[1, 0], acc_pj[1, 1], acc_mg[0],
                            acc_f2[1, 0], acc_f2[1, 1], acc_f2[1, 2]])
    dctx = jnp.concatenate([acc_f1[0, 0], acc_f1[0, 1], acc_f1[0, 2], acc_pj[0, 0], acc_pj[0, 1]] + [zero_d] * 4)
    small = jnp.concatenate([acc_f1[0, 3] + acc_f1[1, 3], acc_pj[0, 2] + acc_pj[1, 2], acc_f2[1, 3],
                             dqg[0], dkg[0], lacc[0], dcw[0:CONV_TAPS].reshape(-1)])
    n_small = small.shape[0]
    pay_b = jnp.concatenate([dlat, dctx, small]).reshape(1, -1)
    gath = allgather_direct(pay_b, "ag_small_grads")
    dlat_loc = lax.dynamic_slice_in_dim(gath[:, 0, :nd], me * mod_cols, mod_cols, axis=1)
    dctx_loc = lax.dynamic_slice_in_dim(gath[:, 0, nd:2 * nd], me * mod_cols, mod_cols, axis=1)
    g_wmod, pc_part, small_sum = mod_backward(call, cctx2, w_mod[0], dlat_loc, dctx_loc, gath, 2 * nd, n_small, "mod_bwd")
    pcs = allgather_direct(pc_part, "ag_cctx")
    g_bmod, g_cctx = bmod_and_cctx_grad(gath, pcs, cctx2, nd, "small_bwd")
    sm = small_sum[0]
    g_conv_full = sm[3 * D + 2 * HEAD_DIM + D:].reshape(CONV_TAPS, Dc)
    g_conv = lax.dynamic_slice_in_dim(g_conv_full, me * cw_loc.shape[1], cw_loc.shape[1], axis=1)
    gsmall = dict(
        c_ctx=g_cctx, w_mod=g_wmod, b_mod=g_bmod, norm1_g=sm[0:D][None], norm2_g=sm[D:2 * D][None],
        norm3_g=sm[2 * D:3 * D][None], q_norm_g=sm[3 * D:3 * D + HEAD_DIM][None],
        k_norm_g=sm[3 * D + HEAD_DIM:3 * D + 2 * HEAD_DIM][None],
        final_g=sm[3 * D + 2 * HEAD_DIM:3 * D + 2 * HEAD_DIM + D][None], conv_w=g_conv)

    g_out, d_out, m_out, v_out = [], [], [], []
    flipped = ("ffn1_w_in", "w_in", "ffn2_w_in")
    for n in order:
        w = weights[n]
        shp = w.shape
        if n in flipped:
            two_d = lambda a: jnp.swapaxes(a[0], 0, 1)
            back = lambda a: jnp.swapaxes(a, 0, 1)[None]
        else:
            two_d = lambda a: a.reshape(-1, shp[-1])
            back = lambda a: a.reshape(shp)
        m, v = moms[n]
        if n in reduced:
            g2, d, nm, nv = adamw_reduced(place, *reduced[n], two_d(w), two_d(m), two_d(v), "adamw_" + n)
        elif n in summed:
            g2, d, nm, nv = adamw_summed(summed[n], two_d(w), two_d(m), two_d(v), "adamw_" + n)
        else:
            g2 = gsmall[n].reshape(two_d(w).shape)
            d, nm, nv = adamw(two_d(w), g2, two_d(m), two_d(v), "adamw_" + n)
        g_out.append(back(g2))
        d_out.append(back(d))
        m_out.append(back(nm))
        v_out.append(back(nv))
    return (loss, grad_x, *g_out, *d_out, *m_out, *v_out)
```

```python
import math

import jax
import jax.numpy as jnp
from jax import lax
from jax.experimental import pallas as pl
from jax.experimental.pallas import tpu as pltpu

F32 = jnp.float32
BF = jnp.bfloat16
EPS = 1e-6
N_DEV = 8
HEAD_DIM = 128
N_Q_HEADS = 8
N_KV_HEADS = 2
GROUP = N_Q_HEADS // N_KV_HEADS
GRID_W = 64
ROPE_THETA = 10000.0
CONV_TAPS = 3
N_MOD = 9
ADAM_LR = 0.001
ADAM_B1 = 0.9
ADAM_B2 = 0.999
ADAM_EPS = 1e-08
ADAM_WD = 0.01
ADAM_STEP = 10
ROW_TILE = 256
VMEM_BIG = 56 << 20
MESH_ID = pl.DeviceIdType.MESH
HIGHEST = lax.Precision.HIGHEST
NT = (((1,), (1,)), ((), ()))
TN = (((0,), (0,)), ((), ()))
LOG2E = math.log2(math.e)


def _pick(n, cands):
    for c in cands:
        if n % c == 0:
            return c
    return n


def _params(vmem=None, sem=None):
    kw = {}
    if vmem is not None:
        kw["vmem_limit_bytes"] = vmem
    if sem is not None:
        kw["dimension_semantics"] = sem
    return pltpu.CompilerParams(**kw)


def _resident(shape):
    nd = len(shape)
    return pl.BlockSpec(shape, lambda *_: (0,) * nd, pipeline_mode=pl.Buffered(1))


def _sigmoid(x):
    return jax.nn.sigmoid(x)


ANY = pl.BlockSpec(memory_space=pl.ANY)


def _coords():
    return lax.axis_index("x"), lax.axis_index("y"), lax.axis_index("c")


def _flip(v, bit):
    return 1 - v if bit else v


def _remote(src, dst, ssem, rsem, dev):
    return pltpu.make_async_remote_copy(src_ref=src, dst_ref=dst, send_sem=ssem, recv_sem=rsem,
                                        device_id=dev, device_id_type=MESH_ID)


def allgather_direct(v, name):
    def body(v_ref, out_ref, ssem, rsem, lsem):
        x, y, c = _coords()
        me = 4 * x + 2 * y + c
        mine = pltpu.make_async_copy(v_ref, out_ref.at[me], lsem)
        mine.start()
        cps = []
        for p in range(1, N_DEV):
            px, py, pc = (p >> 2) & 1, (p >> 1) & 1, p & 1
            cps.append(_remote(v_ref, out_ref.at[me], ssem.at[p - 1], rsem.at[p - 1],
                               (_flip(x, px), _flip(y, py), _flip(c, pc))))
        for cp in cps:
            cp.start()
        for p in range(1, N_DEV):
            px, py, pc = (p >> 2) & 1, (p >> 1) & 1, p & 1
            src = 4 * _flip(x, px) + 2 * _flip(y, py) + _flip(c, pc)
            _remote(v_ref, out_ref.at[src], ssem.at[p - 1], rsem.at[p - 1], (x, y, c)).wait_recv()
        for cp in cps:
            cp.wait_send()
        mine.wait()

    return pl.pallas_call(
        body, name=name,
        out_shape=jax.ShapeDtypeStruct((N_DEV,) + v.shape, v.dtype),
        in_specs=[ANY], out_specs=ANY,
        scratch_shapes=[pltpu.SemaphoreType.DMA((N_DEV - 1,)), pltpu.SemaphoreType.DMA((N_DEV - 1,)),
                        pltpu.SemaphoreType.DMA],
    )(v)


def allgather_two_level(shards, name):
    n = len(shards)

    def body(*refs):
        v_refs, out_refs, (ssem, rsem, lsem) = refs[:n], refs[n:2 * n], refs[2 * n:]
        x, y, c = _coords()
        me = (x, y, c)
        sib = (x, y, 1 - c)
        chips = [(1 - x, y), (x, 1 - y), (1 - x, 1 - y)]

        def slot(w, px, py, pc):
            return out_refs[w].at[4 * px + 2 * py + pc]

        def sem(w, k):
            return ssem.at[7 * w + k], rsem.at[7 * w + k]

        mine = [pltpu.make_async_copy(v_refs[w], slot(w, *me), lsem.at[w]) for w in range(n)]
        for cp in mine:
            cp.start()
        first = []
        for w in range(n):
            first.append(_remote(v_refs[w], slot(w, *me), *sem(w, 0), sib))
            first += [_remote(v_refs[w], slot(w, *me), *sem(w, 1 + j), (*chip, c)) for j, chip in enumerate(chips)]
        for cp in first:
            cp.start()
        passed = []
        for w in range(n):
            for j, chip in enumerate(chips):
                _remote(v_refs[w], slot(w, *chip, c), *sem(w, 1 + j), me).wait_recv()
                cp = _remote(slot(w, *chip, c), slot(w, *chip, c), *sem(w, 4 + j), sib)
                cp.start()
                passed.append(cp)
        for w in range(n):
            _remote(v_refs[w], slot(w, x, y, 1 - c), *sem(w, 0), me).wait_recv()
            for j, chip in enumerate(chips):
                _remote(v_refs[w], slot(w, *chip, 1 - c), *sem(w, 4 + j), me).wait_recv()
        for cp in first + passed:
            cp.wait_send()
        for cp in mine:
            cp.wait()

    return pl.pallas_call(
        body, name=name,
        out_shape=[jax.ShapeDtypeStruct((N_DEV,) + s.shape, s.dtype) for s in shards],
        in_specs=[ANY] * n, out_specs=[ANY] * n,
        scratch_shapes=[pltpu.SemaphoreType.DMA((7 * n,)), pltpu.SemaphoreType.DMA((7 * n,)),
                        pltpu.SemaphoreType.DMA((n,))],
    )(*shards)


def rs_sibling_exchange(gs, name):
    n = len(gs)

    def body(*refs):
        g_refs, out_refs, (ssem, rsem) = refs[:n], refs[n:2 * n], refs[2 * n:]
        x, y, c = _coords()
        sib = (x, y, 1 - c)
        cps = [_remote(g_refs[w].at[2 * k + (1 - c)], out_refs[w].at[k], ssem.at[4 * w + k], rsem.at[4 * w + k], sib)
               for w in range(n) for k in range(4)]
        for cp in cps:
            cp.start()
        for cp in cps:
            cp.wait()

    return pl.pallas_call(
        body, name=name,
        out_shape=[jax.ShapeDtypeStruct((4,) + g.shape[1:], g.dtype) for g in gs],
        in_specs=[ANY] * n, out_specs=[ANY] * n,
        scratch_shapes=[pltpu.SemaphoreType.DMA((4 * n,)), pltpu.SemaphoreType.DMA((4 * n,))],
    )(*gs)


def rs_chip_exchange(ps, name):
    n = len(ps)

    def body(*refs):
        p_refs, out_refs, (ssem, rsem) = refs[:n], refs[n:2 * n], refs[2 * n:]
        x, y, c = _coords()
        chips = [(1 - x, y), (x, 1 - y), (1 - x, 1 - y)]
        cps = [_remote(p_refs[w].at[2 * cx + cy], out_refs[w].at[j], ssem.at[3 * w + j], rsem.at[3 * w + j], (cx, cy, c))
               for w in range(n) for j, (cx, cy) in enumerate(chips)]
        for cp in cps:
            cp.start()
        for cp in cps:
            cp.wait()

    return pl.pallas_call(
        body, name=name,
        out_shape=[jax.ShapeDtypeStruct((3,) + p.shape[1:], p.dtype) for p in ps],
        in_specs=[ANY] * n, out_specs=[ANY] * n,
        scratch_shapes=[pltpu.SemaphoreType.DMA((3 * n,)), pltpu.SemaphoreType.DMA((3 * n,))],
    )(*ps)


class Hosted:
    def __init__(self, gathers=(), scatters=()):
        self.items = [(a, False) for a in gathers] + [(a, True) for a in scatters]
        self.n = len(self.items)
        self.operands = [a for a, _ in self.items]
        self.out_shapes = [jax.ShapeDtypeStruct(a.shape if sc else (N_DEV,) + a.shape, a.dtype) for a, sc in self.items]
        self.scratch = [pltpu.SemaphoreType.DMA((7 * self.n,)), pltpu.SemaphoreType.DMA((7 * self.n,)),
                        pltpu.SemaphoreType.DMA((self.n,))]

    def _copies(self, in_refs, out_refs, ssem, rsem, lsem, arrivals):
        x, y, c = _coords()
        me = 4 * x + 2 * y + c
        remote, local = [], []
        for w, (_, sc) in enumerate(self.items):
            src, dst = in_refs[w], out_refs[w]
            local.append(pltpu.make_async_copy(src.at[me] if sc else src, dst.at[me], lsem.at[w]))
            for p in range(1, N_DEV):
                px, py, pc = _flip(x, (p >> 2) & 1), _flip(y, (p >> 1) & 1), _flip(c, p & 1)
                peer = 4 * px + 2 * py + pc
                k = 7 * w + p - 1
                if arrivals:
                    remote.append(_remote(src.at[me] if sc else src, dst.at[peer], ssem.at[k], rsem.at[k], (x, y, c)))
                else:
                    remote.append(_remote(src.at[peer] if sc else src, dst.at[me], ssem.at[k], rsem.at[k], (px, py, pc)))
        return remote, local

    def start(self, in_refs, out_refs, ssem, rsem, lsem):
        sends, local = self._copies(in_refs, out_refs, ssem, rsem, lsem, False)
        for cp in local + sends:
            cp.start()

    def wait(self, in_refs, out_refs, ssem, rsem, lsem):
        arrivals, local = self._copies(in_refs, out_refs, ssem, rsem, lsem, True)
        for cp in arrivals:
            cp.wait_recv()
        for cp in arrivals:
            cp.wait_send()
        for cp in local:
            cp.wait()


def _call(body, *, name, grid, in_specs, out_specs, out_shape, operands, params, scratch_shapes=(), aliases=None, hosted=None):
    n_in, n_out, n_scr = len(in_specs), len(out_specs), len(scratch_shapes)
    h = hosted.n if hosted is not None else 0

    def wrapped(*refs):
        ins, cins = refs[:n_in], refs[n_in:n_in + h]
        outs, couts = refs[n_in + h:n_in + h + n_out], refs[n_in + h + n_out:n_in + 2 * h + n_out]
        rest = refs[n_in + 2 * h + n_out:]
        scr, sems = rest[:n_scr], rest[n_scr:]
        if h:
            ids = [pl.program_id(a) for a in range(len(grid))]
            first, last = ids[0] == 0, ids[0] == grid[0] - 1
            for a in range(1, len(grid)):
                first, last = first & (ids[a] == 0), last & (ids[a] == grid[a] - 1)

            @pl.when(first)
            def _():
                hosted.start(cins, couts, *sems)

        body(*ins, *outs, *scr)
        if h:
            @pl.when(last)
            def _():
                hosted.wait(cins, couts, *sems)

    res = pl.pallas_call(
        wrapped, name=name, grid=grid,
        in_specs=list(in_specs) + [ANY] * h, out_specs=list(out_specs) + [ANY] * h,
        out_shape=list(out_shape) + (hosted.out_shapes if h else []),
        scratch_shapes=list(scratch_shapes) + (hosted.scratch if h else []),
        input_output_aliases=aliases or {}, compiler_params=params,
    )(*operands, *(hosted.operands if h else []))
    return list(res[:n_out]), list(res[n_out:])


def _row_tile(R, C):
    if R * C <= (1 << 18):
        return R
    return max((d for d in range(8, 257, 8) if R % d == 0), default=R)


def rs_pair_sum(place, gs, recv_a, name):
    _, R, C = gs.shape
    tr = _row_tile(R, C)

    def body(place_ref, g_ref, a_ref, o_ref):
        o_ref[0] = (g_ref[0].astype(F32) + a_ref[0].astype(F32)).astype(o_ref.dtype)

    return pl.pallas_call(
        body, name=name,
        grid_spec=pltpu.PrefetchScalarGridSpec(
            num_scalar_prefetch=1, grid=(4, R // tr),
            in_specs=[pl.BlockSpec((1, tr, C), lambda k, r, pr: (2 * k + pr[0], r, 0)),
                      pl.BlockSpec((1, tr, C), lambda k, r, pr: (k, r, 0))],
            out_specs=pl.BlockSpec((1, tr, C), lambda k, r, pr: (k, r, 0))),
        out_shape=jax.ShapeDtypeStruct((4, R, C), BF),
    )(place, gs, recv_a)


def _cond_rows(call_ref, cctx_ref, z_ref, D):
    z_ref[...] = jnp.zeros_like(z_ref)
    for a in range(N_DEV):
        z_ref[a:a + 1, :] = call_ref[a][:, :D]
    z_ref[N_DEV:N_DEV + 1, :] = cctx_ref[...]


def mod_forward(call, c_ctx, w_loc, b_loc, name):
    D, cols = w_loc.shape

    def body(call_ref, cctx_ref, w_ref, b_ref, o_ref, z_ref):
        _cond_rows(call_ref, cctx_ref, z_ref, D)
        z = z_ref[...]
        s = z * _sigmoid(z)
        o_ref[...] = jnp.dot(s, w_ref[...], preferred_element_type=F32, precision=HIGHEST) + b_ref[...]

    return pl.pallas_call(
        body, name=name, out_shape=jax.ShapeDtypeStruct((16, cols), F32),
        scratch_shapes=[pltpu.VMEM((16, D), F32)],
        compiler_params=_params(vmem=VMEM_BIG),
    )(call, c_ctx, w_loc, b_loc)


def mod_backward(call, c_ctx, w_loc, dlat_loc, dctx_loc, gath, n_small_off, n_small, name):
    D, cols = w_loc.shape

    def body(call_ref, cctx_ref, w_ref, dlat_ref, dctx_ref, g_ref, gw_ref, pc_ref, small_ref, z_ref, dm_ref):
        _cond_rows(call_ref, cctx_ref, z_ref, D)
        z = z_ref[...]
        s = z * _sigmoid(z)
        dctx = dctx_ref[0:1, :]
        for a in range(1, N_DEV):
            dctx = dctx + dctx_ref[a:a + 1, :]
        dm_ref[...] = jnp.zeros_like(dm_ref)
        dm_ref[0:N_DEV, :] = dlat_ref[...]
        dm_ref[N_DEV:N_DEV + 1, :] = dctx
        gw_ref[...] = lax.dot_general(s, dm_ref[...], TN, preferred_element_type=F32, precision=HIGHEST)
        pc_ref[...] = lax.dot_general(dctx, w_ref[...], NT, preferred_element_type=F32, precision=HIGHEST)
        acc = g_ref[0][:, n_small_off:n_small_off + n_small]
        for a in range(1, N_DEV):
            acc = acc + g_ref[a][:, n_small_off:n_small_off + n_small]
        small_ref[...] = acc

    return pl.pallas_call(
        body, name=name,
        out_shape=(jax.ShapeDtypeStruct((D, cols), F32), jax.ShapeDtypeStruct((1, D), F32),
                   jax.ShapeDtypeStruct((1, n_small), F32)),
        scratch_shapes=[pltpu.VMEM((16, D), F32), pltpu.VMEM((16, cols), F32)],
        compiler_params=_params(vmem=VMEM_BIG),
    )(call, c_ctx, w_loc, dlat_loc, dctx_loc, gath)


def bmod_and_cctx_grad(gath, pcs, c_ctx, nd, name):
    D = c_ctx.shape[-1]

    def body(g_ref, pc_ref, cctx_ref, gb_ref, gc_ref):
        acc = g_ref[0][:, :nd] + g_ref[0][:, nd:2 * nd]
        for a in range(1, N_DEV):
            acc = acc + (g_ref[a][:, :nd] + g_ref[a][:, nd:2 * nd])
        gb_ref[...] = acc
        p = pc_ref[0]
        for a in range(1, N_DEV):
            p = p + pc_ref[a]
        z = cctx_ref[...]
        sg = _sigmoid(z)
        gc_ref[...] = p * (sg * (1.0 + z * (1.0 - sg)))

    return pl.pallas_call(
        body, name=name,
        out_shape=(jax.ShapeDtypeStruct((1, nd), F32), jax.ShapeDtypeStruct((1, D), F32)),
    )(gath, pcs, c_ctx)


def _mod_spec(D, which, nctx):
    return pl.BlockSpec((1, 3, D), lambda i: (jnp.where(i < nctx, 0, 3) + which, 0, 0))


def _acc_spec(D, nctx):
    return pl.BlockSpec((1, 8, D), lambda i: (jnp.where(i < nctx, 0, 1), 0, 0))


def _row(tm, n):
    return pl.BlockSpec((tm, n), lambda i: (i, 0))


def _stk(k, tm, n):
    return pl.BlockSpec((k, tm, n), lambda i: (0, i, 0))


def _two_stream_specs(tm, D, nctx):
    return [pl.BlockSpec((tm, D), lambda i: (jnp.minimum(i, nctx - 1), 0)),
            pl.BlockSpec((tm, D), lambda i: (jnp.maximum(i - nctx, 0), 0))]


def _final_norm_loss_backward(x, tgt, gg, i, dx_ref, acc_ref):
    @pl.when(i == 0)
    def _():
        acc_ref[...] = jnp.zeros_like(acc_ref)

    D = x.shape[1]
    r = lax.rsqrt(jnp.mean(x * x, axis=-1, keepdims=True) + EPS)
    xh = x * r
    e = xh * gg - tgt
    part = 0.5 * jnp.sum(jnp.mean(e * e, axis=-1, keepdims=True), axis=0, keepdims=True)
    dy = e * (1.0 / D)
    dyg = dy * gg
    dx_ref[...] = r * (dyg - xh * jnp.mean(dyg * xh, axis=-1, keepdims=True))
    acc_ref[0:1, :] += jnp.sum(dy * xh, axis=0, keepdims=True)
    acc_ref[1:2, :] += jnp.broadcast_to(part, (1, D))


def ffn_forward(srcs, mod6, which, g, ws, w_out, nctx, name, hosted=None, final=None):
    D = srcs[-1].shape[1]
    Tr = sum(s.shape[0] for s in srcs)
    nsh, _, nn = ws.shape
    nh = nsh // 2
    tm = ROW_TILE
    two = len(srcs) == 2
    nfin = 0 if final is None else 2

    def body(*refs):
        x_refs, fin_refs, rest = refs[:len(srcs)], refs[len(srcs):len(srcs) + nfin], refs[len(srcs) + nfin:]
        mod_ref, g_ref, ws_ref, wout_ref, xo_ref, hm_ref, ab_ref, h_ref, f_ref = rest[:9]
        x = jnp.where(pl.program_id(0) < nctx, x_refs[0][...], x_refs[1][...]) if two else x_refs[0][...]
        ms = mod_ref[0]
        shift, scale, gate = ms[0:1], ms[1:2], ms[2:3]
        r = lax.rsqrt(jnp.mean(x * x, axis=-1, keepdims=True) + EPS)
        hb = (((x * r) * g_ref[...]) * (1.0 + scale) + shift).astype(BF)
        hm_ref[...] = hb
        f = jnp.zeros((tm, D), F32)
        for j in range(nh):
            a = jnp.dot(hb, ws_ref[j], preferred_element_type=F32)
            b = jnp.dot(hb, ws_ref[nh + j], preferred_element_type=F32)
            ab_ref[j] = a.astype(BF)
            ab_ref[nh + j] = b.astype(BF)
            h = ((a * _sigmoid(a)) * b).astype(BF)
            h_ref[j] = h
            f = f + jnp.dot(h, wout_ref[j * nn:(j + 1) * nn, :], preferred_element_type=F32)
        f_ref[...] = f.astype(BF)
        xo = x + (0.5 * gate) * f
        if final is None:
            xo_ref[...] = xo
        else:
            _final_norm_loss_backward(xo, fin_refs[0][...], fin_refs[1][...], pl.program_id(0), xo_ref, rest[9])

    src_specs = _two_stream_specs(tm, D, nctx) if two else [_row(tm, D)]
    fin = final is not None
    return _call(
        body, name=name, grid=(Tr // tm,),
        in_specs=src_specs + ([_row(tm, D), _resident((1, D))] if fin else [])
                 + [_mod_spec(D, which, nctx), _resident((1, D)), _resident(ws.shape), _resident(w_out.shape)],
        out_specs=[_row(tm, D), _row(tm, D), _stk(nsh, tm, nn), _stk(nh, tm, nn), _row(tm, D)]
                  + ([pl.BlockSpec((8, D), lambda i: (0, 0))] if fin else []),
        out_shape=[jax.ShapeDtypeStruct((Tr, D), F32), jax.ShapeDtypeStruct((Tr, D), BF),
                   jax.ShapeDtypeStruct((nsh, Tr, nn), BF), jax.ShapeDtypeStruct((nh, Tr, nn), BF),
                   jax.ShapeDtypeStruct((Tr, D), BF)] + ([jax.ShapeDtypeStruct((8, D), F32)] if fin else []),
        operands=[*srcs, *(final or ()), mod6, g, ws, w_out], hosted=hosted,
        params=_params(vmem=VMEM_BIG, sem=("arbitrary",)))


def ffn_backward_rows(dxo, srcs, mod6, which, g, ab, fo, ws, w_out, nctx, name, hosted=None):
    D = srcs[-1].shape[1]
    Tr = sum(s.shape[0] for s in srcs)
    Tl = srcs[-1].shape[0]
    nsh, _, nn = ws.shape
    nh = nsh // 2
    tm = ROW_TILE
    two = len(srcs) == 2

    def body(*refs):
        dxo_ref, x_refs = refs[0], refs[1:1 + len(srcs)]
        mod_ref, g_ref, ab_ref, fo_ref, ws_ref, wout_ref, dx_ref, dab_ref, df_ref, acc_ref = refs[1 + len(srcs):]
        i = pl.program_id(0)

        @pl.when((i == 0) | (i == nctx))
        def _():
            acc_ref[...] = jnp.zeros_like(acc_ref)

        dxo = dxo_ref[...]
        x = jnp.where(i < nctx, x_refs[0][...], x_refs[1][...]) if two else x_refs[0][...]
        ms = mod_ref[0]
        scale, gate = ms[1:2], ms[2:3]
        gg = g_ref[...]
        dgate = jnp.sum(dxo * fo_ref[...].astype(F32), axis=0, keepdims=True) * 0.5
        dfb = (dxo * (0.5 * gate)).astype(BF)
        df_ref[...] = dfb
        dhm = jnp.zeros((tm, D), F32)
        for j in range(nh):
            dh = lax.dot_general(dfb, wout_ref[j * nn:(j + 1) * nn, :], NT, preferred_element_type=F32)
            a = ab_ref[j].astype(F32)
            b = ab_ref[nh + j].astype(F32)
            sg = _sigmoid(a)
            da = ((dh * b) * (sg * (1.0 + a * (1.0 - sg)))).astype(BF)
            db = (dh * (a * sg)).astype(BF)
            dab_ref[j] = da
            dab_ref[nh + j] = db
            dhm = dhm + lax.dot_general(da, ws_ref[j], NT, preferred_element_type=F32)
            dhm = dhm + lax.dot_general(db, ws_ref[nh + j], NT, preferred_element_type=F32)
        r = lax.rsqrt(jnp.mean(x * x, axis=-1, keepdims=True) + EPS)
        xh = x * r
        dshift = jnp.sum(dhm, axis=0, keepdims=True)
        dscale = jnp.sum(dhm * (xh * gg), axis=0, keepdims=True)
        dxh_g = dhm * (1.0 + scale)
        dg = jnp.sum(dxh_g * xh, axis=0, keepdims=True)
        dxh = dxh_g * gg
        dx_ref[...] = dxo + r * (dxh - xh * jnp.mean(dxh * xh, axis=-1, keepdims=True))
        for k, val in enumerate((dshift, dscale, dgate, dg)):
            acc_ref[0, k:k + 1, :] += val

    src_specs = _two_stream_specs(tm, D, nctx) if two else [_row(tm, D)]
    dx_spec = pl.BlockSpec((tm, D), lambda i: (jnp.maximum(i - nctx, 0), 0))
    return _call(
        body, name=name, grid=(Tr // tm,),
        in_specs=[_row(tm, D)] + src_specs + [_mod_spec(D, which, nctx), _resident((1, D)), _stk(nsh, tm, nn), _row(tm, D),
                                              _resident(ws.shape), _resident(w_out.shape)],
        out_specs=[dx_spec, _stk(nsh, tm, nn), _row(tm, D), _acc_spec(D, nctx)],
        out_shape=[jax.ShapeDtypeStruct((Tl, D), F32), jax.ShapeDtypeStruct((nsh, Tr, nn), BF),
                   jax.ShapeDtypeStruct((Tr, D), BF), jax.ShapeDtypeStruct((2, 8, D), F32)],
        operands=[dxo, *srcs, mod6, g, ab, fo, ws, w_out], hosted=hosted,
        params=_params(vmem=VMEM_BIG, sem=("arbitrary",)))


def _token_tile(T):
    return _pick(T, (1024, 768, 512, 256, 128))


def tn_matmul(a, b, name):
    T, K = a.shape
    N = b.shape[1]
    tk = _pick(K, (1024, 1408, 1664, 768, 512, 384, 256, 128))
    tn = _pick(N, (1024, 1408, 1664, 768, 512, 384, 256, 128))
    tt = _token_tile(T)
    nt = T // tt

    def body(a_ref, b_ref, o_ref, acc_ref):
        t = pl.program_id(2)

        @pl.when(t == 0)
        def _():
            acc_ref[...] = jnp.zeros_like(acc_ref)

        acc_ref[...] += lax.dot_general(a_ref[...], b_ref[...], TN, preferred_element_type=F32)

        @pl.when(t == nt - 1)
        def _():
            o_ref[...] = acc_ref[...].astype(BF)

    return pl.pallas_call(
        body, name=name, grid=(K // tk, N // tn, nt),
        in_specs=[pl.BlockSpec((tt, tk), lambda k, n, t: (t, k)), pl.BlockSpec((tt, tn), lambda k, n, t: (t, n))],
        out_specs=pl.BlockSpec((tk, tn), lambda k, n, t: (k, n)),
        out_shape=jax.ShapeDtypeStruct((K, N), BF),
        scratch_shapes=[pltpu.VMEM((tk, tn), F32)],
        compiler_params=_params(vmem=VMEM_BIG, sem=("parallel", "parallel", "arbitrary")),
    )(a, b)


def tn_ffn_in(hm, dab, name, hosted=None):
    T, D = hm.shape
    nsh, _, nn = dab.shape
    per = nsh // 2
    tt = _token_tile(T)
    nt = T // tt

    def body(a_ref, b_ref, o_ref, acc_ref):
        t = pl.program_id(1)

        @pl.when(t == 0)
        def _():
            acc_ref[...] = jnp.zeros_like(acc_ref)

        a = a_ref[...]
        for j in range(per):
            acc_ref[j] += lax.dot_general(b_ref[j], a, TN, preferred_element_type=F32)

        @pl.when(t == nt - 1)
        def _():
            o_ref[...] = acc_ref[...].astype(BF)

    (out,), exchanged = _call(
        body, name=name, grid=(2, nt),
        in_specs=[pl.BlockSpec((tt, D), lambda s, t: (t, 0)), pl.BlockSpec((per, tt, nn), lambda s, t: (s, t, 0))],
        out_specs=[pl.BlockSpec((per, nn, D), lambda s, t: (s, 0, 0))],
        out_shape=[jax.ShapeDtypeStruct((nsh, nn, D), BF)],
        scratch_shapes=[pltpu.VMEM((per, nn, D), F32)],
        operands=[hm, dab], hosted=hosted,
        params=_params(vmem=VMEM_BIG, sem=("arbitrary", "arbitrary")))
    return out, exchanged


def tn_ffn_out(h, df, name):
    nh, T, nn = h.shape
    D = df.shape[1]
    per = nh // 2
    tt = _token_tile(T)
    nt = T // tt

    def body(a_ref, b_ref, o_ref, acc_ref):
        t = pl.program_id(1)

        @pl.when(t == 0)
        def _():
            acc_ref[...] = jnp.zeros_like(acc_ref)

        b = b_ref[...]
        for j in range(per):
            acc_ref[j] += lax.dot_general(a_ref[j], b, TN, preferred_element_type=F32)

        @pl.when(t == nt - 1)
        def _():
            for j in range(per):
                o_ref[j * nn:(j + 1) * nn, :] = acc_ref[j].astype(BF)

    return pl.pallas_call(
        body, name=name, grid=(2, nt),
        in_specs=[pl.BlockSpec((per, tt, nn), lambda s, t: (s, t, 0)), pl.BlockSpec((tt, D), lambda s, t: (t, 0))],
        out_specs=pl.BlockSpec((per * nn, D), lambda s, t: (s, 0)),
        out_shape=jax.ShapeDtypeStruct((nh * nn, D), BF),
        scratch_shapes=[pltpu.VMEM((per, nn, D), F32)],
        compiler_params=_params(vmem=VMEM_BIG, sem=("parallel", "arbitrary")),
    )(h, df)


def _rope_apply(y, cos, s_next, s_prev):
    return y * cos + pltpu.roll(y, HEAD_DIM - 32, 1) * s_next + pltpu.roll(y, 32, 1) * s_prev


def _rope_transpose(dz, cos, s_next, s_prev):
    return dz * cos + pltpu.roll(dz * s_next, 32, 1) + pltpu.roll(dz * s_prev, HEAD_DIM - 32, 1)


def proj_forward(xa, mod6, g, w_in, qg, kg, tabs, offs, nctx, name, hosted=None):
    Tr, D = xa.shape
    P = w_in.shape[1]
    tm = ROW_TILE
    qo, ko = offs["q"], offs["k"]
    qw, kw = N_Q_HEADS * HEAD_DIM, N_KV_HEADS * HEAD_DIM
    scale_q = HEAD_DIM ** -0.5 * LOG2E

    def body(x_ref, mod_ref, g_ref, w_ref, qg_ref, kg_ref, tab_ref, hx_ref, pr_ref, q_ref, k_ref):
        x = x_ref[...]
        ms = mod_ref[0]
        shift, scale = ms[0:1], ms[1:2]
        r = lax.rsqrt(jnp.mean(x * x, axis=-1, keepdims=True) + EPS)
        hb = (((x * r) * g_ref[...]) * (1.0 + scale) + shift).astype(BF)
        hx_ref[...] = hb
        pr = jnp.dot(hb, w_ref[...], preferred_element_type=F32)
        pr_ref[...] = pr.astype(BF)
        cos, s_next, s_prev = tab_ref[0], tab_ref[1], tab_ref[2]

        def head(v, gain):
            n = v * lax.rsqrt(jnp.mean(v * v, axis=-1, keepdims=True) + EPS)
            return _rope_apply(n * gain, cos, s_next, s_prev)

        for h in range(N_Q_HEADS):
            lo = qo + h * HEAD_DIM
            q_ref[:, h * HEAD_DIM:(h + 1) * HEAD_DIM] = (head(pr[:, lo:lo + HEAD_DIM], qg_ref[...]) * scale_q).astype(BF)
        for h in range(N_KV_HEADS):
            lo = ko + h * HEAD_DIM
            k_ref[:, h * HEAD_DIM:(h + 1) * HEAD_DIM] = head(pr[:, lo:lo + HEAD_DIM], kg_ref[...]).astype(BF)

    return _call(
        body, name=name, grid=(Tr // tm,),
        in_specs=[_row(tm, D), _mod_spec(D, 1, nctx), _resident((1, D)), _resident(w_in.shape),
                  _resident((1, HEAD_DIM)), _resident((1, HEAD_DIM)),
                  pl.BlockSpec((3, tm, HEAD_DIM), lambda i: (0, i, 0))],
        out_specs=[_row(tm, D), _row(tm, P), _row(tm, qw), _row(tm, kw)],
        out_shape=[jax.ShapeDtypeStruct((Tr, D), BF), jax.ShapeDtypeStruct((Tr, P), BF),
                   jax.ShapeDtypeStruct((Tr, qw), BF), jax.ShapeDtypeStruct((Tr, kw), BF)],
        operands=[xa, mod6, g, w_in, qg, kg, tabs], hosted=hosted,
        params=_params(vmem=VMEM_BIG, sem=("arbitrary",)))


def _shifted(u, first_row, last_row):
    T = u.shape[0]
    prev = jnp.where(first_row, 0.0, pltpu.roll(u, 1, 0))
    nxt = jnp.where(last_row, 0.0, pltpu.roll(u, T - 1, 0))
    return prev, nxt


def conv_forward(proj, conv_w, offs, Tc, name):
    Ta = proj.shape[0]
    T = Ta - Tc
    Dc = conv_w.shape[1]
    cb = offs["cv"] // 384

    def body(p_ref, w_ref, y_ref):
        rows = lax.broadcasted_iota(jnp.int32, (T, 128), 0)
        u = p_ref[pl.ds(Tc, T), 128:256].astype(F32) * p_ref[pl.ds(Tc, T), 256:384].astype(F32)
        prev, nxt = _shifted(u, rows == 0, rows == T - 1)
        w = w_ref[...]
        cv = prev * w[0:1] + u * w[1:2] + nxt * w[2:3]
        y_ref[...] = (p_ref[pl.ds(Tc, T), 0:128].astype(F32) * cv).astype(BF)

    return pl.pallas_call(
        body, name=name, grid=(Dc // 128,),
        in_specs=[pl.BlockSpec((Ta, 384), lambda j: (0, cb + j)), pl.BlockSpec((CONV_TAPS, 128), lambda j: (0, j))],
        out_specs=pl.BlockSpec((T, 128), lambda j: (0, j)),
        out_shape=jax.ShapeDtypeStruct((T, Dc), BF),
        compiler_params=_params(vmem=VMEM_BIG, sem=("arbitrary",)),
    )(proj, conv_w)


def conv_backward(dproj, dy, proj, conv_w, offs, Tc, name):
    Ta = proj.shape[0]
    T = Ta - Tc
    Dc = conv_w.shape[1]
    cb = offs["cv"] // 384

    def body(dp_any, dy_ref, p_ref, w_ref, o_ref, dw_ref):
        rows = lax.broadcasted_iota(jnp.int32, (T, 128), 0)
        first, last = rows == 0, rows == T - 1
        bg = p_ref[pl.ds(Tc, T), 0:128].astype(F32)
        cg = p_ref[pl.ds(Tc, T), 128:256].astype(F32)
        vc = p_ref[pl.ds(Tc, T), 256:384].astype(F32)
        dy = dy_ref[...].astype(F32)
        u = cg * vc
        prev, nxt = _shifted(u, first, last)
        w = w_ref[...]
        cv = prev * w[0:1] + u * w[1:2] + nxt * w[2:3]
        o_ref[pl.ds(0, Tc), :] = jnp.zeros((Tc, 384), BF)
        o_ref[pl.ds(Tc, T), 0:128] = (dy * cv).astype(BF)
        dcv = dy * bg
        dprev, dnxt = _shifted(dcv, first, last)
        du = dnxt * w[0:1] + dcv * w[1:2] + dprev * w[2:3]
        o_ref[pl.ds(Tc, T), 128:256] = (du * vc).astype(BF)
        o_ref[pl.ds(Tc, T), 256:384] = (du * cg).astype(BF)
        dw_ref[...] = jnp.zeros_like(dw_ref)
        for k, tap in enumerate((prev, u, nxt)):
            dw_ref[k:k + 1, :] = jnp.sum(dcv * tap, axis=0, keepdims=True)

    blk = pl.BlockSpec((Ta, 384), lambda j: (0, cb + j))
    return pl.pallas_call(
        body, name=name, grid=(Dc // 128,),
        in_specs=[ANY, pl.BlockSpec((T, 128), lambda j: (0, j)), blk, pl.BlockSpec((CONV_TAPS, 128), lambda j: (0, j))],
        out_specs=[blk, pl.BlockSpec((8, 128), lambda j: (0, j))],
        out_shape=[jax.ShapeDtypeStruct(dproj.shape, BF), jax.ShapeDtypeStruct((8, Dc), F32)],
        input_output_aliases={0: 0},
        compiler_params=_params(vmem=VMEM_BIG, sem=("arbitrary",)),
    )(dproj, dy, proj, conv_w)


def _kv_chunk(Ta):
    return _pick(Ta, (768, 512, 384, 256, 128))


def _stack_heads(v):
    return jnp.concatenate([v[:, h * HEAD_DIM:(h + 1) * HEAD_DIM] for h in range(GROUP)], axis=0)


def attention_forward(q, k, proj, offs, Tc, name, hosted=None):
    Ta = k.shape[0]
    T = Ta - Tc
    tq = ROW_TILE
    kc = _kv_chunk(Ta)
    nkv = Ta // kc
    gw = GROUP * HEAD_DIM
    vblk = offs["v"] // HEAD_DIM
    qoff = Tc // tq
    n = GROUP * tq

    def body(q_ref, k_ref, v_ref, o_ref, lse_ref, vx_ref, qs_ref, s0_ref, s1_ref, m_ref, acc_ref):
        @pl.when(pl.program_id(1) == 0)
        def _():
            vx_ref[:, 0:HEAD_DIM] = v_ref[...]
            vx_ref[:, HEAD_DIM:2 * HEAD_DIM] = jnp.ones((Ta, HEAD_DIM), BF)

        qs_ref[...] = _stack_heads(q_ref[...])
        m_ref[...] = jnp.full((n, 1), -1e30, F32)
        acc_ref[...] = jnp.zeros((n, 2 * HEAD_DIM), F32)

        def rows(c):
            return pl.ds(pl.multiple_of(c * kc, kc), kc)

        def logits(c, dst):
            dst[...] = lax.dot_general(qs_ref[...], k_ref[rows(c), :], NT, preferred_element_type=F32)

        def consume(src, c):
            s = src[...]
            m_prev = m_ref[...]
            m_new = jnp.maximum(m_prev, jnp.max(s, axis=-1, keepdims=True))
            p = jnp.exp2(s - m_new).astype(BF)
            acc_ref[...] = jnp.exp2(m_prev - m_new) * acc_ref[...] + jnp.dot(p, vx_ref[rows(c), :], preferred_element_type=F32)
            m_ref[...] = m_new

        def pair(i, carry):
            logits(2 * i + 1, s1_ref)
            consume(s0_ref, 2 * i)
            logits(2 * i + 2, s0_ref)
            consume(s1_ref, 2 * i + 1)
            return carry

        logits(0, s0_ref)
        if nkv % 2:
            lax.fori_loop(0, nkv // 2, pair, 0)
            consume(s0_ref, nkv - 1)
        else:
            lax.fori_loop(0, nkv // 2 - 1, pair, 0)
            logits(nkv - 1, s1_ref)
            consume(s0_ref, nkv - 2)
            consume(s1_ref, nkv - 1)
        acc = acc_ref[...]
        l = acc[:, HEAD_DIM:HEAD_DIM + 1]
        o = acc[:, 0:HEAD_DIM] / l
        lse = m_ref[...] + jnp.log2(l)
        for h in range(GROUP):
            o_ref[:, h * HEAD_DIM:(h + 1) * HEAD_DIM] = o[h * tq:(h + 1) * tq].astype(BF)
            lse_ref[0, :, h:h + 1] = lse[h * tq:(h + 1) * tq]

    return _call(
        body, name=name, grid=(N_KV_HEADS, T // tq),
        in_specs=[pl.BlockSpec((tq, gw), lambda j, i: (i + qoff, j)),
                  pl.BlockSpec((Ta, HEAD_DIM), lambda j, i: (0, j)),
                  pl.BlockSpec((Ta, HEAD_DIM), lambda j, i: (0, vblk + j))],
        out_specs=[pl.BlockSpec((tq, gw), lambda j, i: (i, j)),
                   pl.BlockSpec((1, tq, GROUP), lambda j, i: (j, i, 0))],
        out_shape=[jax.ShapeDtypeStruct((T, N_Q_HEADS * HEAD_DIM), BF),
                   jax.ShapeDtypeStruct((N_KV_HEADS, T, GROUP), F32)],
        scratch_shapes=[pltpu.VMEM((Ta, 2 * HEAD_DIM), BF), pltpu.VMEM((n, HEAD_DIM), BF), pltpu.VMEM((n, kc), F32),
                        pltpu.VMEM((n, kc), F32), pltpu.VMEM((n, 1), F32), pltpu.VMEM((n, 2 * HEAD_DIM), F32)],
        operands=[q, k, proj], hosted=hosted,
        params=_params(vmem=VMEM_BIG, sem=("arbitrary", "arbitrary")))


def _norm_rope_backward(dz, raw, gg, cos, s_next, s_prev):
    r = lax.rsqrt(jnp.mean(raw * raw, axis=-1, keepdims=True) + EPS)
    n = raw * r
    dy = _rope_transpose(dz, cos, s_next, s_prev)
    dn = dy * gg
    return r * (dn - n * jnp.mean(dn * n, axis=-1, keepdims=True)), jnp.sum(dy * n, axis=0, keepdims=True)


def attention_backward(dproj, q, k, proj, o, lse, do, qgain, tabs, offs, Tc, name, hosted=None):
    Ta = k.shape[0]
    tq = ROW_TILE
    nctx = Tc // tq
    kc = _kv_chunk(Ta)
    gw = GROUP * HEAD_DIM
    vblk = offs["v"] // HEAD_DIM
    qblk = offs["q"] // gw
    zscale = HEAD_DIM ** -0.5

    def body(dp_any, q_ref, k_ref, v_ref, o_ref, lse_ref, do_ref, raw_ref, g_ref, tab_ref, dqr_ref, dk_ref, dv_ref, dg_ref):
        j, i = pl.program_id(0), pl.program_id(1)

        @pl.when(i == 0)
        def _():
            dk_ref[...] = jnp.zeros_like(dk_ref)
            dv_ref[...] = jnp.zeros_like(dv_ref)

        @pl.when((i == 0) & (j == 0))
        def _():
            dg_ref[...] = jnp.zeros_like(dg_ref)

        @pl.when(i < nctx)
        def _():
            dqr_ref[...] = jnp.zeros_like(dqr_ref)

        @pl.when(i >= nctx)
        def _():
            qs = _stack_heads(q_ref[...])
            dob = do_ref[...]
            dos = _stack_heads(dob)
            delta = jnp.concatenate(
                [jnp.sum(dob[:, h * HEAD_DIM:(h + 1) * HEAD_DIM].astype(F32)
                         * o_ref[:, h * HEAD_DIM:(h + 1) * HEAD_DIM].astype(F32), axis=-1, keepdims=True)
                 for h in range(GROUP)], axis=0)
            lse = jnp.concatenate([lse_ref[0, :, h:h + 1] for h in range(GROUP)], axis=0)

            def step(c, dq):
                rows = pl.ds(pl.multiple_of(c * kc, kc), kc)
                kk = k_ref[rows, :]
                vv = v_ref[rows, :]
                s = lax.dot_general(qs, kk, NT, preferred_element_type=F32)
                p = jnp.exp2(s - lse)
                dp = lax.dot_general(dos, vv, NT, preferred_element_type=F32)
                ds = (p * (dp - delta)).astype(BF)
                dv_ref[rows, :] += lax.dot_general(p.astype(BF), dos, TN, preferred_element_type=F32)
                dk_ref[rows, :] += lax.dot_general(ds, qs, TN, preferred_element_type=F32)
                return dq + jnp.dot(ds, kk, preferred_element_type=F32)

            dq = lax.fori_loop(0, Ta // kc, step, jnp.zeros((GROUP * tq, HEAD_DIM), F32))
            cos, s_next, s_prev = tab_ref[0], tab_ref[1], tab_ref[2]
            dg = jnp.zeros((1, HEAD_DIM), F32)
            for h in range(GROUP):
                sl = slice(h * HEAD_DIM, (h + 1) * HEAD_DIM)
                dr, dgh = _norm_rope_backward(dq[h * tq:(h + 1) * tq] * zscale, raw_ref[:, sl].astype(F32), g_ref[...],
                                              cos, s_next, s_prev)
                dqr_ref[:, sl] = dr.astype(BF)
                dg = dg + dgh
            dg_ref[0:1, :] += dg

    lat = lambda j, i: (jnp.maximum(i - nctx, 0), j)
    (dproj, dk, dv, dqg), exchanged = _call(
        body, name=name, grid=(N_KV_HEADS, Ta // tq),
        in_specs=[ANY, pl.BlockSpec((tq, gw), lambda j, i: (i, j)),
                  pl.BlockSpec((Ta, HEAD_DIM), lambda j, i: (0, j)),
                  pl.BlockSpec((Ta, HEAD_DIM), lambda j, i: (0, vblk + j)),
                  pl.BlockSpec((tq, gw), lat),
                  pl.BlockSpec((1, tq, GROUP), lambda j, i: (j, jnp.maximum(i - nctx, 0), 0)),
                  pl.BlockSpec((tq, gw), lat),
                  pl.BlockSpec((tq, gw), lambda j, i: (i, qblk + j)),
                  pl.BlockSpec((1, HEAD_DIM), lambda j, i: (0, 0)),
                  pl.BlockSpec((3, tq, HEAD_DIM), lambda j, i: (0, i, 0))],
        out_specs=[pl.BlockSpec((tq, gw), lambda j, i: (i, qblk + j)),
                   pl.BlockSpec((Ta, HEAD_DIM), lambda j, i: (0, j)),
                   pl.BlockSpec((Ta, HEAD_DIM), lambda j, i: (0, j)),
                   pl.BlockSpec((8, HEAD_DIM), lambda j, i: (0, 0))],
        out_shape=[jax.ShapeDtypeStruct(dproj.shape, BF),
                   jax.ShapeDtypeStruct((Ta, N_KV_HEADS * HEAD_DIM), F32),
                   jax.ShapeDtypeStruct((Ta, N_KV_HEADS * HEAD_DIM), F32),
                   jax.ShapeDtypeStruct((8, HEAD_DIM), F32)],
        operands=[dproj, q, k, proj, o, lse, do, proj, qgain, tabs], hosted=hosted, aliases={0: 0},
        params=_params(vmem=VMEM_BIG, sem=("arbitrary", "arbitrary")))
    return (dproj, dk, dv, dqg), exchanged


def kv_backward(dproj, dk, dv, proj, gain, tabs, offs, name):
    Ta = proj.shape[0]
    tm = ROW_TILE
    kw = N_KV_HEADS * HEAD_DIM
    cb = offs["k"] // (2 * kw)
    kb = offs["k"] // kw
    zscale = 1.0 / LOG2E

    def body(dp_any, dk_ref, dv_ref, raw_ref, g_ref, tab_ref, o_ref, dg_ref):
        @pl.when(pl.program_id(0) == 0)
        def _():
            dg_ref[...] = jnp.zeros_like(dg_ref)

        cos, s_next, s_prev = tab_ref[0], tab_ref[1], tab_ref[2]
        dg = jnp.zeros((1, HEAD_DIM), F32)
        for h in range(N_KV_HEADS):
            sl = slice(h * HEAD_DIM, (h + 1) * HEAD_DIM)
            dr, dgh = _norm_rope_backward(dk_ref[:, sl] * zscale, raw_ref[:, sl].astype(F32), g_ref[...], cos, s_next, s_prev)
            o_ref[:, sl] = dr.astype(BF)
            dg = dg + dgh
        o_ref[:, kw:2 * kw] = dv_ref[...].astype(BF)
        dg_ref[0:1, :] += dg

    return pl.pallas_call(
        body, name=name, grid=(Ta // tm,),
        in_specs=[ANY, _row(tm, kw), _row(tm, kw), pl.BlockSpec((tm, kw), lambda i: (i, kb)),
                  _resident((1, HEAD_DIM)), pl.BlockSpec((3, tm, HEAD_DIM), lambda i: (0, i, 0))],
        out_specs=[pl.BlockSpec((tm, 2 * kw), lambda i: (i, cb)), pl.BlockSpec((8, HEAD_DIM), lambda i: (0, 0))],
        out_shape=[jax.ShapeDtypeStruct(dproj.shape, BF), jax.ShapeDtypeStruct((8, HEAD_DIM), F32)],
        input_output_aliases={0: 0},
        compiler_params=_params(sem=("arbitrary",)),
    )(dproj, dk, dv, proj, gain, tabs)


def merge_forward(x1, mod6, yc, o, proj, w_bc, w_ba, w_o, offs, Tc, name):
    T, D = yc.shape[0], x1.shape[1]
    tm = ROW_TILE
    roff = Tc // tm
    gb = offs["gt"] // (2 * D)

    def body(x_ref, mod_ref, yc_ref, o_ref, gt_ref, wbc_ref, wba_ref, wo_ref, xo_ref, pc_ref, pa_ref, m_ref, z_ref):
        gate = mod_ref[0][2:3]
        pc = jnp.dot(yc_ref[...], wbc_ref[...], preferred_element_type=F32)
        pa = jnp.dot(o_ref[...], wba_ref[...], preferred_element_type=F32)
        pc_ref[...] = pc.astype(BF)
        pa_ref[...] = pa.astype(BF)
        mb = (_sigmoid(gt_ref[:, 0:D].astype(F32)) * pc + _sigmoid(gt_ref[:, D:2 * D].astype(F32)) * pa).astype(BF)
        m_ref[...] = mb
        z = jnp.dot(mb, wo_ref[...], preferred_element_type=F32)
        z_ref[...] = z.astype(BF)
        xo_ref[...] = x_ref[...] + gate * z

    return pl.pallas_call(
        body, name=name, grid=(T // tm,),
        in_specs=[pl.BlockSpec((tm, D), lambda i: (i + roff, 0)), _mod_spec(D, 1, 0), _row(tm, yc.shape[1]), _row(tm, o.shape[1]),
                  pl.BlockSpec((tm, 2 * D), lambda i: (i + roff, gb)),
                  _resident(w_bc.shape), _resident(w_ba.shape), _resident(w_o.shape)],
        out_specs=[_row(tm, D)] * 5,
        out_shape=[jax.ShapeDtypeStruct((T, D), F32)] + [jax.ShapeDtypeStruct((T, D), BF)] * 4,
        compiler_params=_params(vmem=VMEM_BIG, sem=("arbitrary",)),
    )(x1, mod6, yc, o, proj, w_bc, w_ba, w_o)


def merge_backward_rows(dx2, mod6, z, pc, pa, proj, w_bc, w_ba, w_o, offs, Tc, name):
    T, D = dx2.shape
    Ta, P = proj.shape
    tm = ROW_TILE
    nctx = Tc // tm
    gb = offs["gt"] // (2 * D)
    dcw, dqw = w_bc.shape[0], w_ba.shape[0]

    def body(dx_ref, mod_ref, z_ref, pc_ref, pa_ref, gt_ref, wbc_ref, wba_ref, wo_ref,
             dgt_ref, dg_ref, dpc_ref, dpa_ref, dyc_ref, do_ref, acc_ref):
        i = pl.program_id(0)

        @pl.when(i == 0)
        def _():
            acc_ref[...] = jnp.zeros_like(acc_ref)

        @pl.when(i < nctx)
        def _():
            dgt_ref[...] = jnp.zeros_like(dgt_ref)

        @pl.when(i >= nctx)
        def _():
            gate = mod_ref[0][2:3]
            dx = dx_ref[...]
            acc_ref[0:1, :] += jnp.sum(dx * z_ref[...].astype(F32), axis=0, keepdims=True)
            dgb = (dx * gate).astype(BF)
            dg_ref[...] = dgb
            dm = lax.dot_general(dgb, wo_ref[...], NT, preferred_element_type=F32)
            sc = _sigmoid(gt_ref[:, 0:D].astype(F32))
            sa = _sigmoid(gt_ref[:, D:2 * D].astype(F32))
            pc = pc_ref[...].astype(F32)
            pa = pa_ref[...].astype(F32)
            dpc = (dm * sc).astype(BF)
            dpa = (dm * sa).astype(BF)
            dpc_ref[...] = dpc
            dpa_ref[...] = dpa
            dgt_ref[:, 0:D] = ((dm * pc) * (sc * (1.0 - sc))).astype(BF)
            dgt_ref[:, D:2 * D] = ((dm * pa) * (sa * (1.0 - sa))).astype(BF)
            dyc_ref[...] = lax.dot_general(dpc, wbc_ref[...], NT, preferred_element_type=F32).astype(BF)
            do_ref[...] = lax.dot_general(dpa, wba_ref[...], NT, preferred_element_type=F32).astype(BF)

    lat = lambda n: pl.BlockSpec((tm, n), lambda i: (jnp.maximum(i - nctx, 0), 0))
    return pl.pallas_call(
        body, name=name, grid=(Ta // tm,),
        in_specs=[lat(D), _mod_spec(D, 1, 0), lat(D), lat(D), lat(D),
                  pl.BlockSpec((tm, 2 * D), lambda i: (i, gb)),
                  _resident(w_bc.shape), _resident(w_ba.shape), _resident(w_o.shape)],
        out_specs=[pl.BlockSpec((tm, 2 * D), lambda i: (i, gb)), lat(D), lat(D), lat(D), lat(dcw), lat(dqw),
                   pl.BlockSpec((8, D), lambda i: (0, 0))],
        out_shape=[jax.ShapeDtypeStruct((Ta, P), BF)] + [jax.ShapeDtypeStruct((T, D), BF)] * 3
                  + [jax.ShapeDtypeStruct((T, dcw), BF), jax.ShapeDtypeStruct((T, dqw), BF), jax.ShapeDtypeStruct((8, D), F32)],
        compiler_params=_params(vmem=VMEM_BIG, sem=("arbitrary",)),
    )(dx2, mod6, z, pc, pa, proj, w_bc, w_ba, w_o)


def proj_backward_rows(dproj, dres, xa, mod6, g, w_in, nctx, name, hosted=None):
    Tr, D = xa.shape
    P = w_in.shape[1]
    tm = ROW_TILE

    def body(dp_ref, dres_ref, x_ref, mod_ref, g_ref, w_ref, dx_ref, acc_ref):
        i = pl.program_id(0)

        @pl.when((i == 0) | (i == nctx))
        def _():
            acc_ref[...] = jnp.zeros_like(acc_ref)

        x = x_ref[...]
        scale = mod_ref[0][1:2]
        gg = g_ref[...]
        dhm = lax.dot_general(dp_ref[...], w_ref[...], NT, preferred_element_type=F32)
        r = lax.rsqrt(jnp.mean(x * x, axis=-1, keepdims=True) + EPS)
        xh = x * r
        dshift = jnp.sum(dhm, axis=0, keepdims=True)
        dscale = jnp.sum(dhm * (xh * gg), axis=0, keepdims=True)
        dxh_g = dhm * (1.0 + scale)
        dg = jnp.sum(dxh_g * xh, axis=0, keepdims=True)
        dxh = dxh_g * gg
        res = jnp.where(i < nctx, 0.0, dres_ref[...])
        dx_ref[...] = res + r * (dxh - xh * jnp.mean(dxh * xh, axis=-1, keepdims=True))
        for k, val in enumerate((dshift, dscale, dg)):
            acc_ref[0, k:k + 1, :] += val

    return _call(
        body, name=name, grid=(Tr // tm,),
        in_specs=[_row(tm, P), pl.BlockSpec((tm, D), lambda i: (jnp.maximum(i - nctx, 0), 0)), _row(tm, D),
                  _mod_spec(D, 1, nctx), _resident((1, D)), _resident(w_in.shape)],
        out_specs=[_row(tm, D), _acc_spec(D, nctx)],
        out_shape=[jax.ShapeDtypeStruct((Tr, D), F32), jax.ShapeDtypeStruct((2, 8, D), F32)],
        operands=[dproj, dres, xa, mod6, g, w_in], hosted=hosted,
        params=_params(vmem=VMEM_BIG, sem=("arbitrary",)))


def _adam_update(w, g, m, v):
    c1 = 1.0 - ADAM_B1 ** ADAM_STEP
    c2 = 1.0 - ADAM_B2 ** ADAM_STEP
    m = ADAM_B1 * m + (1.0 - ADAM_B1) * g
    v = ADAM_B2 * v + (1.0 - ADAM_B2) * (g * g)
    return -ADAM_LR * ((m / c1) / (jnp.sqrt(v / c2) + ADAM_EPS) + ADAM_WD * w), m, v


def adamw(w, g, m, v, name):
    R, C = w.shape
    tr = _row_tile(R, C)

    def body(w_ref, g_ref, m_ref, v_ref, d_ref, nm_ref, nv_ref):
        d_ref[...], nm_ref[...], nv_ref[...] = _adam_update(w_ref[...], g_ref[...], m_ref[...], v_ref[...])

    blk = pl.BlockSpec((tr, C), lambda i: (i, 0))
    return pl.pallas_call(
        body, name=name, grid=(R // tr,),
        in_specs=[blk] * 4, out_specs=[blk] * 3,
        out_shape=[jax.ShapeDtypeStruct((R, C), F32)] * 3,
        compiler_params=_params(vmem=VMEM_BIG, sem=("parallel",)),
    )(w, g, m, v)


def adamw_summed(recv, w, m, v, name):
    R, C = w.shape
    tr = _row_tile(R, C)

    def body(r_ref, w_ref, m_ref, v_ref, g_ref, d_ref, nm_ref, nv_ref):
        g = r_ref[0].astype(F32)
        for a in range(1, N_DEV):
            g = g + r_ref[a].astype(F32)
        g_ref[...] = g
        d_ref[...], nm_ref[...], nv_ref[...] = _adam_update(w_ref[...], g, m_ref[...], v_ref[...])

    blk = pl.BlockSpec((tr, C), lambda i: (i, 0))
    return pl.pallas_call(
        body, name=name, grid=(R // tr,),
        in_specs=[pl.BlockSpec((N_DEV, tr, C), lambda i: (0, i, 0)), blk, blk, blk], out_specs=[blk] * 4,
        out_shape=[jax.ShapeDtypeStruct((R, C), F32)] * 4,
        compiler_params=_params(vmem=VMEM_BIG, sem=("parallel",)),
    )(recv, w, m, v)


def adamw_reduced(place, ps, recv_b, w, m, v, name):
    R, C = w.shape
    tr = _row_tile(R, C)

    def body(place_ref, p_ref, b_ref, w_ref, m_ref, v_ref, g_ref, d_ref, nm_ref, nv_ref):
        g = p_ref[0].astype(F32)
        for j in range(3):
            g = g + b_ref[j].astype(F32)
        g_ref[...] = g
        d_ref[...], nm_ref[...], nv_ref[...] = _adam_update(w_ref[...], g, m_ref[...], v_ref[...])

    blk = pl.BlockSpec((tr, C), lambda r, pr: (r, 0))
    return pl.pallas_call(
        body, name=name,
        grid_spec=pltpu.PrefetchScalarGridSpec(
            num_scalar_prefetch=1, grid=(R // tr,),
            in_specs=[pl.BlockSpec((1, tr, C), lambda r, pr: (pr[1], r, 0)),
                      pl.BlockSpec((3, tr, C), lambda r, pr: (0, r, 0)), blk, blk, blk],
            out_specs=[blk] * 4),
        out_shape=[jax.ShapeDtypeStruct((R, C), F32)] * 4,
        compiler_params=_params(vmem=VMEM_BIG),
    )(place, ps, recv_b, w, m, v)


def _rope_tables(T, Tc):
    rows = T // GRID_W
    n_freq = HEAD_DIM // 4
    inv = ROPE_THETA ** (-jnp.arange(n_freq, dtype=F32) / n_freq)
    ang_r = jnp.arange(rows).astype(F32)[:, None] * inv
    ang_c = jnp.arange(GRID_W).astype(F32)[:, None] * inv
    per_row = lambda a: jnp.broadcast_to(a[:, None, :], (rows, GRID_W, n_freq)).reshape(T, n_freq)
    per_col = lambda a: jnp.broadcast_to(a[None, :, :], (rows, GRID_W, n_freq)).reshape(T, n_freq)
    cr, sr = per_row(jnp.cos(ang_r)), per_row(jnp.sin(ang_r))
    cc, sc = per_col(jnp.cos(ang_c)), per_col(jnp.sin(ang_c))
    zero = jnp.zeros_like(sr)
    cos = jnp.concatenate([cr, cr, cc, cc], axis=1)
    s_next = jnp.concatenate([-sr, zero, -sc, zero], axis=1)
    s_prev = jnp.concatenate([zero, sr, zero, sc], axis=1)
    lat = jnp.stack([cos, s_next, s_prev])
    ctx = jnp.stack([jnp.ones((Tc, HEAD_DIM), F32), jnp.zeros((Tc, HEAD_DIM), F32), jnp.zeros((Tc, HEAD_DIM), F32)])
    return jnp.concatenate([ctx, lat], axis=1)


BIG = ("ffn1_w_in", "ffn1_w_out", "w_in", "w_branch_conv", "w_branch_attn", "w_out", "ffn2_w_in", "ffn2_w_out")


def _regroup_w_in(stacked, D, Dc, qw, kw):
    w = stacked.transpose(1, 0, 2).reshape(D, -1)
    o = 0
    parts = {}
    for nme, wd in (("bg", Dc), ("cg", Dc), ("vc", Dc), ("q", qw), ("k", kw), ("v", kw), ("gt", 2 * D)):
        parts[nme] = w[:, o:o + wd]
        o += wd
    nb = Dc // 128
    cv = jnp.stack([parts[n].reshape(D, nb, 128) for n in ("bg", "cg", "vc")], axis=2).reshape(D, 3 * Dc)
    return jnp.concatenate([cv, parts["q"], parts["gt"], parts["k"], parts["v"]], axis=1)


def _ungroup_w_in_grad(gt_, D, Dc, qw, kw):
    nb = Dc // 128
    cv = gt_[:3 * Dc].reshape(nb, 3, 128, D)
    o = 3 * Dc
    q = gt_[o:o + qw]
    gt = gt_[o + qw:o + qw + 2 * D]
    k = gt_[o + qw + 2 * D:o + qw + 2 * D + kw]
    v = gt_[o + qw + 2 * D + kw:]
    nat = jnp.concatenate([cv[:, 0].reshape(Dc, D), cv[:, 1].reshape(Dc, D), cv[:, 2].reshape(Dc, D), q, k, v, gt], axis=0)
    return nat.reshape(N_DEV, -1, D)


def kernel(x, c, ctx, c_ctx, w_mod, b_mod, norm1_g, norm2_g, norm3_g, ffn1_w_in, ffn1_w_out, w_in, conv_w, q_norm_g, k_norm_g, w_branch_conv, w_branch_attn, w_out, ffn2_w_in, ffn2_w_out, final_g, loss_target, m_c_ctx, m_w_mod, m_b_mod, m_norm1_g, m_norm2_g, m_norm3_g, m_ffn1_w_in, m_ffn1_w_out, m_w_in, m_conv_w, m_q_norm_g, m_k_norm_g, m_w_branch_conv, m_w_branch_attn, m_w_out, m_ffn2_w_in, m_ffn2_w_out, m_final_g, v_c_ctx, v_w_mod, v_b_mod, v_norm1_g, v_norm2_g, v_norm3_g, v_ffn1_w_in, v_ffn1_w_out, v_w_in, v_conv_w, v_q_norm_g, v_k_norm_g, v_w_branch_conv, v_w_branch_attn, v_w_out, v_ffn2_w_in, v_ffn2_w_out, v_final_g):
    weights = dict(c_ctx=c_ctx, w_mod=w_mod, b_mod=b_mod, norm1_g=norm1_g, norm2_g=norm2_g, norm3_g=norm3_g,
                   ffn1_w_in=ffn1_w_in, ffn1_w_out=ffn1_w_out, w_in=w_in, conv_w=conv_w, q_norm_g=q_norm_g,
                   k_norm_g=k_norm_g, w_branch_conv=w_branch_conv, w_branch_attn=w_branch_attn, w_out=w_out,
                   ffn2_w_in=ffn2_w_in, ffn2_w_out=ffn2_w_out, final_g=final_g)
    moms = dict(c_ctx=(m_c_ctx, v_c_ctx), w_mod=(m_w_mod, v_w_mod), b_mod=(m_b_mod, v_b_mod),
                norm1_g=(m_norm1_g, v_norm1_g), norm2_g=(m_norm2_g, v_norm2_g), norm3_g=(m_norm3_g, v_norm3_g),
                ffn1_w_in=(m_ffn1_w_in, v_ffn1_w_in), ffn1_w_out=(m_ffn1_w_out, v_ffn1_w_out), w_in=(m_w_in, v_w_in),
                conv_w=(m_conv_w, v_conv_w), q_norm_g=(m_q_norm_g, v_q_norm_g), k_norm_g=(m_k_norm_g, v_k_norm_g),
                w_branch_conv=(m_w_branch_conv, v_w_branch_conv), w_branch_attn=(m_w_branch_attn, v_w_branch_attn),
                w_out=(m_w_out, v_w_out), ffn2_w_in=(m_ffn2_w_in, v_ffn2_w_in), ffn2_w_out=(m_ffn2_w_out, v_ffn2_w_out),
                final_g=(m_final_g, v_final_g))
    order = list(weights)

    T, D = x.shape[1], x.shape[2]
    Tc = ctx.shape[1]
    nctx = Tc // ROW_TILE
    nd = N_MOD * D
    Dc = conv_w.shape[2] * N_DEV
    qw, kw = N_Q_HEADS * HEAD_DIM, N_KV_HEADS * HEAD_DIM
    offs, o = {}, 0
    for nme, wd in (("cv", 3 * Dc), ("q", qw), ("gt", 2 * D), ("k", kw), ("v", kw)):
        offs[nme] = o
        o += wd

    ax, ay, ac = lax.axis_index("x"), lax.axis_index("y"), lax.axis_index("c")
    me = 4 * ax + 2 * ay + ac
    place = jnp.stack([ac, 2 * ax + ay]).astype(jnp.int32)

    shard = {n: weights[n][0].astype(BF) for n in BIG}
    rows2d = lambda a: a.reshape(-1, a.shape[-1])
    full = {}
    full["ffn1_w_in"], g_ffn1_out = allgather_two_level([shard["ffn1_w_in"], shard["ffn1_w_out"]], "ag_ffn1")
    full["ffn1_w_out"] = rows2d(g_ffn1_out)

    mod_cols = w_mod.shape[2]
    cw_loc = conv_w[0]
    cpad = (-(D + CONV_TAPS * cw_loc.shape[1])) % 128
    pay = jnp.concatenate([c.reshape(1, D), cw_loc.reshape(1, -1), jnp.zeros((1, cpad), F32)], axis=1)
    call = allgather_direct(pay, "ag_cond")
    conv_full = call[:, 0, D:D + CONV_TAPS * cw_loc.shape[1]].reshape(N_DEV, CONV_TAPS, -1).transpose(1, 0, 2).reshape(CONV_TAPS, Dc)
    b_loc = lax.dynamic_slice_in_dim(b_mod, me * mod_cols, mod_cols, axis=1)
    cctx2 = c_ctx.reshape(1, D)
    mod_part = mod_forward(call, cctx2, w_mod[0], b_loc, "mod_fwd")
    mod_all = allgather_direct(mod_part, "ag_mod")
    mod_lat = lax.dynamic_index_in_dim(mod_all, me, axis=1, keepdims=False).reshape(nd)
    mod_ctx = mod_all[:, N_DEV, :].reshape(nd)
    mod6 = jnp.stack([mod_ctx, mod_lat]).reshape(6, 3, D)

    tabs = _rope_tables(T, Tc)

    srcs1 = (ctx[0], x[0])
    (xa1, hm1, ab1, h1, f1), (g_w_in,) = ffn_forward(
        srcs1, mod6, 0, norm1_g, full["ffn1_w_in"], full["ffn1_w_out"], nctx, "ffn1_fwd",
        hosted=Hosted(gathers=[shard["w_in"]]))
    full["w_in"] = _regroup_w_in(g_w_in, D, Dc, qw, kw)
    merge_names = ("w_branch_conv", "w_branch_attn", "w_out")
    (hx, proj, qr, kr), g_merge = proj_forward(
        xa1, mod6, norm2_g, full["w_in"], q_norm_g, k_norm_g, tabs, offs, nctx, "proj_fwd",
        hosted=Hosted(gathers=[shard[n] for n in merge_names]))
    full.update({n: rows2d(g) for n, g in zip(merge_names, g_merge)})
    yc = conv_forward(proj, conv_full, offs, Tc, "conv_fwd")
    (oa, lse), (full["ffn2_w_in"], g_ffn2_out) = attention_forward(
        qr, kr, proj, offs, Tc, "attn_fwd", hosted=Hosted(gathers=[shard["ffn2_w_in"], shard["ffn2_w_out"]]))
    full["ffn2_w_out"] = rows2d(g_ffn2_out)
    x2, pc, pa, mm, zz = merge_forward(xa1, mod6, yc, oa, proj, full["w_branch_conv"], full["w_branch_attn"],
                                       full["w_out"], offs, Tc, "merge_fwd")
    (dx3, hm2, ab2, h2, f2, lacc), _ = ffn_forward((x2,), mod6, 2, norm3_g, full["ffn2_w_in"], full["ffn2_w_out"], 0, "ffn2_fwd",
                                                   final=(loss_target[0], final_g.reshape(1, D)))
    loss = lax.psum(lacc[1, 0], ("x", "y", "c"))

    by_dest = lambda g: g.reshape((N_DEV, -1, g.shape[-1]))
    (dx2, dab2, df2, acc_f2), _ = ffn_backward_rows(dx3, (x2,), mod6, 2, norm3_g, ab2, f2, full["ffn2_w_in"], full["ffn2_w_out"], 0, "ffn2_bwd")
    early = {"ffn2_w_out": tn_ffn_out(h2, df2, "ffn2_dwout"), "ffn2_w_in": tn_ffn_in(hm2, dab2, "ffn2_dwin")[0]}
    dproj, dgm, dpc, dpa, dyc, do, acc_mg = merge_backward_rows(dx2, mod6, zz, pc, pa, proj, full["w_branch_conv"],
                                                                full["w_branch_attn"], full["w_out"], offs, Tc, "merge_bwd")
    early["w_out"] = tn_matmul(mm, dgm, "dw_out")
    early["w_branch_conv"] = tn_matmul(yc, dpc, "dw_bc")
    early["w_branch_attn"] = tn_matmul(oa, dpa, "dw_ba")
    dproj, dcw = conv_backward(dproj, dyc, proj, conv_full, offs, Tc, "conv_bwd")
    (dproj, dk, dv, dqg), summed = attention_backward(dproj, qr, kr, proj, oa, lse, do, q_norm_g, tabs, offs, Tc, "attn_bwd",
                                                      hosted=Hosted(scatters=[by_dest(g) for g in early.values()]))
    summed = dict(zip(early, summed))
    dproj, dkg = kv_backward(dproj, dk, dv, proj, k_norm_g, tabs, offs, "kv_bwd")
    g_w_in_grad = _ungroup_w_in_grad(tn_matmul(dproj, hx, "dw_in"), D, Dc, qw, kw)
    (dxa1, acc_pj), (summed["w_in"],) = proj_backward_rows(dproj, dx2, xa1, mod6, norm2_g, full["w_in"], nctx, "proj_bwd",
                                                          hosted=Hosted(scatters=[g_w_in_grad]))
    (grad_x2d, dab1, df1, acc_f1), _ = ffn_backward_rows(
        dxa1, srcs1, mod6, 0, norm1_g, ab1, f1, full["ffn1_w_in"], full["ffn1_w_out"], nctx, "ffn1_bwd")
    g_ffn1_in, (summed["ffn1_w_out"],) = tn_ffn_in(
        hm1, dab1, "ffn1_dwin", hosted=Hosted(scatters=[by_dest(tn_ffn_out(h1, df1, "ffn1_dwout"))]))
    late = ("ffn1_w_in",)
    gs = [by_dest(g_ffn1_in)]
    grad_x = grad_x2d[None]
    recv_a = rs_sibling_exchange(gs, "rs_sibling")
    ps = [rs_pair_sum(place, g, a, "rs_pair_sum_" + n) for n, g, a in zip(late, gs, recv_a)]
    recv_b = rs_chip_exchange(ps, "rs_chips")
    reduced = dict(zip(late, zip(ps, recv_b)))

    zero_d = jnp.zeros((D,), F32)
    dlat = jnp.concatenate([acc_f1[1, 0], acc_f1[1, 1], acc_f1[1, 2], acc_pj[1, 0], acc_pj[1, 1], acc_mg[0],
                            acc_f2[1, 0], acc_f2[1, 1], acc_f2[1, 2]])
    dctx = jnp.concatenate([acc_f1[0, 0], acc_f1[0, 1], acc_f1[0, 2], acc_pj[0, 0], acc_pj[0, 1]] + [zero_d] * 4)
    small = jnp.concatenate([acc_f1[0, 3] + acc_f1[1, 3], acc_pj[0, 2] + acc_pj[1, 2], acc_f2[1, 3],
                             dqg[0], dkg[0], lacc[0], dcw[0:CONV_TAPS].reshape(-1)])
    n_small = small.shape[0]
    pay_b = jnp.concatenate([dlat, dctx, small]).reshape(1, -1)
    gath = allgather_direct(pay_b, "ag_small_grads")
    dlat_loc = lax.dynamic_slice_in_dim(gath[:, 0, :nd], me * mod_cols, mod_cols, axis=1)
    dctx_loc = lax.dynamic_slice_in_dim(gath[:, 0, nd:2 * nd], me * mod_cols, mod_cols, axis=1)
    g_wmod, pc_part, small_sum = mod_backward(call, cctx2, w_mod[0], dlat_loc, dctx_loc, gath, 2 * nd, n_small, "mod_bwd")
    pcs = allgather_direct(pc_part, "ag_cctx")
    g_bmod, g_cctx = bmod_and_cctx_grad(gath, pcs, cctx2, nd, "small_bwd")
    sm = small_sum[0]
    g_conv_full = sm[3 * D + 2 * HEAD_DIM + D:].reshape(CONV_TAPS, Dc)
    g_conv = lax.dynamic_slice_in_dim(g_conv_full, me * cw_loc.shape[1], cw_loc.shape[1], axis=1)
    gsmall = dict(
        c_ctx=g_cctx, w_mod=g_wmod, b_mod=g_bmod, norm1_g=sm[0:D][None], norm2_g=sm[D:2 * D][None],
        norm3_g=sm[2 * D:3 * D][None], q_norm_g=sm[3 * D:3 * D + HEAD_DIM][None],
        k_norm_g=sm[3 * D + HEAD_DIM:3 * D + 2 * HEAD_DIM][None],
        final_g=sm[3 * D + 2 * HEAD_DIM:3 * D + 2 * HEAD_DIM + D][None], conv_w=g_conv)

    g_out, d_out, m_out, v_out = [], [], [], []
    flipped = ("ffn1_w_in", "w_in", "ffn2_w_in")
    for n in order:
        w = weights[n]
        shp = w.shape
        if n in flipped:
            two_d = lambda a: jnp.swapaxes(a[0], 0, 1)
            back = lambda a: jnp.swapaxes(a, 0, 1)[None]
        else:
            two_d = lambda a: a.reshape(-1, shp[-1])
            back = lambda a: a.reshape(shp)
        m, v = moms[n]
        if n in reduced:
            g2, d, nm, nv = adamw_reduced(place, *reduced[n], two_d(w), two_d(m), two_d(v), "adamw_" + n)
        elif n in summed:
            g2, d, nm, nv = adamw_summed(summed[n], two_d(w), two_d(m), two_d(v), "adamw_" + n)
        else:
            g2 = gsmall[n].reshape(two_d(w).shape)
            d, nm, nv = adamw(two_d(w), g2, two_d(m), two_d(v), "adamw_" + n)
        g_out.append(back(g2))
        d_out.append(back(d))
        m_out.append(back(nm))
        v_out.append(back(nv))
    return (loss, grad_x, *g_out, *d_out, *m_out, *v_out)
```

```python
import math

import jax
import jax.numpy as jnp
from jax import lax
from jax.experimental import pallas as pl
from jax.experimental.pallas import tpu as pltpu

F32 = jnp.float32
BF = jnp.bfloat16
EPS = 1e-6
N_DEV = 8
HEAD_DIM = 128
N_Q_HEADS = 8
N_KV_HEADS = 2
GROUP = N_Q_HEADS // N_KV_HEADS
GRID_W = 64
ROPE_THETA = 10000.0
CONV_TAPS = 3
N_MOD = 9
ADAM_LR = 0.001
ADAM_B1 = 0.9
ADAM_B2 = 0.999
ADAM_EPS = 1e-08
ADAM_WD = 0.01
ADAM_STEP = 10
ROW_TILE = 256
VMEM_BIG = 56 << 20
MESH_ID = pl.DeviceIdType.MESH
HIGHEST = lax.Precision.HIGHEST
NT = (((1,), (1,)), ((), ()))
TN = (((0,), (0,)), ((), ()))
LOG2E = math.log2(math.e)


def _pick(n, cands):
    for c in cands:
        if n % c == 0:
            return c
    return n


def _params(vmem=None, sem=None):
    kw = {}
    if vmem is not None:
        kw["vmem_limit_bytes"] = vmem
    if sem is not None:
        kw["dimension_semantics"] = sem
    return pltpu.CompilerParams(**kw)


def _resident(shape):
    nd = len(shape)
    return pl.BlockSpec(shape, lambda *_: (0,) * nd, pipeline_mode=pl.Buffered(1))


def _sigmoid(x):
    return jax.nn.sigmoid(x)


ANY = pl.BlockSpec(memory_space=pl.ANY)


def _coords():
    return lax.axis_index("x"), lax.axis_index("y"), lax.axis_index("c")


def _flip(v, bit):
    return 1 - v if bit else v


def _remote(src, dst, ssem, rsem, dev):
    return pltpu.make_async_remote_copy(src_ref=src, dst_ref=dst, send_sem=ssem, recv_sem=rsem,
                                        device_id=dev, device_id_type=MESH_ID)


def allgather_direct(v, name):
    def body(v_ref, out_ref, ssem, rsem, lsem):
        x, y, c = _coords()
        me = 4 * x + 2 * y + c
        mine = pltpu.make_async_copy(v_ref, out_ref.at[me], lsem)
        mine.start()
        cps = []
        for p in range(1, N_DEV):
            px, py, pc = (p >> 2) & 1, (p >> 1) & 1, p & 1
            cps.append(_remote(v_ref, out_ref.at[me], ssem.at[p - 1], rsem.at[p - 1],
                               (_flip(x, px), _flip(y, py), _flip(c, pc))))
        for cp in cps:
            cp.start()
        for p in range(1, N_DEV):
            px, py, pc = (p >> 2) & 1, (p >> 1) & 1, p & 1
            src = 4 * _flip(x, px) + 2 * _flip(y, py) + _flip(c, pc)
            _remote(v_ref, out_ref.at[src], ssem.at[p - 1], rsem.at[p - 1], (x, y, c)).wait_recv()
        for cp in cps:
            cp.wait_send()
        mine.wait()

    return pl.pallas_call(
        body, name=name,
        out_shape=jax.ShapeDtypeStruct((N_DEV,) + v.shape, v.dtype),
        in_specs=[ANY], out_specs=ANY,
        scratch_shapes=[pltpu.SemaphoreType.DMA((N_DEV - 1,)), pltpu.SemaphoreType.DMA((N_DEV - 1,)),
                        pltpu.SemaphoreType.DMA],
    )(v)


def allgather_two_level(shards, name):
    n = len(shards)

    def body(*refs):
        v_refs, out_refs, (ssem, rsem, lsem) = refs[:n], refs[n:2 * n], refs[2 * n:]
        x, y, c = _coords()
        me = (x, y, c)
        sib = (x, y, 1 - c)
        chips = [(1 - x, y), (x, 1 - y), (1 - x, 1 - y)]

        def slot(w, px, py, pc):
            return out_refs[w].at[4 * px + 2 * py + pc]

        def sem(w, k):
            return ssem.at[7 * w + k], rsem.at[7 * w + k]

        mine = [pltpu.make_async_copy(v_refs[w], slot(w, *me), lsem.at[w]) for w in range(n)]
        for cp in mine:
            cp.start()
        first = []
        for w in range(n):
            first.append(_remote(v_refs[w], slot(w, *me), *sem(w, 0), sib))
            first += [_remote(v_refs[w], slot(w, *me), *sem(w, 1 + j), (*chip, c)) for j, chip in enumerate(chips)]
        for cp in first:
            cp.start()
        passed = []
        for w in range(n):
            for j, chip in enumerate(chips):
                _remote(v_refs[w], slot(w, *chip, c), *sem(w, 1 + j), me).wait_recv()
                cp = _remote(slot(w, *chip, c), slot(w, *chip, c), *sem(w, 4 + j), sib)
                cp.start()
                passed.append(cp)
        for w in range(n):
            _remote(v_refs[w], slot(w, x, y, 1 - c), *sem(w, 0), me).wait_recv()
            for j, chip in enumerate(chips):
                _remote(v_refs[w], slot(w, *chip, 1 - c), *sem(w, 4 + j), me).wait_recv()
        for cp in first + passed:
            cp.wait_send()
        for cp in mine:
            cp.wait()

    return pl.pallas_call(
        body, name=name,
        out_shape=[jax.ShapeDtypeStruct((N_DEV,) + s.shape, s.dtype) for s in shards],
        in_specs=[ANY] * n, out_specs=[ANY] * n,
        scratch_shapes=[pltpu.SemaphoreType.DMA((7 * n,)), pltpu.SemaphoreType.DMA((7 * n,)),
                        pltpu.SemaphoreType.DMA((n,))],
    )(*shards)


def rs_sibling_exchange(gs, name):
    n = len(gs)

    def body(*refs):
        g_refs, out_refs, (ssem, rsem) = refs[:n], refs[n:2 * n], refs[2 * n:]
        x, y, c = _coords()
        sib = (x, y, 1 - c)
        cps = [_remote(g_refs[w].at[2 * k + (1 - c)], out_refs[w].at[k], ssem.at[4 * w + k], rsem.at[4 * w + k], sib)
               for w in range(n) for k in range(4)]
        for cp in cps:
            cp.start()
        for cp in cps:
            cp.wait()

    return pl.pallas_call(
        body, name=name,
        out_shape=[jax.ShapeDtypeStruct((4,) + g.shape[1:], g.dtype) for g in gs],
        in_specs=[ANY] * n, out_specs=[ANY] * n,
        scratch_shapes=[pltpu.SemaphoreType.DMA((4 * n,)), pltpu.SemaphoreType.DMA((4 * n,))],
    )(*gs)


def rs_chip_exchange(ps, name):
    n = len(ps)

    def body(*refs):
        p_refs, out_refs, (ssem, rsem) = refs[:n], refs[n:2 * n], refs[2 * n:]
        x, y, c = _coords()
        chips = [(1 - x, y), (x, 1 - y), (1 - x, 1 - y)]
        cps = [_remote(p_refs[w].at[2 * cx + cy], out_refs[w].at[j], ssem.at[3 * w + j], rsem.at[3 * w + j], (cx, cy, c))
               for w in range(n) for j, (cx, cy) in enumerate(chips)]
        for cp in cps:
            cp.start()
        for cp in cps:
            cp.wait()

    return pl.pallas_call(
        body, name=name,
        out_shape=[jax.ShapeDtypeStruct((3,) + p.shape[1:], p.dtype) for p in ps],
        in_specs=[ANY] * n, out_specs=[ANY] * n,
        scratch_shapes=[pltpu.SemaphoreType.DMA((3 * n,)), pltpu.SemaphoreType.DMA((3 * n,))],
    )(*ps)


class Hosted:
    def __init__(self, gathers=(), scatters=()):
        self.items = [(a, False) for a in gathers] + [(a, True) for a in scatters]
        self.n = len(self.items)
        self.operands = [a for a, _ in self.items]
        self.out_shapes = [jax.ShapeDtypeStruct(a.shape if sc else (N_DEV,) + a.shape, a.dtype) for a, sc in self.items]
        self.scratch = [pltpu.SemaphoreType.DMA((7 * self.n,)), pltpu.SemaphoreType.DMA((7 * self.n,)),
                        pltpu.SemaphoreType.DMA((self.n,))]

    def _copies(self, in_refs, out_refs, ssem, rsem, lsem, arrivals):
        x, y, c = _coords()
        me = 4 * x + 2 * y + c
        remote, local = [], []
        for w, (_, sc) in enumerate(self.items):
            src, dst = in_refs[w], out_refs[w]
            local.append(pltpu.make_async_copy(src.at[me] if sc else src, dst.at[me], lsem.at[w]))
            for p in range(1, N_DEV):
                px, py, pc = _flip(x, (p >> 2) & 1), _flip(y, (p >> 1) & 1), _flip(c, p & 1)
                peer = 4 * px + 2 * py + pc
                k = 7 * w + p - 1
                if arrivals:
                    remote.append(_remote(src.at[me] if sc else src, dst.at[peer], ssem.at[k], rsem.at[k], (x, y, c)))
                else:
                    remote.append(_remote(src.at[peer] if sc else src, dst.at[me], ssem.at[k], rsem.at[k], (px, py, pc)))
        return remote, local

    def start(self, in_refs, out_refs, ssem, rsem, lsem):
        sends, local = self._copies(in_refs, out_refs, ssem, rsem, lsem, False)
        for cp in local + sends:
            cp.start()

    def wait(self, in_refs, out_refs, ssem, rsem, lsem):
        arrivals, local = self._copies(in_refs, out_refs, ssem, rsem, lsem, True)
        for cp in arrivals:
            cp.wait_recv()
        for cp in arrivals:
            cp.wait_send()
        for cp in local:
            cp.wait()


def _call(body, *, name, grid, in_specs, out_specs, out_shape, operands, params, scratch_shapes=(), aliases=None, hosted=None):
    n_in, n_out, n_scr = len(in_specs), len(out_specs), len(scratch_shapes)
    h = hosted.n if hosted is not None else 0

    def wrapped(*refs):
        ins, cins = refs[:n_in], refs[n_in:n_in + h]
        outs, couts = refs[n_in + h:n_in + h + n_out], refs[n_in + h + n_out:n_in + 2 * h + n_out]
        rest = refs[n_in + 2 * h + n_out:]
        scr, sems = rest[:n_scr], rest[n_scr:]
        if h:
            ids = [pl.program_id(a) for a in range(len(grid))]
            first, last = ids[0] == 0, ids[0] == grid[0] - 1
            for a in range(1, len(grid)):
                first, last = first & (ids[a] == 0), last & (ids[a] == grid[a] - 1)

            @pl.when(first)
            def _():
                hosted.start(cins, couts, *sems)

        body(*ins, *outs, *scr)
        if h:
            @pl.when(last)
            def _():
                hosted.wait(cins, couts, *sems)

    res = pl.pallas_call(
        wrapped, name=name, grid=grid,
        in_specs=list(in_specs) + [ANY] * h, out_specs=list(out_specs) + [ANY] * h,
        out_shape=list(out_shape) + (hosted.out_shapes if h else []),
        scratch_shapes=list(scratch_shapes) + (hosted.scratch if h else []),
        input_output_aliases=aliases or {}, compiler_params=params,
    )(*operands, *(hosted.operands if h else []))
    return list(res[:n_out]), list(res[n_out:])


def _row_tile(R, C):
    if R * C <= (1 << 18):
        return R
    return max((d for d in range(8, 257, 8) if R % d == 0), default=R)


def rs_pair_sum(place, gs, recv_a, name):
    _, R, C = gs.shape
    tr = _row_tile(R, C)

    def body(place_ref, g_ref, a_ref, o_ref):
        o_ref[0] = (g_ref[0].astype(F32) + a_ref[0].astype(F32)).astype(o_ref.dtype)

    return pl.pallas_call(
        body, name=name,
        grid_spec=pltpu.PrefetchScalarGridSpec(
            num_scalar_prefetch=1, grid=(4, R // tr),
            in_specs=[pl.BlockSpec((1, tr, C), lambda k, r, pr: (2 * k + pr[0], r, 0)),
                      pl.BlockSpec((1, tr, C), lambda k, r, pr: (k, r, 0))],
            out_specs=pl.BlockSpec((1, tr, C), lambda k, r, pr: (k, r, 0))),
        out_shape=jax.ShapeDtypeStruct((4, R, C), BF),
    )(place, gs, recv_a)


def _cond_rows(call_ref, cctx_ref, z_ref, D):
    z_ref[...] = jnp.zeros_like(z_ref)
    for a in range(N_DEV):
        z_ref[a:a + 1, :] = call_ref[a][:, :D]
    z_ref[N_DEV:N_DEV + 1, :] = cctx_ref[...]


def mod_forward(call, c_ctx, w_loc, b_loc, name):
    D, cols = w_loc.shape

    def body(call_ref, cctx_ref, w_ref, b_ref, o_ref, z_ref):
        _cond_rows(call_ref, cctx_ref, z_ref, D)
        z = z_ref[...]
        s = z * _sigmoid(z)
        o_ref[...] = jnp.dot(s, w_ref[...], preferred_element_type=F32, precision=HIGHEST) + b_ref[...]

    return pl.pallas_call(
        body, name=name, out_shape=jax.ShapeDtypeStruct((16, cols), F32),
        scratch_shapes=[pltpu.VMEM((16, D), F32)],
        compiler_params=_params(vmem=VMEM_BIG),
    )(call, c_ctx, w_loc, b_loc)


def mod_backward(call, c_ctx, w_loc, dlat_loc, dctx_loc, gath, n_small_off, n_small, name):
    D, cols = w_loc.shape

    def body(call_ref, cctx_ref, w_ref, dlat_ref, dctx_ref, g_ref, gw_ref, pc_ref, small_ref, z_ref, dm_ref):
        _cond_rows(call_ref, cctx_ref, z_ref, D)
        z = z_ref[...]
        s = z * _sigmoid(z)
        dctx = dctx_ref[0:1, :]
        for a in range(1, N_DEV):
            dctx = dctx + dctx_ref[a:a + 1, :]
        dm_ref[...] = jnp.zeros_like(dm_ref)
        dm_ref[0:N_DEV, :] = dlat_ref[...]
        dm_ref[N_DEV:N_DEV + 1, :] = dctx
        gw_ref[...] = lax.dot_general(s, dm_ref[...], TN, preferred_element_type=F32, precision=HIGHEST)
        pc_ref[...] = lax.dot_general(dctx, w_ref[...], NT, preferred_element_type=F32, precision=HIGHEST)
        acc = g_ref[0][:, n_small_off:n_small_off + n_small]
        for a in range(1, N_DEV):
            acc = acc + g_ref[a][:, n_small_off:n_small_off + n_small]
        small_ref[...] = acc

    return pl.pallas_call(
        body, name=name,
        out_shape=(jax.ShapeDtypeStruct((D, cols), F32), jax.ShapeDtypeStruct((1, D), F32),
                   jax.ShapeDtypeStruct((1, n_small), F32)),
        scratch_shapes=[pltpu.VMEM((16, D), F32), pltpu.VMEM((16, cols), F32)],
        compiler_params=_params(vmem=VMEM_BIG),
    )(call, c_ctx, w_loc, dlat_loc, dctx_loc, gath)


def bmod_and_cctx_grad(gath, pcs, c_ctx, nd, name):
    D = c_ctx.shape[-1]

    def body(g_ref, pc_ref, cctx_ref, gb_ref, gc_ref):
        acc = g_ref[0][:, :nd] + g_ref[0][:, nd:2 * nd]
        for a in range(1, N_DEV):
            acc = acc + (g_ref[a][:, :nd] + g_ref[a][:, nd:2 * nd])
        gb_ref[...] = acc
        p = pc_ref[0]
        for a in range(1, N_DEV):
            p = p + pc_ref[a]
        z = cctx_ref[...]
        sg = _sigmoid(z)
        gc_ref[...] = p * (sg * (1.0 + z * (1.0 - sg)))

    return pl.pallas_call(
        body, name=name,
        out_shape=(jax.ShapeDtypeStruct((1, nd), F32), jax.ShapeDtypeStruct((1, D), F32)),
    )(gath, pcs, c_ctx)


def _mod_spec(D, which, nctx):
    return pl.BlockSpec((1, 3, D), lambda i: (jnp.where(i < nctx, 0, 3) + which, 0, 0))


def _acc_spec(D, nctx):
    return pl.BlockSpec((1, 8, D), lambda i: (jnp.where(i < nctx, 0, 1), 0, 0))


def _row(tm, n):
    return pl.BlockSpec((tm, n), lambda i: (i, 0))


def _two_stream_specs(tm, D, nctx):
    return [pl.BlockSpec((tm, D), lambda i: (jnp.minimum(i, nctx - 1), 0)),
            pl.BlockSpec((tm, D), lambda i: (jnp.maximum(i - nctx, 0), 0))]


def _final_norm_loss_backward(x, tgt, gg, i, dx_ref, acc_ref):
    @pl.when(i == 0)
    def _():
        acc_ref[...] = jnp.zeros_like(acc_ref)

    D = x.shape[1]
    r = lax.rsqrt(jnp.mean(x * x, axis=-1, keepdims=True) + EPS)
    xh = x * r
    e = xh * gg - tgt
    part = 0.5 * jnp.sum(jnp.mean(e * e, axis=-1, keepdims=True), axis=0, keepdims=True)
    dy = e * (1.0 / D)
    dyg = dy * gg
    dx_ref[...] = r * (dyg - xh * jnp.mean(dyg * xh, axis=-1, keepdims=True))
    acc_ref[0:1, :] += jnp.sum(dy * xh, axis=0, keepdims=True)
    acc_ref[1:2, :] += jnp.broadcast_to(part, (1, D))


def _hidden_chunks(F):
    step = 1024 if F % 256 == 0 else F
    return [(lo, min(lo + step, F)) for lo in range(0, F, step)]


def ffn_forward(srcs, mod6, which, g, wt, w_out, nctx, name, hosted=None, final=None):
    D = srcs[-1].shape[1]
    Tr = sum(s.shape[0] for s in srcs)
    F = wt.shape[0] // 2
    tm = ROW_TILE
    two = len(srcs) == 2
    nfin = 0 if final is None else 2

    def body(*refs):
        x_refs, fin_refs, rest = refs[:len(srcs)], refs[len(srcs):len(srcs) + nfin], refs[len(srcs) + nfin:]
        mod_ref, g_ref, wt_ref, wout_ref, xo_ref, hm_ref, ab_ref, h_ref, f_ref = rest[:9]
        x = jnp.where(pl.program_id(0) < nctx, x_refs[0][...], x_refs[1][...]) if two else x_refs[0][...]
        ms = mod_ref[0]
        shift, scale, gate = ms[0:1], ms[1:2], ms[2:3]
        r = lax.rsqrt(jnp.mean(x * x, axis=-1, keepdims=True) + EPS)
        hb = (((x * r) * g_ref[...]) * (1.0 + scale) + shift).astype(BF)
        hm_ref[...] = hb
        f = jnp.zeros((tm, D), F32)
        for lo, hi in _hidden_chunks(F):
            a = lax.dot_general(hb, wt_ref[lo:hi, :], NT, preferred_element_type=F32)
            b = lax.dot_general(hb, wt_ref[F + lo:F + hi, :], NT, preferred_element_type=F32)
            ab_ref[:, lo:hi] = a.astype(BF)
            ab_ref[:, F + lo:F + hi] = b.astype(BF)
            h = ((a * _sigmoid(a)) * b).astype(BF)
            h_ref[:, lo:hi] = h
            f = f + jnp.dot(h, wout_ref[lo:hi, :], preferred_element_type=F32)
        f_ref[...] = f.astype(BF)
        xo = x + (0.5 * gate) * f
        if final is None:
            xo_ref[...] = xo
        else:
            _final_norm_loss_backward(xo, fin_refs[0][...], fin_refs[1][...], pl.program_id(0), xo_ref, rest[9])

    src_specs = _two_stream_specs(tm, D, nctx) if two else [_row(tm, D)]
    fin = final is not None
    return _call(
        body, name=name, grid=(Tr // tm,),
        in_specs=src_specs + ([_row(tm, D), _resident((1, D))] if fin else [])
                 + [_mod_spec(D, which, nctx), _resident((1, D)), _resident(wt.shape), _resident(w_out.shape)],
        out_specs=[_row(tm, D), _row(tm, D), _row(tm, 2 * F), _row(tm, F), _row(tm, D)]
                  + ([pl.BlockSpec((8, D), lambda i: (0, 0))] if fin else []),
        out_shape=[jax.ShapeDtypeStruct((Tr, D), F32), jax.ShapeDtypeStruct((Tr, D), BF),
                   jax.ShapeDtypeStruct((Tr, 2 * F), BF), jax.ShapeDtypeStruct((Tr, F), BF),
                   jax.ShapeDtypeStruct((Tr, D), BF)] + ([jax.ShapeDtypeStruct((8, D), F32)] if fin else []),
        operands=[*srcs, *(final or ()), mod6, g, wt, w_out], hosted=hosted,
        params=_params(vmem=VMEM_BIG, sem=("arbitrary",)))


def ffn_backward_rows(dxo, srcs, mod6, which, g, ab, fo, wt, w_out, nctx, name, hosted=None):
    D = srcs[-1].shape[1]
    Tr = sum(s.shape[0] for s in srcs)
    Tl = srcs[-1].shape[0]
    F = wt.shape[0] // 2
    tm = ROW_TILE
    two = len(srcs) == 2

    def body(*refs):
        dxo_ref, x_refs = refs[0], refs[1:1 + len(srcs)]
        mod_ref, g_ref, ab_ref, fo_ref, wt_ref, wout_ref, dx_ref, dab_ref, df_ref, acc_ref = refs[1 + len(srcs):]
        i = pl.program_id(0)

        @pl.when((i == 0) | (i == nctx))
        def _():
            acc_ref[...] = jnp.zeros_like(acc_ref)

        dxo = dxo_ref[...]
        x = jnp.where(i < nctx, x_refs[0][...], x_refs[1][...]) if two else x_refs[0][...]
        ms = mod_ref[0]
        scale, gate = ms[1:2], ms[2:3]
        gg = g_ref[...]
        dgate = jnp.sum(dxo * fo_ref[...].astype(F32), axis=0, keepdims=True) * 0.5
        dfb = (dxo * (0.5 * gate)).astype(BF)
        df_ref[...] = dfb
        dhm = jnp.zeros((tm, D), F32)
        for lo, hi in _hidden_chunks(F):
            dh = lax.dot_general(dfb, wout_ref[lo:hi, :], NT, preferred_element_type=F32)
            a = ab_ref[:, lo:hi].astype(F32)
            b = ab_ref[:, F + lo:F + hi].astype(F32)
            sg = _sigmoid(a)
            da = ((dh * b) * (sg * (1.0 + a * (1.0 - sg)))).astype(BF)
            db = (dh * (a * sg)).astype(BF)
            dab_ref[:, lo:hi] = da
            dab_ref[:, F + lo:F + hi] = db
            dhm = dhm + jnp.dot(da, wt_ref[lo:hi, :], preferred_element_type=F32)
            dhm = dhm + jnp.dot(db, wt_ref[F + lo:F + hi, :], preferred_element_type=F32)
        r = lax.rsqrt(jnp.mean(x * x, axis=-1, keepdims=True) + EPS)
        xh = x * r
        dshift = jnp.sum(dhm, axis=0, keepdims=True)
        dscale = jnp.sum(dhm * (xh * gg), axis=0, keepdims=True)
        dxh_g = dhm * (1.0 + scale)
        dg = jnp.sum(dxh_g * xh, axis=0, keepdims=True)
        dxh = dxh_g * gg
        dx_ref[...] = dxo + r * (dxh - xh * jnp.mean(dxh * xh, axis=-1, keepdims=True))
        for k, val in enumerate((dshift, dscale, dgate, dg)):
            acc_ref[0, k:k + 1, :] += val

    src_specs = _two_stream_specs(tm, D, nctx) if two else [_row(tm, D)]
    dx_spec = pl.BlockSpec((tm, D), lambda i: (jnp.maximum(i - nctx, 0), 0))
    return _call(
        body, name=name, grid=(Tr // tm,),
        in_specs=[_row(tm, D)] + src_specs + [_mod_spec(D, which, nctx), _resident((1, D)), _row(tm, 2 * F), _row(tm, D),
                                              _resident(wt.shape), _resident(w_out.shape)],
        out_specs=[dx_spec, _row(tm, 2 * F), _row(tm, D), _acc_spec(D, nctx)],
        out_shape=[jax.ShapeDtypeStruct((Tl, D), F32), jax.ShapeDtypeStruct((Tr, 2 * F), BF),
                   jax.ShapeDtypeStruct((Tr, D), BF), jax.ShapeDtypeStruct((2, 8, D), F32)],
        operands=[dxo, *srcs, mod6, g, ab, fo, wt, w_out], hosted=hosted,
        params=_params(vmem=VMEM_BIG, sem=("arbitrary",)))


def _token_tile(T):
    return _pick(T, (2048, 1408, 1024, 768, 512, 256, 128))


def tn_matmul(a, b, name, hosted=None):
    T, K = a.shape
    N = b.shape[1]
    tk = _pick(K, (1024, 1408, 1664, 768, 512, 384, 256, 128))
    tn = _pick(N, (1024, 1408, 1664, 768, 512, 384, 256, 128))
    tt = _token_tile(T)
    nt = T // tt

    def body(a_ref, b_ref, o_ref, acc_ref):
        t = pl.program_id(2)

        @pl.when(t == 0)
        def _():
            acc_ref[...] = jnp.zeros_like(acc_ref)

        acc_ref[...] += lax.dot_general(a_ref[...], b_ref[...], TN, preferred_element_type=F32)

        @pl.when(t == nt - 1)
        def _():
            o_ref[...] = acc_ref[...].astype(BF)

    (out,), exchanged = _call(
        body, name=name, grid=(K // tk, N // tn, nt),
        in_specs=[pl.BlockSpec((tt, tk), lambda k, n, t: (t, k)), pl.BlockSpec((tt, tn), lambda k, n, t: (t, n))],
        out_specs=[pl.BlockSpec((tk, tn), lambda k, n, t: (k, n))],
        out_shape=[jax.ShapeDtypeStruct((K, N), BF)],
        scratch_shapes=[pltpu.VMEM((tk, tn), F32)],
        operands=[a, b], hosted=hosted,
        params=_params(vmem=VMEM_BIG, sem=("arbitrary", "arbitrary", "arbitrary")))
    return out, exchanged


def _rope_apply(y, cos, s_next, s_prev):
    return y * cos + pltpu.roll(y, HEAD_DIM - 32, 1) * s_next + pltpu.roll(y, 32, 1) * s_prev


def _rope_transpose(dz, cos, s_next, s_prev):
    return dz * cos + pltpu.roll(dz * s_next, 32, 1) + pltpu.roll(dz * s_prev, HEAD_DIM - 32, 1)


def proj_forward(xa, mod6, g, w_in, qg, kg, tabs, offs, nctx, name, hosted=None):
    Tr, D = xa.shape
    P = w_in.shape[1]
    tm = ROW_TILE
    qo, ko = offs["q"], offs["k"]
    qw, kw = N_Q_HEADS * HEAD_DIM, N_KV_HEADS * HEAD_DIM
    scale_q = HEAD_DIM ** -0.5 * LOG2E

    def body(x_ref, mod_ref, g_ref, w_ref, qg_ref, kg_ref, tab_ref, hx_ref, pr_ref, q_ref, k_ref):
        x = x_ref[...]
        ms = mod_ref[0]
        shift, scale = ms[0:1], ms[1:2]
        r = lax.rsqrt(jnp.mean(x * x, axis=-1, keepdims=True) + EPS)
        hb = (((x * r) * g_ref[...]) * (1.0 + scale) + shift).astype(BF)
        hx_ref[...] = hb
        pr = jnp.dot(hb, w_ref[...], preferred_element_type=F32)
        pr_ref[...] = pr.astype(BF)
        cos, s_next, s_prev = tab_ref[0], tab_ref[1], tab_ref[2]

        def head(v, gain):
            n = v * lax.rsqrt(jnp.mean(v * v, axis=-1, keepdims=True) + EPS)
            return _rope_apply(n * gain, cos, s_next, s_prev)

        for h in range(N_Q_HEADS):
            lo = qo + h * HEAD_DIM
            q_ref[:, h * HEAD_DIM:(h + 1) * HEAD_DIM] = (head(pr[:, lo:lo + HEAD_DIM], qg_ref[...]) * scale_q).astype(BF)
        for h in range(N_KV_HEADS):
            lo = ko + h * HEAD_DIM
            k_ref[:, h * HEAD_DIM:(h + 1) * HEAD_DIM] = head(pr[:, lo:lo + HEAD_DIM], kg_ref[...]).astype(BF)

    return _call(
        body, name=name, grid=(Tr // tm,),
        in_specs=[_row(tm, D), _mod_spec(D, 1, nctx), _resident((1, D)), _resident(w_in.shape),
                  _resident((1, HEAD_DIM)), _resident((1, HEAD_DIM)),
                  pl.BlockSpec((3, tm, HEAD_DIM), lambda i: (0, i, 0))],
        out_specs=[_row(tm, D), _row(tm, P), _row(tm, qw), _row(tm, kw)],
        out_shape=[jax.ShapeDtypeStruct((Tr, D), BF), jax.ShapeDtypeStruct((Tr, P), BF),
                   jax.ShapeDtypeStruct((Tr, qw), BF), jax.ShapeDtypeStruct((Tr, kw), BF)],
        operands=[xa, mod6, g, w_in, qg, kg, tabs], hosted=hosted,
        params=_params(vmem=VMEM_BIG, sem=("arbitrary",)))


def _shifted(u, first_row, last_row):
    T = u.shape[0]
    prev = jnp.where(first_row, 0.0, pltpu.roll(u, 1, 0))
    nxt = jnp.where(last_row, 0.0, pltpu.roll(u, T - 1, 0))
    return prev, nxt


def conv_forward(proj, conv_w, offs, Tc, name):
    Ta = proj.shape[0]
    T = Ta - Tc
    Dc = conv_w.shape[1]
    cb = offs["cv"] // 384

    def body(p_ref, w_ref, y_ref):
        rows = lax.broadcasted_iota(jnp.int32, (T, 128), 0)
        u = p_ref[pl.ds(Tc, T), 128:256].astype(F32) * p_ref[pl.ds(Tc, T), 256:384].astype(F32)
        prev, nxt = _shifted(u, rows == 0, rows == T - 1)
        w = w_ref[...]
        cv = prev * w[0:1] + u * w[1:2] + nxt * w[2:3]
        y_ref[...] = (p_ref[pl.ds(Tc, T), 0:128].astype(F32) * cv).astype(BF)

    return pl.pallas_call(
        body, name=name, grid=(Dc // 128,),
        in_specs=[pl.BlockSpec((Ta, 384), lambda j: (0, cb + j)), pl.BlockSpec((CONV_TAPS, 128), lambda j: (0, j))],
        out_specs=pl.BlockSpec((T, 128), lambda j: (0, j)),
        out_shape=jax.ShapeDtypeStruct((T, Dc), BF),
        compiler_params=_params(vmem=VMEM_BIG, sem=("arbitrary",)),
    )(proj, conv_w)


def conv_backward(dproj, dy, proj, conv_w, offs, Tc, name):
    Ta = proj.shape[0]
    T = Ta - Tc
    Dc = conv_w.shape[1]
    cb = offs["cv"] // 384

    def body(dp_any, dy_ref, p_ref, w_ref, o_ref, dw_ref):
        rows = lax.broadcasted_iota(jnp.int32, (T, 128), 0)
        first, last = rows == 0, rows == T - 1
        bg = p_ref[pl.ds(Tc, T), 0:128].astype(F32)
        cg = p_ref[pl.ds(Tc, T), 128:256].astype(F32)
        vc = p_ref[pl.ds(Tc, T), 256:384].astype(F32)
        dy = dy_ref[...].astype(F32)
        u = cg * vc
        prev, nxt = _shifted(u, first, last)
        w = w_ref[...]
        cv = prev * w[0:1] + u * w[1:2] + nxt * w[2:3]
        o_ref[pl.ds(0, Tc), :] = jnp.zeros((Tc, 384), BF)
        o_ref[pl.ds(Tc, T), 0:128] = (dy * cv).astype(BF)
        dcv = dy * bg
        dprev, dnxt = _shifted(dcv, first, last)
        du = dnxt * w[0:1] + dcv * w[1:2] + dprev * w[2:3]
        o_ref[pl.ds(Tc, T), 128:256] = (du * vc).astype(BF)
        o_ref[pl.ds(Tc, T), 256:384] = (du * cg).astype(BF)
        dw_ref[...] = jnp.zeros_like(dw_ref)
        for k, tap in enumerate((prev, u, nxt)):
            dw_ref[k:k + 1, :] = jnp.sum(dcv * tap, axis=0, keepdims=True)

    blk = pl.BlockSpec((Ta, 384), lambda j: (0, cb + j))
    return pl.pallas_call(
        body, name=name, grid=(Dc // 128,),
        in_specs=[ANY, pl.BlockSpec((T, 128), lambda j: (0, j)), blk, pl.BlockSpec((CONV_TAPS, 128), lambda j: (0, j))],
        out_specs=[blk, pl.BlockSpec((8, 128), lambda j: (0, j))],
        out_shape=[jax.ShapeDtypeStruct(dproj.shape, BF), jax.ShapeDtypeStruct((8, Dc), F32)],
        input_output_aliases={0: 0},
        compiler_params=_params(vmem=VMEM_BIG, sem=("arbitrary",)),
    )(dproj, dy, proj, conv_w)


def _kv_chunk(Ta):
    return _pick(Ta, (768, 512, 384, 256, 128))


def _stack_heads(v):
    return jnp.concatenate([v[:, h * HEAD_DIM:(h + 1) * HEAD_DIM] for h in range(GROUP)], axis=0)


def attention_forward(q, k, proj, offs, Tc, name, hosted=None):
    Ta = k.shape[0]
    T = Ta - Tc
    tq = ROW_TILE
    kc = _kv_chunk(Ta)
    nkv = Ta // kc
    gw = GROUP * HEAD_DIM
    vblk = offs["v"] // HEAD_DIM
    qoff = Tc // tq
    n = GROUP * tq

    def body(q_ref, k_ref, v_ref, o_ref, lse_ref, vx_ref, qs_ref, s0_ref, s1_ref, m_ref, acc_ref):
        @pl.when(pl.program_id(1) == 0)
        def _():
            vx_ref[:, 0:HEAD_DIM] = v_ref[...]
            vx_ref[:, HEAD_DIM:2 * HEAD_DIM] = jnp.ones((Ta, HEAD_DIM), BF)

        qs_ref[...] = _stack_heads(q_ref[...])
        m_ref[...] = jnp.full((n, 1), -1e30, F32)
        acc_ref[...] = jnp.zeros((n, 2 * HEAD_DIM), F32)

        def rows(c):
            return pl.ds(pl.multiple_of(c * kc, kc), kc)

        def logits(c, dst):
            dst[...] = lax.dot_general(qs_ref[...], k_ref[rows(c), :], NT, preferred_element_type=F32)

        def consume(src, c):
            s = src[...]
            m_prev = m_ref[...]
            m_new = jnp.maximum(m_prev, jnp.max(s, axis=-1, keepdims=True))
            p = jnp.exp2(s - m_new).astype(BF)
            acc_ref[...] = jnp.exp2(m_prev - m_new) * acc_ref[...] + jnp.dot(p, vx_ref[rows(c), :], preferred_element_type=F32)
            m_ref[...] = m_new

        def pair(i, carry):
            logits(2 * i + 1, s1_ref)
            consume(s0_ref, 2 * i)
            logits(2 * i + 2, s0_ref)
            consume(s1_ref, 2 * i + 1)
            return carry

        logits(0, s0_ref)
        if nkv % 2:
            lax.fori_loop(0, nkv // 2, pair, 0)
            consume(s0_ref, nkv - 1)
        else:
            lax.fori_loop(0, nkv // 2 - 1, pair, 0)
            logits(nkv - 1, s1_ref)
            consume(s0_ref, nkv - 2)
            consume(s1_ref, nkv - 1)
        acc = acc_ref[...]
        l = acc[:, HEAD_DIM:HEAD_DIM + 1]
        o = acc[:, 0:HEAD_DIM] / l
        lse = m_ref[...] + jnp.log2(l)
        for h in range(GROUP):
            o_ref[:, h * HEAD_DIM:(h + 1) * HEAD_DIM] = o[h * tq:(h + 1) * tq].astype(BF)
            lse_ref[0, :, h:h + 1] = lse[h * tq:(h + 1) * tq]

    return _call(
        body, name=name, grid=(N_KV_HEADS, T // tq),
        in_specs=[pl.BlockSpec((tq, gw), lambda j, i: (i + qoff, j)),
                  pl.BlockSpec((Ta, HEAD_DIM), lambda j, i: (0, j)),
                  pl.BlockSpec((Ta, HEAD_DIM), lambda j, i: (0, vblk + j))],
        out_specs=[pl.BlockSpec((tq, gw), lambda j, i: (i, j)),
                   pl.BlockSpec((1, tq, GROUP), lambda j, i: (j, i, 0))],
        out_shape=[jax.ShapeDtypeStruct((T, N_Q_HEADS * HEAD_DIM), BF),
                   jax.ShapeDtypeStruct((N_KV_HEADS, T, GROUP), F32)],
        scratch_shapes=[pltpu.VMEM((Ta, 2 * HEAD_DIM), BF), pltpu.VMEM((n, HEAD_DIM), BF), pltpu.VMEM((n, kc), F32),
                        pltpu.VMEM((n, kc), F32), pltpu.VMEM((n, 1), F32), pltpu.VMEM((n, 2 * HEAD_DIM), F32)],
        operands=[q, k, proj], hosted=hosted,
        params=_params(vmem=VMEM_BIG, sem=("arbitrary", "arbitrary")))


def _norm_rope_backward(dz, raw, gg, cos, s_next, s_prev):
    r = lax.rsqrt(jnp.mean(raw * raw, axis=-1, keepdims=True) + EPS)
    n = raw * r
    dy = _rope_transpose(dz, cos, s_next, s_prev)
    dn = dy * gg
    return r * (dn - n * jnp.mean(dn * n, axis=-1, keepdims=True)), jnp.sum(dy * n, axis=0, keepdims=True)


def attention_backward(dproj, q, k, proj, o, lse, do, qgain, tabs, offs, Tc, name, hosted=None):
    Ta = k.shape[0]
    tq = ROW_TILE
    nctx = Tc // tq
    kc = _kv_chunk(Ta)
    gw = GROUP * HEAD_DIM
    vblk = offs["v"] // HEAD_DIM
    qblk = offs["q"] // gw
    zscale = HEAD_DIM ** -0.5

    def body(dp_any, q_ref, k_ref, v_ref, o_ref, lse_ref, do_ref, raw_ref, g_ref, tab_ref, dqr_ref, dk_ref, dv_ref, dg_ref):
        j, i = pl.program_id(0), pl.program_id(1)

        @pl.when(i == 0)
        def _():
            dk_ref[...] = jnp.zeros_like(dk_ref)
            dv_ref[...] = jnp.zeros_like(dv_ref)

        @pl.when((i == 0) & (j == 0))
        def _():
            dg_ref[...] = jnp.zeros_like(dg_ref)

        @pl.when(i < nctx)
        def _():
            dqr_ref[...] = jnp.zeros_like(dqr_ref)

        @pl.when(i >= nctx)
        def _():
            qs = _stack_heads(q_ref[...])
            dob = do_ref[...]
            dos = _stack_heads(dob)
            delta = jnp.concatenate(
                [jnp.sum(dob[:, h * HEAD_DIM:(h + 1) * HEAD_DIM].astype(F32)
                         * o_ref[:, h * HEAD_DIM:(h + 1) * HEAD_DIM].astype(F32), axis=-1, keepdims=True)
                 for h in range(GROUP)], axis=0)
            lse = jnp.concatenate([lse_ref[0, :, h:h + 1] for h in range(GROUP)], axis=0)

            def step(c, dq):
                rows = pl.ds(pl.multiple_of(c * kc, kc), kc)
                kk = k_ref[rows, :]
                vv = v_ref[rows, :]
                s = lax.dot_general(qs, kk, NT, preferred_element_type=F32)
                p = jnp.exp2(s - lse)
                dp = lax.dot_general(dos, vv, NT, preferred_element_type=F32)
                ds = (p * (dp - delta)).astype(BF)
                dv_ref[rows, :] += lax.dot_general(p.astype(BF), dos, TN, preferred_element_type=F32)
                dk_ref[rows, :] += lax.dot_general(ds, qs, TN, preferred_element_type=F32)
                return dq + jnp.dot(ds, kk, preferred_element_type=F32)

            dq = lax.fori_loop(0, Ta // kc, step, jnp.zeros((GROUP * tq, HEAD_DIM), F32))
            per_head = lambda t: jnp.concatenate([t] * GROUP, axis=0)
            dr, dg = _norm_rope_backward(dq * zscale, _stack_heads(raw_ref[...]).astype(F32), g_ref[...],
                                         per_head(tab_ref[0]), per_head(tab_ref[1]), per_head(tab_ref[2]))
            for h in range(GROUP):
                dqr_ref[:, h * HEAD_DIM:(h + 1) * HEAD_DIM] = dr[h * tq:(h + 1) * tq].astype(BF)
            dg_ref[0:1, :] += dg

    lat = lambda j, i: (jnp.maximum(i - nctx, 0), j)
    (dproj, dk, dv, dqg), exchanged = _call(
        body, name=name, grid=(N_KV_HEADS, Ta // tq),
        in_specs=[ANY, pl.BlockSpec((tq, gw), lambda j, i: (i, j)),
                  pl.BlockSpec((Ta, HEAD_DIM), lambda j, i: (0, j)),
                  pl.BlockSpec((Ta, HEAD_DIM), lambda j, i: (0, vblk + j)),
                  pl.BlockSpec((tq, gw), lat),
                  pl.BlockSpec((1, tq, GROUP), lambda j, i: (j, jnp.maximum(i - nctx, 0), 0)),
                  pl.BlockSpec((tq, gw), lat),
                  pl.BlockSpec((tq, gw), lambda j, i: (i, qblk + j)),
                  pl.BlockSpec((1, HEAD_DIM), lambda j, i: (0, 0)),
                  pl.BlockSpec((3, tq, HEAD_DIM), lambda j, i: (0, i, 0))],
        out_specs=[pl.BlockSpec((tq, gw), lambda j, i: (i, qblk + j)),
                   pl.BlockSpec((Ta, HEAD_DIM), lambda j, i: (0, j)),
                   pl.BlockSpec((Ta, HEAD_DIM), lambda j, i: (0, j)),
                   pl.BlockSpec((8, HEAD_DIM), lambda j, i: (0, 0))],
        out_shape=[jax.ShapeDtypeStruct(dproj.shape, BF),
                   jax.ShapeDtypeStruct((Ta, N_KV_HEADS * HEAD_DIM), F32),
                   jax.ShapeDtypeStruct((Ta, N_KV_HEADS * HEAD_DIM), F32),
                   jax.ShapeDtypeStruct((8, HEAD_DIM), F32)],
        operands=[dproj, q, k, proj, o, lse, do, proj, qgain, tabs], hosted=hosted, aliases={0: 0},
        params=_params(vmem=VMEM_BIG, sem=("arbitrary", "arbitrary")))
    return (dproj, dk, dv, dqg), exchanged


def kv_backward(dproj, dk, dv, proj, gain, tabs, offs, name):
    Ta = proj.shape[0]
    tm = ROW_TILE
    kw = N_KV_HEADS * HEAD_DIM
    cb = offs["k"] // (2 * kw)
    kb = offs["k"] // kw
    zscale = 1.0 / LOG2E

    def body(dp_any, dk_ref, dv_ref, raw_ref, g_ref, tab_ref, o_ref, dg_ref):
        @pl.when(pl.program_id(0) == 0)
        def _():
            dg_ref[...] = jnp.zeros_like(dg_ref)

        cos, s_next, s_prev = tab_ref[0], tab_ref[1], tab_ref[2]
        dg = jnp.zeros((1, HEAD_DIM), F32)
        for h in range(N_KV_HEADS):
            sl = slice(h * HEAD_DIM, (h + 1) * HEAD_DIM)
            dr, dgh = _norm_rope_backward(dk_ref[:, sl] * zscale, raw_ref[:, sl].astype(F32), g_ref[...], cos, s_next, s_prev)
            o_ref[:, sl] = dr.astype(BF)
            dg = dg + dgh
        o_ref[:, kw:2 * kw] = dv_ref[...].astype(BF)
        dg_ref[0:1, :] += dg

    return pl.pallas_call(
        body, name=name, grid=(Ta // tm,),
        in_specs=[ANY, _row(tm, kw), _row(tm, kw), pl.BlockSpec((tm, kw), lambda i: (i, kb)),
                  _resident((1, HEAD_DIM)), pl.BlockSpec((3, tm, HEAD_DIM), lambda i: (0, i, 0))],
        out_specs=[pl.BlockSpec((tm, 2 * kw), lambda i: (i, cb)), pl.BlockSpec((8, HEAD_DIM), lambda i: (0, 0))],
        out_shape=[jax.ShapeDtypeStruct(dproj.shape, BF), jax.ShapeDtypeStruct((8, HEAD_DIM), F32)],
        input_output_aliases={0: 0},
        compiler_params=_params(sem=("arbitrary",)),
    )(dproj, dk, dv, proj, gain, tabs)


def merge_forward(x1, mod6, yc, o, proj, w_bc, w_ba, w_o, offs, Tc, name):
    T, D = yc.shape[0], x1.shape[1]
    tm = ROW_TILE
    roff = Tc // tm
    gb = offs["gt"] // (2 * D)

    def body(x_ref, mod_ref, yc_ref, o_ref, gt_ref, wbc_ref, wba_ref, wo_ref, xo_ref, pc_ref, pa_ref, m_ref, z_ref):
        gate = mod_ref[0][2:3]
        pc = jnp.dot(yc_ref[...], wbc_ref[...], preferred_element_type=F32)
        pa = jnp.dot(o_ref[...], wba_ref[...], preferred_element_type=F32)
        pc_ref[...] = pc.astype(BF)
        pa_ref[...] = pa.astype(BF)
        mb = (_sigmoid(gt_ref[:, 0:D].astype(F32)) * pc + _sigmoid(gt_ref[:, D:2 * D].astype(F32)) * pa).astype(BF)
        m_ref[...] = mb
        z = jnp.dot(mb, wo_ref[...], preferred_element_type=F32)
        z_ref[...] = z.astype(BF)
        xo_ref[...] = x_ref[...] + gate * z

    return pl.pallas_call(
        body, name=name, grid=(T // tm,),
        in_specs=[pl.BlockSpec((tm, D), lambda i: (i + roff, 0)), _mod_spec(D, 1, 0), _row(tm, yc.shape[1]), _row(tm, o.shape[1]),
                  pl.BlockSpec((tm, 2 * D), lambda i: (i + roff, gb)),
                  _resident(w_bc.shape), _resident(w_ba.shape), _resident(w_o.shape)],
        out_specs=[_row(tm, D)] * 5,
        out_shape=[jax.ShapeDtypeStruct((T, D), F32)] + [jax.ShapeDtypeStruct((T, D), BF)] * 4,
        compiler_params=_params(vmem=VMEM_BIG, sem=("arbitrary",)),
    )(x1, mod6, yc, o, proj, w_bc, w_ba, w_o)


def merge_backward_rows(dx2, mod6, z, pc, pa, proj, w_bc, w_ba, w_o, offs, Tc, name):
    T, D = dx2.shape
    Ta, P = proj.shape
    tm = ROW_TILE
    nctx = Tc // tm
    gb = offs["gt"] // (2 * D)
    dcw, dqw = w_bc.shape[0], w_ba.shape[0]

    def body(dx_ref, mod_ref, z_ref, pc_ref, pa_ref, gt_ref, wbc_ref, wba_ref, wo_ref,
             dgt_ref, dg_ref, dpc_ref, dpa_ref, dyc_ref, do_ref, acc_ref):
        i = pl.program_id(0)

        @pl.when(i == 0)
        def _():
            acc_ref[...] = jnp.zeros_like(acc_ref)

        @pl.when(i < nctx)
        def _():
            dgt_ref[...] = jnp.zeros_like(dgt_ref)

        @pl.when(i >= nctx)
        def _():
            gate = mod_ref[0][2:3]
            dx = dx_ref[...]
            acc_ref[0:1, :] += jnp.sum(dx * z_ref[...].astype(F32), axis=0, keepdims=True)
            dgb = (dx * gate).astype(BF)
            dg_ref[...] = dgb
            dm = lax.dot_general(dgb, wo_ref[...], NT, preferred_element_type=F32)
            sc = _sigmoid(gt_ref[:, 0:D].astype(F32))
            sa = _sigmoid(gt_ref[:, D:2 * D].astype(F32))
            pc = pc_ref[...].astype(F32)
            pa = pa_ref[...].astype(F32)
            dpc = (dm * sc).astype(BF)
            dpa = (dm * sa).astype(BF)
            dpc_ref[...] = dpc
            dpa_ref[...] = dpa
            dgt_ref[:, 0:D] = ((dm * pc) * (sc * (1.0 - sc))).astype(BF)
            dgt_ref[:, D:2 * D] = ((dm * pa) * (sa * (1.0 - sa))).astype(BF)
            dyc_ref[...] = lax.dot_general(dpc, wbc_ref[...], NT, preferred_element_type=F32).astype(BF)
            do_ref[...] = lax.dot_general(dpa, wba_ref[...], NT, preferred_element_type=F32).astype(BF)

    lat = lambda n: pl.BlockSpec((tm, n), lambda i: (jnp.maximum(i - nctx, 0), 0))
    return pl.pallas_call(
        body, name=name, grid=(Ta // tm,),
        in_specs=[lat(D), _mod_spec(D, 1, 0), lat(D), lat(D), lat(D),
                  pl.BlockSpec((tm, 2 * D), lambda i: (i, gb)),
                  _resident(w_bc.shape), _resident(w_ba.shape), _resident(w_o.shape)],
        out_specs=[pl.BlockSpec((tm, 2 * D), lambda i: (i, gb)), lat(D), lat(D), lat(D), lat(dcw), lat(dqw),
                   pl.BlockSpec((8, D), lambda i: (0, 0))],
        out_shape=[jax.ShapeDtypeStruct((Ta, P), BF)] + [jax.ShapeDtypeStruct((T, D), BF)] * 3
                  + [jax.ShapeDtypeStruct((T, dcw), BF), jax.ShapeDtypeStruct((T, dqw), BF), jax.ShapeDtypeStruct((8, D), F32)],
        compiler_params=_params(vmem=VMEM_BIG, sem=("arbitrary",)),
    )(dx2, mod6, z, pc, pa, proj, w_bc, w_ba, w_o)


def proj_backward_rows(dproj, dres, xa, mod6, g, w_in, nctx, name, hosted=None):
    Tr, D = xa.shape
    P = w_in.shape[1]
    tm = ROW_TILE

    def body(dp_ref, dres_ref, x_ref, mod_ref, g_ref, w_ref, dx_ref, acc_ref):
        i = pl.program_id(0)

        @pl.when((i == 0) | (i == nctx))
        def _():
            acc_ref[...] = jnp.zeros_like(acc_ref)

        x = x_ref[...]
        scale = mod_ref[0][1:2]
        gg = g_ref[...]
        dhm = lax.dot_general(dp_ref[...], w_ref[...], NT, preferred_element_type=F32)
        r = lax.rsqrt(jnp.mean(x * x, axis=-1, keepdims=True) + EPS)
        xh = x * r
        dshift = jnp.sum(dhm, axis=0, keepdims=True)
        dscale = jnp.sum(dhm * (xh * gg), axis=0, keepdims=True)
        dxh_g = dhm * (1.0 + scale)
        dg = jnp.sum(dxh_g * xh, axis=0, keepdims=True)
        dxh = dxh_g * gg
        res = jnp.where(i < nctx, 0.0, dres_ref[...])
        dx_ref[...] = res + r * (dxh - xh * jnp.mean(dxh * xh, axis=-1, keepdims=True))
        for k, val in enumerate((dshift, dscale, dg)):
            acc_ref[0, k:k + 1, :] += val

    return _call(
        body, name=name, grid=(Tr // tm,),
        in_specs=[_row(tm, P), pl.BlockSpec((tm, D), lambda i: (jnp.maximum(i - nctx, 0), 0)), _row(tm, D),
                  _mod_spec(D, 1, nctx), _resident((1, D)), _resident(w_in.shape)],
        out_specs=[_row(tm, D), _acc_spec(D, nctx)],
        out_shape=[jax.ShapeDtypeStruct((Tr, D), F32), jax.ShapeDtypeStruct((2, 8, D), F32)],
        operands=[dproj, dres, xa, mod6, g, w_in], hosted=hosted,
        params=_params(vmem=VMEM_BIG, sem=("arbitrary",)))


def _adam_update(w, g, m, v):
    c1 = 1.0 - ADAM_B1 ** ADAM_STEP
    c2 = 1.0 - ADAM_B2 ** ADAM_STEP
    m = ADAM_B1 * m + (1.0 - ADAM_B1) * g
    v = ADAM_B2 * v + (1.0 - ADAM_B2) * (g * g)
    return -ADAM_LR * ((m / c1) / (jnp.sqrt(v / c2) + ADAM_EPS) + ADAM_WD * w), m, v


def adamw(w, g, m, v, name):
    R, C = w.shape
    tr = _row_tile(R, C)

    def body(w_ref, g_ref, m_ref, v_ref, d_ref, nm_ref, nv_ref):
        d_ref[...], nm_ref[...], nv_ref[...] = _adam_update(w_ref[...], g_ref[...], m_ref[...], v_ref[...])

    blk = pl.BlockSpec((tr, C), lambda i: (i, 0))
    return pl.pallas_call(
        body, name=name, grid=(R // tr,),
        in_specs=[blk] * 4, out_specs=[blk] * 3,
        out_shape=[jax.ShapeDtypeStruct((R, C), F32)] * 3,
        compiler_params=_params(vmem=VMEM_BIG, sem=("parallel",)),
    )(w, g, m, v)


def adamw_summed(recv, w, m, v, name):
    R, C = w.shape
    tr = _row_tile(R, C)

    def body(r_ref, w_ref, m_ref, v_ref, g_ref, d_ref, nm_ref, nv_ref):
        g = r_ref[0].astype(F32)
        for a in range(1, N_DEV):
            g = g + r_ref[a].astype(F32)
        g_ref[...] = g
        d_ref[...], nm_ref[...], nv_ref[...] = _adam_update(w_ref[...], g, m_ref[...], v_ref[...])

    blk = pl.BlockSpec((tr, C), lambda i: (i, 0))
    return pl.pallas_call(
        body, name=name, grid=(R // tr,),
        in_specs=[pl.BlockSpec((N_DEV, tr, C), lambda i: (0, i, 0)), blk, blk, blk], out_specs=[blk] * 4,
        out_shape=[jax.ShapeDtypeStruct((R, C), F32)] * 4,
        compiler_params=_params(vmem=VMEM_BIG, sem=("parallel",)),
    )(recv, w, m, v)


def adamw_reduced(place, ps, recv_b, w, m, v, name):
    R, C = w.shape
    tr = _row_tile(R, C)

    def body(place_ref, p_ref, b_ref, w_ref, m_ref, v_ref, g_ref, d_ref, nm_ref, nv_ref):
        g = p_ref[0].astype(F32)
        for j in range(3):
            g = g + b_ref[j].astype(F32)
        g_ref[...] = g
        d_ref[...], nm_ref[...], nv_ref[...] = _adam_update(w_ref[...], g, m_ref[...], v_ref[...])

    blk = pl.BlockSpec((tr, C), lambda r, pr: (r, 0))
    return pl.pallas_call(
        body, name=name,
        grid_spec=pltpu.PrefetchScalarGridSpec(
            num_scalar_prefetch=1, grid=(R // tr,),
            in_specs=[pl.BlockSpec((1, tr, C), lambda r, pr: (pr[1], r, 0)),
                      pl.BlockSpec((3, tr, C), lambda r, pr: (0, r, 0)), blk, blk, blk],
            out_specs=[blk] * 4),
        out_shape=[jax.ShapeDtypeStruct((R, C), F32)] * 4,
        compiler_params=_params(vmem=VMEM_BIG),
    )(place, ps, recv_b, w, m, v)


def _rope_tables(T, Tc):
    rows = T // GRID_W
    n_freq = HEAD_DIM // 4
    inv = ROPE_THETA ** (-jnp.arange(n_freq, dtype=F32) / n_freq)
    ang_r = jnp.arange(rows).astype(F32)[:, None] * inv
    ang_c = jnp.arange(GRID_W).astype(F32)[:, None] * inv
    per_row = lambda a: jnp.broadcast_to(a[:, None, :], (rows, GRID_W, n_freq)).reshape(T, n_freq)
    per_col = lambda a: jnp.broadcast_to(a[None, :, :], (rows, GRID_W, n_freq)).reshape(T, n_freq)
    cr, sr = per_row(jnp.cos(ang_r)), per_row(jnp.sin(ang_r))
    cc, sc = per_col(jnp.cos(ang_c)), per_col(jnp.sin(ang_c))
    zero = jnp.zeros_like(sr)
    cos = jnp.concatenate([cr, cr, cc, cc], axis=1)
    s_next = jnp.concatenate([-sr, zero, -sc, zero], axis=1)
    s_prev = jnp.concatenate([zero, sr, zero, sc], axis=1)
    lat = jnp.stack([cos, s_next, s_prev])
    ctx = jnp.stack([jnp.ones((Tc, HEAD_DIM), F32), jnp.zeros((Tc, HEAD_DIM), F32), jnp.zeros((Tc, HEAD_DIM), F32)])
    return jnp.concatenate([ctx, lat], axis=1)


BIG = ("ffn1_w_in", "ffn1_w_out", "w_in", "w_branch_conv", "w_branch_attn", "w_out", "ffn2_w_in", "ffn2_w_out")


def _regroup_w_in(stacked, D, Dc, qw, kw):
    w = stacked.transpose(1, 0, 2).reshape(D, -1)
    o = 0
    parts = {}
    for nme, wd in (("bg", Dc), ("cg", Dc), ("vc", Dc), ("q", qw), ("k", kw), ("v", kw), ("gt", 2 * D)):
        parts[nme] = w[:, o:o + wd]
        o += wd
    nb = Dc // 128
    cv = jnp.stack([parts[n].reshape(D, nb, 128) for n in ("bg", "cg", "vc")], axis=2).reshape(D, 3 * Dc)
    return jnp.concatenate([cv, parts["q"], parts["gt"], parts["k"], parts["v"]], axis=1)


def _ungroup_w_in_grad(gt_, D, Dc, qw, kw):
    nb = Dc // 128
    cv = gt_[:3 * Dc].reshape(nb, 3, 128, D)
    o = 3 * Dc
    q = gt_[o:o + qw]
    gt = gt_[o + qw:o + qw + 2 * D]
    k = gt_[o + qw + 2 * D:o + qw + 2 * D + kw]
    v = gt_[o + qw + 2 * D + kw:]
    nat = jnp.concatenate([cv[:, 0].reshape(Dc, D), cv[:, 1].reshape(Dc, D), cv[:, 2].reshape(Dc, D), q, k, v, gt], axis=0)
    return nat.reshape(N_DEV, -1, D)


def kernel(x, c, ctx, c_ctx, w_mod, b_mod, norm1_g, norm2_g, norm3_g, ffn1_w_in, ffn1_w_out, w_in, conv_w, q_norm_g, k_norm_g, w_branch_conv, w_branch_attn, w_out, ffn2_w_in, ffn2_w_out, final_g, loss_target, m_c_ctx, m_w_mod, m_b_mod, m_norm1_g, m_norm2_g, m_norm3_g, m_ffn1_w_in, m_ffn1_w_out, m_w_in, m_conv_w, m_q_norm_g, m_k_norm_g, m_w_branch_conv, m_w_branch_attn, m_w_out, m_ffn2_w_in, m_ffn2_w_out, m_final_g, v_c_ctx, v_w_mod, v_b_mod, v_norm1_g, v_norm2_g, v_norm3_g, v_ffn1_w_in, v_ffn1_w_out, v_w_in, v_conv_w, v_q_norm_g, v_k_norm_g, v_w_branch_conv, v_w_branch_attn, v_w_out, v_ffn2_w_in, v_ffn2_w_out, v_final_g):
    weights = dict(c_ctx=c_ctx, w_mod=w_mod, b_mod=b_mod, norm1_g=norm1_g, norm2_g=norm2_g, norm3_g=norm3_g,
                   ffn1_w_in=ffn1_w_in, ffn1_w_out=ffn1_w_out, w_in=w_in, conv_w=conv_w, q_norm_g=q_norm_g,
                   k_norm_g=k_norm_g, w_branch_conv=w_branch_conv, w_branch_attn=w_branch_attn, w_out=w_out,
                   ffn2_w_in=ffn2_w_in, ffn2_w_out=ffn2_w_out, final_g=final_g)
    moms = dict(c_ctx=(m_c_ctx, v_c_ctx), w_mod=(m_w_mod, v_w_mod), b_mod=(m_b_mod, v_b_mod),
                norm1_g=(m_norm1_g, v_norm1_g), norm2_g=(m_norm2_g, v_norm2_g), norm3_g=(m_norm3_g, v_norm3_g),
                ffn1_w_in=(m_ffn1_w_in, v_ffn1_w_in), ffn1_w_out=(m_ffn1_w_out, v_ffn1_w_out), w_in=(m_w_in, v_w_in),
                conv_w=(m_conv_w, v_conv_w), q_norm_g=(m_q_norm_g, v_q_norm_g), k_norm_g=(m_k_norm_g, v_k_norm_g),
                w_branch_conv=(m_w_branch_conv, v_w_branch_conv), w_branch_attn=(m_w_branch_attn, v_w_branch_attn),
                w_out=(m_w_out, v_w_out), ffn2_w_in=(m_ffn2_w_in, v_ffn2_w_in), ffn2_w_out=(m_ffn2_w_out, v_ffn2_w_out),
                final_g=(m_final_g, v_final_g))
    order = list(weights)

    T, D = x.shape[1], x.shape[2]
    Tc = ctx.shape[1]
    nctx = Tc // ROW_TILE
    nd = N_MOD * D
    Dc = conv_w.shape[2] * N_DEV
    qw, kw = N_Q_HEADS * HEAD_DIM, N_KV_HEADS * HEAD_DIM
    offs, o = {}, 0
    for nme, wd in (("cv", 3 * Dc), ("q", qw), ("gt", 2 * D), ("k", kw), ("v", kw)):
        offs[nme] = o
        o += wd

    ax, ay, ac = lax.axis_index("x"), lax.axis_index("y"), lax.axis_index("c")
    me = 4 * ax + 2 * ay + ac
    place = jnp.stack([ac, 2 * ax + ay]).astype(jnp.int32)

    shard = {n: (jnp.swapaxes(weights[n][0], 0, 1) if n in ("ffn1_w_in", "ffn2_w_in") else weights[n][0]).astype(BF)
             for n in BIG}
    rows2d = lambda a: a.reshape(-1, a.shape[-1])
    full = {}
    g_ffn1_in, g_ffn1_out = allgather_two_level([shard["ffn1_w_in"], shard["ffn1_w_out"]], "ag_ffn1")
    full["ffn1_w_in"], full["ffn1_w_out"] = rows2d(g_ffn1_in), rows2d(g_ffn1_out)

    mod_cols = w_mod.shape[2]
    cw_loc = conv_w[0]
    cpad = (-(D + CONV_TAPS * cw_loc.shape[1])) % 128
    pay = jnp.concatenate([c.reshape(1, D), cw_loc.reshape(1, -1), jnp.zeros((1, cpad), F32)], axis=1)
    call = allgather_direct(pay, "ag_cond")
    conv_full = call[:, 0, D:D + CONV_TAPS * cw_loc.shape[1]].reshape(N_DEV, CONV_TAPS, -1).transpose(1, 0, 2).reshape(CONV_TAPS, Dc)
    b_loc = lax.dynamic_slice_in_dim(b_mod, me * mod_cols, mod_cols, axis=1)
    cctx2 = c_ctx.reshape(1, D)
    mod_part = mod_forward(call, cctx2, w_mod[0], b_loc, "mod_fwd")
    mod_all = allgather_direct(mod_part, "ag_mod")
    mod_lat = lax.dynamic_index_in_dim(mod_all, me, axis=1, keepdims=False).reshape(nd)
    mod_ctx = mod_all[:, N_DEV, :].reshape(nd)
    mod6 = jnp.stack([mod_ctx, mod_lat]).reshape(6, 3, D)

    tabs = _rope_tables(T, Tc)

    srcs1 = (ctx[0], x[0])
    (xa1, hm1, ab1, h1, f1), (g_w_in,) = ffn_forward(
        srcs1, mod6, 0, norm1_g, full["ffn1_w_in"], full["ffn1_w_out"], nctx, "ffn1_fwd",
        hosted=Hosted(gathers=[shard["w_in"]]))
    full["w_in"] = _regroup_w_in(g_w_in, D, Dc, qw, kw)
    merge_names = ("w_branch_conv", "w_branch_attn", "w_out")
    (hx, proj, qr, kr), g_merge = proj_forward(
        xa1, mod6, norm2_g, full["w_in"], q_norm_g, k_norm_g, tabs, offs, nctx, "proj_fwd",
        hosted=Hosted(gathers=[shard[n] for n in merge_names]))
    full.update({n: rows2d(g) for n, g in zip(merge_names, g_merge)})
    yc = conv_forward(proj, conv_full, offs, Tc, "conv_fwd")
    (oa, lse), (g_ffn2_in, g_ffn2_out) = attention_forward(
        qr, kr, proj, offs, Tc, "attn_fwd", hosted=Hosted(gathers=[shard["ffn2_w_in"], shard["ffn2_w_out"]]))
    full["ffn2_w_in"], full["ffn2_w_out"] = rows2d(g_ffn2_in), rows2d(g_ffn2_out)
    x2, pc, pa, mm, zz = merge_forward(xa1, mod6, yc, oa, proj, full["w_branch_conv"], full["w_branch_attn"],
                                       full["w_out"], offs, Tc, "merge_fwd")
    (dx3, hm2, ab2, h2, f2, lacc), _ = ffn_forward((x2,), mod6, 2, norm3_g, full["ffn2_w_in"], full["ffn2_w_out"], 0, "ffn2_fwd",
                                                   final=(loss_target[0], final_g.reshape(1, D)))
    loss = lax.psum(lacc[1, 0], ("x", "y", "c"))

    by_dest = lambda g: g.reshape((N_DEV, -1, g.shape[-1]))
    (dx2, dab2, df2, acc_f2), _ = ffn_backward_rows(dx3, (x2,), mod6, 2, norm3_g, ab2, f2, full["ffn2_w_in"], full["ffn2_w_out"], 0, "ffn2_bwd")
    early = {"ffn2_w_out": tn_matmul(h2, df2, "ffn2_dwout")[0], "ffn2_w_in": tn_matmul(dab2, hm2, "ffn2_dwin")[0]}
    dproj, dgm, dpc, dpa, dyc, do, acc_mg = merge_backward_rows(dx2, mod6, zz, pc, pa, proj, full["w_branch_conv"],
                                                                full["w_branch_attn"], full["w_out"], offs, Tc, "merge_bwd")
    early["w_out"] = tn_matmul(mm, dgm, "dw_out")[0]
    early["w_branch_conv"] = tn_matmul(yc, dpc, "dw_bc")[0]
    early["w_branch_attn"] = tn_matmul(oa, dpa, "dw_ba")[0]
    dproj, dcw = conv_backward(dproj, dyc, proj, conv_full, offs, Tc, "conv_bwd")
    (dproj, dk, dv, dqg), summed = attention_backward(dproj, qr, kr, proj, oa, lse, do, q_norm_g, tabs, offs, Tc, "attn_bwd",
                                                      hosted=Hosted(scatters=[by_dest(g) for g in early.values()]))
    summed = dict(zip(early, summed))
    dproj, dkg = kv_backward(dproj, dk, dv, proj, k_norm_g, tabs, offs, "kv_bwd")
    g_w_in_grad = _ungroup_w_in_grad(tn_matmul(dproj, hx, "dw_in")[0], D, Dc, qw, kw)
    (dxa1, acc_pj), (summed["w_in"],) = proj_backward_rows(dproj, dx2, xa1, mod6, norm2_g, full["w_in"], nctx, "proj_bwd",
                                                          hosted=Hosted(scatters=[g_w_in_grad]))
    (grad_x2d, dab1, df1, acc_f1), _ = ffn_backward_rows(
        dxa1, srcs1, mod6, 0, norm1_g, ab1, f1, full["ffn1_w_in"], full["ffn1_w_out"], nctx, "ffn1_bwd")
    g_ffn1_w_in, (summed["ffn1_w_out"],) = tn_matmul(
        dab1, hm1, "ffn1_dwin", hosted=Hosted(scatters=[by_dest(tn_matmul(h1, df1, "ffn1_dwout")[0])]))
    late = ("ffn1_w_in",)
    gs = [by_dest(g_ffn1_w_in)]
    grad_x = grad_x2d[None]
    recv_a = rs_sibling_exchange(gs, "rs_sibling")
    ps = [rs_pair_sum(place, g, a, "rs_pair_sum_" + n) for n, g, a in zip(late, gs, recv_a)]
    recv_b = rs_chip_exchange(ps, "rs_chips")
    reduced = dict(zip(late, zip(ps, recv_b)))

    zero_d = jnp.zeros((D,), F32)
    dlat = jnp.concatenate([acc_f1[1, 0], acc_f1[1, 1], acc_f1[1, 2], acc_pj[1, 0], acc_pj[1, 1], acc_mg[0],
                            acc_f2[1, 0], acc_f2[1, 1], acc_f2[1, 2]])
    dctx = jnp.concatenate([acc_f1[0, 0], acc_f1[0, 1], acc_f1[0, 2], acc_pj[0, 0], acc_pj[0, 1]] + [zero_d] * 4)
    small = jnp.concatenate([acc_f1[0, 3] + acc_f1[1, 3], acc_pj[0, 2] + acc_pj[1, 2], acc_f2[1, 3],
                             dqg[0], dkg[0], lacc[0], dcw[0:CONV_TAPS].reshape(-1)])
    n_small = small.shape[0]
    pay_b = jnp.concatenate([dlat, dctx, small]).reshape(1, -1)
    gath = allgather_direct(pay_b, "ag_small_grads")
    dlat_loc = lax.dynamic_slice_in_dim(gath[:, 0, :nd], me * mod_cols, mod_cols, axis=1)
    dctx_loc = lax.dynamic_slice_in_dim(gath[:, 0, nd:2 * nd], me * mod_cols, mod_cols, axis=1)
    g_wmod, pc_part, small_sum = mod_backward(call, cctx2, w_mod[0], dlat_loc, dctx_loc, gath, 2 * nd, n_small, "mod_bwd")
    pcs = allgather_direct(pc_part, "ag_cctx")
    g_bmod, g_cctx = bmod_and_cctx_grad(gath, pcs, cctx2, nd, "small_bwd")
    sm = small_sum[0]
    g_conv_full = sm[3 * D + 2 * HEAD_DIM + D:].reshape(CONV_TAPS, Dc)
    g_conv = lax.dynamic_slice_in_dim(g_conv_full, me * cw_loc.shape[1], cw_loc.shape[1], axis=1)
    gsmall = dict(
        c_ctx=g_cctx, w_mod=g_wmod, b_mod=g_bmod, norm1_g=sm[0:D][None], norm2_g=sm[D:2 * D][None],
        norm3_g=sm[2 * D:3 * D][None], q_norm_g=sm[3 * D:3 * D + HEAD_DIM][None],
        k_norm_g=sm[3 * D + HEAD_DIM:3 * D + 2 * HEAD_DIM][None],
        final_g=sm[3 * D + 2 * HEAD_DIM:3 * D + 2 * HEAD_DIM + D][None], conv_w=g_conv)

    g_out, d_out, m_out, v_out = [], [], [], []
    flipped = ("ffn1_w_in", "w_in", "ffn2_w_in")
    for n in order:
        w = weights[n]
        shp = w.shape
        if n in flipped:
            two_d = lambda a: jnp.swapaxes(a[0], 0, 1)
            back = lambda a: jnp.swapaxes(a, 0, 1)[None]
        else:
            two_d = lambda a: a.reshape(-1, shp[-1])
            back = lambda a: a.reshape(shp)
        m, v = moms[n]
        if n in reduced:
            g2, d, nm, nv = adamw_reduced(place, *reduced[n], two_d(w), two_d(m), two_d(v), "adamw_" + n)
        elif n in summed:
            g2, d, nm, nv = adamw_summed(summed[n], two_d(w), two_d(m), two_d(v), "adamw_" + n)
        else:
            g2 = gsmall[n].reshape(two_d(w).shape)
            d, nm, nv = adamw(two_d(w), g2, two_d(m), two_d(v), "adamw_" + n)
        g_out.append(back(g2))
        d_out.append(back(d))
        m_out.append(back(nm))
        v_out.append(back(nv))
    return (loss, grad_x, *g_out, *d_out, *m_out, *v_out)
```

```python
import math

import jax
import jax.numpy as jnp
from jax import lax
from jax.experimental import pallas as pl
from jax.experimental.pallas import tpu as pltpu

F32 = jnp.float32
BF = jnp.bfloat16
EPS = 1e-6
N_DEV = 8
HEAD_DIM = 128
N_Q_HEADS = 8
N_KV_HEADS = 2
GROUP = N_Q_HEADS // N_KV_HEADS
GRID_W = 64
ROPE_THETA = 10000.0
CONV_TAPS = 3
N_MOD = 9
ADAM_LR = 0.001
ADAM_B1 = 0.9
ADAM_B2 = 0.999
ADAM_EPS = 1e-08
ADAM_WD = 0.01
ADAM_STEP = 10
ROW_TILE = 256
VMEM_BIG = 56 << 20
MESH_ID = pl.DeviceIdType.MESH
HIGHEST = lax.Precision.HIGHEST
NT = (((1,), (1,)), ((), ()))
TN = (((0,), (0,)), ((), ()))
LOG2E = math.log2(math.e)


def _pick(n, cands):
    for c in cands:
        if n % c == 0:
            return c
    return n


def _params(vmem=None, sem=None):
    kw = {}
    if vmem is not None:
        kw["vmem_limit_bytes"] = vmem
    if sem is not None:
        kw["dimension_semantics"] = sem
    return pltpu.CompilerParams(**kw)


def _resident(shape):
    nd = len(shape)
    return pl.BlockSpec(shape, lambda *_: (0,) * nd, pipeline_mode=pl.Buffered(1))


def _sigmoid(x):
    return jax.nn.sigmoid(x)


ANY = pl.BlockSpec(memory_space=pl.ANY)


def _coords():
    return lax.axis_index("x"), lax.axis_index("y"), lax.axis_index("c")


def _flip(v, bit):
    return 1 - v if bit else v


def _remote(src, dst, ssem, rsem, dev):
    return pltpu.make_async_remote_copy(src_ref=src, dst_ref=dst, send_sem=ssem, recv_sem=rsem,
                                        device_id=dev, device_id_type=MESH_ID)


def allgather_direct(v, name):
    def body(v_ref, out_ref, ssem, rsem, lsem):
        x, y, c = _coords()
        me = 4 * x + 2 * y + c
        mine = pltpu.make_async_copy(v_ref, out_ref.at[me], lsem)
        mine.start()
        cps = []
        for p in range(1, N_DEV):
            px, py, pc = (p >> 2) & 1, (p >> 1) & 1, p & 1
            cps.append(_remote(v_ref, out_ref.at[me], ssem.at[p - 1], rsem.at[p - 1],
                               (_flip(x, px), _flip(y, py), _flip(c, pc))))
        for cp in cps:
            cp.start()
        for p in range(1, N_DEV):
            px, py, pc = (p >> 2) & 1, (p >> 1) & 1, p & 1
            src = 4 * _flip(x, px) + 2 * _flip(y, py) + _flip(c, pc)
            _remote(v_ref, out_ref.at[src], ssem.at[p - 1], rsem.at[p - 1], (x, y, c)).wait_recv()
        for cp in cps:
            cp.wait_send()
        mine.wait()

    return pl.pallas_call(
        body, name=name,
        out_shape=jax.ShapeDtypeStruct((N_DEV,) + v.shape, v.dtype),
        in_specs=[ANY], out_specs=ANY,
        scratch_shapes=[pltpu.SemaphoreType.DMA((N_DEV - 1,)), pltpu.SemaphoreType.DMA((N_DEV - 1,)),
                        pltpu.SemaphoreType.DMA],
    )(v)


def allgather_two_level(shards, name, riders=()):
    n = len(shards)
    ride = Hosted(gathers=riders)
    r = ride.n

    def body(*refs):
        v_refs, rin, out_refs, rout = refs[:n], refs[n:n + r], refs[n + r:2 * n + r], refs[2 * n + r:2 * n + 2 * r]
        (ssem, rsem, lsem), rsems = refs[2 * n + 2 * r:2 * n + 2 * r + 3], refs[2 * n + 2 * r + 3:]
        x, y, c = _coords()
        me = (x, y, c)
        sib = (x, y, 1 - c)
        chips = [(1 - x, y), (x, 1 - y), (1 - x, 1 - y)]
        if r:
            ride.start(rin, rout, *rsems)

        def slot(w, px, py, pc):
            return out_refs[w].at[4 * px + 2 * py + pc]

        def sem(w, k):
            return ssem.at[7 * w + k], rsem.at[7 * w + k]

        mine = [pltpu.make_async_copy(v_refs[w], slot(w, *me), lsem.at[w]) for w in range(n)]
        for cp in mine:
            cp.start()
        first = []
        for w in range(n):
            first.append(_remote(v_refs[w], slot(w, *me), *sem(w, 0), sib))
            first += [_remote(v_refs[w], slot(w, *me), *sem(w, 1 + j), (*chip, c)) for j, chip in enumerate(chips)]
        for cp in first:
            cp.start()
        passed = []
        for w in range(n):
            for j, chip in enumerate(chips):
                _remote(v_refs[w], slot(w, *chip, c), *sem(w, 1 + j), me).wait_recv()
                cp = _remote(slot(w, *chip, c), slot(w, *chip, c), *sem(w, 4 + j), sib)
                cp.start()
                passed.append(cp)
        for w in range(n):
            _remote(v_refs[w], slot(w, x, y, 1 - c), *sem(w, 0), me).wait_recv()
            for j, chip in enumerate(chips):
                _remote(v_refs[w], slot(w, *chip, 1 - c), *sem(w, 4 + j), me).wait_recv()
        for cp in first + passed:
            cp.wait_send()
        for cp in mine:
            cp.wait()
        if r:
            ride.wait(rin, rout, *rsems)

    res = pl.pallas_call(
        body, name=name,
        out_shape=[jax.ShapeDtypeStruct((N_DEV,) + s.shape, s.dtype) for s in shards] + ride.out_shapes,
        in_specs=[ANY] * (n + r), out_specs=[ANY] * (n + r),
        scratch_shapes=[pltpu.SemaphoreType.DMA((7 * n,)), pltpu.SemaphoreType.DMA((7 * n,)),
                        pltpu.SemaphoreType.DMA((n,))] + (ride.scratch if r else []),
    )(*shards, *riders)
    return list(res[:n]), list(res[n:])


def _block_exchange(srcs, k, copies, name, riders):
    n = len(srcs)
    ride = Hosted(gathers=riders)
    r = ride.n

    def body(*refs):
        g_refs, rin, out_refs, rout = refs[:n], refs[n:n + r], refs[n + r:2 * n + r], refs[2 * n + r:2 * n + 2 * r]
        (ssem, rsem), rsems = refs[2 * n + 2 * r:2 * n + 2 * r + 2], refs[2 * n + 2 * r + 2:]
        if r:
            ride.start(rin, rout, *rsems)
        cps = copies(g_refs, out_refs, ssem, rsem)
        for cp in cps:
            cp.start()
        for cp in cps:
            cp.wait()
        if r:
            ride.wait(rin, rout, *rsems)

    res = pl.pallas_call(
        body, name=name,
        out_shape=[jax.ShapeDtypeStruct((k,) + g.shape[1:], g.dtype) for g in srcs] + ride.out_shapes,
        in_specs=[ANY] * (n + r), out_specs=[ANY] * (n + r),
        scratch_shapes=[pltpu.SemaphoreType.DMA((k * n,)), pltpu.SemaphoreType.DMA((k * n,))] + (ride.scratch if r else []),
    )(*srcs, *riders)
    return list(res[:n]), list(res[n:])


def rs_sibling_exchange(gs, name, riders=()):
    def copies(g_refs, out_refs, ssem, rsem):
        x, y, c = _coords()
        return [_remote(g_refs[w].at[2 * k + (1 - c)], out_refs[w].at[k], ssem.at[4 * w + k], rsem.at[4 * w + k], (x, y, 1 - c))
                for w in range(len(gs)) for k in range(4)]

    return _block_exchange(gs, 4, copies, name, riders)


def rs_chip_exchange(ps, name, riders=()):
    def copies(p_refs, out_refs, ssem, rsem):
        x, y, c = _coords()
        chips = [(1 - x, y), (x, 1 - y), (1 - x, 1 - y)]
        return [_remote(p_refs[w].at[2 * cx + cy], out_refs[w].at[j], ssem.at[3 * w + j], rsem.at[3 * w + j], (cx, cy, c))
                for w in range(len(ps)) for j, (cx, cy) in enumerate(chips)]

    return _block_exchange(ps, 3, copies, name, riders)


class Hosted:
    def __init__(self, gathers=(), scatters=()):
        self.items = [(a, False) for a in gathers] + [(a, True) for a in scatters]
        self.n = len(self.items)
        self.operands = [a for a, _ in self.items]
        self.out_shapes = [jax.ShapeDtypeStruct(a.shape if sc else (N_DEV,) + a.shape, a.dtype) for a, sc in self.items]
        self.scratch = [pltpu.SemaphoreType.DMA((7 * self.n,)), pltpu.SemaphoreType.DMA((7 * self.n,)),
                        pltpu.SemaphoreType.DMA((self.n,))]

    def _copies(self, in_refs, out_refs, ssem, rsem, lsem, arrivals):
        x, y, c = _coords()
        me = 4 * x + 2 * y + c
        remote, local = [], []
        for w, (_, sc) in enumerate(self.items):
            src, dst = in_refs[w], out_refs[w]
            local.append(pltpu.make_async_copy(src.at[me] if sc else src, dst.at[me], lsem.at[w]))
            for p in range(1, N_DEV):
                px, py, pc = _flip(x, (p >> 2) & 1), _flip(y, (p >> 1) & 1), _flip(c, p & 1)
                peer = 4 * px + 2 * py + pc
                k = 7 * w + p - 1
                if arrivals:
                    remote.append(_remote(src.at[me] if sc else src, dst.at[peer], ssem.at[k], rsem.at[k], (x, y, c)))
                else:
                    remote.append(_remote(src.at[peer] if sc else src, dst.at[me], ssem.at[k], rsem.at[k], (px, py, pc)))
        return remote, local

    def start(self, in_refs, out_refs, ssem, rsem, lsem):
        sends, local = self._copies(in_refs, out_refs, ssem, rsem, lsem, False)
        for cp in local + sends:
            cp.start()

    def wait(self, in_refs, out_refs, ssem, rsem, lsem):
        arrivals, local = self._copies(in_refs, out_refs, ssem, rsem, lsem, True)
        for cp in arrivals:
            cp.wait_recv()
        for cp in arrivals:
            cp.wait_send()
        for cp in local:
            cp.wait()


def _call(body, *, name, grid, in_specs, out_specs, out_shape, operands, params, scratch_shapes=(), aliases=None, hosted=None):
    n_in, n_out, n_scr = len(in_specs), len(out_specs), len(scratch_shapes)
    h = hosted.n if hosted is not None else 0

    def wrapped(*refs):
        ins, cins = refs[:n_in], refs[n_in:n_in + h]
        outs, couts = refs[n_in + h:n_in + h + n_out], refs[n_in + h + n_out:n_in + 2 * h + n_out]
        rest = refs[n_in + 2 * h + n_out:]
        scr, sems = rest[:n_scr], rest[n_scr:]
        if h:
            ids = [pl.program_id(a) for a in range(len(grid))]
            first, last = ids[0] == 0, ids[0] == grid[0] - 1
            for a in range(1, len(grid)):
                first, last = first & (ids[a] == 0), last & (ids[a] == grid[a] - 1)

            @pl.when(first)
            def _():
                hosted.start(cins, couts, *sems)

        body(*ins, *outs, *scr)
        if h:
            @pl.when(last)
            def _():
                hosted.wait(cins, couts, *sems)

    res = pl.pallas_call(
        wrapped, name=name, grid=grid,
        in_specs=list(in_specs) + [ANY] * h, out_specs=list(out_specs) + [ANY] * h,
        out_shape=list(out_shape) + (hosted.out_shapes if h else []),
        scratch_shapes=list(scratch_shapes) + (hosted.scratch if h else []),
        input_output_aliases=aliases or {}, compiler_params=params,
    )(*operands, *(hosted.operands if h else []))
    return list(res[:n_out]), list(res[n_out:])


def _row_tile(R, C):
    if R * C <= (1 << 18):
        return R
    return max((d for d in range(8, 257, 8) if R % d == 0), default=R)


def rs_pair_sum(mine, recv_a, name):
    _, R, C = mine.shape
    tr = _row_tile(R, C)

    def body(g_ref, a_ref, o_ref):
        o_ref[0] = (g_ref[0].astype(F32) + a_ref[0].astype(F32)).astype(o_ref.dtype)

    blk = pl.BlockSpec((1, tr, C), lambda k, r: (k, r, 0))
    return pl.pallas_call(
        body, name=name, grid=(4, R // tr),
        in_specs=[blk, blk], out_specs=blk,
        out_shape=jax.ShapeDtypeStruct((4, R, C), BF),
    )(mine, recv_a)


def _cond_rows(call_ref, cctx_ref, z_ref, D):
    z_ref[...] = jnp.zeros_like(z_ref)
    for a in range(N_DEV):
        z_ref[a:a + 1, :] = call_ref[a][:, :D]
    z_ref[N_DEV:N_DEV + 1, :] = cctx_ref[...]


def mod_forward(call, c_ctx, w_loc, b_loc, name):
    D, cols = w_loc.shape

    def body(call_ref, cctx_ref, w_ref, b_ref, o_ref, z_ref):
        _cond_rows(call_ref, cctx_ref, z_ref, D)
        z = z_ref[...]
        s = z * _sigmoid(z)
        o_ref[...] = jnp.dot(s, w_ref[...], preferred_element_type=F32, precision=HIGHEST) + b_ref[...]

    return pl.pallas_call(
        body, name=name, out_shape=jax.ShapeDtypeStruct((16, cols), F32),
        scratch_shapes=[pltpu.VMEM((16, D), F32)],
        compiler_params=_params(vmem=VMEM_BIG),
    )(call, c_ctx, w_loc, b_loc)


def mod_backward(call, c_ctx, w_loc, dlat_loc, dctx_loc, gath, n_small_off, n_small, name):
    D, cols = w_loc.shape

    def body(call_ref, cctx_ref, w_ref, dlat_ref, dctx_ref, g_ref, gw_ref, pc_ref, small_ref, z_ref, dm_ref):
        _cond_rows(call_ref, cctx_ref, z_ref, D)
        z = z_ref[...]
        s = z * _sigmoid(z)
        dctx = dctx_ref[0:1, :]
        for a in range(1, N_DEV):
            dctx = dctx + dctx_ref[a:a + 1, :]
        dm_ref[...] = jnp.zeros_like(dm_ref)
        dm_ref[0:N_DEV, :] = dlat_ref[...]
        dm_ref[N_DEV:N_DEV + 1, :] = dctx
        gw_ref[...] = lax.dot_general(s, dm_ref[...], TN, preferred_element_type=F32, precision=HIGHEST)
        pc_ref[...] = lax.dot_general(dctx, w_ref[...], NT, preferred_element_type=F32, precision=HIGHEST)
        acc = g_ref[0][:, n_small_off:n_small_off + n_small]
        for a in range(1, N_DEV):
            acc = acc + g_ref[a][:, n_small_off:n_small_off + n_small]
        small_ref[...] = acc

    return pl.pallas_call(
        body, name=name,
        out_shape=(jax.ShapeDtypeStruct((D, cols), F32), jax.ShapeDtypeStruct((1, D), F32),
                   jax.ShapeDtypeStruct((1, n_small), F32)),
        scratch_shapes=[pltpu.VMEM((16, D), F32), pltpu.VMEM((16, cols), F32)],
        compiler_params=_params(vmem=VMEM_BIG),
    )(call, c_ctx, w_loc, dlat_loc, dctx_loc, gath)


def bmod_and_cctx_grad(gath, pcs, c_ctx, nd, name):
    D = c_ctx.shape[-1]

    def body(g_ref, pc_ref, cctx_ref, gb_ref, gc_ref):
        acc = g_ref[0][:, :nd] + g_ref[0][:, nd:2 * nd]
        for a in range(1, N_DEV):
            acc = acc + (g_ref[a][:, :nd] + g_ref[a][:, nd:2 * nd])
        gb_ref[...] = acc
        p = pc_ref[0]
        for a in range(1, N_DEV):
            p = p + pc_ref[a]
        z = cctx_ref[...]
        sg = _sigmoid(z)
        gc_ref[...] = p * (sg * (1.0 + z * (1.0 - sg)))

    return pl.pallas_call(
        body, name=name,
        out_shape=(jax.ShapeDtypeStruct((1, nd), F32), jax.ShapeDtypeStruct((1, D), F32)),
    )(gath, pcs, c_ctx)


def _mod_spec(D, which, nctx):
    return pl.BlockSpec((1, 3, D), lambda i: (jnp.where(i < nctx, 0, 3) + which, 0, 0))


def _acc_spec(D, nctx):
    return pl.BlockSpec((1, 8, D), lambda i: (jnp.where(i < nctx, 0, 1), 0, 0))


def _row(tm, n):
    return pl.BlockSpec((tm, n), lambda i: (i, 0))


def _two_stream_specs(tm, D, nctx):
    return [pl.BlockSpec((tm, D), lambda i: (jnp.minimum(i, nctx - 1), 0)),
            pl.BlockSpec((tm, D), lambda i: (jnp.maximum(i - nctx, 0), 0))]


def _final_norm_loss_backward(x, tgt, gg, i, dx_ref, acc_ref):
    @pl.when(i == 0)
    def _():
        acc_ref[...] = jnp.zeros_like(acc_ref)

    D = x.shape[1]
    r = lax.rsqrt(jnp.mean(x * x, axis=-1, keepdims=True) + EPS)
    xh = x * r
    e = xh * gg - tgt
    part = 0.5 * jnp.sum(jnp.mean(e * e, axis=-1, keepdims=True), axis=0, keepdims=True)
    dy = e * (1.0 / D)
    dyg = dy * gg
    dx_ref[...] = r * (dyg - xh * jnp.mean(dyg * xh, axis=-1, keepdims=True))
    acc_ref[0:1, :] += jnp.sum(dy * xh, axis=0, keepdims=True)
    acc_ref[1:2, :] += jnp.broadcast_to(part, (1, D))


def _hidden_chunks(F):
    step = 1024 if F % 256 == 0 else F
    return [(lo, min(lo + step, F)) for lo in range(0, F, step)]


def ffn_forward(srcs, mod6, which, g, wt, w_out, nctx, name, hosted=None, final=None):
    D = srcs[-1].shape[1]
    Tr = sum(s.shape[0] for s in srcs)
    F = wt.shape[0] // 2
    tm = ROW_TILE
    two = len(srcs) == 2
    nfin = 0 if final is None else 2

    def body(*refs):
        x_refs, fin_refs, rest = refs[:len(srcs)], refs[len(srcs):len(srcs) + nfin], refs[len(srcs) + nfin:]
        mod_ref, g_ref, wt_ref, wout_ref, xo_ref, hm_ref, ab_ref, h_ref, f_ref = rest[:9]
        x = jnp.where(pl.program_id(0) < nctx, x_refs[0][...], x_refs[1][...]) if two else x_refs[0][...]
        ms = mod_ref[0]
        shift, scale, gate = ms[0:1], ms[1:2], ms[2:3]
        r = lax.rsqrt(jnp.mean(x * x, axis=-1, keepdims=True) + EPS)
        hb = (((x * r) * g_ref[...]) * (1.0 + scale) + shift).astype(BF)
        hm_ref[...] = hb
        f = jnp.zeros((tm, D), F32)
        for lo, hi in _hidden_chunks(F):
            a = lax.dot_general(hb, wt_ref[lo:hi, :], NT, preferred_element_type=F32)
            b = lax.dot_general(hb, wt_ref[F + lo:F + hi, :], NT, preferred_element_type=F32)
            ab_ref[:, lo:hi] = a.astype(BF)
            ab_ref[:, F + lo:F + hi] = b.astype(BF)
            h = ((a * _sigmoid(a)) * b).astype(BF)
            h_ref[:, lo:hi] = h
            f = f + jnp.dot(h, wout_ref[lo:hi, :], preferred_element_type=F32)
        f_ref[...] = f.astype(BF)
        xo = x + (0.5 * gate) * f
        if final is None:
            xo_ref[...] = xo
        else:
            _final_norm_loss_backward(xo, fin_refs[0][...], fin_refs[1][...], pl.program_id(0), xo_ref, rest[9])

    src_specs = _two_stream_specs(tm, D, nctx) if two else [_row(tm, D)]
    fin = final is not None
    return _call(
        body, name=name, grid=(Tr // tm,),
        in_specs=src_specs + ([_row(tm, D), _resident((1, D))] if fin else [])
                 + [_mod_spec(D, which, nctx), _resident((1, D)), _resident(wt.shape), _resident(w_out.shape)],
        out_specs=[_row(tm, D), _row(tm, D), _row(tm, 2 * F), _row(tm, F), _row(tm, D)]
                  + ([pl.BlockSpec((8, D), lambda i: (0, 0))] if fin else []),
        out_shape=[jax.ShapeDtypeStruct((Tr, D), F32), jax.ShapeDtypeStruct((Tr, D), BF),
                   jax.ShapeDtypeStruct((Tr, 2 * F), BF), jax.ShapeDtypeStruct((Tr, F), BF),
                   jax.ShapeDtypeStruct((Tr, D), BF)] + ([jax.ShapeDtypeStruct((8, D), F32)] if fin else []),
        operands=[*srcs, *(final or ()), mod6, g, wt, w_out], hosted=hosted,
        params=_params(vmem=VMEM_BIG, sem=("arbitrary",)))


def ffn_backward_rows(dxo, srcs, mod6, which, g, ab, fo, wt, w_out, nctx, name, hosted=None):
    D = srcs[-1].shape[1]
    Tr = sum(s.shape[0] for s in srcs)
    Tl = srcs[-1].shape[0]
    F = wt.shape[0] // 2
    tm = ROW_TILE
    two = len(srcs) == 2

    def body(*refs):
        dxo_ref, x_refs = refs[0], refs[1:1 + len(srcs)]
        mod_ref, g_ref, ab_ref, fo_ref, wt_ref, wout_ref, dx_ref, dab_ref, df_ref, acc_ref = refs[1 + len(srcs):]
        i = pl.program_id(0)

        @pl.when((i == 0) | (i == nctx))
        def _():
            acc_ref[...] = jnp.zeros_like(acc_ref)

        dxo = dxo_ref[...]
        x = jnp.where(i < nctx, x_refs[0][...], x_refs[1][...]) if two else x_refs[0][...]
        ms = mod_ref[0]
        scale, gate = ms[1:2], ms[2:3]
        gg = g_ref[...]
        dgate = jnp.sum(dxo * fo_ref[...].astype(F32), axis=0, keepdims=True) * 0.5
        dfb = (dxo * (0.5 * gate)).astype(BF)
        df_ref[...] = dfb
        dhm = jnp.zeros((tm, D), F32)
        for lo, hi in _hidden_chunks(F):
            dh = lax.dot_general(dfb, wout_ref[lo:hi, :], NT, preferred_element_type=F32)
            a = ab_ref[:, lo:hi].astype(F32)
            b = ab_ref[:, F + lo:F + hi].astype(F32)
            sg = _sigmoid(a)
            da = ((dh * b) * (sg * (1.0 + a * (1.0 - sg)))).astype(BF)
            db = (dh * (a * sg)).astype(BF)
            dab_ref[:, lo:hi] = da
            dab_ref[:, F + lo:F + hi] = db
            dhm = dhm + jnp.dot(da, wt_ref[lo:hi, :], preferred_element_type=F32)
            dhm = dhm + jnp.dot(db, wt_ref[F + lo:F + hi, :], preferred_element_type=F32)
        r = lax.rsqrt(jnp.mean(x * x, axis=-1, keepdims=True) + EPS)
        xh = x * r
        dshift = jnp.sum(dhm, axis=0, keepdims=True)
        dscale = jnp.sum(dhm * (xh * gg), axis=0, keepdims=True)
        dxh_g = dhm * (1.0 + scale)
        dg = jnp.sum(dxh_g * xh, axis=0, keepdims=True)
        dxh = dxh_g * gg
        dx_ref[...] = dxo + r * (dxh - xh * jnp.mean(dxh * xh, axis=-1, keepdims=True))
        for k, val in enumerate((dshift, dscale, dgate, dg)):
            acc_ref[0, k:k + 1, :] += val

    src_specs = _two_stream_specs(tm, D, nctx) if two else [_row(tm, D)]
    dx_spec = pl.BlockSpec((tm, D), lambda i: (jnp.maximum(i - nctx, 0), 0))
    return _call(
        body, name=name, grid=(Tr // tm,),
        in_specs=[_row(tm, D)] + src_specs + [_mod_spec(D, which, nctx), _resident((1, D)), _row(tm, 2 * F), _row(tm, D),
                                              _resident(wt.shape), _resident(w_out.shape)],
        out_specs=[dx_spec, _row(tm, 2 * F), _row(tm, D), _acc_spec(D, nctx)],
        out_shape=[jax.ShapeDtypeStruct((Tl, D), F32), jax.ShapeDtypeStruct((Tr, 2 * F), BF),
                   jax.ShapeDtypeStruct((Tr, D), BF), jax.ShapeDtypeStruct((2, 8, D), F32)],
        operands=[dxo, *srcs, mod6, g, ab, fo, wt, w_out], hosted=hosted,
        params=_params(vmem=VMEM_BIG, sem=("arbitrary",)))


def _token_tile(T):
    return _pick(T, (2048, 1408, 1024, 768, 512, 256, 128))


def tn_matmul(a, b, name, hosted=None):
    T, K = a.shape
    N = b.shape[1]
    tk = _pick(K, (1024, 1408, 1664, 768, 512, 384, 256, 128))
    tn = _pick(N, (1024, 1408, 1664, 768, 512, 384, 256, 128))
    tt = _token_tile(T)
    nt = T // tt

    def body(a_ref, b_ref, o_ref, acc_ref):
        t = pl.program_id(2)

        @pl.when(t == 0)
        def _():
            acc_ref[...] = jnp.zeros_like(acc_ref)

        acc_ref[...] += lax.dot_general(a_ref[...], b_ref[...], TN, preferred_element_type=F32)

        @pl.when(t == nt - 1)
        def _():
            o_ref[...] = acc_ref[...].astype(BF)

    (out,), exchanged = _call(
        body, name=name, grid=(K // tk, N // tn, nt),
        in_specs=[pl.BlockSpec((tt, tk), lambda k, n, t: (t, k)), pl.BlockSpec((tt, tn), lambda k, n, t: (t, n))],
        out_specs=[pl.BlockSpec((tk, tn), lambda k, n, t: (k, n))],
        out_shape=[jax.ShapeDtypeStruct((K, N), BF)],
        scratch_shapes=[pltpu.VMEM((tk, tn), F32)],
        operands=[a, b], hosted=hosted,
        params=_params(vmem=VMEM_BIG, sem=("arbitrary", "arbitrary", "arbitrary")))
    return out, exchanged


def _rope_apply(y, cos, s_next, s_prev):
    return y * cos + pltpu.roll(y, HEAD_DIM - 32, 1) * s_next + pltpu.roll(y, 32, 1) * s_prev


def _rope_transpose(dz, cos, s_next, s_prev):
    return dz * cos + pltpu.roll(dz * s_next, 32, 1) + pltpu.roll(dz * s_prev, HEAD_DIM - 32, 1)


def proj_forward(xa, mod6, g, w_in, qg, kg, tabs, offs, nctx, name, hosted=None):
    Tr, D = xa.shape
    P = w_in.shape[1]
    tm = ROW_TILE
    qo, ko = offs["q"], offs["k"]
    qw, kw = N_Q_HEADS * HEAD_DIM, N_KV_HEADS * HEAD_DIM
    scale_q = HEAD_DIM ** -0.5 * LOG2E

    def body(x_ref, mod_ref, g_ref, w_ref, qg_ref, kg_ref, tab_ref, hx_ref, pr_ref, q_ref, k_ref):
        x = x_ref[...]
        ms = mod_ref[0]
        shift, scale = ms[0:1], ms[1:2]
        r = lax.rsqrt(jnp.mean(x * x, axis=-1, keepdims=True) + EPS)
        hb = (((x * r) * g_ref[...]) * (1.0 + scale) + shift).astype(BF)
        hx_ref[...] = hb
        pr = jnp.dot(hb, w_ref[...], preferred_element_type=F32)
        pr_ref[...] = pr.astype(BF)
        cos, s_next, s_prev = tab_ref[0], tab_ref[1], tab_ref[2]

        def head(v, gain):
            n = v * lax.rsqrt(jnp.mean(v * v, axis=-1, keepdims=True) + EPS)
            return _rope_apply(n * gain, cos, s_next, s_prev)

        for h in range(N_Q_HEADS):
            lo = qo + h * HEAD_DIM
            q_ref[:, h * HEAD_DIM:(h + 1) * HEAD_DIM] = (head(pr[:, lo:lo + HEAD_DIM], qg_ref[...]) * scale_q).astype(BF)
        for h in range(N_KV_HEADS):
            lo = ko + h * HEAD_DIM
            k_ref[:, h * HEAD_DIM:(h + 1) * HEAD_DIM] = head(pr[:, lo:lo + HEAD_DIM], kg_ref[...]).astype(BF)

    return _call(
        body, name=name, grid=(Tr // tm,),
        in_specs=[_row(tm, D), _mod_spec(D, 1, nctx), _resident((1, D)), _resident(w_in.shape),
                  _resident((1, HEAD_DIM)), _resident((1, HEAD_DIM)),
                  pl.BlockSpec((3, tm, HEAD_DIM), lambda i: (0, i, 0))],
        out_specs=[_row(tm, D), _row(tm, P), _row(tm, qw), _row(tm, kw)],
        out_shape=[jax.ShapeDtypeStruct((Tr, D), BF), jax.ShapeDtypeStruct((Tr, P), BF),
                   jax.ShapeDtypeStruct((Tr, qw), BF), jax.ShapeDtypeStruct((Tr, kw), BF)],
        operands=[xa, mod6, g, w_in, qg, kg, tabs], hosted=hosted,
        params=_params(vmem=VMEM_BIG, sem=("arbitrary",)))


def _shifted(u, first_row, last_row):
    T = u.shape[0]
    prev = jnp.where(first_row, 0.0, pltpu.roll(u, 1, 0))
    nxt = jnp.where(last_row, 0.0, pltpu.roll(u, T - 1, 0))
    return prev, nxt


def conv_forward(proj, conv_w, offs, Tc, name):
    Ta = proj.shape[0]
    T = Ta - Tc
    Dc = conv_w.shape[1]
    cb = offs["cv"] // 384

    def body(p_ref, w_ref, y_ref):
        rows = lax.broadcasted_iota(jnp.int32, (T, 128), 0)
        u = p_ref[pl.ds(Tc, T), 128:256].astype(F32) * p_ref[pl.ds(Tc, T), 256:384].astype(F32)
        prev, nxt = _shifted(u, rows == 0, rows == T - 1)
        w = w_ref[...]
        cv = prev * w[0:1] + u * w[1:2] + nxt * w[2:3]
        y_ref[...] = (p_ref[pl.ds(Tc, T), 0:128].astype(F32) * cv).astype(BF)

    return pl.pallas_call(
        body, name=name, grid=(Dc // 128,),
        in_specs=[pl.BlockSpec((Ta, 384), lambda j: (0, cb + j)), pl.BlockSpec((CONV_TAPS, 128), lambda j: (0, j))],
        out_specs=pl.BlockSpec((T, 128), lambda j: (0, j)),
        out_shape=jax.ShapeDtypeStruct((T, Dc), BF),
        compiler_params=_params(vmem=VMEM_BIG, sem=("arbitrary",)),
    )(proj, conv_w)


def conv_backward(dproj, dy, proj, conv_w, offs, Tc, name):
    Ta = proj.shape[0]
    T = Ta - Tc
    Dc = conv_w.shape[1]
    cb = offs["cv"] // 384

    def body(dp_any, dy_ref, p_ref, w_ref, o_ref, dw_ref):
        rows = lax.broadcasted_iota(jnp.int32, (T, 128), 0)
        first, last = rows == 0, rows == T - 1
        bg = p_ref[pl.ds(Tc, T), 0:128].astype(F32)
        cg = p_ref[pl.ds(Tc, T), 128:256].astype(F32)
        vc = p_ref[pl.ds(Tc, T), 256:384].astype(F32)
        dy = dy_ref[...].astype(F32)
        u = cg * vc
        prev, nxt = _shifted(u, first, last)
        w = w_ref[...]
        cv = prev * w[0:1] + u * w[1:2] + nxt * w[2:3]
        o_ref[pl.ds(0, Tc), :] = jnp.zeros((Tc, 384), BF)
        o_ref[pl.ds(Tc, T), 0:128] = (dy * cv).astype(BF)
        dcv = dy * bg
        dprev, dnxt = _shifted(dcv, first, last)
        du = dnxt * w[0:1] + dcv * w[1:2] + dprev * w[2:3]
        o_ref[pl.ds(Tc, T), 128:256] = (du * vc).astype(BF)
        o_ref[pl.ds(Tc, T), 256:384] = (du * cg).astype(BF)
        dw_ref[...] = jnp.zeros_like(dw_ref)
        for k, tap in enumerate((prev, u, nxt)):
            dw_ref[k:k + 1, :] = jnp.sum(dcv * tap, axis=0, keepdims=True)

    blk = pl.BlockSpec((Ta, 384), lambda j: (0, cb + j))
    return pl.pallas_call(
        body, name=name, grid=(Dc // 128,),
        in_specs=[ANY, pl.BlockSpec((T, 128), lambda j: (0, j)), blk, pl.BlockSpec((CONV_TAPS, 128), lambda j: (0, j))],
        out_specs=[blk, pl.BlockSpec((8, 128), lambda j: (0, j))],
        out_shape=[jax.ShapeDtypeStruct(dproj.shape, BF), jax.ShapeDtypeStruct((8, Dc), F32)],
        input_output_aliases={0: 0},
        compiler_params=_params(vmem=VMEM_BIG, sem=("arbitrary",)),
    )(dproj, dy, proj, conv_w)


def _kv_chunk(Ta):
    return _pick(Ta, (768, 512, 384, 256, 128))


def _stack_heads(v):
    return jnp.concatenate([v[:, h * HEAD_DIM:(h + 1) * HEAD_DIM] for h in range(GROUP)], axis=0)


def attention_forward(q, k, proj, offs, Tc, name, hosted=None):
    Ta = k.shape[0]
    T = Ta - Tc
    tq = ROW_TILE
    kc = _kv_chunk(Ta)
    nkv = Ta // kc
    gw = GROUP * HEAD_DIM
    vblk = offs["v"] // HEAD_DIM
    qoff = Tc // tq
    n = GROUP * tq

    def body(q_ref, k_ref, v_ref, o_ref, lse_ref, vx_ref, qs_ref, s0_ref, s1_ref, m_ref, acc_ref):
        @pl.when(pl.program_id(1) == 0)
        def _():
            vx_ref[:, 0:HEAD_DIM] = v_ref[...]
            vx_ref[:, HEAD_DIM:2 * HEAD_DIM] = jnp.ones((Ta, HEAD_DIM), BF)

        qs_ref[...] = _stack_heads(q_ref[...])
        m_ref[...] = jnp.full((n, 1), -1e30, F32)
        acc_ref[...] = jnp.zeros((n, 2 * HEAD_DIM), F32)

        def rows(c):
            return pl.ds(pl.multiple_of(c * kc, kc), kc)

        def logits(c, dst):
            dst[...] = lax.dot_general(qs_ref[...], k_ref[rows(c), :], NT, preferred_element_type=F32)

        def consume(src, c):
            s = src[...]
            m_prev = m_ref[...]
            m_new = jnp.maximum(m_prev, jnp.max(s, axis=-1, keepdims=True))
            p = jnp.exp2(s - m_new).astype(BF)
            acc_ref[...] = jnp.exp2(m_prev - m_new) * acc_ref[...] + jnp.dot(p, vx_ref[rows(c), :], preferred_element_type=F32)
            m_ref[...] = m_new

        def pair(i, carry):
            logits(2 * i + 1, s1_ref)
            consume(s0_ref, 2 * i)
            logits(2 * i + 2, s0_ref)
            consume(s1_ref, 2 * i + 1)
            return carry

        logits(0, s0_ref)
        if nkv % 2:
            lax.fori_loop(0, nkv // 2, pair, 0)
            consume(s0_ref, nkv - 1)
        else:
            lax.fori_loop(0, nkv // 2 - 1, pair, 0)
            logits(nkv - 1, s1_ref)
            consume(s0_ref, nkv - 2)
            consume(s1_ref, nkv - 1)
        acc = acc_ref[...]
        l = acc[:, HEAD_DIM:HEAD_DIM + 1]
        o = acc[:, 0:HEAD_DIM] / l
        lse = m_ref[...] + jnp.log2(l)
        for h in range(GROUP):
            o_ref[:, h * HEAD_DIM:(h + 1) * HEAD_DIM] = o[h * tq:(h + 1) * tq].astype(BF)
            lse_ref[0, :, h:h + 1] = lse[h * tq:(h + 1) * tq]

    return _call(
        body, name=name, grid=(N_KV_HEADS, T // tq),
        in_specs=[pl.BlockSpec((tq, gw), lambda j, i: (i + qoff, j)),
                  pl.BlockSpec((Ta, HEAD_DIM), lambda j, i: (0, j)),
                  pl.BlockSpec((Ta, HEAD_DIM), lambda j, i: (0, vblk + j))],
        out_specs=[pl.BlockSpec((tq, gw), lambda j, i: (i, j)),
                   pl.BlockSpec((1, tq, GROUP), lambda j, i: (j, i, 0))],
        out_shape=[jax.ShapeDtypeStruct((T, N_Q_HEADS * HEAD_DIM), BF),
                   jax.ShapeDtypeStruct((N_KV_HEADS, T, GROUP), F32)],
        scratch_shapes=[pltpu.VMEM((Ta, 2 * HEAD_DIM), BF), pltpu.VMEM((n, HEAD_DIM), BF), pltpu.VMEM((n, kc), F32),
                        pltpu.VMEM((n, kc), F32), pltpu.VMEM((n, 1), F32), pltpu.VMEM((n, 2 * HEAD_DIM), F32)],
        operands=[q, k, proj], hosted=hosted,
        params=_params(vmem=VMEM_BIG, sem=("arbitrary", "arbitrary")))


def _norm_rope_backward(dz, raw, gg, cos, s_next, s_prev):
    r = lax.rsqrt(jnp.mean(raw * raw, axis=-1, keepdims=True) + EPS)
    n = raw * r
    dy = _rope_transpose(dz, cos, s_next, s_prev)
    dn = dy * gg
    return r * (dn - n * jnp.mean(dn * n, axis=-1, keepdims=True)), jnp.sum(dy * n, axis=0, keepdims=True)


def attention_backward(dproj, q, k, proj, o, lse, do, qgain, tabs, offs, Tc, name, hosted=None):
    Ta = k.shape[0]
    tq = ROW_TILE
    nctx = Tc // tq
    kc = _kv_chunk(Ta)
    gw = GROUP * HEAD_DIM
    vblk = offs["v"] // HEAD_DIM
    qblk = offs["q"] // gw
    zscale = HEAD_DIM ** -0.5

    def body(dp_any, q_ref, k_ref, v_ref, o_ref, lse_ref, do_ref, raw_ref, g_ref, tab_ref, dqr_ref, dk_ref, dv_ref, dg_ref):
        j, i = pl.program_id(0), pl.program_id(1)

        @pl.when(i == 0)
        def _():
            dk_ref[...] = jnp.zeros_like(dk_ref)
            dv_ref[...] = jnp.zeros_like(dv_ref)

        @pl.when((i == 0) & (j == 0))
        def _():
            dg_ref[...] = jnp.zeros_like(dg_ref)

        @pl.when(i < nctx)
        def _():
            dqr_ref[...] = jnp.zeros_like(dqr_ref)

        @pl.when(i >= nctx)
        def _():
            qs = _stack_heads(q_ref[...])
            dob = do_ref[...]
            dos = _stack_heads(dob)
            delta = jnp.concatenate(
                [jnp.sum(dob[:, h * HEAD_DIM:(h + 1) * HEAD_DIM].astype(F32)
                         * o_ref[:, h * HEAD_DIM:(h + 1) * HEAD_DIM].astype(F32), axis=-1, keepdims=True)
                 for h in range(GROUP)], axis=0)
            lse = jnp.concatenate([lse_ref[0, :, h:h + 1] for h in range(GROUP)], axis=0)

            def step(c, dq):
                rows = pl.ds(pl.multiple_of(c * kc, kc), kc)
                kk = k_ref[rows, :]
                vv = v_ref[rows, :]
                s = lax.dot_general(qs, kk, NT, preferred_element_type=F32)
                p = jnp.exp2(s - lse)
                dp = lax.dot_general(dos, vv, NT, preferred_element_type=F32)
                ds = (p * (dp - delta)).astype(BF)
                dv_ref[rows, :] += lax.dot_general(p.astype(BF), dos, TN, preferred_element_type=F32)
                dk_ref[rows, :] += lax.dot_general(ds, qs, TN, preferred_element_type=F32)
                return dq + jnp.dot(ds, kk, preferred_element_type=F32)

            dq = lax.fori_loop(0, Ta // kc, step, jnp.zeros((GROUP * tq, HEAD_DIM), F32))
            per_head = lambda t: jnp.concatenate([t] * GROUP, axis=0)
            dr, dg = _norm_rope_backward(dq * zscale, _stack_heads(raw_ref[...]).astype(F32), g_ref[...],
                                         per_head(tab_ref[0]), per_head(tab_ref[1]), per_head(tab_ref[2]))
            for h in range(GROUP):
                dqr_ref[:, h * HEAD_DIM:(h + 1) * HEAD_DIM] = dr[h * tq:(h + 1) * tq].astype(BF)
            dg_ref[0:1, :] += dg

    lat = lambda j, i: (jnp.maximum(i - nctx, 0), j)
    (dproj, dk, dv, dqg), exchanged = _call(
        body, name=name, grid=(N_KV_HEADS, Ta // tq),
        in_specs=[ANY, pl.BlockSpec((tq, gw), lambda j, i: (i, j)),
                  pl.BlockSpec((Ta, HEAD_DIM), lambda j, i: (0, j)),
                  pl.BlockSpec((Ta, HEAD_DIM), lambda j, i: (0, vblk + j)),
                  pl.BlockSpec((tq, gw), lat),
                  pl.BlockSpec((1, tq, GROUP), lambda j, i: (j, jnp.maximum(i - nctx, 0), 0)),
                  pl.BlockSpec((tq, gw), lat),
                  pl.BlockSpec((tq, gw), lambda j, i: (i, qblk + j)),
                  pl.BlockSpec((1, HEAD_DIM), lambda j, i: (0, 0)),
                  pl.BlockSpec((3, tq, HEAD_DIM), lambda j, i: (0, i, 0))],
        out_specs=[pl.BlockSpec((tq, gw), lambda j, i: (i, qblk + j)),
                   pl.BlockSpec((Ta, HEAD_DIM), lambda j, i: (0, j)),
                   pl.BlockSpec((Ta, HEAD_DIM), lambda j, i: (0, j)),
                   pl.BlockSpec((8, HEAD_DIM), lambda j, i: (0, 0))],
        out_shape=[jax.ShapeDtypeStruct(dproj.shape, BF),
                   jax.ShapeDtypeStruct((Ta, N_KV_HEADS * HEAD_DIM), F32),
                   jax.ShapeDtypeStruct((Ta, N_KV_HEADS * HEAD_DIM), F32),
                   jax.ShapeDtypeStruct((8, HEAD_DIM), F32)],
        operands=[dproj, q, k, proj, o, lse, do, proj, qgain, tabs], hosted=hosted, aliases={0: 0},
        params=_params(vmem=VMEM_BIG, sem=("arbitrary", "arbitrary")))
    return (dproj, dk, dv, dqg), exchanged


def kv_backward(dproj, dk, dv, proj, gain, tabs, offs, name):
    Ta = proj.shape[0]
    tm = _pick(Ta, (768, 512, ROW_TILE))
    kw = N_KV_HEADS * HEAD_DIM
    cb = offs["k"] // (2 * kw)
    kb = offs["k"] // kw
    zscale = 1.0 / LOG2E

    def body(dp_any, dk_ref, dv_ref, raw_ref, g_ref, tab_ref, o_ref, dg_ref):
        @pl.when(pl.program_id(0) == 0)
        def _():
            dg_ref[...] = jnp.zeros_like(dg_ref)

        cos, s_next, s_prev = tab_ref[0], tab_ref[1], tab_ref[2]
        dg = jnp.zeros((1, HEAD_DIM), F32)
        for h in range(N_KV_HEADS):
            sl = slice(h * HEAD_DIM, (h + 1) * HEAD_DIM)
            dr, dgh = _norm_rope_backward(dk_ref[:, sl] * zscale, raw_ref[:, sl].astype(F32), g_ref[...], cos, s_next, s_prev)
            o_ref[:, sl] = dr.astype(BF)
            dg = dg + dgh
        o_ref[:, kw:2 * kw] = dv_ref[...].astype(BF)
        dg_ref[0:1, :] += dg

    return pl.pallas_call(
        body, name=name, grid=(Ta // tm,),
        in_specs=[ANY, _row(tm, kw), _row(tm, kw), pl.BlockSpec((tm, kw), lambda i: (i, kb)),
                  _resident((1, HEAD_DIM)), pl.BlockSpec((3, tm, HEAD_DIM), lambda i: (0, i, 0))],
        out_specs=[pl.BlockSpec((tm, 2 * kw), lambda i: (i, cb)), pl.BlockSpec((8, HEAD_DIM), lambda i: (0, 0))],
        out_shape=[jax.ShapeDtypeStruct(dproj.shape, BF), jax.ShapeDtypeStruct((8, HEAD_DIM), F32)],
        input_output_aliases={0: 0},
        compiler_params=_params(sem=("arbitrary",)),
    )(dproj, dk, dv, proj, gain, tabs)


def merge_forward(x1, mod6, yc, o, proj, w_bc, w_ba, w_o, offs, Tc, name):
    T, D = yc.shape[0], x1.shape[1]
    tm = ROW_TILE
    roff = Tc // tm
    gb = offs["gt"] // (2 * D)

    def body(x_ref, mod_ref, yc_ref, o_ref, gt_ref, wbc_ref, wba_ref, wo_ref, xo_ref, pc_ref, pa_ref, m_ref, z_ref):
        gate = mod_ref[0][2:3]
        pc = jnp.dot(yc_ref[...], wbc_ref[...], preferred_element_type=F32)
        pa = jnp.dot(o_ref[...], wba_ref[...], preferred_element_type=F32)
        pc_ref[...] = pc.astype(BF)
        pa_ref[...] = pa.astype(BF)
        mb = (_sigmoid(gt_ref[:, 0:D].astype(F32)) * pc + _sigmoid(gt_ref[:, D:2 * D].astype(F32)) * pa).astype(BF)
        m_ref[...] = mb
        z = jnp.dot(mb, wo_ref[...], preferred_element_type=F32)
        z_ref[...] = z.astype(BF)
        xo_ref[...] = x_ref[...] + gate * z

    return pl.pallas_call(
        body, name=name, grid=(T // tm,),
        in_specs=[pl.BlockSpec((tm, D), lambda i: (i + roff, 0)), _mod_spec(D, 1, 0), _row(tm, yc.shape[1]), _row(tm, o.shape[1]),
                  pl.BlockSpec((tm, 2 * D), lambda i: (i + roff, gb)),
                  _resident(w_bc.shape), _resident(w_ba.shape), _resident(w_o.shape)],
        out_specs=[_row(tm, D)] * 5,
        out_shape=[jax.ShapeDtypeStruct((T, D), F32)] + [jax.ShapeDtypeStruct((T, D), BF)] * 4,
        compiler_params=_params(vmem=VMEM_BIG, sem=("arbitrary",)),
    )(x1, mod6, yc, o, proj, w_bc, w_ba, w_o)


def merge_backward_rows(dx2, mod6, z, pc, pa, proj, w_bc, w_ba, w_o, offs, Tc, name):
    T, D = dx2.shape
    Ta, P = proj.shape
    tm = ROW_TILE
    nctx = Tc // tm
    gb = offs["gt"] // (2 * D)
    dcw, dqw = w_bc.shape[0], w_ba.shape[0]

    def body(dx_ref, mod_ref, z_ref, pc_ref, pa_ref, gt_ref, wbc_ref, wba_ref, wo_ref,
             dgt_ref, dg_ref, dpc_ref, dpa_ref, dyc_ref, do_ref, acc_ref):
        i = pl.program_id(0)

        @pl.when(i == 0)
        def _():
            acc_ref[...] = jnp.zeros_like(acc_ref)

        @pl.when(i < nctx)
        def _():
            dgt_ref[...] = jnp.zeros_like(dgt_ref)

        @pl.when(i >= nctx)
        def _():
            gate = mod_ref[0][2:3]
            dx = dx_ref[...]
            acc_ref[0:1, :] += jnp.sum(dx * z_ref[...].astype(F32), axis=0, keepdims=True)
            dgb = (dx * gate).astype(BF)
            dg_ref[...] = dgb
            dm = lax.dot_general(dgb, wo_ref[...], NT, preferred_element_type=F32)
            sc = _sigmoid(gt_ref[:, 0:D].astype(F32))
            sa = _sigmoid(gt_ref[:, D:2 * D].astype(F32))
            pc = pc_ref[...].astype(F32)
            pa = pa_ref[...].astype(F32)
            dpc = (dm * sc).astype(BF)
            dpa = (dm * sa).astype(BF)
            dpc_ref[...] = dpc
            dpa_ref[...] = dpa
            dgt_ref[:, 0:D] = ((dm * pc) * (sc * (1.0 - sc))).astype(BF)
            dgt_ref[:, D:2 * D] = ((dm * pa) * (sa * (1.0 - sa))).astype(BF)
            dyc_ref[...] = lax.dot_general(dpc, wbc_ref[...], NT, preferred_element_type=F32).astype(BF)
            do_ref[...] = lax.dot_general(dpa, wba_ref[...], NT, preferred_element_type=F32).astype(BF)

    lat = lambda n: pl.BlockSpec((tm, n), lambda i: (jnp.maximum(i - nctx, 0), 0))
    return pl.pallas_call(
        body, name=name, grid=(Ta // tm,),
        in_specs=[lat(D), _mod_spec(D, 1, 0), lat(D), lat(D), lat(D),
                  pl.BlockSpec((tm, 2 * D), lambda i: (i, gb)),
                  _resident(w_bc.shape), _resident(w_ba.shape), _resident(w_o.shape)],
        out_specs=[pl.BlockSpec((tm, 2 * D), lambda i: (i, gb)), lat(D), lat(D), lat(D), lat(dcw), lat(dqw),
                   pl.BlockSpec((8, D), lambda i: (0, 0))],
        out_shape=[jax.ShapeDtypeStruct((Ta, P), BF)] + [jax.ShapeDtypeStruct((T, D), BF)] * 3
                  + [jax.ShapeDtypeStruct((T, dcw), BF), jax.ShapeDtypeStruct((T, dqw), BF), jax.ShapeDtypeStruct((8, D), F32)],
        compiler_params=_params(vmem=VMEM_BIG, sem=("arbitrary",)),
    )(dx2, mod6, z, pc, pa, proj, w_bc, w_ba, w_o)


def proj_backward_rows(dproj, dres, xa, mod6, g, w_in, nctx, name, hosted=None):
    Tr, D = xa.shape
    P = w_in.shape[1]
    tm = ROW_TILE

    def body(dp_ref, dres_ref, x_ref, mod_ref, g_ref, w_ref, dx_ref, acc_ref):
        i = pl.program_id(0)

        @pl.when((i == 0) | (i == nctx))
        def _():
            acc_ref[...] = jnp.zeros_like(acc_ref)

        x = x_ref[...]
        scale = mod_ref[0][1:2]
        gg = g_ref[...]
        dhm = lax.dot_general(dp_ref[...], w_ref[...], NT, preferred_element_type=F32)
        r = lax.rsqrt(jnp.mean(x * x, axis=-1, keepdims=True) + EPS)
        xh = x * r
        dshift = jnp.sum(dhm, axis=0, keepdims=True)
        dscale = jnp.sum(dhm * (xh * gg), axis=0, keepdims=True)
        dxh_g = dhm * (1.0 + scale)
        dg = jnp.sum(dxh_g * xh, axis=0, keepdims=True)
        dxh = dxh_g * gg
        res = jnp.where(i < nctx, 0.0, dres_ref[...])
        dx_ref[...] = res + r * (dxh - xh * jnp.mean(dxh * xh, axis=-1, keepdims=True))
        for k, val in enumerate((dshift, dscale, dg)):
            acc_ref[0, k:k + 1, :] += val

    return _call(
        body, name=name, grid=(Tr // tm,),
        in_specs=[_row(tm, P), pl.BlockSpec((tm, D), lambda i: (jnp.maximum(i - nctx, 0), 0)), _row(tm, D),
                  _mod_spec(D, 1, nctx), _resident((1, D)), _resident(w_in.shape)],
        out_specs=[_row(tm, D), _acc_spec(D, nctx)],
        out_shape=[jax.ShapeDtypeStruct((Tr, D), F32), jax.ShapeDtypeStruct((2, 8, D), F32)],
        operands=[dproj, dres, xa, mod6, g, w_in], hosted=hosted,
        params=_params(vmem=VMEM_BIG, sem=("arbitrary",)))


def _adam_update(w, g, m, v):
    c1 = 1.0 - ADAM_B1 ** ADAM_STEP
    c2 = 1.0 - ADAM_B2 ** ADAM_STEP
    m = ADAM_B1 * m + (1.0 - ADAM_B1) * g
    v = ADAM_B2 * v + (1.0 - ADAM_B2) * (g * g)
    return -ADAM_LR * ((m / c1) / (jnp.sqrt(v / c2) + ADAM_EPS) + ADAM_WD * w), m, v


def adamw(w, g, m, v, name):
    R, C = w.shape
    tr = _row_tile(R, C)

    def body(w_ref, g_ref, m_ref, v_ref, d_ref, nm_ref, nv_ref):
        d_ref[...], nm_ref[...], nv_ref[...] = _adam_update(w_ref[...], g_ref[...], m_ref[...], v_ref[...])

    blk = pl.BlockSpec((tr, C), lambda i: (i, 0))
    return pl.pallas_call(
        body, name=name, grid=(R // tr,),
        in_specs=[blk] * 4, out_specs=[blk] * 3,
        out_shape=[jax.ShapeDtypeStruct((R, C), F32)] * 3,
        compiler_params=_params(vmem=VMEM_BIG, sem=("parallel",)),
    )(w, g, m, v)


def adamw_summed(recv, w, m, v, name):
    R, C = w.shape
    tr = _row_tile(R, C)

    def body(r_ref, w_ref, m_ref, v_ref, g_ref, d_ref, nm_ref, nv_ref):
        g = r_ref[0].astype(F32)
        for a in range(1, N_DEV):
            g = g + r_ref[a].astype(F32)
        g_ref[...] = g
        d_ref[...], nm_ref[...], nv_ref[...] = _adam_update(w_ref[...], g, m_ref[...], v_ref[...])

    blk = pl.BlockSpec((tr, C), lambda i: (i, 0))
    return pl.pallas_call(
        body, name=name, grid=(R // tr,),
        in_specs=[pl.BlockSpec((N_DEV, tr, C), lambda i: (0, i, 0)), blk, blk, blk], out_specs=[blk] * 4,
        out_shape=[jax.ShapeDtypeStruct((R, C), F32)] * 4,
        compiler_params=_params(vmem=VMEM_BIG, sem=("parallel",)),
    )(recv, w, m, v)


def adamw_reduced(p_own, recv_b, w, m, v, name):
    R, C = w.shape
    tr = _row_tile(R, C)

    def body(p_ref, b_ref, w_ref, m_ref, v_ref, g_ref, d_ref, nm_ref, nv_ref):
        g = p_ref[...].astype(F32)
        for j in range(3):
            g = g + b_ref[j].astype(F32)
        g_ref[...] = g
        d_ref[...], nm_ref[...], nv_ref[...] = _adam_update(w_ref[...], g, m_ref[...], v_ref[...])

    blk = pl.BlockSpec((tr, C), lambda r: (r, 0))
    return pl.pallas_call(
        body, name=name, grid=(R // tr,),
        in_specs=[blk, pl.BlockSpec((3, tr, C), lambda r: (0, r, 0)), blk, blk, blk], out_specs=[blk] * 4,
        out_shape=[jax.ShapeDtypeStruct((R, C), F32)] * 4,
        compiler_params=_params(vmem=VMEM_BIG, sem=("parallel",)),
    )(p_own, recv_b, w, m, v)


def _rope_tables(T, Tc):
    rows = T // GRID_W
    n_freq = HEAD_DIM // 4
    inv = ROPE_THETA ** (-jnp.arange(n_freq, dtype=F32) / n_freq)
    ang_r = jnp.arange(rows).astype(F32)[:, None] * inv
    ang_c = jnp.arange(GRID_W).astype(F32)[:, None] * inv
    per_row = lambda a: jnp.broadcast_to(a[:, None, :], (rows, GRID_W, n_freq)).reshape(T, n_freq)
    per_col = lambda a: jnp.broadcast_to(a[None, :, :], (rows, GRID_W, n_freq)).reshape(T, n_freq)
    cr, sr = per_row(jnp.cos(ang_r)), per_row(jnp.sin(ang_r))
    cc, sc = per_col(jnp.cos(ang_c)), per_col(jnp.sin(ang_c))
    zero = jnp.zeros_like(sr)
    cos = jnp.concatenate([cr, cr, cc, cc], axis=1)
    s_next = jnp.concatenate([-sr, zero, -sc, zero], axis=1)
    s_prev = jnp.concatenate([zero, sr, zero, sc], axis=1)
    lat = jnp.stack([cos, s_next, s_prev])
    ctx = jnp.stack([jnp.ones((Tc, HEAD_DIM), F32), jnp.zeros((Tc, HEAD_DIM), F32), jnp.zeros((Tc, HEAD_DIM), F32)])
    return jnp.concatenate([ctx, lat], axis=1)


BIG = ("ffn1_w_in", "ffn1_w_out", "w_in", "w_branch_conv", "w_branch_attn", "w_out", "ffn2_w_in", "ffn2_w_out")


def _regroup_w_in(stacked, D, Dc, qw, kw):
    w = stacked.transpose(1, 0, 2).reshape(D, -1)
    o = 0
    parts = {}
    for nme, wd in (("bg", Dc), ("cg", Dc), ("vc", Dc), ("q", qw), ("k", kw), ("v", kw), ("gt", 2 * D)):
        parts[nme] = w[:, o:o + wd]
        o += wd
    nb = Dc // 128
    cv = jnp.stack([parts[n].reshape(D, nb, 128) for n in ("bg", "cg", "vc")], axis=2).reshape(D, 3 * Dc)
    return jnp.concatenate([cv, parts["q"], parts["gt"], parts["k"], parts["v"]], axis=1)


def _ungroup_w_in_grad(gt_, D, Dc, qw, kw):
    nb = Dc // 128
    cv = gt_[:3 * Dc].reshape(nb, 3, 128, D)
    o = 3 * Dc
    q = gt_[o:o + qw]
    gt = gt_[o + qw:o + qw + 2 * D]
    k = gt_[o + qw + 2 * D:o + qw + 2 * D + kw]
    v = gt_[o + qw + 2 * D + kw:]
    nat = jnp.concatenate([cv[:, 0].reshape(Dc, D), cv[:, 1].reshape(Dc, D), cv[:, 2].reshape(Dc, D), q, k, v, gt], axis=0)
    return nat.reshape(N_DEV, -1, D)


def kernel(x, c, ctx, c_ctx, w_mod, b_mod, norm1_g, norm2_g, norm3_g, ffn1_w_in, ffn1_w_out, w_in, conv_w, q_norm_g, k_norm_g, w_branch_conv, w_branch_attn, w_out, ffn2_w_in, ffn2_w_out, final_g, loss_target, m_c_ctx, m_w_mod, m_b_mod, m_norm1_g, m_norm2_g, m_norm3_g, m_ffn1_w_in, m_ffn1_w_out, m_w_in, m_conv_w, m_q_norm_g, m_k_norm_g, m_w_branch_conv, m_w_branch_attn, m_w_out, m_ffn2_w_in, m_ffn2_w_out, m_final_g, v_c_ctx, v_w_mod, v_b_mod, v_norm1_g, v_norm2_g, v_norm3_g, v_ffn1_w_in, v_ffn1_w_out, v_w_in, v_conv_w, v_q_norm_g, v_k_norm_g, v_w_branch_conv, v_w_branch_attn, v_w_out, v_ffn2_w_in, v_ffn2_w_out, v_final_g):
    weights = dict(c_ctx=c_ctx, w_mod=w_mod, b_mod=b_mod, norm1_g=norm1_g, norm2_g=norm2_g, norm3_g=norm3_g,
                   ffn1_w_in=ffn1_w_in, ffn1_w_out=ffn1_w_out, w_in=w_in, conv_w=conv_w, q_norm_g=q_norm_g,
                   k_norm_g=k_norm_g, w_branch_conv=w_branch_conv, w_branch_attn=w_branch_attn, w_out=w_out,
                   ffn2_w_in=ffn2_w_in, ffn2_w_out=ffn2_w_out, final_g=final_g)
    moms = dict(c_ctx=(m_c_ctx, v_c_ctx), w_mod=(m_w_mod, v_w_mod), b_mod=(m_b_mod, v_b_mod),
                norm1_g=(m_norm1_g, v_norm1_g), norm2_g=(m_norm2_g, v_norm2_g), norm3_g=(m_norm3_g, v_norm3_g),
                ffn1_w_in=(m_ffn1_w_in, v_ffn1_w_in), ffn1_w_out=(m_ffn1_w_out, v_ffn1_w_out), w_in=(m_w_in, v_w_in),
                conv_w=(m_conv_w, v_conv_w), q_norm_g=(m_q_norm_g, v_q_norm_g), k_norm_g=(m_k_norm_g, v_k_norm_g),
                w_branch_conv=(m_w_branch_conv, v_w_branch_conv), w_branch_attn=(m_w_branch_attn, v_w_branch_attn),
                w_out=(m_w_out, v_w_out), ffn2_w_in=(m_ffn2_w_in, v_ffn2_w_in), ffn2_w_out=(m_ffn2_w_out, v_ffn2_w_out),
                final_g=(m_final_g, v_final_g))
    order = list(weights)

    T, D = x.shape[1], x.shape[2]
    Tc = ctx.shape[1]
    nctx = Tc // ROW_TILE
    nd = N_MOD * D
    Dc = conv_w.shape[2] * N_DEV
    qw, kw = N_Q_HEADS * HEAD_DIM, N_KV_HEADS * HEAD_DIM
    offs, o = {}, 0
    for nme, wd in (("cv", 3 * Dc), ("q", qw), ("gt", 2 * D), ("k", kw), ("v", kw)):
        offs[nme] = o
        o += wd

    ax, ay, ac = lax.axis_index("x"), lax.axis_index("y"), lax.axis_index("c")
    me = 4 * ax + 2 * ay + ac

    shard = {n: (jnp.swapaxes(weights[n][0], 0, 1) if n in ("ffn1_w_in", "ffn2_w_in") else weights[n][0]).astype(BF)
             for n in BIG}
    rows2d = lambda a: a.reshape(-1, a.shape[-1])
    full = {}
    mod_cols = w_mod.shape[2]
    cw_loc = conv_w[0]
    cpad = (-(D + CONV_TAPS * cw_loc.shape[1])) % 128
    pay = jnp.concatenate([c.reshape(1, D), cw_loc.reshape(1, -1), jnp.zeros((1, cpad), F32)], axis=1)
    (g_ffn1_in, g_ffn1_out), (call,) = allgather_two_level([shard["ffn1_w_in"], shard["ffn1_w_out"]], "ag_ffn1",
                                                          riders=[pay])
    full["ffn1_w_in"], full["ffn1_w_out"] = rows2d(g_ffn1_in), rows2d(g_ffn1_out)

    conv_full = call[:, 0, D:D + CONV_TAPS * cw_loc.shape[1]].reshape(N_DEV, CONV_TAPS, -1).transpose(1, 0, 2).reshape(CONV_TAPS, Dc)
    b_loc = lax.dynamic_slice_in_dim(b_mod, me * mod_cols, mod_cols, axis=1)
    cctx2 = c_ctx.reshape(1, D)
    mod_part = mod_forward(call, cctx2, w_mod[0], b_loc, "mod_fwd")
    mod_all = allgather_direct(mod_part, "ag_mod")
    mod_lat = lax.dynamic_index_in_dim(mod_all, me, axis=1, keepdims=False).reshape(nd)
    mod_ctx = mod_all[:, N_DEV, :].reshape(nd)
    mod6 = jnp.stack([mod_ctx, mod_lat]).reshape(6, 3, D)

    tabs = _rope_tables(T, Tc)

    srcs1 = (ctx[0], x[0])
    (xa1, hm1, ab1, h1, f1), (g_w_in,) = ffn_forward(
        srcs1, mod6, 0, norm1_g, full["ffn1_w_in"], full["ffn1_w_out"], nctx, "ffn1_fwd",
        hosted=Hosted(gathers=[shard["w_in"]]))
    full["w_in"] = _regroup_w_in(g_w_in, D, Dc, qw, kw)
    merge_names = ("w_branch_conv", "w_branch_attn", "w_out")
    (hx, proj, qr, kr), g_merge = proj_forward(
        xa1, mod6, norm2_g, full["w_in"], q_norm_g, k_norm_g, tabs, offs, nctx, "proj_fwd",
        hosted=Hosted(gathers=[shard[n] for n in merge_names]))
    full.update({n: rows2d(g) for n, g in zip(merge_names, g_merge)})
    yc = conv_forward(proj, conv_full, offs, Tc, "conv_fwd")
    (oa, lse), (g_ffn2_in, g_ffn2_out) = attention_forward(
        qr, kr, proj, offs, Tc, "attn_fwd", hosted=Hosted(gathers=[shard["ffn2_w_in"], shard["ffn2_w_out"]]))
    full["ffn2_w_in"], full["ffn2_w_out"] = rows2d(g_ffn2_in), rows2d(g_ffn2_out)
    x2, pc, pa, mm, zz = merge_forward(xa1, mod6, yc, oa, proj, full["w_branch_conv"], full["w_branch_attn"],
                                       full["w_out"], offs, Tc, "merge_fwd")
    (dx3, hm2, ab2, h2, f2, lacc), _ = ffn_forward((x2,), mod6, 2, norm3_g, full["ffn2_w_in"], full["ffn2_w_out"], 0, "ffn2_fwd",
                                                   final=(loss_target[0], final_g.reshape(1, D)))
    loss = lax.psum(lacc[1, 0], ("x", "y", "c"))

    by_dest = lambda g: g.reshape((N_DEV, -1, g.shape[-1]))
    (dx2, dab2, df2, acc_f2), _ = ffn_backward_rows(dx3, (x2,), mod6, 2, norm3_g, ab2, f2, full["ffn2_w_in"], full["ffn2_w_out"], 0, "ffn2_bwd")
    early = {"ffn2_w_out": tn_matmul(h2, df2, "ffn2_dwout")[0], "ffn2_w_in": tn_matmul(dab2, hm2, "ffn2_dwin")[0]}
    dproj, dgm, dpc, dpa, dyc, do, acc_mg = merge_backward_rows(dx2, mod6, zz, pc, pa, proj, full["w_branch_conv"],
                                                                full["w_branch_attn"], full["w_out"], offs, Tc, "merge_bwd")
    early["w_out"] = tn_matmul(mm, dgm, "dw_out")[0]
    early["w_branch_conv"] = tn_matmul(yc, dpc, "dw_bc")[0]
    early["w_branch_attn"] = tn_matmul(oa, dpa, "dw_ba")[0]
    dproj, dcw = conv_backward(dproj, dyc, proj, conv_full, offs, Tc, "conv_bwd")
    (dproj, dk, dv, dqg), summed = attention_backward(dproj, qr, kr, proj, oa, lse, do, q_norm_g, tabs, offs, Tc, "attn_bwd",
                                                      hosted=Hosted(scatters=[by_dest(g) for g in early.values()]))
    summed = dict(zip(early, summed))
    dproj, dkg = kv_backward(dproj, dk, dv, proj, k_norm_g, tabs, offs, "kv_bwd")
    g_w_in_grad = _ungroup_w_in_grad(tn_matmul(dproj, hx, "dw_in")[0], D, Dc, qw, kw)
    (dxa1, acc_pj), (summed["w_in"],) = proj_backward_rows(dproj, dx2, xa1, mod6, norm2_g, full["w_in"], nctx, "proj_bwd",
                                                          hosted=Hosted(scatters=[g_w_in_grad]))
    (grad_x2d, dab1, df1, acc_f1), _ = ffn_backward_rows(
        dxa1, srcs1, mod6, 0, norm1_g, ab1, f1, full["ffn1_w_in"], full["ffn1_w_out"], nctx, "ffn1_bwd")
    g_ffn1_w_in, (summed["ffn1_w_out"],) = tn_matmul(
        dab1, hm1, "ffn1_dwin", hosted=Hosted(scatters=[by_dest(tn_matmul(h1, df1, "ffn1_dwout")[0])]))
    late = ("ffn1_w_in",)
    gs = [by_dest(g_ffn1_w_in)]
    grad_x = grad_x2d[None]

    zero_d = jnp.zeros((D,), F32)
    dlat = jnp.concatenate([acc_f1[1, 0], acc_f1[1, 1], acc_f1[1, 2], acc_pj[1, 0], acc_pj[1, 1], acc_mg[0],
                            acc_f2[1, 0], acc_f2[1, 1], acc_f2[1, 2]])
    dctx = jnp.concatenate([acc_f1[0, 0], acc_f1[0, 1], acc_f1[0, 2], acc_pj[0, 0], acc_pj[0, 1]] + [zero_d] * 4)
    small = jnp.concatenate([acc_f1[0, 3] + acc_f1[1, 3], acc_pj[0, 2] + acc_pj[1, 2], acc_f2[1, 3],
                             dqg[0], dkg[0], lacc[0], dcw[0:CONV_TAPS].reshape(-1)])
    n_small = small.shape[0]
    pay_b = jnp.concatenate([dlat, dctx, small]).reshape(1, -1)
    recv_a, (gath,) = rs_sibling_exchange(gs, "rs_sibling", riders=[pay_b])
    for_my_core = lambda g: lax.dynamic_index_in_dim(g.reshape((4, 2) + g.shape[1:]), ac, axis=1, keepdims=False)
    ps = [rs_pair_sum(for_my_core(g), a, "rs_pair_sum_" + n) for n, g, a in zip(late, gs, recv_a)]
    dlat_loc = lax.dynamic_slice_in_dim(gath[:, 0, :nd], me * mod_cols, mod_cols, axis=1)
    dctx_loc = lax.dynamic_slice_in_dim(gath[:, 0, nd:2 * nd], me * mod_cols, mod_cols, axis=1)
    g_wmod, pc_part, small_sum = mod_backward(call, cctx2, w_mod[0], dlat_loc, dctx_loc, gath, 2 * nd, n_small, "mod_bwd")
    recv_b, (pcs,) = rs_chip_exchange(ps, "rs_chips", riders=[pc_part])
    my_chip = 2 * ax + ay
    reduced = {n: (lax.dynamic_index_in_dim(p, my_chip, axis=0, keepdims=False), b) for n, p, b in zip(late, ps, recv_b)}
    g_bmod, g_cctx = bmod_and_cctx_grad(gath, pcs, cctx2, nd, "small_bwd")
    sm = small_sum[0]
    g_conv_full = sm[3 * D + 2 * HEAD_DIM + D:].reshape(CONV_TAPS, Dc)
    g_conv = lax.dynamic_slice_in_dim(g_conv_full, me * cw_loc.shape[1], cw_loc.shape[1], axis=1)
    gsmall = dict(
        c_ctx=g_cctx, w_mod=g_wmod, b_mod=g_bmod, norm1_g=sm[0:D][None], norm2_g=sm[D:2 * D][None],
        norm3_g=sm[2 * D:3 * D][None], q_norm_g=sm[3 * D:3 * D + HEAD_DIM][None],
        k_norm_g=sm[3 * D + HEAD_DIM:3 * D + 2 * HEAD_DIM][None],
        final_g=sm[3 * D + 2 * HEAD_DIM:3 * D + 2 * HEAD_DIM + D][None], conv_w=g_conv)

    g_out, d_out, m_out, v_out = [], [], [], []
    flipped = ("ffn1_w_in", "w_in", "ffn2_w_in")
    for n in order:
        w = weights[n]
        shp = w.shape
        if n in flipped:
            two_d = lambda a: jnp.swapaxes(a[0], 0, 1)
            back = lambda a: jnp.swapaxes(a, 0, 1)[None]
        else:
            two_d = lambda a: a.reshape(-1, shp[-1])
            back = lambda a: a.reshape(shp)
        m, v = moms[n]
        if n in reduced:
            g2, d, nm, nv = adamw_reduced(*reduced[n], two_d(w), two_d(m), two_d(v), "adamw_" + n)
        elif n in summed:
            g2, d, nm, nv = adamw_summed(summed[n], two_d(w), two_d(m), two_d(v), "adamw_" + n)
        else:
            g2 = gsmall[n].reshape(two_d(w).shape)
            d, nm, nv = adamw(two_d(w), g2, two_d(m), two_d(v), "adamw_" + n)
        g_out.append(back(g2))
        d_out.append(back(d))
        m_out.append(back(nm))
        v_out.append(back(nv))
    return (loss, grad_x, *g_out, *d_out, *m_out, *v_out)
```

```python
import math

import jax
import jax.numpy as jnp
from jax import lax
from jax.experimental import pallas as pl
from jax.experimental.pallas import tpu as pltpu

F32 = jnp.float32
BF = jnp.bfloat16
EPS = 1e-6
N_DEV = 8
HEAD_DIM = 128
N_Q_HEADS = 8
N_KV_HEADS = 2
GROUP = N_Q_HEADS // N_KV_HEADS
GRID_W = 64
ROPE_THETA = 10000.0
CONV_TAPS = 3
N_MOD = 9
ADAM_LR = 0.001
ADAM_B1 = 0.9
ADAM_B2 = 0.999
ADAM_EPS = 1e-08
ADAM_WD = 0.01
ADAM_STEP = 10
ROW_TILE = 256
VMEM_BIG = 56 << 20
MESH_ID = pl.DeviceIdType.MESH
HIGHEST = lax.Precision.HIGHEST
NT = (((1,), (1,)), ((), ()))
TN = (((0,), (0,)), ((), ()))
LOG2E = math.log2(math.e)


def _pick(n, cands):
    for c in cands:
        if n % c == 0:
            return c
    return n


def _params(vmem=None, sem=None):
    kw = {}
    if vmem is not None:
        kw["vmem_limit_bytes"] = vmem
    if sem is not None:
        kw["dimension_semantics"] = sem
    return pltpu.CompilerParams(**kw)


def _resident(shape):
    nd = len(shape)
    return pl.BlockSpec(shape, lambda *_: (0,) * nd, pipeline_mode=pl.Buffered(1))


def _sigmoid(x):
    return jax.nn.sigmoid(x)


ANY = pl.BlockSpec(memory_space=pl.ANY)


def _coords():
    return lax.axis_index("x"), lax.axis_index("y"), lax.axis_index("c")


def _flip(v, bit):
    return 1 - v if bit else v


def _remote(src, dst, ssem, rsem, dev):
    return pltpu.make_async_remote_copy(src_ref=src, dst_ref=dst, send_sem=ssem, recv_sem=rsem,
                                        device_id=dev, device_id_type=MESH_ID)


def allgather_direct(v, name):
    def body(v_ref, out_ref, ssem, rsem, lsem):
        x, y, c = _coords()
        me = 4 * x + 2 * y + c
        mine = pltpu.make_async_copy(v_ref, out_ref.at[me], lsem)
        mine.start()
        cps = []
        for p in range(1, N_DEV):
            px, py, pc = (p >> 2) & 1, (p >> 1) & 1, p & 1
            cps.append(_remote(v_ref, out_ref.at[me], ssem.at[p - 1], rsem.at[p - 1],
                               (_flip(x, px), _flip(y, py), _flip(c, pc))))
        for cp in cps:
            cp.start()
        for p in range(1, N_DEV):
            px, py, pc = (p >> 2) & 1, (p >> 1) & 1, p & 1
            src = 4 * _flip(x, px) + 2 * _flip(y, py) + _flip(c, pc)
            _remote(v_ref, out_ref.at[src], ssem.at[p - 1], rsem.at[p - 1], (x, y, c)).wait_recv()
        for cp in cps:
            cp.wait_send()
        mine.wait()

    return pl.pallas_call(
        body, name=name,
        out_shape=jax.ShapeDtypeStruct((N_DEV,) + v.shape, v.dtype),
        in_specs=[ANY], out_specs=ANY,
        scratch_shapes=[pltpu.SemaphoreType.DMA((N_DEV - 1,)), pltpu.SemaphoreType.DMA((N_DEV - 1,)),
                        pltpu.SemaphoreType.DMA],
    )(v)


def allgather_two_level(shards, name, riders=()):
    n = len(shards)
    ride = Hosted(gathers=riders)
    r = ride.n

    def body(*refs):
        v_refs, rin, out_refs, rout = refs[:n], refs[n:n + r], refs[n + r:2 * n + r], refs[2 * n + r:2 * n + 2 * r]
        (ssem, rsem, lsem), rsems = refs[2 * n + 2 * r:2 * n + 2 * r + 3], refs[2 * n + 2 * r + 3:]
        x, y, c = _coords()
        me = (x, y, c)
        sib = (x, y, 1 - c)
        chips = [(1 - x, y), (x, 1 - y), (1 - x, 1 - y)]
        if r:
            ride.start(rin, rout, *rsems)

        def slot(w, px, py, pc):
            return out_refs[w].at[4 * px + 2 * py + pc]

        def sem(w, k):
            return ssem.at[7 * w + k], rsem.at[7 * w + k]

        mine = [pltpu.make_async_copy(v_refs[w], slot(w, *me), lsem.at[w]) for w in range(n)]
        for cp in mine:
            cp.start()
        first = []
        for w in range(n):
            first.append(_remote(v_refs[w], slot(w, *me), *sem(w, 0), sib))
            first += [_remote(v_refs[w], slot(w, *me), *sem(w, 1 + j), (*chip, c)) for j, chip in enumerate(chips)]
        for cp in first:
            cp.start()
        passed = []
        for w in range(n):
            for j, chip in enumerate(chips):
                _remote(v_refs[w], slot(w, *chip, c), *sem(w, 1 + j), me).wait_recv()
                cp = _remote(slot(w, *chip, c), slot(w, *chip, c), *sem(w, 4 + j), sib)
                cp.start()
                passed.append(cp)
        for w in range(n):
            _remote(v_refs[w], slot(w, x, y, 1 - c), *sem(w, 0), me).wait_recv()
            for j, chip in enumerate(chips):
                _remote(v_refs[w], slot(w, *chip, 1 - c), *sem(w, 4 + j), me).wait_recv()
        for cp in first + passed:
            cp.wait_send()
        for cp in mine:
            cp.wait()
        if r:
            ride.wait(rin, rout, *rsems)

    res = pl.pallas_call(
        body, name=name,
        out_shape=[jax.ShapeDtypeStruct((N_DEV,) + s.shape, s.dtype) for s in shards] + ride.out_shapes,
        in_specs=[ANY] * (n + r), out_specs=[ANY] * (n + r),
        scratch_shapes=[pltpu.SemaphoreType.DMA((7 * n,)), pltpu.SemaphoreType.DMA((7 * n,)),
                        pltpu.SemaphoreType.DMA((n,))] + (ride.scratch if r else []),
    )(*shards, *riders)
    return list(res[:n]), list(res[n:])


def _block_exchange(srcs, k, copies, name, riders):
    n = len(srcs)
    ride = Hosted(gathers=riders)
    r = ride.n

    def body(*refs):
        g_refs, rin, out_refs, rout = refs[:n], refs[n:n + r], refs[n + r:2 * n + r], refs[2 * n + r:2 * n + 2 * r]
        (ssem, rsem), rsems = refs[2 * n + 2 * r:2 * n + 2 * r + 2], refs[2 * n + 2 * r + 2:]
        if r:
            ride.start(rin, rout, *rsems)
        cps = copies(g_refs, out_refs, ssem, rsem)
        for cp in cps:
            cp.start()
        for cp in cps:
            cp.wait()
        if r:
            ride.wait(rin, rout, *rsems)

    res = pl.pallas_call(
        body, name=name,
        out_shape=[jax.ShapeDtypeStruct((k,) + g.shape[1:], g.dtype) for g in srcs] + ride.out_shapes,
        in_specs=[ANY] * (n + r), out_specs=[ANY] * (n + r),
        scratch_shapes=[pltpu.SemaphoreType.DMA((k * n,)), pltpu.SemaphoreType.DMA((k * n,))] + (ride.scratch if r else []),
    )(*srcs, *riders)
    return list(res[:n]), list(res[n:])


def rs_sibling_exchange(gs, name, riders=()):
    def copies(g_refs, out_refs, ssem, rsem):
        x, y, c = _coords()
        return [_remote(g_refs[w].at[2 * k + (1 - c)], out_refs[w].at[k], ssem.at[4 * w + k], rsem.at[4 * w + k], (x, y, 1 - c))
                for w in range(len(gs)) for k in range(4)]

    return _block_exchange(gs, 4, copies, name, riders)


def rs_chip_exchange(ps, name, riders=()):
    def copies(p_refs, out_refs, ssem, rsem):
        x, y, c = _coords()
        chips = [(1 - x, y), (x, 1 - y), (1 - x, 1 - y)]
        return [_remote(p_refs[w].at[2 * cx + cy], out_refs[w].at[j], ssem.at[3 * w + j], rsem.at[3 * w + j], (cx, cy, c))
                for w in range(len(ps)) for j, (cx, cy) in enumerate(chips)]

    return _block_exchange(ps, 3, copies, name, riders)


SEM = pl.BlockSpec(memory_space=pltpu.SEMAPHORE)
IN_HBM = pl.BlockSpec(memory_space=pltpu.HBM)
DATAFLOW = pltpu.SideEffectType.DATAFLOW_SIDE_EFFECTING


def _scatter_descriptors(src_ref, land_ref, ssem, rsem, arrivals):
    x, y, c = _coords()
    me = 4 * x + 2 * y + c
    cps = []
    for p in range(1, N_DEV):
        px, py, pc = _flip(x, (p >> 2) & 1), _flip(y, (p >> 1) & 1), _flip(c, p & 1)
        peer = 4 * px + 2 * py + pc
        if arrivals:
            cps.append(_remote(src_ref.at[me], land_ref.at[peer], ssem.at[p - 1], rsem.at[p - 1], (x, y, c)))
        else:
            cps.append(_remote(src_ref.at[peer], land_ref.at[me], ssem.at[p - 1], rsem.at[p - 1], (px, py, pc)))
    return cps


def scatter_start(g, name):
    def body(g_ref, land_ref, ssem, rsem, g_thru, land_thru, token):
        for cp in _scatter_descriptors(g_ref, land_ref, ssem, rsem, False):
            cp.start()
        token[...] = jnp.zeros_like(token)

    return pl.pallas_call(
        body, name=name,
        out_shape=(pltpu.SemaphoreType.DMA((N_DEV - 1,)), pltpu.SemaphoreType.DMA((N_DEV - 1,)),
                   pltpu.HBM(g.shape, g.dtype), pltpu.HBM(g.shape, g.dtype), jax.ShapeDtypeStruct((8, 128), F32)),
        in_specs=(IN_HBM, IN_HBM), out_specs=(SEM, SEM, IN_HBM, IN_HBM, pl.BlockSpec(memory_space=pltpu.VMEM)),
        input_output_aliases={0: 2, 1: 3},
        compiler_params=pltpu.CompilerParams(has_side_effects=DATAFLOW),
    )(pltpu.with_memory_space_constraint(g, pltpu.HBM),
      pltpu.with_memory_space_constraint(lax.empty(g.shape, g.dtype), pltpu.HBM))


def scatter_wait(ssem, rsem, g_thru, land_thru, after, name):
    def body(g_ref, land_ref, ssem_ref, rsem_ref, after_ref, g_out, land_out):
        cps = _scatter_descriptors(g_ref, land_ref, ssem_ref, rsem_ref, True)
        for cp in cps:
            cp.wait_send()
        for cp in cps:
            cp.wait_recv()

    return pl.pallas_call(
        body, name=name,
        out_shape=(pltpu.HBM(g_thru.shape, g_thru.dtype), pltpu.HBM(land_thru.shape, land_thru.dtype)),
        in_specs=(IN_HBM, IN_HBM, SEM, SEM, ANY), out_specs=(IN_HBM, IN_HBM),
        input_output_aliases={0: 0, 1: 1},
        compiler_params=pltpu.CompilerParams(has_side_effects=DATAFLOW),
    )(g_thru, land_thru, ssem, rsem, after)


class Hosted:
    def __init__(self, gathers=(), scatters=()):
        self.items = [(a, False) for a in gathers] + [(a, True) for a in scatters]
        self.n = len(self.items)
        self.operands = [a for a, _ in self.items]
        self.out_shapes = [jax.ShapeDtypeStruct(a.shape if sc else (N_DEV,) + a.shape, a.dtype) for a, sc in self.items]
        self.scratch = [pltpu.SemaphoreType.DMA((7 * self.n,)), pltpu.SemaphoreType.DMA((7 * self.n,)),
                        pltpu.SemaphoreType.DMA((self.n,))]

    def _copies(self, in_refs, out_refs, ssem, rsem, lsem, arrivals):
        x, y, c = _coords()
        me = 4 * x + 2 * y + c
        remote, local = [], []
        for w, (_, sc) in enumerate(self.items):
            src, dst = in_refs[w], out_refs[w]
            local.append(pltpu.make_async_copy(src.at[me] if sc else src, dst.at[me], lsem.at[w]))
            for p in range(1, N_DEV):
                px, py, pc = _flip(x, (p >> 2) & 1), _flip(y, (p >> 1) & 1), _flip(c, p & 1)
                peer = 4 * px + 2 * py + pc
                k = 7 * w + p - 1
                if arrivals:
                    remote.append(_remote(src.at[me] if sc else src, dst.at[peer], ssem.at[k], rsem.at[k], (x, y, c)))
                else:
                    remote.append(_remote(src.at[peer] if sc else src, dst.at[me], ssem.at[k], rsem.at[k], (px, py, pc)))
        return remote, local

    def start(self, in_refs, out_refs, ssem, rsem, lsem):
        sends, local = self._copies(in_refs, out_refs, ssem, rsem, lsem, False)
        for cp in local + sends:
            cp.start()

    def wait(self, in_refs, out_refs, ssem, rsem, lsem):
        arrivals, local = self._copies(in_refs, out_refs, ssem, rsem, lsem, True)
        for cp in arrivals:
            cp.wait_recv()
        for cp in arrivals:
            cp.wait_send()
        for cp in local:
            cp.wait()


def _call(body, *, name, grid, in_specs, out_specs, out_shape, operands, params, scratch_shapes=(), aliases=None, hosted=None):
    n_in, n_out, n_scr = len(in_specs), len(out_specs), len(scratch_shapes)
    h = hosted.n if hosted is not None else 0

    def wrapped(*refs):
        ins, cins = refs[:n_in], refs[n_in:n_in + h]
        outs, couts = refs[n_in + h:n_in + h + n_out], refs[n_in + h + n_out:n_in + 2 * h + n_out]
        rest = refs[n_in + 2 * h + n_out:]
        scr, sems = rest[:n_scr], rest[n_scr:]
        if h:
            ids = [pl.program_id(a) for a in range(len(grid))]
            first, last = ids[0] == 0, ids[0] == grid[0] - 1
            for a in range(1, len(grid)):
                first, last = first & (ids[a] == 0), last & (ids[a] == grid[a] - 1)

            @pl.when(first)
            def _():
                hosted.start(cins, couts, *sems)

        body(*ins, *outs, *scr)
        if h:
            @pl.when(last)
            def _():
                hosted.wait(cins, couts, *sems)

    res = pl.pallas_call(
        wrapped, name=name, grid=grid,
        in_specs=list(in_specs) + [ANY] * h, out_specs=list(out_specs) + [ANY] * h,
        out_shape=list(out_shape) + (hosted.out_shapes if h else []),
        scratch_shapes=list(scratch_shapes) + (hosted.scratch if h else []),
        input_output_aliases=aliases or {}, compiler_params=params,
    )(*operands, *(hosted.operands if h else []))
    return list(res[:n_out]), list(res[n_out:])


def _row_tile(R, C):
    if R * C <= (1 << 18):
        return R
    return max((d for d in range(8, 257, 8) if R % d == 0), default=R)


def rs_pair_sum(mine, recv_a, name):
    _, R, C = mine.shape
    tr = _row_tile(R, C)

    def body(g_ref, a_ref, o_ref):
        o_ref[0] = (g_ref[0].astype(F32) + a_ref[0].astype(F32)).astype(o_ref.dtype)

    blk = pl.BlockSpec((1, tr, C), lambda k, r: (k, r, 0))
    return pl.pallas_call(
        body, name=name, grid=(4, R // tr),
        in_specs=[blk, blk], out_specs=blk,
        out_shape=jax.ShapeDtypeStruct((4, R, C), BF),
    )(mine, recv_a)


def _cond_rows(call_ref, cctx_ref, z_ref, D):
    z_ref[...] = jnp.zeros_like(z_ref)
    for a in range(N_DEV):
        z_ref[a:a + 1, :] = call_ref[a][:, :D]
    z_ref[N_DEV:N_DEV + 1, :] = cctx_ref[...]


def mod_forward(call, c_ctx, w_loc, b_loc, name):
    D, cols = w_loc.shape

    def body(call_ref, cctx_ref, w_ref, b_ref, o_ref, z_ref):
        _cond_rows(call_ref, cctx_ref, z_ref, D)
        z = z_ref[...]
        s = z * _sigmoid(z)
        o_ref[...] = jnp.dot(s, w_ref[...], preferred_element_type=F32, precision=HIGHEST) + b_ref[...]

    return pl.pallas_call(
        body, name=name, out_shape=jax.ShapeDtypeStruct((16, cols), F32),
        scratch_shapes=[pltpu.VMEM((16, D), F32)],
        compiler_params=_params(vmem=VMEM_BIG),
    )(call, c_ctx, w_loc, b_loc)


def mod_backward(call, c_ctx, w_loc, dlat_loc, dctx_loc, gath, n_small_off, n_small, name):
    D, cols = w_loc.shape

    def body(call_ref, cctx_ref, w_ref, dlat_ref, dctx_ref, g_ref, gw_ref, pc_ref, small_ref, z_ref, dm_ref):
        _cond_rows(call_ref, cctx_ref, z_ref, D)
        z = z_ref[...]
        s = z * _sigmoid(z)
        dctx = dctx_ref[0:1, :]
        for a in range(1, N_DEV):
            dctx = dctx + dctx_ref[a:a + 1, :]
        dm_ref[...] = jnp.zeros_like(dm_ref)
        dm_ref[0:N_DEV, :] = dlat_ref[...]
        dm_ref[N_DEV:N_DEV + 1, :] = dctx
        gw_ref[...] = lax.dot_general(s, dm_ref[...], TN, preferred_element_type=F32, precision=HIGHEST)
        pc_ref[...] = lax.dot_general(dctx, w_ref[...], NT, preferred_element_type=F32, precision=HIGHEST)
        acc = g_ref[0][:, n_small_off:n_small_off + n_small]
        for a in range(1, N_DEV):
            acc = acc + g_ref[a][:, n_small_off:n_small_off + n_small]
        small_ref[...] = acc

    return pl.pallas_call(
        body, name=name,
        out_shape=(jax.ShapeDtypeStruct((D, cols), F32), jax.ShapeDtypeStruct((1, D), F32),
                   jax.ShapeDtypeStruct((1, n_small), F32)),
        scratch_shapes=[pltpu.VMEM((16, D), F32), pltpu.VMEM((16, cols), F32)],
        compiler_params=_params(vmem=VMEM_BIG),
    )(call, c_ctx, w_loc, dlat_loc, dctx_loc, gath)


def bmod_and_cctx_grad(gath, pcs, c_ctx, nd, name):
    D = c_ctx.shape[-1]

    def body(g_ref, pc_ref, cctx_ref, gb_ref, gc_ref):
        acc = g_ref[0][:, :nd] + g_ref[0][:, nd:2 * nd]
        for a in range(1, N_DEV):
            acc = acc + (g_ref[a][:, :nd] + g_ref[a][:, nd:2 * nd])
        gb_ref[...] = acc
        p = pc_ref[0]
        for a in range(1, N_DEV):
            p = p + pc_ref[a]
        z = cctx_ref[...]
        sg = _sigmoid(z)
        gc_ref[...] = p * (sg * (1.0 + z * (1.0 - sg)))

    return pl.pallas_call(
        body, name=name,
        out_shape=(jax.ShapeDtypeStruct((1, nd), F32), jax.ShapeDtypeStruct((1, D), F32)),
    )(gath, pcs, c_ctx)


def _mod_spec(D, which, nctx):
    return pl.BlockSpec((1, 3, D), lambda i: (jnp.where(i < nctx, 0, 3) + which, 0, 0))


def _acc_spec(D, nctx):
    return pl.BlockSpec((1, 8, D), lambda i: (jnp.where(i < nctx, 0, 1), 0, 0))


def _row(tm, n):
    return pl.BlockSpec((tm, n), lambda i: (i, 0))


def _two_stream_specs(tm, D, nctx):
    return [pl.BlockSpec((tm, D), lambda i: (jnp.minimum(i, nctx - 1), 0)),
            pl.BlockSpec((tm, D), lambda i: (jnp.maximum(i - nctx, 0), 0))]


def _final_norm_loss_backward(x, tgt, gg, i, dx_ref, acc_ref):
    @pl.when(i == 0)
    def _():
        acc_ref[...] = jnp.zeros_like(acc_ref)

    D = x.shape[1]
    r = lax.rsqrt(jnp.mean(x * x, axis=-1, keepdims=True) + EPS)
    xh = x * r
    e = xh * gg - tgt
    part = 0.5 * jnp.sum(jnp.mean(e * e, axis=-1, keepdims=True), axis=0, keepdims=True)
    dy = e * (1.0 / D)
    dyg = dy * gg
    dx_ref[...] = r * (dyg - xh * jnp.mean(dyg * xh, axis=-1, keepdims=True))
    acc_ref[0:1, :] += jnp.sum(dy * xh, axis=0, keepdims=True)
    acc_ref[1:2, :] += jnp.broadcast_to(part, (1, D))


def _hidden_chunks(F):
    step = 1024 if F % 256 == 0 else F
    return [(lo, min(lo + step, F)) for lo in range(0, F, step)]


def ffn_forward(srcs, mod6, which, g, wt, w_out, nctx, name, hosted=None, final=None):
    D = srcs[-1].shape[1]
    Tr = sum(s.shape[0] for s in srcs)
    F = wt.shape[0] // 2
    tm = ROW_TILE
    two = len(srcs) == 2
    nfin = 0 if final is None else 2

    def body(*refs):
        x_refs, fin_refs, rest = refs[:len(srcs)], refs[len(srcs):len(srcs) + nfin], refs[len(srcs) + nfin:]
        mod_ref, g_ref, wt_ref, wout_ref, xo_ref, hm_ref, ab_ref, h_ref, f_ref = rest[:9]
        x = jnp.where(pl.program_id(0) < nctx, x_refs[0][...], x_refs[1][...]) if two else x_refs[0][...]
        ms = mod_ref[0]
        shift, scale, gate = ms[0:1], ms[1:2], ms[2:3]
        r = lax.rsqrt(jnp.mean(x * x, axis=-1, keepdims=True) + EPS)
        hb = (((x * r) * g_ref[...]) * (1.0 + scale) + shift).astype(BF)
        hm_ref[...] = hb
        f = jnp.zeros((tm, D), F32)
        for lo, hi in _hidden_chunks(F):
            a = lax.dot_general(hb, wt_ref[lo:hi, :], NT, preferred_element_type=F32)
            b = lax.dot_general(hb, wt_ref[F + lo:F + hi, :], NT, preferred_element_type=F32)
            ab_ref[:, lo:hi] = a.astype(BF)
            ab_ref[:, F + lo:F + hi] = b.astype(BF)
            h = ((a * _sigmoid(a)) * b).astype(BF)
            h_ref[:, lo:hi] = h
            f = f + jnp.dot(h, wout_ref[lo:hi, :], preferred_element_type=F32)
        f_ref[...] = f.astype(BF)
        xo = x + (0.5 * gate) * f
        if final is None:
            xo_ref[...] = xo
        else:
            _final_norm_loss_backward(xo, fin_refs[0][...], fin_refs[1][...], pl.program_id(0), xo_ref, rest[9])

    src_specs = _two_stream_specs(tm, D, nctx) if two else [_row(tm, D)]
    fin = final is not None
    return _call(
        body, name=name, grid=(Tr // tm,),
        in_specs=src_specs + ([_row(tm, D), _resident((1, D))] if fin else [])
                 + [_mod_spec(D, which, nctx), _resident((1, D)), _resident(wt.shape), _resident(w_out.shape)],
        out_specs=[_row(tm, D), _row(tm, D), _row(tm, 2 * F), _row(tm, F), _row(tm, D)]
                  + ([pl.BlockSpec((8, D), lambda i: (0, 0))] if fin else []),
        out_shape=[jax.ShapeDtypeStruct((Tr, D), F32), jax.ShapeDtypeStruct((Tr, D), BF),
                   jax.ShapeDtypeStruct((Tr, 2 * F), BF), jax.ShapeDtypeStruct((Tr, F), BF),
                   jax.ShapeDtypeStruct((Tr, D), BF)] + ([jax.ShapeDtypeStruct((8, D), F32)] if fin else []),
        operands=[*srcs, *(final or ()), mod6, g, wt, w_out], hosted=hosted,
        params=_params(vmem=VMEM_BIG, sem=("arbitrary",)))


def ffn_backward_rows(dxo, srcs, mod6, which, g, ab, fo, wt, w_out, nctx, name, hosted=None):
    D = srcs[-1].shape[1]
    Tr = sum(s.shape[0] for s in srcs)
    Tl = srcs[-1].shape[0]
    F = wt.shape[0] // 2
    tm = ROW_TILE
    two = len(srcs) == 2

    def body(*refs):
        dxo_ref, x_refs = refs[0], refs[1:1 + len(srcs)]
        mod_ref, g_ref, ab_ref, fo_ref, wt_ref, wout_ref, dx_ref, dab_ref, df_ref, acc_ref = refs[1 + len(srcs):]
        i = pl.program_id(0)

        @pl.when((i == 0) | (i == nctx))
        def _():
            acc_ref[...] = jnp.zeros_like(acc_ref)

        dxo = dxo_ref[...]
        x = jnp.where(i < nctx, x_refs[0][...], x_refs[1][...]) if two else x_refs[0][...]
        ms = mod_ref[0]
        scale, gate = ms[1:2], ms[2:3]
        gg = g_ref[...]
        dgate = jnp.sum(dxo * fo_ref[...].astype(F32), axis=0, keepdims=True) * 0.5
        dfb = (dxo * (0.5 * gate)).astype(BF)
        df_ref[...] = dfb
        dhm = jnp.zeros((tm, D), F32)
        for lo, hi in _hidden_chunks(F):
            dh = lax.dot_general(dfb, wout_ref[lo:hi, :], NT, preferred_element_type=F32)
            a = ab_ref[:, lo:hi].astype(F32)
            b = ab_ref[:, F + lo:F + hi].astype(F32)
            sg = _sigmoid(a)
            da = ((dh * b) * (sg * (1.0 + a * (1.0 - sg)))).astype(BF)
            db = (dh * (a * sg)).astype(BF)
            dab_ref[:, lo:hi] = da
            dab_ref[:, F + lo:F + hi] = db
            dhm = dhm + jnp.dot(da, wt_ref[lo:hi, :], preferred_element_type=F32)
            dhm = dhm + jnp.dot(db, wt_ref[F + lo:F + hi, :], preferred_element_type=F32)
        r = lax.rsqrt(jnp.mean(x * x, axis=-1, keepdims=True) + EPS)
        xh = x * r
        dshift = jnp.sum(dhm, axis=0, keepdims=True)
        dscale = jnp.sum(dhm * (xh * gg), axis=0, keepdims=True)
        dxh_g = dhm * (1.0 + scale)
        dg = jnp.sum(dxh_g * xh, axis=0, keepdims=True)
        dxh = dxh_g * gg
        dx_ref[...] = dxo + r * (dxh - xh * jnp.mean(dxh * xh, axis=-1, keepdims=True))
        for k, val in enumerate((dshift, dscale, dgate, dg)):
            acc_ref[0, k:k + 1, :] += val

    src_specs = _two_stream_specs(tm, D, nctx) if two else [_row(tm, D)]
    dx_spec = pl.BlockSpec((tm, D), lambda i: (jnp.maximum(i - nctx, 0), 0))
    return _call(
        body, name=name, grid=(Tr // tm,),
        in_specs=[_row(tm, D)] + src_specs + [_mod_spec(D, which, nctx), _resident((1, D)), _row(tm, 2 * F), _row(tm, D),
                                              _resident(wt.shape), _resident(w_out.shape)],
        out_specs=[dx_spec, _row(tm, 2 * F), _row(tm, D), _acc_spec(D, nctx)],
        out_shape=[jax.ShapeDtypeStruct((Tl, D), F32), jax.ShapeDtypeStruct((Tr, 2 * F), BF),
                   jax.ShapeDtypeStruct((Tr, D), BF), jax.ShapeDtypeStruct((2, 8, D), F32)],
        operands=[dxo, *srcs, mod6, g, ab, fo, wt, w_out], hosted=hosted,
        params=_params(vmem=VMEM_BIG, sem=("arbitrary",)))


def _token_tile(T):
    return _pick(T, (2048, 1408, 1024, 768, 512, 256, 128))


def tn_matmul(a, b, name, hosted=None):
    T, K = a.shape
    N = b.shape[1]
    tk = _pick(K, (1024, 1408, 1664, 768, 512, 384, 256, 128))
    tn = _pick(N, (1024, 1408, 1664, 768, 512, 384, 256, 128))
    tt = _token_tile(T)
    nt = T // tt

    def body(a_ref, b_ref, o_ref, acc_ref):
        t = pl.program_id(2)

        @pl.when(t == 0)
        def _():
            acc_ref[...] = jnp.zeros_like(acc_ref)

        acc_ref[...] += lax.dot_general(a_ref[...], b_ref[...], TN, preferred_element_type=F32)

        @pl.when(t == nt - 1)
        def _():
            o_ref[...] = acc_ref[...].astype(BF)

    (out,), exchanged = _call(
        body, name=name, grid=(K // tk, N // tn, nt),
        in_specs=[pl.BlockSpec((tt, tk), lambda k, n, t: (t, k)), pl.BlockSpec((tt, tn), lambda k, n, t: (t, n))],
        out_specs=[pl.BlockSpec((tk, tn), lambda k, n, t: (k, n))],
        out_shape=[jax.ShapeDtypeStruct((K, N), BF)],
        scratch_shapes=[pltpu.VMEM((tk, tn), F32)],
        operands=[a, b], hosted=hosted,
        params=_params(vmem=VMEM_BIG, sem=("arbitrary", "arbitrary", "arbitrary")))
    return out, exchanged


def _rope_apply(y, cos, s_next, s_prev):
    return y * cos + pltpu.roll(y, HEAD_DIM - 32, 1) * s_next + pltpu.roll(y, 32, 1) * s_prev


def _rope_transpose(dz, cos, s_next, s_prev):
    return dz * cos + pltpu.roll(dz * s_next, 32, 1) + pltpu.roll(dz * s_prev, HEAD_DIM - 32, 1)


def proj_forward(xa, mod6, g, w_in, qg, kg, tabs, offs, nctx, name, hosted=None):
    Tr, D = xa.shape
    P = w_in.shape[1]
    tm = ROW_TILE
    qo, ko = offs["q"], offs["k"]
    qw, kw = N_Q_HEADS * HEAD_DIM, N_KV_HEADS * HEAD_DIM
    scale_q = HEAD_DIM ** -0.5 * LOG2E

    def body(x_ref, mod_ref, g_ref, w_ref, qg_ref, kg_ref, tab_ref, hx_ref, pr_ref, q_ref, k_ref):
        x = x_ref[...]
        ms = mod_ref[0]
        shift, scale = ms[0:1], ms[1:2]
        r = lax.rsqrt(jnp.mean(x * x, axis=-1, keepdims=True) + EPS)
        hb = (((x * r) * g_ref[...]) * (1.0 + scale) + shift).astype(BF)
        hx_ref[...] = hb
        pr = jnp.dot(hb, w_ref[...], preferred_element_type=F32)
        pr_ref[...] = pr.astype(BF)
        cos, s_next, s_prev = tab_ref[0], tab_ref[1], tab_ref[2]

        def head(v, gain):
            n = v * lax.rsqrt(jnp.mean(v * v, axis=-1, keepdims=True) + EPS)
            return _rope_apply(n * gain, cos, s_next, s_prev)

        for h in range(N_Q_HEADS):
            lo = qo + h * HEAD_DIM
            q_ref[:, h * HEAD_DIM:(h + 1) * HEAD_DIM] = (head(pr[:, lo:lo + HEAD_DIM], qg_ref[...]) * scale_q).astype(BF)
        for h in range(N_KV_HEADS):
            lo = ko + h * HEAD_DIM
            k_ref[:, h * HEAD_DIM:(h + 1) * HEAD_DIM] = head(pr[:, lo:lo + HEAD_DIM], kg_ref[...]).astype(BF)

    return _call(
        body, name=name, grid=(Tr // tm,),
        in_specs=[_row(tm, D), _mod_spec(D, 1, nctx), _resident((1, D)), _resident(w_in.shape),
                  _resident((1, HEAD_DIM)), _resident((1, HEAD_DIM)),
                  pl.BlockSpec((3, tm, HEAD_DIM), lambda i: (0, i, 0))],
        out_specs=[_row(tm, D), _row(tm, P), _row(tm, qw), _row(tm, kw)],
        out_shape=[jax.ShapeDtypeStruct((Tr, D), BF), jax.ShapeDtypeStruct((Tr, P), BF),
                   jax.ShapeDtypeStruct((Tr, qw), BF), jax.ShapeDtypeStruct((Tr, kw), BF)],
        operands=[xa, mod6, g, w_in, qg, kg, tabs], hosted=hosted,
        params=_params(vmem=VMEM_BIG, sem=("arbitrary",)))


def _shifted(u, first_row, last_row):
    T = u.shape[0]
    prev = jnp.where(first_row, 0.0, pltpu.roll(u, 1, 0))
    nxt = jnp.where(last_row, 0.0, pltpu.roll(u, T - 1, 0))
    return prev, nxt


def conv_forward(proj, conv_w, offs, Tc, name):
    Ta = proj.shape[0]
    T = Ta - Tc
    Dc = conv_w.shape[1]
    cb = offs["cv"] // 384

    def body(p_ref, w_ref, y_ref):
        rows = lax.broadcasted_iota(jnp.int32, (T, 128), 0)
        u = p_ref[pl.ds(Tc, T), 128:256].astype(F32) * p_ref[pl.ds(Tc, T), 256:384].astype(F32)
        prev, nxt = _shifted(u, rows == 0, rows == T - 1)
        w = w_ref[...]
        cv = prev * w[0:1] + u * w[1:2] + nxt * w[2:3]
        y_ref[...] = (p_ref[pl.ds(Tc, T), 0:128].astype(F32) * cv).astype(BF)

    return pl.pallas_call(
        body, name=name, grid=(Dc // 128,),
        in_specs=[pl.BlockSpec((Ta, 384), lambda j: (0, cb + j)), pl.BlockSpec((CONV_TAPS, 128), lambda j: (0, j))],
        out_specs=pl.BlockSpec((T, 128), lambda j: (0, j)),
        out_shape=jax.ShapeDtypeStruct((T, Dc), BF),
        compiler_params=_params(vmem=VMEM_BIG, sem=("arbitrary",)),
    )(proj, conv_w)


def conv_backward(dproj, dy, proj, conv_w, offs, Tc, name):
    Ta = proj.shape[0]
    T = Ta - Tc
    Dc = conv_w.shape[1]
    cb = offs["cv"] // 384

    def body(dp_any, dy_ref, p_ref, w_ref, o_ref, dw_ref):
        rows = lax.broadcasted_iota(jnp.int32, (T, 128), 0)
        first, last = rows == 0, rows == T - 1
        bg = p_ref[pl.ds(Tc, T), 0:128].astype(F32)
        cg = p_ref[pl.ds(Tc, T), 128:256].astype(F32)
        vc = p_ref[pl.ds(Tc, T), 256:384].astype(F32)
        dy = dy_ref[...].astype(F32)
        u = cg * vc
        prev, nxt = _shifted(u, first, last)
        w = w_ref[...]
        cv = prev * w[0:1] + u * w[1:2] + nxt * w[2:3]
        o_ref[pl.ds(0, Tc), :] = jnp.zeros((Tc, 384), BF)
        o_ref[pl.ds(Tc, T), 0:128] = (dy * cv).astype(BF)
        dcv = dy * bg
        dprev, dnxt = _shifted(dcv, first, last)
        du = dnxt * w[0:1] + dcv * w[1:2] + dprev * w[2:3]
        o_ref[pl.ds(Tc, T), 128:256] = (du * vc).astype(BF)
        o_ref[pl.ds(Tc, T), 256:384] = (du * cg).astype(BF)
        dw_ref[...] = jnp.zeros_like(dw_ref)
        for k, tap in enumerate((prev, u, nxt)):
            dw_ref[k:k + 1, :] = jnp.sum(dcv * tap, axis=0, keepdims=True)

    blk = pl.BlockSpec((Ta, 384), lambda j: (0, cb + j))
    return pl.pallas_call(
        body, name=name, grid=(Dc // 128,),
        in_specs=[ANY, pl.BlockSpec((T, 128), lambda j: (0, j)), blk, pl.BlockSpec((CONV_TAPS, 128), lambda j: (0, j))],
        out_specs=[blk, pl.BlockSpec((8, 128), lambda j: (0, j))],
        out_shape=[jax.ShapeDtypeStruct(dproj.shape, BF), jax.ShapeDtypeStruct((8, Dc), F32)],
        input_output_aliases={0: 0},
        compiler_params=_params(vmem=VMEM_BIG, sem=("arbitrary",)),
    )(dproj, dy, proj, conv_w)


def _kv_chunk(Ta):
    return _pick(Ta, (768, 512, 384, 256, 128))


def _stack_heads(v):
    return jnp.concatenate([v[:, h * HEAD_DIM:(h + 1) * HEAD_DIM] for h in range(GROUP)], axis=0)


def attention_forward(q, k, proj, offs, Tc, name, hosted=None):
    Ta = k.shape[0]
    T = Ta - Tc
    tq = ROW_TILE
    kc = _kv_chunk(Ta)
    nkv = Ta // kc
    gw = GROUP * HEAD_DIM
    vblk = offs["v"] // HEAD_DIM
    qoff = Tc // tq
    n = GROUP * tq

    def body(q_ref, k_ref, v_ref, o_ref, lse_ref, vx_ref, qs_ref, s0_ref, s1_ref, m_ref, acc_ref):
        @pl.when(pl.program_id(1) == 0)
        def _():
            vx_ref[:, 0:HEAD_DIM] = v_ref[...]
            vx_ref[:, HEAD_DIM:2 * HEAD_DIM] = jnp.ones((Ta, HEAD_DIM), BF)

        qs_ref[...] = _stack_heads(q_ref[...])
        m_ref[...] = jnp.full((n, 1), -1e30, F32)
        acc_ref[...] = jnp.zeros((n, 2 * HEAD_DIM), F32)

        def rows(c):
            return pl.ds(pl.multiple_of(c * kc, kc), kc)

        def logits(c, dst):
            dst[...] = lax.dot_general(qs_ref[...], k_ref[rows(c), :], NT, preferred_element_type=F32)

        def consume(src, c):
            s = src[...]
            m_prev = m_ref[...]
            m_new = jnp.maximum(m_prev, jnp.max(s, axis=-1, keepdims=True))
            p = jnp.exp2(s - m_new).astype(BF)
            acc_ref[...] = jnp.exp2(m_prev - m_new) * acc_ref[...] + jnp.dot(p, vx_ref[rows(c), :], preferred_element_type=F32)
            m_ref[...] = m_new

        def pair(i, carry):
            logits(2 * i + 1, s1_ref)
            consume(s0_ref, 2 * i)
            logits(2 * i + 2, s0_ref)
            consume(s1_ref, 2 * i + 1)
            return carry

        logits(0, s0_ref)
        if nkv % 2:
            lax.fori_loop(0, nkv // 2, pair, 0)
            consume(s0_ref, nkv - 1)
        else:
            lax.fori_loop(0, nkv // 2 - 1, pair, 0)
            logits(nkv - 1, s1_ref)
            consume(s0_ref, nkv - 2)
            consume(s1_ref, nkv - 1)
        acc = acc_ref[...]
        l = acc[:, HEAD_DIM:HEAD_DIM + 1]
        o = acc[:, 0:HEAD_DIM] / l
        lse = m_ref[...] + jnp.log2(l)
        for h in range(GROUP):
            o_ref[:, h * HEAD_DIM:(h + 1) * HEAD_DIM] = o[h * tq:(h + 1) * tq].astype(BF)
            lse_ref[0, :, h:h + 1] = lse[h * tq:(h + 1) * tq]

    return _call(
        body, name=name, grid=(N_KV_HEADS, T // tq),
        in_specs=[pl.BlockSpec((tq, gw), lambda j, i: (i + qoff, j)),
                  pl.BlockSpec((Ta, HEAD_DIM), lambda j, i: (0, j)),
                  pl.BlockSpec((Ta, HEAD_DIM), lambda j, i: (0, vblk + j))],
        out_specs=[pl.BlockSpec((tq, gw), lambda j, i: (i, j)),
                   pl.BlockSpec((1, tq, GROUP), lambda j, i: (j, i, 0))],
        out_shape=[jax.ShapeDtypeStruct((T, N_Q_HEADS * HEAD_DIM), BF),
                   jax.ShapeDtypeStruct((N_KV_HEADS, T, GROUP), F32)],
        scratch_shapes=[pltpu.VMEM((Ta, 2 * HEAD_DIM), BF), pltpu.VMEM((n, HEAD_DIM), BF), pltpu.VMEM((n, kc), F32),
                        pltpu.VMEM((n, kc), F32), pltpu.VMEM((n, 1), F32), pltpu.VMEM((n, 2 * HEAD_DIM), F32)],
        operands=[q, k, proj], hosted=hosted,
        params=_params(vmem=VMEM_BIG, sem=("arbitrary", "arbitrary")))


def _norm_rope_backward(dz, raw, gg, cos, s_next, s_prev):
    r = lax.rsqrt(jnp.mean(raw * raw, axis=-1, keepdims=True) + EPS)
    n = raw * r
    dy = _rope_transpose(dz, cos, s_next, s_prev)
    dn = dy * gg
    return r * (dn - n * jnp.mean(dn * n, axis=-1, keepdims=True)), jnp.sum(dy * n, axis=0, keepdims=True)


def attention_backward(dproj, q, k, proj, o, lse, do, qgain, tabs, offs, Tc, name, hosted=None):
    Ta = k.shape[0]
    tq = ROW_TILE
    nctx = Tc // tq
    kc = _kv_chunk(Ta)
    gw = GROUP * HEAD_DIM
    vblk = offs["v"] // HEAD_DIM
    qblk = offs["q"] // gw
    zscale = HEAD_DIM ** -0.5

    def body(dp_any, q_ref, k_ref, v_ref, o_ref, lse_ref, do_ref, raw_ref, g_ref, tab_ref, dqr_ref, dk_ref, dv_ref, dg_ref):
        j, i = pl.program_id(0), pl.program_id(1)

        @pl.when(i == 0)
        def _():
            dk_ref[...] = jnp.zeros_like(dk_ref)
            dv_ref[...] = jnp.zeros_like(dv_ref)

        @pl.when((i == 0) & (j == 0))
        def _():
            dg_ref[...] = jnp.zeros_like(dg_ref)

        @pl.when(i < nctx)
        def _():
            dqr_ref[...] = jnp.zeros_like(dqr_ref)

        @pl.when(i >= nctx)
        def _():
            qs = _stack_heads(q_ref[...])
            dob = do_ref[...]
            dos = _stack_heads(dob)
            delta = jnp.concatenate(
                [jnp.sum(dob[:, h * HEAD_DIM:(h + 1) * HEAD_DIM].astype(F32)
                         * o_ref[:, h * HEAD_DIM:(h + 1) * HEAD_DIM].astype(F32), axis=-1, keepdims=True)
                 for h in range(GROUP)], axis=0)
            lse = jnp.concatenate([lse_ref[0, :, h:h + 1] for h in range(GROUP)], axis=0)

            def step(c, dq):
                rows = pl.ds(pl.multiple_of(c * kc, kc), kc)
                kk = k_ref[rows, :]
                vv = v_ref[rows, :]
                s = lax.dot_general(qs, kk, NT, preferred_element_type=F32)
                p = jnp.exp2(s - lse)
                dp = lax.dot_general(dos, vv, NT, preferred_element_type=F32)
                ds = (p * (dp - delta)).astype(BF)
                dv_ref[rows, :] += lax.dot_general(p.astype(BF), dos, TN, preferred_element_type=F32)
                dk_ref[rows, :] += lax.dot_general(ds, qs, TN, preferred_element_type=F32)
                return dq + jnp.dot(ds, kk, preferred_element_type=F32)

            dq = lax.fori_loop(0, Ta // kc, step, jnp.zeros((GROUP * tq, HEAD_DIM), F32))
            per_head = lambda t: jnp.concatenate([t] * GROUP, axis=0)
            dr, dg = _norm_rope_backward(dq * zscale, _stack_heads(raw_ref[...]).astype(F32), g_ref[...],
                                         per_head(tab_ref[0]), per_head(tab_ref[1]), per_head(tab_ref[2]))
            for h in range(GROUP):
                dqr_ref[:, h * HEAD_DIM:(h + 1) * HEAD_DIM] = dr[h * tq:(h + 1) * tq].astype(BF)
            dg_ref[0:1, :] += dg

    lat = lambda j, i: (jnp.maximum(i - nctx, 0), j)
    (dproj, dk, dv, dqg), exchanged = _call(
        body, name=name, grid=(N_KV_HEADS, Ta // tq),
        in_specs=[ANY, pl.BlockSpec((tq, gw), lambda j, i: (i, j)),
                  pl.BlockSpec((Ta, HEAD_DIM), lambda j, i: (0, j)),
                  pl.BlockSpec((Ta, HEAD_DIM), lambda j, i: (0, vblk + j)),
                  pl.BlockSpec((tq, gw), lat),
                  pl.BlockSpec((1, tq, GROUP), lambda j, i: (j, jnp.maximum(i - nctx, 0), 0)),
                  pl.BlockSpec((tq, gw), lat),
                  pl.BlockSpec((tq, gw), lambda j, i: (i, qblk + j)),
                  pl.BlockSpec((1, HEAD_DIM), lambda j, i: (0, 0)),
                  pl.BlockSpec((3, tq, HEAD_DIM), lambda j, i: (0, i, 0))],
        out_specs=[pl.BlockSpec((tq, gw), lambda j, i: (i, qblk + j)),
                   pl.BlockSpec((Ta, HEAD_DIM), lambda j, i: (0, j)),
                   pl.BlockSpec((Ta, HEAD_DIM), lambda j, i: (0, j)),
                   pl.BlockSpec((8, HEAD_DIM), lambda j, i: (0, 0))],
        out_shape=[jax.ShapeDtypeStruct(dproj.shape, BF),
                   jax.ShapeDtypeStruct((Ta, N_KV_HEADS * HEAD_DIM), F32),
                   jax.ShapeDtypeStruct((Ta, N_KV_HEADS * HEAD_DIM), F32),
                   jax.ShapeDtypeStruct((8, HEAD_DIM), F32)],
        operands=[dproj, q, k, proj, o, lse, do, proj, qgain, tabs], hosted=hosted, aliases={0: 0},
        params=_params(vmem=VMEM_BIG, sem=("arbitrary", "arbitrary")))
    return (dproj, dk, dv, dqg), exchanged


def kv_backward(dproj, dk, dv, proj, gain, tabs, offs, name):
    Ta = proj.shape[0]
    tm = _pick(Ta, (768, 512, ROW_TILE))
    kw = N_KV_HEADS * HEAD_DIM
    cb = offs["k"] // (2 * kw)
    kb = offs["k"] // kw
    zscale = 1.0 / LOG2E

    def body(dp_any, dk_ref, dv_ref, raw_ref, g_ref, tab_ref, o_ref, dg_ref):
        @pl.when(pl.program_id(0) == 0)
        def _():
            dg_ref[...] = jnp.zeros_like(dg_ref)

        cos, s_next, s_prev = tab_ref[0], tab_ref[1], tab_ref[2]
        dg = jnp.zeros((1, HEAD_DIM), F32)
        for h in range(N_KV_HEADS):
            sl = slice(h * HEAD_DIM, (h + 1) * HEAD_DIM)
            dr, dgh = _norm_rope_backward(dk_ref[:, sl] * zscale, raw_ref[:, sl].astype(F32), g_ref[...], cos, s_next, s_prev)
            o_ref[:, sl] = dr.astype(BF)
            dg = dg + dgh
        o_ref[:, kw:2 * kw] = dv_ref[...].astype(BF)
        dg_ref[0:1, :] += dg

    return pl.pallas_call(
        body, name=name, grid=(Ta // tm,),
        in_specs=[ANY, _row(tm, kw), _row(tm, kw), pl.BlockSpec((tm, kw), lambda i: (i, kb)),
                  _resident((1, HEAD_DIM)), pl.BlockSpec((3, tm, HEAD_DIM), lambda i: (0, i, 0))],
        out_specs=[pl.BlockSpec((tm, 2 * kw), lambda i: (i, cb)), pl.BlockSpec((8, HEAD_DIM), lambda i: (0, 0))],
        out_shape=[jax.ShapeDtypeStruct(dproj.shape, BF), jax.ShapeDtypeStruct((8, HEAD_DIM), F32)],
        input_output_aliases={0: 0},
        compiler_params=_params(sem=("arbitrary",)),
    )(dproj, dk, dv, proj, gain, tabs)


def merge_forward(x1, mod6, yc, o, proj, w_bc, w_ba, w_o, offs, Tc, name):
    T, D = yc.shape[0], x1.shape[1]
    tm = ROW_TILE
    roff = Tc // tm
    gb = offs["gt"] // (2 * D)

    def body(x_ref, mod_ref, yc_ref, o_ref, gt_ref, wbc_ref, wba_ref, wo_ref, xo_ref, pc_ref, pa_ref, m_ref, z_ref):
        gate = mod_ref[0][2:3]
        pc = jnp.dot(yc_ref[...], wbc_ref[...], preferred_element_type=F32)
        pa = jnp.dot(o_ref[...], wba_ref[...], preferred_element_type=F32)
        pc_ref[...] = pc.astype(BF)
        pa_ref[...] = pa.astype(BF)
        mb = (_sigmoid(gt_ref[:, 0:D].astype(F32)) * pc + _sigmoid(gt_ref[:, D:2 * D].astype(F32)) * pa).astype(BF)
        m_ref[...] = mb
        z = jnp.dot(mb, wo_ref[...], preferred_element_type=F32)
        z_ref[...] = z.astype(BF)
        xo_ref[...] = x_ref[...] + gate * z

    return pl.pallas_call(
        body, name=name, grid=(T // tm,),
        in_specs=[pl.BlockSpec((tm, D), lambda i: (i + roff, 0)), _mod_spec(D, 1, 0), _row(tm, yc.shape[1]), _row(tm, o.shape[1]),
                  pl.BlockSpec((tm, 2 * D), lambda i: (i + roff, gb)),
                  _resident(w_bc.shape), _resident(w_ba.shape), _resident(w_o.shape)],
        out_specs=[_row(tm, D)] * 5,
        out_shape=[jax.ShapeDtypeStruct((T, D), F32)] + [jax.ShapeDtypeStruct((T, D), BF)] * 4,
        compiler_params=_params(vmem=VMEM_BIG, sem=("arbitrary",)),
    )(x1, mod6, yc, o, proj, w_bc, w_ba, w_o)


def merge_backward_rows(dx2, mod6, z, pc, pa, proj, w_bc, w_ba, w_o, offs, Tc, name):
    T, D = dx2.shape
    Ta, P = proj.shape
    tm = ROW_TILE
    nctx = Tc // tm
    gb = offs["gt"] // (2 * D)
    dcw, dqw = w_bc.shape[0], w_ba.shape[0]

    def body(dx_ref, mod_ref, z_ref, pc_ref, pa_ref, gt_ref, wbc_ref, wba_ref, wo_ref,
             dgt_ref, dg_ref, dpc_ref, dpa_ref, dyc_ref, do_ref, acc_ref):
        i = pl.program_id(0)

        @pl.when(i == 0)
        def _():
            acc_ref[...] = jnp.zeros_like(acc_ref)

        @pl.when(i < nctx)
        def _():
            dgt_ref[...] = jnp.zeros_like(dgt_ref)

        @pl.when(i >= nctx)
        def _():
            gate = mod_ref[0][2:3]
            dx = dx_ref[...]
            acc_ref[0:1, :] += jnp.sum(dx * z_ref[...].astype(F32), axis=0, keepdims=True)
            dgb = (dx * gate).astype(BF)
            dg_ref[...] = dgb
            dm = lax.dot_general(dgb, wo_ref[...], NT, preferred_element_type=F32)
            sc = _sigmoid(gt_ref[:, 0:D].astype(F32))
            sa = _sigmoid(gt_ref[:, D:2 * D].astype(F32))
            pc = pc_ref[...].astype(F32)
            pa = pa_ref[...].astype(F32)
            dpc = (dm * sc).astype(BF)
            dpa = (dm * sa).astype(BF)
            dpc_ref[...] = dpc
            dpa_ref[...] = dpa
            dgt_ref[:, 0:D] = ((dm * pc) * (sc * (1.0 - sc))).astype(BF)
            dgt_ref[:, D:2 * D] = ((dm * pa) * (sa * (1.0 - sa))).astype(BF)
            dyc_ref[...] = lax.dot_general(dpc, wbc_ref[...], NT, preferred_element_type=F32).astype(BF)
            do_ref[...] = lax.dot_general(dpa, wba_ref[...], NT, preferred_element_type=F32).astype(BF)

    lat = lambda n: pl.BlockSpec((tm, n), lambda i: (jnp.maximum(i - nctx, 0), 0))
    return pl.pallas_call(
        body, name=name, grid=(Ta // tm,),
        in_specs=[lat(D), _mod_spec(D, 1, 0), lat(D), lat(D), lat(D),
                  pl.BlockSpec((tm, 2 * D), lambda i: (i, gb)),
                  _resident(w_bc.shape), _resident(w_ba.shape), _resident(w_o.shape)],
        out_specs=[pl.BlockSpec((tm, 2 * D), lambda i: (i, gb)), lat(D), lat(D), lat(D), lat(dcw), lat(dqw),
                   pl.BlockSpec((8, D), lambda i: (0, 0))],
        out_shape=[jax.ShapeDtypeStruct((Ta, P), BF)] + [jax.ShapeDtypeStruct((T, D), BF)] * 3
                  + [jax.ShapeDtypeStruct((T, dcw), BF), jax.ShapeDtypeStruct((T, dqw), BF), jax.ShapeDtypeStruct((8, D), F32)],
        compiler_params=_params(vmem=VMEM_BIG, sem=("arbitrary",)),
    )(dx2, mod6, z, pc, pa, proj, w_bc, w_ba, w_o)


def proj_backward_rows(dproj, dres, xa, mod6, g, w_in, nctx, name, hosted=None):
    Tr, D = xa.shape
    P = w_in.shape[1]
    tm = ROW_TILE

    def body(dp_ref, dres_ref, x_ref, mod_ref, g_ref, w_ref, dx_ref, acc_ref):
        i = pl.program_id(0)

        @pl.when((i == 0) | (i == nctx))
        def _():
            acc_ref[...] = jnp.zeros_like(acc_ref)

        x = x_ref[...]
        scale = mod_ref[0][1:2]
        gg = g_ref[...]
        dhm = lax.dot_general(dp_ref[...], w_ref[...], NT, preferred_element_type=F32)
        r = lax.rsqrt(jnp.mean(x * x, axis=-1, keepdims=True) + EPS)
        xh = x * r
        dshift = jnp.sum(dhm, axis=0, keepdims=True)
        dscale = jnp.sum(dhm * (xh * gg), axis=0, keepdims=True)
        dxh_g = dhm * (1.0 + scale)
        dg = jnp.sum(dxh_g * xh, axis=0, keepdims=True)
        dxh = dxh_g * gg
        res = jnp.where(i < nctx, 0.0, dres_ref[...])
        dx_ref[...] = res + r * (dxh - xh * jnp.mean(dxh * xh, axis=-1, keepdims=True))
        for k, val in enumerate((dshift, dscale, dg)):
            acc_ref[0, k:k + 1, :] += val

    return _call(
        body, name=name, grid=(Tr // tm,),
        in_specs=[_row(tm, P), pl.BlockSpec((tm, D), lambda i: (jnp.maximum(i - nctx, 0), 0)), _row(tm, D),
                  _mod_spec(D, 1, nctx), _resident((1, D)), _resident(w_in.shape)],
        out_specs=[_row(tm, D), _acc_spec(D, nctx)],
        out_shape=[jax.ShapeDtypeStruct((Tr, D), F32), jax.ShapeDtypeStruct((2, 8, D), F32)],
        operands=[dproj, dres, xa, mod6, g, w_in], hosted=hosted,
        params=_params(vmem=VMEM_BIG, sem=("arbitrary",)))


def _adam_update(w, g, m, v):
    c1 = 1.0 - ADAM_B1 ** ADAM_STEP
    c2 = 1.0 - ADAM_B2 ** ADAM_STEP
    m = ADAM_B1 * m + (1.0 - ADAM_B1) * g
    v = ADAM_B2 * v + (1.0 - ADAM_B2) * (g * g)
    return -ADAM_LR * ((m / c1) / (jnp.sqrt(v / c2) + ADAM_EPS) + ADAM_WD * w), m, v


def adamw(w, g, m, v, name):
    R, C = w.shape
    tr = _row_tile(R, C)

    def body(w_ref, g_ref, m_ref, v_ref, d_ref, nm_ref, nv_ref):
        d_ref[...], nm_ref[...], nv_ref[...] = _adam_update(w_ref[...], g_ref[...], m_ref[...], v_ref[...])

    blk = pl.BlockSpec((tr, C), lambda i: (i, 0))
    return pl.pallas_call(
        body, name=name, grid=(R // tr,),
        in_specs=[blk] * 4, out_specs=[blk] * 3,
        out_shape=[jax.ShapeDtypeStruct((R, C), F32)] * 3,
        compiler_params=_params(vmem=VMEM_BIG, sem=("parallel",)),
    )(w, g, m, v)


def adamw_summed(recv, w, m, v, name):
    R, C = w.shape
    tr = _row_tile(R, C)

    def body(r_ref, w_ref, m_ref, v_ref, g_ref, d_ref, nm_ref, nv_ref):
        g = r_ref[0].astype(F32)
        for a in range(1, N_DEV):
            g = g + r_ref[a].astype(F32)
        g_ref[...] = g
        d_ref[...], nm_ref[...], nv_ref[...] = _adam_update(w_ref[...], g, m_ref[...], v_ref[...])

    blk = pl.BlockSpec((tr, C), lambda i: (i, 0))
    return pl.pallas_call(
        body, name=name, grid=(R // tr,),
        in_specs=[pl.BlockSpec((N_DEV, tr, C), lambda i: (0, i, 0)), blk, blk, blk], out_specs=[blk] * 4,
        out_shape=[jax.ShapeDtypeStruct((R, C), F32)] * 4,
        compiler_params=_params(vmem=VMEM_BIG, sem=("parallel",)),
    )(recv, w, m, v)


def adamw_reduced(p_own, recv_b, w, m, v, name):
    R, C = w.shape
    tr = _row_tile(R, C)

    def body(p_ref, b_ref, w_ref, m_ref, v_ref, g_ref, d_ref, nm_ref, nv_ref):
        g = p_ref[...].astype(F32)
        for j in range(3):
            g = g + b_ref[j].astype(F32)
        g_ref[...] = g
        d_ref[...], nm_ref[...], nv_ref[...] = _adam_update(w_ref[...], g, m_ref[...], v_ref[...])

    blk = pl.BlockSpec((tr, C), lambda r: (r, 0))
    return pl.pallas_call(
        body, name=name, grid=(R // tr,),
        in_specs=[blk, pl.BlockSpec((3, tr, C), lambda r: (0, r, 0)), blk, blk, blk], out_specs=[blk] * 4,
        out_shape=[jax.ShapeDtypeStruct((R, C), F32)] * 4,
        compiler_params=_params(vmem=VMEM_BIG, sem=("parallel",)),
    )(p_own, recv_b, w, m, v)


def _rope_tables(T, Tc):
    rows = T // GRID_W
    n_freq = HEAD_DIM // 4
    inv = ROPE_THETA ** (-jnp.arange(n_freq, dtype=F32) / n_freq)
    ang_r = jnp.arange(rows).astype(F32)[:, None] * inv
    ang_c = jnp.arange(GRID_W).astype(F32)[:, None] * inv
    per_row = lambda a: jnp.broadcast_to(a[:, None, :], (rows, GRID_W, n_freq)).reshape(T, n_freq)
    per_col = lambda a: jnp.broadcast_to(a[None, :, :], (rows, GRID_W, n_freq)).reshape(T, n_freq)
    cr, sr = per_row(jnp.cos(ang_r)), per_row(jnp.sin(ang_r))
    cc, sc = per_col(jnp.cos(ang_c)), per_col(jnp.sin(ang_c))
    zero = jnp.zeros_like(sr)
    cos = jnp.concatenate([cr, cr, cc, cc], axis=1)
    s_next = jnp.concatenate([-sr, zero, -sc, zero], axis=1)
    s_prev = jnp.concatenate([zero, sr, zero, sc], axis=1)
    lat = jnp.stack([cos, s_next, s_prev])
    ctx = jnp.stack([jnp.ones((Tc, HEAD_DIM), F32), jnp.zeros((Tc, HEAD_DIM), F32), jnp.zeros((Tc, HEAD_DIM), F32)])
    return jnp.concatenate([ctx, lat], axis=1)


BIG = ("ffn1_w_in", "ffn1_w_out", "w_in", "w_branch_conv", "w_branch_attn", "w_out", "ffn2_w_in", "ffn2_w_out")


def _regroup_w_in(stacked, D, Dc, qw, kw):
    w = stacked.transpose(1, 0, 2).reshape(D, -1)
    o = 0
    parts = {}
    for nme, wd in (("bg", Dc), ("cg", Dc), ("vc", Dc), ("q", qw), ("k", kw), ("v", kw), ("gt", 2 * D)):
        parts[nme] = w[:, o:o + wd]
        o += wd
    nb = Dc // 128
    cv = jnp.stack([parts[n].reshape(D, nb, 128) for n in ("bg", "cg", "vc")], axis=2).reshape(D, 3 * Dc)
    return jnp.concatenate([cv, parts["q"], parts["gt"], parts["k"], parts["v"]], axis=1)


def _ungroup_w_in_grad(gt_, D, Dc, qw, kw):
    nb = Dc // 128
    cv = gt_[:3 * Dc].reshape(nb, 3, 128, D)
    o = 3 * Dc
    q = gt_[o:o + qw]
    gt = gt_[o + qw:o + qw + 2 * D]
    k = gt_[o + qw + 2 * D:o + qw + 2 * D + kw]
    v = gt_[o + qw + 2 * D + kw:]
    nat = jnp.concatenate([cv[:, 0].reshape(Dc, D), cv[:, 1].reshape(Dc, D), cv[:, 2].reshape(Dc, D), q, k, v, gt], axis=0)
    return nat.reshape(N_DEV, -1, D)


def kernel(x, c, ctx, c_ctx, w_mod, b_mod, norm1_g, norm2_g, norm3_g, ffn1_w_in, ffn1_w_out, w_in, conv_w, q_norm_g, k_norm_g, w_branch_conv, w_branch_attn, w_out, ffn2_w_in, ffn2_w_out, final_g, loss_target, m_c_ctx, m_w_mod, m_b_mod, m_norm1_g, m_norm2_g, m_norm3_g, m_ffn1_w_in, m_ffn1_w_out, m_w_in, m_conv_w, m_q_norm_g, m_k_norm_g, m_w_branch_conv, m_w_branch_attn, m_w_out, m_ffn2_w_in, m_ffn2_w_out, m_final_g, v_c_ctx, v_w_mod, v_b_mod, v_norm1_g, v_norm2_g, v_norm3_g, v_ffn1_w_in, v_ffn1_w_out, v_w_in, v_conv_w, v_q_norm_g, v_k_norm_g, v_w_branch_conv, v_w_branch_attn, v_w_out, v_ffn2_w_in, v_ffn2_w_out, v_final_g):
    weights = dict(c_ctx=c_ctx, w_mod=w_mod, b_mod=b_mod, norm1_g=norm1_g, norm2_g=norm2_g, norm3_g=norm3_g,
                   ffn1_w_in=ffn1_w_in, ffn1_w_out=ffn1_w_out, w_in=w_in, conv_w=conv_w, q_norm_g=q_norm_g,
                   k_norm_g=k_norm_g, w_branch_conv=w_branch_conv, w_branch_attn=w_branch_attn, w_out=w_out,
                   ffn2_w_in=ffn2_w_in, ffn2_w_out=ffn2_w_out, final_g=final_g)
    moms = dict(c_ctx=(m_c_ctx, v_c_ctx), w_mod=(m_w_mod, v_w_mod), b_mod=(m_b_mod, v_b_mod),
                norm1_g=(m_norm1_g, v_norm1_g), norm2_g=(m_norm2_g, v_norm2_g), norm3_g=(m_norm3_g, v_norm3_g),
                ffn1_w_in=(m_ffn1_w_in, v_ffn1_w_in), ffn1_w_out=(m_ffn1_w_out, v_ffn1_w_out), w_in=(m_w_in, v_w_in),
                conv_w=(m_conv_w, v_conv_w), q_norm_g=(m_q_norm_g, v_q_norm_g), k_norm_g=(m_k_norm_g, v_k_norm_g),
                w_branch_conv=(m_w_branch_conv, v_w_branch_conv), w_branch_attn=(m_w_branch_attn, v_w_branch_attn),
                w_out=(m_w_out, v_w_out), ffn2_w_in=(m_ffn2_w_in, v_ffn2_w_in), ffn2_w_out=(m_ffn2_w_out, v_ffn2_w_out),
                final_g=(m_final_g, v_final_g))
    order = list(weights)

    T, D = x.shape[1], x.shape[2]
    Tc = ctx.shape[1]
    nctx = Tc // ROW_TILE
    nd = N_MOD * D
    Dc = conv_w.shape[2] * N_DEV
    qw, kw = N_Q_HEADS * HEAD_DIM, N_KV_HEADS * HEAD_DIM
    offs, o = {}, 0
    for nme, wd in (("cv", 3 * Dc), ("q", qw), ("gt", 2 * D), ("k", kw), ("v", kw)):
        offs[nme] = o
        o += wd

    ax, ay, ac = lax.axis_index("x"), lax.axis_index("y"), lax.axis_index("c")
    me = 4 * ax + 2 * ay + ac

    shard = {n: (jnp.swapaxes(weights[n][0], 0, 1) if n in ("ffn1_w_in", "ffn2_w_in") else weights[n][0]).astype(BF)
             for n in BIG}
    rows2d = lambda a: a.reshape(-1, a.shape[-1])
    full = {}
    mod_cols = w_mod.shape[2]
    cw_loc = conv_w[0]
    cpad = (-(D + CONV_TAPS * cw_loc.shape[1])) % 128
    pay = jnp.concatenate([c.reshape(1, D), cw_loc.reshape(1, -1), jnp.zeros((1, cpad), F32)], axis=1)
    (g_ffn1_in, g_ffn1_out), (call,) = allgather_two_level([shard["ffn1_w_in"], shard["ffn1_w_out"]], "ag_ffn1",
                                                          riders=[pay])
    full["ffn1_w_in"], full["ffn1_w_out"] = rows2d(g_ffn1_in), rows2d(g_ffn1_out)

    conv_full = call[:, 0, D:D + CONV_TAPS * cw_loc.shape[1]].reshape(N_DEV, CONV_TAPS, -1).transpose(1, 0, 2).reshape(CONV_TAPS, Dc)
    b_loc = lax.dynamic_slice_in_dim(b_mod, me * mod_cols, mod_cols, axis=1)
    cctx2 = c_ctx.reshape(1, D)
    mod_part = mod_forward(call, cctx2, w_mod[0], b_loc, "mod_fwd")
    mod_all = allgather_direct(mod_part, "ag_mod")
    mod_lat = lax.dynamic_index_in_dim(mod_all, me, axis=1, keepdims=False).reshape(nd)
    mod_ctx = mod_all[:, N_DEV, :].reshape(nd)
    mod6 = jnp.stack([mod_ctx, mod_lat]).reshape(6, 3, D)

    tabs = _rope_tables(T, Tc)

    srcs1 = (ctx[0], x[0])
    (xa1, hm1, ab1, h1, f1), (g_w_in,) = ffn_forward(
        srcs1, mod6, 0, norm1_g, full["ffn1_w_in"], full["ffn1_w_out"], nctx, "ffn1_fwd",
        hosted=Hosted(gathers=[shard["w_in"]]))
    full["w_in"] = _regroup_w_in(g_w_in, D, Dc, qw, kw)
    merge_names = ("w_branch_conv", "w_branch_attn", "w_out")
    (hx, proj, qr, kr), g_merge = proj_forward(
        xa1, mod6, norm2_g, full["w_in"], q_norm_g, k_norm_g, tabs, offs, nctx, "proj_fwd",
        hosted=Hosted(gathers=[shard[n] for n in merge_names]))
    full.update({n: rows2d(g) for n, g in zip(merge_names, g_merge)})
    yc = conv_forward(proj, conv_full, offs, Tc, "conv_fwd")
    (oa, lse), (g_ffn2_in, g_ffn2_out) = attention_forward(
        qr, kr, proj, offs, Tc, "attn_fwd", hosted=Hosted(gathers=[shard["ffn2_w_in"], shard["ffn2_w_out"]]))
    full["ffn2_w_in"], full["ffn2_w_out"] = rows2d(g_ffn2_in), rows2d(g_ffn2_out)
    x2, pc, pa, mm, zz = merge_forward(xa1, mod6, yc, oa, proj, full["w_branch_conv"], full["w_branch_attn"],
                                       full["w_out"], offs, Tc, "merge_fwd")
    (dx3, hm2, ab2, h2, f2, lacc), _ = ffn_forward((x2,), mod6, 2, norm3_g, full["ffn2_w_in"], full["ffn2_w_out"], 0, "ffn2_fwd",
                                                   final=(loss_target[0], final_g.reshape(1, D)))
    loss = lax.psum(lacc[1, 0], ("x", "y", "c"))

    by_dest = lambda g: g.reshape((N_DEV, -1, g.shape[-1]))
    (dx2, dab2, df2, acc_f2), _ = ffn_backward_rows(dx3, (x2,), mod6, 2, norm3_g, ab2, f2, full["ffn2_w_in"], full["ffn2_w_out"], 0, "ffn2_bwd")
    early = {"ffn2_w_out": tn_matmul(h2, df2, "ffn2_dwout")[0], "ffn2_w_in": tn_matmul(dab2, hm2, "ffn2_dwin")[0]}
    dproj, dgm, dpc, dpa, dyc, do, acc_mg = merge_backward_rows(dx2, mod6, zz, pc, pa, proj, full["w_branch_conv"],
                                                                full["w_branch_attn"], full["w_out"], offs, Tc, "merge_bwd")
    early["w_out"] = tn_matmul(mm, dgm, "dw_out")[0]
    early["w_branch_conv"] = tn_matmul(yc, dpc, "dw_bc")[0]
    early["w_branch_attn"] = tn_matmul(oa, dpa, "dw_ba")[0]
    dproj, dcw = conv_backward(dproj, dyc, proj, conv_full, offs, Tc, "conv_bwd")
    (dproj, dk, dv, dqg), summed = attention_backward(dproj, qr, kr, proj, oa, lse, do, q_norm_g, tabs, offs, Tc, "attn_bwd",
                                                      hosted=Hosted(scatters=[by_dest(g) for g in early.values()]))
    summed = dict(zip(early, summed))
    dproj, dkg = kv_backward(dproj, dk, dv, proj, k_norm_g, tabs, offs, "kv_bwd")
    g_w_in_grad = _ungroup_w_in_grad(tn_matmul(dproj, hx, "dw_in")[0], D, Dc, qw, kw)
    (dxa1, acc_pj), (summed["w_in"],) = proj_backward_rows(dproj, dx2, xa1, mod6, norm2_g, full["w_in"], nctx, "proj_bwd",
                                                          hosted=Hosted(scatters=[g_w_in_grad]))
    (grad_x2d, dab1, df1, acc_f1), _ = ffn_backward_rows(
        dxa1, srcs1, mod6, 0, norm1_g, ab1, f1, full["ffn1_w_in"], full["ffn1_w_out"], nctx, "ffn1_bwd")
    g_ffn1_w_in, (summed["ffn1_w_out"],) = tn_matmul(
        dab1, hm1, "ffn1_dwin", hosted=Hosted(scatters=[by_dest(tn_matmul(h1, df1, "ffn1_dwout")[0])]))
    grad_x = grad_x2d[None]
    last_sems, last_recv_sems, last_src, last_land, token = scatter_start(by_dest(g_ffn1_w_in), "rs_ffn1_start")

    zero_d = jnp.zeros((D,), F32)
    dlat = jnp.concatenate([acc_f1[1, 0], acc_f1[1, 1], acc_f1[1, 2], acc_pj[1, 0], acc_pj[1, 1], acc_mg[0],
                            acc_f2[1, 0], acc_f2[1, 1], acc_f2[1, 2]])
    dctx = jnp.concatenate([acc_f1[0, 0], acc_f1[0, 1], acc_f1[0, 2], acc_pj[0, 0], acc_pj[0, 1]] + [zero_d] * 4)
    small = jnp.concatenate([acc_f1[0, 3] + acc_f1[1, 3], acc_pj[0, 2] + acc_pj[1, 2], acc_f2[1, 3],
                             dqg[0], dkg[0], lacc[0], dcw[0:CONV_TAPS].reshape(-1)])
    n_small = small.shape[0]
    pay_b = jnp.concatenate([dlat, dctx, small]).reshape(1, -1) + token[0:1, 0:1]
    gath = allgather_direct(pay_b, "ag_small_grads")
    dlat_loc = lax.dynamic_slice_in_dim(gath[:, 0, :nd], me * mod_cols, mod_cols, axis=1)
    dctx_loc = lax.dynamic_slice_in_dim(gath[:, 0, nd:2 * nd], me * mod_cols, mod_cols, axis=1)
    g_wmod, pc_part, small_sum = mod_backward(call, cctx2, w_mod[0], dlat_loc, dctx_loc, gath, 2 * nd, n_small, "mod_bwd")
    pcs = allgather_direct(pc_part, "ag_cctx")
    g_bmod, g_cctx = bmod_and_cctx_grad(gath, pcs, cctx2, nd, "small_bwd")
    sm = small_sum[0]
    g_conv_full = sm[3 * D + 2 * HEAD_DIM + D:].reshape(CONV_TAPS, Dc)
    g_conv = lax.dynamic_slice_in_dim(g_conv_full, me * cw_loc.shape[1], cw_loc.shape[1], axis=1)
    gsmall = dict(
        c_ctx=g_cctx, w_mod=g_wmod, b_mod=g_bmod, norm1_g=sm[0:D][None], norm2_g=sm[D:2 * D][None],
        norm3_g=sm[2 * D:3 * D][None], q_norm_g=sm[3 * D:3 * D + HEAD_DIM][None],
        k_norm_g=sm[3 * D + HEAD_DIM:3 * D + 2 * HEAD_DIM][None],
        final_g=sm[3 * D + 2 * HEAD_DIM:3 * D + 2 * HEAD_DIM + D][None], conv_w=g_conv)

    flipped = ("ffn1_w_in", "w_in", "ffn2_w_in")
    last = "ffn1_w_in"
    results = {}

    def update(n):
        w = weights[n]
        shp = w.shape
        if n in flipped:
            two_d = lambda a: jnp.swapaxes(a[0], 0, 1)
            back = lambda a: jnp.swapaxes(a, 0, 1)[None]
        else:
            two_d = lambda a: a.reshape(-1, shp[-1])
            back = lambda a: a.reshape(shp)
        m, v = moms[n]
        if n in summed:
            g2, d, nm, nv = adamw_summed(summed[n], two_d(w), two_d(m), two_d(v), "adamw_" + n)
        else:
            g2 = gsmall[n].reshape(two_d(w).shape)
            d, nm, nv = adamw(two_d(w), g2, two_d(m), two_d(v), "adamw_" + n)
        results[n] = tuple(back(a) for a in (g2, d, nm, nv))
        return d

    after = jnp.concatenate([update(n)[0:1, 0:1] for n in order if n != last], axis=1)
    g_done, land_done = scatter_wait(last_sems, last_recv_sems, last_src, last_land, after, "rs_ffn1_wait")
    own = lax.dynamic_index_in_dim(g_done, me, axis=0, keepdims=True)
    summed[last] = lax.dynamic_update_slice_in_dim(land_done, own, me, axis=0)
    update(last)
    cols = list(zip(*(results[n] for n in order)))
    return (loss, grad_x, *cols[0], *cols[1], *cols[2], *cols[3])
```

```python
import math

import jax
import jax.numpy as jnp
from jax import lax
from jax.experimental import pallas as pl
from jax.experimental.pallas import tpu as pltpu

F32 = jnp.float32
BF = jnp.bfloat16
EPS = 1e-6
N_DEV = 8
HEAD_DIM = 128
N_Q_HEADS = 8
N_KV_HEADS = 2
GROUP = N_Q_HEADS // N_KV_HEADS
GRID_W = 64
ROPE_THETA = 10000.0
CONV_TAPS = 3
N_MOD = 9
ADAM_LR = 0.001
ADAM_B1 = 0.9
ADAM_B2 = 0.999
ADAM_EPS = 1e-08
ADAM_WD = 0.01
ADAM_STEP = 10
ROW_TILE = 256
VMEM_BIG = 56 << 20
MESH_ID = pl.DeviceIdType.MESH
HIGHEST = lax.Precision.HIGHEST
NT = (((1,), (1,)), ((), ()))
TN = (((0,), (0,)), ((), ()))
LOG2E = math.log2(math.e)


def _pick(n, cands):
    for c in cands:
        if n % c == 0:
            return c
    return n


def _params(vmem=None, sem=None):
    kw = {}
    if vmem is not None:
        kw["vmem_limit_bytes"] = vmem
    if sem is not None:
        kw["dimension_semantics"] = sem
    return pltpu.CompilerParams(**kw)


def _resident(shape):
    nd = len(shape)
    return pl.BlockSpec(shape, lambda *_: (0,) * nd, pipeline_mode=pl.Buffered(1))


def _sigmoid(x):
    return jax.nn.sigmoid(x)


ANY = pl.BlockSpec(memory_space=pl.ANY)


def _coords():
    return lax.axis_index("x"), lax.axis_index("y"), lax.axis_index("c")


def _flip(v, bit):
    return 1 - v if bit else v


def _remote(src, dst, ssem, rsem, dev):
    return pltpu.make_async_remote_copy(src_ref=src, dst_ref=dst, send_sem=ssem, recv_sem=rsem,
                                        device_id=dev, device_id_type=MESH_ID)


def allgather_direct(v, name):
    def body(v_ref, out_ref, ssem, rsem, lsem):
        x, y, c = _coords()
        me = 4 * x + 2 * y + c
        mine = pltpu.make_async_copy(v_ref, out_ref.at[me], lsem)
        mine.start()
        cps = []
        for p in range(1, N_DEV):
            px, py, pc = (p >> 2) & 1, (p >> 1) & 1, p & 1
            cps.append(_remote(v_ref, out_ref.at[me], ssem.at[p - 1], rsem.at[p - 1],
                               (_flip(x, px), _flip(y, py), _flip(c, pc))))
        for cp in cps:
            cp.start()
        for p in range(1, N_DEV):
            px, py, pc = (p >> 2) & 1, (p >> 1) & 1, p & 1
            src = 4 * _flip(x, px) + 2 * _flip(y, py) + _flip(c, pc)
            _remote(v_ref, out_ref.at[src], ssem.at[p - 1], rsem.at[p - 1], (x, y, c)).wait_recv()
        for cp in cps:
            cp.wait_send()
        mine.wait()

    return pl.pallas_call(
        body, name=name,
        out_shape=jax.ShapeDtypeStruct((N_DEV,) + v.shape, v.dtype),
        in_specs=[ANY], out_specs=ANY,
        scratch_shapes=[pltpu.SemaphoreType.DMA((N_DEV - 1,)), pltpu.SemaphoreType.DMA((N_DEV - 1,)),
                        pltpu.SemaphoreType.DMA],
    )(v)


def allgather_two_level(shards, name, riders=()):
    n = len(shards)
    ride = Hosted(gathers=riders)
    r = ride.n

    def body(*refs):
        v_refs, rin, out_refs, rout = refs[:n], refs[n:n + r], refs[n + r:2 * n + r], refs[2 * n + r:2 * n + 2 * r]
        (ssem, rsem, lsem), rsems = refs[2 * n + 2 * r:2 * n + 2 * r + 3], refs[2 * n + 2 * r + 3:]
        x, y, c = _coords()
        me = (x, y, c)
        sib = (x, y, 1 - c)
        chips = [(1 - x, y), (x, 1 - y), (1 - x, 1 - y)]
        if r:
            ride.start(rin, rout, *rsems)

        def slot(w, px, py, pc):
            return out_refs[w].at[4 * px + 2 * py + pc]

        def sem(w, k):
            return ssem.at[7 * w + k], rsem.at[7 * w + k]

        mine = [pltpu.make_async_copy(v_refs[w], slot(w, *me), lsem.at[w]) for w in range(n)]
        for cp in mine:
            cp.start()
        first = []
        for w in range(n):
            first.append(_remote(v_refs[w], slot(w, *me), *sem(w, 0), sib))
            first += [_remote(v_refs[w], slot(w, *me), *sem(w, 1 + j), (*chip, c)) for j, chip in enumerate(chips)]
        for cp in first:
            cp.start()
        passed = []
        for w in range(n):
            for j, chip in enumerate(chips):
                _remote(v_refs[w], slot(w, *chip, c), *sem(w, 1 + j), me).wait_recv()
                cp = _remote(slot(w, *chip, c), slot(w, *chip, c), *sem(w, 4 + j), sib)
                cp.start()
                passed.append(cp)
        for w in range(n):
            _remote(v_refs[w], slot(w, x, y, 1 - c), *sem(w, 0), me).wait_recv()
            for j, chip in enumerate(chips):
                _remote(v_refs[w], slot(w, *chip, 1 - c), *sem(w, 4 + j), me).wait_recv()
        for cp in first + passed:
            cp.wait_send()
        for cp in mine:
            cp.wait()
        if r:
            ride.wait(rin, rout, *rsems)

    res = pl.pallas_call(
        body, name=name,
        out_shape=[jax.ShapeDtypeStruct((N_DEV,) + s.shape, s.dtype) for s in shards] + ride.out_shapes,
        in_specs=[ANY] * (n + r), out_specs=[ANY] * (n + r),
        scratch_shapes=[pltpu.SemaphoreType.DMA((7 * n,)), pltpu.SemaphoreType.DMA((7 * n,)),
                        pltpu.SemaphoreType.DMA((n,))] + (ride.scratch if r else []),
    )(*shards, *riders)
    return list(res[:n]), list(res[n:])


def _block_exchange(srcs, k, copies, name, riders):
    n = len(srcs)
    ride = Hosted(gathers=riders)
    r = ride.n

    def body(*refs):
        g_refs, rin, out_refs, rout = refs[:n], refs[n:n + r], refs[n + r:2 * n + r], refs[2 * n + r:2 * n + 2 * r]
        (ssem, rsem), rsems = refs[2 * n + 2 * r:2 * n + 2 * r + 2], refs[2 * n + 2 * r + 2:]
        if r:
            ride.start(rin, rout, *rsems)
        cps = copies(g_refs, out_refs, ssem, rsem)
        for cp in cps:
            cp.start()
        for cp in cps:
            cp.wait()
        if r:
            ride.wait(rin, rout, *rsems)

    res = pl.pallas_call(
        body, name=name,
        out_shape=[jax.ShapeDtypeStruct((k,) + g.shape[1:], g.dtype) for g in srcs] + ride.out_shapes,
        in_specs=[ANY] * (n + r), out_specs=[ANY] * (n + r),
        scratch_shapes=[pltpu.SemaphoreType.DMA((k * n,)), pltpu.SemaphoreType.DMA((k * n,))] + (ride.scratch if r else []),
    )(*srcs, *riders)
    return list(res[:n]), list(res[n:])


def rs_sibling_exchange(gs, name, riders=()):
    def copies(g_refs, out_refs, ssem, rsem):
        x, y, c = _coords()
        return [_remote(g_refs[w].at[2 * k + (1 - c)], out_refs[w].at[k], ssem.at[4 * w + k], rsem.at[4 * w + k], (x, y, 1 - c))
                for w in range(len(gs)) for k in range(4)]

    return _block_exchange(gs, 4, copies, name, riders)


def rs_chip_exchange(ps, name, riders=()):
    def copies(p_refs, out_refs, ssem, rsem):
        x, y, c = _coords()
        chips = [(1 - x, y), (x, 1 - y), (1 - x, 1 - y)]
        return [_remote(p_refs[w].at[2 * cx + cy], out_refs[w].at[j], ssem.at[3 * w + j], rsem.at[3 * w + j], (cx, cy, c))
                for w in range(len(ps)) for j, (cx, cy) in enumerate(chips)]

    return _block_exchange(ps, 3, copies, name, riders)


SEM = pl.BlockSpec(memory_space=pltpu.SEMAPHORE)
IN_HBM = pl.BlockSpec(memory_space=pltpu.HBM)
DATAFLOW = pltpu.SideEffectType.DATAFLOW_SIDE_EFFECTING


def _scatter_descriptors(src_ref, land_ref, ssem, rsem, arrivals):
    x, y, c = _coords()
    me = 4 * x + 2 * y + c
    cps = []
    for p in range(1, N_DEV):
        px, py, pc = _flip(x, (p >> 2) & 1), _flip(y, (p >> 1) & 1), _flip(c, p & 1)
        peer = 4 * px + 2 * py + pc
        if arrivals:
            cps.append(_remote(src_ref.at[me], land_ref.at[peer], ssem.at[p - 1], rsem.at[p - 1], (x, y, c)))
        else:
            cps.append(_remote(src_ref.at[peer], land_ref.at[me], ssem.at[p - 1], rsem.at[p - 1], (px, py, pc)))
    return cps


def scatter_start(g, before, name):
    def body(g_ref, land_ref, before_ref, ssem, rsem, g_thru, land_thru):
        for cp in _scatter_descriptors(g_ref, land_ref, ssem, rsem, False):
            cp.start()

    return pl.pallas_call(
        body, name=name,
        out_shape=(pltpu.SemaphoreType.DMA((N_DEV - 1,)), pltpu.SemaphoreType.DMA((N_DEV - 1,)),
                   pltpu.HBM(g.shape, g.dtype), pltpu.HBM(g.shape, g.dtype)),
        in_specs=(IN_HBM, IN_HBM, ANY), out_specs=(SEM, SEM, IN_HBM, IN_HBM),
        input_output_aliases={0: 2, 1: 3},
        compiler_params=pltpu.CompilerParams(has_side_effects=DATAFLOW),
    )(pltpu.with_memory_space_constraint(g, pltpu.HBM),
      pltpu.with_memory_space_constraint(lax.empty(g.shape, g.dtype), pltpu.HBM), before)


def scatter_wait(ssem, rsem, g_thru, land_thru, after, name):
    def body(g_ref, land_ref, ssem_ref, rsem_ref, after_ref, g_out, land_out):
        cps = _scatter_descriptors(g_ref, land_ref, ssem_ref, rsem_ref, True)
        for cp in cps:
            cp.wait_send()
        for cp in cps:
            cp.wait_recv()

    return pl.pallas_call(
        body, name=name,
        out_shape=(pltpu.HBM(g_thru.shape, g_thru.dtype), pltpu.HBM(land_thru.shape, land_thru.dtype)),
        in_specs=(IN_HBM, IN_HBM, SEM, SEM, ANY), out_specs=(IN_HBM, IN_HBM),
        input_output_aliases={0: 0, 1: 1},
        compiler_params=pltpu.CompilerParams(has_side_effects=DATAFLOW),
    )(g_thru, land_thru, ssem, rsem, after)


class Hosted:
    def __init__(self, gathers=(), scatters=()):
        self.items = [(a, False) for a in gathers] + [(a, True) for a in scatters]
        self.n = len(self.items)
        self.operands = [a for a, _ in self.items]
        self.out_shapes = [jax.ShapeDtypeStruct(a.shape if sc else (N_DEV,) + a.shape, a.dtype) for a, sc in self.items]
        self.scratch = [pltpu.SemaphoreType.DMA((7 * self.n,)), pltpu.SemaphoreType.DMA((7 * self.n,)),
                        pltpu.SemaphoreType.DMA((self.n,))]

    def _copies(self, in_refs, out_refs, ssem, rsem, lsem, arrivals):
        x, y, c = _coords()
        me = 4 * x + 2 * y + c
        remote, local = [], []
        for w, (_, sc) in enumerate(self.items):
            src, dst = in_refs[w], out_refs[w]
            local.append(pltpu.make_async_copy(src.at[me] if sc else src, dst.at[me], lsem.at[w]))
            for p in range(1, N_DEV):
                px, py, pc = _flip(x, (p >> 2) & 1), _flip(y, (p >> 1) & 1), _flip(c, p & 1)
                peer = 4 * px + 2 * py + pc
                k = 7 * w + p - 1
                if arrivals:
                    remote.append(_remote(src.at[me] if sc else src, dst.at[peer], ssem.at[k], rsem.at[k], (x, y, c)))
                else:
                    remote.append(_remote(src.at[peer] if sc else src, dst.at[me], ssem.at[k], rsem.at[k], (px, py, pc)))
        return remote, local

    def start(self, in_refs, out_refs, ssem, rsem, lsem):
        sends, local = self._copies(in_refs, out_refs, ssem, rsem, lsem, False)
        for cp in local + sends:
            cp.start()

    def wait(self, in_refs, out_refs, ssem, rsem, lsem):
        arrivals, local = self._copies(in_refs, out_refs, ssem, rsem, lsem, True)
        for cp in arrivals:
            cp.wait_recv()
        for cp in arrivals:
            cp.wait_send()
        for cp in local:
            cp.wait()


def _call(body, *, name, grid, in_specs, out_specs, out_shape, operands, params, scratch_shapes=(), aliases=None, hosted=None):
    n_in, n_out, n_scr = len(in_specs), len(out_specs), len(scratch_shapes)
    h = hosted.n if hosted is not None else 0

    def wrapped(*refs):
        ins, cins = refs[:n_in], refs[n_in:n_in + h]
        outs, couts = refs[n_in + h:n_in + h + n_out], refs[n_in + h + n_out:n_in + 2 * h + n_out]
        rest = refs[n_in + 2 * h + n_out:]
        scr, sems = rest[:n_scr], rest[n_scr:]
        if h:
            ids = [pl.program_id(a) for a in range(len(grid))]
            first, last = ids[0] == 0, ids[0] == grid[0] - 1
            for a in range(1, len(grid)):
                first, last = first & (ids[a] == 0), last & (ids[a] == grid[a] - 1)

            @pl.when(first)
            def _():
                hosted.start(cins, couts, *sems)

        body(*ins, *outs, *scr)
        if h:
            @pl.when(last)
            def _():
                hosted.wait(cins, couts, *sems)

    res = pl.pallas_call(
        wrapped, name=name, grid=grid,
        in_specs=list(in_specs) + [ANY] * h, out_specs=list(out_specs) + [ANY] * h,
        out_shape=list(out_shape) + (hosted.out_shapes if h else []),
        scratch_shapes=list(scratch_shapes) + (hosted.scratch if h else []),
        input_output_aliases=aliases or {}, compiler_params=params,
    )(*operands, *(hosted.operands if h else []))
    return list(res[:n_out]), list(res[n_out:])


def _row_tile(R, C):
    if R * C <= (1 << 18):
        return R
    return max((d for d in range(8, 257, 8) if R % d == 0), default=R)


def rs_pair_sum(mine, recv_a, name):
    _, R, C = mine.shape
    tr = _row_tile(R, C)

    def body(g_ref, a_ref, o_ref):
        o_ref[0] = (g_ref[0].astype(F32) + a_ref[0].astype(F32)).astype(o_ref.dtype)

    blk = pl.BlockSpec((1, tr, C), lambda k, r: (k, r, 0))
    return pl.pallas_call(
        body, name=name, grid=(4, R // tr),
        in_specs=[blk, blk], out_specs=blk,
        out_shape=jax.ShapeDtypeStruct((4, R, C), BF),
    )(mine, recv_a)


def _cond_rows(call_ref, cctx_ref, z_ref, D):
    z_ref[...] = jnp.zeros_like(z_ref)
    for a in range(N_DEV):
        z_ref[a:a + 1, :] = call_ref[a][:, :D]
    z_ref[N_DEV:N_DEV + 1, :] = cctx_ref[...]


def mod_forward(call, c_ctx, w_loc, b_loc, name):
    D, cols = w_loc.shape

    def body(call_ref, cctx_ref, w_ref, b_ref, o_ref, z_ref):
        _cond_rows(call_ref, cctx_ref, z_ref, D)
        z = z_ref[...]
        s = z * _sigmoid(z)
        o_ref[...] = jnp.dot(s, w_ref[...], preferred_element_type=F32, precision=HIGHEST) + b_ref[...]

    return pl.pallas_call(
        body, name=name, out_shape=jax.ShapeDtypeStruct((16, cols), F32),
        scratch_shapes=[pltpu.VMEM((16, D), F32)],
        compiler_params=_params(vmem=VMEM_BIG),
    )(call, c_ctx, w_loc, b_loc)


def mod_backward(call, c_ctx, w_loc, dlat_loc, dctx_loc, gath, n_small_off, n_small, name):
    D, cols = w_loc.shape

    def body(call_ref, cctx_ref, w_ref, dlat_ref, dctx_ref, g_ref, gw_ref, pc_ref, small_ref, z_ref, dm_ref):
        _cond_rows(call_ref, cctx_ref, z_ref, D)
        z = z_ref[...]
        s = z * _sigmoid(z)
        dctx = dctx_ref[0:1, :]
        for a in range(1, N_DEV):
            dctx = dctx + dctx_ref[a:a + 1, :]
        dm_ref[...] = jnp.zeros_like(dm_ref)
        dm_ref[0:N_DEV, :] = dlat_ref[...]
        dm_ref[N_DEV:N_DEV + 1, :] = dctx
        gw_ref[...] = lax.dot_general(s, dm_ref[...], TN, preferred_element_type=F32, precision=HIGHEST)
        pc_ref[...] = lax.dot_general(dctx, w_ref[...], NT, preferred_element_type=F32, precision=HIGHEST)
        acc = g_ref[0][:, n_small_off:n_small_off + n_small]
        for a in range(1, N_DEV):
            acc = acc + g_ref[a][:, n_small_off:n_small_off + n_small]
        small_ref[...] = acc

    return pl.pallas_call(
        body, name=name,
        out_shape=(jax.ShapeDtypeStruct((D, cols), F32), jax.ShapeDtypeStruct((1, D), F32),
                   jax.ShapeDtypeStruct((1, n_small), F32)),
        scratch_shapes=[pltpu.VMEM((16, D), F32), pltpu.VMEM((16, cols), F32)],
        compiler_params=_params(vmem=VMEM_BIG),
    )(call, c_ctx, w_loc, dlat_loc, dctx_loc, gath)


def bmod_and_cctx_grad(gath, pcs, c_ctx, nd, name):
    D = c_ctx.shape[-1]

    def body(g_ref, pc_ref, cctx_ref, gb_ref, gc_ref):
        acc = g_ref[0][:, :nd] + g_ref[0][:, nd:2 * nd]
        for a in range(1, N_DEV):
            acc = acc + (g_ref[a][:, :nd] + g_ref[a][:, nd:2 * nd])
        gb_ref[...] = acc
        p = pc_ref[0]
        for a in range(1, N_DEV):
            p = p + pc_ref[a]
        z = cctx_ref[...]
        sg = _sigmoid(z)
        gc_ref[...] = p * (sg * (1.0 + z * (1.0 - sg)))

    return pl.pallas_call(
        body, name=name,
        out_shape=(jax.ShapeDtypeStruct((1, nd), F32), jax.ShapeDtypeStruct((1, D), F32)),
    )(gath, pcs, c_ctx)


def _mod_spec(D, which, nctx):
    return pl.BlockSpec((1, 3, D), lambda i: (jnp.where(i < nctx, 0, 3) + which, 0, 0))


def _acc_spec(D, nctx):
    return pl.BlockSpec((1, 8, D), lambda i: (jnp.where(i < nctx, 0, 1), 0, 0))


def _row(tm, n):
    return pl.BlockSpec((tm, n), lambda i: (i, 0))


def _two_stream_specs(tm, D, nctx):
    return [pl.BlockSpec((tm, D), lambda i: (jnp.minimum(i, nctx - 1), 0)),
            pl.BlockSpec((tm, D), lambda i: (jnp.maximum(i - nctx, 0), 0))]


def _final_norm_loss_backward(x, tgt, gg, i, dx_ref, acc_ref):
    @pl.when(i == 0)
    def _():
        acc_ref[...] = jnp.zeros_like(acc_ref)

    D = x.shape[1]
    r = lax.rsqrt(jnp.mean(x * x, axis=-1, keepdims=True) + EPS)
    xh = x * r
    e = xh * gg - tgt
    part = 0.5 * jnp.sum(jnp.mean(e * e, axis=-1, keepdims=True), axis=0, keepdims=True)
    dy = e * (1.0 / D)
    dyg = dy * gg
    dx_ref[...] = r * (dyg - xh * jnp.mean(dyg * xh, axis=-1, keepdims=True))
    acc_ref[0:1, :] += jnp.sum(dy * xh, axis=0, keepdims=True)
    acc_ref[1:2, :] += jnp.broadcast_to(part, (1, D))


def _hidden_chunks(F):
    step = 1024 if F % 256 == 0 else F
    return [(lo, min(lo + step, F)) for lo in range(0, F, step)]


def ffn_forward(srcs, mod6, which, g, wt, w_out, nctx, name, hosted=None, final=None):
    D = srcs[-1].shape[1]
    Tr = sum(s.shape[0] for s in srcs)
    F = wt.shape[0] // 2
    tm = ROW_TILE
    two = len(srcs) == 2
    nfin = 0 if final is None else 2

    def body(*refs):
        x_refs, fin_refs, rest = refs[:len(srcs)], refs[len(srcs):len(srcs) + nfin], refs[len(srcs) + nfin:]
        mod_ref, g_ref, wt_ref, wout_ref, xo_ref, hm_ref, ab_ref, h_ref, f_ref = rest[:9]
        x = jnp.where(pl.program_id(0) < nctx, x_refs[0][...], x_refs[1][...]) if two else x_refs[0][...]
        ms = mod_ref[0]
        shift, scale, gate = ms[0:1], ms[1:2], ms[2:3]
        r = lax.rsqrt(jnp.mean(x * x, axis=-1, keepdims=True) + EPS)
        hb = (((x * r) * g_ref[...]) * (1.0 + scale) + shift).astype(BF)
        hm_ref[...] = hb
        f = jnp.zeros((tm, D), F32)
        for lo, hi in _hidden_chunks(F):
            a = lax.dot_general(hb, wt_ref[lo:hi, :], NT, preferred_element_type=F32)
            b = lax.dot_general(hb, wt_ref[F + lo:F + hi, :], NT, preferred_element_type=F32)
            ab_ref[:, lo:hi] = a.astype(BF)
            ab_ref[:, F + lo:F + hi] = b.astype(BF)
            h = ((a * _sigmoid(a)) * b).astype(BF)
            h_ref[:, lo:hi] = h
            f = f + jnp.dot(h, wout_ref[lo:hi, :], preferred_element_type=F32)
        f_ref[...] = f.astype(BF)
        xo = x + (0.5 * gate) * f
        if final is None:
            xo_ref[...] = xo
        else:
            _final_norm_loss_backward(xo, fin_refs[0][...], fin_refs[1][...], pl.program_id(0), xo_ref, rest[9])

    src_specs = _two_stream_specs(tm, D, nctx) if two else [_row(tm, D)]
    fin = final is not None
    return _call(
        body, name=name, grid=(Tr // tm,),
        in_specs=src_specs + ([_row(tm, D), _resident((1, D))] if fin else [])
                 + [_mod_spec(D, which, nctx), _resident((1, D)), _resident(wt.shape), _resident(w_out.shape)],
        out_specs=[_row(tm, D), _row(tm, D), _row(tm, 2 * F), _row(tm, F), _row(tm, D)]
                  + ([pl.BlockSpec((8, D), lambda i: (0, 0))] if fin else []),
        out_shape=[jax.ShapeDtypeStruct((Tr, D), F32), jax.ShapeDtypeStruct((Tr, D), BF),
                   jax.ShapeDtypeStruct((Tr, 2 * F), BF), jax.ShapeDtypeStruct((Tr, F), BF),
                   jax.ShapeDtypeStruct((Tr, D), BF)] + ([jax.ShapeDtypeStruct((8, D), F32)] if fin else []),
        operands=[*srcs, *(final or ()), mod6, g, wt, w_out], hosted=hosted,
        params=_params(vmem=VMEM_BIG, sem=("arbitrary",)))


def ffn_backward_rows(dxo, srcs, mod6, which, g, ab, fo, wt, w_out, nctx, name, hosted=None):
    D = srcs[-1].shape[1]
    Tr = sum(s.shape[0] for s in srcs)
    Tl = srcs[-1].shape[0]
    F = wt.shape[0] // 2
    tm = ROW_TILE
    two = len(srcs) == 2

    def body(*refs):
        dxo_ref, x_refs = refs[0], refs[1:1 + len(srcs)]
        mod_ref, g_ref, ab_ref, fo_ref, wt_ref, wout_ref, dx_ref, dab_ref, df_ref, acc_ref = refs[1 + len(srcs):]
        i = pl.program_id(0)

        @pl.when((i == 0) | (i == nctx))
        def _():
            acc_ref[...] = jnp.zeros_like(acc_ref)

        dxo = dxo_ref[...]
        x = jnp.where(i < nctx, x_refs[0][...], x_refs[1][...]) if two else x_refs[0][...]
        ms = mod_ref[0]
        scale, gate = ms[1:2], ms[2:3]
        gg = g_ref[...]
        dgate = jnp.sum(dxo * fo_ref[...].astype(F32), axis=0, keepdims=True) * 0.5
        dfb = (dxo * (0.5 * gate)).astype(BF)
        df_ref[...] = dfb
        dhm = jnp.zeros((tm, D), F32)
        for lo, hi in _hidden_chunks(F):
            dh = lax.dot_general(dfb, wout_ref[lo:hi, :], NT, preferred_element_type=F32)
            a = ab_ref[:, lo:hi].astype(F32)
            b = ab_ref[:, F + lo:F + hi].astype(F32)
            sg = _sigmoid(a)
            da = ((dh * b) * (sg * (1.0 + a * (1.0 - sg)))).astype(BF)
            db = (dh * (a * sg)).astype(BF)
            dab_ref[:, lo:hi] = da
            dab_ref[:, F + lo:F + hi] = db
            dhm = dhm + jnp.dot(da, wt_ref[lo:hi, :], preferred_element_type=F32)
            dhm = dhm + jnp.dot(db, wt_ref[F + lo:F + hi, :], preferred_element_type=F32)
        r = lax.rsqrt(jnp.mean(x * x, axis=-1, keepdims=True) + EPS)
        xh = x * r
        dshift = jnp.sum(dhm, axis=0, keepdims=True)
        dscale = jnp.sum(dhm * (xh * gg), axis=0, keepdims=True)
        dxh_g = dhm * (1.0 + scale)
        dg = jnp.sum(dxh_g * xh, axis=0, keepdims=True)
        dxh = dxh_g * gg
        dx_ref[...] = dxo + r * (dxh - xh * jnp.mean(dxh * xh, axis=-1, keepdims=True))
        for k, val in enumerate((dshift, dscale, dgate, dg)):
            acc_ref[0, k:k + 1, :] += val

    src_specs = _two_stream_specs(tm, D, nctx) if two else [_row(tm, D)]
    dx_spec = pl.BlockSpec((tm, D), lambda i: (jnp.maximum(i - nctx, 0), 0))
    return _call(
        body, name=name, grid=(Tr // tm,),
        in_specs=[_row(tm, D)] + src_specs + [_mod_spec(D, which, nctx), _resident((1, D)), _row(tm, 2 * F), _row(tm, D),
                                              _resident(wt.shape), _resident(w_out.shape)],
        out_specs=[dx_spec, _row(tm, 2 * F), _row(tm, D), _acc_spec(D, nctx)],
        out_shape=[jax.ShapeDtypeStruct((Tl, D), F32), jax.ShapeDtypeStruct((Tr, 2 * F), BF),
                   jax.ShapeDtypeStruct((Tr, D), BF), jax.ShapeDtypeStruct((2, 8, D), F32)],
        operands=[dxo, *srcs, mod6, g, ab, fo, wt, w_out], hosted=hosted,
        params=_params(vmem=VMEM_BIG, sem=("arbitrary",)))


def _token_tile(T):
    return _pick(T, (2048, 1408, 1024, 768, 512, 256, 128))


def tn_matmul(a, b, name, hosted=None):
    T, K = a.shape
    N = b.shape[1]
    tk = _pick(K, (1024, 1408, 1664, 768, 512, 384, 256, 128))
    tn = _pick(N, (1024, 1408, 1664, 768, 512, 384, 256, 128))
    tt = _token_tile(T)
    nt = T // tt

    def body(a_ref, b_ref, o_ref, acc_ref):
        t = pl.program_id(2)

        @pl.when(t == 0)
        def _():
            acc_ref[...] = jnp.zeros_like(acc_ref)

        acc_ref[...] += lax.dot_general(a_ref[...], b_ref[...], TN, preferred_element_type=F32)

        @pl.when(t == nt - 1)
        def _():
            o_ref[...] = acc_ref[...].astype(BF)

    (out,), exchanged = _call(
        body, name=name, grid=(K // tk, N // tn, nt),
        in_specs=[pl.BlockSpec((tt, tk), lambda k, n, t: (t, k)), pl.BlockSpec((tt, tn), lambda k, n, t: (t, n))],
        out_specs=[pl.BlockSpec((tk, tn), lambda k, n, t: (k, n))],
        out_shape=[jax.ShapeDtypeStruct((K, N), BF)],
        scratch_shapes=[pltpu.VMEM((tk, tn), F32)],
        operands=[a, b], hosted=hosted,
        params=_params(vmem=VMEM_BIG, sem=("arbitrary", "arbitrary", "arbitrary")))
    return out, exchanged


def _rope_apply(y, cos, s_next, s_prev):
    return y * cos + pltpu.roll(y, HEAD_DIM - 32, 1) * s_next + pltpu.roll(y, 32, 1) * s_prev


def _rope_transpose(dz, cos, s_next, s_prev):
    return dz * cos + pltpu.roll(dz * s_next, 32, 1) + pltpu.roll(dz * s_prev, HEAD_DIM - 32, 1)


def proj_forward(xa, mod6, g, w_in, qg, kg, tabs, offs, nctx, name, hosted=None):
    Tr, D = xa.shape
    P = w_in.shape[1]
    tm = ROW_TILE
    qo, ko = offs["q"], offs["k"]
    qw, kw = N_Q_HEADS * HEAD_DIM, N_KV_HEADS * HEAD_DIM
    scale_q = HEAD_DIM ** -0.5 * LOG2E

    def body(x_ref, mod_ref, g_ref, w_ref, qg_ref, kg_ref, tab_ref, hx_ref, pr_ref, q_ref, k_ref):
        x = x_ref[...]
        ms = mod_ref[0]
        shift, scale = ms[0:1], ms[1:2]
        r = lax.rsqrt(jnp.mean(x * x, axis=-1, keepdims=True) + EPS)
        hb = (((x * r) * g_ref[...]) * (1.0 + scale) + shift).astype(BF)
        hx_ref[...] = hb
        pr = jnp.dot(hb, w_ref[...], preferred_element_type=F32)
        pr_ref[...] = pr.astype(BF)
        cos, s_next, s_prev = tab_ref[0], tab_ref[1], tab_ref[2]

        def head(v, gain):
            n = v * lax.rsqrt(jnp.mean(v * v, axis=-1, keepdims=True) + EPS)
            return _rope_apply(n * gain, cos, s_next, s_prev)

        for h in range(N_Q_HEADS):
            lo = qo + h * HEAD_DIM
            q_ref[:, h * HEAD_DIM:(h + 1) * HEAD_DIM] = (head(pr[:, lo:lo + HEAD_DIM], qg_ref[...]) * scale_q).astype(BF)
        for h in range(N_KV_HEADS):
            lo = ko + h * HEAD_DIM
            k_ref[:, h * HEAD_DIM:(h + 1) * HEAD_DIM] = head(pr[:, lo:lo + HEAD_DIM], kg_ref[...]).astype(BF)

    return _call(
        body, name=name, grid=(Tr // tm,),
        in_specs=[_row(tm, D), _mod_spec(D, 1, nctx), _resident((1, D)), _resident(w_in.shape),
                  _resident((1, HEAD_DIM)), _resident((1, HEAD_DIM)),
                  pl.BlockSpec((3, tm, HEAD_DIM), lambda i: (0, i, 0))],
        out_specs=[_row(tm, D), _row(tm, P), _row(tm, qw), _row(tm, kw)],
        out_shape=[jax.ShapeDtypeStruct((Tr, D), BF), jax.ShapeDtypeStruct((Tr, P), BF),
                   jax.ShapeDtypeStruct((Tr, qw), BF), jax.ShapeDtypeStruct((Tr, kw), BF)],
        operands=[xa, mod6, g, w_in, qg, kg, tabs], hosted=hosted,
        params=_params(vmem=VMEM_BIG, sem=("arbitrary",)))


def _shifted(u, first_row, last_row):
    T = u.shape[0]
    prev = jnp.where(first_row, 0.0, pltpu.roll(u, 1, 0))
    nxt = jnp.where(last_row, 0.0, pltpu.roll(u, T - 1, 0))
    return prev, nxt


def conv_forward(proj, conv_w, offs, Tc, name):
    Ta = proj.shape[0]
    T = Ta - Tc
    Dc = conv_w.shape[1]
    cb = offs["cv"] // 384

    def body(p_ref, w_ref, y_ref):
        rows = lax.broadcasted_iota(jnp.int32, (T, 128), 0)
        u = p_ref[pl.ds(Tc, T), 128:256].astype(F32) * p_ref[pl.ds(Tc, T), 256:384].astype(F32)
        prev, nxt = _shifted(u, rows == 0, rows == T - 1)
        w = w_ref[...]
        cv = prev * w[0:1] + u * w[1:2] + nxt * w[2:3]
        y_ref[...] = (p_ref[pl.ds(Tc, T), 0:128].astype(F32) * cv).astype(BF)

    return pl.pallas_call(
        body, name=name, grid=(Dc // 128,),
        in_specs=[pl.BlockSpec((Ta, 384), lambda j: (0, cb + j)), pl.BlockSpec((CONV_TAPS, 128), lambda j: (0, j))],
        out_specs=pl.BlockSpec((T, 128), lambda j: (0, j)),
        out_shape=jax.ShapeDtypeStruct((T, Dc), BF),
        compiler_params=_params(vmem=VMEM_BIG, sem=("arbitrary",)),
    )(proj, conv_w)


def conv_backward(dproj, dy, proj, conv_w, offs, Tc, name):
    Ta = proj.shape[0]
    T = Ta - Tc
    Dc = conv_w.shape[1]
    cb = offs["cv"] // 384

    def body(dp_any, dy_ref, p_ref, w_ref, o_ref, dw_ref):
        rows = lax.broadcasted_iota(jnp.int32, (T, 128), 0)
        first, last = rows == 0, rows == T - 1
        bg = p_ref[pl.ds(Tc, T), 0:128].astype(F32)
        cg = p_ref[pl.ds(Tc, T), 128:256].astype(F32)
        vc = p_ref[pl.ds(Tc, T), 256:384].astype(F32)
        dy = dy_ref[...].astype(F32)
        u = cg * vc
        prev, nxt = _shifted(u, first, last)
        w = w_ref[...]
        cv = prev * w[0:1] + u * w[1:2] + nxt * w[2:3]
        o_ref[pl.ds(0, Tc), :] = jnp.zeros((Tc, 384), BF)
        o_ref[pl.ds(Tc, T), 0:128] = (dy * cv).astype(BF)
        dcv = dy * bg
        dprev, dnxt = _shifted(dcv, first, last)
        du = dnxt * w[0:1] + dcv * w[1:2] + dprev * w[2:3]
        o_ref[pl.ds(Tc, T), 128:256] = (du * vc).astype(BF)
        o_ref[pl.ds(Tc, T), 256:384] = (du * cg).astype(BF)
        dw_ref[...] = jnp.zeros_like(dw_ref)
        for k, tap in enumerate((prev, u, nxt)):
            dw_ref[k:k + 1, :] = jnp.sum(dcv * tap, axis=0, keepdims=True)

    blk = pl.BlockSpec((Ta, 384), lambda j: (0, cb + j))
    return pl.pallas_call(
        body, name=name, grid=(Dc // 128,),
        in_specs=[ANY, pl.BlockSpec((T, 128), lambda j: (0, j)), blk, pl.BlockSpec((CONV_TAPS, 128), lambda j: (0, j))],
        out_specs=[blk, pl.BlockSpec((8, 128), lambda j: (0, j))],
        out_shape=[jax.ShapeDtypeStruct(dproj.shape, BF), jax.ShapeDtypeStruct((8, Dc), F32)],
        input_output_aliases={0: 0},
        compiler_params=_params(vmem=VMEM_BIG, sem=("arbitrary",)),
    )(dproj, dy, proj, conv_w)


def _kv_chunk(Ta):
    return _pick(Ta, (768, 512, 384, 256, 128))


def _stack_heads(v):
    return jnp.concatenate([v[:, h * HEAD_DIM:(h + 1) * HEAD_DIM] for h in range(GROUP)], axis=0)


def attention_forward(q, k, proj, offs, Tc, name, hosted=None):
    Ta = k.shape[0]
    T = Ta - Tc
    tq = ROW_TILE
    kc = _kv_chunk(Ta)
    nkv = Ta // kc
    gw = GROUP * HEAD_DIM
    vblk = offs["v"] // HEAD_DIM
    qoff = Tc // tq
    n = GROUP * tq

    def body(q_ref, k_ref, v_ref, o_ref, lse_ref, vx_ref, qs_ref, s0_ref, s1_ref, m_ref, acc_ref):
        @pl.when(pl.program_id(1) == 0)
        def _():
            vx_ref[:, 0:HEAD_DIM] = v_ref[...]
            vx_ref[:, HEAD_DIM:2 * HEAD_DIM] = jnp.ones((Ta, HEAD_DIM), BF)

        qs_ref[...] = _stack_heads(q_ref[...])
        m_ref[...] = jnp.full((n, 1), -1e30, F32)
        acc_ref[...] = jnp.zeros((n, 2 * HEAD_DIM), F32)

        def rows(c):
            return pl.ds(pl.multiple_of(c * kc, kc), kc)

        def logits(c, dst):
            dst[...] = lax.dot_general(qs_ref[...], k_ref[rows(c), :], NT, preferred_element_type=F32)

        def consume(src, c):
            s = src[...]
            m_prev = m_ref[...]
            m_new = jnp.maximum(m_prev, jnp.max(s, axis=-1, keepdims=True))
            p = jnp.exp2(s - m_new).astype(BF)
            acc_ref[...] = jnp.exp2(m_prev - m_new) * acc_ref[...] + jnp.dot(p, vx_ref[rows(c), :], preferred_element_type=F32)
            m_ref[...] = m_new

        def pair(i, carry):
            logits(2 * i + 1, s1_ref)
            consume(s0_ref, 2 * i)
            logits(2 * i + 2, s0_ref)
            consume(s1_ref, 2 * i + 1)
            return carry

        logits(0, s0_ref)
        if nkv % 2:
            lax.fori_loop(0, nkv // 2, pair, 0)
            consume(s0_ref, nkv - 1)
        else:
            lax.fori_loop(0, nkv // 2 - 1, pair, 0)
            logits(nkv - 1, s1_ref)
            consume(s0_ref, nkv - 2)
            consume(s1_ref, nkv - 1)
        acc = acc_ref[...]
        l = acc[:, HEAD_DIM:HEAD_DIM + 1]
        o = acc[:, 0:HEAD_DIM] / l
        lse = m_ref[...] + jnp.log2(l)
        for h in range(GROUP):
            o_ref[:, h * HEAD_DIM:(h + 1) * HEAD_DIM] = o[h * tq:(h + 1) * tq].astype(BF)
            lse_ref[0, :, h:h + 1] = lse[h * tq:(h + 1) * tq]

    return _call(
        body, name=name, grid=(N_KV_HEADS, T // tq),
        in_specs=[pl.BlockSpec((tq, gw), lambda j, i: (i + qoff, j)),
                  pl.BlockSpec((Ta, HEAD_DIM), lambda j, i: (0, j)),
                  pl.BlockSpec((Ta, HEAD_DIM), lambda j, i: (0, vblk + j))],
        out_specs=[pl.BlockSpec((tq, gw), lambda j, i: (i, j)),
                   pl.BlockSpec((1, tq, GROUP), lambda j, i: (j, i, 0))],
        out_shape=[jax.ShapeDtypeStruct((T, N_Q_HEADS * HEAD_DIM), BF),
                   jax.ShapeDtypeStruct((N_KV_HEADS, T, GROUP), F32)],
        scratch_shapes=[pltpu.VMEM((Ta, 2 * HEAD_DIM), BF), pltpu.VMEM((n, HEAD_DIM), BF), pltpu.VMEM((n, kc), F32),
                        pltpu.VMEM((n, kc), F32), pltpu.VMEM((n, 1), F32), pltpu.VMEM((n, 2 * HEAD_DIM), F32)],
        operands=[q, k, proj], hosted=hosted,
        params=_params(vmem=VMEM_BIG, sem=("arbitrary", "arbitrary")))


def _norm_rope_backward(dz, raw, gg, cos, s_next, s_prev):
    r = lax.rsqrt(jnp.mean(raw * raw, axis=-1, keepdims=True) + EPS)
    n = raw * r
    dy = _rope_transpose(dz, cos, s_next, s_prev)
    dn = dy * gg
    return r * (dn - n * jnp.mean(dn * n, axis=-1, keepdims=True)), jnp.sum(dy * n, axis=0, keepdims=True)


def attention_backward(dproj, q, k, proj, o, lse, do, qgain, tabs, offs, Tc, name, hosted=None):
    Ta = k.shape[0]
    tq = ROW_TILE
    nctx = Tc // tq
    kc = _kv_chunk(Ta)
    gw = GROUP * HEAD_DIM
    vblk = offs["v"] // HEAD_DIM
    qblk = offs["q"] // gw
    zscale = HEAD_DIM ** -0.5

    def body(dp_any, q_ref, k_ref, v_ref, o_ref, lse_ref, do_ref, raw_ref, g_ref, tab_ref, dqr_ref, dk_ref, dv_ref, dg_ref):
        j, i = pl.program_id(0), pl.program_id(1)

        @pl.when(i == 0)
        def _():
            dk_ref[...] = jnp.zeros_like(dk_ref)
            dv_ref[...] = jnp.zeros_like(dv_ref)

        @pl.when((i == 0) & (j == 0))
        def _():
            dg_ref[...] = jnp.zeros_like(dg_ref)

        @pl.when(i < nctx)
        def _():
            dqr_ref[...] = jnp.zeros_like(dqr_ref)

        @pl.when(i >= nctx)
        def _():
            qs = _stack_heads(q_ref[...])
            dob = do_ref[...]
            dos = _stack_heads(dob)
            delta = jnp.concatenate(
                [jnp.sum(dob[:, h * HEAD_DIM:(h + 1) * HEAD_DIM].astype(F32)
                         * o_ref[:, h * HEAD_DIM:(h + 1) * HEAD_DIM].astype(F32), axis=-1, keepdims=True)
                 for h in range(GROUP)], axis=0)
            lse = jnp.concatenate([lse_ref[0, :, h:h + 1] for h in range(GROUP)], axis=0)

            def step(c, dq):
                rows = pl.ds(pl.multiple_of(c * kc, kc), kc)
                kk = k_ref[rows, :]
                vv = v_ref[rows, :]
                s = lax.dot_general(qs, kk, NT, preferred_element_type=F32)
                p = jnp.exp2(s - lse)
                dp = lax.dot_general(dos, vv, NT, preferred_element_type=F32)
                ds = (p * (dp - delta)).astype(BF)
                dv_ref[rows, :] += lax.dot_general(p.astype(BF), dos, TN, preferred_element_type=F32)
                dk_ref[rows, :] += lax.dot_general(ds, qs, TN, preferred_element_type=F32)
                return dq + jnp.dot(ds, kk, preferred_element_type=F32)

            dq = lax.fori_loop(0, Ta // kc, step, jnp.zeros((GROUP * tq, HEAD_DIM), F32))
            per_head = lambda t: jnp.concatenate([t] * GROUP, axis=0)
            dr, dg = _norm_rope_backward(dq * zscale, _stack_heads(raw_ref[...]).astype(F32), g_ref[...],
                                         per_head(tab_ref[0]), per_head(tab_ref[1]), per_head(tab_ref[2]))
            for h in range(GROUP):
                dqr_ref[:, h * HEAD_DIM:(h + 1) * HEAD_DIM] = dr[h * tq:(h + 1) * tq].astype(BF)
            dg_ref[0:1, :] += dg

    lat = lambda j, i: (jnp.maximum(i - nctx, 0), j)
    (dproj, dk, dv, dqg), exchanged = _call(
        body, name=name, grid=(N_KV_HEADS, Ta // tq),
        in_specs=[ANY, pl.BlockSpec((tq, gw), lambda j, i: (i, j)),
                  pl.BlockSpec((Ta, HEAD_DIM), lambda j, i: (0, j)),
                  pl.BlockSpec((Ta, HEAD_DIM), lambda j, i: (0, vblk + j)),
                  pl.BlockSpec((tq, gw), lat),
                  pl.BlockSpec((1, tq, GROUP), lambda j, i: (j, jnp.maximum(i - nctx, 0), 0)),
                  pl.BlockSpec((tq, gw), lat),
                  pl.BlockSpec((tq, gw), lambda j, i: (i, qblk + j)),
                  pl.BlockSpec((1, HEAD_DIM), lambda j, i: (0, 0)),
                  pl.BlockSpec((3, tq, HEAD_DIM), lambda j, i: (0, i, 0))],
        out_specs=[pl.BlockSpec((tq, gw), lambda j, i: (i, qblk + j)),
                   pl.BlockSpec((Ta, HEAD_DIM), lambda j, i: (0, j)),
                   pl.BlockSpec((Ta, HEAD_DIM), lambda j, i: (0, j)),
                   pl.BlockSpec((8, HEAD_DIM), lambda j, i: (0, 0))],
        out_shape=[jax.ShapeDtypeStruct(dproj.shape, BF),
                   jax.ShapeDtypeStruct((Ta, N_KV_HEADS * HEAD_DIM), F32),
                   jax.ShapeDtypeStruct((Ta, N_KV_HEADS * HEAD_DIM), F32),
                   jax.ShapeDtypeStruct((8, HEAD_DIM), F32)],
        operands=[dproj, q, k, proj, o, lse, do, proj, qgain, tabs], hosted=hosted, aliases={0: 0},
        params=_params(vmem=VMEM_BIG, sem=("arbitrary", "arbitrary")))
    return (dproj, dk, dv, dqg), exchanged


def kv_backward(dproj, dk, dv, proj, gain, tabs, offs, name):
    Ta = proj.shape[0]
    tm = _pick(Ta, (768, 512, ROW_TILE))
    kw = N_KV_HEADS * HEAD_DIM
    cb = offs["k"] // (2 * kw)
    kb = offs["k"] // kw
    zscale = 1.0 / LOG2E

    def body(dp_any, dk_ref, dv_ref, raw_ref, g_ref, tab_ref, o_ref, dg_ref):
        @pl.when(pl.program_id(0) == 0)
        def _():
            dg_ref[...] = jnp.zeros_like(dg_ref)

        cos, s_next, s_prev = tab_ref[0], tab_ref[1], tab_ref[2]
        dg = jnp.zeros((1, HEAD_DIM), F32)
        for h in range(N_KV_HEADS):
            sl = slice(h * HEAD_DIM, (h + 1) * HEAD_DIM)
            dr, dgh = _norm_rope_backward(dk_ref[:, sl] * zscale, raw_ref[:, sl].astype(F32), g_ref[...], cos, s_next, s_prev)
            o_ref[:, sl] = dr.astype(BF)
            dg = dg + dgh
        o_ref[:, kw:2 * kw] = dv_ref[...].astype(BF)
        dg_ref[0:1, :] += dg

    return pl.pallas_call(
        body, name=name, grid=(Ta // tm,),
        in_specs=[ANY, _row(tm, kw), _row(tm, kw), pl.BlockSpec((tm, kw), lambda i: (i, kb)),
                  _resident((1, HEAD_DIM)), pl.BlockSpec((3, tm, HEAD_DIM), lambda i: (0, i, 0))],
        out_specs=[pl.BlockSpec((tm, 2 * kw), lambda i: (i, cb)), pl.BlockSpec((8, HEAD_DIM), lambda i: (0, 0))],
        out_shape=[jax.ShapeDtypeStruct(dproj.shape, BF), jax.ShapeDtypeStruct((8, HEAD_DIM), F32)],
        input_output_aliases={0: 0},
        compiler_params=_params(sem=("arbitrary",)),
    )(dproj, dk, dv, proj, gain, tabs)


def merge_forward(x1, mod6, yc, o, proj, w_bc, w_ba, w_o, offs, Tc, name):
    T, D = yc.shape[0], x1.shape[1]
    tm = ROW_TILE
    roff = Tc // tm
    gb = offs["gt"] // (2 * D)

    def body(x_ref, mod_ref, yc_ref, o_ref, gt_ref, wbc_ref, wba_ref, wo_ref, xo_ref, pc_ref, pa_ref, m_ref, z_ref):
        gate = mod_ref[0][2:3]
        pc = jnp.dot(yc_ref[...], wbc_ref[...], preferred_element_type=F32)
        pa = jnp.dot(o_ref[...], wba_ref[...], preferred_element_type=F32)
        pc_ref[...] = pc.astype(BF)
        pa_ref[...] = pa.astype(BF)
        mb = (_sigmoid(gt_ref[:, 0:D].astype(F32)) * pc + _sigmoid(gt_ref[:, D:2 * D].astype(F32)) * pa).astype(BF)
        m_ref[...] = mb
        z = jnp.dot(mb, wo_ref[...], preferred_element_type=F32)
        z_ref[...] = z.astype(BF)
        xo_ref[...] = x_ref[...] + gate * z

    return pl.pallas_call(
        body, name=name, grid=(T // tm,),
        in_specs=[pl.BlockSpec((tm, D), lambda i: (i + roff, 0)), _mod_spec(D, 1, 0), _row(tm, yc.shape[1]), _row(tm, o.shape[1]),
                  pl.BlockSpec((tm, 2 * D), lambda i: (i + roff, gb)),
                  _resident(w_bc.shape), _resident(w_ba.shape), _resident(w_o.shape)],
        out_specs=[_row(tm, D)] * 5,
        out_shape=[jax.ShapeDtypeStruct((T, D), F32)] + [jax.ShapeDtypeStruct((T, D), BF)] * 4,
        compiler_params=_params(vmem=VMEM_BIG, sem=("arbitrary",)),
    )(x1, mod6, yc, o, proj, w_bc, w_ba, w_o)


def merge_backward_rows(dx2, mod6, z, pc, pa, proj, w_bc, w_ba, w_o, offs, Tc, name):
    T, D = dx2.shape
    Ta, P = proj.shape
    tm = ROW_TILE
    nctx = Tc // tm
    gb = offs["gt"] // (2 * D)
    dcw, dqw = w_bc.shape[0], w_ba.shape[0]

    def body(dx_ref, mod_ref, z_ref, pc_ref, pa_ref, gt_ref, wbc_ref, wba_ref, wo_ref,
             dgt_ref, dg_ref, dpc_ref, dpa_ref, dyc_ref, do_ref, acc_ref):
        i = pl.program_id(0)

        @pl.when(i == 0)
        def _():
            acc_ref[...] = jnp.zeros_like(acc_ref)

        @pl.when(i < nctx)
        def _():
            dgt_ref[...] = jnp.zeros_like(dgt_ref)

        @pl.when(i >= nctx)
        def _():
            gate = mod_ref[0][2:3]
            dx = dx_ref[...]
            acc_ref[0:1, :] += jnp.sum(dx * z_ref[...].astype(F32), axis=0, keepdims=True)
            dgb = (dx * gate).astype(BF)
            dg_ref[...] = dgb
            dm = lax.dot_general(dgb, wo_ref[...], NT, preferred_element_type=F32)
            sc = _sigmoid(gt_ref[:, 0:D].astype(F32))
            sa = _sigmoid(gt_ref[:, D:2 * D].astype(F32))
            pc = pc_ref[...].astype(F32)
            pa = pa_ref[...].astype(F32)
            dpc = (dm * sc).astype(BF)
            dpa = (dm * sa).astype(BF)
            dpc_ref[...] = dpc
            dpa_ref[...] = dpa
            dgt_ref[:, 0:D] = ((dm * pc) * (sc * (1.0 - sc))).astype(BF)
            dgt_ref[:, D:2 * D] = ((dm * pa) * (sa * (1.0 - sa))).astype(BF)
            dyc_ref[...] = lax.dot_general(dpc, wbc_ref[...], NT, preferred_element_type=F32).astype(BF)
            do_ref[...] = lax.dot_general(dpa, wba_ref[...], NT, preferred_element_type=F32).astype(BF)

    lat = lambda n: pl.BlockSpec((tm, n), lambda i: (jnp.maximum(i - nctx, 0), 0))
    return pl.pallas_call(
        body, name=name, grid=(Ta // tm,),
        in_specs=[lat(D), _mod_spec(D, 1, 0), lat(D), lat(D), lat(D),
                  pl.BlockSpec((tm, 2 * D), lambda i: (i, gb)),
                  _resident(w_bc.shape), _resident(w_ba.shape), _resident(w_o.shape)],
        out_specs=[pl.BlockSpec((tm, 2 * D), lambda i: (i, gb)), lat(D), lat(D), lat(D), lat(dcw), lat(dqw),
                   pl.BlockSpec((8, D), lambda i: (0, 0))],
        out_shape=[jax.ShapeDtypeStruct((Ta, P), BF)] + [jax.ShapeDtypeStruct((T, D), BF)] * 3
                  + [jax.ShapeDtypeStruct((T, dcw), BF), jax.ShapeDtypeStruct((T, dqw), BF), jax.ShapeDtypeStruct((8, D), F32)],
        compiler_params=_params(vmem=VMEM_BIG, sem=("arbitrary",)),
    )(dx2, mod6, z, pc, pa, proj, w_bc, w_ba, w_o)


def proj_backward_rows(dproj, dres, xa, mod6, g, w_in, nctx, name, hosted=None):
    Tr, D = xa.shape
    P = w_in.shape[1]
    tm = ROW_TILE

    def body(dp_ref, dres_ref, x_ref, mod_ref, g_ref, w_ref, dx_ref, acc_ref):
        i = pl.program_id(0)

        @pl.when((i == 0) | (i == nctx))
        def _():
            acc_ref[...] = jnp.zeros_like(acc_ref)

        x = x_ref[...]
        scale = mod_ref[0][1:2]
        gg = g_ref[...]
        dhm = lax.dot_general(dp_ref[...], w_ref[...], NT, preferred_element_type=F32)
        r = lax.rsqrt(jnp.mean(x * x, axis=-1, keepdims=True) + EPS)
        xh = x * r
        dshift = jnp.sum(dhm, axis=0, keepdims=True)
        dscale = jnp.sum(dhm * (xh * gg), axis=0, keepdims=True)
        dxh_g = dhm * (1.0 + scale)
        dg = jnp.sum(dxh_g * xh, axis=0, keepdims=True)
        dxh = dxh_g * gg
        res = jnp.where(i < nctx, 0.0, dres_ref[...])
        dx_ref[...] = res + r * (dxh - xh * jnp.mean(dxh * xh, axis=-1, keepdims=True))
        for k, val in enumerate((dshift, dscale, dg)):
            acc_ref[0, k:k + 1, :] += val

    return _call(
        body, name=name, grid=(Tr // tm,),
        in_specs=[_row(tm, P), pl.BlockSpec((tm, D), lambda i: (jnp.maximum(i - nctx, 0), 0)), _row(tm, D),
                  _mod_spec(D, 1, nctx), _resident((1, D)), _resident(w_in.shape)],
        out_specs=[_row(tm, D), _acc_spec(D, nctx)],
        out_shape=[jax.ShapeDtypeStruct((Tr, D), F32), jax.ShapeDtypeStruct((2, 8, D), F32)],
        operands=[dproj, dres, xa, mod6, g, w_in], hosted=hosted,
        params=_params(vmem=VMEM_BIG, sem=("arbitrary",)))


def _adam_update(w, g, m, v):
    c1 = 1.0 - ADAM_B1 ** ADAM_STEP
    c2 = 1.0 - ADAM_B2 ** ADAM_STEP
    m = ADAM_B1 * m + (1.0 - ADAM_B1) * g
    v = ADAM_B2 * v + (1.0 - ADAM_B2) * (g * g)
    return -ADAM_LR * ((m / c1) / (jnp.sqrt(v / c2) + ADAM_EPS) + ADAM_WD * w), m, v


def adamw(w, g, m, v, name):
    R, C = w.shape
    tr = _row_tile(R, C)

    def body(w_ref, g_ref, m_ref, v_ref, d_ref, nm_ref, nv_ref):
        d_ref[...], nm_ref[...], nv_ref[...] = _adam_update(w_ref[...], g_ref[...], m_ref[...], v_ref[...])

    blk = pl.BlockSpec((tr, C), lambda i: (i, 0))
    return pl.pallas_call(
        body, name=name, grid=(R // tr,),
        in_specs=[blk] * 4, out_specs=[blk] * 3,
        out_shape=[jax.ShapeDtypeStruct((R, C), F32)] * 3,
        compiler_params=_params(vmem=VMEM_BIG, sem=("parallel",)),
    )(w, g, m, v)


def adamw_summed(recv, w, m, v, name):
    R, C = w.shape
    tr = _row_tile(R, C)

    def body(r_ref, w_ref, m_ref, v_ref, g_ref, d_ref, nm_ref, nv_ref):
        g = r_ref[0].astype(F32)
        for a in range(1, N_DEV):
            g = g + r_ref[a].astype(F32)
        g_ref[...] = g
        d_ref[...], nm_ref[...], nv_ref[...] = _adam_update(w_ref[...], g, m_ref[...], v_ref[...])

    blk = pl.BlockSpec((tr, C), lambda i: (i, 0))
    return pl.pallas_call(
        body, name=name, grid=(R // tr,),
        in_specs=[pl.BlockSpec((N_DEV, tr, C), lambda i: (0, i, 0)), blk, blk, blk], out_specs=[blk] * 4,
        out_shape=[jax.ShapeDtypeStruct((R, C), F32)] * 4,
        compiler_params=_params(vmem=VMEM_BIG, sem=("parallel",)),
    )(recv, w, m, v)


def adamw_reduced(p_own, recv_b, w, m, v, name):
    R, C = w.shape
    tr = _row_tile(R, C)

    def body(p_ref, b_ref, w_ref, m_ref, v_ref, g_ref, d_ref, nm_ref, nv_ref):
        g = p_ref[...].astype(F32)
        for j in range(3):
            g = g + b_ref[j].astype(F32)
        g_ref[...] = g
        d_ref[...], nm_ref[...], nv_ref[...] = _adam_update(w_ref[...], g, m_ref[...], v_ref[...])

    blk = pl.BlockSpec((tr, C), lambda r: (r, 0))
    return pl.pallas_call(
        body, name=name, grid=(R // tr,),
        in_specs=[blk, pl.BlockSpec((3, tr, C), lambda r: (0, r, 0)), blk, blk, blk], out_specs=[blk] * 4,
        out_shape=[jax.ShapeDtypeStruct((R, C), F32)] * 4,
        compiler_params=_params(vmem=VMEM_BIG, sem=("parallel",)),
    )(p_own, recv_b, w, m, v)


def _rope_tables(T, Tc):
    rows = T // GRID_W
    n_freq = HEAD_DIM // 4
    inv = ROPE_THETA ** (-jnp.arange(n_freq, dtype=F32) / n_freq)
    ang_r = jnp.arange(rows).astype(F32)[:, None] * inv
    ang_c = jnp.arange(GRID_W).astype(F32)[:, None] * inv
    per_row = lambda a: jnp.broadcast_to(a[:, None, :], (rows, GRID_W, n_freq)).reshape(T, n_freq)
    per_col = lambda a: jnp.broadcast_to(a[None, :, :], (rows, GRID_W, n_freq)).reshape(T, n_freq)
    cr, sr = per_row(jnp.cos(ang_r)), per_row(jnp.sin(ang_r))
    cc, sc = per_col(jnp.cos(ang_c)), per_col(jnp.sin(ang_c))
    zero = jnp.zeros_like(sr)
    cos = jnp.concatenate([cr, cr, cc, cc], axis=1)
    s_next = jnp.concatenate([-sr, zero, -sc, zero], axis=1)
    s_prev = jnp.concatenate([zero, sr, zero, sc], axis=1)
    lat = jnp.stack([cos, s_next, s_prev])
    ctx = jnp.stack([jnp.ones((Tc, HEAD_DIM), F32), jnp.zeros((Tc, HEAD_DIM), F32), jnp.zeros((Tc, HEAD_DIM), F32)])
    return jnp.concatenate([ctx, lat], axis=1)


BIG = ("ffn1_w_in", "ffn1_w_out", "w_in", "w_branch_conv", "w_branch_attn", "w_out", "ffn2_w_in", "ffn2_w_out")


def _regroup_w_in(stacked, D, Dc, qw, kw):
    w = stacked.transpose(1, 0, 2).reshape(D, -1)
    o = 0
    parts = {}
    for nme, wd in (("bg", Dc), ("cg", Dc), ("vc", Dc), ("q", qw), ("k", kw), ("v", kw), ("gt", 2 * D)):
        parts[nme] = w[:, o:o + wd]
        o += wd
    nb = Dc // 128
    cv = jnp.stack([parts[n].reshape(D, nb, 128) for n in ("bg", "cg", "vc")], axis=2).reshape(D, 3 * Dc)
    return jnp.concatenate([cv, parts["q"], parts["gt"], parts["k"], parts["v"]], axis=1)


def _ungroup_w_in_grad(gt_, D, Dc, qw, kw):
    nb = Dc // 128
    cv = gt_[:3 * Dc].reshape(nb, 3, 128, D)
    o = 3 * Dc
    q = gt_[o:o + qw]
    gt = gt_[o + qw:o + qw + 2 * D]
    k = gt_[o + qw + 2 * D:o + qw + 2 * D + kw]
    v = gt_[o + qw + 2 * D + kw:]
    nat = jnp.concatenate([cv[:, 0].reshape(Dc, D), cv[:, 1].reshape(Dc, D), cv[:, 2].reshape(Dc, D), q, k, v, gt], axis=0)
    return nat.reshape(N_DEV, -1, D)


def kernel(x, c, ctx, c_ctx, w_mod, b_mod, norm1_g, norm2_g, norm3_g, ffn1_w_in, ffn1_w_out, w_in, conv_w, q_norm_g, k_norm_g, w_branch_conv, w_branch_attn, w_out, ffn2_w_in, ffn2_w_out, final_g, loss_target, m_c_ctx, m_w_mod, m_b_mod, m_norm1_g, m_norm2_g, m_norm3_g, m_ffn1_w_in, m_ffn1_w_out, m_w_in, m_conv_w, m_q_norm_g, m_k_norm_g, m_w_branch_conv, m_w_branch_attn, m_w_out, m_ffn2_w_in, m_ffn2_w_out, m_final_g, v_c_ctx, v_w_mod, v_b_mod, v_norm1_g, v_norm2_g, v_norm3_g, v_ffn1_w_in, v_ffn1_w_out, v_w_in, v_conv_w, v_q_norm_g, v_k_norm_g, v_w_branch_conv, v_w_branch_attn, v_w_out, v_ffn2_w_in, v_ffn2_w_out, v_final_g):
    weights = dict(c_ctx=c_ctx, w_mod=w_mod, b_mod=b_mod, norm1_g=norm1_g, norm2_g=norm2_g, norm3_g=norm3_g,
                   ffn1_w_in=ffn1_w_in, ffn1_w_out=ffn1_w_out, w_in=w_in, conv_w=conv_w, q_norm_g=q_norm_g,
                   k_norm_g=k_norm_g, w_branch_conv=w_branch_conv, w_branch_attn=w_branch_attn, w_out=w_out,
                   ffn2_w_in=ffn2_w_in, ffn2_w_out=ffn2_w_out, final_g=final_g)
    moms = dict(c_ctx=(m_c_ctx, v_c_ctx), w_mod=(m_w_mod, v_w_mod), b_mod=(m_b_mod, v_b_mod),
                norm1_g=(m_norm1_g, v_norm1_g), norm2_g=(m_norm2_g, v_norm2_g), norm3_g=(m_norm3_g, v_norm3_g),
                ffn1_w_in=(m_ffn1_w_in, v_ffn1_w_in), ffn1_w_out=(m_ffn1_w_out, v_ffn1_w_out), w_in=(m_w_in, v_w_in),
                conv_w=(m_conv_w, v_conv_w), q_norm_g=(m_q_norm_g, v_q_norm_g), k_norm_g=(m_k_norm_g, v_k_norm_g),
                w_branch_conv=(m_w_branch_conv, v_w_branch_conv), w_branch_attn=(m_w_branch_attn, v_w_branch_attn),
                w_out=(m_w_out, v_w_out), ffn2_w_in=(m_ffn2_w_in, v_ffn2_w_in), ffn2_w_out=(m_ffn2_w_out, v_ffn2_w_out),
                final_g=(m_final_g, v_final_g))
    order = list(weights)

    T, D = x.shape[1], x.shape[2]
    Tc = ctx.shape[1]
    nctx = Tc // ROW_TILE
    nd = N_MOD * D
    Dc = conv_w.shape[2] * N_DEV
    qw, kw = N_Q_HEADS * HEAD_DIM, N_KV_HEADS * HEAD_DIM
    offs, o = {}, 0
    for nme, wd in (("cv", 3 * Dc), ("q", qw), ("gt", 2 * D), ("k", kw), ("v", kw)):
        offs[nme] = o
        o += wd

    ax, ay, ac = lax.axis_index("x"), lax.axis_index("y"), lax.axis_index("c")
    me = 4 * ax + 2 * ay + ac

    shard = {n: (jnp.swapaxes(weights[n][0], 0, 1) if n in ("ffn1_w_in", "ffn2_w_in") else weights[n][0]).astype(BF)
             for n in BIG}
    rows2d = lambda a: a.reshape(-1, a.shape[-1])
    full = {}
    mod_cols = w_mod.shape[2]
    cw_loc = conv_w[0]
    cpad = (-(D + CONV_TAPS * cw_loc.shape[1])) % 128
    pay = jnp.concatenate([c.reshape(1, D), cw_loc.reshape(1, -1), jnp.zeros((1, cpad), F32)], axis=1)
    (g_ffn1_in, g_ffn1_out), (call,) = allgather_two_level([shard["ffn1_w_in"], shard["ffn1_w_out"]], "ag_ffn1",
                                                          riders=[pay])
    full["ffn1_w_in"], full["ffn1_w_out"] = rows2d(g_ffn1_in), rows2d(g_ffn1_out)

    conv_full = call[:, 0, D:D + CONV_TAPS * cw_loc.shape[1]].reshape(N_DEV, CONV_TAPS, -1).transpose(1, 0, 2).reshape(CONV_TAPS, Dc)
    b_loc = lax.dynamic_slice_in_dim(b_mod, me * mod_cols, mod_cols, axis=1)
    cctx2 = c_ctx.reshape(1, D)
    mod_part = mod_forward(call, cctx2, w_mod[0], b_loc, "mod_fwd")
    mod_all = allgather_direct(mod_part, "ag_mod")
    mod_lat = lax.dynamic_index_in_dim(mod_all, me, axis=1, keepdims=False).reshape(nd)
    mod_ctx = mod_all[:, N_DEV, :].reshape(nd)
    mod6 = jnp.stack([mod_ctx, mod_lat]).reshape(6, 3, D)

    tabs = _rope_tables(T, Tc)

    srcs1 = (ctx[0], x[0])
    (xa1, hm1, ab1, h1, f1), (g_w_in,) = ffn_forward(
        srcs1, mod6, 0, norm1_g, full["ffn1_w_in"], full["ffn1_w_out"], nctx, "ffn1_fwd",
        hosted=Hosted(gathers=[shard["w_in"]]))
    full["w_in"] = _regroup_w_in(g_w_in, D, Dc, qw, kw)
    merge_names = ("w_branch_conv", "w_branch_attn", "w_out")
    (hx, proj, qr, kr), g_merge = proj_forward(
        xa1, mod6, norm2_g, full["w_in"], q_norm_g, k_norm_g, tabs, offs, nctx, "proj_fwd",
        hosted=Hosted(gathers=[shard[n] for n in merge_names]))
    full.update({n: rows2d(g) for n, g in zip(merge_names, g_merge)})
    yc = conv_forward(proj, conv_full, offs, Tc, "conv_fwd")
    (oa, lse), (g_ffn2_in, g_ffn2_out) = attention_forward(
        qr, kr, proj, offs, Tc, "attn_fwd", hosted=Hosted(gathers=[shard["ffn2_w_in"], shard["ffn2_w_out"]]))
    full["ffn2_w_in"], full["ffn2_w_out"] = rows2d(g_ffn2_in), rows2d(g_ffn2_out)
    x2, pc, pa, mm, zz = merge_forward(xa1, mod6, yc, oa, proj, full["w_branch_conv"], full["w_branch_attn"],
                                       full["w_out"], offs, Tc, "merge_fwd")
    (dx3, hm2, ab2, h2, f2, lacc), _ = ffn_forward((x2,), mod6, 2, norm3_g, full["ffn2_w_in"], full["ffn2_w_out"], 0, "ffn2_fwd",
                                                   final=(loss_target[0], final_g.reshape(1, D)))
    loss = lax.psum(lacc[1, 0], ("x", "y", "c"))

    by_dest = lambda g: g.reshape((N_DEV, -1, g.shape[-1]))
    (dx2, dab2, df2, acc_f2), _ = ffn_backward_rows(dx3, (x2,), mod6, 2, norm3_g, ab2, f2, full["ffn2_w_in"], full["ffn2_w_out"], 0, "ffn2_bwd")
    early = {"ffn2_w_out": tn_matmul(h2, df2, "ffn2_dwout")[0], "ffn2_w_in": tn_matmul(dab2, hm2, "ffn2_dwin")[0]}
    dproj, dgm, dpc, dpa, dyc, do, acc_mg = merge_backward_rows(dx2, mod6, zz, pc, pa, proj, full["w_branch_conv"],
                                                                full["w_branch_attn"], full["w_out"], offs, Tc, "merge_bwd")
    early["w_out"] = tn_matmul(mm, dgm, "dw_out")[0]
    early["w_branch_conv"] = tn_matmul(yc, dpc, "dw_bc")[0]
    early["w_branch_attn"] = tn_matmul(oa, dpa, "dw_ba")[0]
    dproj, dcw = conv_backward(dproj, dyc, proj, conv_full, offs, Tc, "conv_bwd")
    (dproj, dk, dv, dqg), summed = attention_backward(dproj, qr, kr, proj, oa, lse, do, q_norm_g, tabs, offs, Tc, "attn_bwd",
                                                      hosted=Hosted(scatters=[by_dest(g) for g in early.values()]))
    summed = dict(zip(early, summed))
    dproj, dkg = kv_backward(dproj, dk, dv, proj, k_norm_g, tabs, offs, "kv_bwd")
    g_w_in_grad = _ungroup_w_in_grad(tn_matmul(dproj, hx, "dw_in")[0], D, Dc, qw, kw)
    (dxa1, acc_pj), (summed["w_in"],) = proj_backward_rows(dproj, dx2, xa1, mod6, norm2_g, full["w_in"], nctx, "proj_bwd",
                                                          hosted=Hosted(scatters=[g_w_in_grad]))
    (grad_x2d, dab1, df1, acc_f1), _ = ffn_backward_rows(
        dxa1, srcs1, mod6, 0, norm1_g, ab1, f1, full["ffn1_w_in"], full["ffn1_w_out"], nctx, "ffn1_bwd")
    g_ffn1_w_in, (summed["ffn1_w_out"],) = tn_matmul(
        dab1, hm1, "ffn1_dwin", hosted=Hosted(scatters=[by_dest(tn_matmul(h1, df1, "ffn1_dwout")[0])]))
    grad_x = grad_x2d[None]

    zero_d = jnp.zeros((D,), F32)
    dlat = jnp.concatenate([acc_f1[1, 0], acc_f1[1, 1], acc_f1[1, 2], acc_pj[1, 0], acc_pj[1, 1], acc_mg[0],
                            acc_f2[1, 0], acc_f2[1, 1], acc_f2[1, 2]])
    dctx = jnp.concatenate([acc_f1[0, 0], acc_f1[0, 1], acc_f1[0, 2], acc_pj[0, 0], acc_pj[0, 1]] + [zero_d] * 4)
    small = jnp.concatenate([acc_f1[0, 3] + acc_f1[1, 3], acc_pj[0, 2] + acc_pj[1, 2], acc_f2[1, 3],
                             dqg[0], dkg[0], lacc[0], dcw[0:CONV_TAPS].reshape(-1)])
    n_small = small.shape[0]
    pay_b = jnp.concatenate([dlat, dctx, small]).reshape(1, -1)
    gath = allgather_direct(pay_b, "ag_small_grads")
    dlat_loc = lax.dynamic_slice_in_dim(gath[:, 0, :nd], me * mod_cols, mod_cols, axis=1)
    dctx_loc = lax.dynamic_slice_in_dim(gath[:, 0, nd:2 * nd], me * mod_cols, mod_cols, axis=1)
    g_wmod, pc_part, small_sum = mod_backward(call, cctx2, w_mod[0], dlat_loc, dctx_loc, gath, 2 * nd, n_small, "mod_bwd")
    pcs = allgather_direct(pc_part, "ag_cctx")
    g_bmod, g_cctx = bmod_and_cctx_grad(gath, pcs, cctx2, nd, "small_bwd")
    sm = small_sum[0]
    g_conv_full = sm[3 * D + 2 * HEAD_DIM + D:].reshape(CONV_TAPS, Dc)
    g_conv = lax.dynamic_slice_in_dim(g_conv_full, me * cw_loc.shape[1], cw_loc.shape[1], axis=1)
    gsmall = dict(
        c_ctx=g_cctx, w_mod=g_wmod, b_mod=g_bmod, norm1_g=sm[0:D][None], norm2_g=sm[D:2 * D][None],
        norm3_g=sm[2 * D:3 * D][None], q_norm_g=sm[3 * D:3 * D + HEAD_DIM][None],
        k_norm_g=sm[3 * D + HEAD_DIM:3 * D + 2 * HEAD_DIM][None],
        final_g=sm[3 * D + 2 * HEAD_DIM:3 * D + 2 * HEAD_DIM + D][None], conv_w=g_conv)

    flipped = ("ffn1_w_in", "w_in", "ffn2_w_in")
    last = "ffn1_w_in"
    results = {}

    def update(n):
        w = weights[n]
        shp = w.shape
        if n in flipped:
            two_d = lambda a: jnp.swapaxes(a[0], 0, 1)
            back = lambda a: jnp.swapaxes(a, 0, 1)[None]
        else:
            two_d = lambda a: a.reshape(-1, shp[-1])
            back = lambda a: a.reshape(shp)
        m, v = moms[n]
        if n in summed:
            g2, d, nm, nv = adamw_summed(summed[n], two_d(w), two_d(m), two_d(v), "adamw_" + n)
        else:
            g2 = gsmall[n].reshape(two_d(w).shape)
            d, nm, nv = adamw(two_d(w), g2, two_d(m), two_d(v), "adamw_" + n)
        results[n] = tuple(back(a) for a in (g2, d, nm, nv))
        return d

    last_sems, last_recv_sems, last_src, last_land = scatter_start(by_dest(g_ffn1_w_in), g_cctx, "rs_ffn1_start")
    heavy = ("w_mod", "w_in", "ffn2_w_in", "ffn2_w_out", "ffn1_w_out", "w_out")
    deltas = {n: update(n) for n in order if n != last}
    after = jnp.concatenate([deltas[n][0:1, 0:1] for n in heavy], axis=1)
    g_done, land_done = scatter_wait(last_sems, last_recv_sems, last_src, last_land, after, "rs_ffn1_wait")
    own = lax.dynamic_index_in_dim(g_done, me, axis=0, keepdims=True)
    summed[last] = lax.dynamic_update_slice_in_dim(land_done, own, me, axis=0)
    update(last)
    cols = list(zip(*(results[n] for n in order)))
    return (loss, grad_x, *cols[0], *cols[1], *cols[2], *cols[3])
```

```python
import math

import jax
import jax.numpy as jnp
from jax import lax
from jax.experimental import pallas as pl
from jax.experimental.pallas import tpu as pltpu

F32 = jnp.float32
BF = jnp.bfloat16
EPS = 1e-6
N_DEV = 8
HEAD_DIM = 128
N_Q_HEADS = 8
N_KV_HEADS = 2
GROUP = N_Q_HEADS // N_KV_HEADS
GRID_W = 64
ROPE_THETA = 10000.0
CONV_TAPS = 3
N_MOD = 9
ADAM_LR = 0.001
ADAM_B1 = 0.9
ADAM_B2 = 0.999
ADAM_EPS = 1e-08
ADAM_WD = 0.01
ADAM_STEP = 10
ROW_TILE = 256
VMEM_BIG = 56 << 20
MESH_ID = pl.DeviceIdType.MESH
HIGHEST = lax.Precision.HIGHEST
NT = (((1,), (1,)), ((), ()))
TN = (((0,), (0,)), ((), ()))
LOG2E = math.log2(math.e)


def _pick(n, cands):
    for c in cands:
        if n % c == 0:
            return c
    return n


def _params(vmem=None, sem=None):
    kw = {}
    if vmem is not None:
        kw["vmem_limit_bytes"] = vmem
    if sem is not None:
        kw["dimension_semantics"] = sem
    return pltpu.CompilerParams(**kw)


def _resident(shape):
    nd = len(shape)
    return pl.BlockSpec(shape, lambda *_: (0,) * nd, pipeline_mode=pl.Buffered(1))


def _sigmoid(x):
    return jax.nn.sigmoid(x)


ANY = pl.BlockSpec(memory_space=pl.ANY)


def _coords():
    return lax.axis_index("x"), lax.axis_index("y"), lax.axis_index("c")


def _flip(v, bit):
    return 1 - v if bit else v


def _remote(src, dst, ssem, rsem, dev):
    return pltpu.make_async_remote_copy(src_ref=src, dst_ref=dst, send_sem=ssem, recv_sem=rsem,
                                        device_id=dev, device_id_type=MESH_ID)


def allgather_direct(v, name):
    def body(v_ref, out_ref, ssem, rsem, lsem):
        x, y, c = _coords()
        me = 4 * x + 2 * y + c
        mine = pltpu.make_async_copy(v_ref, out_ref.at[me], lsem)
        mine.start()
        cps = []
        for p in range(1, N_DEV):
            px, py, pc = (p >> 2) & 1, (p >> 1) & 1, p & 1
            cps.append(_remote(v_ref, out_ref.at[me], ssem.at[p - 1], rsem.at[p - 1],
                               (_flip(x, px), _flip(y, py), _flip(c, pc))))
        for cp in cps:
            cp.start()
        for p in range(1, N_DEV):
            px, py, pc = (p >> 2) & 1, (p >> 1) & 1, p & 1
            src = 4 * _flip(x, px) + 2 * _flip(y, py) + _flip(c, pc)
            _remote(v_ref, out_ref.at[src], ssem.at[p - 1], rsem.at[p - 1], (x, y, c)).wait_recv()
        for cp in cps:
            cp.wait_send()
        mine.wait()

    return pl.pallas_call(
        body, name=name,
        out_shape=jax.ShapeDtypeStruct((N_DEV,) + v.shape, v.dtype),
        in_specs=[ANY], out_specs=ANY,
        scratch_shapes=[pltpu.SemaphoreType.DMA((N_DEV - 1,)), pltpu.SemaphoreType.DMA((N_DEV - 1,)),
                        pltpu.SemaphoreType.DMA],
    )(v)


def allgather_two_level(shards, name, riders=()):
    n = len(shards)
    ride = Hosted(gathers=riders)
    r = ride.n

    def body(*refs):
        v_refs, rin, out_refs, rout = refs[:n], refs[n:n + r], refs[n + r:2 * n + r], refs[2 * n + r:2 * n + 2 * r]
        (ssem, rsem, lsem), rsems = refs[2 * n + 2 * r:2 * n + 2 * r + 3], refs[2 * n + 2 * r + 3:]
        x, y, c = _coords()
        me = (x, y, c)
        sib = (x, y, 1 - c)
        chips = [(1 - x, y), (x, 1 - y), (1 - x, 1 - y)]
        if r:
            ride.start(rin, rout, *rsems)

        def slot(w, px, py, pc):
            return out_refs[w].at[4 * px + 2 * py + pc]

        def sem(w, k):
            return ssem.at[7 * w + k], rsem.at[7 * w + k]

        mine = [pltpu.make_async_copy(v_refs[w], slot(w, *me), lsem.at[w]) for w in range(n)]
        for cp in mine:
            cp.start()
        first = []
        for w in range(n):
            first.append(_remote(v_refs[w], slot(w, *me), *sem(w, 0), sib))
            first += [_remote(v_refs[w], slot(w, *me), *sem(w, 1 + j), (*chip, c)) for j, chip in enumerate(chips)]
        for cp in first:
            cp.start()
        passed = []
        for w in range(n):
            for j, chip in enumerate(chips):
                _remote(v_refs[w], slot(w, *chip, c), *sem(w, 1 + j), me).wait_recv()
                cp = _remote(slot(w, *chip, c), slot(w, *chip, c), *sem(w, 4 + j), sib)
                cp.start()
                passed.append(cp)
        for w in range(n):
            _remote(v_refs[w], slot(w, x, y, 1 - c), *sem(w, 0), me).wait_recv()
            for j, chip in enumerate(chips):
                _remote(v_refs[w], slot(w, *chip, 1 - c), *sem(w, 4 + j), me).wait_recv()
        for cp in first + passed:
            cp.wait_send()
        for cp in mine:
            cp.wait()
        if r:
            ride.wait(rin, rout, *rsems)

    res = pl.pallas_call(
        body, name=name,
        out_shape=[jax.ShapeDtypeStruct((N_DEV,) + s.shape, s.dtype) for s in shards] + ride.out_shapes,
        in_specs=[ANY] * (n + r), out_specs=[ANY] * (n + r),
        scratch_shapes=[pltpu.SemaphoreType.DMA((7 * n,)), pltpu.SemaphoreType.DMA((7 * n,)),
                        pltpu.SemaphoreType.DMA((n,))] + (ride.scratch if r else []),
    )(*shards, *riders)
    return list(res[:n]), list(res[n:])


def _block_exchange(srcs, k, copies, name, riders):
    n = len(srcs)
    ride = Hosted(gathers=riders)
    r = ride.n

    def body(*refs):
        g_refs, rin, out_refs, rout = refs[:n], refs[n:n + r], refs[n + r:2 * n + r], refs[2 * n + r:2 * n + 2 * r]
        (ssem, rsem), rsems = refs[2 * n + 2 * r:2 * n + 2 * r + 2], refs[2 * n + 2 * r + 2:]
        if r:
            ride.start(rin, rout, *rsems)
        cps = copies(g_refs, out_refs, ssem, rsem)
        for cp in cps:
            cp.start()
        for cp in cps:
            cp.wait()
        if r:
            ride.wait(rin, rout, *rsems)

    res = pl.pallas_call(
        body, name=name,
        out_shape=[jax.ShapeDtypeStruct((k,) + g.shape[1:], g.dtype) for g in srcs] + ride.out_shapes,
        in_specs=[ANY] * (n + r), out_specs=[ANY] * (n + r),
        scratch_shapes=[pltpu.SemaphoreType.DMA((k * n,)), pltpu.SemaphoreType.DMA((k * n,))] + (ride.scratch if r else []),
    )(*srcs, *riders)
    return list(res[:n]), list(res[n:])


def rs_sibling_exchange(gs, name, riders=()):
    def copies(g_refs, out_refs, ssem, rsem):
        x, y, c = _coords()
        return [_remote(g_refs[w].at[2 * k + (1 - c)], out_refs[w].at[k], ssem.at[4 * w + k], rsem.at[4 * w + k], (x, y, 1 - c))
                for w in range(len(gs)) for k in range(4)]

    return _block_exchange(gs, 4, copies, name, riders)


def rs_chip_exchange(ps, name, riders=()):
    def copies(p_refs, out_refs, ssem, rsem):
        x, y, c = _coords()
        chips = [(1 - x, y), (x, 1 - y), (1 - x, 1 - y)]
        return [_remote(p_refs[w].at[2 * cx + cy], out_refs[w].at[j], ssem.at[3 * w + j], rsem.at[3 * w + j], (cx, cy, c))
                for w in range(len(ps)) for j, (cx, cy) in enumerate(chips)]

    return _block_exchange(ps, 3, copies, name, riders)


SEM = pl.BlockSpec(memory_space=pltpu.SEMAPHORE)
IN_HBM = pl.BlockSpec(memory_space=pltpu.HBM)
DATAFLOW = pltpu.SideEffectType.DATAFLOW_SIDE_EFFECTING


def _scatter_descriptors(src_ref, land_ref, ssem, rsem, arrivals):
    x, y, c = _coords()
    me = 4 * x + 2 * y + c
    cps = []
    for p in range(1, N_DEV):
        px, py, pc = _flip(x, (p >> 2) & 1), _flip(y, (p >> 1) & 1), _flip(c, p & 1)
        peer = 4 * px + 2 * py + pc
        if arrivals:
            cps.append(_remote(src_ref.at[me], land_ref.at[peer], ssem.at[p - 1], rsem.at[p - 1], (x, y, c)))
        else:
            cps.append(_remote(src_ref.at[peer], land_ref.at[me], ssem.at[p - 1], rsem.at[p - 1], (px, py, pc)))
    return cps


def scatter_start(g, before, name):
    def body(g_ref, land_ref, before_ref, ssem, rsem, g_thru, land_thru, token):
        for cp in _scatter_descriptors(g_ref, land_ref, ssem, rsem, False):
            cp.start()
        token[...] = jnp.zeros_like(token)

    return pl.pallas_call(
        body, name=name,
        out_shape=(pltpu.SemaphoreType.DMA((N_DEV - 1,)), pltpu.SemaphoreType.DMA((N_DEV - 1,)),
                   pltpu.HBM(g.shape, g.dtype), pltpu.HBM(g.shape, g.dtype), jax.ShapeDtypeStruct((8, 128), F32)),
        in_specs=(IN_HBM, IN_HBM, ANY), out_specs=(SEM, SEM, IN_HBM, IN_HBM, pl.BlockSpec(memory_space=pltpu.VMEM)),
        input_output_aliases={0: 2, 1: 3},
        compiler_params=pltpu.CompilerParams(has_side_effects=DATAFLOW),
    )(pltpu.with_memory_space_constraint(g, pltpu.HBM),
      pltpu.with_memory_space_constraint(lax.empty(g.shape, g.dtype), pltpu.HBM), before)


def scatter_wait(ssem, rsem, g_thru, land_thru, after, name):
    def body(g_ref, land_ref, ssem_ref, rsem_ref, after_ref, g_out, land_out):
        cps = _scatter_descriptors(g_ref, land_ref, ssem_ref, rsem_ref, True)
        for cp in cps:
            cp.wait_send()
        for cp in cps:
            cp.wait_recv()

    return pl.pallas_call(
        body, name=name,
        out_shape=(pltpu.HBM(g_thru.shape, g_thru.dtype), pltpu.HBM(land_thru.shape, land_thru.dtype)),
        in_specs=(IN_HBM, IN_HBM, SEM, SEM, ANY), out_specs=(IN_HBM, IN_HBM),
        input_output_aliases={0: 0, 1: 1},
        compiler_params=pltpu.CompilerParams(has_side_effects=DATAFLOW),
    )(g_thru, land_thru, ssem, rsem, after)


class Hosted:
    def __init__(self, gathers=(), scatters=()):
        self.items = [(a, False) for a in gathers] + [(a, True) for a in scatters]
        self.n = len(self.items)
        self.operands = [a for a, _ in self.items]
        self.out_shapes = [jax.ShapeDtypeStruct(a.shape if sc else (N_DEV,) + a.shape, a.dtype) for a, sc in self.items]
        self.scratch = [pltpu.SemaphoreType.DMA((7 * self.n,)), pltpu.SemaphoreType.DMA((7 * self.n,)),
                        pltpu.SemaphoreType.DMA((self.n,))]

    def _copies(self, in_refs, out_refs, ssem, rsem, lsem, arrivals):
        x, y, c = _coords()
        me = 4 * x + 2 * y + c
        remote, local = [], []
        for w, (_, sc) in enumerate(self.items):
            src, dst = in_refs[w], out_refs[w]
            local.append(pltpu.make_async_copy(src.at[me] if sc else src, dst.at[me], lsem.at[w]))
            for p in range(1, N_DEV):
                px, py, pc = _flip(x, (p >> 2) & 1), _flip(y, (p >> 1) & 1), _flip(c, p & 1)
                peer = 4 * px + 2 * py + pc
                k = 7 * w + p - 1
                if arrivals:
                    remote.append(_remote(src.at[me] if sc else src, dst.at[peer], ssem.at[k], rsem.at[k], (x, y, c)))
                else:
                    remote.append(_remote(src.at[peer] if sc else src, dst.at[me], ssem.at[k], rsem.at[k], (px, py, pc)))
        return remote, local

    def start(self, in_refs, out_refs, ssem, rsem, lsem):
        sends, local = self._copies(in_refs, out_refs, ssem, rsem, lsem, False)
        for cp in local + sends:
            cp.start()

    def wait(self, in_refs, out_refs, ssem, rsem, lsem):
        arrivals, local = self._copies(in_refs, out_refs, ssem, rsem, lsem, True)
        for cp in arrivals:
            cp.wait_recv()
        for cp in arrivals:
            cp.wait_send()
        for cp in local:
            cp.wait()


def _call(body, *, name, grid, in_specs, out_specs, out_shape, operands, params, scratch_shapes=(), aliases=None, hosted=None):
    n_in, n_out, n_scr = len(in_specs), len(out_specs), len(scratch_shapes)
    h = hosted.n if hosted is not None else 0

    def wrapped(*refs):
        ins, cins = refs[:n_in], refs[n_in:n_in + h]
        outs, couts = refs[n_in + h:n_in + h + n_out], refs[n_in + h + n_out:n_in + 2 * h + n_out]
        rest = refs[n_in + 2 * h + n_out:]
        scr, sems = rest[:n_scr], rest[n_scr:]
        if h:
            ids = [pl.program_id(a) for a in range(len(grid))]
            first, last = ids[0] == 0, ids[0] == grid[0] - 1
            for a in range(1, len(grid)):
                first, last = first & (ids[a] == 0), last & (ids[a] == grid[a] - 1)

            @pl.when(first)
            def _():
                hosted.start(cins, couts, *sems)

        body(*ins, *outs, *scr)
        if h:
            @pl.when(last)
            def _():
                hosted.wait(cins, couts, *sems)

    res = pl.pallas_call(
        wrapped, name=name, grid=grid,
        in_specs=list(in_specs) + [ANY] * h, out_specs=list(out_specs) + [ANY] * h,
        out_shape=list(out_shape) + (hosted.out_shapes if h else []),
        scratch_shapes=list(scratch_shapes) + (hosted.scratch if h else []),
        input_output_aliases=aliases or {}, compiler_params=params,
    )(*operands, *(hosted.operands if h else []))
    return list(res[:n_out]), list(res[n_out:])


def _row_tile(R, C):
    if R * C <= (1 << 18):
        return R
    return max((d for d in range(8, 257, 8) if R % d == 0), default=R)


def rs_pair_sum(mine, recv_a, name):
    _, R, C = mine.shape
    tr = _row_tile(R, C)

    def body(g_ref, a_ref, o_ref):
        o_ref[0] = (g_ref[0].astype(F32) + a_ref[0].astype(F32)).astype(o_ref.dtype)

    blk = pl.BlockSpec((1, tr, C), lambda k, r: (k, r, 0))
    return pl.pallas_call(
        body, name=name, grid=(4, R // tr),
        in_specs=[blk, blk], out_specs=blk,
        out_shape=jax.ShapeDtypeStruct((4, R, C), BF),
    )(mine, recv_a)


def _cond_rows(call_ref, cctx_ref, z_ref, D):
    z_ref[...] = jnp.zeros_like(z_ref)
    for a in range(N_DEV):
        z_ref[a:a + 1, :] = call_ref[a][:, :D]
    z_ref[N_DEV:N_DEV + 1, :] = cctx_ref[...]


def mod_forward(call, c_ctx, w_loc, b_loc, name):
    D, cols = w_loc.shape

    def body(call_ref, cctx_ref, w_ref, b_ref, o_ref, z_ref):
        _cond_rows(call_ref, cctx_ref, z_ref, D)
        z = z_ref[...]
        s = z * _sigmoid(z)
        o_ref[...] = jnp.dot(s, w_ref[...], preferred_element_type=F32, precision=HIGHEST) + b_ref[...]

    return pl.pallas_call(
        body, name=name, out_shape=jax.ShapeDtypeStruct((16, cols), F32),
        scratch_shapes=[pltpu.VMEM((16, D), F32)],
        compiler_params=_params(vmem=VMEM_BIG),
    )(call, c_ctx, w_loc, b_loc)


def mod_backward(call, c_ctx, w_loc, dlat_loc, dctx_loc, gath, n_small_off, n_small, name):
    D, cols = w_loc.shape

    def body(call_ref, cctx_ref, w_ref, dlat_ref, dctx_ref, g_ref, gw_ref, pc_ref, small_ref, z_ref, dm_ref):
        _cond_rows(call_ref, cctx_ref, z_ref, D)
        z = z_ref[...]
        s = z * _sigmoid(z)
        dctx = dctx_ref[0:1, :]
        for a in range(1, N_DEV):
            dctx = dctx + dctx_ref[a:a + 1, :]
        dm_ref[...] = jnp.zeros_like(dm_ref)
        dm_ref[0:N_DEV, :] = dlat_ref[...]
        dm_ref[N_DEV:N_DEV + 1, :] = dctx
        gw_ref[...] = lax.dot_general(s, dm_ref[...], TN, preferred_element_type=F32, precision=HIGHEST)
        pc_ref[...] = lax.dot_general(dctx, w_ref[...], NT, preferred_element_type=F32, precision=HIGHEST)
        acc = g_ref[0][:, n_small_off:n_small_off + n_small]
        for a in range(1, N_DEV):
            acc = acc + g_ref[a][:, n_small_off:n_small_off + n_small]
        small_ref[...] = acc

    return pl.pallas_call(
        body, name=name,
        out_shape=(jax.ShapeDtypeStruct((D, cols), F32), jax.ShapeDtypeStruct((1, D), F32),
                   jax.ShapeDtypeStruct((1, n_small), F32)),
        scratch_shapes=[pltpu.VMEM((16, D), F32), pltpu.VMEM((16, cols), F32)],
        compiler_params=_params(vmem=VMEM_BIG),
    )(call, c_ctx, w_loc, dlat_loc, dctx_loc, gath)


def bmod_and_cctx_grad(gath, pcs, c_ctx, nd, name):
    D = c_ctx.shape[-1]

    def body(g_ref, pc_ref, cctx_ref, gb_ref, gc_ref):
        acc = g_ref[0][:, :nd] + g_ref[0][:, nd:2 * nd]
        for a in range(1, N_DEV):
            acc = acc + (g_ref[a][:, :nd] + g_ref[a][:, nd:2 * nd])
        gb_ref[...] = acc
        p = pc_ref[0]
        for a in range(1, N_DEV):
            p = p + pc_ref[a]
        z = cctx_ref[...]
        sg = _sigmoid(z)
        gc_ref[...] = p * (sg * (1.0 + z * (1.0 - sg)))

    return pl.pallas_call(
        body, name=name,
        out_shape=(jax.ShapeDtypeStruct((1, nd), F32), jax.ShapeDtypeStruct((1, D), F32)),
    )(gath, pcs, c_ctx)


def _mod_spec(D, which, nctx):
    return pl.BlockSpec((1, 3, D), lambda i: (jnp.where(i < nctx, 0, 3) + which, 0, 0))


def _acc_spec(D, nctx):
    return pl.BlockSpec((1, 8, D), lambda i: (jnp.where(i < nctx, 0, 1), 0, 0))


def _row(tm, n):
    return pl.BlockSpec((tm, n), lambda i: (i, 0))


def _two_stream_specs(tm, D, nctx):
    return [pl.BlockSpec((tm, D), lambda i: (jnp.minimum(i, nctx - 1), 0)),
            pl.BlockSpec((tm, D), lambda i: (jnp.maximum(i - nctx, 0), 0))]


def _final_norm_loss_backward(x, tgt, gg, i, dx_ref, acc_ref):
    @pl.when(i == 0)
    def _():
        acc_ref[...] = jnp.zeros_like(acc_ref)

    D = x.shape[1]
    r = lax.rsqrt(jnp.mean(x * x, axis=-1, keepdims=True) + EPS)
    xh = x * r
    e = xh * gg - tgt
    part = 0.5 * jnp.sum(jnp.mean(e * e, axis=-1, keepdims=True), axis=0, keepdims=True)
    dy = e * (1.0 / D)
    dyg = dy * gg
    dx_ref[...] = r * (dyg - xh * jnp.mean(dyg * xh, axis=-1, keepdims=True))
    acc_ref[0:1, :] += jnp.sum(dy * xh, axis=0, keepdims=True)
    acc_ref[1:2, :] += jnp.broadcast_to(part, (1, D))


def _hidden_chunks(F):
    step = 1024 if F % 256 == 0 else F
    return [(lo, min(lo + step, F)) for lo in range(0, F, step)]


def ffn_forward(srcs, mod6, which, g, wt, w_out, nctx, name, hosted=None, final=None):
    D = srcs[-1].shape[1]
    Tr = sum(s.shape[0] for s in srcs)
    F = wt.shape[0] // 2
    tm = ROW_TILE
    two = len(srcs) == 2
    nfin = 0 if final is None else 2

    def body(*refs):
        x_refs, fin_refs, rest = refs[:len(srcs)], refs[len(srcs):len(srcs) + nfin], refs[len(srcs) + nfin:]
        mod_ref, g_ref, wt_ref, wout_ref, xo_ref, hm_ref, ab_ref, h_ref, f_ref = rest[:9]
        x = jnp.where(pl.program_id(0) < nctx, x_refs[0][...], x_refs[1][...]) if two else x_refs[0][...]
        ms = mod_ref[0]
        shift, scale, gate = ms[0:1], ms[1:2], ms[2:3]
        r = lax.rsqrt(jnp.mean(x * x, axis=-1, keepdims=True) + EPS)
        hb = (((x * r) * g_ref[...]) * (1.0 + scale) + shift).astype(BF)
        hm_ref[...] = hb
        f = jnp.zeros((tm, D), F32)
        for lo, hi in _hidden_chunks(F):
            a = lax.dot_general(hb, wt_ref[lo:hi, :], NT, preferred_element_type=F32)
            b = lax.dot_general(hb, wt_ref[F + lo:F + hi, :], NT, preferred_element_type=F32)
            ab_ref[:, lo:hi] = a.astype(BF)
            ab_ref[:, F + lo:F + hi] = b.astype(BF)
            h = ((a * _sigmoid(a)) * b).astype(BF)
            h_ref[:, lo:hi] = h
            f = f + jnp.dot(h, wout_ref[lo:hi, :], preferred_element_type=F32)
        f_ref[...] = f.astype(BF)
        xo = x + (0.5 * gate) * f
        if final is None:
            xo_ref[...] = xo
        else:
            _final_norm_loss_backward(xo, fin_refs[0][...], fin_refs[1][...], pl.program_id(0), xo_ref, rest[9])

    src_specs = _two_stream_specs(tm, D, nctx) if two else [_row(tm, D)]
    fin = final is not None
    return _call(
        body, name=name, grid=(Tr // tm,),
        in_specs=src_specs + ([_row(tm, D), _resident((1, D))] if fin else [])
                 + [_mod_spec(D, which, nctx), _resident((1, D)), _resident(wt.shape), _resident(w_out.shape)],
        out_specs=[_row(tm, D), _row(tm, D), _row(tm, 2 * F), _row(tm, F), _row(tm, D)]
                  + ([pl.BlockSpec((8, D), lambda i: (0, 0))] if fin else []),
        out_shape=[jax.ShapeDtypeStruct((Tr, D), F32), jax.ShapeDtypeStruct((Tr, D), BF),
                   jax.ShapeDtypeStruct((Tr, 2 * F), BF), jax.ShapeDtypeStruct((Tr, F), BF),
                   jax.ShapeDtypeStruct((Tr, D), BF)] + ([jax.ShapeDtypeStruct((8, D), F32)] if fin else []),
        operands=[*srcs, *(final or ()), mod6, g, wt, w_out], hosted=hosted,
        params=_params(vmem=VMEM_BIG, sem=("arbitrary",)))


def ffn_backward_rows(dxo, srcs, mod6, which, g, ab, fo, wt, w_out, nctx, name, hosted=None):
    D = srcs[-1].shape[1]
    Tr = sum(s.shape[0] for s in srcs)
    Tl = srcs[-1].shape[0]
    F = wt.shape[0] // 2
    tm = ROW_TILE
    two = len(srcs) == 2

    def body(*refs):
        dxo_ref, x_refs = refs[0], refs[1:1 + len(srcs)]
        mod_ref, g_ref, ab_ref, fo_ref, wt_ref, wout_ref, dx_ref, dab_ref, df_ref, acc_ref = refs[1 + len(srcs):]
        i = pl.program_id(0)

        @pl.when((i == 0) | (i == nctx))
        def _():
            acc_ref[...] = jnp.zeros_like(acc_ref)

        dxo = dxo_ref[...]
        x = jnp.where(i < nctx, x_refs[0][...], x_refs[1][...]) if two else x_refs[0][...]
        ms = mod_ref[0]
        scale, gate = ms[1:2], ms[2:3]
        gg = g_ref[...]
        dgate = jnp.sum(dxo * fo_ref[...].astype(F32), axis=0, keepdims=True) * 0.5
        dfb = (dxo * (0.5 * gate)).astype(BF)
        df_ref[...] = dfb
        dhm = jnp.zeros((tm, D), F32)
        for lo, hi in _hidden_chunks(F):
            dh = lax.dot_general(dfb, wout_ref[lo:hi, :], NT, preferred_element_type=F32)
            a = ab_ref[:, lo:hi].astype(F32)
            b = ab_ref[:, F + lo:F + hi].astype(F32)
            sg = _sigmoid(a)
            da = ((dh * b) * (sg * (1.0 + a * (1.0 - sg)))).astype(BF)
            db = (dh * (a * sg)).astype(BF)
            dab_ref[:, lo:hi] = da
            dab_ref[:, F + lo:F + hi] = db
            dhm = dhm + jnp.dot(da, wt_ref[lo:hi, :], preferred_element_type=F32)
            dhm = dhm + jnp.dot(db, wt_ref[F + lo:F + hi, :], preferred_element_type=F32)
        r = lax.rsqrt(jnp.mean(x * x, axis=-1, keepdims=True) + EPS)
        xh = x * r
        dshift = jnp.sum(dhm, axis=0, keepdims=True)
        dscale = jnp.sum(dhm * (xh * gg), axis=0, keepdims=True)
        dxh_g = dhm * (1.0 + scale)
        dg = jnp.sum(dxh_g * xh, axis=0, keepdims=True)
        dxh = dxh_g * gg
        dx_ref[...] = dxo + r * (dxh - xh * jnp.mean(dxh * xh, axis=-1, keepdims=True))
        for k, val in enumerate((dshift, dscale, dgate, dg)):
            acc_ref[0, k:k + 1, :] += val

    src_specs = _two_stream_specs(tm, D, nctx) if two else [_row(tm, D)]
    dx_spec = pl.BlockSpec((tm, D), lambda i: (jnp.maximum(i - nctx, 0), 0))
    return _call(
        body, name=name, grid=(Tr // tm,),
        in_specs=[_row(tm, D)] + src_specs + [_mod_spec(D, which, nctx), _resident((1, D)), _row(tm, 2 * F), _row(tm, D),
                                              _resident(wt.shape), _resident(w_out.shape)],
        out_specs=[dx_spec, _row(tm, 2 * F), _row(tm, D), _acc_spec(D, nctx)],
        out_shape=[jax.ShapeDtypeStruct((Tl, D), F32), jax.ShapeDtypeStruct((Tr, 2 * F), BF),
                   jax.ShapeDtypeStruct((Tr, D), BF), jax.ShapeDtypeStruct((2, 8, D), F32)],
        operands=[dxo, *srcs, mod6, g, ab, fo, wt, w_out], hosted=hosted,
        params=_params(vmem=VMEM_BIG, sem=("arbitrary",)))


def _token_tile(T):
    return _pick(T, (2048, 1408, 1024, 768, 512, 256, 128))


def tn_matmul(a, b, name, hosted=None):
    T, K = a.shape
    N = b.shape[1]
    tk = _pick(K, (1024, 1408, 1664, 768, 512, 384, 256, 128))
    tn = _pick(N, (1024, 1408, 1664, 768, 512, 384, 256, 128))
    tt = _token_tile(T)
    nt = T // tt

    def body(a_ref, b_ref, o_ref, acc_ref):
        t = pl.program_id(2)

        @pl.when(t == 0)
        def _():
            acc_ref[...] = jnp.zeros_like(acc_ref)

        acc_ref[...] += lax.dot_general(a_ref[...], b_ref[...], TN, preferred_element_type=F32)

        @pl.when(t == nt - 1)
        def _():
            o_ref[...] = acc_ref[...].astype(BF)

    (out,), exchanged = _call(
        body, name=name, grid=(K // tk, N // tn, nt),
        in_specs=[pl.BlockSpec((tt, tk), lambda k, n, t: (t, k)), pl.BlockSpec((tt, tn), lambda k, n, t: (t, n))],
        out_specs=[pl.BlockSpec((tk, tn), lambda k, n, t: (k, n))],
        out_shape=[jax.ShapeDtypeStruct((K, N), BF)],
        scratch_shapes=[pltpu.VMEM((tk, tn), F32)],
        operands=[a, b], hosted=hosted,
        params=_params(vmem=VMEM_BIG, sem=("arbitrary", "arbitrary", "arbitrary")))
    return out, exchanged


def _rope_apply(y, cos, s_next, s_prev):
    return y * cos + pltpu.roll(y, HEAD_DIM - 32, 1) * s_next + pltpu.roll(y, 32, 1) * s_prev


def _rope_transpose(dz, cos, s_next, s_prev):
    return dz * cos + pltpu.roll(dz * s_next, 32, 1) + pltpu.roll(dz * s_prev, HEAD_DIM - 32, 1)


def proj_forward(xa, mod6, g, w_in, qg, kg, tabs, offs, nctx, name, hosted=None):
    Tr, D = xa.shape
    P = w_in.shape[1]
    tm = ROW_TILE
    qo, ko = offs["q"], offs["k"]
    qw, kw = N_Q_HEADS * HEAD_DIM, N_KV_HEADS * HEAD_DIM
    scale_q = HEAD_DIM ** -0.5 * LOG2E

    def body(x_ref, mod_ref, g_ref, w_ref, qg_ref, kg_ref, tab_ref, hx_ref, pr_ref, q_ref, k_ref):
        x = x_ref[...]
        ms = mod_ref[0]
        shift, scale = ms[0:1], ms[1:2]
        r = lax.rsqrt(jnp.mean(x * x, axis=-1, keepdims=True) + EPS)
        hb = (((x * r) * g_ref[...]) * (1.0 + scale) + shift).astype(BF)
        hx_ref[...] = hb
        pr = jnp.dot(hb, w_ref[...], preferred_element_type=F32)
        pr_ref[...] = pr.astype(BF)
        cos, s_next, s_prev = tab_ref[0], tab_ref[1], tab_ref[2]

        def head(v, gain):
            n = v * lax.rsqrt(jnp.mean(v * v, axis=-1, keepdims=True) + EPS)
            return _rope_apply(n * gain, cos, s_next, s_prev)

        for h in range(N_Q_HEADS):
            lo = qo + h * HEAD_DIM
            q_ref[:, h * HEAD_DIM:(h + 1) * HEAD_DIM] = (head(pr[:, lo:lo + HEAD_DIM], qg_ref[...]) * scale_q).astype(BF)
        for h in range(N_KV_HEADS):
            lo = ko + h * HEAD_DIM
            k_ref[:, h * HEAD_DIM:(h + 1) * HEAD_DIM] = head(pr[:, lo:lo + HEAD_DIM], kg_ref[...]).astype(BF)

    return _call(
        body, name=name, grid=(Tr // tm,),
        in_specs=[_row(tm, D), _mod_spec(D, 1, nctx), _resident((1, D)), _resident(w_in.shape),
                  _resident((1, HEAD_DIM)), _resident((1, HEAD_DIM)),
                  pl.BlockSpec((3, tm, HEAD_DIM), lambda i: (0, i, 0))],
        out_specs=[_row(tm, D), _row(tm, P), _row(tm, qw), _row(tm, kw)],
        out_shape=[jax.ShapeDtypeStruct((Tr, D), BF), jax.ShapeDtypeStruct((Tr, P), BF),
                   jax.ShapeDtypeStruct((Tr, qw), BF), jax.ShapeDtypeStruct((Tr, kw), BF)],
        operands=[xa, mod6, g, w_in, qg, kg, tabs], hosted=hosted,
        params=_params(vmem=VMEM_BIG, sem=("arbitrary",)))


def _shifted(u, first_row, last_row):
    T = u.shape[0]
    prev = jnp.where(first_row, 0.0, pltpu.roll(u, 1, 0))
    nxt = jnp.where(last_row, 0.0, pltpu.roll(u, T - 1, 0))
    return prev, nxt


def conv_forward(proj, conv_w, offs, Tc, name):
    Ta = proj.shape[0]
    T = Ta - Tc
    Dc = conv_w.shape[1]
    cb = offs["cv"] // 384

    def body(p_ref, w_ref, y_ref):
        rows = lax.broadcasted_iota(jnp.int32, (T, 128), 0)
        u = p_ref[pl.ds(Tc, T), 128:256].astype(F32) * p_ref[pl.ds(Tc, T), 256:384].astype(F32)
        prev, nxt = _shifted(u, rows == 0, rows == T - 1)
        w = w_ref[...]
        cv = prev * w[0:1] + u * w[1:2] + nxt * w[2:3]
        y_ref[...] = (p_ref[pl.ds(Tc, T), 0:128].astype(F32) * cv).astype(BF)

    return pl.pallas_call(
        body, name=name, grid=(Dc // 128,),
        in_specs=[pl.BlockSpec((Ta, 384), lambda j: (0, cb + j)), pl.BlockSpec((CONV_TAPS, 128), lambda j: (0, j))],
        out_specs=pl.BlockSpec((T, 128), lambda j: (0, j)),
        out_shape=jax.ShapeDtypeStruct((T, Dc), BF),
        compiler_params=_params(vmem=VMEM_BIG, sem=("arbitrary",)),
    )(proj, conv_w)


def conv_backward(dproj, dy, proj, conv_w, offs, Tc, name):
    Ta = proj.shape[0]
    T = Ta - Tc
    Dc = conv_w.shape[1]
    cb = offs["cv"] // 384

    def body(dp_any, dy_ref, p_ref, w_ref, o_ref, dw_ref):
        rows = lax.broadcasted_iota(jnp.int32, (T, 128), 0)
        first, last = rows == 0, rows == T - 1
        bg = p_ref[pl.ds(Tc, T), 0:128].astype(F32)
        cg = p_ref[pl.ds(Tc, T), 128:256].astype(F32)
        vc = p_ref[pl.ds(Tc, T), 256:384].astype(F32)
        dy = dy_ref[...].astype(F32)
        u = cg * vc
        prev, nxt = _shifted(u, first, last)
        w = w_ref[...]
        cv = prev * w[0:1] + u * w[1:2] + nxt * w[2:3]
        o_ref[pl.ds(0, Tc), :] = jnp.zeros((Tc, 384), BF)
        o_ref[pl.ds(Tc, T), 0:128] = (dy * cv).astype(BF)
        dcv = dy * bg
        dprev, dnxt = _shifted(dcv, first, last)
        du = dnxt * w[0:1] + dcv * w[1:2] + dprev * w[2:3]
        o_ref[pl.ds(Tc, T), 128:256] = (du * vc).astype(BF)
        o_ref[pl.ds(Tc, T), 256:384] = (du * cg).astype(BF)
        dw_ref[...] = jnp.zeros_like(dw_ref)
        for k, tap in enumerate((prev, u, nxt)):
            dw_ref[k:k + 1, :] = jnp.sum(dcv * tap, axis=0, keepdims=True)

    blk = pl.BlockSpec((Ta, 384), lambda j: (0, cb + j))
    return pl.pallas_call(
        body, name=name, grid=(Dc // 128,),
        in_specs=[ANY, pl.BlockSpec((T, 128), lambda j: (0, j)), blk, pl.BlockSpec((CONV_TAPS, 128), lambda j: (0, j))],
        out_specs=[blk, pl.BlockSpec((8, 128), lambda j: (0, j))],
        out_shape=[jax.ShapeDtypeStruct(dproj.shape, BF), jax.ShapeDtypeStruct((8, Dc), F32)],
        input_output_aliases={0: 0},
        compiler_params=_params(vmem=VMEM_BIG, sem=("arbitrary",)),
    )(dproj, dy, proj, conv_w)


def _kv_chunk(Ta):
    return _pick(Ta, (768, 512, 384, 256, 128))


def _stack_heads(v):
    return jnp.concatenate([v[:, h * HEAD_DIM:(h + 1) * HEAD_DIM] for h in range(GROUP)], axis=0)


def attention_forward(q, k, proj, offs, Tc, name, hosted=None):
    Ta = k.shape[0]
    T = Ta - Tc
    tq = ROW_TILE
    kc = _kv_chunk(Ta)
    nkv = Ta // kc
    gw = GROUP * HEAD_DIM
    vblk = offs["v"] // HEAD_DIM
    qoff = Tc // tq
    n = GROUP * tq

    def body(q_ref, k_ref, v_ref, o_ref, lse_ref, vx_ref, qs_ref, s0_ref, s1_ref, m_ref, acc_ref):
        @pl.when(pl.program_id(1) == 0)
        def _():
            vx_ref[:, 0:HEAD_DIM] = v_ref[...]
            vx_ref[:, HEAD_DIM:2 * HEAD_DIM] = jnp.ones((Ta, HEAD_DIM), BF)

        qs_ref[...] = _stack_heads(q_ref[...])
        m_ref[...] = jnp.full((n, 1), -1e30, F32)
        acc_ref[...] = jnp.zeros((n, 2 * HEAD_DIM), F32)

        def rows(c):
            return pl.ds(pl.multiple_of(c * kc, kc), kc)

        def logits(c, dst):
            dst[...] = lax.dot_general(qs_ref[...], k_ref[rows(c), :], NT, preferred_element_type=F32)

        def consume(src, c):
            s = src[...]
            m_prev = m_ref[...]
            m_new = jnp.maximum(m_prev, jnp.max(s, axis=-1, keepdims=True))
            p = jnp.exp2(s - m_new).astype(BF)
            acc_ref[...] = jnp.exp2(m_prev - m_new) * acc_ref[...] + jnp.dot(p, vx_ref[rows(c), :], preferred_element_type=F32)
            m_ref[...] = m_new

        def pair(i, carry):
            logits(2 * i + 1, s1_ref)
            consume(s0_ref, 2 * i)
            logits(2 * i + 2, s0_ref)
            consume(s1_ref, 2 * i + 1)
            return carry

        logits(0, s0_ref)
        if nkv % 2:
            lax.fori_loop(0, nkv // 2, pair, 0)
            consume(s0_ref, nkv - 1)
        else:
            lax.fori_loop(0, nkv // 2 - 1, pair, 0)
            logits(nkv - 1, s1_ref)
            consume(s0_ref, nkv - 2)
            consume(s1_ref, nkv - 1)
        acc = acc_ref[...]
        l = acc[:, HEAD_DIM:HEAD_DIM + 1]
        o = acc[:, 0:HEAD_DIM] / l
        lse = m_ref[...] + jnp.log2(l)
        for h in range(GROUP):
            o_ref[:, h * HEAD_DIM:(h + 1) * HEAD_DIM] = o[h * tq:(h + 1) * tq].astype(BF)
            lse_ref[0, :, h:h + 1] = lse[h * tq:(h + 1) * tq]

    return _call(
        body, name=name, grid=(N_KV_HEADS, T // tq),
        in_specs=[pl.BlockSpec((tq, gw), lambda j, i: (i + qoff, j)),
                  pl.BlockSpec((Ta, HEAD_DIM), lambda j, i: (0, j)),
                  pl.BlockSpec((Ta, HEAD_DIM), lambda j, i: (0, vblk + j))],
        out_specs=[pl.BlockSpec((tq, gw), lambda j, i: (i, j)),
                   pl.BlockSpec((1, tq, GROUP), lambda j, i: (j, i, 0))],
        out_shape=[jax.ShapeDtypeStruct((T, N_Q_HEADS * HEAD_DIM), BF),
                   jax.ShapeDtypeStruct((N_KV_HEADS, T, GROUP), F32)],
        scratch_shapes=[pltpu.VMEM((Ta, 2 * HEAD_DIM), BF), pltpu.VMEM((n, HEAD_DIM), BF), pltpu.VMEM((n, kc), F32),
                        pltpu.VMEM((n, kc), F32), pltpu.VMEM((n, 1), F32), pltpu.VMEM((n, 2 * HEAD_DIM), F32)],
        operands=[q, k, proj], hosted=hosted,
        params=_params(vmem=VMEM_BIG, sem=("arbitrary", "arbitrary")))


def _norm_rope_backward(dz, raw, gg, cos, s_next, s_prev):
    r = lax.rsqrt(jnp.mean(raw * raw, axis=-1, keepdims=True) + EPS)
    n = raw * r
    dy = _rope_transpose(dz, cos, s_next, s_prev)
    dn = dy * gg
    return r * (dn - n * jnp.mean(dn * n, axis=-1, keepdims=True)), jnp.sum(dy * n, axis=0, keepdims=True)


def attention_backward(dproj, q, k, proj, o, lse, do, qgain, tabs, offs, Tc, name, hosted=None):
    Ta = k.shape[0]
    tq = ROW_TILE
    nctx = Tc // tq
    kc = _kv_chunk(Ta)
    gw = GROUP * HEAD_DIM
    vblk = offs["v"] // HEAD_DIM
    qblk = offs["q"] // gw
    zscale = HEAD_DIM ** -0.5

    def body(dp_any, q_ref, k_ref, v_ref, o_ref, lse_ref, do_ref, raw_ref, g_ref, tab_ref, dqr_ref, dk_ref, dv_ref, dg_ref):
        j, i = pl.program_id(0), pl.program_id(1)

        @pl.when(i == 0)
        def _():
            dk_ref[...] = jnp.zeros_like(dk_ref)
            dv_ref[...] = jnp.zeros_like(dv_ref)

        @pl.when((i == 0) & (j == 0))
        def _():
            dg_ref[...] = jnp.zeros_like(dg_ref)

        @pl.when(i < nctx)
        def _():
            dqr_ref[...] = jnp.zeros_like(dqr_ref)

        @pl.when(i >= nctx)
        def _():
            qs = _stack_heads(q_ref[...])
            dob = do_ref[...]
            dos = _stack_heads(dob)
            delta = jnp.concatenate(
                [jnp.sum(dob[:, h * HEAD_DIM:(h + 1) * HEAD_DIM].astype(F32)
                         * o_ref[:, h * HEAD_DIM:(h + 1) * HEAD_DIM].astype(F32), axis=-1, keepdims=True)
                 for h in range(GROUP)], axis=0)
            lse = jnp.concatenate([lse_ref[0, :, h:h + 1] for h in range(GROUP)], axis=0)

            def step(c, dq):
                rows = pl.ds(pl.multiple_of(c * kc, kc), kc)
                kk = k_ref[rows, :]
                vv = v_ref[rows, :]
                s = lax.dot_general(qs, kk, NT, preferred_element_type=F32)
                p = jnp.exp2(s - lse)
                dp = lax.dot_general(dos, vv, NT, preferred_element_type=F32)
                ds = (p * (dp - delta)).astype(BF)
                dv_ref[rows, :] += lax.dot_general(p.astype(BF), dos, TN, preferred_element_type=F32)
                dk_ref[rows, :] += lax.dot_general(ds, qs, TN, preferred_element_type=F32)
                return dq + jnp.dot(ds, kk, preferred_element_type=F32)

            dq = lax.fori_loop(0, Ta // kc, step, jnp.zeros((GROUP * tq, HEAD_DIM), F32))
            per_head = lambda t: jnp.concatenate([t] * GROUP, axis=0)
            dr, dg = _norm_rope_backward(dq * zscale, _stack_heads(raw_ref[...]).astype(F32), g_ref[...],
                                         per_head(tab_ref[0]), per_head(tab_ref[1]), per_head(tab_ref[2]))
            for h in range(GROUP):
                dqr_ref[:, h * HEAD_DIM:(h + 1) * HEAD_DIM] = dr[h * tq:(h + 1) * tq].astype(BF)
            dg_ref[0:1, :] += dg

    lat = lambda j, i: (jnp.maximum(i - nctx, 0), j)
    (dproj, dk, dv, dqg), exchanged = _call(
        body, name=name, grid=(N_KV_HEADS, Ta // tq),
        in_specs=[ANY, pl.BlockSpec((tq, gw), lambda j, i: (i, j)),
                  pl.BlockSpec((Ta, HEAD_DIM), lambda j, i: (0, j)),
                  pl.BlockSpec((Ta, HEAD_DIM), lambda j, i: (0, vblk + j)),
                  pl.BlockSpec((tq, gw), lat),
                  pl.BlockSpec((1, tq, GROUP), lambda j, i: (j, jnp.maximum(i - nctx, 0), 0)),
                  pl.BlockSpec((tq, gw), lat),
                  pl.BlockSpec((tq, gw), lambda j, i: (i, qblk + j)),
                  pl.BlockSpec((1, HEAD_DIM), lambda j, i: (0, 0)),
                  pl.BlockSpec((3, tq, HEAD_DIM), lambda j, i: (0, i, 0))],
        out_specs=[pl.BlockSpec((tq, gw), lambda j, i: (i, qblk + j)),
                   pl.BlockSpec((Ta, HEAD_DIM), lambda j, i: (0, j)),
                   pl.BlockSpec((Ta, HEAD_DIM), lambda j, i: (0, j)),
                   pl.BlockSpec((8, HEAD_DIM), lambda j, i: (0, 0))],
        out_shape=[jax.ShapeDtypeStruct(dproj.shape, BF),
                   jax.ShapeDtypeStruct((Ta, N_KV_HEADS * HEAD_DIM), F32),
                   jax.ShapeDtypeStruct((Ta, N_KV_HEADS * HEAD_DIM), F32),
                   jax.ShapeDtypeStruct((8, HEAD_DIM), F32)],
        operands=[dproj, q, k, proj, o, lse, do, proj, qgain, tabs], hosted=hosted, aliases={0: 0},
        params=_params(vmem=VMEM_BIG, sem=("arbitrary", "arbitrary")))
    return (dproj, dk, dv, dqg), exchanged


def kv_backward(dproj, dk, dv, proj, gain, tabs, offs, name):
    Ta = proj.shape[0]
    tm = _pick(Ta, (768, 512, ROW_TILE))
    kw = N_KV_HEADS * HEAD_DIM
    cb = offs["k"] // (2 * kw)
    kb = offs["k"] // kw
    zscale = 1.0 / LOG2E

    def body(dp_any, dk_ref, dv_ref, raw_ref, g_ref, tab_ref, o_ref, dg_ref):
        @pl.when(pl.program_id(0) == 0)
        def _():
            dg_ref[...] = jnp.zeros_like(dg_ref)

        cos, s_next, s_prev = tab_ref[0], tab_ref[1], tab_ref[2]
        dg = jnp.zeros((1, HEAD_DIM), F32)
        for h in range(N_KV_HEADS):
            sl = slice(h * HEAD_DIM, (h + 1) * HEAD_DIM)
            dr, dgh = _norm_rope_backward(dk_ref[:, sl] * zscale, raw_ref[:, sl].astype(F32), g_ref[...], cos, s_next, s_prev)
            o_ref[:, sl] = dr.astype(BF)
            dg = dg + dgh
        o_ref[:, kw:2 * kw] = dv_ref[...].astype(BF)
        dg_ref[0:1, :] += dg

    return pl.pallas_call(
        body, name=name, grid=(Ta // tm,),
        in_specs=[ANY, _row(tm, kw), _row(tm, kw), pl.BlockSpec((tm, kw), lambda i: (i, kb)),
                  _resident((1, HEAD_DIM)), pl.BlockSpec((3, tm, HEAD_DIM), lambda i: (0, i, 0))],
        out_specs=[pl.BlockSpec((tm, 2 * kw), lambda i: (i, cb)), pl.BlockSpec((8, HEAD_DIM), lambda i: (0, 0))],
        out_shape=[jax.ShapeDtypeStruct(dproj.shape, BF), jax.ShapeDtypeStruct((8, HEAD_DIM), F32)],
        input_output_aliases={0: 0},
        compiler_params=_params(sem=("arbitrary",)),
    )(dproj, dk, dv, proj, gain, tabs)


def merge_forward(x1, mod6, yc, o, proj, w_bc, w_ba, w_o, offs, Tc, name):
    T, D = yc.shape[0], x1.shape[1]
    tm = ROW_TILE
    roff = Tc // tm
    gb = offs["gt"] // (2 * D)

    def body(x_ref, mod_ref, yc_ref, o_ref, gt_ref, wbc_ref, wba_ref, wo_ref, xo_ref, pc_ref, pa_ref, m_ref, z_ref):
        gate = mod_ref[0][2:3]
        pc = jnp.dot(yc_ref[...], wbc_ref[...], preferred_element_type=F32)
        pa = jnp.dot(o_ref[...], wba_ref[...], preferred_element_type=F32)
        pc_ref[...] = pc.astype(BF)
        pa_ref[...] = pa.astype(BF)
        mb = (_sigmoid(gt_ref[:, 0:D].astype(F32)) * pc + _sigmoid(gt_ref[:, D:2 * D].astype(F32)) * pa).astype(BF)
        m_ref[...] = mb
        z = jnp.dot(mb, wo_ref[...], preferred_element_type=F32)
        z_ref[...] = z.astype(BF)
        xo_ref[...] = x_ref[...] + gate * z

    return pl.pallas_call(
        body, name=name, grid=(T // tm,),
        in_specs=[pl.BlockSpec((tm, D), lambda i: (i + roff, 0)), _mod_spec(D, 1, 0), _row(tm, yc.shape[1]), _row(tm, o.shape[1]),
                  pl.BlockSpec((tm, 2 * D), lambda i: (i + roff, gb)),
                  _resident(w_bc.shape), _resident(w_ba.shape), _resident(w_o.shape)],
        out_specs=[_row(tm, D)] * 5,
        out_shape=[jax.ShapeDtypeStruct((T, D), F32)] + [jax.ShapeDtypeStruct((T, D), BF)] * 4,
        compiler_params=_params(vmem=VMEM_BIG, sem=("arbitrary",)),
    )(x1, mod6, yc, o, proj, w_bc, w_ba, w_o)


def merge_backward_rows(dx2, mod6, z, pc, pa, proj, w_bc, w_ba, w_o, offs, Tc, name):
    T, D = dx2.shape
    Ta, P = proj.shape
    tm = ROW_TILE
    nctx = Tc // tm
    gb = offs["gt"] // (2 * D)
    dcw, dqw = w_bc.shape[0], w_ba.shape[0]

    def body(dx_ref, mod_ref, z_ref, pc_ref, pa_ref, gt_ref, wbc_ref, wba_ref, wo_ref,
             dgt_ref, dg_ref, dpc_ref, dpa_ref, dyc_ref, do_ref, acc_ref):
        i = pl.program_id(0)

        @pl.when(i == 0)
        def _():
            acc_ref[...] = jnp.zeros_like(acc_ref)

        @pl.when(i < nctx)
        def _():
            dgt_ref[...] = jnp.zeros_like(dgt_ref)

        @pl.when(i >= nctx)
        def _():
            gate = mod_ref[0][2:3]
            dx = dx_ref[...]
            acc_ref[0:1, :] += jnp.sum(dx * z_ref[...].astype(F32), axis=0, keepdims=True)
            dgb = (dx * gate).astype(BF)
            dg_ref[...] = dgb
            dm = lax.dot_general(dgb, wo_ref[...], NT, preferred_element_type=F32)
            sc = _sigmoid(gt_ref[:, 0:D].astype(F32))
            sa = _sigmoid(gt_ref[:, D:2 * D].astype(F32))
            pc = pc_ref[...].astype(F32)
            pa = pa_ref[...].astype(F32)
            dpc = (dm * sc).astype(BF)
            dpa = (dm * sa).astype(BF)
            dpc_ref[...] = dpc
            dpa_ref[...] = dpa
            dgt_ref[:, 0:D] = ((dm * pc) * (sc * (1.0 - sc))).astype(BF)
            dgt_ref[:, D:2 * D] = ((dm * pa) * (sa * (1.0 - sa))).astype(BF)
            dyc_ref[...] = lax.dot_general(dpc, wbc_ref[...], NT, preferred_element_type=F32).astype(BF)
            do_ref[...] = lax.dot_general(dpa, wba_ref[...], NT, preferred_element_type=F32).astype(BF)

    lat = lambda n: pl.BlockSpec((tm, n), lambda i: (jnp.maximum(i - nctx, 0), 0))
    return pl.pallas_call(
        body, name=name, grid=(Ta // tm,),
        in_specs=[lat(D), _mod_spec(D, 1, 0), lat(D), lat(D), lat(D),
                  pl.BlockSpec((tm, 2 * D), lambda i: (i, gb)),
                  _resident(w_bc.shape), _resident(w_ba.shape), _resident(w_o.shape)],
        out_specs=[pl.BlockSpec((tm, 2 * D), lambda i: (i, gb)), lat(D), lat(D), lat(D), lat(dcw), lat(dqw),
                   pl.BlockSpec((8, D), lambda i: (0, 0))],
        out_shape=[jax.ShapeDtypeStruct((Ta, P), BF)] + [jax.ShapeDtypeStruct((T, D), BF)] * 3
                  + [jax.ShapeDtypeStruct((T, dcw), BF), jax.ShapeDtypeStruct((T, dqw), BF), jax.ShapeDtypeStruct((8, D), F32)],
        compiler_params=_params(vmem=VMEM_BIG, sem=("arbitrary",)),
    )(dx2, mod6, z, pc, pa, proj, w_bc, w_ba, w_o)


def proj_backward_rows(dproj, dres, xa, mod6, g, w_in, nctx, name, hosted=None):
    Tr, D = xa.shape
    P = w_in.shape[1]
    tm = ROW_TILE

    def body(dp_ref, dres_ref, x_ref, mod_ref, g_ref, w_ref, dx_ref, acc_ref):
        i = pl.program_id(0)

        @pl.when((i == 0) | (i == nctx))
        def _():
            acc_ref[...] = jnp.zeros_like(acc_ref)

        x = x_ref[...]
        scale = mod_ref[0][1:2]
        gg = g_ref[...]
        dhm = lax.dot_general(dp_ref[...], w_ref[...], NT, preferred_element_type=F32)
        r = lax.rsqrt(jnp.mean(x * x, axis=-1, keepdims=True) + EPS)
        xh = x * r
        dshift = jnp.sum(dhm, axis=0, keepdims=True)
        dscale = jnp.sum(dhm * (xh * gg), axis=0, keepdims=True)
        dxh_g = dhm * (1.0 + scale)
        dg = jnp.sum(dxh_g * xh, axis=0, keepdims=True)
        dxh = dxh_g * gg
        res = jnp.where(i < nctx, 0.0, dres_ref[...])
        dx_ref[...] = res + r * (dxh - xh * jnp.mean(dxh * xh, axis=-1, keepdims=True))
        for k, val in enumerate((dshift, dscale, dg)):
            acc_ref[0, k:k + 1, :] += val

    return _call(
        body, name=name, grid=(Tr // tm,),
        in_specs=[_row(tm, P), pl.BlockSpec((tm, D), lambda i: (jnp.maximum(i - nctx, 0), 0)), _row(tm, D),
                  _mod_spec(D, 1, nctx), _resident((1, D)), _resident(w_in.shape)],
        out_specs=[_row(tm, D), _acc_spec(D, nctx)],
        out_shape=[jax.ShapeDtypeStruct((Tr, D), F32), jax.ShapeDtypeStruct((2, 8, D), F32)],
        operands=[dproj, dres, xa, mod6, g, w_in], hosted=hosted,
        params=_params(vmem=VMEM_BIG, sem=("arbitrary",)))


def _adam_update(w, g, m, v):
    c1 = 1.0 - ADAM_B1 ** ADAM_STEP
    c2 = 1.0 - ADAM_B2 ** ADAM_STEP
    m = ADAM_B1 * m + (1.0 - ADAM_B1) * g
    v = ADAM_B2 * v + (1.0 - ADAM_B2) * (g * g)
    return -ADAM_LR * ((m / c1) / (jnp.sqrt(v / c2) + ADAM_EPS) + ADAM_WD * w), m, v


def adamw(w, g, m, v, name, dep=None):
    R, C = w.shape
    tr = _row_tile(R, C)
    deps = [] if dep is None else [dep]

    def body(w_ref, g_ref, m_ref, v_ref, *rest):
        d_ref, nm_ref, nv_ref = rest[len(deps):]
        d_ref[...], nm_ref[...], nv_ref[...] = _adam_update(w_ref[...], g_ref[...], m_ref[...], v_ref[...])

    blk = pl.BlockSpec((tr, C), lambda i: (i, 0))
    return pl.pallas_call(
        body, name=name, grid=(R // tr,),
        in_specs=[blk] * 4 + [ANY] * len(deps), out_specs=[blk] * 3,
        out_shape=[jax.ShapeDtypeStruct((R, C), F32)] * 3,
        compiler_params=_params(vmem=VMEM_BIG, sem=("parallel",)),
    )(w, g, m, v, *deps)


def adamw_summed(recv, w, m, v, name, dep=None):
    R, C = w.shape
    tr = _row_tile(R, C)
    deps = [] if dep is None else [dep]

    def body(r_ref, w_ref, m_ref, v_ref, *rest):
        g_ref, d_ref, nm_ref, nv_ref = rest[len(deps):]
        g = r_ref[0].astype(F32)
        for a in range(1, N_DEV):
            g = g + r_ref[a].astype(F32)
        g_ref[...] = g
        d_ref[...], nm_ref[...], nv_ref[...] = _adam_update(w_ref[...], g, m_ref[...], v_ref[...])

    blk = pl.BlockSpec((tr, C), lambda i: (i, 0))
    return pl.pallas_call(
        body, name=name, grid=(R // tr,),
        in_specs=[pl.BlockSpec((N_DEV, tr, C), lambda i: (0, i, 0)), blk, blk, blk] + [ANY] * len(deps), out_specs=[blk] * 4,
        out_shape=[jax.ShapeDtypeStruct((R, C), F32)] * 4,
        compiler_params=_params(vmem=VMEM_BIG, sem=("parallel",)),
    )(recv, w, m, v, *deps)


def adamw_reduced(p_own, recv_b, w, m, v, name):
    R, C = w.shape
    tr = _row_tile(R, C)

    def body(p_ref, b_ref, w_ref, m_ref, v_ref, g_ref, d_ref, nm_ref, nv_ref):
        g = p_ref[...].astype(F32)
        for j in range(3):
            g = g + b_ref[j].astype(F32)
        g_ref[...] = g
        d_ref[...], nm_ref[...], nv_ref[...] = _adam_update(w_ref[...], g, m_ref[...], v_ref[...])

    blk = pl.BlockSpec((tr, C), lambda r: (r, 0))
    return pl.pallas_call(
        body, name=name, grid=(R // tr,),
        in_specs=[blk, pl.BlockSpec((3, tr, C), lambda r: (0, r, 0)), blk, blk, blk], out_specs=[blk] * 4,
        out_shape=[jax.ShapeDtypeStruct((R, C), F32)] * 4,
        compiler_params=_params(vmem=VMEM_BIG, sem=("parallel",)),
    )(p_own, recv_b, w, m, v)


def _rope_tables(T, Tc):
    rows = T // GRID_W
    n_freq = HEAD_DIM // 4
    inv = ROPE_THETA ** (-jnp.arange(n_freq, dtype=F32) / n_freq)
    ang_r = jnp.arange(rows).astype(F32)[:, None] * inv
    ang_c = jnp.arange(GRID_W).astype(F32)[:, None] * inv
    per_row = lambda a: jnp.broadcast_to(a[:, None, :], (rows, GRID_W, n_freq)).reshape(T, n_freq)
    per_col = lambda a: jnp.broadcast_to(a[None, :, :], (rows, GRID_W, n_freq)).reshape(T, n_freq)
    cr, sr = per_row(jnp.cos(ang_r)), per_row(jnp.sin(ang_r))
    cc, sc = per_col(jnp.cos(ang_c)), per_col(jnp.sin(ang_c))
    zero = jnp.zeros_like(sr)
    cos = jnp.concatenate([cr, cr, cc, cc], axis=1)
    s_next = jnp.concatenate([-sr, zero, -sc, zero], axis=1)
    s_prev = jnp.concatenate([zero, sr, zero, sc], axis=1)
    lat = jnp.stack([cos, s_next, s_prev])
    ctx = jnp.stack([jnp.ones((Tc, HEAD_DIM), F32), jnp.zeros((Tc, HEAD_DIM), F32), jnp.zeros((Tc, HEAD_DIM), F32)])
    return jnp.concatenate([ctx, lat], axis=1)


BIG = ("ffn1_w_in", "ffn1_w_out", "w_in", "w_branch_conv", "w_branch_attn", "w_out", "ffn2_w_in", "ffn2_w_out")


def _regroup_w_in(stacked, D, Dc, qw, kw):
    w = stacked.transpose(1, 0, 2).reshape(D, -1)
    o = 0
    parts = {}
    for nme, wd in (("bg", Dc), ("cg", Dc), ("vc", Dc), ("q", qw), ("k", kw), ("v", kw), ("gt", 2 * D)):
        parts[nme] = w[:, o:o + wd]
        o += wd
    nb = Dc // 128
    cv = jnp.stack([parts[n].reshape(D, nb, 128) for n in ("bg", "cg", "vc")], axis=2).reshape(D, 3 * Dc)
    return jnp.concatenate([cv, parts["q"], parts["gt"], parts["k"], parts["v"]], axis=1)


def _ungroup_w_in_grad(gt_, D, Dc, qw, kw):
    nb = Dc // 128
    cv = gt_[:3 * Dc].reshape(nb, 3, 128, D)
    o = 3 * Dc
    q = gt_[o:o + qw]
    gt = gt_[o + qw:o + qw + 2 * D]
    k = gt_[o + qw + 2 * D:o + qw + 2 * D + kw]
    v = gt_[o + qw + 2 * D + kw:]
    nat = jnp.concatenate([cv[:, 0].reshape(Dc, D), cv[:, 1].reshape(Dc, D), cv[:, 2].reshape(Dc, D), q, k, v, gt], axis=0)
    return nat.reshape(N_DEV, -1, D)


def kernel(x, c, ctx, c_ctx, w_mod, b_mod, norm1_g, norm2_g, norm3_g, ffn1_w_in, ffn1_w_out, w_in, conv_w, q_norm_g, k_norm_g, w_branch_conv, w_branch_attn, w_out, ffn2_w_in, ffn2_w_out, final_g, loss_target, m_c_ctx, m_w_mod, m_b_mod, m_norm1_g, m_norm2_g, m_norm3_g, m_ffn1_w_in, m_ffn1_w_out, m_w_in, m_conv_w, m_q_norm_g, m_k_norm_g, m_w_branch_conv, m_w_branch_attn, m_w_out, m_ffn2_w_in, m_ffn2_w_out, m_final_g, v_c_ctx, v_w_mod, v_b_mod, v_norm1_g, v_norm2_g, v_norm3_g, v_ffn1_w_in, v_ffn1_w_out, v_w_in, v_conv_w, v_q_norm_g, v_k_norm_g, v_w_branch_conv, v_w_branch_attn, v_w_out, v_ffn2_w_in, v_ffn2_w_out, v_final_g):
    weights = dict(c_ctx=c_ctx, w_mod=w_mod, b_mod=b_mod, norm1_g=norm1_g, norm2_g=norm2_g, norm3_g=norm3_g,
                   ffn1_w_in=ffn1_w_in, ffn1_w_out=ffn1_w_out, w_in=w_in, conv_w=conv_w, q_norm_g=q_norm_g,
                   k_norm_g=k_norm_g, w_branch_conv=w_branch_conv, w_branch_attn=w_branch_attn, w_out=w_out,
                   ffn2_w_in=ffn2_w_in, ffn2_w_out=ffn2_w_out, final_g=final_g)
    moms = dict(c_ctx=(m_c_ctx, v_c_ctx), w_mod=(m_w_mod, v_w_mod), b_mod=(m_b_mod, v_b_mod),
                norm1_g=(m_norm1_g, v_norm1_g), norm2_g=(m_norm2_g, v_norm2_g), norm3_g=(m_norm3_g, v_norm3_g),
                ffn1_w_in=(m_ffn1_w_in, v_ffn1_w_in), ffn1_w_out=(m_ffn1_w_out, v_ffn1_w_out), w_in=(m_w_in, v_w_in),
                conv_w=(m_conv_w, v_conv_w), q_norm_g=(m_q_norm_g, v_q_norm_g), k_norm_g=(m_k_norm_g, v_k_norm_g),
                w_branch_conv=(m_w_branch_conv, v_w_branch_conv), w_branch_attn=(m_w_branch_attn, v_w_branch_attn),
                w_out=(m_w_out, v_w_out), ffn2_w_in=(m_ffn2_w_in, v_ffn2_w_in), ffn2_w_out=(m_ffn2_w_out, v_ffn2_w_out),
                final_g=(m_final_g, v_final_g))
    order = list(weights)

    T, D = x.shape[1], x.shape[2]
    Tc = ctx.shape[1]
    nctx = Tc // ROW_TILE
    nd = N_MOD * D
    Dc = conv_w.shape[2] * N_DEV
    qw, kw = N_Q_HEADS * HEAD_DIM, N_KV_HEADS * HEAD_DIM
    offs, o = {}, 0
    for nme, wd in (("cv", 3 * Dc), ("q", qw), ("gt", 2 * D), ("k", kw), ("v", kw)):
        offs[nme] = o
        o += wd

    ax, ay, ac = lax.axis_index("x"), lax.axis_index("y"), lax.axis_index("c")
    me = 4 * ax + 2 * ay + ac

    shard = {n: (jnp.swapaxes(weights[n][0], 0, 1) if n in ("ffn1_w_in", "ffn2_w_in") else weights[n][0]).astype(BF)
             for n in BIG}
    rows2d = lambda a: a.reshape(-1, a.shape[-1])
    full = {}
    mod_cols = w_mod.shape[2]
    cw_loc = conv_w[0]
    cpad = (-(D + CONV_TAPS * cw_loc.shape[1])) % 128
    pay = jnp.concatenate([c.reshape(1, D), cw_loc.reshape(1, -1), jnp.zeros((1, cpad), F32)], axis=1)
    (g_ffn1_in, g_ffn1_out), (call,) = allgather_two_level([shard["ffn1_w_in"], shard["ffn1_w_out"]], "ag_ffn1",
                                                          riders=[pay])
    full["ffn1_w_in"], full["ffn1_w_out"] = rows2d(g_ffn1_in), rows2d(g_ffn1_out)

    conv_full = call[:, 0, D:D + CONV_TAPS * cw_loc.shape[1]].reshape(N_DEV, CONV_TAPS, -1).transpose(1, 0, 2).reshape(CONV_TAPS, Dc)
    b_loc = lax.dynamic_slice_in_dim(b_mod, me * mod_cols, mod_cols, axis=1)
    cctx2 = c_ctx.reshape(1, D)
    mod_part = mod_forward(call, cctx2, w_mod[0], b_loc, "mod_fwd")
    mod_all = allgather_direct(mod_part, "ag_mod")
    mod_lat = lax.dynamic_index_in_dim(mod_all, me, axis=1, keepdims=False).reshape(nd)
    mod_ctx = mod_all[:, N_DEV, :].reshape(nd)
    mod6 = jnp.stack([mod_ctx, mod_lat]).reshape(6, 3, D)

    tabs = _rope_tables(T, Tc)

    srcs1 = (ctx[0], x[0])
    (xa1, hm1, ab1, h1, f1), (g_w_in,) = ffn_forward(
        srcs1, mod6, 0, norm1_g, full["ffn1_w_in"], full["ffn1_w_out"], nctx, "ffn1_fwd",
        hosted=Hosted(gathers=[shard["w_in"]]))
    full["w_in"] = _regroup_w_in(g_w_in, D, Dc, qw, kw)
    merge_names = ("w_branch_conv", "w_branch_attn", "w_out")
    (hx, proj, qr, kr), g_merge = proj_forward(
        xa1, mod6, norm2_g, full["w_in"], q_norm_g, k_norm_g, tabs, offs, nctx, "proj_fwd",
        hosted=Hosted(gathers=[shard[n] for n in merge_names]))
    full.update({n: rows2d(g) for n, g in zip(merge_names, g_merge)})
    yc = conv_forward(proj, conv_full, offs, Tc, "conv_fwd")
    (oa, lse), (g_ffn2_in, g_ffn2_out) = attention_forward(
        qr, kr, proj, offs, Tc, "attn_fwd", hosted=Hosted(gathers=[shard["ffn2_w_in"], shard["ffn2_w_out"]]))
    full["ffn2_w_in"], full["ffn2_w_out"] = rows2d(g_ffn2_in), rows2d(g_ffn2_out)
    x2, pc, pa, mm, zz = merge_forward(xa1, mod6, yc, oa, proj, full["w_branch_conv"], full["w_branch_attn"],
                                       full["w_out"], offs, Tc, "merge_fwd")
    (dx3, hm2, ab2, h2, f2, lacc), _ = ffn_forward((x2,), mod6, 2, norm3_g, full["ffn2_w_in"], full["ffn2_w_out"], 0, "ffn2_fwd",
                                                   final=(loss_target[0], final_g.reshape(1, D)))
    loss = lax.psum(lacc[1, 0], ("x", "y", "c"))

    by_dest = lambda g: g.reshape((N_DEV, -1, g.shape[-1]))
    (dx2, dab2, df2, acc_f2), _ = ffn_backward_rows(dx3, (x2,), mod6, 2, norm3_g, ab2, f2, full["ffn2_w_in"], full["ffn2_w_out"], 0, "ffn2_bwd")
    early = {"ffn2_w_out": tn_matmul(h2, df2, "ffn2_dwout")[0], "ffn2_w_in": tn_matmul(dab2, hm2, "ffn2_dwin")[0]}
    dproj, dgm, dpc, dpa, dyc, do, acc_mg = merge_backward_rows(dx2, mod6, zz, pc, pa, proj, full["w_branch_conv"],
                                                                full["w_branch_attn"], full["w_out"], offs, Tc, "merge_bwd")
    early["w_out"] = tn_matmul(mm, dgm, "dw_out")[0]
    early["w_branch_conv"] = tn_matmul(yc, dpc, "dw_bc")[0]
    early["w_branch_attn"] = tn_matmul(oa, dpa, "dw_ba")[0]
    dproj, dcw = conv_backward(dproj, dyc, proj, conv_full, offs, Tc, "conv_bwd")
    (dproj, dk, dv, dqg), summed = attention_backward(dproj, qr, kr, proj, oa, lse, do, q_norm_g, tabs, offs, Tc, "attn_bwd",
                                                      hosted=Hosted(scatters=[by_dest(g) for g in early.values()]))
    summed = dict(zip(early, summed))
    dproj, dkg = kv_backward(dproj, dk, dv, proj, k_norm_g, tabs, offs, "kv_bwd")
    g_w_in_grad = _ungroup_w_in_grad(tn_matmul(dproj, hx, "dw_in")[0], D, Dc, qw, kw)
    (dxa1, acc_pj), (summed["w_in"],) = proj_backward_rows(dproj, dx2, xa1, mod6, norm2_g, full["w_in"], nctx, "proj_bwd",
                                                          hosted=Hosted(scatters=[g_w_in_grad]))
    (grad_x2d, dab1, df1, acc_f1), _ = ffn_backward_rows(
        dxa1, srcs1, mod6, 0, norm1_g, ab1, f1, full["ffn1_w_in"], full["ffn1_w_out"], nctx, "ffn1_bwd")
    g_ffn1_w_in, (summed["ffn1_w_out"],) = tn_matmul(
        dab1, hm1, "ffn1_dwin", hosted=Hosted(scatters=[by_dest(tn_matmul(h1, df1, "ffn1_dwout")[0])]))
    grad_x = grad_x2d[None]

    zero_d = jnp.zeros((D,), F32)
    dlat = jnp.concatenate([acc_f1[1, 0], acc_f1[1, 1], acc_f1[1, 2], acc_pj[1, 0], acc_pj[1, 1], acc_mg[0],
                            acc_f2[1, 0], acc_f2[1, 1], acc_f2[1, 2]])
    dctx = jnp.concatenate([acc_f1[0, 0], acc_f1[0, 1], acc_f1[0, 2], acc_pj[0, 0], acc_pj[0, 1]] + [zero_d] * 4)
    small = jnp.concatenate([acc_f1[0, 3] + acc_f1[1, 3], acc_pj[0, 2] + acc_pj[1, 2], acc_f2[1, 3],
                             dqg[0], dkg[0], lacc[0], dcw[0:CONV_TAPS].reshape(-1)])
    n_small = small.shape[0]
    pay_b = jnp.concatenate([dlat, dctx, small]).reshape(1, -1)
    gath = allgather_direct(pay_b, "ag_small_grads")
    dlat_loc = lax.dynamic_slice_in_dim(gath[:, 0, :nd], me * mod_cols, mod_cols, axis=1)
    dctx_loc = lax.dynamic_slice_in_dim(gath[:, 0, nd:2 * nd], me * mod_cols, mod_cols, axis=1)
    g_wmod, pc_part, small_sum = mod_backward(call, cctx2, w_mod[0], dlat_loc, dctx_loc, gath, 2 * nd, n_small, "mod_bwd")
    pcs = allgather_direct(pc_part, "ag_cctx")
    g_bmod, g_cctx = bmod_and_cctx_grad(gath, pcs, cctx2, nd, "small_bwd")
    sm = small_sum[0]
    g_conv_full = sm[3 * D + 2 * HEAD_DIM + D:].reshape(CONV_TAPS, Dc)
    g_conv = lax.dynamic_slice_in_dim(g_conv_full, me * cw_loc.shape[1], cw_loc.shape[1], axis=1)
    gsmall = dict(
        c_ctx=g_cctx, w_mod=g_wmod, b_mod=g_bmod, norm1_g=sm[0:D][None], norm2_g=sm[D:2 * D][None],
        norm3_g=sm[2 * D:3 * D][None], q_norm_g=sm[3 * D:3 * D + HEAD_DIM][None],
        k_norm_g=sm[3 * D + HEAD_DIM:3 * D + 2 * HEAD_DIM][None],
        final_g=sm[3 * D + 2 * HEAD_DIM:3 * D + 2 * HEAD_DIM + D][None], conv_w=g_conv)

    flipped = ("ffn1_w_in", "w_in", "ffn2_w_in")
    last = "ffn1_w_in"
    results = {}

    def update(n, dep=None):
        w = weights[n]
        shp = w.shape
        if n in flipped:
            two_d = lambda a: jnp.swapaxes(a[0], 0, 1)
            back = lambda a: jnp.swapaxes(a, 0, 1)[None]
        else:
            two_d = lambda a: a.reshape(-1, shp[-1])
            back = lambda a: a.reshape(shp)
        m, v = moms[n]
        if n in summed:
            g2, d, nm, nv = adamw_summed(summed[n], two_d(w), two_d(m), two_d(v), "adamw_" + n, dep=dep)
        else:
            g2 = gsmall[n].reshape(two_d(w).shape)
            d, nm, nv = adamw(two_d(w), g2, two_d(m), two_d(v), "adamw_" + n, dep=dep)
        results[n] = tuple(back(a) for a in (g2, d, nm, nv))
        return d

    last_sems, last_recv_sems, last_src, last_land, token = scatter_start(by_dest(g_ffn1_w_in), g_cctx, "rs_ffn1_start")
    heavy = ("w_mod", "w_in", "ffn2_w_in", "ffn2_w_out", "ffn1_w_out", "w_out")
    deltas = {n: update(n, dep=token) for n in order if n != last}
    after = jnp.concatenate([deltas[n][0:1, 0:1] for n in heavy], axis=1)
    g_done, land_done = scatter_wait(last_sems, last_recv_sems, last_src, last_land, after, "rs_ffn1_wait")
    own = lax.dynamic_index_in_dim(g_done, me, axis=0, keepdims=True)
    summed[last] = lax.dynamic_update_slice_in_dim(land_done, own, me, axis=0)
    update(last)
    cols = list(zip(*(results[n] for n in order)))
    return (loss, grad_x, *cols[0], *cols[1], *cols[2], *cols[3])
```

```python
import math

import jax
import jax.numpy as jnp
from jax import lax
from jax.experimental import pallas as pl
from jax.experimental.pallas import tpu as pltpu

F32 = jnp.float32
BF = jnp.bfloat16
EPS = 1e-6
N_DEV = 8
HEAD_DIM = 128
N_Q_HEADS = 8
N_KV_HEADS = 2
GROUP = N_Q_HEADS // N_KV_HEADS
GRID_W = 64
ROPE_THETA = 10000.0
CONV_TAPS = 3
N_MOD = 9
ADAM_LR = 0.001
ADAM_B1 = 0.9
ADAM_B2 = 0.999
ADAM_EPS = 1e-08
ADAM_WD = 0.01
ADAM_STEP = 10
ROW_TILE = 256
VMEM_BIG = 56 << 20
MESH_ID = pl.DeviceIdType.MESH
HIGHEST = lax.Precision.HIGHEST
NT = (((1,), (1,)), ((), ()))
TN = (((0,), (0,)), ((), ()))
LOG2E = math.log2(math.e)


def _pick(n, cands):
    for c in cands:
        if n % c == 0:
            return c
    return n


def _params(vmem=None, sem=None):
    kw = {}
    if vmem is not None:
        kw["vmem_limit_bytes"] = vmem
    if sem is not None:
        kw["dimension_semantics"] = sem
    return pltpu.CompilerParams(**kw)


def _resident(shape):
    nd = len(shape)
    return pl.BlockSpec(shape, lambda *_: (0,) * nd, pipeline_mode=pl.Buffered(1))


def _sigmoid(x):
    return jax.nn.sigmoid(x)


ANY = pl.BlockSpec(memory_space=pl.ANY)


def _coords():
    return lax.axis_index("x"), lax.axis_index("y"), lax.axis_index("c")


def _flip(v, bit):
    return 1 - v if bit else v


def _remote(src, dst, ssem, rsem, dev):
    return pltpu.make_async_remote_copy(src_ref=src, dst_ref=dst, send_sem=ssem, recv_sem=rsem,
                                        device_id=dev, device_id_type=MESH_ID)


def allgather_direct(v, name):
    def body(v_ref, out_ref, ssem, rsem, lsem):
        x, y, c = _coords()
        me = 4 * x + 2 * y + c
        mine = pltpu.make_async_copy(v_ref, out_ref.at[me], lsem)
        mine.start()
        cps = []
        for p in range(1, N_DEV):
            px, py, pc = (p >> 2) & 1, (p >> 1) & 1, p & 1
            cps.append(_remote(v_ref, out_ref.at[me], ssem.at[p - 1], rsem.at[p - 1],
                               (_flip(x, px), _flip(y, py), _flip(c, pc))))
        for cp in cps:
            cp.start()
        for p in range(1, N_DEV):
            px, py, pc = (p >> 2) & 1, (p >> 1) & 1, p & 1
            src = 4 * _flip(x, px) + 2 * _flip(y, py) + _flip(c, pc)
            _remote(v_ref, out_ref.at[src], ssem.at[p - 1], rsem.at[p - 1], (x, y, c)).wait_recv()
        for cp in cps:
            cp.wait_send()
        mine.wait()

    return pl.pallas_call(
        body, name=name,
        out_shape=jax.ShapeDtypeStruct((N_DEV,) + v.shape, v.dtype),
        in_specs=[ANY], out_specs=ANY,
        scratch_shapes=[pltpu.SemaphoreType.DMA((N_DEV - 1,)), pltpu.SemaphoreType.DMA((N_DEV - 1,)),
                        pltpu.SemaphoreType.DMA],
    )(v)


def allgather_two_level(shards, name, riders=()):
    n = len(shards)
    ride = Hosted(gathers=riders)
    r = ride.n

    def body(*refs):
        v_refs, rin, out_refs, rout = refs[:n], refs[n:n + r], refs[n + r:2 * n + r], refs[2 * n + r:2 * n + 2 * r]
        (ssem, rsem, lsem), rsems = refs[2 * n + 2 * r:2 * n + 2 * r + 3], refs[2 * n + 2 * r + 3:]
        x, y, c = _coords()
        me = (x, y, c)
        sib = (x, y, 1 - c)
        chips = [(1 - x, y), (x, 1 - y), (1 - x, 1 - y)]
        if r:
            ride.start(rin, rout, *rsems)

        def slot(w, px, py, pc):
            return out_refs[w].at[4 * px + 2 * py + pc]

        def sem(w, k):
            return ssem.at[7 * w + k], rsem.at[7 * w + k]

        mine = [pltpu.make_async_copy(v_refs[w], slot(w, *me), lsem.at[w]) for w in range(n)]
        for cp in mine:
            cp.start()
        first = []
        for w in range(n):
            first.append(_remote(v_refs[w], slot(w, *me), *sem(w, 0), sib))
            first += [_remote(v_refs[w], slot(w, *me), *sem(w, 1 + j), (*chip, c)) for j, chip in enumerate(chips)]
        for cp in first:
            cp.start()
        passed = []
        for w in range(n):
            for j, chip in enumerate(chips):
                _remote(v_refs[w], slot(w, *chip, c), *sem(w, 1 + j), me).wait_recv()
                cp = _remote(slot(w, *chip, c), slot(w, *chip, c), *sem(w, 4 + j), sib)
                cp.start()
                passed.append(cp)
        for w in range(n):
            _remote(v_refs[w], slot(w, x, y, 1 - c), *sem(w, 0), me).wait_recv()
            for j, chip in enumerate(chips):
                _remote(v_refs[w], slot(w, *chip, 1 - c), *sem(w, 4 + j), me).wait_recv()
        for cp in first + passed:
            cp.wait_send()
        for cp in mine:
            cp.wait()
        if r:
            ride.wait(rin, rout, *rsems)

    res = pl.pallas_call(
        body, name=name,
        out_shape=[jax.ShapeDtypeStruct((N_DEV,) + s.shape, s.dtype) for s in shards] + ride.out_shapes,
        in_specs=[ANY] * (n + r), out_specs=[ANY] * (n + r),
        scratch_shapes=[pltpu.SemaphoreType.DMA((7 * n,)), pltpu.SemaphoreType.DMA((7 * n,)),
                        pltpu.SemaphoreType.DMA((n,))] + (ride.scratch if r else []),
    )(*shards, *riders)
    return list(res[:n]), list(res[n:])


SEM = pl.BlockSpec(memory_space=pltpu.SEMAPHORE)
IN_HBM = pl.BlockSpec(memory_space=pltpu.HBM)
DATAFLOW = pltpu.SideEffectType.DATAFLOW_SIDE_EFFECTING


def _scatter_descriptors(src_ref, land_ref, ssem, rsem, arrivals):
    x, y, c = _coords()
    me = 4 * x + 2 * y + c
    cps = []
    for p in range(1, N_DEV):
        px, py, pc = _flip(x, (p >> 2) & 1), _flip(y, (p >> 1) & 1), _flip(c, p & 1)
        peer = 4 * px + 2 * py + pc
        if arrivals:
            cps.append(_remote(src_ref.at[me], land_ref.at[peer], ssem.at[p - 1], rsem.at[p - 1], (x, y, c)))
        else:
            cps.append(_remote(src_ref.at[peer], land_ref.at[me], ssem.at[p - 1], rsem.at[p - 1], (px, py, pc)))
    return cps


def scatter_start(g, before, name):
    nb = len(before)

    def body(g_ref, land_ref, *rest):
        ssem, rsem, g_thru, land_thru, token = rest[nb:]
        for cp in _scatter_descriptors(g_ref, land_ref, ssem, rsem, False):
            cp.start()
        token[...] = jnp.zeros_like(token)

    return pl.pallas_call(
        body, name=name,
        out_shape=(pltpu.SemaphoreType.DMA((N_DEV - 1,)), pltpu.SemaphoreType.DMA((N_DEV - 1,)),
                   pltpu.HBM(g.shape, g.dtype), pltpu.HBM(g.shape, g.dtype), jax.ShapeDtypeStruct((8, 128), F32)),
        in_specs=(IN_HBM, IN_HBM) + (ANY,) * nb, out_specs=(SEM, SEM, IN_HBM, IN_HBM, pl.BlockSpec(memory_space=pltpu.VMEM)),
        input_output_aliases={0: 2, 1: 3},
        compiler_params=pltpu.CompilerParams(has_side_effects=DATAFLOW),
    )(pltpu.with_memory_space_constraint(g, pltpu.HBM),
      pltpu.with_memory_space_constraint(lax.empty(g.shape, g.dtype), pltpu.HBM), *before)


def scatter_wait(ssem, rsem, g_thru, land_thru, after, name):
    def body(g_ref, land_ref, ssem_ref, rsem_ref, after_ref, g_out, land_out):
        cps = _scatter_descriptors(g_ref, land_ref, ssem_ref, rsem_ref, True)
        for cp in cps:
            cp.wait_send()
        for cp in cps:
            cp.wait_recv()

    return pl.pallas_call(
        body, name=name,
        out_shape=(pltpu.HBM(g_thru.shape, g_thru.dtype), pltpu.HBM(land_thru.shape, land_thru.dtype)),
        in_specs=(IN_HBM, IN_HBM, SEM, SEM, ANY), out_specs=(IN_HBM, IN_HBM),
        input_output_aliases={0: 0, 1: 1},
        compiler_params=pltpu.CompilerParams(has_side_effects=DATAFLOW),
    )(g_thru, land_thru, ssem, rsem, after)


class Hosted:
    def __init__(self, gathers=(), scatters=()):
        self.items = [(a, False) for a in gathers] + [(a, True) for a in scatters]
        self.n = len(self.items)
        self.operands = [a for a, _ in self.items]
        self.out_shapes = [jax.ShapeDtypeStruct(a.shape if sc else (N_DEV,) + a.shape, a.dtype) for a, sc in self.items]
        self.scratch = [pltpu.SemaphoreType.DMA((7 * self.n,)), pltpu.SemaphoreType.DMA((7 * self.n,)),
                        pltpu.SemaphoreType.DMA((self.n,))]

    def _copies(self, in_refs, out_refs, ssem, rsem, lsem, arrivals):
        x, y, c = _coords()
        me = 4 * x + 2 * y + c
        remote, local = [], []
        for w, (_, sc) in enumerate(self.items):
            src, dst = in_refs[w], out_refs[w]
            local.append(pltpu.make_async_copy(src.at[me] if sc else src, dst.at[me], lsem.at[w]))
            for p in range(1, N_DEV):
                px, py, pc = _flip(x, (p >> 2) & 1), _flip(y, (p >> 1) & 1), _flip(c, p & 1)
                peer = 4 * px + 2 * py + pc
                k = 7 * w + p - 1
                if arrivals:
                    remote.append(_remote(src.at[me] if sc else src, dst.at[peer], ssem.at[k], rsem.at[k], (x, y, c)))
                else:
                    remote.append(_remote(src.at[peer] if sc else src, dst.at[me], ssem.at[k], rsem.at[k], (px, py, pc)))
        return remote, local

    def start(self, in_refs, out_refs, ssem, rsem, lsem):
        sends, local = self._copies(in_refs, out_refs, ssem, rsem, lsem, False)
        for cp in local + sends:
            cp.start()

    def wait(self, in_refs, out_refs, ssem, rsem, lsem):
        arrivals, local = self._copies(in_refs, out_refs, ssem, rsem, lsem, True)
        for cp in arrivals:
            cp.wait_recv()
        for cp in arrivals:
            cp.wait_send()
        for cp in local:
            cp.wait()


def _call(body, *, name, grid, in_specs, out_specs, out_shape, operands, params, scratch_shapes=(), aliases=None, hosted=None):
    n_in, n_out, n_scr = len(in_specs), len(out_specs), len(scratch_shapes)
    h = hosted.n if hosted is not None else 0

    def wrapped(*refs):
        ins, cins = refs[:n_in], refs[n_in:n_in + h]
        outs, couts = refs[n_in + h:n_in + h + n_out], refs[n_in + h + n_out:n_in + 2 * h + n_out]
        rest = refs[n_in + 2 * h + n_out:]
        scr, sems = rest[:n_scr], rest[n_scr:]
        if h:
            ids = [pl.program_id(a) for a in range(len(grid))]
            first, last = ids[0] == 0, ids[0] == grid[0] - 1
            for a in range(1, len(grid)):
                first, last = first & (ids[a] == 0), last & (ids[a] == grid[a] - 1)

            @pl.when(first)
            def _():
                hosted.start(cins, couts, *sems)

        body(*ins, *outs, *scr)
        if h:
            @pl.when(last)
            def _():
                hosted.wait(cins, couts, *sems)

    res = pl.pallas_call(
        wrapped, name=name, grid=grid,
        in_specs=list(in_specs) + [ANY] * h, out_specs=list(out_specs) + [ANY] * h,
        out_shape=list(out_shape) + (hosted.out_shapes if h else []),
        scratch_shapes=list(scratch_shapes) + (hosted.scratch if h else []),
        input_output_aliases=aliases or {}, compiler_params=params,
    )(*operands, *(hosted.operands if h else []))
    return list(res[:n_out]), list(res[n_out:])


def _row_tile(R, C):
    if R * C <= (1 << 18):
        return R
    return max((d for d in range(8, 257, 8) if R % d == 0), default=R)


def _cond_rows(call_ref, cctx_ref, z_ref, D):
    z_ref[...] = jnp.zeros_like(z_ref)
    for a in range(N_DEV):
        z_ref[a:a + 1, :] = call_ref[a][:, :D]
    z_ref[N_DEV:N_DEV + 1, :] = cctx_ref[...]


def mod_forward(call, c_ctx, w_loc, b_loc, name):
    D, cols = w_loc.shape

    def body(call_ref, cctx_ref, w_ref, b_ref, o_ref, z_ref):
        _cond_rows(call_ref, cctx_ref, z_ref, D)
        z = z_ref[...]
        s = z * _sigmoid(z)
        o_ref[...] = jnp.dot(s, w_ref[...], preferred_element_type=F32, precision=HIGHEST) + b_ref[...]

    return pl.pallas_call(
        body, name=name, out_shape=jax.ShapeDtypeStruct((16, cols), F32),
        scratch_shapes=[pltpu.VMEM((16, D), F32)],
        compiler_params=_params(vmem=VMEM_BIG),
    )(call, c_ctx, w_loc, b_loc)


def mod_backward(call, c_ctx, w_loc, dlat_loc, dctx_loc, gath, n_small_off, n_small, name):
    D, cols = w_loc.shape

    def body(call_ref, cctx_ref, w_ref, dlat_ref, dctx_ref, g_ref, gw_ref, pc_ref, small_ref, z_ref, dm_ref):
        _cond_rows(call_ref, cctx_ref, z_ref, D)
        z = z_ref[...]
        s = z * _sigmoid(z)
        dctx = dctx_ref[0:1, :]
        for a in range(1, N_DEV):
            dctx = dctx + dctx_ref[a:a + 1, :]
        dm_ref[...] = jnp.zeros_like(dm_ref)
        dm_ref[0:N_DEV, :] = dlat_ref[...]
        dm_ref[N_DEV:N_DEV + 1, :] = dctx
        gw_ref[...] = lax.dot_general(s, dm_ref[...], TN, preferred_element_type=F32, precision=HIGHEST)
        pc_ref[...] = lax.dot_general(dctx, w_ref[...], NT, preferred_element_type=F32, precision=HIGHEST)
        acc = g_ref[0][:, n_small_off:n_small_off + n_small]
        for a in range(1, N_DEV):
            acc = acc + g_ref[a][:, n_small_off:n_small_off + n_small]
        small_ref[...] = acc

    return pl.pallas_call(
        body, name=name,
        out_shape=(jax.ShapeDtypeStruct((D, cols), F32), jax.ShapeDtypeStruct((1, D), F32),
                   jax.ShapeDtypeStruct((1, n_small), F32)),
        scratch_shapes=[pltpu.VMEM((16, D), F32), pltpu.VMEM((16, cols), F32)],
        compiler_params=_params(vmem=VMEM_BIG),
    )(call, c_ctx, w_loc, dlat_loc, dctx_loc, gath)


def bmod_and_cctx_grad(gath, pcs, c_ctx, nd, name):
    D = c_ctx.shape[-1]

    def body(g_ref, pc_ref, cctx_ref, gb_ref, gc_ref):
        acc = g_ref[0][:, :nd] + g_ref[0][:, nd:2 * nd]
        for a in range(1, N_DEV):
            acc = acc + (g_ref[a][:, :nd] + g_ref[a][:, nd:2 * nd])
        gb_ref[...] = acc
        p = pc_ref[0]
        for a in range(1, N_DEV):
            p = p + pc_ref[a]
        z = cctx_ref[...]
        sg = _sigmoid(z)
        gc_ref[...] = p * (sg * (1.0 + z * (1.0 - sg)))

    return pl.pallas_call(
        body, name=name,
        out_shape=(jax.ShapeDtypeStruct((1, nd), F32), jax.ShapeDtypeStruct((1, D), F32)),
    )(gath, pcs, c_ctx)


def _mod_spec(D, which, nctx):
    return pl.BlockSpec((1, 3, D), lambda i: (jnp.where(i < nctx, 0, 3) + which, 0, 0))


def _acc_spec(D, nctx):
    return pl.BlockSpec((1, 8, D), lambda i: (jnp.where(i < nctx, 0, 1), 0, 0))


def _row(tm, n):
    return pl.BlockSpec((tm, n), lambda i: (i, 0))


def _two_stream_specs(tm, D, nctx):
    return [pl.BlockSpec((tm, D), lambda i: (jnp.minimum(i, nctx - 1), 0)),
            pl.BlockSpec((tm, D), lambda i: (jnp.maximum(i - nctx, 0), 0))]


def _final_norm_loss_backward(x, tgt, gg, i, dx_ref, acc_ref):
    @pl.when(i == 0)
    def _():
        acc_ref[...] = jnp.zeros_like(acc_ref)

    D = x.shape[1]
    r = lax.rsqrt(jnp.mean(x * x, axis=-1, keepdims=True) + EPS)
    xh = x * r
    e = xh * gg - tgt
    part = 0.5 * jnp.sum(jnp.mean(e * e, axis=-1, keepdims=True), axis=0, keepdims=True)
    dy = e * (1.0 / D)
    dyg = dy * gg
    dx_ref[...] = r * (dyg - xh * jnp.mean(dyg * xh, axis=-1, keepdims=True))
    acc_ref[0:1, :] += jnp.sum(dy * xh, axis=0, keepdims=True)
    acc_ref[1:2, :] += jnp.broadcast_to(part, (1, D))


def _hidden_chunks(F):
    step = 1024 if F % 256 == 0 else F
    return [(lo, min(lo + step, F)) for lo in range(0, F, step)]


def ffn_forward(srcs, mod6, which, g, wt, w_out, nctx, name, hosted=None, final=None):
    D = srcs[-1].shape[1]
    Tr = sum(s.shape[0] for s in srcs)
    F = wt.shape[0] // 2
    tm = ROW_TILE
    two = len(srcs) == 2
    nfin = 0 if final is None else 2

    def body(*refs):
        x_refs, fin_refs, rest = refs[:len(srcs)], refs[len(srcs):len(srcs) + nfin], refs[len(srcs) + nfin:]
        mod_ref, g_ref, wt_ref, wout_ref, xo_ref, hm_ref, ab_ref, h_ref, f_ref = rest[:9]
        x = jnp.where(pl.program_id(0) < nctx, x_refs[0][...], x_refs[1][...]) if two else x_refs[0][...]
        ms = mod_ref[0]
        shift, scale, gate = ms[0:1], ms[1:2], ms[2:3]
        r = lax.rsqrt(jnp.mean(x * x, axis=-1, keepdims=True) + EPS)
        hb = (((x * r) * g_ref[...]) * (1.0 + scale) + shift).astype(BF)
        hm_ref[...] = hb
        f = jnp.zeros((tm, D), F32)
        for lo, hi in _hidden_chunks(F):
            a = lax.dot_general(hb, wt_ref[lo:hi, :], NT, preferred_element_type=F32)
            b = lax.dot_general(hb, wt_ref[F + lo:F + hi, :], NT, preferred_element_type=F32)
            ab_ref[:, lo:hi] = a.astype(BF)
            ab_ref[:, F + lo:F + hi] = b.astype(BF)
            h = ((a * _sigmoid(a)) * b).astype(BF)
            h_ref[:, lo:hi] = h
            f = f + jnp.dot(h, wout_ref[lo:hi, :], preferred_element_type=F32)
        f_ref[...] = f.astype(BF)
        xo = x + (0.5 * gate) * f
        if final is None:
            xo_ref[...] = xo
        else:
            _final_norm_loss_backward(xo, fin_refs[0][...], fin_refs[1][...], pl.program_id(0), xo_ref, rest[9])

    src_specs = _two_stream_specs(tm, D, nctx) if two else [_row(tm, D)]
    fin = final is not None
    return _call(
        body, name=name, grid=(Tr // tm,),
        in_specs=src_specs + ([_row(tm, D), _resident((1, D))] if fin else [])
                 + [_mod_spec(D, which, nctx), _resident((1, D)), _resident(wt.shape), _resident(w_out.shape)],
        out_specs=[_row(tm, D), _row(tm, D), _row(tm, 2 * F), _row(tm, F), _row(tm, D)]
                  + ([pl.BlockSpec((8, D), lambda i: (0, 0))] if fin else []),
        out_shape=[jax.ShapeDtypeStruct((Tr, D), F32), jax.ShapeDtypeStruct((Tr, D), BF),
                   jax.ShapeDtypeStruct((Tr, 2 * F), BF), jax.ShapeDtypeStruct((Tr, F), BF),
                   jax.ShapeDtypeStruct((Tr, D), BF)] + ([jax.ShapeDtypeStruct((8, D), F32)] if fin else []),
        operands=[*srcs, *(final or ()), mod6, g, wt, w_out], hosted=hosted,
        params=_params(vmem=VMEM_BIG, sem=("arbitrary",)))


def ffn_backward_rows(dxo, srcs, mod6, which, g, ab, fo, wt, w_out, nctx, name, hosted=None):
    D = srcs[-1].shape[1]
    Tr = sum(s.shape[0] for s in srcs)
    Tl = srcs[-1].shape[0]
    F = wt.shape[0] // 2
    tm = ROW_TILE
    two = len(srcs) == 2

    def body(*refs):
        dxo_ref, x_refs = refs[0], refs[1:1 + len(srcs)]
        mod_ref, g_ref, ab_ref, fo_ref, wt_ref, wout_ref, dx_ref, dab_ref, df_ref, acc_ref = refs[1 + len(srcs):]
        i = pl.program_id(0)

        @pl.when((i == 0) | (i == nctx))
        def _():
            acc_ref[...] = jnp.zeros_like(acc_ref)

        dxo = dxo_ref[...]
        x = jnp.where(i < nctx, x_refs[0][...], x_refs[1][...]) if two else x_refs[0][...]
        ms = mod_ref[0]
        scale, gate = ms[1:2], ms[2:3]
        gg = g_ref[...]
        dgate = jnp.sum(dxo * fo_ref[...].astype(F32), axis=0, keepdims=True) * 0.5
        dfb = (dxo * (0.5 * gate)).astype(BF)
        df_ref[...] = dfb
        dhm = jnp.zeros((tm, D), F32)
        for lo, hi in _hidden_chunks(F):
            dh = lax.dot_general(dfb, wout_ref[lo:hi, :], NT, preferred_element_type=F32)
            a = ab_ref[:, lo:hi].astype(F32)
            b = ab_ref[:, F + lo:F + hi].astype(F32)
            sg = _sigmoid(a)
            da = ((dh * b) * (sg * (1.0 + a * (1.0 - sg)))).astype(BF)
            db = (dh * (a * sg)).astype(BF)
            dab_ref[:, lo:hi] = da
            dab_ref[:, F + lo:F + hi] = db
            dhm = dhm + jnp.dot(da, wt_ref[lo:hi, :], preferred_element_type=F32)
            dhm = dhm + jnp.dot(db, wt_ref[F + lo:F + hi, :], preferred_element_type=F32)
        r = lax.rsqrt(jnp.mean(x * x, axis=-1, keepdims=True) + EPS)
        xh = x * r
        dshift = jnp.sum(dhm, axis=0, keepdims=True)
        dscale = jnp.sum(dhm * (xh * gg), axis=0, keepdims=True)
        dxh_g = dhm * (1.0 + scale)
        dg = jnp.sum(dxh_g * xh, axis=0, keepdims=True)
        dxh = dxh_g * gg
        dx_ref[...] = dxo + r * (dxh - xh * jnp.mean(dxh * xh, axis=-1, keepdims=True))
        for k, val in enumerate((dshift, dscale, dgate, dg)):
            acc_ref[0, k:k + 1, :] += val

    src_specs = _two_stream_specs(tm, D, nctx) if two else [_row(tm, D)]
    dx_spec = pl.BlockSpec((tm, D), lambda i: (jnp.maximum(i - nctx, 0), 0))
    return _call(
        body, name=name, grid=(Tr // tm,),
        in_specs=[_row(tm, D)] + src_specs + [_mod_spec(D, which, nctx), _resident((1, D)), _row(tm, 2 * F), _row(tm, D),
                                              _resident(wt.shape), _resident(w_out.shape)],
        out_specs=[dx_spec, _row(tm, 2 * F), _row(tm, D), _acc_spec(D, nctx)],
        out_shape=[jax.ShapeDtypeStruct((Tl, D), F32), jax.ShapeDtypeStruct((Tr, 2 * F), BF),
                   jax.ShapeDtypeStruct((Tr, D), BF), jax.ShapeDtypeStruct((2, 8, D), F32)],
        operands=[dxo, *srcs, mod6, g, ab, fo, wt, w_out], hosted=hosted,
        params=_params(vmem=VMEM_BIG, sem=("arbitrary",)))


def _token_tile(T):
    return _pick(T, (2048, 1408, 1024, 768, 512, 256, 128))


def tn_matmul(a, b, name, hosted=None):
    T, K = a.shape
    N = b.shape[1]
    tk = _pick(K, (1024, 1408, 1664, 768, 512, 384, 256, 128))
    tn = _pick(N, (1024, 1408, 1664, 768, 512, 384, 256, 128))
    tt = _token_tile(T)
    nt = T // tt

    def body(a_ref, b_ref, o_ref, acc_ref):
        t = pl.program_id(2)

        @pl.when(t == 0)
        def _():
            acc_ref[...] = jnp.zeros_like(acc_ref)

        acc_ref[...] += lax.dot_general(a_ref[...], b_ref[...], TN, preferred_element_type=F32)

        @pl.when(t == nt - 1)
        def _():
            o_ref[...] = acc_ref[...].astype(BF)

    (out,), exchanged = _call(
        body, name=name, grid=(K // tk, N // tn, nt),
        in_specs=[pl.BlockSpec((tt, tk), lambda k, n, t: (t, k)), pl.BlockSpec((tt, tn), lambda k, n, t: (t, n))],
        out_specs=[pl.BlockSpec((tk, tn), lambda k, n, t: (k, n))],
        out_shape=[jax.ShapeDtypeStruct((K, N), BF)],
        scratch_shapes=[pltpu.VMEM((tk, tn), F32)],
        operands=[a, b], hosted=hosted,
        params=_params(vmem=VMEM_BIG, sem=("arbitrary", "arbitrary", "arbitrary")))
    return out, exchanged


def _rope_apply(y, cos, s_next, s_prev):
    return y * cos + pltpu.roll(y, HEAD_DIM - 32, 1) * s_next + pltpu.roll(y, 32, 1) * s_prev


def _rope_transpose(dz, cos, s_next, s_prev):
    return dz * cos + pltpu.roll(dz * s_next, 32, 1) + pltpu.roll(dz * s_prev, HEAD_DIM - 32, 1)


def proj_forward(xa, mod6, g, w_in, qg, kg, tabs, offs, nctx, name, hosted=None):
    Tr, D = xa.shape
    P = w_in.shape[1]
    tm = ROW_TILE
    qo, ko = offs["q"], offs["k"]
    qw, kw = N_Q_HEADS * HEAD_DIM, N_KV_HEADS * HEAD_DIM
    scale_q = HEAD_DIM ** -0.5 * LOG2E

    def body(x_ref, mod_ref, g_ref, w_ref, qg_ref, kg_ref, tab_ref, hx_ref, pr_ref, q_ref, k_ref):
        x = x_ref[...]
        ms = mod_ref[0]
        shift, scale = ms[0:1], ms[1:2]
        r = lax.rsqrt(jnp.mean(x * x, axis=-1, keepdims=True) + EPS)
        hb = (((x * r) * g_ref[...]) * (1.0 + scale) + shift).astype(BF)
        hx_ref[...] = hb
        pr = jnp.dot(hb, w_ref[...], preferred_element_type=F32)
        pr_ref[...] = pr.astype(BF)
        cos, s_next, s_prev = tab_ref[0], tab_ref[1], tab_ref[2]

        def head(v, gain):
            n = v * lax.rsqrt(jnp.mean(v * v, axis=-1, keepdims=True) + EPS)
            return _rope_apply(n * gain, cos, s_next, s_prev)

        for h in range(N_Q_HEADS):
            lo = qo + h * HEAD_DIM
            q_ref[:, h * HEAD_DIM:(h + 1) * HEAD_DIM] = (head(pr[:, lo:lo + HEAD_DIM], qg_ref[...]) * scale_q).astype(BF)
        for h in range(N_KV_HEADS):
            lo = ko + h * HEAD_DIM
            k_ref[:, h * HEAD_DIM:(h + 1) * HEAD_DIM] = head(pr[:, lo:lo + HEAD_DIM], kg_ref[...]).astype(BF)

    return _call(
        body, name=name, grid=(Tr // tm,),
        in_specs=[_row(tm, D), _mod_spec(D, 1, nctx), _resident((1, D)), _resident(w_in.shape),
                  _resident((1, HEAD_DIM)), _resident((1, HEAD_DIM)),
                  pl.BlockSpec((3, tm, HEAD_DIM), lambda i: (0, i, 0))],
        out_specs=[_row(tm, D), _row(tm, P), _row(tm, qw), _row(tm, kw)],
        out_shape=[jax.ShapeDtypeStruct((Tr, D), BF), jax.ShapeDtypeStruct((Tr, P), BF),
                   jax.ShapeDtypeStruct((Tr, qw), BF), jax.ShapeDtypeStruct((Tr, kw), BF)],
        operands=[xa, mod6, g, w_in, qg, kg, tabs], hosted=hosted,
        params=_params(vmem=VMEM_BIG, sem=("arbitrary",)))


def _shifted(u, first_row, last_row):
    T = u.shape[0]
    prev = jnp.where(first_row, 0.0, pltpu.roll(u, 1, 0))
    nxt = jnp.where(last_row, 0.0, pltpu.roll(u, T - 1, 0))
    return prev, nxt


def conv_forward(proj, conv_w, offs, Tc, name):
    Ta = proj.shape[0]
    T = Ta - Tc
    Dc = conv_w.shape[1]
    cb = offs["cv"] // 384

    def body(p_ref, w_ref, y_ref):
        rows = lax.broadcasted_iota(jnp.int32, (T, 128), 0)
        u = p_ref[pl.ds(Tc, T), 128:256].astype(F32) * p_ref[pl.ds(Tc, T), 256:384].astype(F32)
        prev, nxt = _shifted(u, rows == 0, rows == T - 1)
        w = w_ref[...]
        cv = prev * w[0:1] + u * w[1:2] + nxt * w[2:3]
        y_ref[...] = (p_ref[pl.ds(Tc, T), 0:128].astype(F32) * cv).astype(BF)

    return pl.pallas_call(
        body, name=name, grid=(Dc // 128,),
        in_specs=[pl.BlockSpec((Ta, 384), lambda j: (0, cb + j)), pl.BlockSpec((CONV_TAPS, 128), lambda j: (0, j))],
        out_specs=pl.BlockSpec((T, 128), lambda j: (0, j)),
        out_shape=jax.ShapeDtypeStruct((T, Dc), BF),
        compiler_params=_params(vmem=VMEM_BIG, sem=("arbitrary",)),
    )(proj, conv_w)


def conv_backward(dproj, dy, proj, conv_w, offs, Tc, name):
    Ta = proj.shape[0]
    T = Ta - Tc
    Dc = conv_w.shape[1]
    cb = offs["cv"] // 384

    def body(dp_any, dy_ref, p_ref, w_ref, o_ref, dw_ref):
        rows = lax.broadcasted_iota(jnp.int32, (T, 128), 0)
        first, last = rows == 0, rows == T - 1
        bg = p_ref[pl.ds(Tc, T), 0:128].astype(F32)
        cg = p_ref[pl.ds(Tc, T), 128:256].astype(F32)
        vc = p_ref[pl.ds(Tc, T), 256:384].astype(F32)
        dy = dy_ref[...].astype(F32)
        u = cg * vc
        prev, nxt = _shifted(u, first, last)
        w = w_ref[...]
        cv = prev * w[0:1] + u * w[1:2] + nxt * w[2:3]
        o_ref[pl.ds(0, Tc), :] = jnp.zeros((Tc, 384), BF)
        o_ref[pl.ds(Tc, T), 0:128] = (dy * cv).astype(BF)
        dcv = dy * bg
        dprev, dnxt = _shifted(dcv, first, last)
        du = dnxt * w[0:1] + dcv * w[1:2] + dprev * w[2:3]
        o_ref[pl.ds(Tc, T), 128:256] = (du * vc).astype(BF)
        o_ref[pl.ds(Tc, T), 256:384] = (du * cg).astype(BF)
        dw_ref[...] = jnp.zeros_like(dw_ref)
        for k, tap in enumerate((prev, u, nxt)):
            dw_ref[k:k + 1, :] = jnp.sum(dcv * tap, axis=0, keepdims=True)

    blk = pl.BlockSpec((Ta, 384), lambda j: (0, cb + j))
    return pl.pallas_call(
        body, name=name, grid=(Dc // 128,),
        in_specs=[ANY, pl.BlockSpec((T, 128), lambda j: (0, j)), blk, pl.BlockSpec((CONV_TAPS, 128), lambda j: (0, j))],
        out_specs=[blk, pl.BlockSpec((8, 128), lambda j: (0, j))],
        out_shape=[jax.ShapeDtypeStruct(dproj.shape, BF), jax.ShapeDtypeStruct((8, Dc), F32)],
        input_output_aliases={0: 0},
        compiler_params=_params(vmem=VMEM_BIG, sem=("arbitrary",)),
    )(dproj, dy, proj, conv_w)


def _kv_chunk(Ta):
    return _pick(Ta, (768, 512, 384, 256, 128))


def _stack_heads(v):
    return jnp.concatenate([v[:, h * HEAD_DIM:(h + 1) * HEAD_DIM] for h in range(GROUP)], axis=0)


def attention_forward(q, k, proj, offs, Tc, name, hosted=None):
    Ta = k.shape[0]
    T = Ta - Tc
    tq = ROW_TILE
    kc = _kv_chunk(Ta)
    nkv = Ta // kc
    gw = GROUP * HEAD_DIM
    vblk = offs["v"] // HEAD_DIM
    qoff = Tc // tq
    n = GROUP * tq

    def body(q_ref, k_ref, v_ref, o_ref, lse_ref, vx_ref, qs_ref, s0_ref, s1_ref, m_ref, acc_ref):
        @pl.when(pl.program_id(1) == 0)
        def _():
            vx_ref[:, 0:HEAD_DIM] = v_ref[...]
            vx_ref[:, HEAD_DIM:2 * HEAD_DIM] = jnp.ones((Ta, HEAD_DIM), BF)

        qs_ref[...] = _stack_heads(q_ref[...])
        m_ref[...] = jnp.full((n, 1), -1e30, F32)
        acc_ref[...] = jnp.zeros((n, 2 * HEAD_DIM), F32)

        def rows(c):
            return pl.ds(pl.multiple_of(c * kc, kc), kc)

        def logits(c, dst):
            dst[...] = lax.dot_general(qs_ref[...], k_ref[rows(c), :], NT, preferred_element_type=F32)

        def consume(src, c):
            s = src[...]
            m_prev = m_ref[...]
            m_new = jnp.maximum(m_prev, jnp.max(s, axis=-1, keepdims=True))
            p = jnp.exp2(s - m_new).astype(BF)
            acc_ref[...] = jnp.exp2(m_prev - m_new) * acc_ref[...] + jnp.dot(p, vx_ref[rows(c), :], preferred_element_type=F32)
            m_ref[...] = m_new

        def pair(i, carry):
            logits(2 * i + 1, s1_ref)
            consume(s0_ref, 2 * i)
            logits(2 * i + 2, s0_ref)
            consume(s1_ref, 2 * i + 1)
            return carry

        logits(0, s0_ref)
        if nkv % 2:
            lax.fori_loop(0, nkv // 2, pair, 0)
            consume(s0_ref, nkv - 1)
        else:
            lax.fori_loop(0, nkv // 2 - 1, pair, 0)
            logits(nkv - 1, s1_ref)
            consume(s0_ref, nkv - 2)
            consume(s1_ref, nkv - 1)
        acc = acc_ref[...]
        l = acc[:, HEAD_DIM:HEAD_DIM + 1]
        o = acc[:, 0:HEAD_DIM] / l
        lse = m_ref[...] + jnp.log2(l)
        for h in range(GROUP):
            o_ref[:, h * HEAD_DIM:(h + 1) * HEAD_DIM] = o[h * tq:(h + 1) * tq].astype(BF)
            lse_ref[0, :, h:h + 1] = lse[h * tq:(h + 1) * tq]

    return _call(
        body, name=name, grid=(N_KV_HEADS, T // tq),
        in_specs=[pl.BlockSpec((tq, gw), lambda j, i: (i + qoff, j)),
                  pl.BlockSpec((Ta, HEAD_DIM), lambda j, i: (0, j)),
                  pl.BlockSpec((Ta, HEAD_DIM), lambda j, i: (0, vblk + j))],
        out_specs=[pl.BlockSpec((tq, gw), lambda j, i: (i, j)),
                   pl.BlockSpec((1, tq, GROUP), lambda j, i: (j, i, 0))],
        out_shape=[jax.ShapeDtypeStruct((T, N_Q_HEADS * HEAD_DIM), BF),
                   jax.ShapeDtypeStruct((N_KV_HEADS, T, GROUP), F32)],
        scratch_shapes=[pltpu.VMEM((Ta, 2 * HEAD_DIM), BF), pltpu.VMEM((n, HEAD_DIM), BF), pltpu.VMEM((n, kc), F32),
                        pltpu.VMEM((n, kc), F32), pltpu.VMEM((n, 1), F32), pltpu.VMEM((n, 2 * HEAD_DIM), F32)],
        operands=[q, k, proj], hosted=hosted,
        params=_params(vmem=VMEM_BIG, sem=("arbitrary", "arbitrary")))


def _norm_rope_backward(dz, raw, gg, cos, s_next, s_prev):
    r = lax.rsqrt(jnp.mean(raw * raw, axis=-1, keepdims=True) + EPS)
    n = raw * r
    dy = _rope_transpose(dz, cos, s_next, s_prev)
    dn = dy * gg
    return r * (dn - n * jnp.mean(dn * n, axis=-1, keepdims=True)), jnp.sum(dy * n, axis=0, keepdims=True)


def attention_backward(dproj, q, k, proj, o, lse, do, qgain, tabs, offs, Tc, name, hosted=None):
    Ta = k.shape[0]
    tq = ROW_TILE
    nctx = Tc // tq
    kc = _kv_chunk(Ta)
    gw = GROUP * HEAD_DIM
    vblk = offs["v"] // HEAD_DIM
    qblk = offs["q"] // gw
    zscale = HEAD_DIM ** -0.5

    def body(dp_any, q_ref, k_ref, v_ref, o_ref, lse_ref, do_ref, raw_ref, g_ref, tab_ref, dqr_ref, dk_ref, dv_ref, dg_ref):
        j, i = pl.program_id(0), pl.program_id(1)

        @pl.when(i == 0)
        def _():
            dk_ref[...] = jnp.zeros_like(dk_ref)
            dv_ref[...] = jnp.zeros_like(dv_ref)

        @pl.when((i == 0) & (j == 0))
        def _():
            dg_ref[...] = jnp.zeros_like(dg_ref)

        @pl.when(i < nctx)
        def _():
            dqr_ref[...] = jnp.zeros_like(dqr_ref)

        @pl.when(i >= nctx)
        def _():
            qs = _stack_heads(q_ref[...])
            dob = do_ref[...]
            dos = _stack_heads(dob)
            delta = jnp.concatenate(
                [jnp.sum(dob[:, h * HEAD_DIM:(h + 1) * HEAD_DIM].astype(F32)
                         * o_ref[:, h * HEAD_DIM:(h + 1) * HEAD_DIM].astype(F32), axis=-1, keepdims=True)
                 for h in range(GROUP)], axis=0)
            lse = jnp.concatenate([lse_ref[0, :, h:h + 1] for h in range(GROUP)], axis=0)

            def step(c, dq):
                rows = pl.ds(pl.multiple_of(c * kc, kc), kc)
                kk = k_ref[rows, :]
                vv = v_ref[rows, :]
                s = lax.dot_general(qs, kk, NT, preferred_element_type=F32)
                p = jnp.exp2(s - lse)
                dp = lax.dot_general(dos, vv, NT, preferred_element_type=F32)
                ds = (p * (dp - delta)).astype(BF)
                dv_ref[rows, :] += lax.dot_general(p.astype(BF), dos, TN, preferred_element_type=F32)
                dk_ref[rows, :] += lax.dot_general(ds, qs, TN, preferred_element_type=F32)
                return dq + jnp.dot(ds, kk, preferred_element_type=F32)

            dq = lax.fori_loop(0, Ta // kc, step, jnp.zeros((GROUP * tq, HEAD_DIM), F32))
            per_head = lambda t: jnp.concatenate([t] * GROUP, axis=0)
            dr, dg = _norm_rope_backward(dq * zscale, _stack_heads(raw_ref[...]).astype(F32), g_ref[...],
                                         per_head(tab_ref[0]), per_head(tab_ref[1]), per_head(tab_ref[2]))
            for h in range(GROUP):
                dqr_ref[:, h * HEAD_DIM:(h + 1) * HEAD_DIM] = dr[h * tq:(h + 1) * tq].astype(BF)
            dg_ref[0:1, :] += dg

    lat = lambda j, i: (jnp.maximum(i - nctx, 0), j)
    (dproj, dk, dv, dqg), exchanged = _call(
        body, name=name, grid=(N_KV_HEADS, Ta // tq),
        in_specs=[ANY, pl.BlockSpec((tq, gw), lambda j, i: (i, j)),
                  pl.BlockSpec((Ta, HEAD_DIM), lambda j, i: (0, j)),
                  pl.BlockSpec((Ta, HEAD_DIM), lambda j, i: (0, vblk + j)),
                  pl.BlockSpec((tq, gw), lat),
                  pl.BlockSpec((1, tq, GROUP), lambda j, i: (j, jnp.maximum(i - nctx, 0), 0)),
                  pl.BlockSpec((tq, gw), lat),
                  pl.BlockSpec((tq, gw), lambda j, i: (i, qblk + j)),
                  pl.BlockSpec((1, HEAD_DIM), lambda j, i: (0, 0)),
                  pl.BlockSpec((3, tq, HEAD_DIM), lambda j, i: (0, i, 0))],
        out_specs=[pl.BlockSpec((tq, gw), lambda j, i: (i, qblk + j)),
                   pl.BlockSpec((Ta, HEAD_DIM), lambda j, i: (0, j)),
                   pl.BlockSpec((Ta, HEAD_DIM), lambda j, i: (0, j)),
                   pl.BlockSpec((8, HEAD_DIM), lambda j, i: (0, 0))],
        out_shape=[jax.ShapeDtypeStruct(dproj.shape, BF),
                   jax.ShapeDtypeStruct((Ta, N_KV_HEADS * HEAD_DIM), F32),
                   jax.ShapeDtypeStruct((Ta, N_KV_HEADS * HEAD_DIM), F32),
                   jax.ShapeDtypeStruct((8, HEAD_DIM), F32)],
        operands=[dproj, q, k, proj, o, lse, do, proj, qgain, tabs], hosted=hosted, aliases={0: 0},
        params=_params(vmem=VMEM_BIG, sem=("arbitrary", "arbitrary")))
    return (dproj, dk, dv, dqg), exchanged


def kv_backward(dproj, dk, dv, proj, gain, tabs, offs, name):
    Ta = proj.shape[0]
    tm = _pick(Ta, (768, 512, ROW_TILE))
    kw = N_KV_HEADS * HEAD_DIM
    cb = offs["k"] // (2 * kw)
    kb = offs["k"] // kw
    zscale = 1.0 / LOG2E

    def body(dp_any, dk_ref, dv_ref, raw_ref, g_ref, tab_ref, o_ref, dg_ref):
        @pl.when(pl.program_id(0) == 0)
        def _():
            dg_ref[...] = jnp.zeros_like(dg_ref)

        cos, s_next, s_prev = tab_ref[0], tab_ref[1], tab_ref[2]
        dg = jnp.zeros((1, HEAD_DIM), F32)
        for h in range(N_KV_HEADS):
            sl = slice(h * HEAD_DIM, (h + 1) * HEAD_DIM)
            dr, dgh = _norm_rope_backward(dk_ref[:, sl] * zscale, raw_ref[:, sl].astype(F32), g_ref[...], cos, s_next, s_prev)
            o_ref[:, sl] = dr.astype(BF)
            dg = dg + dgh
        o_ref[:, kw:2 * kw] = dv_ref[...].astype(BF)
        dg_ref[0:1, :] += dg

    return pl.pallas_call(
        body, name=name, grid=(Ta // tm,),
        in_specs=[ANY, _row(tm, kw), _row(tm, kw), pl.BlockSpec((tm, kw), lambda i: (i, kb)),
                  _resident((1, HEAD_DIM)), pl.BlockSpec((3, tm, HEAD_DIM), lambda i: (0, i, 0))],
        out_specs=[pl.BlockSpec((tm, 2 * kw), lambda i: (i, cb)), pl.BlockSpec((8, HEAD_DIM), lambda i: (0, 0))],
        out_shape=[jax.ShapeDtypeStruct(dproj.shape, BF), jax.ShapeDtypeStruct((8, HEAD_DIM), F32)],
        input_output_aliases={0: 0},
        compiler_params=_params(sem=("arbitrary",)),
    )(dproj, dk, dv, proj, gain, tabs)


def merge_forward(x1, mod6, yc, o, proj, w_bc, w_ba, w_o, offs, Tc, name):
    T, D = yc.shape[0], x1.shape[1]
    tm = ROW_TILE
    roff = Tc // tm
    gb = offs["gt"] // (2 * D)

    def body(x_ref, mod_ref, yc_ref, o_ref, gt_ref, wbc_ref, wba_ref, wo_ref, xo_ref, pc_ref, pa_ref, m_ref, z_ref):
        gate = mod_ref[0][2:3]
        pc = jnp.dot(yc_ref[...], wbc_ref[...], preferred_element_type=F32)
        pa = jnp.dot(o_ref[...], wba_ref[...], preferred_element_type=F32)
        pc_ref[...] = pc.astype(BF)
        pa_ref[...] = pa.astype(BF)
        mb = (_sigmoid(gt_ref[:, 0:D].astype(F32)) * pc + _sigmoid(gt_ref[:, D:2 * D].astype(F32)) * pa).astype(BF)
        m_ref[...] = mb
        z = jnp.dot(mb, wo_ref[...], preferred_element_type=F32)
        z_ref[...] = z.astype(BF)
        xo_ref[...] = x_ref[...] + gate * z

    return pl.pallas_call(
        body, name=name, grid=(T // tm,),
        in_specs=[pl.BlockSpec((tm, D), lambda i: (i + roff, 0)), _mod_spec(D, 1, 0), _row(tm, yc.shape[1]), _row(tm, o.shape[1]),
                  pl.BlockSpec((tm, 2 * D), lambda i: (i + roff, gb)),
                  _resident(w_bc.shape), _resident(w_ba.shape), _resident(w_o.shape)],
        out_specs=[_row(tm, D)] * 5,
        out_shape=[jax.ShapeDtypeStruct((T, D), F32)] + [jax.ShapeDtypeStruct((T, D), BF)] * 4,
        compiler_params=_params(vmem=VMEM_BIG, sem=("arbitrary",)),
    )(x1, mod6, yc, o, proj, w_bc, w_ba, w_o)


def merge_backward_rows(dx2, mod6, z, pc, pa, proj, w_bc, w_ba, w_o, offs, Tc, name):
    T, D = dx2.shape
    Ta, P = proj.shape
    tm = ROW_TILE
    nctx = Tc // tm
    gb = offs["gt"] // (2 * D)
    dcw, dqw = w_bc.shape[0], w_ba.shape[0]

    def body(dx_ref, mod_ref, z_ref, pc_ref, pa_ref, gt_ref, wbc_ref, wba_ref, wo_ref,
             dgt_ref, dg_ref, dpc_ref, dpa_ref, dyc_ref, do_ref, acc_ref):
        i = pl.program_id(0)

        @pl.when(i == 0)
        def _():
            acc_ref[...] = jnp.zeros_like(acc_ref)

        @pl.when(i < nctx)
        def _():
            dgt_ref[...] = jnp.zeros_like(dgt_ref)

        @pl.when(i >= nctx)
        def _():
            gate = mod_ref[0][2:3]
            dx = dx_ref[...]
            acc_ref[0:1, :] += jnp.sum(dx * z_ref[...].astype(F32), axis=0, keepdims=True)
            dgb = (dx * gate).astype(BF)
            dg_ref[...] = dgb
            dm = lax.dot_general(dgb, wo_ref[...], NT, preferred_element_type=F32)
            sc = _sigmoid(gt_ref[:, 0:D].astype(F32))
            sa = _sigmoid(gt_ref[:, D:2 * D].astype(F32))
            pc = pc_ref[...].astype(F32)
            pa = pa_ref[...].astype(F32)
            dpc = (dm * sc).astype(BF)
            dpa = (dm * sa).astype(BF)
            dpc_ref[...] = dpc
            dpa_ref[...] = dpa
            dgt_ref[:, 0:D] = ((dm * pc) * (sc * (1.0 - sc))).astype(BF)
            dgt_ref[:, D:2 * D] = ((dm * pa) * (sa * (1.0 - sa))).astype(BF)
            dyc_ref[...] = lax.dot_general(dpc, wbc_ref[...], NT, preferred_element_type=F32).astype(BF)
            do_ref[...] = lax.dot_general(dpa, wba_ref[...], NT, preferred_element_type=F32).astype(BF)

    lat = lambda n: pl.BlockSpec((tm, n), lambda i: (jnp.maximum(i - nctx, 0), 0))
    return pl.pallas_call(
        body, name=name, grid=(Ta // tm,),
        in_specs=[lat(D), _mod_spec(D, 1, 0), lat(D), lat(D), lat(D),
                  pl.BlockSpec((tm, 2 * D), lambda i: (i, gb)),
                  _resident(w_bc.shape), _resident(w_ba.shape), _resident(w_o.shape)],
        out_specs=[pl.BlockSpec((tm, 2 * D), lambda i: (i, gb)), lat(D), lat(D), lat(D), lat(dcw), lat(dqw),
                   pl.BlockSpec((8, D), lambda i: (0, 0))],
        out_shape=[jax.ShapeDtypeStruct((Ta, P), BF)] + [jax.ShapeDtypeStruct((T, D), BF)] * 3
                  + [jax.ShapeDtypeStruct((T, dcw), BF), jax.ShapeDtypeStruct((T, dqw), BF), jax.ShapeDtypeStruct((8, D), F32)],
        compiler_params=_params(vmem=VMEM_BIG, sem=("arbitrary",)),
    )(dx2, mod6, z, pc, pa, proj, w_bc, w_ba, w_o)


def proj_backward_rows(dproj, dres, xa, mod6, g, w_in, nctx, name, hosted=None):
    Tr, D = xa.shape
    P = w_in.shape[1]
    tm = ROW_TILE

    def body(dp_ref, dres_ref, x_ref, mod_ref, g_ref, w_ref, dx_ref, acc_ref):
        i = pl.program_id(0)

        @pl.when((i == 0) | (i == nctx))
        def _():
            acc_ref[...] = jnp.zeros_like(acc_ref)

        x = x_ref[...]
        scale = mod_ref[0][1:2]
        gg = g_ref[...]
        dhm = lax.dot_general(dp_ref[...], w_ref[...], NT, preferred_element_type=F32)
        r = lax.rsqrt(jnp.mean(x * x, axis=-1, keepdims=True) + EPS)
        xh = x * r
        dshift = jnp.sum(dhm, axis=0, keepdims=True)
        dscale = jnp.sum(dhm * (xh * gg), axis=0, keepdims=True)
        dxh_g = dhm * (1.0 + scale)
        dg = jnp.sum(dxh_g * xh, axis=0, keepdims=True)
        dxh = dxh_g * gg
        res = jnp.where(i < nctx, 0.0, dres_ref[...])
        dx_ref[...] = res + r * (dxh - xh * jnp.mean(dxh * xh, axis=-1, keepdims=True))
        for k, val in enumerate((dshift, dscale, dg)):
            acc_ref[0, k:k + 1, :] += val

    return _call(
        body, name=name, grid=(Tr // tm,),
        in_specs=[_row(tm, P), pl.BlockSpec((tm, D), lambda i: (jnp.maximum(i - nctx, 0), 0)), _row(tm, D),
                  _mod_spec(D, 1, nctx), _resident((1, D)), _resident(w_in.shape)],
        out_specs=[_row(tm, D), _acc_spec(D, nctx)],
        out_shape=[jax.ShapeDtypeStruct((Tr, D), F32), jax.ShapeDtypeStruct((2, 8, D), F32)],
        operands=[dproj, dres, xa, mod6, g, w_in], hosted=hosted,
        params=_params(vmem=VMEM_BIG, sem=("arbitrary",)))


def _adam_update(w, g, m, v):
    c1 = 1.0 - ADAM_B1 ** ADAM_STEP
    c2 = 1.0 - ADAM_B2 ** ADAM_STEP
    m = ADAM_B1 * m + (1.0 - ADAM_B1) * g
    v = ADAM_B2 * v + (1.0 - ADAM_B2) * (g * g)
    return -ADAM_LR * ((m / c1) / (jnp.sqrt(v / c2) + ADAM_EPS) + ADAM_WD * w), m, v


def adamw(w, g, m, v, name, dep=None):
    R, C = w.shape
    tr = _row_tile(R, C)
    deps = [] if dep is None else [dep]

    def body(w_ref, g_ref, m_ref, v_ref, *rest):
        d_ref, nm_ref, nv_ref = rest[len(deps):]
        d_ref[...], nm_ref[...], nv_ref[...] = _adam_update(w_ref[...], g_ref[...], m_ref[...], v_ref[...])

    blk = pl.BlockSpec((tr, C), lambda i: (i, 0))
    return pl.pallas_call(
        body, name=name, grid=(R // tr,),
        in_specs=[blk] * 4 + [ANY] * len(deps), out_specs=[blk] * 3,
        out_shape=[jax.ShapeDtypeStruct((R, C), F32)] * 3,
        compiler_params=_params(vmem=VMEM_BIG, sem=("parallel",)),
    )(w, g, m, v, *deps)


def adamw_summed(recv, w, m, v, name, dep=None):
    R, C = w.shape
    tr = _row_tile(R, C)
    deps = [] if dep is None else [dep]

    def body(r_ref, w_ref, m_ref, v_ref, *rest):
        g_ref, d_ref, nm_ref, nv_ref = rest[len(deps):]
        g = r_ref[0].astype(F32)
        for a in range(1, N_DEV):
            g = g + r_ref[a].astype(F32)
        g_ref[...] = g
        d_ref[...], nm_ref[...], nv_ref[...] = _adam_update(w_ref[...], g, m_ref[...], v_ref[...])

    blk = pl.BlockSpec((tr, C), lambda i: (i, 0))
    return pl.pallas_call(
        body, name=name, grid=(R // tr,),
        in_specs=[pl.BlockSpec((N_DEV, tr, C), lambda i: (0, i, 0)), blk, blk, blk] + [ANY] * len(deps), out_specs=[blk] * 4,
        out_shape=[jax.ShapeDtypeStruct((R, C), F32)] * 4,
        compiler_params=_params(vmem=VMEM_BIG, sem=("parallel",)),
    )(recv, w, m, v, *deps)


def _rope_tables(T, Tc):
    rows = T // GRID_W
    n_freq = HEAD_DIM // 4
    inv = ROPE_THETA ** (-jnp.arange(n_freq, dtype=F32) / n_freq)
    ang_r = jnp.arange(rows).astype(F32)[:, None] * inv
    ang_c = jnp.arange(GRID_W).astype(F32)[:, None] * inv
    per_row = lambda a: jnp.broadcast_to(a[:, None, :], (rows, GRID_W, n_freq)).reshape(T, n_freq)
    per_col = lambda a: jnp.broadcast_to(a[None, :, :], (rows, GRID_W, n_freq)).reshape(T, n_freq)
    cr, sr = per_row(jnp.cos(ang_r)), per_row(jnp.sin(ang_r))
    cc, sc = per_col(jnp.cos(ang_c)), per_col(jnp.sin(ang_c))
    zero = jnp.zeros_like(sr)
    cos = jnp.concatenate([cr, cr, cc, cc], axis=1)
    s_next = jnp.concatenate([-sr, zero, -sc, zero], axis=1)
    s_prev = jnp.concatenate([zero, sr, zero, sc], axis=1)
    lat = jnp.stack([cos, s_next, s_prev])
    ctx = jnp.stack([jnp.ones((Tc, HEAD_DIM), F32), jnp.zeros((Tc, HEAD_DIM), F32), jnp.zeros((Tc, HEAD_DIM), F32)])
    return jnp.concatenate([ctx, lat], axis=1)


BIG = ("ffn1_w_in", "ffn1_w_out", "w_in", "w_branch_conv", "w_branch_attn", "w_out", "ffn2_w_in", "ffn2_w_out")


def _regroup_w_in(stacked, D, Dc, qw, kw):
    w = stacked.transpose(1, 0, 2).reshape(D, -1)
    o = 0
    parts = {}
    for nme, wd in (("bg", Dc), ("cg", Dc), ("vc", Dc), ("q", qw), ("k", kw), ("v", kw), ("gt", 2 * D)):
        parts[nme] = w[:, o:o + wd]
        o += wd
    nb = Dc // 128
    cv = jnp.stack([parts[n].reshape(D, nb, 128) for n in ("bg", "cg", "vc")], axis=2).reshape(D, 3 * Dc)
    return jnp.concatenate([cv, parts["q"], parts["gt"], parts["k"], parts["v"]], axis=1)


def _ungroup_w_in_grad(gt_, D, Dc, qw, kw):
    nb = Dc // 128
    cv = gt_[:3 * Dc].reshape(nb, 3, 128, D)
    o = 3 * Dc
    q = gt_[o:o + qw]
    gt = gt_[o + qw:o + qw + 2 * D]
    k = gt_[o + qw + 2 * D:o + qw + 2 * D + kw]
    v = gt_[o + qw + 2 * D + kw:]
    nat = jnp.concatenate([cv[:, 0].reshape(Dc, D), cv[:, 1].reshape(Dc, D), cv[:, 2].reshape(Dc, D), q, k, v, gt], axis=0)
    return nat.reshape(N_DEV, -1, D)


def kernel(x, c, ctx, c_ctx, w_mod, b_mod, norm1_g, norm2_g, norm3_g, ffn1_w_in, ffn1_w_out, w_in, conv_w, q_norm_g, k_norm_g, w_branch_conv, w_branch_attn, w_out, ffn2_w_in, ffn2_w_out, final_g, loss_target, m_c_ctx, m_w_mod, m_b_mod, m_norm1_g, m_norm2_g, m_norm3_g, m_ffn1_w_in, m_ffn1_w_out, m_w_in, m_conv_w, m_q_norm_g, m_k_norm_g, m_w_branch_conv, m_w_branch_attn, m_w_out, m_ffn2_w_in, m_ffn2_w_out, m_final_g, v_c_ctx, v_w_mod, v_b_mod, v_norm1_g, v_norm2_g, v_norm3_g, v_ffn1_w_in, v_ffn1_w_out, v_w_in, v_conv_w, v_q_norm_g, v_k_norm_g, v_w_branch_conv, v_w_branch_attn, v_w_out, v_ffn2_w_in, v_ffn2_w_out, v_final_g):
    weights = dict(c_ctx=c_ctx, w_mod=w_mod, b_mod=b_mod, norm1_g=norm1_g, norm2_g=norm2_g, norm3_g=norm3_g,
                   ffn1_w_in=ffn1_w_in, ffn1_w_out=ffn1_w_out, w_in=w_in, conv_w=conv_w, q_norm_g=q_norm_g,
                   k_norm_g=k_norm_g, w_branch_conv=w_branch_conv, w_branch_attn=w_branch_attn, w_out=w_out,
                   ffn2_w_in=ffn2_w_in, ffn2_w_out=ffn2_w_out, final_g=final_g)
    moms = dict(c_ctx=(m_c_ctx, v_c_ctx), w_mod=(m_w_mod, v_w_mod), b_mod=(m_b_mod, v_b_mod),
                norm1_g=(m_norm1_g, v_norm1_g), norm2_g=(m_norm2_g, v_norm2_g), norm3_g=(m_norm3_g, v_norm3_g),
                ffn1_w_in=(m_ffn1_w_in, v_ffn1_w_in), ffn1_w_out=(m_ffn1_w_out, v_ffn1_w_out), w_in=(m_w_in, v_w_in),
                conv_w=(m_conv_w, v_conv_w), q_norm_g=(m_q_norm_g, v_q_norm_g), k_norm_g=(m_k_norm_g, v_k_norm_g),
                w_branch_conv=(m_w_branch_conv, v_w_branch_conv), w_branch_attn=(m_w_branch_attn, v_w_branch_attn),
                w_out=(m_w_out, v_w_out), ffn2_w_in=(m_ffn2_w_in, v_ffn2_w_in), ffn2_w_out=(m_ffn2_w_out, v_ffn2_w_out),
                final_g=(m_final_g, v_final_g))
    order = list(weights)

    T, D = x.shape[1], x.shape[2]
    Tc = ctx.shape[1]
    nctx = Tc // ROW_TILE
    nd = N_MOD * D
    Dc = conv_w.shape[2] * N_DEV
    qw, kw = N_Q_HEADS * HEAD_DIM, N_KV_HEADS * HEAD_DIM
    offs, o = {}, 0
    for nme, wd in (("cv", 3 * Dc), ("q", qw), ("gt", 2 * D), ("k", kw), ("v", kw)):
        offs[nme] = o
        o += wd

    ax, ay, ac = lax.axis_index("x"), lax.axis_index("y"), lax.axis_index("c")
    me = 4 * ax + 2 * ay + ac

    shard = {n: (jnp.swapaxes(weights[n][0], 0, 1) if n in ("ffn1_w_in", "ffn2_w_in") else weights[n][0]).astype(BF)
             for n in BIG}
    rows2d = lambda a: a.reshape(-1, a.shape[-1])
    full = {}
    mod_cols = w_mod.shape[2]
    cw_loc = conv_w[0]
    cpad = (-(D + CONV_TAPS * cw_loc.shape[1])) % 128
    pay = jnp.concatenate([c.reshape(1, D), cw_loc.reshape(1, -1), jnp.zeros((1, cpad), F32)], axis=1)
    (g_ffn1_in, g_ffn1_out), (call,) = allgather_two_level([shard["ffn1_w_in"], shard["ffn1_w_out"]], "ag_ffn1",
                                                          riders=[pay])
    full["ffn1_w_in"], full["ffn1_w_out"] = rows2d(g_ffn1_in), rows2d(g_ffn1_out)

    conv_full = call[:, 0, D:D + CONV_TAPS * cw_loc.shape[1]].reshape(N_DEV, CONV_TAPS, -1).transpose(1, 0, 2).reshape(CONV_TAPS, Dc)
    b_loc = lax.dynamic_slice_in_dim(b_mod, me * mod_cols, mod_cols, axis=1)
    cctx2 = c_ctx.reshape(1, D)
    mod_part = mod_forward(call, cctx2, w_mod[0], b_loc, "mod_fwd")
    mod_all = allgather_direct(mod_part, "ag_mod")
    mod_lat = lax.dynamic_index_in_dim(mod_all, me, axis=1, keepdims=False).reshape(nd)
    mod_ctx = mod_all[:, N_DEV, :].reshape(nd)
    mod6 = jnp.stack([mod_ctx, mod_lat]).reshape(6, 3, D)

    tabs = _rope_tables(T, Tc)

    srcs1 = (ctx[0], x[0])
    (xa1, hm1, ab1, h1, f1), (g_w_in,) = ffn_forward(
        srcs1, mod6, 0, norm1_g, full["ffn1_w_in"], full["ffn1_w_out"], nctx, "ffn1_fwd",
        hosted=Hosted(gathers=[shard["w_in"]]))
    full["w_in"] = _regroup_w_in(g_w_in, D, Dc, qw, kw)
    merge_names = ("w_branch_conv", "w_branch_attn", "w_out")
    (hx, proj, qr, kr), g_merge = proj_forward(
        xa1, mod6, norm2_g, full["w_in"], q_norm_g, k_norm_g, tabs, offs, nctx, "proj_fwd",
        hosted=Hosted(gathers=[shard[n] for n in merge_names]))
    full.update({n: rows2d(g) for n, g in zip(merge_names, g_merge)})
    yc = conv_forward(proj, conv_full, offs, Tc, "conv_fwd")
    (oa, lse), (g_ffn2_in, g_ffn2_out) = attention_forward(
        qr, kr, proj, offs, Tc, "attn_fwd", hosted=Hosted(gathers=[shard["ffn2_w_in"], shard["ffn2_w_out"]]))
    full["ffn2_w_in"], full["ffn2_w_out"] = rows2d(g_ffn2_in), rows2d(g_ffn2_out)
    x2, pc, pa, mm, zz = merge_forward(xa1, mod6, yc, oa, proj, full["w_branch_conv"], full["w_branch_attn"],
                                       full["w_out"], offs, Tc, "merge_fwd")
    (dx3, hm2, ab2, h2, f2, lacc), _ = ffn_forward((x2,), mod6, 2, norm3_g, full["ffn2_w_in"], full["ffn2_w_out"], 0, "ffn2_fwd",
                                                   final=(loss_target[0], final_g.reshape(1, D)))
    loss = lax.psum(lacc[1, 0], ("x", "y", "c"))

    by_dest = lambda g: g.reshape((N_DEV, -1, g.shape[-1]))
    (dx2, dab2, df2, acc_f2), _ = ffn_backward_rows(dx3, (x2,), mod6, 2, norm3_g, ab2, f2, full["ffn2_w_in"], full["ffn2_w_out"], 0, "ffn2_bwd")
    early = {"ffn2_w_out": tn_matmul(h2, df2, "ffn2_dwout")[0], "ffn2_w_in": tn_matmul(dab2, hm2, "ffn2_dwin")[0]}
    dproj, dgm, dpc, dpa, dyc, do, acc_mg = merge_backward_rows(dx2, mod6, zz, pc, pa, proj, full["w_branch_conv"],
                                                                full["w_branch_attn"], full["w_out"], offs, Tc, "merge_bwd")
    early["w_out"] = tn_matmul(mm, dgm, "dw_out")[0]
    early["w_branch_conv"] = tn_matmul(yc, dpc, "dw_bc")[0]
    early["w_branch_attn"] = tn_matmul(oa, dpa, "dw_ba")[0]
    dproj, dcw = conv_backward(dproj, dyc, proj, conv_full, offs, Tc, "conv_bwd")
    (dproj, dk, dv, dqg), summed = attention_backward(dproj, qr, kr, proj, oa, lse, do, q_norm_g, tabs, offs, Tc, "attn_bwd",
                                                      hosted=Hosted(scatters=[by_dest(g) for g in early.values()]))
    summed = dict(zip(early, summed))
    dproj, dkg = kv_backward(dproj, dk, dv, proj, k_norm_g, tabs, offs, "kv_bwd")
    g_w_in_grad = _ungroup_w_in_grad(tn_matmul(dproj, hx, "dw_in")[0], D, Dc, qw, kw)
    (dxa1, acc_pj), (summed["w_in"],) = proj_backward_rows(dproj, dx2, xa1, mod6, norm2_g, full["w_in"], nctx, "proj_bwd",
                                                          hosted=Hosted(scatters=[g_w_in_grad]))
    (grad_x2d, dab1, df1, acc_f1), _ = ffn_backward_rows(
        dxa1, srcs1, mod6, 0, norm1_g, ab1, f1, full["ffn1_w_in"], full["ffn1_w_out"], nctx, "ffn1_bwd")
    g_ffn1_w_in, (summed["ffn1_w_out"],) = tn_matmul(
        dab1, hm1, "ffn1_dwin", hosted=Hosted(scatters=[by_dest(tn_matmul(h1, df1, "ffn1_dwout")[0])]))
    grad_x = grad_x2d[None]

    zero_d = jnp.zeros((D,), F32)
    dlat = jnp.concatenate([acc_f1[1, 0], acc_f1[1, 1], acc_f1[1, 2], acc_pj[1, 0], acc_pj[1, 1], acc_mg[0],
                            acc_f2[1, 0], acc_f2[1, 1], acc_f2[1, 2]])
    dctx = jnp.concatenate([acc_f1[0, 0], acc_f1[0, 1], acc_f1[0, 2], acc_pj[0, 0], acc_pj[0, 1]] + [zero_d] * 4)
    small = jnp.concatenate([acc_f1[0, 3] + acc_f1[1, 3], acc_pj[0, 2] + acc_pj[1, 2], acc_f2[1, 3],
                             dqg[0], dkg[0], lacc[0], dcw[0:CONV_TAPS].reshape(-1)])
    n_small = small.shape[0]
    pay_b = jnp.concatenate([dlat, dctx, small]).reshape(1, -1)
    gath = allgather_direct(pay_b, "ag_small_grads")
    dlat_loc = lax.dynamic_slice_in_dim(gath[:, 0, :nd], me * mod_cols, mod_cols, axis=1)
    dctx_loc = lax.dynamic_slice_in_dim(gath[:, 0, nd:2 * nd], me * mod_cols, mod_cols, axis=1)
    g_wmod, pc_part, small_sum = mod_backward(call, cctx2, w_mod[0], dlat_loc, dctx_loc, gath, 2 * nd, n_small, "mod_bwd")
    pcs = allgather_direct(pc_part, "ag_cctx")
    g_bmod, g_cctx = bmod_and_cctx_grad(gath, pcs, cctx2, nd, "small_bwd")
    sm = small_sum[0]
    g_conv_full = sm[3 * D + 2 * HEAD_DIM + D:].reshape(CONV_TAPS, Dc)
    g_conv = lax.dynamic_slice_in_dim(g_conv_full, me * cw_loc.shape[1], cw_loc.shape[1], axis=1)
    gsmall = dict(
        c_ctx=g_cctx, w_mod=g_wmod, b_mod=g_bmod, norm1_g=sm[0:D][None], norm2_g=sm[D:2 * D][None],
        norm3_g=sm[2 * D:3 * D][None], q_norm_g=sm[3 * D:3 * D + HEAD_DIM][None],
        k_norm_g=sm[3 * D + HEAD_DIM:3 * D + 2 * HEAD_DIM][None],
        final_g=sm[3 * D + 2 * HEAD_DIM:3 * D + 2 * HEAD_DIM + D][None], conv_w=g_conv)

    flipped = ("ffn1_w_in", "w_in", "ffn2_w_in")
    last = "ffn1_w_in"
    results = {}

    def update(n, dep=None):
        w = weights[n]
        shp = w.shape
        if n in flipped:
            two_d = lambda a: jnp.swapaxes(a[0], 0, 1)
            back = lambda a: jnp.swapaxes(a, 0, 1)[None]
        else:
            two_d = lambda a: a.reshape(-1, shp[-1])
            back = lambda a: a.reshape(shp)
        m, v = moms[n]
        if n in summed:
            g2, d, nm, nv = adamw_summed(summed[n], two_d(w), two_d(m), two_d(v), "adamw_" + n, dep=dep)
        else:
            g2 = gsmall[n].reshape(two_d(w).shape)
            d, nm, nv = adamw(two_d(w), g2, two_d(m), two_d(v), "adamw_" + n, dep=dep)
        results[n] = tuple(back(a) for a in (g2, d, nm, nv))
        return d

    last_sems, last_recv_sems, last_src, last_land, token = scatter_start(
        by_dest(g_ffn1_w_in), [g_cctx, loss.reshape(1, 1)], "rs_ffn1_start")
    heavy = ("w_mod", "w_in", "ffn2_w_in", "ffn2_w_out", "ffn1_w_out", "w_out")
    deltas = {n: update(n, dep=token) for n in order if n != last}
    after = jnp.concatenate([deltas[n][0:1, 0:1] for n in heavy], axis=1)
    g_done, land_done = scatter_wait(last_sems, last_recv_sems, last_src, last_land, after, "rs_ffn1_wait")
    own = lax.dynamic_index_in_dim(g_done, me, axis=0, keepdims=True)
    summed[last] = lax.dynamic_update_slice_in_dim(land_done, own, me, axis=0)
    update(last)
    cols = list(zip(*(results[n] for n in order)))
    return (loss, grad_x, *cols[0], *cols[1], *cols[2], *cols[3])
```

```python
import math

import jax
import jax.numpy as jnp
from jax import lax
from jax.experimental import pallas as pl
from jax.experimental.pallas import tpu as pltpu

F32 = jnp.float32
BF = jnp.bfloat16
EPS = 1e-6
N_DEV = 8
HEAD_DIM = 128
N_Q_HEADS = 8
N_KV_HEADS = 2
GROUP = N_Q_HEADS // N_KV_HEADS
GRID_W = 64
ROPE_THETA = 10000.0
CONV_TAPS = 3
N_MOD = 9
ADAM_LR = 0.001
ADAM_B1 = 0.9
ADAM_B2 = 0.999
ADAM_EPS = 1e-08
ADAM_WD = 0.01
ADAM_STEP = 10
ROW_TILE = 256
VMEM_BIG = 56 << 20
MESH_ID = pl.DeviceIdType.MESH
HIGHEST = lax.Precision.HIGHEST
NT = (((1,), (1,)), ((), ()))
TN = (((0,), (0,)), ((), ()))
LOG2E = math.log2(math.e)


def _pick(n, cands):
    for c in cands:
        if n % c == 0:
            return c
    return n


def _params(vmem=None, sem=None):
    kw = {}
    if vmem is not None:
        kw["vmem_limit_bytes"] = vmem
    if sem is not None:
        kw["dimension_semantics"] = sem
    return pltpu.CompilerParams(**kw)


def _resident(shape):
    nd = len(shape)
    return pl.BlockSpec(shape, lambda *_: (0,) * nd, pipeline_mode=pl.Buffered(1))


def _sigmoid(x):
    return jax.nn.sigmoid(x)


ANY = pl.BlockSpec(memory_space=pl.ANY)


def _coords():
    return lax.axis_index("x"), lax.axis_index("y"), lax.axis_index("c")


def _flip(v, bit):
    return 1 - v if bit else v


def _remote(src, dst, ssem, rsem, dev):
    return pltpu.make_async_remote_copy(src_ref=src, dst_ref=dst, send_sem=ssem, recv_sem=rsem,
                                        device_id=dev, device_id_type=MESH_ID)


def allgather_direct(v, name):
    def body(v_ref, out_ref, ssem, rsem, lsem):
        x, y, c = _coords()
        me = 4 * x + 2 * y + c
        mine = pltpu.make_async_copy(v_ref, out_ref.at[me], lsem)
        mine.start()
        cps = []
        for p in range(1, N_DEV):
            px, py, pc = (p >> 2) & 1, (p >> 1) & 1, p & 1
            cps.append(_remote(v_ref, out_ref.at[me], ssem.at[p - 1], rsem.at[p - 1],
                               (_flip(x, px), _flip(y, py), _flip(c, pc))))
        for cp in cps:
            cp.start()
        for p in range(1, N_DEV):
            px, py, pc = (p >> 2) & 1, (p >> 1) & 1, p & 1
            src = 4 * _flip(x, px) + 2 * _flip(y, py) + _flip(c, pc)
            _remote(v_ref, out_ref.at[src], ssem.at[p - 1], rsem.at[p - 1], (x, y, c)).wait_recv()
        for cp in cps:
            cp.wait_send()
        mine.wait()

    return pl.pallas_call(
        body, name=name,
        out_shape=jax.ShapeDtypeStruct((N_DEV,) + v.shape, v.dtype),
        in_specs=[ANY], out_specs=ANY,
        scratch_shapes=[pltpu.SemaphoreType.DMA((N_DEV - 1,)), pltpu.SemaphoreType.DMA((N_DEV - 1,)),
                        pltpu.SemaphoreType.DMA],
    )(v)


def allgather_two_level(shards, name, riders=()):
    n = len(shards)
    ride = Hosted(gathers=riders)
    r = ride.n

    def body(*refs):
        v_refs, rin, out_refs, rout = refs[:n], refs[n:n + r], refs[n + r:2 * n + r], refs[2 * n + r:2 * n + 2 * r]
        (ssem, rsem, lsem), rsems = refs[2 * n + 2 * r:2 * n + 2 * r + 3], refs[2 * n + 2 * r + 3:]
        x, y, c = _coords()
        me = (x, y, c)
        sib = (x, y, 1 - c)
        chips = [(1 - x, y), (x, 1 - y), (1 - x, 1 - y)]
        if r:
            ride.start(rin, rout, *rsems)

        def slot(w, px, py, pc):
            return out_refs[w].at[4 * px + 2 * py + pc]

        def sem(w, k):
            return ssem.at[7 * w + k], rsem.at[7 * w + k]

        mine = [pltpu.make_async_copy(v_refs[w], slot(w, *me), lsem.at[w]) for w in range(n)]
        for cp in mine:
            cp.start()
        first = []
        for w in range(n):
            first.append(_remote(v_refs[w], slot(w, *me), *sem(w, 0), sib))
            first += [_remote(v_refs[w], slot(w, *me), *sem(w, 1 + j), (*chip, c)) for j, chip in enumerate(chips)]
        for cp in first:
            cp.start()
        passed = []
        for w in range(n):
            for j, chip in enumerate(chips):
                _remote(v_refs[w], slot(w, *chip, c), *sem(w, 1 + j), me).wait_recv()
                cp = _remote(slot(w, *chip, c), slot(w, *chip, c), *sem(w, 4 + j), sib)
                cp.start()
                passed.append(cp)
        for w in range(n):
            _remote(v_refs[w], slot(w, x, y, 1 - c), *sem(w, 0), me).wait_recv()
            for j, chip in enumerate(chips):
                _remote(v_refs[w], slot(w, *chip, 1 - c), *sem(w, 4 + j), me).wait_recv()
        for cp in first + passed:
            cp.wait_send()
        for cp in mine:
            cp.wait()
        if r:
            ride.wait(rin, rout, *rsems)

    res = pl.pallas_call(
        body, name=name,
        out_shape=[jax.ShapeDtypeStruct((N_DEV,) + s.shape, s.dtype) for s in shards] + ride.out_shapes,
        in_specs=[ANY] * (n + r), out_specs=[ANY] * (n + r),
        scratch_shapes=[pltpu.SemaphoreType.DMA((7 * n,)), pltpu.SemaphoreType.DMA((7 * n,)),
                        pltpu.SemaphoreType.DMA((n,))] + (ride.scratch if r else []),
    )(*shards, *riders)
    return list(res[:n]), list(res[n:])


SEM = pl.BlockSpec(memory_space=pltpu.SEMAPHORE)
IN_HBM = pl.BlockSpec(memory_space=pltpu.HBM)
DATAFLOW = pltpu.SideEffectType.DATAFLOW_SIDE_EFFECTING


def _scatter_descriptors(src_ref, land_ref, ssem, rsem, arrivals):
    x, y, c = _coords()
    me = 4 * x + 2 * y + c
    cps = []
    for p in range(1, N_DEV):
        px, py, pc = _flip(x, (p >> 2) & 1), _flip(y, (p >> 1) & 1), _flip(c, p & 1)
        peer = 4 * px + 2 * py + pc
        if arrivals:
            cps.append(_remote(src_ref.at[me], land_ref.at[peer], ssem.at[p - 1], rsem.at[p - 1], (x, y, c)))
        else:
            cps.append(_remote(src_ref.at[peer], land_ref.at[me], ssem.at[p - 1], rsem.at[p - 1], (px, py, pc)))
    return cps


def scatter_start(g, before, name):
    nb = len(before)

    def body(g_ref, land_ref, *rest):
        ssem, rsem, g_thru, land_thru, token = rest[nb:]
        for cp in _scatter_descriptors(g_ref, land_ref, ssem, rsem, False):
            cp.start()
        token[...] = jnp.zeros_like(token)

    return pl.pallas_call(
        body, name=name,
        out_shape=(pltpu.SemaphoreType.DMA((N_DEV - 1,)), pltpu.SemaphoreType.DMA((N_DEV - 1,)),
                   pltpu.HBM(g.shape, g.dtype), pltpu.HBM(g.shape, g.dtype), jax.ShapeDtypeStruct((8, 128), F32)),
        in_specs=(IN_HBM, IN_HBM) + (ANY,) * nb, out_specs=(SEM, SEM, IN_HBM, IN_HBM, pl.BlockSpec(memory_space=pltpu.VMEM)),
        input_output_aliases={0: 2, 1: 3},
        compiler_params=pltpu.CompilerParams(has_side_effects=DATAFLOW),
    )(pltpu.with_memory_space_constraint(g, pltpu.HBM),
      pltpu.with_memory_space_constraint(lax.empty(g.shape, g.dtype), pltpu.HBM), *before)


def scatter_wait(ssem, rsem, g_thru, land_thru, after, name):
    def body(g_ref, land_ref, ssem_ref, rsem_ref, after_ref, g_out, land_out):
        cps = _scatter_descriptors(g_ref, land_ref, ssem_ref, rsem_ref, True)
        for cp in cps:
            cp.wait_send()
        for cp in cps:
            cp.wait_recv()

    return pl.pallas_call(
        body, name=name,
        out_shape=(pltpu.HBM(g_thru.shape, g_thru.dtype), pltpu.HBM(land_thru.shape, land_thru.dtype)),
        in_specs=(IN_HBM, IN_HBM, SEM, SEM, ANY), out_specs=(IN_HBM, IN_HBM),
        input_output_aliases={0: 0, 1: 1},
        compiler_params=pltpu.CompilerParams(has_side_effects=DATAFLOW),
    )(g_thru, land_thru, ssem, rsem, after)


class Hosted:
    def __init__(self, gathers=(), scatters=()):
        self.items = [(a, False) for a in gathers] + [(a, True) for a in scatters]
        self.n = len(self.items)
        self.operands = [a for a, _ in self.items]
        self.out_shapes = [jax.ShapeDtypeStruct(a.shape if sc else (N_DEV,) + a.shape, a.dtype) for a, sc in self.items]
        self.scratch = [pltpu.SemaphoreType.DMA((7 * self.n,)), pltpu.SemaphoreType.DMA((7 * self.n,)),
                        pltpu.SemaphoreType.DMA((self.n,))]

    def _copies(self, in_refs, out_refs, ssem, rsem, lsem, arrivals):
        x, y, c = _coords()
        me = 4 * x + 2 * y + c
        remote, local = [], []
        for w, (_, sc) in enumerate(self.items):
            src, dst = in_refs[w], out_refs[w]
            local.append(pltpu.make_async_copy(src.at[me] if sc else src, dst.at[me], lsem.at[w]))
            for p in range(1, N_DEV):
                px, py, pc = _flip(x, (p >> 2) & 1), _flip(y, (p >> 1) & 1), _flip(c, p & 1)
                peer = 4 * px + 2 * py + pc
                k = 7 * w + p - 1
                if arrivals:
                    remote.append(_remote(src.at[me] if sc else src, dst.at[peer], ssem.at[k], rsem.at[k], (x, y, c)))
                else:
                    remote.append(_remote(src.at[peer] if sc else src, dst.at[me], ssem.at[k], rsem.at[k], (px, py, pc)))
        return remote, local

    def start(self, in_refs, out_refs, ssem, rsem, lsem):
        sends, local = self._copies(in_refs, out_refs, ssem, rsem, lsem, False)
        for cp in local + sends:
            cp.start()

    def wait(self, in_refs, out_refs, ssem, rsem, lsem):
        arrivals, local = self._copies(in_refs, out_refs, ssem, rsem, lsem, True)
        for cp in arrivals:
            cp.wait_recv()
        for cp in arrivals:
            cp.wait_send()
        for cp in local:
            cp.wait()


def _call(body, *, name, grid, in_specs, out_specs, out_shape, operands, params, scratch_shapes=(), aliases=None, hosted=None):
    n_in, n_out, n_scr = len(in_specs), len(out_specs), len(scratch_shapes)
    h = hosted.n if hosted is not None else 0

    def wrapped(*refs):
        ins, cins = refs[:n_in], refs[n_in:n_in + h]
        outs, couts = refs[n_in + h:n_in + h + n_out], refs[n_in + h + n_out:n_in + 2 * h + n_out]
        rest = refs[n_in + 2 * h + n_out:]
        scr, sems = rest[:n_scr], rest[n_scr:]
        if h:
            ids = [pl.program_id(a) for a in range(len(grid))]
            first, last = ids[0] == 0, ids[0] == grid[0] - 1
            for a in range(1, len(grid)):
                first, last = first & (ids[a] == 0), last & (ids[a] == grid[a] - 1)

            @pl.when(first)
            def _():
                hosted.start(cins, couts, *sems)

        body(*ins, *outs, *scr)
        if h:
            @pl.when(last)
            def _():
                hosted.wait(cins, couts, *sems)

    res = pl.pallas_call(
        wrapped, name=name, grid=grid,
        in_specs=list(in_specs) + [ANY] * h, out_specs=list(out_specs) + [ANY] * h,
        out_shape=list(out_shape) + (hosted.out_shapes if h else []),
        scratch_shapes=list(scratch_shapes) + (hosted.scratch if h else []),
        input_output_aliases=aliases or {}, compiler_params=params,
    )(*operands, *(hosted.operands if h else []))
    return list(res[:n_out]), list(res[n_out:])


def _row_tile(R, C):
    if R * C <= (1 << 18):
        return R
    return max((d for d in range(8, 257, 8) if R % d == 0), default=R)


def _cond_rows(call_ref, cctx_ref, z_ref, D):
    z_ref[...] = jnp.zeros_like(z_ref)
    for a in range(N_DEV):
        z_ref[a:a + 1, :] = call_ref[a][:, :D]
    z_ref[N_DEV:N_DEV + 1, :] = cctx_ref[...]


def mod_forward(call, c_ctx, w_loc, b_loc, name):
    D, cols = w_loc.shape

    def body(call_ref, cctx_ref, w_ref, b_ref, o_ref, z_ref):
        _cond_rows(call_ref, cctx_ref, z_ref, D)
        z = z_ref[...]
        s = z * _sigmoid(z)
        o_ref[...] = jnp.dot(s, w_ref[...], preferred_element_type=F32, precision=HIGHEST) + b_ref[...]

    return pl.pallas_call(
        body, name=name, out_shape=jax.ShapeDtypeStruct((16, cols), F32),
        scratch_shapes=[pltpu.VMEM((16, D), F32)],
        compiler_params=_params(vmem=VMEM_BIG),
    )(call, c_ctx, w_loc, b_loc)


def mod_backward(call, c_ctx, w_loc, dlat_loc, dctx_loc, gath, n_small_off, n_small, name):
    D, cols = w_loc.shape

    def body(call_ref, cctx_ref, w_ref, dlat_ref, dctx_ref, g_ref, gw_ref, pc_ref, small_ref, z_ref, dm_ref):
        _cond_rows(call_ref, cctx_ref, z_ref, D)
        z = z_ref[...]
        s = z * _sigmoid(z)
        dctx = dctx_ref[0:1, :]
        for a in range(1, N_DEV):
            dctx = dctx + dctx_ref[a:a + 1, :]
        dm_ref[...] = jnp.zeros_like(dm_ref)
        dm_ref[0:N_DEV, :] = dlat_ref[...]
        dm_ref[N_DEV:N_DEV + 1, :] = dctx
        gw_ref[...] = lax.dot_general(s, dm_ref[...], TN, preferred_element_type=F32, precision=HIGHEST)
        pc_ref[...] = lax.dot_general(dctx, w_ref[...], NT, preferred_element_type=F32, precision=HIGHEST)
        acc = g_ref[0][:, n_small_off:n_small_off + n_small]
        for a in range(1, N_DEV):
            acc = acc + g_ref[a][:, n_small_off:n_small_off + n_small]
        small_ref[...] = acc

    return pl.pallas_call(
        body, name=name,
        out_shape=(jax.ShapeDtypeStruct((D, cols), F32), jax.ShapeDtypeStruct((1, D), F32),
                   jax.ShapeDtypeStruct((1, n_small), F32)),
        scratch_shapes=[pltpu.VMEM((16, D), F32), pltpu.VMEM((16, cols), F32)],
        compiler_params=_params(vmem=VMEM_BIG),
    )(call, c_ctx, w_loc, dlat_loc, dctx_loc, gath)


def bmod_and_cctx_grad(gath, pcs, c_ctx, nd, name):
    D = c_ctx.shape[-1]

    def body(g_ref, pc_ref, cctx_ref, gb_ref, gc_ref):
        acc = g_ref[0][:, :nd] + g_ref[0][:, nd:2 * nd]
        for a in range(1, N_DEV):
            acc = acc + (g_ref[a][:, :nd] + g_ref[a][:, nd:2 * nd])
        gb_ref[...] = acc
        p = pc_ref[0]
        for a in range(1, N_DEV):
            p = p + pc_ref[a]
        z = cctx_ref[...]
        sg = _sigmoid(z)
        gc_ref[...] = p * (sg * (1.0 + z * (1.0 - sg)))

    return pl.pallas_call(
        body, name=name,
        out_shape=(jax.ShapeDtypeStruct((1, nd), F32), jax.ShapeDtypeStruct((1, D), F32)),
    )(gath, pcs, c_ctx)


def _mod_spec(D, which, nctx):
    return pl.BlockSpec((1, 3, D), lambda i: (jnp.where(i < nctx, 0, 3) + which, 0, 0))


def _acc_spec(D, nctx):
    return pl.BlockSpec((1, 8, D), lambda i: (jnp.where(i < nctx, 0, 1), 0, 0))


def _row(tm, n):
    return pl.BlockSpec((tm, n), lambda i: (i, 0))


def _two_stream_specs(tm, D, nctx):
    return [pl.BlockSpec((tm, D), lambda i: (jnp.minimum(i, nctx - 1), 0)),
            pl.BlockSpec((tm, D), lambda i: (jnp.maximum(i - nctx, 0), 0))]


def _final_norm_loss_backward(x, tgt, gg, i, dx_ref, acc_ref):
    @pl.when(i == 0)
    def _():
        acc_ref[...] = jnp.zeros_like(acc_ref)

    D = x.shape[1]
    r = lax.rsqrt(jnp.mean(x * x, axis=-1, keepdims=True) + EPS)
    xh = x * r
    e = xh * gg - tgt
    part = 0.5 * jnp.sum(jnp.mean(e * e, axis=-1, keepdims=True), axis=0, keepdims=True)
    dy = e * (1.0 / D)
    dyg = dy * gg
    dx_ref[...] = r * (dyg - xh * jnp.mean(dyg * xh, axis=-1, keepdims=True))
    acc_ref[0:1, :] += jnp.sum(dy * xh, axis=0, keepdims=True)
    acc_ref[1:2, :] += jnp.broadcast_to(part, (1, D))


def _hidden_chunks(F):
    step = 1024 if F % 256 == 0 else F
    return [(lo, min(lo + step, F)) for lo in range(0, F, step)]


def ffn_forward(srcs, mod6, which, g, wt, w_out, nctx, name, hosted=None, final=None):
    D = srcs[-1].shape[1]
    Tr = sum(s.shape[0] for s in srcs)
    F = wt.shape[0] // 2
    tm = ROW_TILE
    two = len(srcs) == 2
    nfin = 0 if final is None else 2

    def body(*refs):
        x_refs, fin_refs, rest = refs[:len(srcs)], refs[len(srcs):len(srcs) + nfin], refs[len(srcs) + nfin:]
        mod_ref, g_ref, wt_ref, wout_ref, xo_ref, hm_ref, ab_ref, h_ref, f_ref = rest[:9]
        x = jnp.where(pl.program_id(0) < nctx, x_refs[0][...], x_refs[1][...]) if two else x_refs[0][...]
        ms = mod_ref[0]
        shift, scale, gate = ms[0:1], ms[1:2], ms[2:3]
        r = lax.rsqrt(jnp.mean(x * x, axis=-1, keepdims=True) + EPS)
        hb = (((x * r) * g_ref[...]) * (1.0 + scale) + shift).astype(BF)
        hm_ref[...] = hb
        f = jnp.zeros((tm, D), F32)
        for lo, hi in _hidden_chunks(F):
            a = lax.dot_general(hb, wt_ref[lo:hi, :], NT, preferred_element_type=F32)
            b = lax.dot_general(hb, wt_ref[F + lo:F + hi, :], NT, preferred_element_type=F32)
            ab_ref[:, lo:hi] = a.astype(BF)
            ab_ref[:, F + lo:F + hi] = b.astype(BF)
            h = ((a * _sigmoid(a)) * b).astype(BF)
            h_ref[:, lo:hi] = h
            f = f + jnp.dot(h, wout_ref[lo:hi, :], preferred_element_type=F32)
        f_ref[...] = f.astype(BF)
        xo = x + (0.5 * gate) * f
        if final is None:
            xo_ref[...] = xo
        else:
            _final_norm_loss_backward(xo, fin_refs[0][...], fin_refs[1][...], pl.program_id(0), xo_ref, rest[9])

    src_specs = _two_stream_specs(tm, D, nctx) if two else [_row(tm, D)]
    fin = final is not None
    return _call(
        body, name=name, grid=(Tr // tm,),
        in_specs=src_specs + ([_row(tm, D), _resident((1, D))] if fin else [])
                 + [_mod_spec(D, which, nctx), _resident((1, D)), _resident(wt.shape), _resident(w_out.shape)],
        out_specs=[_row(tm, D), _row(tm, D), _row(tm, 2 * F), _row(tm, F), _row(tm, D)]
                  + ([pl.BlockSpec((8, D), lambda i: (0, 0))] if fin else []),
        out_shape=[jax.ShapeDtypeStruct((Tr, D), F32), jax.ShapeDtypeStruct((Tr, D), BF),
                   jax.ShapeDtypeStruct((Tr, 2 * F), BF), jax.ShapeDtypeStruct((Tr, F), BF),
                   jax.ShapeDtypeStruct((Tr, D), BF)] + ([jax.ShapeDtypeStruct((8, D), F32)] if fin else []),
        operands=[*srcs, *(final or ()), mod6, g, wt, w_out], hosted=hosted,
        params=_params(vmem=VMEM_BIG, sem=("arbitrary",)))


def ffn_backward_rows(dxo, srcs, mod6, which, g, ab, fo, wt, w_out, nctx, name, hosted=None):
    D = srcs[-1].shape[1]
    Tr = sum(s.shape[0] for s in srcs)
    Tl = srcs[-1].shape[0]
    F = wt.shape[0] // 2
    tm = ROW_TILE
    two = len(srcs) == 2

    def body(*refs):
        dxo_ref, x_refs = refs[0], refs[1:1 + len(srcs)]
        mod_ref, g_ref, ab_ref, fo_ref, wt_ref, wout_ref, dx_ref, dab_ref, df_ref, acc_ref = refs[1 + len(srcs):]
        i = pl.program_id(0)

        @pl.when((i == 0) | (i == nctx))
        def _():
            acc_ref[...] = jnp.zeros_like(acc_ref)

        dxo = dxo_ref[...]
        x = jnp.where(i < nctx, x_refs[0][...], x_refs[1][...]) if two else x_refs[0][...]
        ms = mod_ref[0]
        scale, gate = ms[1:2], ms[2:3]
        gg = g_ref[...]
        dgate = jnp.sum(dxo * fo_ref[...].astype(F32), axis=0, keepdims=True) * 0.5
        dfb = (dxo * (0.5 * gate)).astype(BF)
        df_ref[...] = dfb
        dhm = jnp.zeros((tm, D), F32)
        for lo, hi in _hidden_chunks(F):
            dh = lax.dot_general(dfb, wout_ref[lo:hi, :], NT, preferred_element_type=F32)
            a = ab_ref[:, lo:hi].astype(F32)
            b = ab_ref[:, F + lo:F + hi].astype(F32)
            sg = _sigmoid(a)
            da = ((dh * b) * (sg * (1.0 + a * (1.0 - sg)))).astype(BF)
            db = (dh * (a * sg)).astype(BF)
            dab_ref[:, lo:hi] = da
            dab_ref[:, F + lo:F + hi] = db
            dhm = dhm + jnp.dot(da, wt_ref[lo:hi, :], preferred_element_type=F32)
            dhm = dhm + jnp.dot(db, wt_ref[F + lo:F + hi, :], preferred_element_type=F32)
        r = lax.rsqrt(jnp.mean(x * x, axis=-1, keepdims=True) + EPS)
        xh = x * r
        dshift = jnp.sum(dhm, axis=0, keepdims=True)
        dscale = jnp.sum(dhm * (xh * gg), axis=0, keepdims=True)
        dxh_g = dhm * (1.0 + scale)
        dg = jnp.sum(dxh_g * xh, axis=0, keepdims=True)
        dxh = dxh_g * gg
        dx_ref[...] = dxo + r * (dxh - xh * jnp.mean(dxh * xh, axis=-1, keepdims=True))
        for k, val in enumerate((dshift, dscale, dgate, dg)):
            acc_ref[0, k:k + 1, :] += val

    src_specs = _two_stream_specs(tm, D, nctx) if two else [_row(tm, D)]
    dx_spec = pl.BlockSpec((tm, D), lambda i: (jnp.maximum(i - nctx, 0), 0))
    return _call(
        body, name=name, grid=(Tr // tm,),
        in_specs=[_row(tm, D)] + src_specs + [_mod_spec(D, which, nctx), _resident((1, D)), _row(tm, 2 * F), _row(tm, D),
                                              _resident(wt.shape), _resident(w_out.shape)],
        out_specs=[dx_spec, _row(tm, 2 * F), _row(tm, D), _acc_spec(D, nctx)],
        out_shape=[jax.ShapeDtypeStruct((Tl, D), F32), jax.ShapeDtypeStruct((Tr, 2 * F), BF),
                   jax.ShapeDtypeStruct((Tr, D), BF), jax.ShapeDtypeStruct((2, 8, D), F32)],
        operands=[dxo, *srcs, mod6, g, ab, fo, wt, w_out], hosted=hosted,
        params=_params(vmem=VMEM_BIG, sem=("arbitrary",)))


def _token_tile(T):
    return _pick(T, (2048, 1408, 1024, 768, 512, 256, 128))


def tn_matmul(a, b, name, hosted=None):
    T, K = a.shape
    N = b.shape[1]
    tk = _pick(K, (1024, 1408, 1664, 768, 512, 384, 256, 128))
    tn = _pick(N, (1024, 1408, 1664, 768, 512, 384, 256, 128))
    tt = _token_tile(T)
    nt = T // tt

    def body(a_ref, b_ref, o_ref, acc_ref):
        t = pl.program_id(2)

        @pl.when(t == 0)
        def _():
            acc_ref[...] = jnp.zeros_like(acc_ref)

        acc_ref[...] += lax.dot_general(a_ref[...], b_ref[...], TN, preferred_element_type=F32)

        @pl.when(t == nt - 1)
        def _():
            o_ref[...] = acc_ref[...].astype(BF)

    (out,), exchanged = _call(
        body, name=name, grid=(K // tk, N // tn, nt),
        in_specs=[pl.BlockSpec((tt, tk), lambda k, n, t: (t, k)), pl.BlockSpec((tt, tn), lambda k, n, t: (t, n))],
        out_specs=[pl.BlockSpec((tk, tn), lambda k, n, t: (k, n))],
        out_shape=[jax.ShapeDtypeStruct((K, N), BF)],
        scratch_shapes=[pltpu.VMEM((tk, tn), F32)],
        operands=[a, b], hosted=hosted,
        params=_params(vmem=VMEM_BIG, sem=("arbitrary", "arbitrary", "arbitrary")))
    return out, exchanged


def _rope_apply(y, cos, s_next, s_prev):
    return y * cos + pltpu.roll(y, HEAD_DIM - 32, 1) * s_next + pltpu.roll(y, 32, 1) * s_prev


def _rope_transpose(dz, cos, s_next, s_prev):
    return dz * cos + pltpu.roll(dz * s_next, 32, 1) + pltpu.roll(dz * s_prev, HEAD_DIM - 32, 1)


def proj_forward(xa, mod6, g, w_in, qg, kg, tabs, offs, nctx, name, hosted=None):
    Tr, D = xa.shape
    P = w_in.shape[1]
    tm = ROW_TILE
    qo, ko = offs["q"], offs["k"]
    qw, kw = N_Q_HEADS * HEAD_DIM, N_KV_HEADS * HEAD_DIM
    scale_q = HEAD_DIM ** -0.5 * LOG2E

    def body(x_ref, mod_ref, g_ref, w_ref, qg_ref, kg_ref, tab_ref, hx_ref, pr_ref, q_ref, k_ref):
        x = x_ref[...]
        ms = mod_ref[0]
        shift, scale = ms[0:1], ms[1:2]
        r = lax.rsqrt(jnp.mean(x * x, axis=-1, keepdims=True) + EPS)
        hb = (((x * r) * g_ref[...]) * (1.0 + scale) + shift).astype(BF)
        hx_ref[...] = hb
        pr = jnp.dot(hb, w_ref[...], preferred_element_type=F32)
        pr_ref[...] = pr.astype(BF)
        cos, s_next, s_prev = tab_ref[0], tab_ref[1], tab_ref[2]

        def head(v, gain):
            n = v * lax.rsqrt(jnp.mean(v * v, axis=-1, keepdims=True) + EPS)
            return _rope_apply(n * gain, cos, s_next, s_prev)

        for h in range(N_Q_HEADS):
            lo = qo + h * HEAD_DIM
            q_ref[:, h * HEAD_DIM:(h + 1) * HEAD_DIM] = (head(pr[:, lo:lo + HEAD_DIM], qg_ref[...]) * scale_q).astype(BF)
        for h in range(N_KV_HEADS):
            lo = ko + h * HEAD_DIM
            k_ref[:, h * HEAD_DIM:(h + 1) * HEAD_DIM] = head(pr[:, lo:lo + HEAD_DIM], kg_ref[...]).astype(BF)

    return _call(
        body, name=name, grid=(Tr // tm,),
        in_specs=[_row(tm, D), _mod_spec(D, 1, nctx), _resident((1, D)), _resident(w_in.shape),
                  _resident((1, HEAD_DIM)), _resident((1, HEAD_DIM)),
                  pl.BlockSpec((3, tm, HEAD_DIM), lambda i: (0, i, 0))],
        out_specs=[_row(tm, D), _row(tm, P), _row(tm, qw), _row(tm, kw)],
        out_shape=[jax.ShapeDtypeStruct((Tr, D), BF), jax.ShapeDtypeStruct((Tr, P), BF),
                   jax.ShapeDtypeStruct((Tr, qw), BF), jax.ShapeDtypeStruct((Tr, kw), BF)],
        operands=[xa, mod6, g, w_in, qg, kg, tabs], hosted=hosted,
        params=_params(vmem=VMEM_BIG, sem=("arbitrary",)))


def _shifted(u, first_row, last_row):
    T = u.shape[0]
    prev = jnp.where(first_row, 0.0, pltpu.roll(u, 1, 0))
    nxt = jnp.where(last_row, 0.0, pltpu.roll(u, T - 1, 0))
    return prev, nxt


def conv_forward(proj, conv_w, offs, Tc, name):
    Ta = proj.shape[0]
    T = Ta - Tc
    Dc = conv_w.shape[1]
    cb = offs["cv"] // 384

    def body(p_ref, w_ref, y_ref):
        rows = lax.broadcasted_iota(jnp.int32, (T, 128), 0)
        u = p_ref[pl.ds(Tc, T), 128:256].astype(F32) * p_ref[pl.ds(Tc, T), 256:384].astype(F32)
        prev, nxt = _shifted(u, rows == 0, rows == T - 1)
        w = w_ref[...]
        cv = prev * w[0:1] + u * w[1:2] + nxt * w[2:3]
        y_ref[...] = (p_ref[pl.ds(Tc, T), 0:128].astype(F32) * cv).astype(BF)

    return pl.pallas_call(
        body, name=name, grid=(Dc // 128,),
        in_specs=[pl.BlockSpec((Ta, 384), lambda j: (0, cb + j)), pl.BlockSpec((CONV_TAPS, 128), lambda j: (0, j))],
        out_specs=pl.BlockSpec((T, 128), lambda j: (0, j)),
        out_shape=jax.ShapeDtypeStruct((T, Dc), BF),
        compiler_params=_params(vmem=VMEM_BIG, sem=("arbitrary",)),
    )(proj, conv_w)


def conv_backward(dproj, dy, proj, conv_w, offs, Tc, name):
    Ta = proj.shape[0]
    T = Ta - Tc
    Dc = conv_w.shape[1]
    cb = offs["cv"] // 384

    def body(dp_any, dy_ref, p_ref, w_ref, o_ref, dw_ref):
        rows = lax.broadcasted_iota(jnp.int32, (T, 128), 0)
        first, last = rows == 0, rows == T - 1
        bg = p_ref[pl.ds(Tc, T), 0:128].astype(F32)
        cg = p_ref[pl.ds(Tc, T), 128:256].astype(F32)
        vc = p_ref[pl.ds(Tc, T), 256:384].astype(F32)
        dy = dy_ref[...].astype(F32)
        u = cg * vc
        prev, nxt = _shifted(u, first, last)
        w = w_ref[...]
        cv = prev * w[0:1] + u * w[1:2] + nxt * w[2:3]
        o_ref[pl.ds(0, Tc), :] = jnp.zeros((Tc, 384), BF)
        o_ref[pl.ds(Tc, T), 0:128] = (dy * cv).astype(BF)
        dcv = dy * bg
        dprev, dnxt = _shifted(dcv, first, last)
        du = dnxt * w[0:1] + dcv * w[1:2] + dprev * w[2:3]
        o_ref[pl.ds(Tc, T), 128:256] = (du * vc).astype(BF)
        o_ref[pl.ds(Tc, T), 256:384] = (du * cg).astype(BF)
        dw_ref[...] = jnp.zeros_like(dw_ref)
        for k, tap in enumerate((prev, u, nxt)):
            dw_ref[k:k + 1, :] = jnp.sum(dcv * tap, axis=0, keepdims=True)

    blk = pl.BlockSpec((Ta, 384), lambda j: (0, cb + j))
    return pl.pallas_call(
        body, name=name, grid=(Dc // 128,),
        in_specs=[ANY, pl.BlockSpec((T, 128), lambda j: (0, j)), blk, pl.BlockSpec((CONV_TAPS, 128), lambda j: (0, j))],
        out_specs=[blk, pl.BlockSpec((8, 128), lambda j: (0, j))],
        out_shape=[jax.ShapeDtypeStruct(dproj.shape, BF), jax.ShapeDtypeStruct((8, Dc), F32)],
        input_output_aliases={0: 0},
        compiler_params=_params(vmem=VMEM_BIG, sem=("arbitrary",)),
    )(dproj, dy, proj, conv_w)


def _kv_chunk(Ta):
    return _pick(Ta, (768, 512, 384, 256, 128))


def _stack_heads(v):
    return jnp.concatenate([v[:, h * HEAD_DIM:(h + 1) * HEAD_DIM] for h in range(GROUP)], axis=0)


def attention_forward(q, k, proj, offs, Tc, name, hosted=None):
    Ta = k.shape[0]
    T = Ta - Tc
    tq = ROW_TILE
    kc = _kv_chunk(Ta)
    nkv = Ta // kc
    gw = GROUP * HEAD_DIM
    vblk = offs["v"] // HEAD_DIM
    qoff = Tc // tq
    n = GROUP * tq

    def body(q_ref, k_ref, v_ref, o_ref, lse_ref, vx_ref, qs_ref, s0_ref, s1_ref, m_ref, acc_ref):
        @pl.when(pl.program_id(1) == 0)
        def _():
            vx_ref[:, 0:HEAD_DIM] = v_ref[...]
            vx_ref[:, HEAD_DIM:2 * HEAD_DIM] = jnp.ones((Ta, HEAD_DIM), BF)

        qs_ref[...] = _stack_heads(q_ref[...])
        m_ref[...] = jnp.full((n, 1), -1e30, F32)
        acc_ref[...] = jnp.zeros((n, 2 * HEAD_DIM), F32)

        def rows(c):
            return pl.ds(pl.multiple_of(c * kc, kc), kc)

        def logits(c, dst):
            dst[...] = lax.dot_general(qs_ref[...], k_ref[rows(c), :], NT, preferred_element_type=F32)

        def consume(src, c):
            s = src[...]
            m_prev = m_ref[...]
            m_new = jnp.maximum(m_prev, jnp.max(s, axis=-1, keepdims=True))
            p = jnp.exp2(s - m_new).astype(BF)
            acc_ref[...] = jnp.exp2(m_prev - m_new) * acc_ref[...] + jnp.dot(p, vx_ref[rows(c), :], preferred_element_type=F32)
            m_ref[...] = m_new

        def pair(i, carry):
            logits(2 * i + 1, s1_ref)
            consume(s0_ref, 2 * i)
            logits(2 * i + 2, s0_ref)
            consume(s1_ref, 2 * i + 1)
            return carry

        logits(0, s0_ref)
        if nkv % 2:
            lax.fori_loop(0, nkv // 2, pair, 0)
            consume(s0_ref, nkv - 1)
        else:
            lax.fori_loop(0, nkv // 2 - 1, pair, 0)
            logits(nkv - 1, s1_ref)
            consume(s0_ref, nkv - 2)
            consume(s1_ref, nkv - 1)
        acc = acc_ref[...]
        l = acc[:, HEAD_DIM:HEAD_DIM + 1]
        o = acc[:, 0:HEAD_DIM] / l
        lse = m_ref[...] + jnp.log2(l)
        for h in range(GROUP):
            o_ref[:, h * HEAD_DIM:(h + 1) * HEAD_DIM] = o[h * tq:(h + 1) * tq].astype(BF)
            lse_ref[0, :, h:h + 1] = lse[h * tq:(h + 1) * tq]

    return _call(
        body, name=name, grid=(N_KV_HEADS, T // tq),
        in_specs=[pl.BlockSpec((tq, gw), lambda j, i: (i + qoff, j)),
                  pl.BlockSpec((Ta, HEAD_DIM), lambda j, i: (0, j)),
                  pl.BlockSpec((Ta, HEAD_DIM), lambda j, i: (0, vblk + j))],
        out_specs=[pl.BlockSpec((tq, gw), lambda j, i: (i, j)),
                   pl.BlockSpec((1, tq, GROUP), lambda j, i: (j, i, 0))],
        out_shape=[jax.ShapeDtypeStruct((T, N_Q_HEADS * HEAD_DIM), BF),
                   jax.ShapeDtypeStruct((N_KV_HEADS, T, GROUP), F32)],
        scratch_shapes=[pltpu.VMEM((Ta, 2 * HEAD_DIM), BF), pltpu.VMEM((n, HEAD_DIM), BF), pltpu.VMEM((n, kc), F32),
                        pltpu.VMEM((n, kc), F32), pltpu.VMEM((n, 1), F32), pltpu.VMEM((n, 2 * HEAD_DIM), F32)],
        operands=[q, k, proj], hosted=hosted,
        params=_params(vmem=VMEM_BIG, sem=("arbitrary", "arbitrary")))


def _norm_rope_backward(dz, raw, gg, cos, s_next, s_prev):
    r = lax.rsqrt(jnp.mean(raw * raw, axis=-1, keepdims=True) + EPS)
    n = raw * r
    dy = _rope_transpose(dz, cos, s_next, s_prev)
    dn = dy * gg
    return r * (dn - n * jnp.mean(dn * n, axis=-1, keepdims=True)), jnp.sum(dy * n, axis=0, keepdims=True)


def attention_backward(dproj, q, k, proj, o, lse, do, qgain, tabs, offs, Tc, name, hosted=None):
    Ta = k.shape[0]
    tq = ROW_TILE
    nctx = Tc // tq
    kc = _kv_chunk(Ta)
    gw = GROUP * HEAD_DIM
    vblk = offs["v"] // HEAD_DIM
    qblk = offs["q"] // gw
    zscale = HEAD_DIM ** -0.5

    def body(dp_any, q_ref, k_ref, v_ref, o_ref, lse_ref, do_ref, raw_ref, g_ref, tab_ref, dqr_ref, dk_ref, dv_ref, dg_ref):
        j, i = pl.program_id(0), pl.program_id(1)

        @pl.when(i == 0)
        def _():
            dk_ref[...] = jnp.zeros_like(dk_ref)
            dv_ref[...] = jnp.zeros_like(dv_ref)

        @pl.when((i == 0) & (j == 0))
        def _():
            dg_ref[...] = jnp.zeros_like(dg_ref)

        @pl.when(i < nctx)
        def _():
            dqr_ref[...] = jnp.zeros_like(dqr_ref)

        @pl.when(i >= nctx)
        def _():
            qs = _stack_heads(q_ref[...])
            dob = do_ref[...]
            dos = _stack_heads(dob)
            delta = jnp.concatenate(
                [jnp.sum(dob[:, h * HEAD_DIM:(h + 1) * HEAD_DIM].astype(F32)
                         * o_ref[:, h * HEAD_DIM:(h + 1) * HEAD_DIM].astype(F32), axis=-1, keepdims=True)
                 for h in range(GROUP)], axis=0)
            lse = jnp.concatenate([lse_ref[0, :, h:h + 1] for h in range(GROUP)], axis=0)

            def step(c, dq):
                rows = pl.ds(pl.multiple_of(c * kc, kc), kc)
                kk = k_ref[rows, :]
                vv = v_ref[rows, :]
                s = lax.dot_general(qs, kk, NT, preferred_element_type=F32)
                p = jnp.exp2(s - lse)
                dp = lax.dot_general(dos, vv, NT, preferred_element_type=F32)
                ds = (p * (dp - delta)).astype(BF)
                dv_ref[rows, :] += lax.dot_general(p.astype(BF), dos, TN, preferred_element_type=F32)
                dk_ref[rows, :] += lax.dot_general(ds, qs, TN, preferred_element_type=F32)
                return dq + jnp.dot(ds, kk, preferred_element_type=F32)

            dq = lax.fori_loop(0, Ta // kc, step, jnp.zeros((GROUP * tq, HEAD_DIM), F32))
            per_head = lambda t: jnp.concatenate([t] * GROUP, axis=0)
            dr, dg = _norm_rope_backward(dq * zscale, _stack_heads(raw_ref[...]).astype(F32), g_ref[...],
                                         per_head(tab_ref[0]), per_head(tab_ref[1]), per_head(tab_ref[2]))
            for h in range(GROUP):
                dqr_ref[:, h * HEAD_DIM:(h + 1) * HEAD_DIM] = dr[h * tq:(h + 1) * tq].astype(BF)
            dg_ref[0:1, :] += dg

    lat = lambda j, i: (jnp.maximum(i - nctx, 0), j)
    (dproj, dk, dv, dqg), exchanged = _call(
        body, name=name, grid=(N_KV_HEADS, Ta // tq),
        in_specs=[ANY, pl.BlockSpec((tq, gw), lambda j, i: (i, j)),
                  pl.BlockSpec((Ta, HEAD_DIM), lambda j, i: (0, j)),
                  pl.BlockSpec((Ta, HEAD_DIM), lambda j, i: (0, vblk + j)),
                  pl.BlockSpec((tq, gw), lat),
                  pl.BlockSpec((1, tq, GROUP), lambda j, i: (j, jnp.maximum(i - nctx, 0), 0)),
                  pl.BlockSpec((tq, gw), lat),
                  pl.BlockSpec((tq, gw), lambda j, i: (i, qblk + j)),
                  pl.BlockSpec((1, HEAD_DIM), lambda j, i: (0, 0)),
                  pl.BlockSpec((3, tq, HEAD_DIM), lambda j, i: (0, i, 0))],
        out_specs=[pl.BlockSpec((tq, gw), lambda j, i: (i, qblk + j)),
                   pl.BlockSpec((Ta, HEAD_DIM), lambda j, i: (0, j)),
                   pl.BlockSpec((Ta, HEAD_DIM), lambda j, i: (0, j)),
                   pl.BlockSpec((8, HEAD_DIM), lambda j, i: (0, 0))],
        out_shape=[jax.ShapeDtypeStruct(dproj.shape, BF),
                   jax.ShapeDtypeStruct((Ta, N_KV_HEADS * HEAD_DIM), F32),
                   jax.ShapeDtypeStruct((Ta, N_KV_HEADS * HEAD_DIM), F32),
                   jax.ShapeDtypeStruct((8, HEAD_DIM), F32)],
        operands=[dproj, q, k, proj, o, lse, do, proj, qgain, tabs], hosted=hosted, aliases={0: 0},
        params=_params(vmem=VMEM_BIG, sem=("arbitrary", "arbitrary")))
    return (dproj, dk, dv, dqg), exchanged


def kv_backward(dproj, dk, dv, proj, gain, tabs, offs, name):
    Ta = proj.shape[0]
    tm = _pick(Ta, (768, 512, ROW_TILE))
    kw = N_KV_HEADS * HEAD_DIM
    cb = offs["k"] // (2 * kw)
    kb = offs["k"] // kw
    zscale = 1.0 / LOG2E

    def body(dp_any, dk_ref, dv_ref, raw_ref, g_ref, tab_ref, o_ref, dg_ref):
        @pl.when(pl.program_id(0) == 0)
        def _():
            dg_ref[...] = jnp.zeros_like(dg_ref)

        cos, s_next, s_prev = tab_ref[0], tab_ref[1], tab_ref[2]
        dg = jnp.zeros((1, HEAD_DIM), F32)
        for h in range(N_KV_HEADS):
            sl = slice(h * HEAD_DIM, (h + 1) * HEAD_DIM)
            dr, dgh = _norm_rope_backward(dk_ref[:, sl] * zscale, raw_ref[:, sl].astype(F32), g_ref[...], cos, s_next, s_prev)
            o_ref[:, sl] = dr.astype(BF)
            dg = dg + dgh
        o_ref[:, kw:2 * kw] = dv_ref[...].astype(BF)
        dg_ref[0:1, :] += dg

    return pl.pallas_call(
        body, name=name, grid=(Ta // tm,),
        in_specs=[ANY, _row(tm, kw), _row(tm, kw), pl.BlockSpec((tm, kw), lambda i: (i, kb)),
                  _resident((1, HEAD_DIM)), pl.BlockSpec((3, tm, HEAD_DIM), lambda i: (0, i, 0))],
        out_specs=[pl.BlockSpec((tm, 2 * kw), lambda i: (i, cb)), pl.BlockSpec((8, HEAD_DIM), lambda i: (0, 0))],
        out_shape=[jax.ShapeDtypeStruct(dproj.shape, BF), jax.ShapeDtypeStruct((8, HEAD_DIM), F32)],
        input_output_aliases={0: 0},
        compiler_params=_params(sem=("arbitrary",)),
    )(dproj, dk, dv, proj, gain, tabs)


def merge_forward(x1, mod6, yc, o, proj, w_bc, w_ba, w_o, offs, Tc, name):
    T, D = yc.shape[0], x1.shape[1]
    tm = ROW_TILE
    roff = Tc // tm
    gb = offs["gt"] // (2 * D)

    def body(x_ref, mod_ref, yc_ref, o_ref, gt_ref, wbc_ref, wba_ref, wo_ref, xo_ref, pc_ref, pa_ref, m_ref, z_ref):
        gate = mod_ref[0][2:3]
        pc = jnp.dot(yc_ref[...], wbc_ref[...], preferred_element_type=F32)
        pa = jnp.dot(o_ref[...], wba_ref[...], preferred_element_type=F32)
        pc_ref[...] = pc.astype(BF)
        pa_ref[...] = pa.astype(BF)
        mb = (_sigmoid(gt_ref[:, 0:D].astype(F32)) * pc + _sigmoid(gt_ref[:, D:2 * D].astype(F32)) * pa).astype(BF)
        m_ref[...] = mb
        z = jnp.dot(mb, wo_ref[...], preferred_element_type=F32)
        z_ref[...] = z.astype(BF)
        xo_ref[...] = x_ref[...] + gate * z

    return pl.pallas_call(
        body, name=name, grid=(T // tm,),
        in_specs=[pl.BlockSpec((tm, D), lambda i: (i + roff, 0)), _mod_spec(D, 1, 0), _row(tm, yc.shape[1]), _row(tm, o.shape[1]),
                  pl.BlockSpec((tm, 2 * D), lambda i: (i + roff, gb)),
                  _resident(w_bc.shape), _resident(w_ba.shape), _resident(w_o.shape)],
        out_specs=[_row(tm, D)] * 5,
        out_shape=[jax.ShapeDtypeStruct((T, D), F32)] + [jax.ShapeDtypeStruct((T, D), BF)] * 4,
        compiler_params=_params(vmem=VMEM_BIG, sem=("arbitrary",)),
    )(x1, mod6, yc, o, proj, w_bc, w_ba, w_o)


def merge_backward_rows(dx2, mod6, z, pc, pa, proj, w_bc, w_ba, w_o, offs, Tc, name):
    T, D = dx2.shape
    Ta, P = proj.shape
    tm = ROW_TILE
    nctx = Tc // tm
    gb = offs["gt"] // (2 * D)
    dcw, dqw = w_bc.shape[0], w_ba.shape[0]

    def body(dx_ref, mod_ref, z_ref, pc_ref, pa_ref, gt_ref, wbc_ref, wba_ref, wo_ref,
             dgt_ref, dg_ref, dpc_ref, dpa_ref, dyc_ref, do_ref, acc_ref):
        i = pl.program_id(0)

        @pl.when(i == 0)
        def _():
            acc_ref[...] = jnp.zeros_like(acc_ref)

        @pl.when(i < nctx)
        def _():
            dgt_ref[...] = jnp.zeros_like(dgt_ref)

        @pl.when(i >= nctx)
        def _():
            gate = mod_ref[0][2:3]
            dx = dx_ref[...]
            acc_ref[0:1, :] += jnp.sum(dx * z_ref[...].astype(F32), axis=0, keepdims=True)
            dgb = (dx * gate).astype(BF)
            dg_ref[...] = dgb
            dm = lax.dot_general(dgb, wo_ref[...], NT, preferred_element_type=F32)
            sc = _sigmoid(gt_ref[:, 0:D].astype(F32))
            sa = _sigmoid(gt_ref[:, D:2 * D].astype(F32))
            pc = pc_ref[...].astype(F32)
            pa = pa_ref[...].astype(F32)
            dpc = (dm * sc).astype(BF)
            dpa = (dm * sa).astype(BF)
            dpc_ref[...] = dpc
            dpa_ref[...] = dpa
            dgt_ref[:, 0:D] = ((dm * pc) * (sc * (1.0 - sc))).astype(BF)
            dgt_ref[:, D:2 * D] = ((dm * pa) * (sa * (1.0 - sa))).astype(BF)
            dyc_ref[...] = lax.dot_general(dpc, wbc_ref[...], NT, preferred_element_type=F32).astype(BF)
            do_ref[...] = lax.dot_general(dpa, wba_ref[...], NT, preferred_element_type=F32).astype(BF)

    lat = lambda n: pl.BlockSpec((tm, n), lambda i: (jnp.maximum(i - nctx, 0), 0))
    return pl.pallas_call(
        body, name=name, grid=(Ta // tm,),
        in_specs=[lat(D), _mod_spec(D, 1, 0), lat(D), lat(D), lat(D),
                  pl.BlockSpec((tm, 2 * D), lambda i: (i, gb)),
                  _resident(w_bc.shape), _resident(w_ba.shape), _resident(w_o.shape)],
        out_specs=[pl.BlockSpec((tm, 2 * D), lambda i: (i, gb)), lat(D), lat(D), lat(D), lat(dcw), lat(dqw),
                   pl.BlockSpec((8, D), lambda i: (0, 0))],
        out_shape=[jax.ShapeDtypeStruct((Ta, P), BF)] + [jax.ShapeDtypeStruct((T, D), BF)] * 3
                  + [jax.ShapeDtypeStruct((T, dcw), BF), jax.ShapeDtypeStruct((T, dqw), BF), jax.ShapeDtypeStruct((8, D), F32)],
        compiler_params=_params(vmem=VMEM_BIG, sem=("arbitrary",)),
    )(dx2, mod6, z, pc, pa, proj, w_bc, w_ba, w_o)


def proj_backward_rows(dproj, dres, xa, mod6, g, w_in, nctx, name, hosted=None):
    Tr, D = xa.shape
    P = w_in.shape[1]
    tm = ROW_TILE

    def body(dp_ref, dres_ref, x_ref, mod_ref, g_ref, w_ref, dx_ref, acc_ref):
        i = pl.program_id(0)

        @pl.when((i == 0) | (i == nctx))
        def _():
            acc_ref[...] = jnp.zeros_like(acc_ref)

        x = x_ref[...]
        scale = mod_ref[0][1:2]
        gg = g_ref[...]
        dhm = lax.dot_general(dp_ref[...], w_ref[...], NT, preferred_element_type=F32)
        r = lax.rsqrt(jnp.mean(x * x, axis=-1, keepdims=True) + EPS)
        xh = x * r
        dshift = jnp.sum(dhm, axis=0, keepdims=True)
        dscale = jnp.sum(dhm * (xh * gg), axis=0, keepdims=True)
        dxh_g = dhm * (1.0 + scale)
        dg = jnp.sum(dxh_g * xh, axis=0, keepdims=True)
        dxh = dxh_g * gg
        res = jnp.where(i < nctx, 0.0, dres_ref[...])
        dx_ref[...] = res + r * (dxh - xh * jnp.mean(dxh * xh, axis=-1, keepdims=True))
        for k, val in enumerate((dshift, dscale, dg)):
            acc_ref[0, k:k + 1, :] += val

    return _call(
        body, name=name, grid=(Tr // tm,),
        in_specs=[_row(tm, P), pl.BlockSpec((tm, D), lambda i: (jnp.maximum(i - nctx, 0), 0)), _row(tm, D),
                  _mod_spec(D, 1, nctx), _resident((1, D)), _resident(w_in.shape)],
        out_specs=[_row(tm, D), _acc_spec(D, nctx)],
        out_shape=[jax.ShapeDtypeStruct((Tr, D), F32), jax.ShapeDtypeStruct((2, 8, D), F32)],
        operands=[dproj, dres, xa, mod6, g, w_in], hosted=hosted,
        params=_params(vmem=VMEM_BIG, sem=("arbitrary",)))


def _adam_update(w, g, m, v):
    c1 = 1.0 - ADAM_B1 ** ADAM_STEP
    c2 = 1.0 - ADAM_B2 ** ADAM_STEP
    m = ADAM_B1 * m + (1.0 - ADAM_B1) * g
    v = ADAM_B2 * v + (1.0 - ADAM_B2) * (g * g)
    return -ADAM_LR * ((m / c1) / (jnp.sqrt(v / c2) + ADAM_EPS) + ADAM_WD * w), m, v


def adamw(w, g, m, v, name, dep=None):
    R, C = w.shape
    tr = _row_tile(R, C)
    deps = [] if dep is None else [dep]

    def body(w_ref, g_ref, m_ref, v_ref, *rest):
        d_ref, nm_ref, nv_ref = rest[len(deps):]
        d_ref[...], nm_ref[...], nv_ref[...] = _adam_update(w_ref[...], g_ref[...], m_ref[...], v_ref[...])

    blk = pl.BlockSpec((tr, C), lambda i: (i, 0))
    return pl.pallas_call(
        body, name=name, grid=(R // tr,),
        in_specs=[blk] * 4 + [ANY] * len(deps), out_specs=[blk] * 3,
        out_shape=[jax.ShapeDtypeStruct((R, C), F32)] * 3,
        compiler_params=_params(vmem=VMEM_BIG, sem=("parallel",)),
    )(w, g, m, v, *deps)


def adamw_summed(recv, w, m, v, name, dep=None):
    R, C = w.shape
    tr = _row_tile(R, C)
    deps = [] if dep is None else [dep]

    def body(r_ref, w_ref, m_ref, v_ref, *rest):
        g_ref, d_ref, nm_ref, nv_ref = rest[len(deps):]
        g = r_ref[0].astype(F32)
        for a in range(1, N_DEV):
            g = g + r_ref[a].astype(F32)
        g_ref[...] = g
        d_ref[...], nm_ref[...], nv_ref[...] = _adam_update(w_ref[...], g, m_ref[...], v_ref[...])

    blk = pl.BlockSpec((tr, C), lambda i: (i, 0))
    return pl.pallas_call(
        body, name=name, grid=(R // tr,),
        in_specs=[pl.BlockSpec((N_DEV, tr, C), lambda i: (0, i, 0)), blk, blk, blk] + [ANY] * len(deps), out_specs=[blk] * 4,
        out_shape=[jax.ShapeDtypeStruct((R, C), F32)] * 4,
        compiler_params=_params(vmem=VMEM_BIG, sem=("parallel",)),
    )(recv, w, m, v, *deps)


def _rope_tables(T, Tc):
    rows = T // GRID_W
    n_freq = HEAD_DIM // 4
    inv = ROPE_THETA ** (-jnp.arange(n_freq, dtype=F32) / n_freq)
    ang_r = jnp.arange(rows).astype(F32)[:, None] * inv
    ang_c = jnp.arange(GRID_W).astype(F32)[:, None] * inv
    per_row = lambda a: jnp.broadcast_to(a[:, None, :], (rows, GRID_W, n_freq)).reshape(T, n_freq)
    per_col = lambda a: jnp.broadcast_to(a[None, :, :], (rows, GRID_W, n_freq)).reshape(T, n_freq)
    cr, sr = per_row(jnp.cos(ang_r)), per_row(jnp.sin(ang_r))
    cc, sc = per_col(jnp.cos(ang_c)), per_col(jnp.sin(ang_c))
    zero = jnp.zeros_like(sr)
    cos = jnp.concatenate([cr, cr, cc, cc], axis=1)
    s_next = jnp.concatenate([-sr, zero, -sc, zero], axis=1)
    s_prev = jnp.concatenate([zero, sr, zero, sc], axis=1)
    lat = jnp.stack([cos, s_next, s_prev])
    ctx = jnp.stack([jnp.ones((Tc, HEAD_DIM), F32), jnp.zeros((Tc, HEAD_DIM), F32), jnp.zeros((Tc, HEAD_DIM), F32)])
    return jnp.concatenate([ctx, lat], axis=1)


BIG = ("ffn1_w_in", "ffn1_w_out", "w_in", "w_branch_conv", "w_branch_attn", "w_out", "ffn2_w_in", "ffn2_w_out")


def _regroup_w_in(stacked, D, Dc, qw, kw):
    w = stacked.transpose(1, 0, 2).reshape(D, -1)
    o = 0
    parts = {}
    for nme, wd in (("bg", Dc), ("cg", Dc), ("vc", Dc), ("q", qw), ("k", kw), ("v", kw), ("gt", 2 * D)):
        parts[nme] = w[:, o:o + wd]
        o += wd
    nb = Dc // 128
    cv = jnp.stack([parts[n].reshape(D, nb, 128) for n in ("bg", "cg", "vc")], axis=2).reshape(D, 3 * Dc)
    return jnp.concatenate([cv, parts["q"], parts["gt"], parts["k"], parts["v"]], axis=1)


def _ungroup_w_in_grad(gt_, D, Dc, qw, kw):
    nb = Dc // 128
    cv = gt_[:3 * Dc].reshape(nb, 3, 128, D)
    o = 3 * Dc
    q = gt_[o:o + qw]
    gt = gt_[o + qw:o + qw + 2 * D]
    k = gt_[o + qw + 2 * D:o + qw + 2 * D + kw]
    v = gt_[o + qw + 2 * D + kw:]
    nat = jnp.concatenate([cv[:, 0].reshape(Dc, D), cv[:, 1].reshape(Dc, D), cv[:, 2].reshape(Dc, D), q, k, v, gt], axis=0)
    return nat.reshape(N_DEV, -1, D)


def kernel(x, c, ctx, c_ctx, w_mod, b_mod, norm1_g, norm2_g, norm3_g, ffn1_w_in, ffn1_w_out, w_in, conv_w, q_norm_g, k_norm_g, w_branch_conv, w_branch_attn, w_out, ffn2_w_in, ffn2_w_out, final_g, loss_target, m_c_ctx, m_w_mod, m_b_mod, m_norm1_g, m_norm2_g, m_norm3_g, m_ffn1_w_in, m_ffn1_w_out, m_w_in, m_conv_w, m_q_norm_g, m_k_norm_g, m_w_branch_conv, m_w_branch_attn, m_w_out, m_ffn2_w_in, m_ffn2_w_out, m_final_g, v_c_ctx, v_w_mod, v_b_mod, v_norm1_g, v_norm2_g, v_norm3_g, v_ffn1_w_in, v_ffn1_w_out, v_w_in, v_conv_w, v_q_norm_g, v_k_norm_g, v_w_branch_conv, v_w_branch_attn, v_w_out, v_ffn2_w_in, v_ffn2_w_out, v_final_g):
    weights = dict(c_ctx=c_ctx, w_mod=w_mod, b_mod=b_mod, norm1_g=norm1_g, norm2_g=norm2_g, norm3_g=norm3_g,
                   ffn1_w_in=ffn1_w_in, ffn1_w_out=ffn1_w_out, w_in=w_in, conv_w=conv_w, q_norm_g=q_norm_g,
                   k_norm_g=k_norm_g, w_branch_conv=w_branch_conv, w_branch_attn=w_branch_attn, w_out=w_out,
                   ffn2_w_in=ffn2_w_in, ffn2_w_out=ffn2_w_out, final_g=final_g)
    moms = dict(c_ctx=(m_c_ctx, v_c_ctx), w_mod=(m_w_mod, v_w_mod), b_mod=(m_b_mod, v_b_mod),
                norm1_g=(m_norm1_g, v_norm1_g), norm2_g=(m_norm2_g, v_norm2_g), norm3_g=(m_norm3_g, v_norm3_g),
                ffn1_w_in=(m_ffn1_w_in, v_ffn1_w_in), ffn1_w_out=(m_ffn1_w_out, v_ffn1_w_out), w_in=(m_w_in, v_w_in),
                conv_w=(m_conv_w, v_conv_w), q_norm_g=(m_q_norm_g, v_q_norm_g), k_norm_g=(m_k_norm_g, v_k_norm_g),
                w_branch_conv=(m_w_branch_conv, v_w_branch_conv), w_branch_attn=(m_w_branch_attn, v_w_branch_attn),
                w_out=(m_w_out, v_w_out), ffn2_w_in=(m_ffn2_w_in, v_ffn2_w_in), ffn2_w_out=(m_ffn2_w_out, v_ffn2_w_out),
                final_g=(m_final_g, v_final_g))
    order = list(weights)

    T, D = x.shape[1], x.shape[2]
    Tc = ctx.shape[1]
    nctx = Tc // ROW_TILE
    nd = N_MOD * D
    Dc = conv_w.shape[2] * N_DEV
    qw, kw = N_Q_HEADS * HEAD_DIM, N_KV_HEADS * HEAD_DIM
    offs, o = {}, 0
    for nme, wd in (("cv", 3 * Dc), ("q", qw), ("gt", 2 * D), ("k", kw), ("v", kw)):
        offs[nme] = o
        o += wd

    ax, ay, ac = lax.axis_index("x"), lax.axis_index("y"), lax.axis_index("c")
    me = 4 * ax + 2 * ay + ac

    shard = {n: (jnp.swapaxes(weights[n][0], 0, 1) if n in ("ffn1_w_in", "ffn2_w_in") else weights[n][0]).astype(BF)
             for n in BIG}
    rows2d = lambda a: a.reshape(-1, a.shape[-1])
    full = {}
    mod_cols = w_mod.shape[2]
    cw_loc = conv_w[0]
    cpad = (-(D + CONV_TAPS * cw_loc.shape[1])) % 128
    pay = jnp.concatenate([c.reshape(1, D), cw_loc.reshape(1, -1), jnp.zeros((1, cpad), F32)], axis=1)
    (g_ffn1_in, g_ffn1_out), (call,) = allgather_two_level([shard["ffn1_w_in"], shard["ffn1_w_out"]], "ag_ffn1",
                                                          riders=[pay])
    full["ffn1_w_in"], full["ffn1_w_out"] = rows2d(g_ffn1_in), rows2d(g_ffn1_out)

    conv_full = call[:, 0, D:D + CONV_TAPS * cw_loc.shape[1]].reshape(N_DEV, CONV_TAPS, -1).transpose(1, 0, 2).reshape(CONV_TAPS, Dc)
    b_loc = lax.dynamic_slice_in_dim(b_mod, me * mod_cols, mod_cols, axis=1)
    cctx2 = c_ctx.reshape(1, D)
    mod_part = mod_forward(call, cctx2, w_mod[0], b_loc, "mod_fwd")
    mod_all = allgather_direct(mod_part, "ag_mod")
    mod_lat = lax.dynamic_index_in_dim(mod_all, me, axis=1, keepdims=False).reshape(nd)
    mod_ctx = mod_all[:, N_DEV, :].reshape(nd)
    mod6 = jnp.stack([mod_ctx, mod_lat]).reshape(6, 3, D)

    tabs = _rope_tables(T, Tc)

    srcs1 = (ctx[0], x[0])
    (xa1, hm1, ab1, h1, f1), (g_w_in,) = ffn_forward(
        srcs1, mod6, 0, norm1_g, full["ffn1_w_in"], full["ffn1_w_out"], nctx, "ffn1_fwd",
        hosted=Hosted(gathers=[shard["w_in"]]))
    full["w_in"] = _regroup_w_in(g_w_in, D, Dc, qw, kw)
    merge_names = ("w_branch_conv", "w_branch_attn", "w_out")
    (hx, proj, qr, kr), g_merge = proj_forward(
        xa1, mod6, norm2_g, full["w_in"], q_norm_g, k_norm_g, tabs, offs, nctx, "proj_fwd",
        hosted=Hosted(gathers=[shard[n] for n in merge_names]))
    full.update({n: rows2d(g) for n, g in zip(merge_names, g_merge)})
    yc = conv_forward(proj, conv_full, offs, Tc, "conv_fwd")
    (oa, lse), (g_ffn2_in, g_ffn2_out) = attention_forward(
        qr, kr, proj, offs, Tc, "attn_fwd", hosted=Hosted(gathers=[shard["ffn2_w_in"], shard["ffn2_w_out"]]))
    full["ffn2_w_in"], full["ffn2_w_out"] = rows2d(g_ffn2_in), rows2d(g_ffn2_out)
    x2, pc, pa, mm, zz = merge_forward(xa1, mod6, yc, oa, proj, full["w_branch_conv"], full["w_branch_attn"],
                                       full["w_out"], offs, Tc, "merge_fwd")
    (dx3, hm2, ab2, h2, f2, lacc), _ = ffn_forward((x2,), mod6, 2, norm3_g, full["ffn2_w_in"], full["ffn2_w_out"], 0, "ffn2_fwd",
                                                   final=(loss_target[0], final_g.reshape(1, D)))

    by_dest = lambda g: g.reshape((N_DEV, -1, g.shape[-1]))
    (dx2, dab2, df2, acc_f2), _ = ffn_backward_rows(dx3, (x2,), mod6, 2, norm3_g, ab2, f2, full["ffn2_w_in"], full["ffn2_w_out"], 0, "ffn2_bwd")
    early = {"ffn2_w_out": tn_matmul(h2, df2, "ffn2_dwout")[0], "ffn2_w_in": tn_matmul(dab2, hm2, "ffn2_dwin")[0]}
    dproj, dgm, dpc, dpa, dyc, do, acc_mg = merge_backward_rows(dx2, mod6, zz, pc, pa, proj, full["w_branch_conv"],
                                                                full["w_branch_attn"], full["w_out"], offs, Tc, "merge_bwd")
    early["w_out"] = tn_matmul(mm, dgm, "dw_out")[0]
    early["w_branch_conv"] = tn_matmul(yc, dpc, "dw_bc")[0]
    early["w_branch_attn"] = tn_matmul(oa, dpa, "dw_ba")[0]
    dproj, dcw = conv_backward(dproj, dyc, proj, conv_full, offs, Tc, "conv_bwd")
    (dproj, dk, dv, dqg), summed = attention_backward(dproj, qr, kr, proj, oa, lse, do, q_norm_g, tabs, offs, Tc, "attn_bwd",
                                                      hosted=Hosted(scatters=[by_dest(g) for g in early.values()]))
    summed = dict(zip(early, summed))
    dproj, dkg = kv_backward(dproj, dk, dv, proj, k_norm_g, tabs, offs, "kv_bwd")
    g_w_in_grad = _ungroup_w_in_grad(tn_matmul(dproj, hx, "dw_in")[0], D, Dc, qw, kw)
    (dxa1, acc_pj), (summed["w_in"],) = proj_backward_rows(dproj, dx2, xa1, mod6, norm2_g, full["w_in"], nctx, "proj_bwd",
                                                          hosted=Hosted(scatters=[g_w_in_grad]))
    (grad_x2d, dab1, df1, acc_f1), _ = ffn_backward_rows(
        dxa1, srcs1, mod6, 0, norm1_g, ab1, f1, full["ffn1_w_in"], full["ffn1_w_out"], nctx, "ffn1_bwd")
    g_ffn1_w_in, (summed["ffn1_w_out"],) = tn_matmul(
        dab1, hm1, "ffn1_dwin", hosted=Hosted(scatters=[by_dest(tn_matmul(h1, df1, "ffn1_dwout")[0])]))
    grad_x = grad_x2d[None]

    zero_d = jnp.zeros((D,), F32)
    dlat = jnp.concatenate([acc_f1[1, 0], acc_f1[1, 1], acc_f1[1, 2], acc_pj[1, 0], acc_pj[1, 1], acc_mg[0],
                            acc_f2[1, 0], acc_f2[1, 1], acc_f2[1, 2]])
    dctx = jnp.concatenate([acc_f1[0, 0], acc_f1[0, 1], acc_f1[0, 2], acc_pj[0, 0], acc_pj[0, 1]] + [zero_d] * 4)
    small = jnp.concatenate([acc_f1[0, 3] + acc_f1[1, 3], acc_pj[0, 2] + acc_pj[1, 2], acc_f2[1, 3],
                             dqg[0], dkg[0], lacc[0], dcw[0:CONV_TAPS].reshape(-1), lacc[1, 0:128]])
    n_small = small.shape[0]
    pay_b = jnp.concatenate([dlat, dctx, small]).reshape(1, -1)
    gath = allgather_direct(pay_b, "ag_small_grads")
    dlat_loc = lax.dynamic_slice_in_dim(gath[:, 0, :nd], me * mod_cols, mod_cols, axis=1)
    dctx_loc = lax.dynamic_slice_in_dim(gath[:, 0, nd:2 * nd], me * mod_cols, mod_cols, axis=1)
    g_wmod, pc_part, small_sum = mod_backward(call, cctx2, w_mod[0], dlat_loc, dctx_loc, gath, 2 * nd, n_small, "mod_bwd")
    pcs = allgather_direct(pc_part, "ag_cctx")
    g_bmod, g_cctx = bmod_and_cctx_grad(gath, pcs, cctx2, nd, "small_bwd")
    sm = small_sum[0]
    loss = sm[n_small - 1]
    conv_off = 4 * D + 2 * HEAD_DIM
    g_conv_full = sm[conv_off:conv_off + CONV_TAPS * Dc].reshape(CONV_TAPS, Dc)
    g_conv = lax.dynamic_slice_in_dim(g_conv_full, me * cw_loc.shape[1], cw_loc.shape[1], axis=1)
    gsmall = dict(
        c_ctx=g_cctx, w_mod=g_wmod, b_mod=g_bmod, norm1_g=sm[0:D][None], norm2_g=sm[D:2 * D][None],
        norm3_g=sm[2 * D:3 * D][None], q_norm_g=sm[3 * D:3 * D + HEAD_DIM][None],
        k_norm_g=sm[3 * D + HEAD_DIM:3 * D + 2 * HEAD_DIM][None],
        final_g=sm[3 * D + 2 * HEAD_DIM:3 * D + 2 * HEAD_DIM + D][None], conv_w=g_conv)

    flipped = ("ffn1_w_in", "w_in", "ffn2_w_in")
    last = "ffn1_w_in"
    results = {}

    def update(n, dep=None):
        w = weights[n]
        shp = w.shape
        if n in flipped:
            two_d = lambda a: jnp.swapaxes(a[0], 0, 1)
            back = lambda a: jnp.swapaxes(a, 0, 1)[None]
        else:
            two_d = lambda a: a.reshape(-1, shp[-1])
            back = lambda a: a.reshape(shp)
        m, v = moms[n]
        if n in summed:
            g2, d, nm, nv = adamw_summed(summed[n], two_d(w), two_d(m), two_d(v), "adamw_" + n, dep=dep)
        else:
            g2 = gsmall[n].reshape(two_d(w).shape)
            d, nm, nv = adamw(two_d(w), g2, two_d(m), two_d(v), "adamw_" + n, dep=dep)
        results[n] = tuple(back(a) for a in (g2, d, nm, nv))
        return d

    last_sems, last_recv_sems, last_src, last_land, token = scatter_start(
        by_dest(g_ffn1_w_in), [g_cctx], "rs_ffn1_start")
    heavy = ("w_mod", "w_in", "ffn2_w_in", "ffn2_w_out", "ffn1_w_out", "w_out")
    deltas = {n: update(n, dep=token) for n in order if n != last}
    after = jnp.concatenate([deltas[n][0:1, 0:1] for n in heavy], axis=1)
    g_done, land_done = scatter_wait(last_sems, last_recv_sems, last_src, last_land, after, "rs_ffn1_wait")
    own = lax.dynamic_index_in_dim(g_done, me, axis=0, keepdims=True)
    summed[last] = lax.dynamic_update_slice_in_dim(land_done, own, me, axis=0)
    update(last)
    cols = list(zip(*(results[n] for n in order)))
    return (loss, grad_x, *cols[0], *cols[1], *cols[2], *cols[3])
```

```python
import math

import jax
import jax.numpy as jnp
from jax import lax
from jax.experimental import pallas as pl
from jax.experimental.pallas import tpu as pltpu

F32 = jnp.float32
BF = jnp.bfloat16
EPS = 1e-6
N_DEV = 8
HEAD_DIM = 128
N_Q_HEADS = 8
N_KV_HEADS = 2
GROUP = N_Q_HEADS // N_KV_HEADS
GRID_W = 64
ROPE_THETA = 10000.0
CONV_TAPS = 3
N_MOD = 9
ADAM_LR = 0.001
ADAM_B1 = 0.9
ADAM_B2 = 0.999
ADAM_EPS = 1e-08
ADAM_WD = 0.01
ADAM_STEP = 10
ROW_TILE = 256
VMEM_BIG = 56 << 20
MESH_ID = pl.DeviceIdType.MESH
HIGHEST = lax.Precision.HIGHEST
NT = (((1,), (1,)), ((), ()))
TN = (((0,), (0,)), ((), ()))
LOG2E = math.log2(math.e)


def _pick(n, cands):
    for c in cands:
        if n % c == 0:
            return c
    return n


def _params(vmem=None, sem=None):
    kw = {}
    if vmem is not None:
        kw["vmem_limit_bytes"] = vmem
    if sem is not None:
        kw["dimension_semantics"] = sem
    return pltpu.CompilerParams(**kw)


def _resident(shape):
    nd = len(shape)
    return pl.BlockSpec(shape, lambda *_: (0,) * nd, pipeline_mode=pl.Buffered(1))


def _sigmoid(x):
    return jax.nn.sigmoid(x)


ANY = pl.BlockSpec(memory_space=pl.ANY)


def _coords():
    return lax.axis_index("x"), lax.axis_index("y"), lax.axis_index("c")


def _flip(v, bit):
    return 1 - v if bit else v


def _remote(src, dst, ssem, rsem, dev):
    return pltpu.make_async_remote_copy(src_ref=src, dst_ref=dst, send_sem=ssem, recv_sem=rsem,
                                        device_id=dev, device_id_type=MESH_ID)


def allgather_direct(v, name):
    def body(v_ref, out_ref, ssem, rsem, lsem):
        x, y, c = _coords()
        me = 4 * x + 2 * y + c
        mine = pltpu.make_async_copy(v_ref, out_ref.at[me], lsem)
        mine.start()
        cps = []
        for p in range(1, N_DEV):
            px, py, pc = (p >> 2) & 1, (p >> 1) & 1, p & 1
            cps.append(_remote(v_ref, out_ref.at[me], ssem.at[p - 1], rsem.at[p - 1],
                               (_flip(x, px), _flip(y, py), _flip(c, pc))))
        for cp in cps:
            cp.start()
        for p in range(1, N_DEV):
            px, py, pc = (p >> 2) & 1, (p >> 1) & 1, p & 1
            src = 4 * _flip(x, px) + 2 * _flip(y, py) + _flip(c, pc)
            _remote(v_ref, out_ref.at[src], ssem.at[p - 1], rsem.at[p - 1], (x, y, c)).wait_recv()
        for cp in cps:
            cp.wait_send()
        mine.wait()

    return pl.pallas_call(
        body, name=name,
        out_shape=jax.ShapeDtypeStruct((N_DEV,) + v.shape, v.dtype),
        in_specs=[ANY], out_specs=ANY,
        scratch_shapes=[pltpu.SemaphoreType.DMA((N_DEV - 1,)), pltpu.SemaphoreType.DMA((N_DEV - 1,)),
                        pltpu.SemaphoreType.DMA],
    )(v)


def allgather_two_level(shards, name, riders=()):
    n = len(shards)
    ride = Hosted(gathers=riders)
    r = ride.n

    def body(*refs):
        v_refs, rin, out_refs, rout = refs[:n], refs[n:n + r], refs[n + r:2 * n + r], refs[2 * n + r:2 * n + 2 * r]
        (ssem, rsem, lsem), rsems = refs[2 * n + 2 * r:2 * n + 2 * r + 3], refs[2 * n + 2 * r + 3:]
        x, y, c = _coords()
        me = (x, y, c)
        sib = (x, y, 1 - c)
        chips = [(1 - x, y), (x, 1 - y), (1 - x, 1 - y)]
        if r:
            ride.start(rin, rout, *rsems)

        def slot(w, px, py, pc):
            return out_refs[w].at[4 * px + 2 * py + pc]

        def sem(w, k):
            return ssem.at[7 * w + k], rsem.at[7 * w + k]

        mine = [pltpu.make_async_copy(v_refs[w], slot(w, *me), lsem.at[w]) for w in range(n)]
        for cp in mine:
            cp.start()
        first = []
        for w in range(n):
            first.append(_remote(v_refs[w], slot(w, *me), *sem(w, 0), sib))
            first += [_remote(v_refs[w], slot(w, *me), *sem(w, 1 + j), (*chip, c)) for j, chip in enumerate(chips)]
        for cp in first:
            cp.start()
        passed = []
        for w in range(n):
            for j, chip in enumerate(chips):
                _remote(v_refs[w], slot(w, *chip, c), *sem(w, 1 + j), me).wait_recv()
                cp = _remote(slot(w, *chip, c), slot(w, *chip, c), *sem(w, 4 + j), sib)
                cp.start()
                passed.append(cp)
        for w in range(n):
            _remote(v_refs[w], slot(w, x, y, 1 - c), *sem(w, 0), me).wait_recv()
            for j, chip in enumerate(chips):
                _remote(v_refs[w], slot(w, *chip, 1 - c), *sem(w, 4 + j), me).wait_recv()
        for cp in first + passed:
            cp.wait_send()
        for cp in mine:
            cp.wait()
        if r:
            ride.wait(rin, rout, *rsems)

    res = pl.pallas_call(
        body, name=name,
        out_shape=[jax.ShapeDtypeStruct((N_DEV,) + s.shape, s.dtype) for s in shards] + ride.out_shapes,
        in_specs=[ANY] * (n + r), out_specs=[ANY] * (n + r),
        scratch_shapes=[pltpu.SemaphoreType.DMA((7 * n,)), pltpu.SemaphoreType.DMA((7 * n,)),
                        pltpu.SemaphoreType.DMA((n,))] + (ride.scratch if r else []),
    )(*shards, *riders)
    return list(res[:n]), list(res[n:])


SEM = pl.BlockSpec(memory_space=pltpu.SEMAPHORE)
IN_HBM = pl.BlockSpec(memory_space=pltpu.HBM)
DATAFLOW = pltpu.SideEffectType.DATAFLOW_SIDE_EFFECTING


def _scatter_descriptors(src_ref, land_ref, ssem, rsem, arrivals):
    x, y, c = _coords()
    me = 4 * x + 2 * y + c
    cps = []
    for p in range(1, N_DEV):
        px, py, pc = _flip(x, (p >> 2) & 1), _flip(y, (p >> 1) & 1), _flip(c, p & 1)
        peer = 4 * px + 2 * py + pc
        if arrivals:
            cps.append(_remote(src_ref.at[me], land_ref.at[peer], ssem.at[p - 1], rsem.at[p - 1], (x, y, c)))
        else:
            cps.append(_remote(src_ref.at[peer], land_ref.at[me], ssem.at[p - 1], rsem.at[p - 1], (px, py, pc)))
    return cps


def scatter_start(g, before, name):
    nb = len(before)

    def body(g_ref, land_ref, *rest):
        ssem, rsem, g_thru, land_thru, token = rest[nb:]
        for cp in _scatter_descriptors(g_ref, land_ref, ssem, rsem, False):
            cp.start()
        token[...] = jnp.zeros_like(token)

    return pl.pallas_call(
        body, name=name,
        out_shape=(pltpu.SemaphoreType.DMA((N_DEV - 1,)), pltpu.SemaphoreType.DMA((N_DEV - 1,)),
                   pltpu.HBM(g.shape, g.dtype), pltpu.HBM(g.shape, g.dtype), jax.ShapeDtypeStruct((8, 128), F32)),
        in_specs=(IN_HBM, IN_HBM) + (ANY,) * nb, out_specs=(SEM, SEM, IN_HBM, IN_HBM, pl.BlockSpec(memory_space=pltpu.VMEM)),
        input_output_aliases={0: 2, 1: 3},
        compiler_params=pltpu.CompilerParams(has_side_effects=DATAFLOW),
    )(pltpu.with_memory_space_constraint(g, pltpu.HBM),
      pltpu.with_memory_space_constraint(lax.empty(g.shape, g.dtype), pltpu.HBM), *before)


def scatter_wait(ssem, rsem, g_thru, land_thru, after, name):
    def body(g_ref, land_ref, ssem_ref, rsem_ref, after_ref, g_out, land_out):
        cps = _scatter_descriptors(g_ref, land_ref, ssem_ref, rsem_ref, True)
        for cp in cps:
            cp.wait_send()
        for cp in cps:
            cp.wait_recv()

    return pl.pallas_call(
        body, name=name,
        out_shape=(pltpu.HBM(g_thru.shape, g_thru.dtype), pltpu.HBM(land_thru.shape, land_thru.dtype)),
        in_specs=(IN_HBM, IN_HBM, SEM, SEM, ANY), out_specs=(IN_HBM, IN_HBM),
        input_output_aliases={0: 0, 1: 1},
        compiler_params=pltpu.CompilerParams(has_side_effects=DATAFLOW),
    )(g_thru, land_thru, ssem, rsem, after)


class Hosted:
    def __init__(self, gathers=(), scatters=()):
        self.items = [(a, False) for a in gathers] + [(a, True) for a in scatters]
        self.n = len(self.items)
        self.operands = [a for a, _ in self.items]
        self.out_shapes = [jax.ShapeDtypeStruct(a.shape if sc else (N_DEV,) + a.shape, a.dtype) for a, sc in self.items]
        self.scratch = [pltpu.SemaphoreType.DMA((7 * self.n,)), pltpu.SemaphoreType.DMA((7 * self.n,)),
                        pltpu.SemaphoreType.DMA((self.n,))]

    def _copies(self, in_refs, out_refs, ssem, rsem, lsem, arrivals):
        x, y, c = _coords()
        me = 4 * x + 2 * y + c
        remote, local = [], []
        for w, (_, sc) in enumerate(self.items):
            src, dst = in_refs[w], out_refs[w]
            local.append(pltpu.make_async_copy(src.at[me] if sc else src, dst.at[me], lsem.at[w]))
            for p in range(1, N_DEV):
                px, py, pc = _flip(x, (p >> 2) & 1), _flip(y, (p >> 1) & 1), _flip(c, p & 1)
                peer = 4 * px + 2 * py + pc
                k = 7 * w + p - 1
                if arrivals:
                    remote.append(_remote(src.at[me] if sc else src, dst.at[peer], ssem.at[k], rsem.at[k], (x, y, c)))
                else:
                    remote.append(_remote(src.at[peer] if sc else src, dst.at[me], ssem.at[k], rsem.at[k], (px, py, pc)))
        return remote, local

    def start(self, in_refs, out_refs, ssem, rsem, lsem):
        sends, local = self._copies(in_refs, out_refs, ssem, rsem, lsem, False)
        for cp in local + sends:
            cp.start()

    def wait(self, in_refs, out_refs, ssem, rsem, lsem):
        arrivals, local = self._copies(in_refs, out_refs, ssem, rsem, lsem, True)
        for cp in arrivals:
            cp.wait_recv()
        for cp in arrivals:
            cp.wait_send()
        for cp in local:
            cp.wait()


def _call(body, *, name, grid, in_specs, out_specs, out_shape, operands, params, scratch_shapes=(), aliases=None, hosted=None):
    n_in, n_out, n_scr = len(in_specs), len(out_specs), len(scratch_shapes)
    h = hosted.n if hosted is not None else 0

    def wrapped(*refs):
        ins, cins = refs[:n_in], refs[n_in:n_in + h]
        outs, couts = refs[n_in + h:n_in + h + n_out], refs[n_in + h + n_out:n_in + 2 * h + n_out]
        rest = refs[n_in + 2 * h + n_out:]
        scr, sems = rest[:n_scr], rest[n_scr:]
        if h:
            ids = [pl.program_id(a) for a in range(len(grid))]
            first, last = ids[0] == 0, ids[0] == grid[0] - 1
            for a in range(1, len(grid)):
                first, last = first & (ids[a] == 0), last & (ids[a] == grid[a] - 1)

            @pl.when(first)
            def _():
                hosted.start(cins, couts, *sems)

        body(*ins, *outs, *scr)
        if h:
            @pl.when(last)
            def _():
                hosted.wait(cins, couts, *sems)

    res = pl.pallas_call(
        wrapped, name=name, grid=grid,
        in_specs=list(in_specs) + [ANY] * h, out_specs=list(out_specs) + [ANY] * h,
        out_shape=list(out_shape) + (hosted.out_shapes if h else []),
        scratch_shapes=list(scratch_shapes) + (hosted.scratch if h else []),
        input_output_aliases=aliases or {}, compiler_params=params,
    )(*operands, *(hosted.operands if h else []))
    return list(res[:n_out]), list(res[n_out:])


def _row_tile(R, C):
    if R * C <= (1 << 18):
        return R
    return max((d for d in range(8, 257, 8) if R % d == 0), default=R)


def _cond_rows(call_ref, cctx_ref, z_ref, D):
    z_ref[...] = jnp.zeros_like(z_ref)
    for a in range(N_DEV):
        z_ref[a:a + 1, :] = call_ref[a][:, :D]
    z_ref[N_DEV:N_DEV + 1, :] = cctx_ref[...]


def mod_forward(call, c_ctx, w_loc, b_loc, name):
    D, cols = w_loc.shape

    def body(call_ref, cctx_ref, w_ref, b_ref, o_ref, z_ref):
        _cond_rows(call_ref, cctx_ref, z_ref, D)
        z = z_ref[...]
        s = z * _sigmoid(z)
        o_ref[...] = jnp.dot(s, w_ref[...], preferred_element_type=F32, precision=HIGHEST) + b_ref[...]

    return pl.pallas_call(
        body, name=name, out_shape=jax.ShapeDtypeStruct((16, cols), F32),
        scratch_shapes=[pltpu.VMEM((16, D), F32)],
        compiler_params=_params(vmem=VMEM_BIG),
    )(call, c_ctx, w_loc, b_loc)


def mod_backward(call, c_ctx, w_loc, dlat_loc, dctx_loc, gath, n_small_off, n_small, name):
    D, cols = w_loc.shape

    def body(call_ref, cctx_ref, w_ref, dlat_ref, dctx_ref, g_ref, gw_ref, pc_ref, small_ref, z_ref, dm_ref):
        _cond_rows(call_ref, cctx_ref, z_ref, D)
        z = z_ref[...]
        s = z * _sigmoid(z)
        dctx = dctx_ref[0:1, :]
        for a in range(1, N_DEV):
            dctx = dctx + dctx_ref[a:a + 1, :]
        dm_ref[...] = jnp.zeros_like(dm_ref)
        dm_ref[0:N_DEV, :] = dlat_ref[...]
        dm_ref[N_DEV:N_DEV + 1, :] = dctx
        gw_ref[...] = lax.dot_general(s, dm_ref[...], TN, preferred_element_type=F32, precision=HIGHEST)
        pc_ref[...] = lax.dot_general(dctx, w_ref[...], NT, preferred_element_type=F32, precision=HIGHEST)
        acc = g_ref[0][:, n_small_off:n_small_off + n_small]
        for a in range(1, N_DEV):
            acc = acc + g_ref[a][:, n_small_off:n_small_off + n_small]
        small_ref[...] = acc

    return pl.pallas_call(
        body, name=name,
        out_shape=(jax.ShapeDtypeStruct((D, cols), F32), jax.ShapeDtypeStruct((1, D), F32),
                   jax.ShapeDtypeStruct((1, n_small), F32)),
        scratch_shapes=[pltpu.VMEM((16, D), F32), pltpu.VMEM((16, cols), F32)],
        compiler_params=_params(vmem=VMEM_BIG),
    )(call, c_ctx, w_loc, dlat_loc, dctx_loc, gath)


def bmod_and_cctx_grad(gath, pcs, c_ctx, nd, name):
    D = c_ctx.shape[-1]

    def body(g_ref, pc_ref, cctx_ref, gb_ref, gc_ref):
        acc = g_ref[0][:, :nd] + g_ref[0][:, nd:2 * nd]
        for a in range(1, N_DEV):
            acc = acc + (g_ref[a][:, :nd] + g_ref[a][:, nd:2 * nd])
        gb_ref[...] = acc
        p = pc_ref[0]
        for a in range(1, N_DEV):
            p = p + pc_ref[a]
        z = cctx_ref[...]
        sg = _sigmoid(z)
        gc_ref[...] = p * (sg * (1.0 + z * (1.0 - sg)))

    return pl.pallas_call(
        body, name=name,
        out_shape=(jax.ShapeDtypeStruct((1, nd), F32), jax.ShapeDtypeStruct((1, D), F32)),
    )(gath, pcs, c_ctx)


def _mod_spec(D, which, nctx):
    return pl.BlockSpec((1, 3, D), lambda i: (jnp.where(i < nctx, 0, 3) + which, 0, 0))


def _acc_spec(D, nctx):
    return pl.BlockSpec((1, 8, D), lambda i: (jnp.where(i < nctx, 0, 1), 0, 0))


def _row(tm, n):
    return pl.BlockSpec((tm, n), lambda i: (i, 0))


def _two_stream_specs(tm, D, nctx):
    return [pl.BlockSpec((tm, D), lambda i: (jnp.minimum(i, nctx - 1), 0)),
            pl.BlockSpec((tm, D), lambda i: (jnp.maximum(i - nctx, 0), 0))]


def _final_norm_loss_backward(x, tgt, gg, i, dx_ref, acc_ref):
    @pl.when(i == 0)
    def _():
        acc_ref[...] = jnp.zeros_like(acc_ref)

    D = x.shape[1]
    r = lax.rsqrt(jnp.mean(x * x, axis=-1, keepdims=True) + EPS)
    xh = x * r
    e = xh * gg - tgt
    part = 0.5 * jnp.sum(jnp.mean(e * e, axis=-1, keepdims=True), axis=0, keepdims=True)
    dy = e * (1.0 / D)
    dyg = dy * gg
    dx_ref[...] = r * (dyg - xh * jnp.mean(dyg * xh, axis=-1, keepdims=True))
    acc_ref[0:1, :] += jnp.sum(dy * xh, axis=0, keepdims=True)
    acc_ref[1:2, :] += jnp.broadcast_to(part, (1, D))


def _hidden_chunks(F):
    step = 1024 if F % 256 == 0 else F
    return [(lo, min(lo + step, F)) for lo in range(0, F, step)]


def ffn_forward(srcs, mod6, which, g, wt, w_out, nctx, name, hosted=None, final=None):
    D = srcs[-1].shape[1]
    Tr = sum(s.shape[0] for s in srcs)
    F = wt.shape[0] // 2
    tm = ROW_TILE
    two = len(srcs) == 2
    nfin = 0 if final is None else 2

    def body(*refs):
        x_refs, fin_refs, rest = refs[:len(srcs)], refs[len(srcs):len(srcs) + nfin], refs[len(srcs) + nfin:]
        mod_ref, g_ref, wt_ref, wout_ref, xo_ref, hm_ref, ab_ref, h_ref, f_ref = rest[:9]
        x = jnp.where(pl.program_id(0) < nctx, x_refs[0][...], x_refs[1][...]) if two else x_refs[0][...]
        ms = mod_ref[0]
        shift, scale, gate = ms[0:1], ms[1:2], ms[2:3]
        r = lax.rsqrt(jnp.mean(x * x, axis=-1, keepdims=True) + EPS)
        hb = (((x * r) * g_ref[...]) * (1.0 + scale) + shift).astype(BF)
        hm_ref[...] = hb
        f = jnp.zeros((tm, D), F32)
        for lo, hi in _hidden_chunks(F):
            a = lax.dot_general(hb, wt_ref[lo:hi, :], NT, preferred_element_type=F32)
            b = lax.dot_general(hb, wt_ref[F + lo:F + hi, :], NT, preferred_element_type=F32)
            ab_ref[:, lo:hi] = a.astype(BF)
            ab_ref[:, F + lo:F + hi] = b.astype(BF)
            h = ((a * _sigmoid(a)) * b).astype(BF)
            h_ref[:, lo:hi] = h
            f = f + jnp.dot(h, wout_ref[lo:hi, :], preferred_element_type=F32)
        f_ref[...] = f.astype(BF)
        xo = x + (0.5 * gate) * f
        if final is None:
            xo_ref[...] = xo
        else:
            _final_norm_loss_backward(xo, fin_refs[0][...], fin_refs[1][...], pl.program_id(0), xo_ref, rest[9])

    src_specs = _two_stream_specs(tm, D, nctx) if two else [_row(tm, D)]
    fin = final is not None
    return _call(
        body, name=name, grid=(Tr // tm,),
        in_specs=src_specs + ([_row(tm, D), _resident((1, D))] if fin else [])
                 + [_mod_spec(D, which, nctx), _resident((1, D)), _resident(wt.shape), _resident(w_out.shape)],
        out_specs=[_row(tm, D), _row(tm, D), _row(tm, 2 * F), _row(tm, F), _row(tm, D)]
                  + ([pl.BlockSpec((8, D), lambda i: (0, 0))] if fin else []),
        out_shape=[jax.ShapeDtypeStruct((Tr, D), F32), jax.ShapeDtypeStruct((Tr, D), BF),
                   jax.ShapeDtypeStruct((Tr, 2 * F), BF), jax.ShapeDtypeStruct((Tr, F), BF),
                   jax.ShapeDtypeStruct((Tr, D), BF)] + ([jax.ShapeDtypeStruct((8, D), F32)] if fin else []),
        operands=[*srcs, *(final or ()), mod6, g, wt, w_out], hosted=hosted,
        params=_params(vmem=VMEM_BIG, sem=("arbitrary",)))


def ffn_backward_rows(dxo, srcs, mod6, which, g, ab, fo, wt, w_out, nctx, name, hosted=None):
    D = srcs[-1].shape[1]
    Tr = sum(s.shape[0] for s in srcs)
    Tl = srcs[-1].shape[0]
    F = wt.shape[0] // 2
    tm = ROW_TILE
    two = len(srcs) == 2

    def body(*refs):
        dxo_ref, x_refs = refs[0], refs[1:1 + len(srcs)]
        mod_ref, g_ref, ab_ref, fo_ref, wt_ref, wout_ref, dx_ref, dab_ref, df_ref, acc_ref = refs[1 + len(srcs):]
        i = pl.program_id(0)

        @pl.when((i == 0) | (i == nctx))
        def _():
            acc_ref[...] = jnp.zeros_like(acc_ref)

        dxo = dxo_ref[...]
        x = jnp.where(i < nctx, x_refs[0][...], x_refs[1][...]) if two else x_refs[0][...]
        ms = mod_ref[0]
        scale, gate = ms[1:2], ms[2:3]
        gg = g_ref[...]
        dgate = jnp.sum(dxo * fo_ref[...].astype(F32), axis=0, keepdims=True) * 0.5
        dfb = (dxo * (0.5 * gate)).astype(BF)
        df_ref[...] = dfb
        dhm = jnp.zeros((tm, D), F32)
        for lo, hi in _hidden_chunks(F):
            dh = lax.dot_general(dfb, wout_ref[lo:hi, :], NT, preferred_element_type=F32)
            a = ab_ref[:, lo:hi].astype(F32)
            b = ab_ref[:, F + lo:F + hi].astype(F32)
            sg = _sigmoid(a)
            da = ((dh * b) * (sg * (1.0 + a * (1.0 - sg)))).astype(BF)
            db = (dh * (a * sg)).astype(BF)
            dab_ref[:, lo:hi] = da
            dab_ref[:, F + lo:F + hi] = db
            dhm = dhm + jnp.dot(da, wt_ref[lo:hi, :], preferred_element_type=F32)
            dhm = dhm + jnp.dot(db, wt_ref[F + lo:F + hi, :], preferred_element_type=F32)
        r = lax.rsqrt(jnp.mean(x * x, axis=-1, keepdims=True) + EPS)
        xh = x * r
        dshift = jnp.sum(dhm, axis=0, keepdims=True)
        dscale = jnp.sum(dhm * (xh * gg), axis=0, keepdims=True)
        dxh_g = dhm * (1.0 + scale)
        dg = jnp.sum(dxh_g * xh, axis=0, keepdims=True)
        dxh = dxh_g * gg
        dx_ref[...] = dxo + r * (dxh - xh * jnp.mean(dxh * xh, axis=-1, keepdims=True))
        for k, val in enumerate((dshift, dscale, dgate, dg)):
            acc_ref[0, k:k + 1, :] += val

    src_specs = _two_stream_specs(tm, D, nctx) if two else [_row(tm, D)]
    dx_spec = pl.BlockSpec((tm, D), lambda i: (jnp.maximum(i - nctx, 0), 0))
    return _call(
        body, name=name, grid=(Tr // tm,),
        in_specs=[_row(tm, D)] + src_specs + [_mod_spec(D, which, nctx), _resident((1, D)), _row(tm, 2 * F), _row(tm, D),
                                              _resident(wt.shape), _resident(w_out.shape)],
        out_specs=[dx_spec, _row(tm, 2 * F), _row(tm, D), _acc_spec(D, nctx)],
        out_shape=[jax.ShapeDtypeStruct((Tl, D), F32), jax.ShapeDtypeStruct((Tr, 2 * F), BF),
                   jax.ShapeDtypeStruct((Tr, D), BF), jax.ShapeDtypeStruct((2, 8, D), F32)],
        operands=[dxo, *srcs, mod6, g, ab, fo, wt, w_out], hosted=hosted,
        params=_params(vmem=VMEM_BIG, sem=("arbitrary",)))


def _token_tile(T):
    return _pick(T, (2048, 1408, 1024, 768, 512, 256, 128))


def tn_matmul(a, b, name, hosted=None):
    T, K = a.shape
    N = b.shape[1]
    tk = _pick(K, (1024, 1408, 1664, 768, 512, 384, 256, 128))
    tn = _pick(N, (1024, 1408, 1664, 768, 512, 384, 256, 128))
    tt = _token_tile(T)
    nt = T // tt

    def body(a_ref, b_ref, o_ref, acc_ref):
        t = pl.program_id(2)

        @pl.when(t == 0)
        def _():
            acc_ref[...] = jnp.zeros_like(acc_ref)

        acc_ref[...] += lax.dot_general(a_ref[...], b_ref[...], TN, preferred_element_type=F32)

        @pl.when(t == nt - 1)
        def _():
            o_ref[...] = acc_ref[...].astype(BF)

    (out,), exchanged = _call(
        body, name=name, grid=(K // tk, N // tn, nt),
        in_specs=[pl.BlockSpec((tt, tk), lambda k, n, t: (t, k)), pl.BlockSpec((tt, tn), lambda k, n, t: (t, n))],
        out_specs=[pl.BlockSpec((tk, tn), lambda k, n, t: (k, n))],
        out_shape=[jax.ShapeDtypeStruct((K, N), BF)],
        scratch_shapes=[pltpu.VMEM((tk, tn), F32)],
        operands=[a, b], hosted=hosted,
        params=_params(vmem=VMEM_BIG, sem=("arbitrary", "arbitrary", "arbitrary")))
    return out, exchanged


def _rope_apply(y, cos, s_next, s_prev):
    return y * cos + pltpu.roll(y, HEAD_DIM - 32, 1) * s_next + pltpu.roll(y, 32, 1) * s_prev


def _rope_transpose(dz, cos, s_next, s_prev):
    return dz * cos + pltpu.roll(dz * s_next, 32, 1) + pltpu.roll(dz * s_prev, HEAD_DIM - 32, 1)


def proj_forward(xa, mod6, g, w_in, qg, kg, tabs, offs, nctx, name, hosted=None):
    Tr, D = xa.shape
    P = w_in.shape[1]
    tm = ROW_TILE
    qo, ko = offs["q"], offs["k"]
    qw, kw = N_Q_HEADS * HEAD_DIM, N_KV_HEADS * HEAD_DIM
    scale_q = HEAD_DIM ** -0.5 * LOG2E

    def body(x_ref, mod_ref, g_ref, w_ref, qg_ref, kg_ref, tab_ref, hx_ref, pr_ref, q_ref, k_ref):
        x = x_ref[...]
        ms = mod_ref[0]
        shift, scale = ms[0:1], ms[1:2]
        r = lax.rsqrt(jnp.mean(x * x, axis=-1, keepdims=True) + EPS)
        hb = (((x * r) * g_ref[...]) * (1.0 + scale) + shift).astype(BF)
        hx_ref[...] = hb
        pr = jnp.dot(hb, w_ref[...], preferred_element_type=F32)
        pr_ref[...] = pr.astype(BF)
        cos, s_next, s_prev = tab_ref[0], tab_ref[1], tab_ref[2]

        def head(v, gain):
            n = v * lax.rsqrt(jnp.mean(v * v, axis=-1, keepdims=True) + EPS)
            return _rope_apply(n * gain, cos, s_next, s_prev)

        for h in range(N_Q_HEADS):
            lo = qo + h * HEAD_DIM
            q_ref[:, h * HEAD_DIM:(h + 1) * HEAD_DIM] = (head(pr[:, lo:lo + HEAD_DIM], qg_ref[...]) * scale_q).astype(BF)
        for h in range(N_KV_HEADS):
            lo = ko + h * HEAD_DIM
            k_ref[:, h * HEAD_DIM:(h + 1) * HEAD_DIM] = head(pr[:, lo:lo + HEAD_DIM], kg_ref[...]).astype(BF)

    return _call(
        body, name=name, grid=(Tr // tm,),
        in_specs=[_row(tm, D), _mod_spec(D, 1, nctx), _resident((1, D)), _resident(w_in.shape),
                  _resident((1, HEAD_DIM)), _resident((1, HEAD_DIM)),
                  pl.BlockSpec((3, tm, HEAD_DIM), lambda i: (0, i, 0))],
        out_specs=[_row(tm, D), _row(tm, P), _row(tm, qw), _row(tm, kw)],
        out_shape=[jax.ShapeDtypeStruct((Tr, D), BF), jax.ShapeDtypeStruct((Tr, P), BF),
                   jax.ShapeDtypeStruct((Tr, qw), BF), jax.ShapeDtypeStruct((Tr, kw), BF)],
        operands=[xa, mod6, g, w_in, qg, kg, tabs], hosted=hosted,
        params=_params(vmem=VMEM_BIG, sem=("arbitrary",)))


def _shifted(u, first_row, last_row):
    T = u.shape[0]
    prev = jnp.where(first_row, 0.0, pltpu.roll(u, 1, 0))
    nxt = jnp.where(last_row, 0.0, pltpu.roll(u, T - 1, 0))
    return prev, nxt


def conv_forward(proj, conv_w, offs, Tc, name):
    Ta = proj.shape[0]
    T = Ta - Tc
    Dc = conv_w.shape[1]
    cb = offs["cv"] // 384

    def body(p_ref, w_ref, y_ref):
        rows = lax.broadcasted_iota(jnp.int32, (T, 128), 0)
        u = p_ref[pl.ds(Tc, T), 128:256].astype(F32) * p_ref[pl.ds(Tc, T), 256:384].astype(F32)
        prev, nxt = _shifted(u, rows == 0, rows == T - 1)
        w = w_ref[...]
        cv = prev * w[0:1] + u * w[1:2] + nxt * w[2:3]
        y_ref[...] = (p_ref[pl.ds(Tc, T), 0:128].astype(F32) * cv).astype(BF)

    return pl.pallas_call(
        body, name=name, grid=(Dc // 128,),
        in_specs=[pl.BlockSpec((Ta, 384), lambda j: (0, cb + j)), pl.BlockSpec((CONV_TAPS, 128), lambda j: (0, j))],
        out_specs=pl.BlockSpec((T, 128), lambda j: (0, j)),
        out_shape=jax.ShapeDtypeStruct((T, Dc), BF),
        compiler_params=_params(vmem=VMEM_BIG, sem=("arbitrary",)),
    )(proj, conv_w)


def conv_backward(dproj, dy, proj, conv_w, offs, Tc, name):
    Ta = proj.shape[0]
    T = Ta - Tc
    Dc = conv_w.shape[1]
    cb = offs["cv"] // 384

    def body(dp_any, dy_ref, p_ref, w_ref, o_ref, dw_ref):
        rows = lax.broadcasted_iota(jnp.int32, (T, 128), 0)
        first, last = rows == 0, rows == T - 1
        bg = p_ref[pl.ds(Tc, T), 0:128].astype(F32)
        cg = p_ref[pl.ds(Tc, T), 128:256].astype(F32)
        vc = p_ref[pl.ds(Tc, T), 256:384].astype(F32)
        dy = dy_ref[...].astype(F32)
        u = cg * vc
        prev, nxt = _shifted(u, first, last)
        w = w_ref[...]
        cv = prev * w[0:1] + u * w[1:2] + nxt * w[2:3]
        o_ref[pl.ds(0, Tc), :] = jnp.zeros((Tc, 384), BF)
        o_ref[pl.ds(Tc, T), 0:128] = (dy * cv).astype(BF)
        dcv = dy * bg
        dprev, dnxt = _shifted(dcv, first, last)
        du = dnxt * w[0:1] + dcv * w[1:2] + dprev * w[2:3]
        o_ref[pl.ds(Tc, T), 128:256] = (du * vc).astype(BF)
        o_ref[pl.ds(Tc, T), 256:384] = (du * cg).astype(BF)
        dw_ref[...] = jnp.zeros_like(dw_ref)
        for k, tap in enumerate((prev, u, nxt)):
            dw_ref[k:k + 1, :] = jnp.sum(dcv * tap, axis=0, keepdims=True)

    blk = pl.BlockSpec((Ta, 384), lambda j: (0, cb + j))
    return pl.pallas_call(
        body, name=name, grid=(Dc // 128,),
        in_specs=[ANY, pl.BlockSpec((T, 128), lambda j: (0, j)), blk, pl.BlockSpec((CONV_TAPS, 128), lambda j: (0, j))],
        out_specs=[blk, pl.BlockSpec((8, 128), lambda j: (0, j))],
        out_shape=[jax.ShapeDtypeStruct(dproj.shape, BF), jax.ShapeDtypeStruct((8, Dc), F32)],
        input_output_aliases={0: 0},
        compiler_params=_params(vmem=VMEM_BIG, sem=("arbitrary",)),
    )(dproj, dy, proj, conv_w)


def _kv_chunk(Ta):
    return _pick(Ta, (768, 512, 384, 256, 128))


def _stack_heads(v):
    return jnp.concatenate([v[:, h * HEAD_DIM:(h + 1) * HEAD_DIM] for h in range(GROUP)], axis=0)


def attention_forward(q, k, proj, offs, Tc, name, hosted=None):
    Ta = k.shape[0]
    T = Ta - Tc
    tq = ROW_TILE
    kc = _kv_chunk(Ta)
    nkv = Ta // kc
    gw = GROUP * HEAD_DIM
    vblk = offs["v"] // HEAD_DIM
    qoff = Tc // tq
    n = GROUP * tq

    def body(q_ref, k_ref, v_ref, o_ref, lse_ref, vx_ref, qs_ref, s0_ref, s1_ref, m_ref, acc_ref):
        @pl.when(pl.program_id(1) == 0)
        def _():
            vx_ref[:, 0:HEAD_DIM] = v_ref[...]
            vx_ref[:, HEAD_DIM:2 * HEAD_DIM] = jnp.ones((Ta, HEAD_DIM), BF)

        qs_ref[...] = _stack_heads(q_ref[...])
        m_ref[...] = jnp.full((n, 1), -1e30, F32)
        acc_ref[...] = jnp.zeros((n, 2 * HEAD_DIM), F32)

        def rows(c):
            return pl.ds(pl.multiple_of(c * kc, kc), kc)

        def logits(c, dst):
            dst[...] = lax.dot_general(qs_ref[...], k_ref[rows(c), :], NT, preferred_element_type=F32)

        def consume(src, c):
            s = src[...]
            m_prev = m_ref[...]
            m_new = jnp.maximum(m_prev, jnp.max(s, axis=-1, keepdims=True))
            p = jnp.exp2(s - m_new).astype(BF)
            acc_ref[...] = jnp.exp2(m_prev - m_new) * acc_ref[...] + jnp.dot(p, vx_ref[rows(c), :], preferred_element_type=F32)
            m_ref[...] = m_new

        def pair(i, carry):
            logits(2 * i + 1, s1_ref)
            consume(s0_ref, 2 * i)
            logits(2 * i + 2, s0_ref)
            consume(s1_ref, 2 * i + 1)
            return carry

        logits(0, s0_ref)
        if nkv % 2:
            lax.fori_loop(0, nkv // 2, pair, 0)
            consume(s0_ref, nkv - 1)
        else:
            lax.fori_loop(0, nkv // 2 - 1, pair, 0)
            logits(nkv - 1, s1_ref)
            consume(s0_ref, nkv - 2)
            consume(s1_ref, nkv - 1)
        acc = acc_ref[...]
        l = acc[:, HEAD_DIM:HEAD_DIM + 1]
        o = acc[:, 0:HEAD_DIM] / l
        lse = m_ref[...] + jnp.log2(l)
        for h in range(GROUP):
            o_ref[:, h * HEAD_DIM:(h + 1) * HEAD_DIM] = o[h * tq:(h + 1) * tq].astype(BF)
            lse_ref[0, :, h:h + 1] = lse[h * tq:(h + 1) * tq]

    return _call(
        body, name=name, grid=(N_KV_HEADS, T // tq),
        in_specs=[pl.BlockSpec((tq, gw), lambda j, i: (i + qoff, j)),
                  pl.BlockSpec((Ta, HEAD_DIM), lambda j, i: (0, j)),
                  pl.BlockSpec((Ta, HEAD_DIM), lambda j, i: (0, vblk + j))],
        out_specs=[pl.BlockSpec((tq, gw), lambda j, i: (i, j)),
                   pl.BlockSpec((1, tq, GROUP), lambda j, i: (j, i, 0))],
        out_shape=[jax.ShapeDtypeStruct((T, N_Q_HEADS * HEAD_DIM), BF),
                   jax.ShapeDtypeStruct((N_KV_HEADS, T, GROUP), F32)],
        scratch_shapes=[pltpu.VMEM((Ta, 2 * HEAD_DIM), BF), pltpu.VMEM((n, HEAD_DIM), BF), pltpu.VMEM((n, kc), F32),
                        pltpu.VMEM((n, kc), F32), pltpu.VMEM((n, 1), F32), pltpu.VMEM((n, 2 * HEAD_DIM), F32)],
        operands=[q, k, proj], hosted=hosted,
        params=_params(vmem=VMEM_BIG, sem=("arbitrary", "arbitrary")))


def _norm_rope_backward(dz, raw, gg, cos, s_next, s_prev):
    r = lax.rsqrt(jnp.mean(raw * raw, axis=-1, keepdims=True) + EPS)
    n = raw * r
    dy = _rope_transpose(dz, cos, s_next, s_prev)
    dn = dy * gg
    return r * (dn - n * jnp.mean(dn * n, axis=-1, keepdims=True)), jnp.sum(dy * n, axis=0, keepdims=True)


def attention_backward(dproj, q, k, proj, o, lse, do, qgain, tabs, offs, Tc, name, hosted=None):
    Ta = k.shape[0]
    tq = ROW_TILE
    nctx = Tc // tq
    kc = _pick(Ta, (1408, 768, 512, 384, 256, 128))
    gw = GROUP * HEAD_DIM
    vblk = offs["v"] // HEAD_DIM
    qblk = offs["q"] // gw
    zscale = HEAD_DIM ** -0.5

    def body(dp_any, q_ref, k_ref, v_ref, o_ref, lse_ref, do_ref, raw_ref, g_ref, tab_ref, dqr_ref, dk_ref, dv_ref, dg_ref):
        j, i = pl.program_id(0), pl.program_id(1)

        @pl.when(i == 0)
        def _():
            dk_ref[...] = jnp.zeros_like(dk_ref)
            dv_ref[...] = jnp.zeros_like(dv_ref)

        @pl.when((i == 0) & (j == 0))
        def _():
            dg_ref[...] = jnp.zeros_like(dg_ref)

        @pl.when(i < nctx)
        def _():
            dqr_ref[...] = jnp.zeros_like(dqr_ref)

        @pl.when(i >= nctx)
        def _():
            qs = _stack_heads(q_ref[...])
            dob = do_ref[...]
            dos = _stack_heads(dob)
            delta = jnp.concatenate(
                [jnp.sum(dob[:, h * HEAD_DIM:(h + 1) * HEAD_DIM].astype(F32)
                         * o_ref[:, h * HEAD_DIM:(h + 1) * HEAD_DIM].astype(F32), axis=-1, keepdims=True)
                 for h in range(GROUP)], axis=0)
            lse = jnp.concatenate([lse_ref[0, :, h:h + 1] for h in range(GROUP)], axis=0)

            def step(c, dq):
                rows = pl.ds(pl.multiple_of(c * kc, kc), kc)
                kk = k_ref[rows, :]
                vv = v_ref[rows, :]
                s = lax.dot_general(qs, kk, NT, preferred_element_type=F32)
                p = jnp.exp2(s - lse)
                dp = lax.dot_general(dos, vv, NT, preferred_element_type=F32)
                ds = (p * (dp - delta)).astype(BF)
                dv_ref[rows, :] += lax.dot_general(p.astype(BF), dos, TN, preferred_element_type=F32)
                dk_ref[rows, :] += lax.dot_general(ds, qs, TN, preferred_element_type=F32)
                return dq + jnp.dot(ds, kk, preferred_element_type=F32)

            dq = lax.fori_loop(0, Ta // kc, step, jnp.zeros((GROUP * tq, HEAD_DIM), F32))
            per_head = lambda t: jnp.concatenate([t] * GROUP, axis=0)
            dr, dg = _norm_rope_backward(dq * zscale, _stack_heads(raw_ref[...]).astype(F32), g_ref[...],
                                         per_head(tab_ref[0]), per_head(tab_ref[1]), per_head(tab_ref[2]))
            for h in range(GROUP):
                dqr_ref[:, h * HEAD_DIM:(h + 1) * HEAD_DIM] = dr[h * tq:(h + 1) * tq].astype(BF)
            dg_ref[0:1, :] += dg

    lat = lambda j, i: (jnp.maximum(i - nctx, 0), j)
    (dproj, dk, dv, dqg), exchanged = _call(
        body, name=name, grid=(N_KV_HEADS, Ta // tq),
        in_specs=[ANY, pl.BlockSpec((tq, gw), lambda j, i: (i, j)),
                  pl.BlockSpec((Ta, HEAD_DIM), lambda j, i: (0, j)),
                  pl.BlockSpec((Ta, HEAD_DIM), lambda j, i: (0, vblk + j)),
                  pl.BlockSpec((tq, gw), lat),
                  pl.BlockSpec((1, tq, GROUP), lambda j, i: (j, jnp.maximum(i - nctx, 0), 0)),
                  pl.BlockSpec((tq, gw), lat),
                  pl.BlockSpec((tq, gw), lambda j, i: (i, qblk + j)),
                  pl.BlockSpec((1, HEAD_DIM), lambda j, i: (0, 0)),
                  pl.BlockSpec((3, tq, HEAD_DIM), lambda j, i: (0, i, 0))],
        out_specs=[pl.BlockSpec((tq, gw), lambda j, i: (i, qblk + j)),
                   pl.BlockSpec((Ta, HEAD_DIM), lambda j, i: (0, j)),
                   pl.BlockSpec((Ta, HEAD_DIM), lambda j, i: (0, j)),
                   pl.BlockSpec((8, HEAD_DIM), lambda j, i: (0, 0))],
        out_shape=[jax.ShapeDtypeStruct(dproj.shape, BF),
                   jax.ShapeDtypeStruct((Ta, N_KV_HEADS * HEAD_DIM), F32),
                   jax.ShapeDtypeStruct((Ta, N_KV_HEADS * HEAD_DIM), F32),
                   jax.ShapeDtypeStruct((8, HEAD_DIM), F32)],
        operands=[dproj, q, k, proj, o, lse, do, proj, qgain, tabs], hosted=hosted, aliases={0: 0},
        params=_params(vmem=VMEM_BIG, sem=("arbitrary", "arbitrary")))
    return (dproj, dk, dv, dqg), exchanged


def kv_backward(dproj, dk, dv, proj, gain, tabs, offs, name):
    Ta = proj.shape[0]
    tm = _pick(Ta, (768, 512, ROW_TILE))
    kw = N_KV_HEADS * HEAD_DIM
    cb = offs["k"] // (2 * kw)
    kb = offs["k"] // kw
    zscale = 1.0 / LOG2E

    def body(dp_any, dk_ref, dv_ref, raw_ref, g_ref, tab_ref, o_ref, dg_ref):
        @pl.when(pl.program_id(0) == 0)
        def _():
            dg_ref[...] = jnp.zeros_like(dg_ref)

        cos, s_next, s_prev = tab_ref[0], tab_ref[1], tab_ref[2]
        dg = jnp.zeros((1, HEAD_DIM), F32)
        for h in range(N_KV_HEADS):
            sl = slice(h * HEAD_DIM, (h + 1) * HEAD_DIM)
            dr, dgh = _norm_rope_backward(dk_ref[:, sl] * zscale, raw_ref[:, sl].astype(F32), g_ref[...], cos, s_next, s_prev)
            o_ref[:, sl] = dr.astype(BF)
            dg = dg + dgh
        o_ref[:, kw:2 * kw] = dv_ref[...].astype(BF)
        dg_ref[0:1, :] += dg

    return pl.pallas_call(
        body, name=name, grid=(Ta // tm,),
        in_specs=[ANY, _row(tm, kw), _row(tm, kw), pl.BlockSpec((tm, kw), lambda i: (i, kb)),
                  _resident((1, HEAD_DIM)), pl.BlockSpec((3, tm, HEAD_DIM), lambda i: (0, i, 0))],
        out_specs=[pl.BlockSpec((tm, 2 * kw), lambda i: (i, cb)), pl.BlockSpec((8, HEAD_DIM), lambda i: (0, 0))],
        out_shape=[jax.ShapeDtypeStruct(dproj.shape, BF), jax.ShapeDtypeStruct((8, HEAD_DIM), F32)],
        input_output_aliases={0: 0},
        compiler_params=_params(sem=("arbitrary",)),
    )(dproj, dk, dv, proj, gain, tabs)


def merge_forward(x1, mod6, yc, o, proj, w_bc, w_ba, w_o, offs, Tc, name):
    T, D = yc.shape[0], x1.shape[1]
    tm = ROW_TILE
    roff = Tc // tm
    gb = offs["gt"] // (2 * D)

    def body(x_ref, mod_ref, yc_ref, o_ref, gt_ref, wbc_ref, wba_ref, wo_ref, xo_ref, pc_ref, pa_ref, m_ref, z_ref):
        gate = mod_ref[0][2:3]
        pc = jnp.dot(yc_ref[...], wbc_ref[...], preferred_element_type=F32)
        pa = jnp.dot(o_ref[...], wba_ref[...], preferred_element_type=F32)
        pc_ref[...] = pc.astype(BF)
        pa_ref[...] = pa.astype(BF)
        mb = (_sigmoid(gt_ref[:, 0:D].astype(F32)) * pc + _sigmoid(gt_ref[:, D:2 * D].astype(F32)) * pa).astype(BF)
        m_ref[...] = mb
        z = jnp.dot(mb, wo_ref[...], preferred_element_type=F32)
        z_ref[...] = z.astype(BF)
        xo_ref[...] = x_ref[...] + gate * z

    return pl.pallas_call(
        body, name=name, grid=(T // tm,),
        in_specs=[pl.BlockSpec((tm, D), lambda i: (i + roff, 0)), _mod_spec(D, 1, 0), _row(tm, yc.shape[1]), _row(tm, o.shape[1]),
                  pl.BlockSpec((tm, 2 * D), lambda i: (i + roff, gb)),
                  _resident(w_bc.shape), _resident(w_ba.shape), _resident(w_o.shape)],
        out_specs=[_row(tm, D)] * 5,
        out_shape=[jax.ShapeDtypeStruct((T, D), F32)] + [jax.ShapeDtypeStruct((T, D), BF)] * 4,
        compiler_params=_params(vmem=VMEM_BIG, sem=("arbitrary",)),
    )(x1, mod6, yc, o, proj, w_bc, w_ba, w_o)


def merge_backward_rows(dx2, mod6, z, pc, pa, proj, w_bc, w_ba, w_o, offs, Tc, name):
    T, D = dx2.shape
    Ta, P = proj.shape
    tm = ROW_TILE
    nctx = Tc // tm
    gb = offs["gt"] // (2 * D)
    dcw, dqw = w_bc.shape[0], w_ba.shape[0]

    def body(dx_ref, mod_ref, z_ref, pc_ref, pa_ref, gt_ref, wbc_ref, wba_ref, wo_ref,
             dgt_ref, dg_ref, dpc_ref, dpa_ref, dyc_ref, do_ref, acc_ref):
        i = pl.program_id(0)

        @pl.when(i == 0)
        def _():
            acc_ref[...] = jnp.zeros_like(acc_ref)

        @pl.when(i < nctx)
        def _():
            dgt_ref[...] = jnp.zeros_like(dgt_ref)

        @pl.when(i >= nctx)
        def _():
            gate = mod_ref[0][2:3]
            dx = dx_ref[...]
            acc_ref[0:1, :] += jnp.sum(dx * z_ref[...].astype(F32), axis=0, keepdims=True)
            dgb = (dx * gate).astype(BF)
            dg_ref[...] = dgb
            dm = lax.dot_general(dgb, wo_ref[...], NT, preferred_element_type=F32)
            sc = _sigmoid(gt_ref[:, 0:D].astype(F32))
            sa = _sigmoid(gt_ref[:, D:2 * D].astype(F32))
            pc = pc_ref[...].astype(F32)
            pa = pa_ref[...].astype(F32)
            dpc = (dm * sc).astype(BF)
            dpa = (dm * sa).astype(BF)
            dpc_ref[...] = dpc
            dpa_ref[...] = dpa
            dgt_ref[:, 0:D] = ((dm * pc) * (sc * (1.0 - sc))).astype(BF)
            dgt_ref[:, D:2 * D] = ((dm * pa) * (sa * (1.0 - sa))).astype(BF)
            dyc_ref[...] = lax.dot_general(dpc, wbc_ref[...], NT, preferred_element_type=F32).astype(BF)
            do_ref[...] = lax.dot_general(dpa, wba_ref[...], NT, preferred_element_type=F32).astype(BF)

    lat = lambda n: pl.BlockSpec((tm, n), lambda i: (jnp.maximum(i - nctx, 0), 0))
    return pl.pallas_call(
        body, name=name, grid=(Ta // tm,),
        in_specs=[lat(D), _mod_spec(D, 1, 0), lat(D), lat(D), lat(D),
                  pl.BlockSpec((tm, 2 * D), lambda i: (i, gb)),
                  _resident(w_bc.shape), _resident(w_ba.shape), _resident(w_o.shape)],
        out_specs=[pl.BlockSpec((tm, 2 * D), lambda i: (i, gb)), lat(D), lat(D), lat(D), lat(dcw), lat(dqw),
                   pl.BlockSpec((8, D), lambda i: (0, 0))],
        out_shape=[jax.ShapeDtypeStruct((Ta, P), BF)] + [jax.ShapeDtypeStruct((T, D), BF)] * 3
                  + [jax.ShapeDtypeStruct((T, dcw), BF), jax.ShapeDtypeStruct((T, dqw), BF), jax.ShapeDtypeStruct((8, D), F32)],
        compiler_params=_params(vmem=VMEM_BIG, sem=("arbitrary",)),
    )(dx2, mod6, z, pc, pa, proj, w_bc, w_ba, w_o)


def proj_backward_rows(dproj, dres, xa, mod6, g, w_in, nctx, name, hosted=None):
    Tr, D = xa.shape
    P = w_in.shape[1]
    tm = ROW_TILE

    def body(dp_ref, dres_ref, x_ref, mod_ref, g_ref, w_ref, dx_ref, acc_ref):
        i = pl.program_id(0)

        @pl.when((i == 0) | (i == nctx))
        def _():
            acc_ref[...] = jnp.zeros_like(acc_ref)

        x = x_ref[...]
        scale = mod_ref[0][1:2]
        gg = g_ref[...]
        dhm = lax.dot_general(dp_ref[...], w_ref[...], NT, preferred_element_type=F32)
        r = lax.rsqrt(jnp.mean(x * x, axis=-1, keepdims=True) + EPS)
        xh = x * r
        dshift = jnp.sum(dhm, axis=0, keepdims=True)
        dscale = jnp.sum(dhm * (xh * gg), axis=0, keepdims=True)
        dxh_g = dhm * (1.0 + scale)
        dg = jnp.sum(dxh_g * xh, axis=0, keepdims=True)
        dxh = dxh_g * gg
        res = jnp.where(i < nctx, 0.0, dres_ref[...])
        dx_ref[...] = res + r * (dxh - xh * jnp.mean(dxh * xh, axis=-1, keepdims=True))
        for k, val in enumerate((dshift, dscale, dg)):
            acc_ref[0, k:k + 1, :] += val

    return _call(
        body, name=name, grid=(Tr // tm,),
        in_specs=[_row(tm, P), pl.BlockSpec((tm, D), lambda i: (jnp.maximum(i - nctx, 0), 0)), _row(tm, D),
                  _mod_spec(D, 1, nctx), _resident((1, D)), _resident(w_in.shape)],
        out_specs=[_row(tm, D), _acc_spec(D, nctx)],
        out_shape=[jax.ShapeDtypeStruct((Tr, D), F32), jax.ShapeDtypeStruct((2, 8, D), F32)],
        operands=[dproj, dres, xa, mod6, g, w_in], hosted=hosted,
        params=_params(vmem=VMEM_BIG, sem=("arbitrary",)))


def _adam_update(w, g, m, v):
    c1 = 1.0 - ADAM_B1 ** ADAM_STEP
    c2 = 1.0 - ADAM_B2 ** ADAM_STEP
    m = ADAM_B1 * m + (1.0 - ADAM_B1) * g
    v = ADAM_B2 * v + (1.0 - ADAM_B2) * (g * g)
    return -ADAM_LR * ((m / c1) / (jnp.sqrt(v / c2) + ADAM_EPS) + ADAM_WD * w), m, v


def adamw(w, g, m, v, name, dep=None):
    R, C = w.shape
    tr = _row_tile(R, C)
    deps = [] if dep is None else [dep]

    def body(w_ref, g_ref, m_ref, v_ref, *rest):
        d_ref, nm_ref, nv_ref = rest[len(deps):]
        d_ref[...], nm_ref[...], nv_ref[...] = _adam_update(w_ref[...], g_ref[...], m_ref[...], v_ref[...])

    blk = pl.BlockSpec((tr, C), lambda i: (i, 0))
    return pl.pallas_call(
        body, name=name, grid=(R // tr,),
        in_specs=[blk] * 4 + [ANY] * len(deps), out_specs=[blk] * 3,
        out_shape=[jax.ShapeDtypeStruct((R, C), F32)] * 3,
        compiler_params=_params(vmem=VMEM_BIG, sem=("parallel",)),
    )(w, g, m, v, *deps)


def adamw_summed(recv, w, m, v, name, dep=None):
    R, C = w.shape
    tr = _row_tile(R, C)
    deps = [] if dep is None else [dep]

    def body(r_ref, w_ref, m_ref, v_ref, *rest):
        g_ref, d_ref, nm_ref, nv_ref = rest[len(deps):]
        g = r_ref[0].astype(F32)
        for a in range(1, N_DEV):
            g = g + r_ref[a].astype(F32)
        g_ref[...] = g
        d_ref[...], nm_ref[...], nv_ref[...] = _adam_update(w_ref[...], g, m_ref[...], v_ref[...])

    blk = pl.BlockSpec((tr, C), lambda i: (i, 0))
    return pl.pallas_call(
        body, name=name, grid=(R // tr,),
        in_specs=[pl.BlockSpec((N_DEV, tr, C), lambda i: (0, i, 0)), blk, blk, blk] + [ANY] * len(deps), out_specs=[blk] * 4,
        out_shape=[jax.ShapeDtypeStruct((R, C), F32)] * 4,
        compiler_params=_params(vmem=VMEM_BIG, sem=("parallel",)),
    )(recv, w, m, v, *deps)


def _rope_tables(T, Tc):
    rows = T // GRID_W
    n_freq = HEAD_DIM // 4
    inv = ROPE_THETA ** (-jnp.arange(n_freq, dtype=F32) / n_freq)
    ang_r = jnp.arange(rows).astype(F32)[:, None] * inv
    ang_c = jnp.arange(GRID_W).astype(F32)[:, None] * inv
    per_row = lambda a: jnp.broadcast_to(a[:, None, :], (rows, GRID_W, n_freq)).reshape(T, n_freq)
    per_col = lambda a: jnp.broadcast_to(a[None, :, :], (rows, GRID_W, n_freq)).reshape(T, n_freq)
    cr, sr = per_row(jnp.cos(ang_r)), per_row(jnp.sin(ang_r))
    cc, sc = per_col(jnp.cos(ang_c)), per_col(jnp.sin(ang_c))
    zero = jnp.zeros_like(sr)
    cos = jnp.concatenate([cr, cr, cc, cc], axis=1)
    s_next = jnp.concatenate([-sr, zero, -sc, zero], axis=1)
    s_prev = jnp.concatenate([zero, sr, zero, sc], axis=1)
    lat = jnp.stack([cos, s_next, s_prev])
    ctx = jnp.stack([jnp.ones((Tc, HEAD_DIM), F32), jnp.zeros((Tc, HEAD_DIM), F32), jnp.zeros((Tc, HEAD_DIM), F32)])
    return jnp.concatenate([ctx, lat], axis=1)


BIG = ("ffn1_w_in", "ffn1_w_out", "w_in", "w_branch_conv", "w_branch_attn", "w_out", "ffn2_w_in", "ffn2_w_out")


def _regroup_w_in(stacked, D, Dc, qw, kw):
    w = stacked.transpose(1, 0, 2).reshape(D, -1)
    o = 0
    parts = {}
    for nme, wd in (("bg", Dc), ("cg", Dc), ("vc", Dc), ("q", qw), ("k", kw), ("v", kw), ("gt", 2 * D)):
        parts[nme] = w[:, o:o + wd]
        o += wd
    nb = Dc // 128
    cv = jnp.stack([parts[n].reshape(D, nb, 128) for n in ("bg", "cg", "vc")], axis=2).reshape(D, 3 * Dc)
    return jnp.concatenate([cv, parts["q"], parts["gt"], parts["k"], parts["v"]], axis=1)


def _ungroup_w_in_grad(gt_, D, Dc, qw, kw):
    nb = Dc // 128
    cv = gt_[:3 * Dc].reshape(nb, 3, 128, D)
    o = 3 * Dc
    q = gt_[o:o + qw]
    gt = gt_[o + qw:o + qw + 2 * D]
    k = gt_[o + qw + 2 * D:o + qw + 2 * D + kw]
    v = gt_[o + qw + 2 * D + kw:]
    nat = jnp.concatenate([cv[:, 0].reshape(Dc, D), cv[:, 1].reshape(Dc, D), cv[:, 2].reshape(Dc, D), q, k, v, gt], axis=0)
    return nat.reshape(N_DEV, -1, D)


def kernel(x, c, ctx, c_ctx, w_mod, b_mod, norm1_g, norm2_g, norm3_g, ffn1_w_in, ffn1_w_out, w_in, conv_w, q_norm_g, k_norm_g, w_branch_conv, w_branch_attn, w_out, ffn2_w_in, ffn2_w_out, final_g, loss_target, m_c_ctx, m_w_mod, m_b_mod, m_norm1_g, m_norm2_g, m_norm3_g, m_ffn1_w_in, m_ffn1_w_out, m_w_in, m_conv_w, m_q_norm_g, m_k_norm_g, m_w_branch_conv, m_w_branch_attn, m_w_out, m_ffn2_w_in, m_ffn2_w_out, m_final_g, v_c_ctx, v_w_mod, v_b_mod, v_norm1_g, v_norm2_g, v_norm3_g, v_ffn1_w_in, v_ffn1_w_out, v_w_in, v_conv_w, v_q_norm_g, v_k_norm_g, v_w_branch_conv, v_w_branch_attn, v_w_out, v_ffn2_w_in, v_ffn2_w_out, v_final_g):
    weights = dict(c_ctx=c_ctx, w_mod=w_mod, b_mod=b_mod, norm1_g=norm1_g, norm2_g=norm2_g, norm3_g=norm3_g,
                   ffn1_w_in=ffn1_w_in, ffn1_w_out=ffn1_w_out, w_in=w_in, conv_w=conv_w, q_norm_g=q_norm_g,
                   k_norm_g=k_norm_g, w_branch_conv=w_branch_conv, w_branch_attn=w_branch_attn, w_out=w_out,
                   ffn2_w_in=ffn2_w_in, ffn2_w_out=ffn2_w_out, final_g=final_g)
    moms = dict(c_ctx=(m_c_ctx, v_c_ctx), w_mod=(m_w_mod, v_w_mod), b_mod=(m_b_mod, v_b_mod),
                norm1_g=(m_norm1_g, v_norm1_g), norm2_g=(m_norm2_g, v_norm2_g), norm3_g=(m_norm3_g, v_norm3_g),
                ffn1_w_in=(m_ffn1_w_in, v_ffn1_w_in), ffn1_w_out=(m_ffn1_w_out, v_ffn1_w_out), w_in=(m_w_in, v_w_in),
                conv_w=(m_conv_w, v_conv_w), q_norm_g=(m_q_norm_g, v_q_norm_g), k_norm_g=(m_k_norm_g, v_k_norm_g),
                w_branch_conv=(m_w_branch_conv, v_w_branch_conv), w_branch_attn=(m_w_branch_attn, v_w_branch_attn),
                w_out=(m_w_out, v_w_out), ffn2_w_in=(m_ffn2_w_in, v_ffn2_w_in), ffn2_w_out=(m_ffn2_w_out, v_ffn2_w_out),
                final_g=(m_final_g, v_final_g))
    order = list(weights)

    T, D = x.shape[1], x.shape[2]
    Tc = ctx.shape[1]
    nctx = Tc // ROW_TILE
    nd = N_MOD * D
    Dc = conv_w.shape[2] * N_DEV
    qw, kw = N_Q_HEADS * HEAD_DIM, N_KV_HEADS * HEAD_DIM
    offs, o = {}, 0
    for nme, wd in (("cv", 3 * Dc), ("q", qw), ("gt", 2 * D), ("k", kw), ("v", kw)):
        offs[nme] = o
        o += wd

    ax, ay, ac = lax.axis_index("x"), lax.axis_index("y"), lax.axis_index("c")
    me = 4 * ax + 2 * ay + ac

    shard = {n: (jnp.swapaxes(weights[n][0], 0, 1) if n in ("ffn1_w_in", "ffn2_w_in") else weights[n][0]).astype(BF)
             for n in BIG}
    rows2d = lambda a: a.reshape(-1, a.shape[-1])
    full = {}
    mod_cols = w_mod.shape[2]
    cw_loc = conv_w[0]
    cpad = (-(D + CONV_TAPS * cw_loc.shape[1])) % 128
    pay = jnp.concatenate([c.reshape(1, D), cw_loc.reshape(1, -1), jnp.zeros((1, cpad), F32)], axis=1)
    (g_ffn1_in, g_ffn1_out), (call,) = allgather_two_level([shard["ffn1_w_in"], shard["ffn1_w_out"]], "ag_ffn1",
                                                          riders=[pay])
    full["ffn1_w_in"], full["ffn1_w_out"] = rows2d(g_ffn1_in), rows2d(g_ffn1_out)

    conv_full = call[:, 0, D:D + CONV_TAPS * cw_loc.shape[1]].reshape(N_DEV, CONV_TAPS, -1).transpose(1, 0, 2).reshape(CONV_TAPS, Dc)
    b_loc = lax.dynamic_slice_in_dim(b_mod, me * mod_cols, mod_cols, axis=1)
    cctx2 = c_ctx.reshape(1, D)
    mod_part = mod_forward(call, cctx2, w_mod[0], b_loc, "mod_fwd")
    mod_all = allgather_direct(mod_part, "ag_mod")
    mod_lat = lax.dynamic_index_in_dim(mod_all, me, axis=1, keepdims=False).reshape(nd)
    mod_ctx = mod_all[:, N_DEV, :].reshape(nd)
    mod6 = jnp.stack([mod_ctx, mod_lat]).reshape(6, 3, D)

    tabs = _rope_tables(T, Tc)

    srcs1 = (ctx[0], x[0])
    (xa1, hm1, ab1, h1, f1), (g_w_in,) = ffn_forward(
        srcs1, mod6, 0, norm1_g, full["ffn1_w_in"], full["ffn1_w_out"], nctx, "ffn1_fwd",
        hosted=Hosted(gathers=[shard["w_in"]]))
    full["w_in"] = _regroup_w_in(g_w_in, D, Dc, qw, kw)
    merge_names = ("w_branch_conv", "w_branch_attn", "w_out")
    (hx, proj, qr, kr), g_merge = proj_forward(
        xa1, mod6, norm2_g, full["w_in"], q_norm_g, k_norm_g, tabs, offs, nctx, "proj_fwd",
        hosted=Hosted(gathers=[shard[n] for n in merge_names]))
    full.update({n: rows2d(g) for n, g in zip(merge_names, g_merge)})
    yc = conv_forward(proj, conv_full, offs, Tc, "conv_fwd")
    (oa, lse), (g_ffn2_in, g_ffn2_out) = attention_forward(
        qr, kr, proj, offs, Tc, "attn_fwd", hosted=Hosted(gathers=[shard["ffn2_w_in"], shard["ffn2_w_out"]]))
    full["ffn2_w_in"], full["ffn2_w_out"] = rows2d(g_ffn2_in), rows2d(g_ffn2_out)
    x2, pc, pa, mm, zz = merge_forward(xa1, mod6, yc, oa, proj, full["w_branch_conv"], full["w_branch_attn"],
                                       full["w_out"], offs, Tc, "merge_fwd")
    (dx3, hm2, ab2, h2, f2, lacc), _ = ffn_forward((x2,), mod6, 2, norm3_g, full["ffn2_w_in"], full["ffn2_w_out"], 0, "ffn2_fwd",
                                                   final=(loss_target[0], final_g.reshape(1, D)))

    by_dest = lambda g: g.reshape((N_DEV, -1, g.shape[-1]))
    (dx2, dab2, df2, acc_f2), _ = ffn_backward_rows(dx3, (x2,), mod6, 2, norm3_g, ab2, f2, full["ffn2_w_in"], full["ffn2_w_out"], 0, "ffn2_bwd")
    early = {"ffn2_w_out": tn_matmul(h2, df2, "ffn2_dwout")[0], "ffn2_w_in": tn_matmul(dab2, hm2, "ffn2_dwin")[0]}
    dproj, dgm, dpc, dpa, dyc, do, acc_mg = merge_backward_rows(dx2, mod6, zz, pc, pa, proj, full["w_branch_conv"],
                                                                full["w_branch_attn"], full["w_out"], offs, Tc, "merge_bwd")
    early["w_out"] = tn_matmul(mm, dgm, "dw_out")[0]
    early["w_branch_conv"] = tn_matmul(yc, dpc, "dw_bc")[0]
    early["w_branch_attn"] = tn_matmul(oa, dpa, "dw_ba")[0]
    dproj, dcw = conv_backward(dproj, dyc, proj, conv_full, offs, Tc, "conv_bwd")
    (dproj, dk, dv, dqg), summed = attention_backward(dproj, qr, kr, proj, oa, lse, do, q_norm_g, tabs, offs, Tc, "attn_bwd",
                                                      hosted=Hosted(scatters=[by_dest(g) for g in early.values()]))
    summed = dict(zip(early, summed))
    dproj, dkg = kv_backward(dproj, dk, dv, proj, k_norm_g, tabs, offs, "kv_bwd")
    g_w_in_grad = _ungroup_w_in_grad(tn_matmul(dproj, hx, "dw_in")[0], D, Dc, qw, kw)
    (dxa1, acc_pj), (summed["w_in"],) = proj_backward_rows(dproj, dx2, xa1, mod6, norm2_g, full["w_in"], nctx, "proj_bwd",
                                                          hosted=Hosted(scatters=[g_w_in_grad]))
    (grad_x2d, dab1, df1, acc_f1), _ = ffn_backward_rows(
        dxa1, srcs1, mod6, 0, norm1_g, ab1, f1, full["ffn1_w_in"], full["ffn1_w_out"], nctx, "ffn1_bwd")
    g_ffn1_w_in, (summed["ffn1_w_out"],) = tn_matmul(
        dab1, hm1, "ffn1_dwin", hosted=Hosted(scatters=[by_dest(tn_matmul(h1, df1, "ffn1_dwout")[0])]))
    grad_x = grad_x2d[None]

    zero_d = jnp.zeros((D,), F32)
    dlat = jnp.concatenate([acc_f1[1, 0], acc_f1[1, 1], acc_f1[1, 2], acc_pj[1, 0], acc_pj[1, 1], acc_mg[0],
                            acc_f2[1, 0], acc_f2[1, 1], acc_f2[1, 2]])
    dctx = jnp.concatenate([acc_f1[0, 0], acc_f1[0, 1], acc_f1[0, 2], acc_pj[0, 0], acc_pj[0, 1]] + [zero_d] * 4)
    small = jnp.concatenate([acc_f1[0, 3] + acc_f1[1, 3], acc_pj[0, 2] + acc_pj[1, 2], acc_f2[1, 3],
                             dqg[0], dkg[0], lacc[0], dcw[0:CONV_TAPS].reshape(-1), lacc[1, 0:128]])
    n_small = small.shape[0]
    pay_b = jnp.concatenate([dlat, dctx, small]).reshape(1, -1)
    gath = allgather_direct(pay_b, "ag_small_grads")
    dlat_loc = lax.dynamic_slice_in_dim(gath[:, 0, :nd], me * mod_cols, mod_cols, axis=1)
    dctx_loc = lax.dynamic_slice_in_dim(gath[:, 0, nd:2 * nd], me * mod_cols, mod_cols, axis=1)
    g_wmod, pc_part, small_sum = mod_backward(call, cctx2, w_mod[0], dlat_loc, dctx_loc, gath, 2 * nd, n_small, "mod_bwd")
    pcs = allgather_direct(pc_part, "ag_cctx")
    g_bmod, g_cctx = bmod_and_cctx_grad(gath, pcs, cctx2, nd, "small_bwd")
    sm = small_sum[0]
    loss = sm[n_small - 1]
    conv_off = 4 * D + 2 * HEAD_DIM
    g_conv_full = sm[conv_off:conv_off + CONV_TAPS * Dc].reshape(CONV_TAPS, Dc)
    g_conv = lax.dynamic_slice_in_dim(g_conv_full, me * cw_loc.shape[1], cw_loc.shape[1], axis=1)
    gsmall = dict(
        c_ctx=g_cctx, w_mod=g_wmod, b_mod=g_bmod, norm1_g=sm[0:D][None], norm2_g=sm[D:2 * D][None],
        norm3_g=sm[2 * D:3 * D][None], q_norm_g=sm[3 * D:3 * D + HEAD_DIM][None],
        k_norm_g=sm[3 * D + HEAD_DIM:3 * D + 2 * HEAD_DIM][None],
        final_g=sm[3 * D + 2 * HEAD_DIM:3 * D + 2 * HEAD_DIM + D][None], conv_w=g_conv)

    flipped = ("ffn1_w_in", "w_in", "ffn2_w_in")
    last = "ffn1_w_in"
    results = {}

    def update(n, dep=None):
        w = weights[n]
        shp = w.shape
        if n in flipped:
            two_d = lambda a: jnp.swapaxes(a[0], 0, 1)
            back = lambda a: jnp.swapaxes(a, 0, 1)[None]
        else:
            two_d = lambda a: a.reshape(-1, shp[-1])
            back = lambda a: a.reshape(shp)
        m, v = moms[n]
        if n in summed:
            g2, d, nm, nv = adamw_summed(summed[n], two_d(w), two_d(m), two_d(v), "adamw_" + n, dep=dep)
        else:
            g2 = gsmall[n].reshape(two_d(w).shape)
            d, nm, nv = adamw(two_d(w), g2, two_d(m), two_d(v), "adamw_" + n, dep=dep)
        results[n] = tuple(back(a) for a in (g2, d, nm, nv))
        return d

    last_sems, last_recv_sems, last_src, last_land, token = scatter_start(
        by_dest(g_ffn1_w_in), [g_cctx], "rs_ffn1_start")
    heavy = ("w_mod", "w_in", "ffn2_w_in", "ffn2_w_out", "ffn1_w_out", "w_out")
    deltas = {n: update(n, dep=token) for n in order if n != last}
    after = jnp.concatenate([deltas[n][0:1, 0:1] for n in heavy], axis=1)
    g_done, land_done = scatter_wait(last_sems, last_recv_sems, last_src, last_land, after, "rs_ffn1_wait")
    own = lax.dynamic_index_in_dim(g_done, me, axis=0, keepdims=True)
    summed[last] = lax.dynamic_update_slice_in_dim(land_done, own, me, axis=0)
    update(last)
    cols = list(zip(*(results[n] for n in order)))
    return (loss, grad_x, *cols[0], *cols[1], *cols[2], *cols[3])
```

```python
import math

import jax
import jax.numpy as jnp
from jax import lax
from jax.experimental import pallas as pl
from jax.experimental.pallas import tpu as pltpu

F32 = jnp.float32
BF = jnp.bfloat16
EPS = 1e-6
N_DEV = 8
HEAD_DIM = 128
N_Q_HEADS = 8
N_KV_HEADS = 2
GROUP = N_Q_HEADS // N_KV_HEADS
GRID_W = 64
ROPE_THETA = 10000.0
CONV_TAPS = 3
N_MOD = 9
ADAM_LR = 0.001
ADAM_B1 = 0.9
ADAM_B2 = 0.999
ADAM_EPS = 1e-08
ADAM_WD = 0.01
ADAM_STEP = 10
ROW_TILE = 256
VMEM_BIG = 56 << 20
MESH_ID = pl.DeviceIdType.MESH
HIGHEST = lax.Precision.HIGHEST
NT = (((1,), (1,)), ((), ()))
TN = (((0,), (0,)), ((), ()))
LOG2E = math.log2(math.e)


def _pick(n, cands):
    for c in cands:
        if n % c == 0:
            return c
    return n


def _params(vmem=None, sem=None):
    kw = {}
    if vmem is not None:
        kw["vmem_limit_bytes"] = vmem
    if sem is not None:
        kw["dimension_semantics"] = sem
    return pltpu.CompilerParams(**kw)


def _resident(shape):
    nd = len(shape)
    return pl.BlockSpec(shape, lambda *_: (0,) * nd, pipeline_mode=pl.Buffered(1))


def _sigmoid(x):
    return jax.nn.sigmoid(x)


ANY = pl.BlockSpec(memory_space=pl.ANY)


def _coords():
    return lax.axis_index("x"), lax.axis_index("y"), lax.axis_index("c")


def _flip(v, bit):
    return 1 - v if bit else v


def _remote(src, dst, ssem, rsem, dev):
    return pltpu.make_async_remote_copy(src_ref=src, dst_ref=dst, send_sem=ssem, recv_sem=rsem,
                                        device_id=dev, device_id_type=MESH_ID)


def allgather_direct(v, name):
    def body(v_ref, out_ref, ssem, rsem, lsem):
        x, y, c = _coords()
        me = 4 * x + 2 * y + c
        mine = pltpu.make_async_copy(v_ref, out_ref.at[me], lsem)
        mine.start()
        cps = []
        for p in range(1, N_DEV):
            px, py, pc = (p >> 2) & 1, (p >> 1) & 1, p & 1
            cps.append(_remote(v_ref, out_ref.at[me], ssem.at[p - 1], rsem.at[p - 1],
                               (_flip(x, px), _flip(y, py), _flip(c, pc))))
        for cp in cps:
            cp.start()
        for p in range(1, N_DEV):
            px, py, pc = (p >> 2) & 1, (p >> 1) & 1, p & 1
            src = 4 * _flip(x, px) + 2 * _flip(y, py) + _flip(c, pc)
            _remote(v_ref, out_ref.at[src], ssem.at[p - 1], rsem.at[p - 1], (x, y, c)).wait_recv()
        for cp in cps:
            cp.wait_send()
        mine.wait()

    return pl.pallas_call(
        body, name=name,
        out_shape=jax.ShapeDtypeStruct((N_DEV,) + v.shape, v.dtype),
        in_specs=[ANY], out_specs=ANY,
        scratch_shapes=[pltpu.SemaphoreType.DMA((N_DEV - 1,)), pltpu.SemaphoreType.DMA((N_DEV - 1,)),
                        pltpu.SemaphoreType.DMA],
    )(v)


def allgather_two_level(shards, name, riders=()):
    n = len(shards)
    ride = Hosted(gathers=riders)
    r = ride.n

    def body(*refs):
        v_refs, rin, out_refs, rout = refs[:n], refs[n:n + r], refs[n + r:2 * n + r], refs[2 * n + r:2 * n + 2 * r]
        (ssem, rsem, lsem), rsems = refs[2 * n + 2 * r:2 * n + 2 * r + 3], refs[2 * n + 2 * r + 3:]
        x, y, c = _coords()
        me = (x, y, c)
        sib = (x, y, 1 - c)
        chips = [(1 - x, y), (x, 1 - y), (1 - x, 1 - y)]
        if r:
            ride.start(rin, rout, *rsems)

        def slot(w, px, py, pc):
            return out_refs[w].at[4 * px + 2 * py + pc]

        def sem(w, k):
            return ssem.at[7 * w + k], rsem.at[7 * w + k]

        mine = [pltpu.make_async_copy(v_refs[w], slot(w, *me), lsem.at[w]) for w in range(n)]
        for cp in mine:
            cp.start()
        first = []
        for w in range(n):
            first.append(_remote(v_refs[w], slot(w, *me), *sem(w, 0), sib))
            first += [_remote(v_refs[w], slot(w, *me), *sem(w, 1 + j), (*chip, c)) for j, chip in enumerate(chips)]
        for cp in first:
            cp.start()
        passed = []
        for w in range(n):
            for j, chip in enumerate(chips):
                _remote(v_refs[w], slot(w, *chip, c), *sem(w, 1 + j), me).wait_recv()
                cp = _remote(slot(w, *chip, c), slot(w, *chip, c), *sem(w, 4 + j), sib)
                cp.start()
                passed.append(cp)
        for w in range(n):
            _remote(v_refs[w], slot(w, x, y, 1 - c), *sem(w, 0), me).wait_recv()
            for j, chip in enumerate(chips):
                _remote(v_refs[w], slot(w, *chip, 1 - c), *sem(w, 4 + j), me).wait_recv()
        for cp in first + passed:
            cp.wait_send()
        for cp in mine:
            cp.wait()
        if r:
            ride.wait(rin, rout, *rsems)

    res = pl.pallas_call(
        body, name=name,
        out_shape=[jax.ShapeDtypeStruct((N_DEV,) + s.shape, s.dtype) for s in shards] + ride.out_shapes,
        in_specs=[ANY] * (n + r), out_specs=[ANY] * (n + r),
        scratch_shapes=[pltpu.SemaphoreType.DMA((7 * n,)), pltpu.SemaphoreType.DMA((7 * n,)),
                        pltpu.SemaphoreType.DMA((n,))] + (ride.scratch if r else []),
    )(*shards, *riders)
    return list(res[:n]), list(res[n:])


SEM = pl.BlockSpec(memory_space=pltpu.SEMAPHORE)
IN_HBM = pl.BlockSpec(memory_space=pltpu.HBM)
DATAFLOW = pltpu.SideEffectType.DATAFLOW_SIDE_EFFECTING


def _scatter_descriptors(src_ref, land_ref, ssem, rsem, arrivals):
    x, y, c = _coords()
    me = 4 * x + 2 * y + c
    cps = []
    for p in range(1, N_DEV):
        px, py, pc = _flip(x, (p >> 2) & 1), _flip(y, (p >> 1) & 1), _flip(c, p & 1)
        peer = 4 * px + 2 * py + pc
        if arrivals:
            cps.append(_remote(src_ref.at[me], land_ref.at[peer], ssem.at[p - 1], rsem.at[p - 1], (x, y, c)))
        else:
            cps.append(_remote(src_ref.at[peer], land_ref.at[me], ssem.at[p - 1], rsem.at[p - 1], (px, py, pc)))
    return cps


def scatter_start(g, before, name):
    nb = len(before)

    def body(g_ref, land_ref, *rest):
        ssem, rsem, g_thru, land_thru, token = rest[nb:]
        for cp in _scatter_descriptors(g_ref, land_ref, ssem, rsem, False):
            cp.start()
        token[...] = jnp.zeros_like(token)

    return pl.pallas_call(
        body, name=name,
        out_shape=(pltpu.SemaphoreType.DMA((N_DEV - 1,)), pltpu.SemaphoreType.DMA((N_DEV - 1,)),
                   pltpu.HBM(g.shape, g.dtype), pltpu.HBM(g.shape, g.dtype), jax.ShapeDtypeStruct((8, 128), F32)),
        in_specs=(IN_HBM, IN_HBM) + (ANY,) * nb, out_specs=(SEM, SEM, IN_HBM, IN_HBM, pl.BlockSpec(memory_space=pltpu.VMEM)),
        input_output_aliases={0: 2, 1: 3},
        compiler_params=pltpu.CompilerParams(has_side_effects=DATAFLOW),
    )(pltpu.with_memory_space_constraint(g, pltpu.HBM),
      pltpu.with_memory_space_constraint(lax.empty(g.shape, g.dtype), pltpu.HBM), *before)


def scatter_wait(ssem, rsem, g_thru, land_thru, after, name):
    def body(g_ref, land_ref, ssem_ref, rsem_ref, after_ref, g_out, land_out):
        cps = _scatter_descriptors(g_ref, land_ref, ssem_ref, rsem_ref, True)
        for cp in cps:
            cp.wait_send()
        for cp in cps:
            cp.wait_recv()

    return pl.pallas_call(
        body, name=name,
        out_shape=(pltpu.HBM(g_thru.shape, g_thru.dtype), pltpu.HBM(land_thru.shape, land_thru.dtype)),
        in_specs=(IN_HBM, IN_HBM, SEM, SEM, ANY), out_specs=(IN_HBM, IN_HBM),
        input_output_aliases={0: 0, 1: 1},
        compiler_params=pltpu.CompilerParams(has_side_effects=DATAFLOW),
    )(g_thru, land_thru, ssem, rsem, after)


class Hosted:
    def __init__(self, gathers=(), scatters=()):
        self.items = [(a, False) for a in gathers] + [(a, True) for a in scatters]
        self.n = len(self.items)
        self.operands = [a for a, _ in self.items]
        self.out_shapes = [jax.ShapeDtypeStruct(a.shape if sc else (N_DEV,) + a.shape, a.dtype) for a, sc in self.items]
        self.scratch = [pltpu.SemaphoreType.DMA((7 * self.n,)), pltpu.SemaphoreType.DMA((7 * self.n,)),
                        pltpu.SemaphoreType.DMA((self.n,))]

    def _copies(self, in_refs, out_refs, ssem, rsem, lsem, arrivals):
        x, y, c = _coords()
        me = 4 * x + 2 * y + c
        remote, local = [], []
        for w, (_, sc) in enumerate(self.items):
            src, dst = in_refs[w], out_refs[w]
            local.append(pltpu.make_async_copy(src.at[me] if sc else src, dst.at[me], lsem.at[w]))
            for p in range(1, N_DEV):
                px, py, pc = _flip(x, (p >> 2) & 1), _flip(y, (p >> 1) & 1), _flip(c, p & 1)
                peer = 4 * px + 2 * py + pc
                k = 7 * w + p - 1
                if arrivals:
                    remote.append(_remote(src.at[me] if sc else src, dst.at[peer], ssem.at[k], rsem.at[k], (x, y, c)))
                else:
                    remote.append(_remote(src.at[peer] if sc else src, dst.at[me], ssem.at[k], rsem.at[k], (px, py, pc)))
        return remote, local

    def start(self, in_refs, out_refs, ssem, rsem, lsem):
        sends, local = self._copies(in_refs, out_refs, ssem, rsem, lsem, False)
        for cp in local + sends:
            cp.start()

    def wait(self, in_refs, out_refs, ssem, rsem, lsem):
        arrivals, local = self._copies(in_refs, out_refs, ssem, rsem, lsem, True)
        for cp in arrivals:
            cp.wait_recv()
        for cp in arrivals:
            cp.wait_send()
        for cp in local:
            cp.wait()


def _call(body, *, name, grid, in_specs, out_specs, out_shape, operands, params, scratch_shapes=(), aliases=None, hosted=None):
    n_in, n_out, n_scr = len(in_specs), len(out_specs), len(scratch_shapes)
    h = hosted.n if hosted is not None else 0

    def wrapped(*refs):
        ins, cins = refs[:n_in], refs[n_in:n_in + h]
        outs, couts = refs[n_in + h:n_in + h + n_out], refs[n_in + h + n_out:n_in + 2 * h + n_out]
        rest = refs[n_in + 2 * h + n_out:]
        scr, sems = rest[:n_scr], rest[n_scr:]
        if h:
            ids = [pl.program_id(a) for a in range(len(grid))]
            first, last = ids[0] == 0, ids[0] == grid[0] - 1
            for a in range(1, len(grid)):
                first, last = first & (ids[a] == 0), last & (ids[a] == grid[a] - 1)

            @pl.when(first)
            def _():
                hosted.start(cins, couts, *sems)

        body(*ins, *outs, *scr)
        if h:
            @pl.when(last)
            def _():
                hosted.wait(cins, couts, *sems)

    res = pl.pallas_call(
        wrapped, name=name, grid=grid,
        in_specs=list(in_specs) + [ANY] * h, out_specs=list(out_specs) + [ANY] * h,
        out_shape=list(out_shape) + (hosted.out_shapes if h else []),
        scratch_shapes=list(scratch_shapes) + (hosted.scratch if h else []),
        input_output_aliases=aliases or {}, compiler_params=params,
    )(*operands, *(hosted.operands if h else []))
    return list(res[:n_out]), list(res[n_out:])


def _row_tile(R, C):
    if R * C <= (1 << 18):
        return R
    return max((d for d in range(8, 257, 8) if R % d == 0), default=R)


def _cond_rows(call_ref, cctx_ref, z_ref, D):
    z_ref[...] = jnp.zeros_like(z_ref)
    for a in range(N_DEV):
        z_ref[a:a + 1, :] = call_ref[a][:, :D]
    z_ref[N_DEV:N_DEV + 1, :] = cctx_ref[...]


def mod_forward(call, c_ctx, w_loc, b_loc, name):
    D, cols = w_loc.shape

    def body(call_ref, cctx_ref, w_ref, b_ref, o_ref, z_ref):
        _cond_rows(call_ref, cctx_ref, z_ref, D)
        z = z_ref[...]
        s = z * _sigmoid(z)
        o_ref[...] = jnp.dot(s, w_ref[...], preferred_element_type=F32, precision=HIGHEST) + b_ref[...]

    return pl.pallas_call(
        body, name=name, out_shape=jax.ShapeDtypeStruct((16, cols), F32),
        scratch_shapes=[pltpu.VMEM((16, D), F32)],
        compiler_params=_params(vmem=VMEM_BIG),
    )(call, c_ctx, w_loc, b_loc)


def mod_backward(call, c_ctx, w_loc, dlat_loc, dctx_loc, gath, n_small_off, n_small, name):
    D, cols = w_loc.shape

    def body(call_ref, cctx_ref, w_ref, dlat_ref, dctx_ref, g_ref, gw_ref, pc_ref, small_ref, z_ref, dm_ref):
        _cond_rows(call_ref, cctx_ref, z_ref, D)
        z = z_ref[...]
        s = z * _sigmoid(z)
        dctx = dctx_ref[0:1, :]
        for a in range(1, N_DEV):
            dctx = dctx + dctx_ref[a:a + 1, :]
        dm_ref[...] = jnp.zeros_like(dm_ref)
        dm_ref[0:N_DEV, :] = dlat_ref[...]
        dm_ref[N_DEV:N_DEV + 1, :] = dctx
        gw_ref[...] = lax.dot_general(s, dm_ref[...], TN, preferred_element_type=F32, precision=HIGHEST)
        pc_ref[...] = lax.dot_general(dctx, w_ref[...], NT, preferred_element_type=F32, precision=HIGHEST)
        acc = g_ref[0][:, n_small_off:n_small_off + n_small]
        for a in range(1, N_DEV):
            acc = acc + g_ref[a][:, n_small_off:n_small_off + n_small]
        small_ref[...] = acc

    return pl.pallas_call(
        body, name=name,
        out_shape=(jax.ShapeDtypeStruct((D, cols), F32), jax.ShapeDtypeStruct((1, D), F32),
                   jax.ShapeDtypeStruct((1, n_small), F32)),
        scratch_shapes=[pltpu.VMEM((16, D), F32), pltpu.VMEM((16, cols), F32)],
        compiler_params=_params(vmem=VMEM_BIG),
    )(call, c_ctx, w_loc, dlat_loc, dctx_loc, gath)


def bmod_and_cctx_grad(gath, pcs, c_ctx, nd, name):
    D = c_ctx.shape[-1]

    def body(g_ref, pc_ref, cctx_ref, gb_ref, gc_ref):
        acc = g_ref[0][:, :nd] + g_ref[0][:, nd:2 * nd]
        for a in range(1, N_DEV):
            acc = acc + (g_ref[a][:, :nd] + g_ref[a][:, nd:2 * nd])
        gb_ref[...] = acc
        p = pc_ref[0]
        for a in range(1, N_DEV):
            p = p + pc_ref[a]
        z = cctx_ref[...]
        sg = _sigmoid(z)
        gc_ref[...] = p * (sg * (1.0 + z * (1.0 - sg)))

    return pl.pallas_call(
        body, name=name,
        out_shape=(jax.ShapeDtypeStruct((1, nd), F32), jax.ShapeDtypeStruct((1, D), F32)),
    )(gath, pcs, c_ctx)


def _mod_spec(D, which, nctx):
    return pl.BlockSpec((1, 3, D), lambda i: (jnp.where(i < nctx, 0, 3) + which, 0, 0))


def _acc_spec(D, nctx):
    return pl.BlockSpec((1, 8, D), lambda i: (jnp.where(i < nctx, 0, 1), 0, 0))


def _row(tm, n):
    return pl.BlockSpec((tm, n), lambda i: (i, 0))


def _two_stream_specs(tm, D, nctx):
    return [pl.BlockSpec((tm, D), lambda i: (jnp.minimum(i, nctx - 1), 0)),
            pl.BlockSpec((tm, D), lambda i: (jnp.maximum(i - nctx, 0), 0))]


def _final_norm_loss_backward(x, tgt, gg, i, dx_ref, acc_ref):
    @pl.when(i == 0)
    def _():
        acc_ref[...] = jnp.zeros_like(acc_ref)

    D = x.shape[1]
    r = lax.rsqrt(jnp.mean(x * x, axis=-1, keepdims=True) + EPS)
    xh = x * r
    e = xh * gg - tgt
    part = 0.5 * jnp.sum(jnp.mean(e * e, axis=-1, keepdims=True), axis=0, keepdims=True)
    dy = e * (1.0 / D)
    dyg = dy * gg
    dx_ref[...] = r * (dyg - xh * jnp.mean(dyg * xh, axis=-1, keepdims=True))
    acc_ref[0:1, :] += jnp.sum(dy * xh, axis=0, keepdims=True)
    acc_ref[1:2, :] += jnp.broadcast_to(part, (1, D))


def _hidden_chunks(F):
    step = 1024 if F % 256 == 0 else F
    return [(lo, min(lo + step, F)) for lo in range(0, F, step)]


def ffn_forward(srcs, mod6, which, g, wt, w_out, nctx, name, hosted=None, final=None):
    D = srcs[-1].shape[1]
    Tr = sum(s.shape[0] for s in srcs)
    F = wt.shape[0] // 2
    tm = ROW_TILE
    two = len(srcs) == 2
    nfin = 0 if final is None else 2

    def body(*refs):
        x_refs, fin_refs, rest = refs[:len(srcs)], refs[len(srcs):len(srcs) + nfin], refs[len(srcs) + nfin:]
        mod_ref, g_ref, wt_ref, wout_ref, xo_ref, hm_ref, ab_ref, h_ref, f_ref = rest[:9]
        x = jnp.where(pl.program_id(0) < nctx, x_refs[0][...], x_refs[1][...]) if two else x_refs[0][...]
        ms = mod_ref[0]
        shift, scale, gate = ms[0:1], ms[1:2], ms[2:3]
        r = lax.rsqrt(jnp.mean(x * x, axis=-1, keepdims=True) + EPS)
        hb = (((x * r) * g_ref[...]) * (1.0 + scale) + shift).astype(BF)
        hm_ref[...] = hb
        f = jnp.zeros((tm, D), F32)
        for lo, hi in _hidden_chunks(F):
            a = lax.dot_general(hb, wt_ref[lo:hi, :], NT, preferred_element_type=F32)
            b = lax.dot_general(hb, wt_ref[F + lo:F + hi, :], NT, preferred_element_type=F32)
            ab_ref[:, lo:hi] = a.astype(BF)
            ab_ref[:, F + lo:F + hi] = b.astype(BF)
            h = ((a * _sigmoid(a)) * b).astype(BF)
            h_ref[:, lo:hi] = h
            f = f + jnp.dot(h, wout_ref[lo:hi, :], preferred_element_type=F32)
        f_ref[...] = f.astype(BF)
        xo = x + (0.5 * gate) * f
        if final is None:
            xo_ref[...] = xo
        else:
            _final_norm_loss_backward(xo, fin_refs[0][...], fin_refs[1][...], pl.program_id(0), xo_ref, rest[9])

    src_specs = _two_stream_specs(tm, D, nctx) if two else [_row(tm, D)]
    fin = final is not None
    return _call(
        body, name=name, grid=(Tr // tm,),
        in_specs=src_specs + ([_row(tm, D), _resident((1, D))] if fin else [])
                 + [_mod_spec(D, which, nctx), _resident((1, D)), _resident(wt.shape), _resident(w_out.shape)],
        out_specs=[_row(tm, D), _row(tm, D), _row(tm, 2 * F), _row(tm, F), _row(tm, D)]
                  + ([pl.BlockSpec((8, D), lambda i: (0, 0))] if fin else []),
        out_shape=[jax.ShapeDtypeStruct((Tr, D), F32), jax.ShapeDtypeStruct((Tr, D), BF),
                   jax.ShapeDtypeStruct((Tr, 2 * F), BF), jax.ShapeDtypeStruct((Tr, F), BF),
                   jax.ShapeDtypeStruct((Tr, D), BF)] + ([jax.ShapeDtypeStruct((8, D), F32)] if fin else []),
        operands=[*srcs, *(final or ()), mod6, g, wt, w_out], hosted=hosted,
        params=_params(vmem=VMEM_BIG, sem=("arbitrary",)))


def ffn_backward_rows(dxo, srcs, mod6, which, g, ab, fo, wt, w_out, nctx, name, hosted=None):
    D = srcs[-1].shape[1]
    Tr = sum(s.shape[0] for s in srcs)
    Tl = srcs[-1].shape[0]
    F = wt.shape[0] // 2
    tm = ROW_TILE
    two = len(srcs) == 2

    def body(*refs):
        dxo_ref, x_refs = refs[0], refs[1:1 + len(srcs)]
        mod_ref, g_ref, ab_ref, fo_ref, wt_ref, wout_ref, dx_ref, dab_ref, df_ref, acc_ref = refs[1 + len(srcs):]
        i = pl.program_id(0)

        @pl.when((i == 0) | (i == nctx))
        def _():
            acc_ref[...] = jnp.zeros_like(acc_ref)

        dxo = dxo_ref[...]
        x = jnp.where(i < nctx, x_refs[0][...], x_refs[1][...]) if two else x_refs[0][...]
        ms = mod_ref[0]
        scale, gate = ms[1:2], ms[2:3]
        gg = g_ref[...]
        dgate = jnp.sum(dxo * fo_ref[...].astype(F32), axis=0, keepdims=True) * 0.5
        dfb = (dxo * (0.5 * gate)).astype(BF)
        df_ref[...] = dfb
        dhm = jnp.zeros((tm, D), F32)
        for lo, hi in _hidden_chunks(F):
            dh = lax.dot_general(dfb, wout_ref[lo:hi, :], NT, preferred_element_type=F32)
            a = ab_ref[:, lo:hi].astype(F32)
            b = ab_ref[:, F + lo:F + hi].astype(F32)
            sg = _sigmoid(a)
            da = ((dh * b) * (sg * (1.0 + a * (1.0 - sg)))).astype(BF)
            db = (dh * (a * sg)).astype(BF)
            dab_ref[:, lo:hi] = da
            dab_ref[:, F + lo:F + hi] = db
            dhm = dhm + jnp.dot(da, wt_ref[lo:hi, :], preferred_element_type=F32)
            dhm = dhm + jnp.dot(db, wt_ref[F + lo:F + hi, :], preferred_element_type=F32)
        r = lax.rsqrt(jnp.mean(x * x, axis=-1, keepdims=True) + EPS)
        xh = x * r
        dshift = jnp.sum(dhm, axis=0, keepdims=True)
        dscale = jnp.sum(dhm * (xh * gg), axis=0, keepdims=True)
        dxh_g = dhm * (1.0 + scale)
        dg = jnp.sum(dxh_g * xh, axis=0, keepdims=True)
        dxh = dxh_g * gg
        dx_ref[...] = dxo + r * (dxh - xh * jnp.mean(dxh * xh, axis=-1, keepdims=True))
        for k, val in enumerate((dshift, dscale, dgate, dg)):
            acc_ref[0, k:k + 1, :] += val

    src_specs = _two_stream_specs(tm, D, nctx) if two else [_row(tm, D)]
    dx_spec = pl.BlockSpec((tm, D), lambda i: (jnp.maximum(i - nctx, 0), 0))
    return _call(
        body, name=name, grid=(Tr // tm,),
        in_specs=[_row(tm, D)] + src_specs + [_mod_spec(D, which, nctx), _resident((1, D)), _row(tm, 2 * F), _row(tm, D),
                                              _resident(wt.shape), _resident(w_out.shape)],
        out_specs=[dx_spec, _row(tm, 2 * F), _row(tm, D), _acc_spec(D, nctx)],
        out_shape=[jax.ShapeDtypeStruct((Tl, D), F32), jax.ShapeDtypeStruct((Tr, 2 * F), BF),
                   jax.ShapeDtypeStruct((Tr, D), BF), jax.ShapeDtypeStruct((2, 8, D), F32)],
        operands=[dxo, *srcs, mod6, g, ab, fo, wt, w_out], hosted=hosted,
        params=_params(vmem=VMEM_BIG, sem=("arbitrary",)))


def _token_tile(T):
    return _pick(T, (2048, 1408, 1024, 768, 512, 256, 128))


def tn_matmul(a, b, name, hosted=None):
    T, K = a.shape
    N = b.shape[1]
    tk = _pick(K, (1024, 1408, 1664, 768, 512, 384, 256, 128))
    tn = _pick(N, (1024, 1408, 1664, 768, 512, 384, 256, 128))
    tt = _token_tile(T)
    nt = T // tt

    def body(a_ref, b_ref, o_ref, acc_ref):
        t = pl.program_id(2)

        @pl.when(t == 0)
        def _():
            acc_ref[...] = jnp.zeros_like(acc_ref)

        acc_ref[...] += lax.dot_general(a_ref[...], b_ref[...], TN, preferred_element_type=F32)

        @pl.when(t == nt - 1)
        def _():
            o_ref[...] = acc_ref[...].astype(BF)

    (out,), exchanged = _call(
        body, name=name, grid=(K // tk, N // tn, nt),
        in_specs=[pl.BlockSpec((tt, tk), lambda k, n, t: (t, k)), pl.BlockSpec((tt, tn), lambda k, n, t: (t, n))],
        out_specs=[pl.BlockSpec((tk, tn), lambda k, n, t: (k, n))],
        out_shape=[jax.ShapeDtypeStruct((K, N), BF)],
        scratch_shapes=[pltpu.VMEM((tk, tn), F32)],
        operands=[a, b], hosted=hosted,
        params=_params(vmem=VMEM_BIG, sem=("arbitrary", "arbitrary", "arbitrary")))
    return out, exchanged


def _rope_apply(y, cos, s_next, s_prev):
    return y * cos + pltpu.roll(y, HEAD_DIM - 32, 1) * s_next + pltpu.roll(y, 32, 1) * s_prev


def _rope_transpose(dz, cos, s_next, s_prev):
    return dz * cos + pltpu.roll(dz * s_next, 32, 1) + pltpu.roll(dz * s_prev, HEAD_DIM - 32, 1)


def proj_forward(xa, mod6, g, w_in, qg, kg, tabs, offs, nctx, name, hosted=None):
    Tr, D = xa.shape
    P = w_in.shape[1]
    tm = ROW_TILE
    qo, ko = offs["q"], offs["k"]
    qw, kw = N_Q_HEADS * HEAD_DIM, N_KV_HEADS * HEAD_DIM
    scale_q = HEAD_DIM ** -0.5 * LOG2E

    def body(x_ref, mod_ref, g_ref, w_ref, qg_ref, kg_ref, tab_ref, hx_ref, pr_ref, q_ref, k_ref):
        x = x_ref[...]
        ms = mod_ref[0]
        shift, scale = ms[0:1], ms[1:2]
        r = lax.rsqrt(jnp.mean(x * x, axis=-1, keepdims=True) + EPS)
        hb = (((x * r) * g_ref[...]) * (1.0 + scale) + shift).astype(BF)
        hx_ref[...] = hb
        pr = jnp.dot(hb, w_ref[...], preferred_element_type=F32)
        pr_ref[...] = pr.astype(BF)
        cos, s_next, s_prev = tab_ref[0], tab_ref[1], tab_ref[2]

        def head(v, gain):
            n = v * lax.rsqrt(jnp.mean(v * v, axis=-1, keepdims=True) + EPS)
            return _rope_apply(n * gain, cos, s_next, s_prev)

        for h in range(N_Q_HEADS):
            lo = qo + h * HEAD_DIM
            q_ref[:, h * HEAD_DIM:(h + 1) * HEAD_DIM] = (head(pr[:, lo:lo + HEAD_DIM], qg_ref[...]) * scale_q).astype(BF)
        for h in range(N_KV_HEADS):
            lo = ko + h * HEAD_DIM
            k_ref[:, h * HEAD_DIM:(h + 1) * HEAD_DIM] = head(pr[:, lo:lo + HEAD_DIM], kg_ref[...]).astype(BF)

    return _call(
        body, name=name, grid=(Tr // tm,),
        in_specs=[_row(tm, D), _mod_spec(D, 1, nctx), _resident((1, D)), _resident(w_in.shape),
                  _resident((1, HEAD_DIM)), _resident((1, HEAD_DIM)),
                  pl.BlockSpec((3, tm, HEAD_DIM), lambda i: (0, i, 0))],
        out_specs=[_row(tm, D), _row(tm, P), _row(tm, qw), _row(tm, kw)],
        out_shape=[jax.ShapeDtypeStruct((Tr, D), BF), jax.ShapeDtypeStruct((Tr, P), BF),
                   jax.ShapeDtypeStruct((Tr, qw), BF), jax.ShapeDtypeStruct((Tr, kw), BF)],
        operands=[xa, mod6, g, w_in, qg, kg, tabs], hosted=hosted,
        params=_params(vmem=VMEM_BIG, sem=("arbitrary",)))


def _shifted(u, first_row, last_row):
    T = u.shape[0]
    prev = jnp.where(first_row, 0.0, pltpu.roll(u, 1, 0))
    nxt = jnp.where(last_row, 0.0, pltpu.roll(u, T - 1, 0))
    return prev, nxt


def conv_forward(proj, conv_w, offs, Tc, name):
    Ta = proj.shape[0]
    T = Ta - Tc
    Dc = conv_w.shape[1]
    cb = offs["cv"] // 384

    def body(p_ref, w_ref, y_ref):
        rows = lax.broadcasted_iota(jnp.int32, (T, 128), 0)
        u = p_ref[pl.ds(Tc, T), 128:256].astype(F32) * p_ref[pl.ds(Tc, T), 256:384].astype(F32)
        prev, nxt = _shifted(u, rows == 0, rows == T - 1)
        w = w_ref[...]
        cv = prev * w[0:1] + u * w[1:2] + nxt * w[2:3]
        y_ref[...] = (p_ref[pl.ds(Tc, T), 0:128].astype(F32) * cv).astype(BF)

    return pl.pallas_call(
        body, name=name, grid=(Dc // 128,),
        in_specs=[pl.BlockSpec((Ta, 384), lambda j: (0, cb + j)), pl.BlockSpec((CONV_TAPS, 128), lambda j: (0, j))],
        out_specs=pl.BlockSpec((T, 128), lambda j: (0, j)),
        out_shape=jax.ShapeDtypeStruct((T, Dc), BF),
        compiler_params=_params(vmem=VMEM_BIG, sem=("arbitrary",)),
    )(proj, conv_w)


def conv_backward(dproj, dy, proj, conv_w, offs, Tc, name):
    Ta = proj.shape[0]
    T = Ta - Tc
    Dc = conv_w.shape[1]
    cb = offs["cv"] // 384

    def body(dp_any, dy_ref, p_ref, w_ref, o_ref, dw_ref):
        rows = lax.broadcasted_iota(jnp.int32, (T, 128), 0)
        first, last = rows == 0, rows == T - 1
        bg = p_ref[pl.ds(Tc, T), 0:128].astype(F32)
        cg = p_ref[pl.ds(Tc, T), 128:256].astype(F32)
        vc = p_ref[pl.ds(Tc, T), 256:384].astype(F32)
        dy = dy_ref[...].astype(F32)
        u = cg * vc
        prev, nxt = _shifted(u, first, last)
        w = w_ref[...]
        cv = prev * w[0:1] + u * w[1:2] + nxt * w[2:3]
        o_ref[pl.ds(0, Tc), :] = jnp.zeros((Tc, 384), BF)
        o_ref[pl.ds(Tc, T), 0:128] = (dy * cv).astype(BF)
        dcv = dy * bg
        dprev, dnxt = _shifted(dcv, first, last)
        du = dnxt * w[0:1] + dcv * w[1:2] + dprev * w[2:3]
        o_ref[pl.ds(Tc, T), 128:256] = (du * vc).astype(BF)
        o_ref[pl.ds(Tc, T), 256:384] = (du * cg).astype(BF)
        dw_ref[...] = jnp.zeros_like(dw_ref)
        for k, tap in enumerate((prev, u, nxt)):
            dw_ref[k:k + 1, :] = jnp.sum(dcv * tap, axis=0, keepdims=True)

    blk = pl.BlockSpec((Ta, 384), lambda j: (0, cb + j))
    return pl.pallas_call(
        body, name=name, grid=(Dc // 128,),
        in_specs=[ANY, pl.BlockSpec((T, 128), lambda j: (0, j)), blk, pl.BlockSpec((CONV_TAPS, 128), lambda j: (0, j))],
        out_specs=[blk, pl.BlockSpec((8, 128), lambda j: (0, j))],
        out_shape=[jax.ShapeDtypeStruct(dproj.shape, BF), jax.ShapeDtypeStruct((8, Dc), F32)],
        input_output_aliases={0: 0},
        compiler_params=_params(vmem=VMEM_BIG, sem=("arbitrary",)),
    )(dproj, dy, proj, conv_w)


def _kv_chunk(Ta):
    return _pick(Ta, (768, 512, 384, 256, 128))


def _stack_heads(v):
    return jnp.concatenate([v[:, h * HEAD_DIM:(h + 1) * HEAD_DIM] for h in range(GROUP)], axis=0)


def attention_forward(q, k, proj, offs, Tc, name, hosted=None):
    Ta = k.shape[0]
    T = Ta - Tc
    tq = ROW_TILE
    kc = _kv_chunk(Ta)
    nkv = Ta // kc
    gw = GROUP * HEAD_DIM
    vblk = offs["v"] // HEAD_DIM
    qoff = Tc // tq
    n = GROUP * tq

    def body(q_ref, k_ref, v_ref, o_ref, lse_ref, vx_ref, qs_ref, s0_ref, s1_ref, m_ref, acc_ref):
        @pl.when(pl.program_id(1) == 0)
        def _():
            vx_ref[:, 0:HEAD_DIM] = v_ref[...]
            vx_ref[:, HEAD_DIM:2 * HEAD_DIM] = jnp.ones((Ta, HEAD_DIM), BF)

        qs_ref[...] = _stack_heads(q_ref[...])
        m_ref[...] = jnp.full((n, 1), -1e30, F32)
        acc_ref[...] = jnp.zeros((n, 2 * HEAD_DIM), F32)

        def rows(c):
            return pl.ds(pl.multiple_of(c * kc, kc), kc)

        def logits(c, dst):
            dst[...] = lax.dot_general(qs_ref[...], k_ref[rows(c), :], NT, preferred_element_type=F32)

        def consume(src, c):
            s = src[...]
            m_prev = m_ref[...]
            m_new = jnp.maximum(m_prev, jnp.max(s, axis=-1, keepdims=True))
            p = jnp.exp2(s - m_new).astype(BF)
            acc_ref[...] = jnp.exp2(m_prev - m_new) * acc_ref[...] + jnp.dot(p, vx_ref[rows(c), :], preferred_element_type=F32)
            m_ref[...] = m_new

        def pair(i, carry):
            logits(2 * i + 1, s1_ref)
            consume(s0_ref, 2 * i)
            logits(2 * i + 2, s0_ref)
            consume(s1_ref, 2 * i + 1)
            return carry

        logits(0, s0_ref)
        if nkv % 2:
            lax.fori_loop(0, nkv // 2, pair, 0)
            consume(s0_ref, nkv - 1)
        else:
            lax.fori_loop(0, nkv // 2 - 1, pair, 0)
            logits(nkv - 1, s1_ref)
            consume(s0_ref, nkv - 2)
            consume(s1_ref, nkv - 1)
        acc = acc_ref[...]
        l = acc[:, HEAD_DIM:HEAD_DIM + 1]
        o = acc[:, 0:HEAD_DIM] / l
        lse = m_ref[...] + jnp.log2(l)
        for h in range(GROUP):
            o_ref[:, h * HEAD_DIM:(h + 1) * HEAD_DIM] = o[h * tq:(h + 1) * tq].astype(BF)
            lse_ref[0, :, h:h + 1] = lse[h * tq:(h + 1) * tq]

    return _call(
        body, name=name, grid=(N_KV_HEADS, T // tq),
        in_specs=[pl.BlockSpec((tq, gw), lambda j, i: (i + qoff, j)),
                  pl.BlockSpec((Ta, HEAD_DIM), lambda j, i: (0, j)),
                  pl.BlockSpec((Ta, HEAD_DIM), lambda j, i: (0, vblk + j))],
        out_specs=[pl.BlockSpec((tq, gw), lambda j, i: (i, j)),
                   pl.BlockSpec((1, tq, GROUP), lambda j, i: (j, i, 0))],
        out_shape=[jax.ShapeDtypeStruct((T, N_Q_HEADS * HEAD_DIM), BF),
                   jax.ShapeDtypeStruct((N_KV_HEADS, T, GROUP), F32)],
        scratch_shapes=[pltpu.VMEM((Ta, 2 * HEAD_DIM), BF), pltpu.VMEM((n, HEAD_DIM), BF), pltpu.VMEM((n, kc), F32),
                        pltpu.VMEM((n, kc), F32), pltpu.VMEM((n, 1), F32), pltpu.VMEM((n, 2 * HEAD_DIM), F32)],
        operands=[q, k, proj], hosted=hosted,
        params=_params(vmem=VMEM_BIG, sem=("arbitrary", "arbitrary")))


def _norm_rope_backward(dz, raw, gg, cos, s_next, s_prev):
    r = lax.rsqrt(jnp.mean(raw * raw, axis=-1, keepdims=True) + EPS)
    n = raw * r
    dy = _rope_transpose(dz, cos, s_next, s_prev)
    dn = dy * gg
    return r * (dn - n * jnp.mean(dn * n, axis=-1, keepdims=True)), jnp.sum(dy * n, axis=0, keepdims=True)


def attention_backward(dproj, q, k, proj, o, lse, do, qgain, tabs, offs, Tc, name, hosted=None):
    Ta = k.shape[0]
    tq = ROW_TILE
    nctx = Tc // tq
    kc = _pick(Ta, (2816, 1408, 768, 512, 384, 256, 128))
    gw = GROUP * HEAD_DIM
    vblk = offs["v"] // HEAD_DIM
    qblk = offs["q"] // gw
    zscale = HEAD_DIM ** -0.5

    def body(dp_any, q_ref, k_ref, v_ref, o_ref, lse_ref, do_ref, raw_ref, g_ref, tab_ref, dqr_ref, dk_ref, dv_ref, dg_ref):
        j, i = pl.program_id(0), pl.program_id(1)

        @pl.when(i == 0)
        def _():
            dk_ref[...] = jnp.zeros_like(dk_ref)
            dv_ref[...] = jnp.zeros_like(dv_ref)

        @pl.when((i == 0) & (j == 0))
        def _():
            dg_ref[...] = jnp.zeros_like(dg_ref)

        @pl.when(i < nctx)
        def _():
            dqr_ref[...] = jnp.zeros_like(dqr_ref)

        @pl.when(i >= nctx)
        def _():
            qs = _stack_heads(q_ref[...])
            dob = do_ref[...]
            dos = _stack_heads(dob)
            delta = jnp.concatenate(
                [jnp.sum(dob[:, h * HEAD_DIM:(h + 1) * HEAD_DIM].astype(F32)
                         * o_ref[:, h * HEAD_DIM:(h + 1) * HEAD_DIM].astype(F32), axis=-1, keepdims=True)
                 for h in range(GROUP)], axis=0)
            lse = jnp.concatenate([lse_ref[0, :, h:h + 1] for h in range(GROUP)], axis=0)

            def step(c, dq):
                rows = pl.ds(pl.multiple_of(c * kc, kc), kc)
                kk = k_ref[rows, :]
                vv = v_ref[rows, :]
                s = lax.dot_general(qs, kk, NT, preferred_element_type=F32)
                p = jnp.exp2(s - lse)
                dp = lax.dot_general(dos, vv, NT, preferred_element_type=F32)
                ds = (p * (dp - delta)).astype(BF)
                dv_ref[rows, :] += lax.dot_general(p.astype(BF), dos, TN, preferred_element_type=F32)
                dk_ref[rows, :] += lax.dot_general(ds, qs, TN, preferred_element_type=F32)
                return dq + jnp.dot(ds, kk, preferred_element_type=F32)

            dq = lax.fori_loop(0, Ta // kc, step, jnp.zeros((GROUP * tq, HEAD_DIM), F32))
            per_head = lambda t: jnp.concatenate([t] * GROUP, axis=0)
            dr, dg = _norm_rope_backward(dq * zscale, _stack_heads(raw_ref[...]).astype(F32), g_ref[...],
                                         per_head(tab_ref[0]), per_head(tab_ref[1]), per_head(tab_ref[2]))
            for h in range(GROUP):
                dqr_ref[:, h * HEAD_DIM:(h + 1) * HEAD_DIM] = dr[h * tq:(h + 1) * tq].astype(BF)
            dg_ref[0:1, :] += dg

    lat = lambda j, i: (jnp.maximum(i - nctx, 0), j)
    (dproj, dk, dv, dqg), exchanged = _call(
        body, name=name, grid=(N_KV_HEADS, Ta // tq),
        in_specs=[ANY, pl.BlockSpec((tq, gw), lambda j, i: (i, j)),
                  pl.BlockSpec((Ta, HEAD_DIM), lambda j, i: (0, j)),
                  pl.BlockSpec((Ta, HEAD_DIM), lambda j, i: (0, vblk + j)),
                  pl.BlockSpec((tq, gw), lat),
                  pl.BlockSpec((1, tq, GROUP), lambda j, i: (j, jnp.maximum(i - nctx, 0), 0)),
                  pl.BlockSpec((tq, gw), lat),
                  pl.BlockSpec((tq, gw), lambda j, i: (i, qblk + j)),
                  pl.BlockSpec((1, HEAD_DIM), lambda j, i: (0, 0)),
                  pl.BlockSpec((3, tq, HEAD_DIM), lambda j, i: (0, i, 0))],
        out_specs=[pl.BlockSpec((tq, gw), lambda j, i: (i, qblk + j)),
                   pl.BlockSpec((Ta, HEAD_DIM), lambda j, i: (0, j)),
                   pl.BlockSpec((Ta, HEAD_DIM), lambda j, i: (0, j)),
                   pl.BlockSpec((8, HEAD_DIM), lambda j, i: (0, 0))],
        out_shape=[jax.ShapeDtypeStruct(dproj.shape, BF),
                   jax.ShapeDtypeStruct((Ta, N_KV_HEADS * HEAD_DIM), F32),
                   jax.ShapeDtypeStruct((Ta, N_KV_HEADS * HEAD_DIM), F32),
                   jax.ShapeDtypeStruct((8, HEAD_DIM), F32)],
        operands=[dproj, q, k, proj, o, lse, do, proj, qgain, tabs], hosted=hosted, aliases={0: 0},
        params=_params(vmem=VMEM_BIG, sem=("arbitrary", "arbitrary")))
    return (dproj, dk, dv, dqg), exchanged


def kv_backward(dproj, dk, dv, proj, gain, tabs, offs, name):
    Ta = proj.shape[0]
    tm = _pick(Ta, (768, 512, ROW_TILE))
    kw = N_KV_HEADS * HEAD_DIM
    cb = offs["k"] // (2 * kw)
    kb = offs["k"] // kw
    zscale = 1.0 / LOG2E

    def body(dp_any, dk_ref, dv_ref, raw_ref, g_ref, tab_ref, o_ref, dg_ref):
        @pl.when(pl.program_id(0) == 0)
        def _():
            dg_ref[...] = jnp.zeros_like(dg_ref)

        cos, s_next, s_prev = tab_ref[0], tab_ref[1], tab_ref[2]
        dg = jnp.zeros((1, HEAD_DIM), F32)
        for h in range(N_KV_HEADS):
            sl = slice(h * HEAD_DIM, (h + 1) * HEAD_DIM)
            dr, dgh = _norm_rope_backward(dk_ref[:, sl] * zscale, raw_ref[:, sl].astype(F32), g_ref[...], cos, s_next, s_prev)
            o_ref[:, sl] = dr.astype(BF)
            dg = dg + dgh
        o_ref[:, kw:2 * kw] = dv_ref[...].astype(BF)
        dg_ref[0:1, :] += dg

    return pl.pallas_call(
        body, name=name, grid=(Ta // tm,),
        in_specs=[ANY, _row(tm, kw), _row(tm, kw), pl.BlockSpec((tm, kw), lambda i: (i, kb)),
                  _resident((1, HEAD_DIM)), pl.BlockSpec((3, tm, HEAD_DIM), lambda i: (0, i, 0))],
        out_specs=[pl.BlockSpec((tm, 2 * kw), lambda i: (i, cb)), pl.BlockSpec((8, HEAD_DIM), lambda i: (0, 0))],
        out_shape=[jax.ShapeDtypeStruct(dproj.shape, BF), jax.ShapeDtypeStruct((8, HEAD_DIM), F32)],
        input_output_aliases={0: 0},
        compiler_params=_params(sem=("arbitrary",)),
    )(dproj, dk, dv, proj, gain, tabs)


def merge_forward(x1, mod6, yc, o, proj, w_bc, w_ba, w_o, offs, Tc, name):
    T, D = yc.shape[0], x1.shape[1]
    tm = ROW_TILE
    roff = Tc // tm
    gb = offs["gt"] // (2 * D)

    def body(x_ref, mod_ref, yc_ref, o_ref, gt_ref, wbc_ref, wba_ref, wo_ref, xo_ref, pc_ref, pa_ref, m_ref, z_ref):
        gate = mod_ref[0][2:3]
        pc = jnp.dot(yc_ref[...], wbc_ref[...], preferred_element_type=F32)
        pa = jnp.dot(o_ref[...], wba_ref[...], preferred_element_type=F32)
        pc_ref[...] = pc.astype(BF)
        pa_ref[...] = pa.astype(BF)
        mb = (_sigmoid(gt_ref[:, 0:D].astype(F32)) * pc + _sigmoid(gt_ref[:, D:2 * D].astype(F32)) * pa).astype(BF)
        m_ref[...] = mb
        z = jnp.dot(mb, wo_ref[...], preferred_element_type=F32)
        z_ref[...] = z.astype(BF)
        xo_ref[...] = x_ref[...] + gate * z

    return pl.pallas_call(
        body, name=name, grid=(T // tm,),
        in_specs=[pl.BlockSpec((tm, D), lambda i: (i + roff, 0)), _mod_spec(D, 1, 0), _row(tm, yc.shape[1]), _row(tm, o.shape[1]),
                  pl.BlockSpec((tm, 2 * D), lambda i: (i + roff, gb)),
                  _resident(w_bc.shape), _resident(w_ba.shape), _resident(w_o.shape)],
        out_specs=[_row(tm, D)] * 5,
        out_shape=[jax.ShapeDtypeStruct((T, D), F32)] + [jax.ShapeDtypeStruct((T, D), BF)] * 4,
        compiler_params=_params(vmem=VMEM_BIG, sem=("arbitrary",)),
    )(x1, mod6, yc, o, proj, w_bc, w_ba, w_o)


def merge_backward_rows(dx2, mod6, z, pc, pa, proj, w_bc, w_ba, w_o, offs, Tc, name):
    T, D = dx2.shape
    Ta, P = proj.shape
    tm = ROW_TILE
    nctx = Tc // tm
    gb = offs["gt"] // (2 * D)
    dcw, dqw = w_bc.shape[0], w_ba.shape[0]

    def body(dx_ref, mod_ref, z_ref, pc_ref, pa_ref, gt_ref, wbc_ref, wba_ref, wo_ref,
             dgt_ref, dg_ref, dpc_ref, dpa_ref, dyc_ref, do_ref, acc_ref):
        i = pl.program_id(0)

        @pl.when(i == 0)
        def _():
            acc_ref[...] = jnp.zeros_like(acc_ref)

        @pl.when(i < nctx)
        def _():
            dgt_ref[...] = jnp.zeros_like(dgt_ref)

        @pl.when(i >= nctx)
        def _():
            gate = mod_ref[0][2:3]
            dx = dx_ref[...]
            acc_ref[0:1, :] += jnp.sum(dx * z_ref[...].astype(F32), axis=0, keepdims=True)
            dgb = (dx * gate).astype(BF)
            dg_ref[...] = dgb
            dm = lax.dot_general(dgb, wo_ref[...], NT, preferred_element_type=F32)
            sc = _sigmoid(gt_ref[:, 0:D].astype(F32))
            sa = _sigmoid(gt_ref[:, D:2 * D].astype(F32))
            pc = pc_ref[...].astype(F32)
            pa = pa_ref[...].astype(F32)
            dpc = (dm * sc).astype(BF)
            dpa = (dm * sa).astype(BF)
            dpc_ref[...] = dpc
            dpa_ref[...] = dpa
            dgt_ref[:, 0:D] = ((dm * pc) * (sc * (1.0 - sc))).astype(BF)
            dgt_ref[:, D:2 * D] = ((dm * pa) * (sa * (1.0 - sa))).astype(BF)
            dyc_ref[...] = lax.dot_general(dpc, wbc_ref[...], NT, preferred_element_type=F32).astype(BF)
            do_ref[...] = lax.dot_general(dpa, wba_ref[...], NT, preferred_element_type=F32).astype(BF)

    lat = lambda n: pl.BlockSpec((tm, n), lambda i: (jnp.maximum(i - nctx, 0), 0))
    return pl.pallas_call(
        body, name=name, grid=(Ta // tm,),
        in_specs=[lat(D), _mod_spec(D, 1, 0), lat(D), lat(D), lat(D),
                  pl.BlockSpec((tm, 2 * D), lambda i: (i, gb)),
                  _resident(w_bc.shape), _resident(w_ba.shape), _resident(w_o.shape)],
        out_specs=[pl.BlockSpec((tm, 2 * D), lambda i: (i, gb)), lat(D), lat(D), lat(D), lat(dcw), lat(dqw),
                   pl.BlockSpec((8, D), lambda i: (0, 0))],
        out_shape=[jax.ShapeDtypeStruct((Ta, P), BF)] + [jax.ShapeDtypeStruct((T, D), BF)] * 3
                  + [jax.ShapeDtypeStruct((T, dcw), BF), jax.ShapeDtypeStruct((T, dqw), BF), jax.ShapeDtypeStruct((8, D), F32)],
        compiler_params=_params(vmem=VMEM_BIG, sem=("arbitrary",)),
    )(dx2, mod6, z, pc, pa, proj, w_bc, w_ba, w_o)


def proj_backward_rows(dproj, dres, xa, mod6, g, w_in, nctx, name, hosted=None):
    Tr, D = xa.shape
    P = w_in.shape[1]
    tm = ROW_TILE

    def body(dp_ref, dres_ref, x_ref, mod_ref, g_ref, w_ref, dx_ref, acc_ref):
        i = pl.program_id(0)

        @pl.when((i == 0) | (i == nctx))
        def _():
            acc_ref[...] = jnp.zeros_like(acc_ref)

        x = x_ref[...]
        scale = mod_ref[0][1:2]
        gg = g_ref[...]
        dhm = lax.dot_general(dp_ref[...], w_ref[...], NT, preferred_element_type=F32)
        r = lax.rsqrt(jnp.mean(x * x, axis=-1, keepdims=True) + EPS)
        xh = x * r
        dshift = jnp.sum(dhm, axis=0, keepdims=True)
        dscale = jnp.sum(dhm * (xh * gg), axis=0, keepdims=True)
        dxh_g = dhm * (1.0 + scale)
        dg = jnp.sum(dxh_g * xh, axis=0, keepdims=True)
        dxh = dxh_g * gg
        res = jnp.where(i < nctx, 0.0, dres_ref[...])
        dx_ref[...] = res + r * (dxh - xh * jnp.mean(dxh * xh, axis=-1, keepdims=True))
        for k, val in enumerate((dshift, dscale, dg)):
            acc_ref[0, k:k + 1, :] += val

    return _call(
        body, name=name, grid=(Tr // tm,),
        in_specs=[_row(tm, P), pl.BlockSpec((tm, D), lambda i: (jnp.maximum(i - nctx, 0), 0)), _row(tm, D),
                  _mod_spec(D, 1, nctx), _resident((1, D)), _resident(w_in.shape)],
        out_specs=[_row(tm, D), _acc_spec(D, nctx)],
        out_shape=[jax.ShapeDtypeStruct((Tr, D), F32), jax.ShapeDtypeStruct((2, 8, D), F32)],
        operands=[dproj, dres, xa, mod6, g, w_in], hosted=hosted,
        params=_params(vmem=VMEM_BIG, sem=("arbitrary",)))


def _adam_update(w, g, m, v):
    c1 = 1.0 - ADAM_B1 ** ADAM_STEP
    c2 = 1.0 - ADAM_B2 ** ADAM_STEP
    m = ADAM_B1 * m + (1.0 - ADAM_B1) * g
    v = ADAM_B2 * v + (1.0 - ADAM_B2) * (g * g)
    return -ADAM_LR * ((m / c1) / (jnp.sqrt(v / c2) + ADAM_EPS) + ADAM_WD * w), m, v


def adamw(w, g, m, v, name, dep=None):
    R, C = w.shape
    tr = _row_tile(R, C)
    deps = [] if dep is None else [dep]

    def body(w_ref, g_ref, m_ref, v_ref, *rest):
        d_ref, nm_ref, nv_ref = rest[len(deps):]
        d_ref[...], nm_ref[...], nv_ref[...] = _adam_update(w_ref[...], g_ref[...], m_ref[...], v_ref[...])

    blk = pl.BlockSpec((tr, C), lambda i: (i, 0))
    return pl.pallas_call(
        body, name=name, grid=(R // tr,),
        in_specs=[blk] * 4 + [ANY] * len(deps), out_specs=[blk] * 3,
        out_shape=[jax.ShapeDtypeStruct((R, C), F32)] * 3,
        compiler_params=_params(vmem=VMEM_BIG, sem=("parallel",)),
    )(w, g, m, v, *deps)


def adamw_summed(recv, w, m, v, name, dep=None):
    R, C = w.shape
    tr = _row_tile(R, C)
    deps = [] if dep is None else [dep]

    def body(r_ref, w_ref, m_ref, v_ref, *rest):
        g_ref, d_ref, nm_ref, nv_ref = rest[len(deps):]
        g = r_ref[0].astype(F32)
        for a in range(1, N_DEV):
            g = g + r_ref[a].astype(F32)
        g_ref[...] = g
        d_ref[...], nm_ref[...], nv_ref[...] = _adam_update(w_ref[...], g, m_ref[...], v_ref[...])

    blk = pl.BlockSpec((tr, C), lambda i: (i, 0))
    return pl.pallas_call(
        body, name=name, grid=(R // tr,),
        in_specs=[pl.BlockSpec((N_DEV, tr, C), lambda i: (0, i, 0)), blk, blk, blk] + [ANY] * len(deps), out_specs=[blk] * 4,
        out_shape=[jax.ShapeDtypeStruct((R, C), F32)] * 4,
        compiler_params=_params(vmem=VMEM_BIG, sem=("parallel",)),
    )(recv, w, m, v, *deps)


def _rope_tables(T, Tc):
    rows = T // GRID_W
    n_freq = HEAD_DIM // 4
    inv = ROPE_THETA ** (-jnp.arange(n_freq, dtype=F32) / n_freq)
    ang_r = jnp.arange(rows).astype(F32)[:, None] * inv
    ang_c = jnp.arange(GRID_W).astype(F32)[:, None] * inv
    per_row = lambda a: jnp.broadcast_to(a[:, None, :], (rows, GRID_W, n_freq)).reshape(T, n_freq)
    per_col = lambda a: jnp.broadcast_to(a[None, :, :], (rows, GRID_W, n_freq)).reshape(T, n_freq)
    cr, sr = per_row(jnp.cos(ang_r)), per_row(jnp.sin(ang_r))
    cc, sc = per_col(jnp.cos(ang_c)), per_col(jnp.sin(ang_c))
    zero = jnp.zeros_like(sr)
    cos = jnp.concatenate([cr, cr, cc, cc], axis=1)
    s_next = jnp.concatenate([-sr, zero, -sc, zero], axis=1)
    s_prev = jnp.concatenate([zero, sr, zero, sc], axis=1)
    lat = jnp.stack([cos, s_next, s_prev])
    ctx = jnp.stack([jnp.ones((Tc, HEAD_DIM), F32), jnp.zeros((Tc, HEAD_DIM), F32), jnp.zeros((Tc, HEAD_DIM), F32)])
    return jnp.concatenate([ctx, lat], axis=1)


BIG = ("ffn1_w_in", "ffn1_w_out", "w_in", "w_branch_conv", "w_branch_attn", "w_out", "ffn2_w_in", "ffn2_w_out")


def _regroup_w_in(stacked, D, Dc, qw, kw):
    w = stacked.transpose(1, 0, 2).reshape(D, -1)
    o = 0
    parts = {}
    for nme, wd in (("bg", Dc), ("cg", Dc), ("vc", Dc), ("q", qw), ("k", kw), ("v", kw), ("gt", 2 * D)):
        parts[nme] = w[:, o:o + wd]
        o += wd
    nb = Dc // 128
    cv = jnp.stack([parts[n].reshape(D, nb, 128) for n in ("bg", "cg", "vc")], axis=2).reshape(D, 3 * Dc)
    return jnp.concatenate([cv, parts["q"], parts["gt"], parts["k"], parts["v"]], axis=1)


def _ungroup_w_in_grad(gt_, D, Dc, qw, kw):
    nb = Dc // 128
    cv = gt_[:3 * Dc].reshape(nb, 3, 128, D)
    o = 3 * Dc
    q = gt_[o:o + qw]
    gt = gt_[o + qw:o + qw + 2 * D]
    k = gt_[o + qw + 2 * D:o + qw + 2 * D + kw]
    v = gt_[o + qw + 2 * D + kw:]
    nat = jnp.concatenate([cv[:, 0].reshape(Dc, D), cv[:, 1].reshape(Dc, D), cv[:, 2].reshape(Dc, D), q, k, v, gt], axis=0)
    return nat.reshape(N_DEV, -1, D)


def kernel(x, c, ctx, c_ctx, w_mod, b_mod, norm1_g, norm2_g, norm3_g, ffn1_w_in, ffn1_w_out, w_in, conv_w, q_norm_g, k_norm_g, w_branch_conv, w_branch_attn, w_out, ffn2_w_in, ffn2_w_out, final_g, loss_target, m_c_ctx, m_w_mod, m_b_mod, m_norm1_g, m_norm2_g, m_norm3_g, m_ffn1_w_in, m_ffn1_w_out, m_w_in, m_conv_w, m_q_norm_g, m_k_norm_g, m_w_branch_conv, m_w_branch_attn, m_w_out, m_ffn2_w_in, m_ffn2_w_out, m_final_g, v_c_ctx, v_w_mod, v_b_mod, v_norm1_g, v_norm2_g, v_norm3_g, v_ffn1_w_in, v_ffn1_w_out, v_w_in, v_conv_w, v_q_norm_g, v_k_norm_g, v_w_branch_conv, v_w_branch_attn, v_w_out, v_ffn2_w_in, v_ffn2_w_out, v_final_g):
    weights = dict(c_ctx=c_ctx, w_mod=w_mod, b_mod=b_mod, norm1_g=norm1_g, norm2_g=norm2_g, norm3_g=norm3_g,
                   ffn1_w_in=ffn1_w_in, ffn1_w_out=ffn1_w_out, w_in=w_in, conv_w=conv_w, q_norm_g=q_norm_g,
                   k_norm_g=k_norm_g, w_branch_conv=w_branch_conv, w_branch_attn=w_branch_attn, w_out=w_out,
                   ffn2_w_in=ffn2_w_in, ffn2_w_out=ffn2_w_out, final_g=final_g)
    moms = dict(c_ctx=(m_c_ctx, v_c_ctx), w_mod=(m_w_mod, v_w_mod), b_mod=(m_b_mod, v_b_mod),
                norm1_g=(m_norm1_g, v_norm1_g), norm2_g=(m_norm2_g, v_norm2_g), norm3_g=(m_norm3_g, v_norm3_g),
                ffn1_w_in=(m_ffn1_w_in, v_ffn1_w_in), ffn1_w_out=(m_ffn1_w_out, v_ffn1_w_out), w_in=(m_w_in, v_w_in),
                conv_w=(m_conv_w, v_conv_w), q_norm_g=(m_q_norm_g, v_q_norm_g), k_norm_g=(m_k_norm_g, v_k_norm_g),
                w_branch_conv=(m_w_branch_conv, v_w_branch_conv), w_branch_attn=(m_w_branch_attn, v_w_branch_attn),
                w_out=(m_w_out, v_w_out), ffn2_w_in=(m_ffn2_w_in, v_ffn2_w_in), ffn2_w_out=(m_ffn2_w_out, v_ffn2_w_out),
                final_g=(m_final_g, v_final_g))
    order = list(weights)

    T, D = x.shape[1], x.shape[2]
    Tc = ctx.shape[1]
    nctx = Tc // ROW_TILE
    nd = N_MOD * D
    Dc = conv_w.shape[2] * N_DEV
    qw, kw = N_Q_HEADS * HEAD_DIM, N_KV_HEADS * HEAD_DIM
    offs, o = {}, 0
    for nme, wd in (("cv", 3 * Dc), ("q", qw), ("gt", 2 * D), ("k", kw), ("v", kw)):
        offs[nme] = o
        o += wd

    ax, ay, ac = lax.axis_index("x"), lax.axis_index("y"), lax.axis_index("c")
    me = 4 * ax + 2 * ay + ac

    shard = {n: (jnp.swapaxes(weights[n][0], 0, 1) if n in ("ffn1_w_in", "ffn2_w_in") else weights[n][0]).astype(BF)
             for n in BIG}
    rows2d = lambda a: a.reshape(-1, a.shape[-1])
    full = {}
    mod_cols = w_mod.shape[2]
    cw_loc = conv_w[0]
    cpad = (-(D + CONV_TAPS * cw_loc.shape[1])) % 128
    pay = jnp.concatenate([c.reshape(1, D), cw_loc.reshape(1, -1), jnp.zeros((1, cpad), F32)], axis=1)
    (g_ffn1_in, g_ffn1_out), (call,) = allgather_two_level([shard["ffn1_w_in"], shard["ffn1_w_out"]], "ag_ffn1",
                                                          riders=[pay])
    full["ffn1_w_in"], full["ffn1_w_out"] = rows2d(g_ffn1_in), rows2d(g_ffn1_out)

    conv_full = call[:, 0, D:D + CONV_TAPS * cw_loc.shape[1]].reshape(N_DEV, CONV_TAPS, -1).transpose(1, 0, 2).reshape(CONV_TAPS, Dc)
    b_loc = lax.dynamic_slice_in_dim(b_mod, me * mod_cols, mod_cols, axis=1)
    cctx2 = c_ctx.reshape(1, D)
    mod_part = mod_forward(call, cctx2, w_mod[0], b_loc, "mod_fwd")
    mod_all = allgather_direct(mod_part, "ag_mod")
    mod_lat = lax.dynamic_index_in_dim(mod_all, me, axis=1, keepdims=False).reshape(nd)
    mod_ctx = mod_all[:, N_DEV, :].reshape(nd)
    mod6 = jnp.stack([mod_ctx, mod_lat]).reshape(6, 3, D)

    tabs = _rope_tables(T, Tc)

    srcs1 = (ctx[0], x[0])
    (xa1, hm1, ab1, h1, f1), (g_w_in,) = ffn_forward(
        srcs1, mod6, 0, norm1_g, full["ffn1_w_in"], full["ffn1_w_out"], nctx, "ffn1_fwd",
        hosted=Hosted(gathers=[shard["w_in"]]))
    full["w_in"] = _regroup_w_in(g_w_in, D, Dc, qw, kw)
    merge_names = ("w_branch_conv", "w_branch_attn", "w_out")
    (hx, proj, qr, kr), g_merge = proj_forward(
        xa1, mod6, norm2_g, full["w_in"], q_norm_g, k_norm_g, tabs, offs, nctx, "proj_fwd",
        hosted=Hosted(gathers=[shard[n] for n in merge_names]))
    full.update({n: rows2d(g) for n, g in zip(merge_names, g_merge)})
    yc = conv_forward(proj, conv_full, offs, Tc, "conv_fwd")
    (oa, lse), (g_ffn2_in, g_ffn2_out) = attention_forward(
        qr, kr, proj, offs, Tc, "attn_fwd", hosted=Hosted(gathers=[shard["ffn2_w_in"], shard["ffn2_w_out"]]))
    full["ffn2_w_in"], full["ffn2_w_out"] = rows2d(g_ffn2_in), rows2d(g_ffn2_out)
    x2, pc, pa, mm, zz = merge_forward(xa1, mod6, yc, oa, proj, full["w_branch_conv"], full["w_branch_attn"],
                                       full["w_out"], offs, Tc, "merge_fwd")
    (dx3, hm2, ab2, h2, f2, lacc), _ = ffn_forward((x2,), mod6, 2, norm3_g, full["ffn2_w_in"], full["ffn2_w_out"], 0, "ffn2_fwd",
                                                   final=(loss_target[0], final_g.reshape(1, D)))

    by_dest = lambda g: g.reshape((N_DEV, -1, g.shape[-1]))
    (dx2, dab2, df2, acc_f2), _ = ffn_backward_rows(dx3, (x2,), mod6, 2, norm3_g, ab2, f2, full["ffn2_w_in"], full["ffn2_w_out"], 0, "ffn2_bwd")
    early = {"ffn2_w_out": tn_matmul(h2, df2, "ffn2_dwout")[0], "ffn2_w_in": tn_matmul(dab2, hm2, "ffn2_dwin")[0]}
    dproj, dgm, dpc, dpa, dyc, do, acc_mg = merge_backward_rows(dx2, mod6, zz, pc, pa, proj, full["w_branch_conv"],
                                                                full["w_branch_attn"], full["w_out"], offs, Tc, "merge_bwd")
    early["w_out"] = tn_matmul(mm, dgm, "dw_out")[0]
    early["w_branch_conv"] = tn_matmul(yc, dpc, "dw_bc")[0]
    early["w_branch_attn"] = tn_matmul(oa, dpa, "dw_ba")[0]
    dproj, dcw = conv_backward(dproj, dyc, proj, conv_full, offs, Tc, "conv_bwd")
    (dproj, dk, dv, dqg), summed = attention_backward(dproj, qr, kr, proj, oa, lse, do, q_norm_g, tabs, offs, Tc, "attn_bwd",
                                                      hosted=Hosted(scatters=[by_dest(g) for g in early.values()]))
    summed = dict(zip(early, summed))
    dproj, dkg = kv_backward(dproj, dk, dv, proj, k_norm_g, tabs, offs, "kv_bwd")
    g_w_in_grad = _ungroup_w_in_grad(tn_matmul(dproj, hx, "dw_in")[0], D, Dc, qw, kw)
    (dxa1, acc_pj), (summed["w_in"],) = proj_backward_rows(dproj, dx2, xa1, mod6, norm2_g, full["w_in"], nctx, "proj_bwd",
                                                          hosted=Hosted(scatters=[g_w_in_grad]))
    (grad_x2d, dab1, df1, acc_f1), _ = ffn_backward_rows(
        dxa1, srcs1, mod6, 0, norm1_g, ab1, f1, full["ffn1_w_in"], full["ffn1_w_out"], nctx, "ffn1_bwd")
    g_ffn1_w_in, (summed["ffn1_w_out"],) = tn_matmul(
        dab1, hm1, "ffn1_dwin", hosted=Hosted(scatters=[by_dest(tn_matmul(h1, df1, "ffn1_dwout")[0])]))
    grad_x = grad_x2d[None]

    zero_d = jnp.zeros((D,), F32)
    dlat = jnp.concatenate([acc_f1[1, 0], acc_f1[1, 1], acc_f1[1, 2], acc_pj[1, 0], acc_pj[1, 1], acc_mg[0],
                            acc_f2[1, 0], acc_f2[1, 1], acc_f2[1, 2]])
    dctx = jnp.concatenate([acc_f1[0, 0], acc_f1[0, 1], acc_f1[0, 2], acc_pj[0, 0], acc_pj[0, 1]] + [zero_d] * 4)
    small = jnp.concatenate([acc_f1[0, 3] + acc_f1[1, 3], acc_pj[0, 2] + acc_pj[1, 2], acc_f2[1, 3],
                             dqg[0], dkg[0], lacc[0], dcw[0:CONV_TAPS].reshape(-1), lacc[1, 0:128]])
    n_small = small.shape[0]
    pay_b = jnp.concatenate([dlat, dctx, small]).reshape(1, -1)
    gath = allgather_direct(pay_b, "ag_small_grads")
    dlat_loc = lax.dynamic_slice_in_dim(gath[:, 0, :nd], me * mod_cols, mod_cols, axis=1)
    dctx_loc = lax.dynamic_slice_in_dim(gath[:, 0, nd:2 * nd], me * mod_cols, mod_cols, axis=1)
    g_wmod, pc_part, small_sum = mod_backward(call, cctx2, w_mod[0], dlat_loc, dctx_loc, gath, 2 * nd, n_small, "mod_bwd")
    pcs = allgather_direct(pc_part, "ag_cctx")
    g_bmod, g_cctx = bmod_and_cctx_grad(gath, pcs, cctx2, nd, "small_bwd")
    sm = small_sum[0]
    loss = sm[n_small - 1]
    conv_off = 4 * D + 2 * HEAD_DIM
    g_conv_full = sm[conv_off:conv_off + CONV_TAPS * Dc].reshape(CONV_TAPS, Dc)
    g_conv = lax.dynamic_slice_in_dim(g_conv_full, me * cw_loc.shape[1], cw_loc.shape[1], axis=1)
    gsmall = dict(
        c_ctx=g_cctx, w_mod=g_wmod, b_mod=g_bmod, norm1_g=sm[0:D][None], norm2_g=sm[D:2 * D][None],
        norm3_g=sm[2 * D:3 * D][None], q_norm_g=sm[3 * D:3 * D + HEAD_DIM][None],
        k_norm_g=sm[3 * D + HEAD_DIM:3 * D + 2 * HEAD_DIM][None],
        final_g=sm[3 * D + 2 * HEAD_DIM:3 * D + 2 * HEAD_DIM + D][None], conv_w=g_conv)

    flipped = ("ffn1_w_in", "w_in", "ffn2_w_in")
    last = "ffn1_w_in"
    results = {}

    def update(n, dep=None):
        w = weights[n]
        shp = w.shape
        if n in flipped:
            two_d = lambda a: jnp.swapaxes(a[0], 0, 1)
            back = lambda a: jnp.swapaxes(a, 0, 1)[None]
        else:
            two_d = lambda a: a.reshape(-1, shp[-1])
            back = lambda a: a.reshape(shp)
        m, v = moms[n]
        if n in summed:
            g2, d, nm, nv = adamw_summed(summed[n], two_d(w), two_d(m), two_d(v), "adamw_" + n, dep=dep)
        else:
            g2 = gsmall[n].reshape(two_d(w).shape)
            d, nm, nv = adamw(two_d(w), g2, two_d(m), two_d(v), "adamw_" + n, dep=dep)
        results[n] = tuple(back(a) for a in (g2, d, nm, nv))
        return d

    last_sems, last_recv_sems, last_src, last_land, token = scatter_start(
        by_dest(g_ffn1_w_in), [g_cctx], "rs_ffn1_start")
    heavy = ("w_mod", "w_in", "ffn2_w_in", "ffn2_w_out", "ffn1_w_out", "w_out")
    deltas = {n: update(n, dep=token) for n in order if n != last}
    after = jnp.concatenate([deltas[n][0:1, 0:1] for n in heavy], axis=1)
    g_done, land_done = scatter_wait(last_sems, last_recv_sems, last_src, last_land, after, "rs_ffn1_wait")
    own = lax.dynamic_index_in_dim(g_done, me, axis=0, keepdims=True)
    summed[last] = lax.dynamic_update_slice_in_dim(land_done, own, me, axis=0)
    update(last)
    cols = list(zip(*(results[n] for n in order)))
    return (loss, grad_x, *cols[0], *cols[1], *cols[2], *cols[3])
```
